```python
import jax, jax.numpy as jnp
from jax import lax
import numpy as np

D_MODEL = 2048
BATCH = 8
SEQ = 4096
DEPTH = 4

HEAD_DIM = 128
A_WIDTH = D_MODEL // 2
A_HEADS = A_WIDTH // HEAD_DIM
DILATED_PATTERNS = ((128, 1), (512, 4), (2048, 16))
ATTN_BLOCK = 64
ROPE_THETA = 10000.0
NEG_INF = -1e30
B_WIDTH = D_MODEL // 2
B_CONV = 3
C_WIDTH = D_MODEL
C_GROUPS = 8
C_CHUNK = 128
AB_IN_WIDTH = 4 * A_WIDTH + 4 * B_WIDTH
SG_IN_WIDTH = 3 * C_WIDTH
N_EVEN = (DEPTH + 1) // 2
N_ODD = DEPTH // 2
EPS = 1e-6

kernel_name = 'hybrid_dilated_attn_shortconv_sgu_adaln'


def rms_norm(x, g):
    xf = x.astype(jnp.float32)
    y = xf * lax.rsqrt(jnp.mean(xf * xf, axis=-1, keepdims=True) + EPS)
    return (y * g.astype(jnp.float32)).astype(x.dtype)


def layer_norm(x, g, b):
    xf = x.astype(jnp.float32)
    mu = jnp.mean(xf, axis=-1, keepdims=True)
    xc = xf - mu
    y = xc * lax.rsqrt(jnp.mean(xc * xc, axis=-1, keepdims=True) + EPS)
    return (y * g.astype(jnp.float32) + b.astype(jnp.float32)).astype(x.dtype)


def ada_modulation(c, w_mod, b_mod):
    m = jax.nn.silu(c) @ w_mod + b_mod
    shift, scale, gate = jnp.split(m, 3, axis=-1)
    return shift[:, None, :], scale[:, None, :], gate[:, None, :]


def rope(t, pos):
    half = t.shape[-1] // 2
    inv = ROPE_THETA ** (-jnp.arange(half, dtype=jnp.float32) / half)
    ang = pos[:, None] * inv[None, :]
    cos = jnp.cos(ang)[None, :, None, :]
    sin = jnp.sin(ang)[None, :, None, :]
    tf = t.astype(jnp.float32)
    t1, t2 = tf[..., :half], tf[..., half:]
    out = jnp.concatenate([t1 * cos - t2 * sin, t2 * cos + t1 * sin], axis=-1)
    return out.astype(t.dtype)


def dilated_window_attention(q, k, v, dilation, radius):
    b, h, s, hd = q.shape
    sub_len = s // dilation
    n_blk = -(-sub_len // ATTN_BLOCK)
    lp = n_blk * ATTN_BLOCK
    pad = lp - sub_len

    def to_sub(t):
        return t.reshape(b, h, sub_len, dilation, hd).transpose(0, 1, 3, 2, 4)

    qs = jnp.pad(to_sub(q), ((0, 0), (0, 0), (0, 0), (0, pad), (0, 0)))
    qb = qs.reshape(b, h, dilation, n_blk, ATTN_BLOCK, hd)
    halo = ((0, 0), (0, 0), (0, 0), (ATTN_BLOCK, pad + ATTN_BLOCK), (0, 0))
    kp = jnp.pad(to_sub(k), halo)
    vp = jnp.pad(to_sub(v), halo)

    def band(t):
        return jnp.concatenate(
            [t[:, :, :, o:o + lp].reshape(b, h, dilation, n_blk, ATTN_BLOCK, hd)
             for o in (0, ATTN_BLOCK, 2 * ATTN_BLOCK)], axis=-2)

    kb, vb = band(kp), band(vp)
    blk = jnp.arange(n_blk)[:, None, None] * ATTN_BLOCK
    q_idx = blk + jnp.arange(ATTN_BLOCK)[None, :, None]
    k_idx = blk - ATTN_BLOCK + jnp.arange(3 * ATTN_BLOCK)[None, None, :]
    valid = (jnp.abs(q_idx - k_idx) <= radius) & (k_idx >= 0) & (k_idx < sub_len)

    scores = jnp.einsum('bhrnqd,bhrnkd->bhrnqk', qb, kb,
                        preferred_element_type=jnp.float32) * (hd ** -0.5)
    scores = jnp.where(valid, scores, NEG_INF)
    m = jnp.max(scores, axis=-1, keepdims=True)
    p = jnp.exp(scores - m)
    den = jnp.sum(p, axis=-1, keepdims=True)
    o = jnp.einsum('bhrnqk,bhrnkd->bhrnqd', p, vb.astype(jnp.float32)) / den
    lse = (m + jnp.log(den))[..., 0]
    o = o.reshape(b, h, dilation, lp, hd)[:, :, :, :sub_len]
    o = o.transpose(0, 1, 3, 2, 4).reshape(b, h, s, hd)
    lse = lse.reshape(b, h, dilation, lp)[..., :sub_len].transpose(0, 1, 3, 2).reshape(b, h, s)
    return o, lse


def dilated_mixture_attention(q, k, v):
    outs, lses = [], []
    for window, dilation in DILATED_PATTERNS:
        o, lse = dilated_window_attention(q, k, v, dilation, window // (2 * dilation))
        outs.append(o)
        lses.append(lse)
    w = jax.nn.softmax(jnp.stack(lses, axis=0), axis=0)
    return jnp.einsum('pbhs,pbhsd->bhsd', w, jnp.stack(outs, axis=0))


def short_conv(u, w):
    return lax.conv_general_dilated(
        u, w[:, None, :].astype(u.dtype), window_strides=(1,), padding=((1, 1),),
        dimension_numbers=('NWC', 'WIO', 'NWC'), feature_group_count=u.shape[-1])


def mixer_ab(h, w_in, conv_w, w_out):
    b, s, _ = h.shape
    proj = h @ w_in
    cuts = np.cumsum([A_WIDTH] * 4 + [B_WIDTH] * 3).tolist()
    q, k, v, z_a, u_b, g_b, g_c, z_b = jnp.split(proj, cuts, axis=-1)
    pos = jnp.arange(s, dtype=jnp.float32)
    q = rope(q.reshape(b, s, A_HEADS, HEAD_DIM), pos).transpose(0, 2, 1, 3)
    k = rope(k.reshape(b, s, A_HEADS, HEAD_DIM), pos).transpose(0, 2, 1, 3)
    v = v.reshape(b, s, A_HEADS, HEAD_DIM).transpose(0, 2, 1, 3)
    attn = dilated_mixture_attention(q, k, v)
    y_a = attn.transpose(0, 2, 1, 3).reshape(b, s, A_WIDTH).astype(h.dtype) * jax.nn.silu(z_a)
    y_b = g_b * short_conv(g_c * u_b, conv_w) * jax.nn.silu(z_b)
    return jnp.concatenate([y_a, y_b], axis=-1) @ w_out


def mixer_sgu(h, w_in, ln_g, ln_b, w_s, b_s, w_out):
    b, s, _ = h.shape
    u, v, z = jnp.split(h @ w_in, 3, axis=-1)
    u = jax.nn.gelu(u)
    v = layer_norm(jax.nn.gelu(v), ln_g, ln_b)
    v = v.reshape(b, s // C_CHUNK, C_CHUNK, C_GROUPS, C_WIDTH // C_GROUPS)
    mixed = jnp.einsum('gts,bnsgc->bntgc', w_s, v) + b_s.T[None, None, :, :, None]
    y = u * mixed.reshape(b, s, C_WIDTH) * jax.nn.silu(z)
    return y @ w_out


def _fwd_setup_inputs(seed: int = 0) -> dict:
    key = jax.random.key(seed)
    ks = jax.random.split(key, 20)
    D = D_MODEL

    def nrm(k, shape, scale):
        return jax.random.normal(k, shape, jnp.float32) * scale

    return {
        'x': nrm(ks[0], (BATCH, SEQ, D), 1.0),
        'c': nrm(ks[1], (BATCH, D), 1.0),
        'ab_norm_g': 1.0 + nrm(ks[2], (N_EVEN, D), 0.02),
        'ab_w_mod': nrm(ks[3], (N_EVEN, D, 3 * D), 0.5 * D ** -0.5),
        'ab_b_mod': nrm(ks[4], (N_EVEN, 3 * D), 0.01),
        'ab_w_in': nrm(ks[5], (N_EVEN, D, AB_IN_WIDTH), D ** -0.5),
        'ab_conv_w': nrm(ks[6], (N_EVEN, B_CONV, B_WIDTH), B_CONV ** -0.5),
        'ab_w_out': nrm(ks[7], (N_EVEN, A_WIDTH + B_WIDTH, D), (A_WIDTH + B_WIDTH) ** -0.5),
        'sg_norm_g': 1.0 + nrm(ks[8], (N_ODD, D), 0.02),
        'sg_w_mod': nrm(ks[9], (N_ODD, D, 3 * D), 0.5 * D ** -0.5),
        'sg_b_mod': nrm(ks[10], (N_ODD, 3 * D), 0.01),
        'sg_w_in': nrm(ks[11], (N_ODD, D, SG_IN_WIDTH), D ** -0.5),
        'sg_ln_g': 1.0 + nrm(ks[12], (N_ODD, C_WIDTH), 0.02),
        'sg_ln_b': nrm(ks[13], (N_ODD, C_WIDTH), 0.01),
        'sg_w_s': nrm(ks[14], (N_ODD, C_GROUPS, C_CHUNK, C_CHUNK), C_CHUNK ** -0.5),
        'sg_b_s': 1.0 + nrm(ks[15], (N_ODD, C_GROUPS, C_CHUNK), 0.01),
        'sg_w_out': nrm(ks[16], (N_ODD, C_WIDTH, D), C_WIDTH ** -0.5),
        'final_norm_g': 1.0 + nrm(ks[17], (D,), 0.02),
    }


def _fwd_reference(x, c, ab_norm_g, ab_w_mod, ab_b_mod, ab_w_in, ab_conv_w, ab_w_out,
              sg_norm_g, sg_w_mod, sg_b_mod, sg_w_in, sg_ln_g, sg_ln_b, sg_w_s, sg_b_s,
              sg_w_out, final_norm_g):
    for layer in range(DEPTH):
        i = layer // 2
        if layer % 2 == 0:
            shift, scale, gate = ada_modulation(c, ab_w_mod[i], ab_b_mod[i])
            h = rms_norm(x, ab_norm_g[i]) * (1.0 + scale) + shift
            out = mixer_ab(h, ab_w_in[i], ab_conv_w[i], ab_w_out[i])
        else:
            shift, scale, gate = ada_modulation(c, sg_w_mod[i], sg_b_mod[i])
            h = rms_norm(x, sg_norm_g[i]) * (1.0 + scale) + shift
            out = mixer_sgu(h, sg_w_in[i], sg_ln_g[i], sg_ln_b[i], sg_w_s[i], sg_b_s[i], sg_w_out[i])
        x = x + gate * out
    return rms_norm(x, final_norm_g)


import jax as _jax
import jax.numpy as _jnp

TWIN_FORMAT = 'train_step'
FWD_PARAMS = ['x', 'c', 'ab_norm_g', 'ab_w_mod', 'ab_b_mod', 'ab_w_in', 'ab_conv_w', 'ab_w_out', 'sg_norm_g', 'sg_w_mod', 'sg_b_mod', 'sg_w_in', 'sg_ln_g', 'sg_ln_b', 'sg_w_s', 'sg_b_s', 'sg_w_out', 'final_norm_g']
TWIN_WEIGHTS = ['ab_norm_g', 'ab_w_mod', 'ab_b_mod', 'ab_w_in', 'ab_conv_w', 'ab_w_out', 'sg_norm_g', 'sg_w_mod', 'sg_b_mod', 'sg_w_in', 'sg_ln_g', 'sg_ln_b', 'sg_w_s', 'sg_b_s', 'sg_w_out', 'final_norm_g']
TWIN_DIFF_INPUT = 'x'
TWIN_INPUTS = ['x', 'c', 'ab_norm_g', 'ab_w_mod', 'ab_b_mod', 'ab_w_in', 'ab_conv_w', 'ab_w_out', 'sg_norm_g', 'sg_w_mod', 'sg_b_mod', 'sg_w_in', 'sg_ln_g', 'sg_ln_b', 'sg_w_s', 'sg_b_s', 'sg_w_out', 'final_norm_g', 'loss_target', 'm_ab_norm_g', 'm_ab_w_mod', 'm_ab_b_mod', 'm_ab_w_in', 'm_ab_conv_w', 'm_ab_w_out', 'm_sg_norm_g', 'm_sg_w_mod', 'm_sg_b_mod', 'm_sg_w_in', 'm_sg_ln_g', 'm_sg_ln_b', 'm_sg_w_s', 'm_sg_b_s', 'm_sg_w_out', 'm_final_norm_g', 'v_ab_norm_g', 'v_ab_w_mod', 'v_ab_b_mod', 'v_ab_w_in', 'v_ab_conv_w', 'v_ab_w_out', 'v_sg_norm_g', 'v_sg_w_mod', 'v_sg_b_mod', 'v_sg_w_in', 'v_sg_ln_g', 'v_sg_ln_b', 'v_sg_w_s', 'v_sg_b_s', 'v_sg_w_out', 'v_final_norm_g']
TWIN_OUTPUTS = ['loss', 'grad_x', 'grad_ab_norm_g', 'grad_ab_w_mod', 'grad_ab_b_mod', 'grad_ab_w_in', 'grad_ab_conv_w', 'grad_ab_w_out', 'grad_sg_norm_g', 'grad_sg_w_mod', 'grad_sg_b_mod', 'grad_sg_w_in', 'grad_sg_ln_g', 'grad_sg_ln_b', 'grad_sg_w_s', 'grad_sg_b_s', 'grad_sg_w_out', 'grad_final_norm_g', 'delta_ab_norm_g', 'delta_ab_w_mod', 'delta_ab_b_mod', 'delta_ab_w_in', 'delta_ab_conv_w', 'delta_ab_w_out', 'delta_sg_norm_g', 'delta_sg_w_mod', 'delta_sg_b_mod', 'delta_sg_w_in', 'delta_sg_ln_g', 'delta_sg_ln_b', 'delta_sg_w_s', 'delta_sg_b_s', 'delta_sg_w_out', 'delta_final_norm_g', 'new_m_ab_norm_g', 'new_m_ab_w_mod', 'new_m_ab_b_mod', 'new_m_ab_w_in', 'new_m_ab_conv_w', 'new_m_ab_w_out', 'new_m_sg_norm_g', 'new_m_sg_w_mod', 'new_m_sg_b_mod', 'new_m_sg_w_in', 'new_m_sg_ln_g', 'new_m_sg_ln_b', 'new_m_sg_w_s', 'new_m_sg_b_s', 'new_m_sg_w_out', 'new_m_final_norm_g', 'new_v_ab_norm_g', 'new_v_ab_w_mod', 'new_v_ab_b_mod', 'new_v_ab_w_in', 'new_v_ab_conv_w', 'new_v_ab_w_out', 'new_v_sg_norm_g', 'new_v_sg_w_mod', 'new_v_sg_b_mod', 'new_v_sg_w_in', 'new_v_sg_ln_g', 'new_v_sg_ln_b', 'new_v_sg_w_s', 'new_v_sg_b_s', 'new_v_sg_w_out', 'new_v_final_norm_g']
TWIN_LEAF_KINDS = {'loss': 'loss', 'grad_x': 'grad_x', 'grad_ab_norm_g': 'grad_w', 'grad_ab_w_mod': 'grad_w', 'grad_ab_b_mod': 'grad_w', 'grad_ab_w_in': 'grad_w', 'grad_ab_conv_w': 'grad_w', 'grad_ab_w_out': 'grad_w', 'grad_sg_norm_g': 'grad_w', 'grad_sg_w_mod': 'grad_w', 'grad_sg_b_mod': 'grad_w', 'grad_sg_w_in': 'grad_w', 'grad_sg_ln_g': 'grad_w', 'grad_sg_ln_b': 'grad_w', 'grad_sg_w_s': 'grad_w', 'grad_sg_b_s': 'grad_w', 'grad_sg_w_out': 'grad_w', 'grad_final_norm_g': 'grad_w', 'delta_ab_norm_g': 'delta_w', 'delta_ab_w_mod': 'delta_w', 'delta_ab_b_mod': 'delta_w', 'delta_ab_w_in': 'delta_w', 'delta_ab_conv_w': 'delta_w', 'delta_ab_w_out': 'delta_w', 'delta_sg_norm_g': 'delta_w', 'delta_sg_w_mod': 'delta_w', 'delta_sg_b_mod': 'delta_w', 'delta_sg_w_in': 'delta_w', 'delta_sg_ln_g': 'delta_w', 'delta_sg_ln_b': 'delta_w', 'delta_sg_w_s': 'delta_w', 'delta_sg_b_s': 'delta_w', 'delta_sg_w_out': 'delta_w', 'delta_final_norm_g': 'delta_w', 'new_m_ab_norm_g': 'new_m', 'new_m_ab_w_mod': 'new_m', 'new_m_ab_b_mod': 'new_m', 'new_m_ab_w_in': 'new_m', 'new_m_ab_conv_w': 'new_m', 'new_m_ab_w_out': 'new_m', 'new_m_sg_norm_g': 'new_m', 'new_m_sg_w_mod': 'new_m', 'new_m_sg_b_mod': 'new_m', 'new_m_sg_w_in': 'new_m', 'new_m_sg_ln_g': 'new_m', 'new_m_sg_ln_b': 'new_m', 'new_m_sg_w_s': 'new_m', 'new_m_sg_b_s': 'new_m', 'new_m_sg_w_out': 'new_m', 'new_m_final_norm_g': 'new_m', 'new_v_ab_norm_g': 'new_v', 'new_v_ab_w_mod': 'new_v', 'new_v_ab_b_mod': 'new_v', 'new_v_ab_w_in': 'new_v', 'new_v_ab_conv_w': 'new_v', 'new_v_ab_w_out': 'new_v', 'new_v_sg_norm_g': 'new_v', 'new_v_sg_w_mod': 'new_v', 'new_v_sg_b_mod': 'new_v', 'new_v_sg_w_in': 'new_v', 'new_v_sg_ln_g': 'new_v', 'new_v_sg_ln_b': 'new_v', 'new_v_sg_w_s': 'new_v', 'new_v_sg_b_s': 'new_v', 'new_v_sg_w_out': 'new_v', 'new_v_final_norm_g': 'new_v'}


def _forward(args):
    return _fwd_reference(*[args[k] for k in FWD_PARAMS])


def _output_shape():
    def fwd():
        inp = _fwd_setup_inputs(0)
        return _fwd_reference(*[inp[k] for k in FWD_PARAMS])
    out = _jax.eval_shape(fwd)
    return out.shape, out.dtype

N_MICROBATCH = 1
ADAM_LR = 0.001
ADAM_B1 = 0.9
ADAM_B2 = 0.999
ADAM_EPS = 1e-08
ADAM_WD = 0.01
ADAM_STEP = 10
PER_EXAMPLE_BATCH_AXIS = {'x': 0, 'c': 0, 'loss_target': 0}
SHARED_INPUTS = []
_WEIGHT_DTYPES = {'ab_norm_g': _jnp.float32, 'ab_w_mod': _jnp.float32, 'ab_b_mod': _jnp.float32, 'ab_w_in': _jnp.float32, 'ab_conv_w': _jnp.float32, 'ab_w_out': _jnp.float32, 'sg_norm_g': _jnp.float32, 'sg_w_mod': _jnp.float32, 'sg_b_mod': _jnp.float32, 'sg_w_in': _jnp.float32, 'sg_ln_g': _jnp.float32, 'sg_ln_b': _jnp.float32, 'sg_w_s': _jnp.float32, 'sg_b_s': _jnp.float32, 'sg_w_out': _jnp.float32, 'final_norm_g': _jnp.float32}
MOMENT_SCALE = {'ab_norm_g': 3.220218e-02, 'ab_w_mod': 2.555504e-02, 'ab_b_mod': 4.300484e-02, 'ab_w_in': 1.667853e-02, 'ab_conv_w': 2.282339e-02, 'ab_w_out': 1.672882e-02, 'sg_norm_g': 2.802078e-02, 'sg_w_mod': 2.445771e-02, 'sg_b_mod': 4.087418e-02, 'sg_w_in': 1.672868e-02, 'sg_ln_g': 1.222811e-02, 'sg_ln_b': 1.204963e-02, 'sg_w_s': 1.744000e-02, 'sg_b_s': 1.767748e-02, 'sg_w_out': 1.729681e-02, 'final_norm_g': 1.599449e+01}


def _to_microbatches(a, axis):
    t = _jnp.moveaxis(a, axis, 0)
    t = t.reshape((N_MICROBATCH, t.shape[0] // N_MICROBATCH) + t.shape[1:])
    return _jnp.moveaxis(t, 1, axis + 1)


def setup_inputs(seed: int = 0) -> dict:
    inp = _fwd_setup_inputs(seed)
    key = _jax.random.fold_in(_jax.random.key(seed), 7919)
    shape, _ = _output_shape()
    out = dict(inp)
    out["loss_target"] = _jax.random.normal(_jax.random.fold_in(key, 0), shape, _jnp.float32)
    for i, name in enumerate(TWIN_WEIGHTS):
        w = inp[name].astype(_jnp.float32)
        if MOMENT_SCALE is None:
            s = _jnp.sqrt(_jnp.mean(_jnp.square(w)) + 1e-30)
        else:
            s = MOMENT_SCALE[name]
        km, kv = _jax.random.split(_jax.random.fold_in(key, i + 1))
        out[name] = w
        out["m_" + name] = s * _jax.random.normal(km, w.shape, _jnp.float32)
        out["v_" + name] = (s * s) * _jax.random.uniform(kv, w.shape, _jnp.float32, 0.5, 1.5)
    if N_MICROBATCH > 1:
        for name, axis in PER_EXAMPLE_BATCH_AXIS.items():
            out[name] = _to_microbatches(out[name], axis)
    return {'x': out['x'], 'c': out['c'], 'ab_norm_g': out['ab_norm_g'], 'ab_w_mod': out['ab_w_mod'], 'ab_b_mod': out['ab_b_mod'], 'ab_w_in': out['ab_w_in'], 'ab_conv_w': out['ab_conv_w'], 'ab_w_out': out['ab_w_out'], 'sg_norm_g': out['sg_norm_g'], 'sg_w_mod': out['sg_w_mod'], 'sg_b_mod': out['sg_b_mod'], 'sg_w_in': out['sg_w_in'], 'sg_ln_g': out['sg_ln_g'], 'sg_ln_b': out['sg_ln_b'], 'sg_w_s': out['sg_w_s'], 'sg_b_s': out['sg_b_s'], 'sg_w_out': out['sg_w_out'], 'final_norm_g': out['final_norm_g'], 'loss_target': out['loss_target'], 'm_ab_norm_g': out['m_ab_norm_g'], 'm_ab_w_mod': out['m_ab_w_mod'], 'm_ab_b_mod': out['m_ab_b_mod'], 'm_ab_w_in': out['m_ab_w_in'], 'm_ab_conv_w': out['m_ab_conv_w'], 'm_ab_w_out': out['m_ab_w_out'], 'm_sg_norm_g': out['m_sg_norm_g'], 'm_sg_w_mod': out['m_sg_w_mod'], 'm_sg_b_mod': out['m_sg_b_mod'], 'm_sg_w_in': out['m_sg_w_in'], 'm_sg_ln_g': out['m_sg_ln_g'], 'm_sg_ln_b': out['m_sg_ln_b'], 'm_sg_w_s': out['m_sg_w_s'], 'm_sg_b_s': out['m_sg_b_s'], 'm_sg_w_out': out['m_sg_w_out'], 'm_final_norm_g': out['m_final_norm_g'], 'v_ab_norm_g': out['v_ab_norm_g'], 'v_ab_w_mod': out['v_ab_w_mod'], 'v_ab_b_mod': out['v_ab_b_mod'], 'v_ab_w_in': out['v_ab_w_in'], 'v_ab_conv_w': out['v_ab_conv_w'], 'v_ab_w_out': out['v_ab_w_out'], 'v_sg_norm_g': out['v_sg_norm_g'], 'v_sg_w_mod': out['v_sg_w_mod'], 'v_sg_b_mod': out['v_sg_b_mod'], 'v_sg_w_in': out['v_sg_w_in'], 'v_sg_ln_g': out['v_sg_ln_g'], 'v_sg_ln_b': out['v_sg_ln_b'], 'v_sg_w_s': out['v_sg_w_s'], 'v_sg_b_s': out['v_sg_b_s'], 'v_sg_w_out': out['v_sg_w_out'], 'v_final_norm_g': out['v_final_norm_g']}


def _loss(weights, diff, rest, loss_target):
    with _jax.named_scope("forward"):
        args = {**rest, TWIN_DIFF_INPUT: diff, **{k: w.astype(_WEIGHT_DTYPES[k]) for k, w in weights.items()}}
        y = _forward(args)
    with _jax.named_scope("loss_head"):
        err = _jnp.square(y.astype(_jnp.float32) - loss_target)
        return 0.5 * _jnp.sum(_jnp.mean(err, axis=-1)) if err.ndim else 0.5 * err


def _adamw(w, g, m, v):
    m = ADAM_B1 * m + (1.0 - ADAM_B1) * g
    v = ADAM_B2 * v + (1.0 - ADAM_B2) * _jnp.square(g)
    m_hat = m / (1.0 - ADAM_B1 ** ADAM_STEP)
    v_hat = v / (1.0 - ADAM_B2 ** ADAM_STEP)
    delta = -ADAM_LR * (m_hat / (_jnp.sqrt(v_hat) + ADAM_EPS) + ADAM_WD * w)
    return delta, m, v


def reference(x, c, ab_norm_g, ab_w_mod, ab_b_mod, ab_w_in, ab_conv_w, ab_w_out, sg_norm_g, sg_w_mod, sg_b_mod, sg_w_in, sg_ln_g, sg_ln_b, sg_w_s, sg_b_s, sg_w_out, final_norm_g, loss_target, m_ab_norm_g, m_ab_w_mod, m_ab_b_mod, m_ab_w_in, m_ab_conv_w, m_ab_w_out, m_sg_norm_g, m_sg_w_mod, m_sg_b_mod, m_sg_w_in, m_sg_ln_g, m_sg_ln_b, m_sg_w_s, m_sg_b_s, m_sg_w_out, m_final_norm_g, v_ab_norm_g, v_ab_w_mod, v_ab_b_mod, v_ab_w_in, v_ab_conv_w, v_ab_w_out, v_sg_norm_g, v_sg_w_mod, v_sg_b_mod, v_sg_w_in, v_sg_ln_g, v_sg_ln_b, v_sg_w_s, v_sg_b_s, v_sg_w_out, v_final_norm_g):
    given = dict(x=x, c=c, ab_norm_g=ab_norm_g, ab_w_mod=ab_w_mod, ab_b_mod=ab_b_mod, ab_w_in=ab_w_in, ab_conv_w=ab_conv_w, ab_w_out=ab_w_out, sg_norm_g=sg_norm_g, sg_w_mod=sg_w_mod, sg_b_mod=sg_b_mod, sg_w_in=sg_w_in, sg_ln_g=sg_ln_g, sg_ln_b=sg_ln_b, sg_w_s=sg_w_s, sg_b_s=sg_b_s, sg_w_out=sg_w_out, final_norm_g=final_norm_g, loss_target=loss_target, m_ab_norm_g=m_ab_norm_g, m_ab_w_mod=m_ab_w_mod, m_ab_b_mod=m_ab_b_mod, m_ab_w_in=m_ab_w_in, m_ab_conv_w=m_ab_conv_w, m_ab_w_out=m_ab_w_out, m_sg_norm_g=m_sg_norm_g, m_sg_w_mod=m_sg_w_mod, m_sg_b_mod=m_sg_b_mod, m_sg_w_in=m_sg_w_in, m_sg_ln_g=m_sg_ln_g, m_sg_ln_b=m_sg_ln_b, m_sg_w_s=m_sg_w_s, m_sg_b_s=m_sg_b_s, m_sg_w_out=m_sg_w_out, m_final_norm_g=m_final_norm_g, v_ab_norm_g=v_ab_norm_g, v_ab_w_mod=v_ab_w_mod, v_ab_b_mod=v_ab_b_mod, v_ab_w_in=v_ab_w_in, v_ab_conv_w=v_ab_conv_w, v_ab_w_out=v_ab_w_out, v_sg_norm_g=v_sg_norm_g, v_sg_w_mod=v_sg_w_mod, v_sg_b_mod=v_sg_b_mod, v_sg_w_in=v_sg_w_in, v_sg_ln_g=v_sg_ln_g, v_sg_ln_b=v_sg_ln_b, v_sg_w_s=v_sg_w_s, v_sg_b_s=v_sg_b_s, v_sg_w_out=v_sg_w_out, v_final_norm_g=v_final_norm_g)
    weights = {n: given[n] for n in TWIN_WEIGHTS}
    shared = {n: given[n] for n in SHARED_INPUTS}
    per_example = {n: given[n] for n in ['x', 'c']}
    grad_fn = _jax.value_and_grad(_loss, argnums=(0, 1))

    def one_microbatch(ex, loss_target):
        ex = dict(ex)
        diff = ex.pop(TWIN_DIFF_INPUT)
        return grad_fn(weights, diff, {**shared, **ex}, loss_target)

    if N_MICROBATCH == 1:
        loss, (grad_w, grad_x) = one_microbatch(per_example, given["loss_target"])
    else:
        def body(carry, xs):
            loss_sum, grad_sum = carry
            l_k, (gw_k, gx_k) = one_microbatch(xs[0], xs[1])
            with _jax.named_scope("update"):
                return (loss_sum + l_k, _jax.tree.map(_jnp.add, grad_sum, gw_k)), gx_k

        init = (_jnp.zeros((), _jnp.float32), _jax.tree.map(_jnp.zeros_like, weights))
        (loss, grad_w), grad_x = _jax.lax.scan(body, init, (per_example, given["loss_target"]))
    with _jax.named_scope("update"):
        delta_w, new_m, new_v = {}, {}, {}
        for n in TWIN_WEIGHTS:
            delta_w[n], new_m[n], new_v[n] = _adamw(weights[n], grad_w[n], given["m_" + n], given["v_" + n])
    return (loss, grad_x, *[grad_w[n] for n in TWIN_WEIGHTS], *[delta_w[n] for n in TWIN_WEIGHTS],
            *[new_m[n] for n in TWIN_WEIGHTS], *[new_v[n] for n in TWIN_WEIGHTS])
```

```python
import functools
import math

import numpy as np
import jax
import jax.numpy as jnp
from jax import lax
from jax.experimental import pallas as pl
from jax.experimental.pallas import tpu as pltpu

F32 = jnp.float32
BF16 = jnp.bfloat16

NDEV = 8
NCHIP = 4
EPS = 1e-6
HEAD_DIM = 128
ROPE_THETA = 10000.0
DILATED_PATTERNS = ((128, 1), (512, 4), (2048, 16))
NEG_INF = -1e30
C_CHUNK = 128
C_GROUPS = 8
ADAM_LR = 0.001
ADAM_B1 = 0.9
ADAM_B2 = 0.999
ADAM_EPS = 1e-08
ADAM_WD = 0.01
ADAM_STEP = 10
GELU_K = math.sqrt(2.0 / math.pi)
GELU_C = 0.044715

VMEM_LIMIT_BYTES = 56 * 1024 * 1024
ATTN_TILE = 512
ROW_TILE = 256
MESH = pl.DeviceIdType.MESH
ANY = pl.BlockSpec(memory_space=pl.ANY)


def _cp(*sem):
    return pltpu.CompilerParams(dimension_semantics=sem, vmem_limit_bytes=VMEM_LIMIT_BYTES)


def _sigmoid(z):
    return 1.0 / (1.0 + jnp.exp(-z))


def _silu_and_grad(z):
    s = _sigmoid(z)
    return z * s, s * (1.0 + z * (1.0 - s))


def _gelu_and_grad(x):
    x2 = x * x
    t = jnp.tanh(GELU_K * (x + GELU_C * x2 * x))
    g = 0.5 * x * (1.0 + t)
    dg = 0.5 * (1.0 + t) + 0.5 * x * (1.0 - t * t) * (GELU_K * (1.0 + 3.0 * GELU_C * x2))
    return g, dg


def _position():
    return lax.axis_index("x"), lax.axis_index("y"), lax.axis_index("c")


def _all_gather(arrs, name):
    n = len(arrs)

    def body(*refs):
        ins, outs = refs[:n], refs[n:2 * n]
        send_sems, recv_sems, local_sems = refs[2 * n:]
        x, y, c = _position()
        me, sibling = (x, y, c), (x, y, 1 - c)
        chips = [(1 - x, y), (x, 1 - y), (1 - x, 1 - y)]

        def copy(a, k, block, to, src=None):
            dst = outs[a].at[4 * block[0] + 2 * block[1] + block[2]]
            return pltpu.make_async_remote_copy(
                src_ref=dst if src is None else src, dst_ref=dst,
                send_sem=send_sems.at[a, k], recv_sem=recv_sems.at[a, k],
                device_id=to, device_id_type=MESH)

        mine = [pltpu.make_async_copy(ins[a], outs[a].at[4 * x + 2 * y + c], local_sems.at[a])
                for a in range(n)]
        for cp in mine:
            cp.start()
        first = []
        for a in range(n):
            first.append(copy(a, 0, me, sibling, src=ins[a]))
            first += [copy(a, 1 + j, me, (*chip, c), src=ins[a]) for j, chip in enumerate(chips)]
        for cp in first:
            cp.start()
        passed = []
        for j, chip in enumerate(chips):
            for a in range(n):
                copy(a, 1 + j, (*chip, c), me).wait_recv()
                fwd = copy(a, 4 + j, (*chip, c), sibling)
                fwd.start()
                passed.append(fwd)
        for a in range(n):
            copy(a, 0, sibling, me).wait_recv()
            for j, chip in enumerate(chips):
                copy(a, 4 + j, (*chip, 1 - c), me).wait_recv()
        for cp in first + passed:
            cp.wait_send()
        for cp in mine:
            cp.wait()

    return pl.pallas_call(
        body, name=name,
        out_shape=[jax.ShapeDtypeStruct((NDEV,) + a.shape, a.dtype) for a in arrs],
        in_specs=[ANY] * n, out_specs=[ANY] * n,
        scratch_shapes=[pltpu.SemaphoreType.DMA((n, 7)), pltpu.SemaphoreType.DMA((n, 7)),
                        pltpu.SemaphoreType.DMA((n,))],
    )(*arrs)


def _rs_sibling(gs, name):
    n = len(gs)

    def body(*refs):
        ins, outs = refs[:n], refs[n:2 * n]
        send_sems, recv_sems = refs[2 * n:]
        x, y, c = _position()
        copies = []
        for a in range(n):
            for k in range(NCHIP):
                copies.append(pltpu.make_async_remote_copy(
                    src_ref=ins[a].at[2 * k + (1 - c)], dst_ref=outs[a].at[k],
                    send_sem=send_sems.at[a, k], recv_sem=recv_sems.at[a, k],
                    device_id=(x, y, 1 - c), device_id_type=MESH))
        for cp in copies:
            cp.start()
        for cp in copies:
            cp.wait_recv()
        for cp in copies:
            cp.wait_send()

    return pl.pallas_call(
        body, name=name,
        out_shape=[jax.ShapeDtypeStruct((NCHIP,) + g.shape[1:], g.dtype) for g in gs],
        in_specs=[ANY] * n, out_specs=[ANY] * n,
        scratch_shapes=[pltpu.SemaphoreType.DMA((n, NCHIP)), pltpu.SemaphoreType.DMA((n, NCHIP))],
    )(*gs)


def _rs_chips(ps, name):
    n = len(ps)

    def body(*refs):
        ins, outs = refs[:n], refs[n:2 * n]
        send_sems, recv_sems, local_sems = refs[2 * n:]
        x, y, c = _position()
        mychip = 2 * x + y
        chips = [(1 - x, y), (x, 1 - y), (1 - x, 1 - y)]
        mine = [pltpu.make_async_copy(ins[a].at[mychip], outs[a].at[mychip], local_sems.at[a])
                for a in range(n)]
        for cp in mine:
            cp.start()
        sends = []
        for a in range(n):
            for j, chip in enumerate(chips):
                sends.append(pltpu.make_async_remote_copy(
                    src_ref=ins[a].at[2 * chip[0] + chip[1]], dst_ref=outs[a].at[mychip],
                    send_sem=send_sems.at[a, j], recv_sem=recv_sems.at[a, j],
                    device_id=(*chip, c), device_id_type=MESH))
        for cp in sends:
            cp.start()
        for a in range(n):
            for j, chip in enumerate(chips):
                slot = outs[a].at[2 * chip[0] + chip[1]]
                pltpu.make_async_remote_copy(
                    src_ref=slot, dst_ref=slot, send_sem=send_sems.at[a, j], recv_sem=recv_sems.at[a, j],
                    device_id=(*chip, c), device_id_type=MESH).wait_recv()
        for cp in sends:
            cp.wait_send()
        for cp in mine:
            cp.wait()

    return pl.pallas_call(
        body, name=name,
        out_shape=[jax.ShapeDtypeStruct(p.shape, p.dtype) for p in ps],
        in_specs=[ANY] * n, out_specs=[ANY] * n,
        scratch_shapes=[pltpu.SemaphoreType.DMA((n, 3)), pltpu.SemaphoreType.DMA((n, 3)),
                        pltpu.SemaphoreType.DMA((n,))],
    )(*ps)


def _adamw(w, g, m, v):
    m2 = ADAM_B1 * m + (1.0 - ADAM_B1) * g
    v2 = ADAM_B2 * v + (1.0 - ADAM_B2) * (g * g)
    m_hat = m2 / (1.0 - ADAM_B1 ** ADAM_STEP)
    v_hat = v2 / (1.0 - ADAM_B2 ** ADAM_STEP)
    delta = -ADAM_LR * (m_hat / (jnp.sqrt(v_hat) + ADAM_EPS) + ADAM_WD * w)
    return delta, m2, v2


def _add_sibling(g, recv, c_idx, name):
    _, R, C = g.shape
    tr = min(R, 512)

    def body(c_ref, g_ref, r_ref, o_ref):
        o_ref[...] = g_ref[...] + r_ref[...]

    return pl.pallas_call(
        body, name=name,
        grid_spec=pltpu.PrefetchScalarGridSpec(
            num_scalar_prefetch=1, grid=(NCHIP, R // tr),
            in_specs=[pl.BlockSpec((1, tr, C), lambda k, i, c_ref: (2 * k + c_ref[0], i, 0)),
                      pl.BlockSpec((1, tr, C), lambda k, i, c_ref: (k, i, 0))],
            out_specs=pl.BlockSpec((1, tr, C), lambda k, i, c_ref: (k, i, 0))),
        out_shape=jax.ShapeDtypeStruct((NCHIP, R, C), F32),
        compiler_params=_cp("parallel", "parallel"),
    )(c_idx, g, recv)


def _sum_adam(parts, w, m, v, name):
    K, R, C = parts.shape
    tr = min(R, 256)

    def body(p_ref, w_ref, m_ref, v_ref, g_ref, d_ref, m2_ref, v2_ref):
        g = p_ref[0]
        for k in range(1, K):
            g = g + p_ref[k]
        delta, m2, v2 = _adamw(w_ref[...], g, m_ref[...], v_ref[...])
        g_ref[...] = g
        d_ref[...] = delta
        m2_ref[...] = m2
        v2_ref[...] = v2

    blk = pl.BlockSpec((tr, C), lambda i: (i, 0))
    shp = jax.ShapeDtypeStruct((R, C), F32)
    return pl.pallas_call(
        body, name=name, grid=(R // tr,),
        in_specs=[pl.BlockSpec((K, tr, C), lambda i: (0, i, 0)), blk, blk, blk],
        out_specs=[blk] * 4, out_shape=[shp] * 4,
        compiler_params=_cp("parallel"),
    )(parts, w, m, v)


def _sum_rows(parts, name):
    K, R, C = parts.shape
    tr = min(R, 256)
    while R % tr:
        tr //= 2

    def body(p_ref, o_ref):
        g = p_ref[0]
        for k in range(1, K):
            g = g + p_ref[k]
        o_ref[...] = g

    return pl.pallas_call(
        body, name=name, grid=(R // tr,),
        in_specs=[pl.BlockSpec((K, tr, C), lambda i: (0, i, 0))],
        out_specs=pl.BlockSpec((tr, C), lambda i: (i, 0)),
        out_shape=jax.ShapeDtypeStruct((R, C), F32),
        compiler_params=_cp("parallel"),
    )(parts)


def _adam_only(g, w, m, v, name):
    R, C = g.shape
    tr = min(R, 256)
    while R % tr:
        tr //= 2

    def body(g_ref, w_ref, m_ref, v_ref, d_ref, m2_ref, v2_ref):
        delta, m2, v2 = _adamw(w_ref[...], g_ref[...], m_ref[...], v_ref[...])
        d_ref[...] = delta
        m2_ref[...] = m2
        v2_ref[...] = v2

    blk = pl.BlockSpec((tr, C), lambda i: (i, 0))
    shp = jax.ShapeDtypeStruct((R, C), F32)
    return pl.pallas_call(
        body, name=name, grid=(R // tr,), in_specs=[blk] * 4, out_specs=[blk] * 3,
        out_shape=[shp] * 3, compiler_params=_cp("parallel"),
    )(g, w, m, v)


def _mod_fwd(c_all, w_mod, b_cols, name):
    L, D, n = w_mod.shape
    B = c_all.shape[0]

    def body(c_ref, w_ref, b_ref, o_ref):
        cv = c_ref[...]
        sc = (cv * _sigmoid(cv)).astype(BF16)
        o_ref[0] = jnp.dot(sc, w_ref[0].astype(BF16), preferred_element_type=F32) + b_ref[0]

    return pl.pallas_call(
        body, name=name, grid=(L,),
        in_specs=[pl.BlockSpec((B, D), lambda l: (0, 0)),
                  pl.BlockSpec((1, D, n), lambda l: (l, 0, 0)),
                  pl.BlockSpec((1, 1, n), lambda l: (l, 0, 0))],
        out_specs=pl.BlockSpec((1, B, n), lambda l: (l, 0, 0)),
        out_shape=jax.ShapeDtypeStruct((L, B, n), F32),
        compiler_params=_cp("parallel"),
    )(c_all, w_mod, b_cols)


def _wmod_grad_adam(sc_t, dm, w, m, v, name):
    L, D, n = w.shape
    KP = sc_t.shape[1]
    tr = min(D, 512)

    def body(s_ref, dm_ref, w_ref, m_ref, v_ref, g_ref, d_ref, m2_ref, v2_ref):
        g = jnp.dot(s_ref[...], dm_ref[0], preferred_element_type=F32,
                    precision=lax.Precision.HIGHEST)
        delta, m2, v2 = _adamw(w_ref[0], g, m_ref[0], v_ref[0])
        g_ref[0] = g
        d_ref[0] = delta
        m2_ref[0] = m2
        v2_ref[0] = v2

    blk = pl.BlockSpec((1, tr, n), lambda l, i: (l, i, 0))
    shp = jax.ShapeDtypeStruct((L, D, n), F32)
    return pl.pallas_call(
        body, name=name, grid=(L, D // tr),
        in_specs=[pl.BlockSpec((tr, KP), lambda l, i: (i, 0)),
                  pl.BlockSpec((1, KP, n), lambda l, i: (l, 0, 0)), blk, blk, blk],
        out_specs=[blk] * 4, out_shape=[shp] * 4,
        compiler_params=_cp("parallel", "parallel"),
    )(sc_t, dm, w, m, v)


def _vec_spec(D):
    return pl.BlockSpec((1, D), lambda i: (0, 0))


def _pre(x, res, gate, g, scale, shift, name):
    S, D = x.shape
    tr = min(S, ROW_TILE)
    has_res = res is not None
    row = pl.BlockSpec((tr, D), lambda i: (i, 0))

    def body(*refs):
        if has_res:
            x_ref, r_ref, gate_ref, g_ref, sc_ref, sh_ref, xl_ref, h_ref = refs
            xv = x_ref[...] + gate_ref[...] * r_ref[...]
            xl_ref[...] = xv
        else:
            x_ref, g_ref, sc_ref, sh_ref, h_ref = refs
            xv = x_ref[...]
        r = lax.rsqrt(jnp.mean(xv * xv, axis=-1, keepdims=True) + EPS)
        y = (xv * r) * g_ref[...]
        h_ref[...] = (y * (1.0 + sc_ref[...]) + sh_ref[...]).astype(BF16)

    vec = _vec_spec(D)
    if has_res:
        xl, h = pl.pallas_call(
            body, name=name, grid=(S // tr,),
            in_specs=[row, row, vec, vec, vec, vec], out_specs=[row, row],
            out_shape=[jax.ShapeDtypeStruct((S, D), F32), jax.ShapeDtypeStruct((S, D), BF16)],
            compiler_params=_cp("parallel"),
        )(x, res, gate, g, scale, shift)
        return xl, h
    h = pl.pallas_call(
        body, name=name, grid=(S // tr,),
        in_specs=[row, vec, vec, vec], out_specs=row,
        out_shape=jax.ShapeDtypeStruct((S, D), BF16),
        compiler_params=_cp("parallel"),
    )(x, g, scale, shift)
    return x, h


def _pre_bwd(xl, dh, dx_in, g, scale, name):
    S, D = xl.shape
    tr = min(S, ROW_TILE)
    nsteps = S // tr
    row = pl.BlockSpec((tr, D), lambda i: (i, 0))
    vec = _vec_spec(D)

    def body(x_ref, dh_ref, dxin_ref, g_ref, sc_ref, dx_ref, dsh_ref, dsc_ref, dg_ref, acc_sh, acc_t):
        i = pl.program_id(0)
        xv = x_ref[...]
        dh = dh_ref[...]
        r = lax.rsqrt(jnp.mean(xv * xv, axis=-1, keepdims=True) + EPS)
        xn = xv * r
        part_sh = jnp.sum(dh.reshape(tr // 8, 8, D), axis=0)
        part_t = jnp.sum((dh * xn).reshape(tr // 8, 8, D), axis=0)

        @pl.when(i == 0)
        def _():
            acc_sh[...] = part_sh
            acc_t[...] = part_t

        @pl.when(i > 0)
        def _():
            acc_sh[...] += part_sh
            acc_t[...] += part_t

        dxn = dh * (g_ref[...] * (1.0 + sc_ref[...]))
        dx_ref[...] = dxin_ref[...] + r * (dxn - xn * jnp.mean(dxn * xn, axis=-1, keepdims=True))

        @pl.when(i == nsteps - 1)
        def _():
            t = jnp.sum(acc_t[...], axis=0, keepdims=True)
            dsh_ref[...] = jnp.sum(acc_sh[...], axis=0, keepdims=True)
            dsc_ref[...] = t * g_ref[...]
            dg_ref[...] = t * (1.0 + sc_ref[...])

    v = jax.ShapeDtypeStruct((1, D), F32)
    return pl.pallas_call(
        body, name=name, grid=(nsteps,),
        in_specs=[row, row, row, vec, vec], out_specs=[row, vec, vec, vec],
        out_shape=[jax.ShapeDtypeStruct((S, D), F32), v, v, v],
        scratch_shapes=[pltpu.VMEM((8, D), F32), pltpu.VMEM((8, D), F32)],
        compiler_params=_cp("arbitrary"),
    )(xl, dh, dx_in, g, scale)


def _post_bwd(dx, out, gate, name):
    S, D = dx.shape
    tr = min(S, ROW_TILE)
    nsteps = S // tr
    row = pl.BlockSpec((tr, D), lambda i: (i, 0))
    vec = _vec_spec(D)

    def body(dx_ref, o_ref, gate_ref, do_ref, dg_ref, acc):
        i = pl.program_id(0)
        dxv = dx_ref[...]
        do_ref[...] = (dxv * gate_ref[...]).astype(BF16)
        part = jnp.sum((dxv * o_ref[...]).reshape(tr // 8, 8, D), axis=0)

        @pl.when(i == 0)
        def _():
            acc[...] = part

        @pl.when(i > 0)
        def _():
            acc[...] += part

        @pl.when(i == nsteps - 1)
        def _():
            dg_ref[...] = jnp.sum(acc[...], axis=0, keepdims=True)

    return pl.pallas_call(
        body, name=name, grid=(nsteps,),
        in_specs=[row, row, vec], out_specs=[row, vec],
        out_shape=[jax.ShapeDtypeStruct((S, D), BF16), jax.ShapeDtypeStruct((1, D), F32)],
        scratch_shapes=[pltpu.VMEM((8, D), F32)],
        compiler_params=_cp("arbitrary"),
    )(dx, out, gate)


def _loss_head(x, res, gate, gf, tgt, name):
    S, D = x.shape
    tr = min(S, ROW_TILE)
    nsteps = S // tr
    row = pl.BlockSpec((tr, D), lambda i: (i, 0))
    vec = _vec_spec(D)

    def body(x_ref, r_ref, gate_ref, gf_ref, t_ref, dx_ref, loss_ref, dgf_ref, acc, lacc):
        i = pl.program_id(0)
        xv = x_ref[...] + gate_ref[...] * r_ref[...]
        r = lax.rsqrt(jnp.mean(xv * xv, axis=-1, keepdims=True) + EPS)
        xn = xv * r
        err = xn * gf_ref[...] - t_ref[...]
        row_loss = jnp.mean(err * err, axis=-1, keepdims=True)
        lpart = 0.5 * jnp.sum(row_loss, axis=0, keepdims=True)
        dy = err * (1.0 / D)
        part = jnp.sum((dy * xn).reshape(tr // 8, 8, D), axis=0)

        @pl.when(i == 0)
        def _():
            acc[...] = part
            lacc[...] = lpart

        @pl.when(i > 0)
        def _():
            acc[...] += part
            lacc[...] += lpart

        dxn = dy * gf_ref[...]
        dx_ref[...] = r * (dxn - xn * jnp.mean(dxn * xn, axis=-1, keepdims=True))

        @pl.when(i == nsteps - 1)
        def _():
            dgf_ref[...] = jnp.sum(acc[...], axis=0, keepdims=True)
            loss_ref[...] = lacc[...]

    return pl.pallas_call(
        body, name=name, grid=(nsteps,),
        in_specs=[row, row, vec, vec, row],
        out_specs=[row, pl.BlockSpec((1, 1), lambda i: (0, 0)), vec],
        out_shape=[jax.ShapeDtypeStruct((S, D), F32), jax.ShapeDtypeStruct((1, 1), F32),
                   jax.ShapeDtypeStruct((1, D), F32)],
        scratch_shapes=[pltpu.VMEM((8, D), F32), pltpu.VMEM((1, 1), F32)],
        compiler_params=_cp("arbitrary"),
    )(x, res, gate, gf, tgt)


NN = (((1,), (0,)), ((), ()))
NT = (((1,), (1,)), ((), ()))
TN = (((0,), (0,)), ((), ()))


def _mm(name, a, b, out_shape, grid, a_spec, b_spec, o_spec, dims, a2d, b2d, k_axis, sem, alias=None):
    def body(*refs):
        a_ref, b_ref, o_ref = refs[0], refs[1], refs[-1]
        r = lax.dot_general(a_ref[...].reshape(a2d), b_ref[...].reshape(b2d), dims,
                            preferred_element_type=F32)
        r = r.reshape(o_ref.shape)
        if k_axis is None:
            o_ref[...] = r.astype(o_ref.dtype)
        else:
            k = pl.program_id(k_axis)

            @pl.when(k == 0)
            def _():
                o_ref[...] = r

            @pl.when(k > 0)
            def _():
                o_ref[...] += r

    operands, in_specs, aliases = [a, b], [a_spec, b_spec], {}
    if alias is not None:
        operands.append(alias)
        in_specs.append(ANY)
        aliases = {2: 0}
    return pl.pallas_call(
        body, name=name, grid=grid, in_specs=in_specs, out_specs=o_spec, out_shape=out_shape,
        input_output_aliases=aliases, compiler_params=_cp(*sem),
    )(*operands)


def _tile(n, pref):
    t = min(n, pref)
    while n % t:
        t -= 128
    return t


def _mm_nn_in(a, w, l, name):
    M, K = a.shape
    _, _, _, n = w.shape
    tm, tn = min(M, 512), _tile(n, 1024)
    nb = n // tn
    return _mm(name, a, w, jax.ShapeDtypeStruct((M, NDEV * n), F32), (NDEV * nb, M // tm),
               pl.BlockSpec((tm, K), lambda j, i: (i, 0)),
               pl.BlockSpec((1, 1, K, tn), lambda j, i: (j // nb, l, 0, j % nb)),
               pl.BlockSpec((tm, tn), lambda j, i: (i, j)),
               NN, (tm, K), (K, tn), None, ("parallel", "parallel"))


def _mm_nn_out(a, w, l, name):
    M, K = a.shape
    _, _, kb, N = w.shape
    tm, tn = min(M, 512), _tile(N, 1024)
    return _mm(name, a, w, jax.ShapeDtypeStruct((M, N), F32), (N // tn, M // tm),
               pl.BlockSpec((tm, K), lambda j, i: (i, 0)),
               pl.BlockSpec((NDEV, 1, kb, tn), lambda j, i: (0, l, 0, j)),
               pl.BlockSpec((tm, tn), lambda j, i: (i, j)),
               NN, (tm, K), (K, tn), None, ("parallel", "parallel"))


def _mm_nt_in(a, w, l, name):
    M, _ = a.shape
    _, _, K, n = w.shape
    tm, tk, tc = min(M, 1024), _tile(K, 1024), _tile(n, 1024)
    nb = n // tc
    return _mm(name, a, w, jax.ShapeDtypeStruct((M, K), F32), (M // tm, K // tk, NDEV * nb),
               pl.BlockSpec((tm, tc), lambda i, j, k: (i, k)),
               pl.BlockSpec((1, 1, tk, tc), lambda i, j, k: (k // nb, l, j, k % nb)),
               pl.BlockSpec((tm, tk), lambda i, j, k: (i, j)),
               NT, (tm, tc), (tk, tc), 2, ("parallel", "parallel", "arbitrary"))


def _mm_nt_out(a, w, l, name):
    M, N = a.shape
    _, _, kb, _ = w.shape
    K = NDEV * kb
    tm, tk, tc = min(M, 1024), _tile(K, 1024), _tile(N, 1024)
    per = tk // kb
    return _mm(name, a, w, jax.ShapeDtypeStruct((M, K), F32), (M // tm, K // tk, N // tc),
               pl.BlockSpec((tm, tc), lambda i, j, k: (i, k)),
               pl.BlockSpec((per, 1, kb, tc), lambda i, j, k: (j, l, 0, k)),
               pl.BlockSpec((tm, tk), lambda i, j, k: (i, j)),
               NT, (tm, tc), (tk, tc), 2, ("parallel", "parallel", "arbitrary"))


def _mm_tn_in(a, b, l, L, buf, name):
    S, K = a.shape
    n = b.shape[1] // NDEV
    ts, tk, tn = min(S, 1024), _tile(K, 1024), _tile(n, 1024)
    nb = n // tn
    return _mm(name, a, b, jax.ShapeDtypeStruct((NDEV, L, K, n), F32), (NDEV * nb, K // tk, S // ts),
               pl.BlockSpec((ts, tk), lambda j, i, s: (s, i)),
               pl.BlockSpec((ts, tn), lambda j, i, s: (s, j)),
               pl.BlockSpec((1, 1, tk, tn), lambda j, i, s: (j // nb, l, i, j % nb)),
               TN, (ts, tk), (ts, tn), 2, ("parallel", "parallel", "arbitrary"), alias=buf)


def _mm_tn_out(a, b, l, L, buf, name):
    S, K = a.shape
    N = b.shape[1]
    kb = K // NDEV
    ts, tk, tn = min(S, 1024), _tile(K, 1024), _tile(N, 1024)
    per = tk // kb
    return _mm(name, a, b, jax.ShapeDtypeStruct((NDEV, L, kb, N), F32), (N // tn, K // tk, S // ts),
               pl.BlockSpec((ts, tk), lambda j, i, s: (s, i)),
               pl.BlockSpec((ts, tn), lambda j, i, s: (s, j)),
               pl.BlockSpec((per, 1, kb, tn), lambda j, i, s: (i, l, 0, j)),
               TN, (ts, tk), (ts, tn), 2, ("parallel", "parallel", "arbitrary"), alias=buf)


def _attn_bias(T):
    reach = max(w // 2 for w, _ in DILATED_PATTERNS)
    hb = -(-reach // T)
    i = np.arange(T)[:, None]
    j = np.arange(T)[None, :]
    tiles = []
    for d in range(-hb, hb + 1):
        rel = j + d * T - i
        mult = np.zeros((T, T), np.float64)
        for window, dil in DILATED_PATTERNS:
            radius = window // (2 * dil)
            mult += (rel % dil == 0) & (np.abs(rel) <= radius * dil)
        tiles.append(np.where(mult > 0, np.log(np.maximum(mult, 1.0)), NEG_INF))
    return jnp.asarray(np.stack(tiles), F32)


def _rope_tables(S):
    half = HEAD_DIM // 2
    pos = jnp.arange(S, dtype=F32)
    inv = ROPE_THETA ** (-jnp.arange(half, dtype=F32) / half)
    ang = pos[:, None] * inv[None, :]
    cos, sin = jnp.cos(ang), jnp.sin(ang)
    return jnp.concatenate([cos, cos], axis=-1), jnp.concatenate([-sin, sin], axis=-1)


def _rope_apply(t, cosf, sinf, heads, sign):
    outs = []
    for hh in range(heads):
        th = t[:, hh * HEAD_DIM:(hh + 1) * HEAD_DIM]
        outs.append(th * cosf + sign * (pltpu.roll(th, HEAD_DIM // 2, 1) * sinf))
    return outs


def _rope_qkv(proj, cosf, sinf, W, name):
    S = proj.shape[0]
    tr = min(S, ROW_TILE)
    heads = W // HEAD_DIM

    def body(q_ref, k_ref, v_ref, c_ref, s_ref, qo_ref, ko_ref, vo_ref):
        cosf_v, sinf_v = c_ref[...], s_ref[...]
        for src, dst in ((q_ref, qo_ref), (k_ref, ko_ref)):
            for hh, val in enumerate(_rope_apply(src[...], cosf_v, sinf_v, heads, 1.0)):
                dst[:, hh * HEAD_DIM:(hh + 1) * HEAD_DIM] = val.astype(BF16)
        vo_ref[...] = v_ref[...].astype(BF16)

    piece = lambda p: pl.BlockSpec((tr, W), lambda i: (i, p))
    tab = pl.BlockSpec((tr, HEAD_DIM), lambda i: (i, 0))
    out = pl.BlockSpec((tr, W), lambda i: (i, 0))
    shp = jax.ShapeDtypeStruct((S, W), BF16)
    return pl.pallas_call(
        body, name=name, grid=(S // tr,),
        in_specs=[piece(0), piece(1), piece(2), tab, tab], out_specs=[out] * 3, out_shape=[shp] * 3,
        compiler_params=_cp("parallel"),
    )(proj, proj, proj, cosf, sinf)


def _attn_fwd(q, k, v, bias, name):
    S, W = q.shape
    H = W // HEAD_DIM
    nd, T, _ = bias.shape
    hb, nq = nd // 2, S // T
    scale = HEAD_DIM ** -0.5

    def body(q_ref, k_ref, v_ref, b_ref, o_ref, lse_ref, m_s, l_s, acc_s):
        i, d = pl.program_id(1), pl.program_id(2)
        j = i + d - hb

        @pl.when(d == 0)
        def _():
            m_s[...] = jnp.full(m_s.shape, -jnp.inf, F32)
            l_s[...] = jnp.zeros(l_s.shape, F32)
            acc_s[...] = jnp.zeros(acc_s.shape, F32)

        @pl.when((j >= 0) & (j < nq))
        def _():
            s = lax.dot_general(q_ref[...], k_ref[...], NT, preferred_element_type=F32) * scale + b_ref[d]
            m_old = m_s[...]
            m_new = jnp.maximum(m_old, jnp.max(s, axis=1, keepdims=True))
            p = jnp.exp(s - m_new)
            alpha = jnp.exp(m_old - m_new)
            l_s[...] = alpha * l_s[...] + jnp.sum(p, axis=1, keepdims=True)
            acc_s[...] = alpha * acc_s[...] + jnp.dot(p.astype(BF16), v_ref[...], preferred_element_type=F32)
            m_s[...] = m_new

        @pl.when(d == nd - 1)
        def _():
            o_ref[...] = acc_s[...] / l_s[...]
            lse_ref[0] = m_s[...] + jnp.log(l_s[...])

    kv = pl.BlockSpec((T, HEAD_DIM), lambda h, i, d: (jnp.clip(i + d - hb, 0, nq - 1), h))
    return pl.pallas_call(
        body, name=name, grid=(H, nq, nd),
        in_specs=[pl.BlockSpec((T, HEAD_DIM), lambda h, i, d: (i, h)), kv, kv,
                  pl.BlockSpec((nd, T, T), lambda h, i, d: (0, 0, 0))],
        out_specs=[pl.BlockSpec((T, HEAD_DIM), lambda h, i, d: (i, h)),
                   pl.BlockSpec((1, T, 1), lambda h, i, d: (h, i, 0))],
        out_shape=[jax.ShapeDtypeStruct((S, W), F32), jax.ShapeDtypeStruct((H, S, 1), F32)],
        scratch_shapes=[pltpu.VMEM((T, 1), F32), pltpu.VMEM((T, 1), F32), pltpu.VMEM((T, HEAD_DIM), F32)],
        compiler_params=_cp("parallel", "parallel", "arbitrary"),
    )(q, k, v, bias)


def _attn_bwd(q, k, v, do, lse, delta, bias, name):
    S, W = q.shape
    H = W // HEAD_DIM
    nd, T, _ = bias.shape
    hb, nq = nd // 2, S // T
    scale = HEAD_DIM ** -0.5

    def body(q_ref, do_ref, lse_ref, dl_ref, k_ref, v_ref, b_ref, dq_ref, dk_ref, dv_ref):
        j, d = pl.program_id(1), pl.program_id(2)
        i = j + d - hb

        @pl.when((j == 0) & (d == 0))
        def _():
            dq_ref[...] = jnp.zeros(dq_ref.shape, F32)

        @pl.when(d == 0)
        def _():
            dk_ref[...] = jnp.zeros(dk_ref.shape, F32)
            dv_ref[...] = jnp.zeros(dv_ref.shape, F32)

        @pl.when((i >= 0) & (i < nq))
        def _():
            qv, kv_, dov = q_ref[...], k_ref[...], do_ref[...]
            s = lax.dot_general(qv, kv_, NT, preferred_element_type=F32) * scale + b_ref[nd - 1 - d]
            p = jnp.exp(s - lse_ref[0])
            dp = lax.dot_general(dov, v_ref[...], NT, preferred_element_type=F32)
            ds = (p * (dp - dl_ref[0]) * scale).astype(BF16)
            dv_ref[...] += lax.dot_general(p.astype(BF16), dov, TN, preferred_element_type=F32)
            dk_ref[...] += lax.dot_general(ds, qv, TN, preferred_element_type=F32)
            rows = pl.ds(pl.multiple_of(i * T, T), T)
            dq_ref[rows, :] += jnp.dot(ds, kv_, preferred_element_type=F32)

    qi = lambda h, j, d: (jnp.clip(j + d - hb, 0, nq - 1), h)
    qs = pl.BlockSpec((T, HEAD_DIM), qi)
    col = pl.BlockSpec((1, T, 1), lambda h, j, d: (h, jnp.clip(j + d - hb, 0, nq - 1), 0))
    kv = pl.BlockSpec((T, HEAD_DIM), lambda h, j, d: (j, h))
    shp = jax.ShapeDtypeStruct((S, W), F32)
    return pl.pallas_call(
        body, name=name, grid=(H, nq, nd),
        in_specs=[qs, qs, col, col, kv, kv, pl.BlockSpec((nd, T, T), lambda h, j, d: (0, 0, 0))],
        out_specs=[pl.BlockSpec((S, HEAD_DIM), lambda h, j, d: (0, h)), kv, kv],
        out_shape=[shp, shp, shp],
        compiler_params=_cp("parallel", "arbitrary", "arbitrary"),
    )(q, do, lse, delta, k, v, bias)


def _halo_specs(S, tr, W, piece):
    per, last = tr // 8, S // 8 - 1
    prev = pl.BlockSpec((8, W), lambda i: (jnp.maximum(i * per - 1, 0), piece))
    nxt = pl.BlockSpec((8, W), lambda i: (jnp.minimum((i + 1) * per, last), piece))
    return prev, nxt


def _shifted(t, before, after, tr):
    rows = lax.broadcasted_iota(jnp.int32, (tr, 1), 0)
    prev = jnp.where(rows == 0, before, pltpu.roll(t, 1, 0))
    nxt = jnp.where(rows == tr - 1, after, pltpu.roll(t, tr - 1, 0))
    return prev, nxt


def _ab_mix(attn, proj, conv_w, W, name):
    S = attn.shape[0]
    tr = min(S, ROW_TILE)
    nsteps = S // tr

    def body(a_ref, za_ref, ub_ref, gb_ref, gc_ref, zb_ref, ubp, ubn, gcp, gcn, w_ref, y_ref):
        i = pl.program_id(0)
        t = gc_ref[...] * ub_ref[...]
        before = jnp.where(i == 0, 0.0, (gcp[...] * ubp[...])[7:8, :])
        after = jnp.where(i == nsteps - 1, 0.0, (gcn[...] * ubn[...])[0:1, :])
        t_prev, t_next = _shifted(t, before, after, tr)
        w = w_ref[...]
        cv = w[0:1, :] * t_prev + w[1:2, :] * t + w[2:3, :] * t_next
        silu_a, _ = _silu_and_grad(za_ref[...])
        silu_b, _ = _silu_and_grad(zb_ref[...])
        y_ref[:, :W] = (a_ref[...] * silu_a).astype(BF16)
        y_ref[:, W:] = (gb_ref[...] * cv * silu_b).astype(BF16)

    piece = lambda p: pl.BlockSpec((tr, W), lambda i: (i, p))
    ubp, ubn = _halo_specs(S, tr, W, 4)
    gcp, gcn = _halo_specs(S, tr, W, 6)
    return pl.pallas_call(
        body, name=name, grid=(nsteps,),
        in_specs=[pl.BlockSpec((tr, W), lambda i: (i, 0)), piece(3), piece(4), piece(5), piece(6), piece(7),
                  ubp, ubn, gcp, gcn, pl.BlockSpec((3, W), lambda i: (0, 0))],
        out_specs=pl.BlockSpec((tr, 2 * W), lambda i: (i, 0)),
        out_shape=jax.ShapeDtypeStruct((S, 2 * W), BF16),
        compiler_params=_cp("parallel"),
    )(attn, proj, proj, proj, proj, proj, proj, proj, proj, proj, conv_w)


def _dattn_prep(dy, proj, attn, W, name):
    S = attn.shape[0]
    H = W // HEAD_DIM
    tr = min(S, ROW_TILE)

    def body(dy_ref, za_ref, a_ref, do_ref, dl_ref):
        silu_a, _ = _silu_and_grad(za_ref[...])
        do = dy_ref[...] * silu_a
        do_ref[...] = do.astype(BF16)
        prod = do * a_ref[...]
        for hh in range(H):
            dl_ref[hh] = jnp.sum(prod[:, hh * HEAD_DIM:(hh + 1) * HEAD_DIM], axis=1, keepdims=True)

    row = pl.BlockSpec((tr, W), lambda i: (i, 0))
    return pl.pallas_call(
        body, name=name, grid=(S // tr,),
        in_specs=[row, pl.BlockSpec((tr, W), lambda i: (i, 3)), row],
        out_specs=[row, pl.BlockSpec((H, tr, 1), lambda i: (0, i, 0))],
        out_shape=[jax.ShapeDtypeStruct((S, W), BF16), jax.ShapeDtypeStruct((H, S, 1), F32)],
        compiler_params=_cp("parallel"),
    )(dy, proj, attn)


def _ab_bwd(dy, attn, proj, dqr, dkr, dv, cosf, sinf, conv_w, W, name):
    S = attn.shape[0]
    tr = min(S, ROW_TILE // 2)
    nsteps = S // tr
    heads = W // HEAD_DIM

    def body(dya_ref, dyb_ref, a_ref, za_ref, ub_ref, gb_ref, gc_ref, zb_ref, dq_ref, dk_ref, dv_ref,
             c_ref, s_ref, w_ref, dybp, dybn, gbp, gbn, zbp, zbn, ubp, ubn, gcp, gcn,
             dp_ref, dw_ref, acc):
        i = pl.program_id(0)
        first, last = i == 0, i == nsteps - 1
        w = w_ref[...]
        w0, w1, w2 = w[0:1, :], w[1:2, :], w[2:3, :]
        ub, gb, gc, zb = ub_ref[...], gb_ref[...], gc_ref[...], zb_ref[...]
        dyb = dyb_ref[...]
        silu_a, dsilu_a = _silu_and_grad(za_ref[...])
        silu_b, dsilu_b = _silu_and_grad(zb)
        t = gc * ub
        t_prev, t_next = _shifted(t, jnp.where(first, 0.0, (gcp[...] * ubp[...])[7:8, :]),
                                  jnp.where(last, 0.0, (gcn[...] * ubn[...])[0:1, :]), tr)
        cv = w0 * t_prev + w1 * t + w2 * t_next
        dcv = dyb * gb * silu_b
        halo_p = dybp[...] * gbp[...] * _silu_and_grad(zbp[...])[0]
        halo_n = dybn[...] * gbn[...] * _silu_and_grad(zbn[...])[0]
        dcv_prev, dcv_next = _shifted(dcv, jnp.where(first, 0.0, halo_p[7:8, :]),
                                      jnp.where(last, 0.0, halo_n[0:1, :]), tr)
        dt = w0 * dcv_next + w1 * dcv + w2 * dcv_prev
        cosf_v, sinf_v = c_ref[...], s_ref[...]
        for src, base in ((dq_ref, 0), (dk_ref, W)):
            for hh, val in enumerate(_rope_apply(src[...], cosf_v, sinf_v, heads, -1.0)):
                dp_ref[:, base + hh * HEAD_DIM:base + (hh + 1) * HEAD_DIM] = val.astype(BF16)
        dp_ref[:, 2 * W:3 * W] = dv_ref[...].astype(BF16)
        dp_ref[:, 3 * W:4 * W] = (dya_ref[...] * a_ref[...] * dsilu_a).astype(BF16)
        dp_ref[:, 4 * W:5 * W] = (dt * gc).astype(BF16)
        dp_ref[:, 5 * W:6 * W] = (dyb * cv * silu_b).astype(BF16)
        dp_ref[:, 6 * W:7 * W] = (dt * ub).astype(BF16)
        dp_ref[:, 7 * W:8 * W] = (dyb * gb * cv * dsilu_b).astype(BF16)
        tap = lax.broadcasted_iota(jnp.int32, (8, 1), 0)
        part = (jnp.where(tap == 0, jnp.sum(dcv * t_prev, axis=0, keepdims=True), 0.0)
                + jnp.where(tap == 1, jnp.sum(dcv * t, axis=0, keepdims=True), 0.0)
                + jnp.where(tap == 2, jnp.sum(dcv * t_next, axis=0, keepdims=True), 0.0))

        @pl.when(first)
        def _():
            acc[...] = part

        @pl.when(i > 0)
        def _():
            acc[...] += part

        @pl.when(last)
        def _():
            dw_ref[...] = acc[...]

    row = pl.BlockSpec((tr, W), lambda i: (i, 0))
    piece = lambda p: pl.BlockSpec((tr, W), lambda i: (i, p))
    tab = pl.BlockSpec((tr, HEAD_DIM), lambda i: (i, 0))
    dybp, dybn = _halo_specs(S, tr, W, 1)
    gbp, gbn = _halo_specs(S, tr, W, 5)
    zbp, zbn = _halo_specs(S, tr, W, 7)
    ubp, ubn = _halo_specs(S, tr, W, 4)
    gcp, gcn = _halo_specs(S, tr, W, 6)
    return pl.pallas_call(
        body, name=name, grid=(nsteps,),
        in_specs=[piece(0), piece(1), row, piece(3), piece(4), piece(5), piece(6), piece(7), row, row, row,
                  tab, tab, pl.BlockSpec((3, W), lambda i: (0, 0)),
                  dybp, dybn, gbp, gbn, zbp, zbn, ubp, ubn, gcp, gcn],
        out_specs=[pl.BlockSpec((tr, 8 * W), lambda i: (i, 0)), pl.BlockSpec((8, W), lambda i: (0, 0))],
        out_shape=[jax.ShapeDtypeStruct((S, 8 * W), BF16), jax.ShapeDtypeStruct((8, W), F32)],
        scratch_shapes=[pltpu.VMEM((8, W), F32)],
        compiler_params=_cp("arbitrary"),
    )(dy, dy, attn, proj, proj, proj, proj, proj, dqr, dkr, dv, cosf, sinf, conv_w,
      dy, dy, proj, proj, proj, proj, proj, proj, proj, proj)


def _sgu_core(p_ref, lng_ref, lnb_ref, ws_ref, bst_ref, Dc):
    gw = Dc // C_GROUPS
    u_raw, v_raw, z = p_ref[:, :Dc], p_ref[:, Dc:2 * Dc], p_ref[:, 2 * Dc:]
    u, du = _gelu_and_grad(u_raw)
    vg, dvg = _gelu_and_grad(v_raw)
    mu = jnp.mean(vg, axis=-1, keepdims=True)
    vc = vg - mu
    rstd = lax.rsqrt(jnp.mean(vc * vc, axis=-1, keepdims=True) + EPS)
    vhat = vc * rstd
    vn = (vhat * lng_ref[...] + lnb_ref[...]).astype(BF16)
    bst = bst_ref[...]
    mixed = jnp.concatenate(
        [jnp.dot(ws_ref[g].astype(BF16), vn[:, g * gw:(g + 1) * gw], preferred_element_type=F32)
         + bst[:, g:g + 1] for g in range(C_GROUPS)], axis=1)
    sz, dsz = _silu_and_grad(z)
    return u, du, dvg, rstd, vhat, vn, mixed, sz, dsz


def _sgu_fwd(proj, ln_g, ln_b, w_s, b_st, name):
    S, Dc3 = proj.shape
    Dc = Dc3 // 3
    vec = pl.BlockSpec((1, Dc), lambda i: (0, 0))

    def body(p_ref, lng_ref, lnb_ref, ws_ref, bst_ref, y_ref):
        u, _, _, _, _, _, mixed, sz, _ = _sgu_core(p_ref, lng_ref, lnb_ref, ws_ref, bst_ref, Dc)
        y_ref[...] = (u * mixed * sz).astype(BF16)

    return pl.pallas_call(
        body, name=name, grid=(S // C_CHUNK,),
        in_specs=[pl.BlockSpec((C_CHUNK, Dc3), lambda i: (i, 0)), vec, vec,
                  pl.BlockSpec((C_GROUPS, C_CHUNK, C_CHUNK), lambda i: (0, 0, 0)),
                  pl.BlockSpec((C_CHUNK, C_GROUPS), lambda i: (0, 0))],
        out_specs=pl.BlockSpec((C_CHUNK, Dc), lambda i: (i, 0)),
        out_shape=jax.ShapeDtypeStruct((S, Dc), BF16),
        compiler_params=_cp("parallel"),
    )(proj, ln_g, ln_b, w_s, b_st)


def _sgu_bwd(proj, dy, ln_g, ln_b, w_s, w_st, b_st, name):
    S, Dc3 = proj.shape
    Dc = Dc3 // 3
    gw = Dc // C_GROUPS
    nsteps = S // C_CHUNK
    vec = pl.BlockSpec((1, Dc), lambda i: (0, 0))
    wspec = pl.BlockSpec((C_GROUPS, C_CHUNK, C_CHUNK), lambda i: (0, 0, 0))

    def body(p_ref, dy_ref, lng_ref, lnb_ref, ws_ref, wst_ref, bst_ref,
             dp_ref, dws_ref, dbs_ref, dlg_ref, dlb_ref, acc_w, acc_b, acc_g, acc_lb):
        i = pl.program_id(0)
        u, du, dvg, rstd, vhat, vn, mixed, sz, dsz = _sgu_core(p_ref, lng_ref, lnb_ref, ws_ref, bst_ref, Dc)
        dy = dy_ref[...]
        dmixed = dy * u * sz
        dmb = dmixed.astype(BF16)

        @pl.when(i == 0)
        def _():
            acc_w[...] = jnp.zeros(acc_w.shape, F32)
            acc_b[...] = jnp.zeros(acc_b.shape, F32)
            acc_g[...] = jnp.zeros(acc_g.shape, F32)
            acc_lb[...] = jnp.zeros(acc_lb.shape, F32)

        dvn_parts = []
        for g in range(C_GROUPS):
            dmg = dmb[:, g * gw:(g + 1) * gw]
            acc_w[g] += lax.dot_general(dmg, vn[:, g * gw:(g + 1) * gw], NT, preferred_element_type=F32)
            acc_b[g] += dmixed[:, g * gw:(g + 1) * gw]
            dvn_parts.append(jnp.dot(wst_ref[g].astype(BF16), dmg, preferred_element_type=F32))
        dvn = jnp.concatenate(dvn_parts, axis=1)
        acc_g[...] += jnp.sum((dvn * vhat).reshape(C_CHUNK // 8, 8, Dc), axis=0)
        acc_lb[...] += jnp.sum(dvn.reshape(C_CHUNK // 8, 8, Dc), axis=0)
        dvh = dvn * lng_ref[...]
        dvgelu = rstd * (dvh - jnp.mean(dvh, axis=-1, keepdims=True)
                         - vhat * jnp.mean(dvh * vhat, axis=-1, keepdims=True))
        dp_ref[:, :Dc] = (dy * mixed * sz * du).astype(BF16)
        dp_ref[:, Dc:2 * Dc] = (dvgelu * dvg).astype(BF16)
        dp_ref[:, 2 * Dc:] = (dy * u * mixed * dsz).astype(BF16)

        @pl.when(i == nsteps - 1)
        def _():
            dws_ref[...] = acc_w[...]
            for g in range(C_GROUPS):
                dbs_ref[g] = jnp.sum(acc_b[g], axis=1, keepdims=True)
            dlg_ref[...] = jnp.sum(acc_g[...], axis=0, keepdims=True)
            dlb_ref[...] = jnp.sum(acc_lb[...], axis=0, keepdims=True)

    v = jax.ShapeDtypeStruct((1, Dc), F32)
    return pl.pallas_call(
        body, name=name, grid=(nsteps,),
        in_specs=[pl.BlockSpec((C_CHUNK, Dc3), lambda i: (i, 0)), pl.BlockSpec((C_CHUNK, Dc), lambda i: (i, 0)),
                  vec, vec, wspec, wspec, pl.BlockSpec((C_CHUNK, C_GROUPS), lambda i: (0, 0))],
        out_specs=[pl.BlockSpec((C_CHUNK, Dc3), lambda i: (i, 0)), wspec,
                   pl.BlockSpec((C_GROUPS, C_CHUNK, 1), lambda i: (0, 0, 0)), vec, vec],
        out_shape=[jax.ShapeDtypeStruct((S, Dc3), BF16),
                   jax.ShapeDtypeStruct((C_GROUPS, C_CHUNK, C_CHUNK), F32),
                   jax.ShapeDtypeStruct((C_GROUPS, C_CHUNK, 1), F32), v, v],
        scratch_shapes=[pltpu.VMEM((C_GROUPS, C_CHUNK, C_CHUNK), F32), pltpu.VMEM((C_GROUPS, C_CHUNK, gw), F32),
                        pltpu.VMEM((8, Dc), F32), pltpu.VMEM((8, Dc), F32)],
        compiler_params=_cp("arbitrary"),
    )(proj, dy, ln_g, ln_b, w_s, w_st, b_st)


def _pack(vectors, multiple):
    flat = jnp.concatenate([v.reshape(-1) for v in vectors])
    pad = (-flat.shape[0]) % multiple
    return jnp.pad(flat, (0, pad)).reshape(-1, 128)


def _unshard(g, off, shape):
    L, rest = shape[0], shape[1:]
    size = int(np.prod(shape))
    piece = g[:, off:off + size].reshape((NDEV,) + tuple(shape))
    nd = piece.ndim
    perm = tuple(range(1, nd - 1)) + (0, nd - 1)
    full = jnp.transpose(piece, perm)
    return full.reshape(tuple(shape[:-1]) + (NDEV * shape[-1],)), off + size


def kernel(x, c, ab_norm_g, ab_w_mod, ab_b_mod, ab_w_in, ab_conv_w, ab_w_out, sg_norm_g, sg_w_mod, sg_b_mod, sg_w_in, sg_ln_g, sg_ln_b, sg_w_s, sg_b_s, sg_w_out, final_norm_g, loss_target, m_ab_norm_g, m_ab_w_mod, m_ab_b_mod, m_ab_w_in, m_ab_conv_w, m_ab_w_out, m_sg_norm_g, m_sg_w_mod, m_sg_b_mod, m_sg_w_in, m_sg_ln_g, m_sg_ln_b, m_sg_w_s, m_sg_b_s, m_sg_w_out, m_final_norm_g, v_ab_norm_g, v_ab_w_mod, v_ab_b_mod, v_ab_w_in, v_ab_conv_w, v_ab_w_out, v_sg_norm_g, v_sg_w_mod, v_sg_b_mod, v_sg_w_in, v_sg_ln_g, v_sg_ln_b, v_sg_w_s, v_sg_b_s, v_sg_w_out, v_final_norm_g):
    _, S, D = x.shape
    L = ab_norm_g.shape[0]
    W = ab_conv_w.shape[2] * NDEV
    n_ab, n_sg = ab_w_in.shape[2], sg_w_in.shape[2]
    n_mod = ab_w_mod.shape[2]
    kb = ab_w_out.shape[1]
    xi, yi, ci = _position()
    dev = 4 * xi + 2 * yi + ci
    x2, tgt = x.reshape(S, D), loss_target.reshape(S, D)

    small = [c, ab_conv_w, sg_norm_g, sg_ln_g, sg_ln_b]
    (g1,) = _all_gather([_pack(small, 1024)], "ag_small")
    g1 = g1.reshape(NDEV, -1)
    c_all = g1[:, :D]
    off = D
    conv_full, off = _unshard(g1, off, ab_conv_w.shape)
    sg_norm_full, off = _unshard(g1, off, sg_norm_g.shape)
    ln_g_full, off = _unshard(g1, off, sg_ln_g.shape)
    ln_b_full, off = _unshard(g1, off, sg_ln_b.shape)

    ab_b_cols = lax.dynamic_slice_in_dim(ab_b_mod, dev * n_mod, n_mod, axis=1)
    m_ab = _mod_fwd(c_all, ab_w_mod, ab_b_cols.reshape(L, 1, n_mod), "mod_fwd_ab")
    m_sg = _mod_fwd(c_all, sg_w_mod, sg_b_mod.reshape(L, 1, n_mod), "mod_fwd_sg")
    m_part = jnp.stack([m_ab, m_sg]).transpose(2, 0, 1, 3).reshape(NDEV, 2 * L * n_mod)
    (g2,) = _all_gather([m_part], "ag_mod")
    mine = lax.dynamic_index_in_dim(g2, dev, axis=1, keepdims=False)
    mods = mine.reshape(NDEV, 2, L, n_mod).transpose(1, 2, 0, 3).reshape(2, L, 3 * D)

    def mod_of(kind, i):
        m = mods[kind, i]
        return m[:D].reshape(1, D), m[D:2 * D].reshape(1, D), m[2 * D:].reshape(1, D)

    gathered = _all_gather(
        [ab_w_in.astype(BF16).reshape(L * D, n_ab), ab_w_out.astype(BF16).reshape(L * kb, D),
         sg_w_in.astype(BF16).reshape(L * D, n_sg), sg_w_out.astype(BF16).reshape(L * kb, D)], "ag_weights")
    w_in = [gathered[0].reshape(NDEV, L, D, n_ab), gathered[2].reshape(NDEV, L, D, n_sg)]
    w_out = [gathered[1].reshape(NDEV, L, kb, D), gathered[3].reshape(NDEV, L, kb, D)]

    cosf, sinf = _rope_tables(S)
    T = min(S, ATTN_TILE)
    bias = _attn_bias(T)
    norm_g = [ab_norm_g, sg_norm_full]
    w_s_t = jnp.swapaxes(sg_w_s, -1, -2)
    b_s_t = jnp.swapaxes(sg_b_s, -1, -2)

    saved = []
    x_cur, res, gate_prev = x2, None, None
    for layer in range(2 * L):
        kind, i = layer % 2, layer // 2
        tag = f"{'ab' if kind == 0 else 'sg'}{i}"
        shift, scale, gate = mod_of(kind, i)
        g = norm_g[kind][i].reshape(1, D)
        xl, h = _pre(x_cur, res, gate_prev, g, scale, shift, f"pre_{tag}")
        proj = _mm_nn_in(h, w_in[kind], i, f"proj_{tag}")
        rec = dict(xl=xl, h=h, proj=proj, g=g, scale=scale, gate=gate)
        if kind == 0:
            qr, kr, vb = _rope_qkv(proj, cosf, sinf, W, f"rope_{tag}")
            attn, lse = _attn_fwd(qr, kr, vb, bias, f"attn_{tag}")
            y = _ab_mix(attn, proj, conv_full[i], W, f"mix_{tag}")
            rec.update(qr=qr, kr=kr, vb=vb, attn=attn, lse=lse)
        else:
            y = _sgu_fwd(proj, ln_g_full[i].reshape(1, D), ln_b_full[i].reshape(1, D), sg_w_s[i], b_s_t[i],
                         f"sgu_{tag}")
        out = _mm_nn_out(y, w_out[kind], i, f"out_{tag}")
        rec.update(y=y, out=out)
        saved.append(rec)
        x_cur, res, gate_prev = xl, out, gate

    dx, loss_part, d_final_g = _loss_head(x_cur, res, gate_prev, final_norm_g.reshape(1, D), tgt, "loss_head")
    loss = lax.psum(loss_part[0, 0], ("x", "y", "c"))

    dw_in, dw_out = [None, None], [None, None]
    dm = [[None] * L, [None] * L]
    d_norm = [[None] * L, [None] * L]
    d_conv, d_lng, d_lnb, d_ws, d_bs = [None] * L, [None] * L, [None] * L, [None] * L, [None] * L
    for layer in reversed(range(2 * L)):
        kind, i = layer % 2, layer // 2
        tag = f"{'ab' if kind == 0 else 'sg'}{i}"
        rec = saved[layer]
        dout, dgate = _post_bwd(dx, rec["out"], rec["gate"], f"post_bwd_{tag}")
        dy = _mm_nt_out(dout, w_out[kind], i, f"dy_{tag}")
        dw_out[kind] = _mm_tn_out(rec["y"], dout, i, L, dw_out[kind], f"dwout_{tag}")
        if kind == 0:
            do, delta = _dattn_prep(dy, rec["proj"], rec["attn"], W, f"dattn_{tag}")
            dqr, dkr, dvv = _attn_bwd(rec["qr"], rec["kr"], rec["vb"], do, rec["lse"], delta, bias,
                                      f"attn_bwd_{tag}")
            dproj, dcw = _ab_bwd(dy, rec["attn"], rec["proj"], dqr, dkr, dvv, cosf, sinf, conv_full[i], W,
                                 f"mix_bwd_{tag}")
            d_conv[i] = dcw[:3]
        else:
            dproj, d_ws[i], dbs, d_lng[i], d_lnb[i] = _sgu_bwd(
                rec["proj"], dy, ln_g_full[i].reshape(1, D), ln_b_full[i].reshape(1, D),
                sg_w_s[i], w_s_t[i], b_s_t[i], f"sgu_bwd_{tag}")
            d_bs[i] = dbs.reshape(C_GROUPS, C_CHUNK)
        dh = _mm_nt_in(dproj, w_in[kind], i, f"dh_{tag}")
        dw_in[kind] = _mm_tn_in(rec["h"], dproj, i, L, dw_in[kind], f"dwin_{tag}")
        dx, dshift, dscale, d_norm[kind][i] = _pre_bwd(rec["xl"], dh, dx, rec["g"], rec["scale"],
                                                       f"pre_bwd_{tag}")
        dm[kind][i] = jnp.concatenate([dshift, dscale, dgate], axis=1).reshape(3 * D)
    grad_x = dx.reshape(1, S, D)

    big = [dw_in[0].reshape(NDEV, L * D, n_ab), dw_out[0].reshape(NDEV, L * kb, D),
           dw_in[1].reshape(NDEV, L * D, n_sg), dw_out[1].reshape(NDEV, L * kb, D)]
    from_sibling = _rs_sibling(big, "rs_sibling")
    c_idx = ci.reshape(1).astype(jnp.int32)
    names = ["ab_w_in", "ab_w_out", "sg_w_in", "sg_w_out"]
    chip_part = [_add_sibling(g, r, c_idx, f"rs_add_{nm}") for g, r, nm in zip(big, from_sibling, names)]
    from_chips = _rs_chips(chip_part, "rs_chips")
    big_w = [(ab_w_in, m_ab_w_in, v_ab_w_in), (ab_w_out, m_ab_w_out, v_ab_w_out),
             (sg_w_in, m_sg_w_in, v_sg_w_in), (sg_w_out, m_sg_w_out, v_sg_w_out)]
    big_res = {}
    for nm, parts, (w, m, v) in zip(names, from_chips, big_w):
        shp = w.shape
        flat = lambda a: a.reshape(shp[0] * shp[1], shp[2])
        outs = _sum_adam(parts, flat(w), flat(m), flat(v), f"adam_{nm}")
        big_res[nm] = [o.reshape(shp) for o in outs]

    stack = lambda xs: jnp.stack(xs)
    pack_items = [stack(dm[0]), stack(dm[1]), stack(d_norm[0]).reshape(L, D), stack(d_conv),
                  stack(d_norm[1]).reshape(L, D), stack(d_lng).reshape(L, D), stack(d_lnb).reshape(L, D),
                  stack(d_ws), stack(d_bs), d_final_g]
    (g3,) = _all_gather([_pack(pack_items, 1024)], "ag_grads")
    P = g3.shape[1] * g3.shape[2]
    tot = _sum_rows(g3, "sum_small").reshape(P)
    g3 = g3.reshape(NDEV, P)
    sizes = [int(np.prod(p.shape)) for p in pack_items]
    offs = np.concatenate([[0], np.cumsum(sizes)]).tolist()
    seg = lambda k, shape: tot[offs[k]:offs[k + 1]].reshape(shape)

    def shard(full, n):
        return lax.dynamic_slice_in_dim(full, dev * n, n, axis=full.ndim - 1)

    g_ab_b_mod = seg(0, (L, 3 * D))
    g_sg_b_mod = shard(seg(1, (L, 3 * D)), n_mod)
    g_ab_norm = seg(2, (L, D))
    g_conv = shard(seg(3, (L, 3, W)), W // NDEV)
    g_sg_norm = shard(seg(4, (L, D)), kb)
    g_ln_g = shard(seg(5, (L, D)), kb)
    g_ln_b = shard(seg(6, (L, D)), kb)
    g_w_s = seg(7, sg_w_s.shape)
    g_b_s = seg(8, sg_b_s.shape)
    g_final = seg(9, (D,))

    small_w = [("ab_norm_g", g_ab_norm, ab_norm_g, m_ab_norm_g, v_ab_norm_g),
               ("ab_b_mod", g_ab_b_mod, ab_b_mod, m_ab_b_mod, v_ab_b_mod),
               ("ab_conv_w", g_conv, ab_conv_w, m_ab_conv_w, v_ab_conv_w),
               ("sg_norm_g", g_sg_norm, sg_norm_g, m_sg_norm_g, v_sg_norm_g),
               ("sg_b_mod", g_sg_b_mod, sg_b_mod, m_sg_b_mod, v_sg_b_mod),
               ("sg_ln_g", g_ln_g, sg_ln_g, m_sg_ln_g, v_sg_ln_g),
               ("sg_ln_b", g_ln_b, sg_ln_b, m_sg_ln_b, v_sg_ln_b),
               ("sg_w_s", g_w_s, sg_w_s, m_sg_w_s, v_sg_w_s),
               ("sg_b_s", g_b_s, sg_b_s, m_sg_b_s, v_sg_b_s),
               ("final_norm_g", g_final, final_norm_g, m_final_norm_g, v_final_norm_g)]
    packed = [_pack([t[k] for t in small_w], 1024) for k in (1, 2, 3, 4)]
    upd = _adam_only(*packed, "adam_small")
    small_res = {}
    o = 0
    for nm, g, w, _, _ in small_w:
        size = int(np.prod(w.shape))
        small_res[nm] = [g] + [u.reshape(-1)[o:o + size].reshape(w.shape) for u in upd]
        o += size

    KP = 128
    sc_t = jnp.pad((c_all * jax.nn.sigmoid(c_all)).T, ((0, 0), (0, KP - NDEV)))
    mod_res = {}
    for kind, nm, (w, m, v) in ((0, "ab_w_mod", (ab_w_mod, m_ab_w_mod, v_ab_w_mod)),
                                (1, "sg_w_mod", (sg_w_mod, m_sg_w_mod, v_sg_w_mod))):
        dm_all = g3[:, offs[kind]:offs[kind + 1]].reshape(NDEV, L, 3 * D)
        cols = jnp.pad(shard(dm_all, n_mod).transpose(1, 0, 2), ((0, 0), (0, KP - NDEV), (0, 0)))
        mod_res[nm] = _wmod_grad_adam(sc_t, cols, w, m, v, f"adam_{nm}")

    order = ["ab_norm_g", "ab_w_mod", "ab_b_mod", "ab_w_in", "ab_conv_w", "ab_w_out", "sg_norm_g", "sg_w_mod",
             "sg_b_mod", "sg_w_in", "sg_ln_g", "sg_ln_b", "sg_w_s", "sg_b_s", "sg_w_out", "final_norm_g"]
    res = {**big_res, **small_res, **mod_res}
    outs = [loss, grad_x]
    for k in range(4):
        outs += [res[nm][k] for nm in order]
    return tuple(outs)
```

```python
import functools
import math

import numpy as np
import jax
import jax.numpy as jnp
from jax import lax
from jax.experimental import pallas as pl
from jax.experimental.pallas import tpu as pltpu

F32 = jnp.float32
BF16 = jnp.bfloat16

NDEV = 8
NCHIP = 4
EPS = 1e-6
HEAD_DIM = 128
ROPE_THETA = 10000.0
DILATED_PATTERNS = ((128, 1), (512, 4), (2048, 16))
NEG_INF = -1e30
C_CHUNK = 128
C_GROUPS = 8
ADAM_LR = 0.001
ADAM_B1 = 0.9
ADAM_B2 = 0.999
ADAM_EPS = 1e-08
ADAM_WD = 0.01
ADAM_STEP = 10
GELU_K = math.sqrt(2.0 / math.pi)
GELU_C = 0.044715

VMEM_LIMIT_BYTES = 56 * 1024 * 1024
ATTN_TILE = 512
ROW_TILE = 256
MESH = pl.DeviceIdType.MESH
ANY = pl.BlockSpec(memory_space=pl.ANY)


def _cp(*sem):
    return pltpu.CompilerParams(dimension_semantics=sem, vmem_limit_bytes=VMEM_LIMIT_BYTES)


def _sigmoid(z):
    return 1.0 / (1.0 + jnp.exp(-z))


def _silu_and_grad(z):
    s = _sigmoid(z)
    return z * s, s * (1.0 + z * (1.0 - s))


def _gelu_and_grad(x):
    x2 = x * x
    t = jnp.tanh(GELU_K * (x + GELU_C * x2 * x))
    g = 0.5 * x * (1.0 + t)
    dg = 0.5 * (1.0 + t) + 0.5 * x * (1.0 - t * t) * (GELU_K * (1.0 + 3.0 * GELU_C * x2))
    return g, dg


def _position():
    return lax.axis_index("x"), lax.axis_index("y"), lax.axis_index("c")


def _chips(x, y):
    return [(1 - x, y), (x, 1 - y), (1 - x, 1 - y)]


class _Gather:
    def __init__(self, arrs):
        n = len(arrs)
        self.arrs = list(arrs)
        self.out_shape = [jax.ShapeDtypeStruct((NDEV,) + a.shape, a.dtype) for a in arrs]
        self.scratch = [pltpu.SemaphoreType.DMA((n, 7)), pltpu.SemaphoreType.DMA((n, 7)),
                        pltpu.SemaphoreType.DMA((n,))]

    def _copies(self, ins, outs, sems):
        send_sems, recv_sems, local_sems = sems
        x, y, c = _position()

        def copy(a, k, block, to, src=None):
            dst = outs[a].at[4 * block[0] + 2 * block[1] + block[2]]
            return pltpu.make_async_remote_copy(
                src_ref=dst if src is None else src, dst_ref=dst,
                send_sem=send_sems.at[a, k], recv_sem=recv_sems.at[a, k],
                device_id=to, device_id_type=MESH)

        n = len(ins)
        me, sibling = (x, y, c), (x, y, 1 - c)
        mine = [pltpu.make_async_copy(ins[a], outs[a].at[4 * x + 2 * y + c], local_sems.at[a]) for a in range(n)]
        first = []
        for a in range(n):
            first.append(copy(a, 0, me, sibling, src=ins[a]))
            first += [copy(a, 1 + j, me, (*chip, c), src=ins[a]) for j, chip in enumerate(_chips(x, y))]
        return copy, mine, first

    def start(self, ins, outs, sems):
        _, mine, first = self._copies(ins, outs, sems)
        for cp in mine + first:
            cp.start()

    def finish(self, ins, outs, sems):
        copy, mine, first = self._copies(ins, outs, sems)
        x, y, c = _position()
        me, sibling = (x, y, c), (x, y, 1 - c)
        passed = []
        for j, chip in enumerate(_chips(x, y)):
            for a in range(len(ins)):
                copy(a, 1 + j, (*chip, c), me).wait_recv()
                fwd = copy(a, 4 + j, (*chip, c), sibling)
                fwd.start()
                passed.append(fwd)
        for a in range(len(ins)):
            copy(a, 0, sibling, me).wait_recv()
            for j, chip in enumerate(_chips(x, y)):
                copy(a, 4 + j, (*chip, 1 - c), me).wait_recv()
        for cp in first + passed:
            cp.wait_send()
        for cp in mine:
            cp.wait()


class _ToSibling:
    def __init__(self, gs):
        n = len(gs)
        self.arrs = list(gs)
        self.out_shape = [jax.ShapeDtypeStruct((NCHIP,) + g.shape[1:], g.dtype) for g in gs]
        self.scratch = [pltpu.SemaphoreType.DMA((n, NCHIP)), pltpu.SemaphoreType.DMA((n, NCHIP))]

    def _copies(self, ins, outs, sems):
        send_sems, recv_sems = sems
        x, y, c = _position()
        return [pltpu.make_async_remote_copy(
            src_ref=ins[a].at[2 * k + (1 - c)], dst_ref=outs[a].at[k],
            send_sem=send_sems.at[a, k], recv_sem=recv_sems.at[a, k],
            device_id=(x, y, 1 - c), device_id_type=MESH) for a in range(len(ins)) for k in range(NCHIP)]

    def start(self, ins, outs, sems):
        for cp in self._copies(ins, outs, sems):
            cp.start()

    def finish(self, ins, outs, sems):
        copies = self._copies(ins, outs, sems)
        for cp in copies:
            cp.wait_recv()
        for cp in copies:
            cp.wait_send()


class _ToChips:
    def __init__(self, ps):
        n = len(ps)
        self.arrs = list(ps)
        self.out_shape = [jax.ShapeDtypeStruct(p.shape, p.dtype) for p in ps]
        self.scratch = [pltpu.SemaphoreType.DMA((n, 3)), pltpu.SemaphoreType.DMA((n, 3)),
                        pltpu.SemaphoreType.DMA((n,))]

    def _copies(self, ins, outs, sems):
        send_sems, recv_sems, local_sems = sems
        x, y, c = _position()
        mychip = 2 * x + y
        n = len(ins)
        mine = [pltpu.make_async_copy(ins[a].at[mychip], outs[a].at[mychip], local_sems.at[a]) for a in range(n)]
        sends, recvs = [], []
        for a in range(n):
            for j, chip in enumerate(_chips(x, y)):
                sends.append(pltpu.make_async_remote_copy(
                    src_ref=ins[a].at[2 * chip[0] + chip[1]], dst_ref=outs[a].at[mychip],
                    send_sem=send_sems.at[a, j], recv_sem=recv_sems.at[a, j],
                    device_id=(*chip, c), device_id_type=MESH))
                slot = outs[a].at[2 * chip[0] + chip[1]]
                recvs.append(pltpu.make_async_remote_copy(
                    src_ref=slot, dst_ref=slot, send_sem=send_sems.at[a, j], recv_sem=recv_sems.at[a, j],
                    device_id=(*chip, c), device_id_type=MESH))
        return mine, sends, recvs

    def start(self, ins, outs, sems):
        mine, sends, _ = self._copies(ins, outs, sems)
        for cp in mine + sends:
            cp.start()

    def finish(self, ins, outs, sems):
        mine, sends, recvs = self._copies(ins, outs, sems)
        for cp in recvs:
            cp.wait_recv()
        for cp in sends:
            cp.wait_send()
        for cp in mine:
            cp.wait()


def _comm_only(comm, name):
    n_in, n_out = len(comm.arrs), len(comm.out_shape)

    def body(*refs):
        ins, outs, sems = refs[:n_in], refs[n_in:n_in + n_out], refs[n_in + n_out:]
        comm.start(ins, outs, sems)
        comm.finish(ins, outs, sems)

    return pl.pallas_call(
        body, name=name, out_shape=comm.out_shape, in_specs=[ANY] * n_in, out_specs=[ANY] * n_out,
        scratch_shapes=comm.scratch,
    )(*comm.arrs)


def _hosted_call(body, operands, *, name, grid, in_specs, out_specs, out_shape, scratch_shapes=(), sem=(),
                 aliases=None, comm=None):
    single = not isinstance(out_shape, (list, tuple))
    o_specs = [out_specs] if single else list(out_specs)
    o_shape = [out_shape] if single else list(out_shape)
    n_in, n_out, n_scr = len(in_specs), len(o_shape), len(scratch_shapes)
    if comm is None:
        res = pl.pallas_call(body, name=name, grid=grid, in_specs=list(in_specs), out_specs=o_specs,
                             out_shape=o_shape, scratch_shapes=list(scratch_shapes),
                             input_output_aliases=aliases or {}, compiler_params=_cp(*sem))(*operands)
        return (res[0] if single else res), []
    c_in, c_out = len(comm.arrs), len(comm.out_shape)

    def wrapped(*refs):
        ins, cins = refs[:n_in], refs[n_in:n_in + c_in]
        o0 = n_in + c_in
        outs, couts = refs[o0:o0 + n_out], refs[o0 + n_out:o0 + n_out + c_out]
        s0 = o0 + n_out + c_out
        scr, csems = refs[s0:s0 + n_scr], refs[s0 + n_scr:]
        pids = [pl.program_id(a) for a in range(len(grid))]
        first = functools.reduce(jnp.logical_and, [p == 0 for p in pids])
        last = functools.reduce(jnp.logical_and, [p == g - 1 for p, g in zip(pids, grid)])

        @pl.when(first)
        def _():
            comm.start(cins, couts, csems)

        body(*ins, *outs, *scr)

        @pl.when(last)
        def _():
            comm.finish(cins, couts, csems)

    res = pl.pallas_call(
        wrapped, name=name, grid=grid, in_specs=list(in_specs) + [ANY] * c_in, out_specs=o_specs + [ANY] * c_out,
        out_shape=o_shape + comm.out_shape, scratch_shapes=list(scratch_shapes) + comm.scratch,
        input_output_aliases=aliases or {}, compiler_params=_cp(*(["arbitrary"] * len(grid))),
    )(*operands, *comm.arrs)
    return (res[0] if single else res[:n_out]), res[n_out:]


def _adamw(w, g, m, v):
    m2 = ADAM_B1 * m + (1.0 - ADAM_B1) * g
    v2 = ADAM_B2 * v + (1.0 - ADAM_B2) * (g * g)
    m_hat = m2 / (1.0 - ADAM_B1 ** ADAM_STEP)
    v_hat = v2 / (1.0 - ADAM_B2 ** ADAM_STEP)
    delta = -ADAM_LR * (m_hat / (jnp.sqrt(v_hat) + ADAM_EPS) + ADAM_WD * w)
    return delta, m2, v2


def _add_sibling(g, recv, c_idx, name):
    _, R, C = g.shape
    tr = min(R, 512)

    def body(c_ref, g_ref, r_ref, o_ref):
        o_ref[...] = (g_ref[...] + r_ref[...]).astype(BF16)

    return pl.pallas_call(
        body, name=name,
        grid_spec=pltpu.PrefetchScalarGridSpec(
            num_scalar_prefetch=1, grid=(NCHIP, R // tr),
            in_specs=[pl.BlockSpec((1, tr, C), lambda k, i, c_ref: (2 * k + c_ref[0], i, 0)),
                      pl.BlockSpec((1, tr, C), lambda k, i, c_ref: (k, i, 0))],
            out_specs=pl.BlockSpec((1, tr, C), lambda k, i, c_ref: (k, i, 0))),
        out_shape=jax.ShapeDtypeStruct((NCHIP, R, C), BF16),
        compiler_params=_cp("parallel", "parallel"),
    )(c_idx, g, recv)


def _sum_adam(parts, w, m, v, l, prev, name):
    K, R, C = parts.shape
    LR = w.shape[0]
    tr = min(R, 256)
    nb = R // tr

    def body(p_ref, w_ref, m_ref, v_ref, *rest):
        g_ref, d_ref, m2_ref, v2_ref = rest[-4:]
        g = p_ref[0].astype(F32)
        for k in range(1, K):
            g = g + p_ref[k].astype(F32)
        delta, m2, v2 = _adamw(w_ref[...], g, m_ref[...], v_ref[...])
        g_ref[...] = g
        d_ref[...] = delta
        m2_ref[...] = m2
        v2_ref[...] = v2

    blk = pl.BlockSpec((tr, C), lambda i: (l * nb + i, 0))
    shp = jax.ShapeDtypeStruct((LR, C), F32)
    operands = [parts, w, m, v] + (list(prev) if prev is not None else [])
    return pl.pallas_call(
        body, name=name, grid=(nb,),
        in_specs=[pl.BlockSpec((K, tr, C), lambda i: (0, i, 0)), blk, blk, blk] + [ANY] * (len(operands) - 4),
        out_specs=[blk] * 4, out_shape=[shp] * 4,
        input_output_aliases={4 + k: k for k in range(len(operands) - 4)},
        compiler_params=_cp("parallel"),
    )(*operands)


def _sum_rows(parts, name):
    K, R, C = parts.shape
    tr = min(R, 256)
    while R % tr:
        tr //= 2

    def body(p_ref, o_ref):
        g = p_ref[0]
        for k in range(1, K):
            g = g + p_ref[k]
        o_ref[...] = g

    return pl.pallas_call(
        body, name=name, grid=(R // tr,),
        in_specs=[pl.BlockSpec((K, tr, C), lambda i: (0, i, 0))],
        out_specs=pl.BlockSpec((tr, C), lambda i: (i, 0)),
        out_shape=jax.ShapeDtypeStruct((R, C), F32),
        compiler_params=_cp("parallel"),
    )(parts)


def _adam_only(g, w, m, v, name):
    R, C = g.shape
    tr = min(R, 256)
    while R % tr:
        tr //= 2

    def body(g_ref, w_ref, m_ref, v_ref, d_ref, m2_ref, v2_ref):
        delta, m2, v2 = _adamw(w_ref[...], g_ref[...], m_ref[...], v_ref[...])
        d_ref[...] = delta
        m2_ref[...] = m2
        v2_ref[...] = v2

    blk = pl.BlockSpec((tr, C), lambda i: (i, 0))
    shp = jax.ShapeDtypeStruct((R, C), F32)
    return pl.pallas_call(
        body, name=name, grid=(R // tr,), in_specs=[blk] * 4, out_specs=[blk] * 3,
        out_shape=[shp] * 3, compiler_params=_cp("parallel"),
    )(g, w, m, v)


def _mod_fwd(c_all, w_mod, b_cols, name):
    L, D, n = w_mod.shape
    B = c_all.shape[0]

    def body(c_ref, w_ref, b_ref, o_ref):
        cv = c_ref[...]
        sc = (cv * _sigmoid(cv)).astype(BF16)
        o_ref[0] = jnp.dot(sc, w_ref[0].astype(BF16), preferred_element_type=F32) + b_ref[0]

    return pl.pallas_call(
        body, name=name, grid=(L,),
        in_specs=[pl.BlockSpec((B, D), lambda l: (0, 0)),
                  pl.BlockSpec((1, D, n), lambda l: (l, 0, 0)),
                  pl.BlockSpec((1, 1, n), lambda l: (l, 0, 0))],
        out_specs=pl.BlockSpec((1, B, n), lambda l: (l, 0, 0)),
        out_shape=jax.ShapeDtypeStruct((L, B, n), F32),
        compiler_params=_cp("parallel"),
    )(c_all, w_mod, b_cols)


def _wmod_grad_adam(sc_t, dm, w, m, v, name):
    L, D, n = w.shape
    KP = sc_t.shape[1]
    tr = min(D, 512)

    def body(s_ref, dm_ref, w_ref, m_ref, v_ref, g_ref, d_ref, m2_ref, v2_ref):
        g = jnp.dot(s_ref[...], dm_ref[0], preferred_element_type=F32,
                    precision=lax.Precision.HIGHEST)
        delta, m2, v2 = _adamw(w_ref[0], g, m_ref[0], v_ref[0])
        g_ref[0] = g
        d_ref[0] = delta
        m2_ref[0] = m2
        v2_ref[0] = v2

    blk = pl.BlockSpec((1, tr, n), lambda l, i: (l, i, 0))
    shp = jax.ShapeDtypeStruct((L, D, n), F32)
    return pl.pallas_call(
        body, name=name, grid=(L, D // tr),
        in_specs=[pl.BlockSpec((tr, KP), lambda l, i: (i, 0)),
                  pl.BlockSpec((1, KP, n), lambda l, i: (l, 0, 0)), blk, blk, blk],
        out_specs=[blk] * 4, out_shape=[shp] * 4,
        compiler_params=_cp("parallel", "parallel"),
    )(sc_t, dm, w, m, v)


def _vec_spec(D):
    return pl.BlockSpec((1, D), lambda i: (0, 0))


def _pre(x, res, gate, g, scale, shift, name):
    S, D = x.shape
    tr = min(S, ROW_TILE)
    has_res = res is not None
    row = pl.BlockSpec((tr, D), lambda i: (i, 0))

    def body(*refs):
        if has_res:
            x_ref, r_ref, gate_ref, g_ref, sc_ref, sh_ref, xl_ref, h_ref = refs
            xv = x_ref[...] + gate_ref[...] * r_ref[...]
            xl_ref[...] = xv
        else:
            x_ref, g_ref, sc_ref, sh_ref, h_ref = refs
            xv = x_ref[...]
        r = lax.rsqrt(jnp.mean(xv * xv, axis=-1, keepdims=True) + EPS)
        y = (xv * r) * g_ref[...]
        h_ref[...] = (y * (1.0 + sc_ref[...]) + sh_ref[...]).astype(BF16)

    vec = _vec_spec(D)
    if has_res:
        xl, h = pl.pallas_call(
            body, name=name, grid=(S // tr,),
            in_specs=[row, row, vec, vec, vec, vec], out_specs=[row, row],
            out_shape=[jax.ShapeDtypeStruct((S, D), F32), jax.ShapeDtypeStruct((S, D), BF16)],
            compiler_params=_cp("parallel"),
        )(x, res, gate, g, scale, shift)
        return xl, h
    h = pl.pallas_call(
        body, name=name, grid=(S // tr,),
        in_specs=[row, vec, vec, vec], out_specs=row,
        out_shape=jax.ShapeDtypeStruct((S, D), BF16),
        compiler_params=_cp("parallel"),
    )(x, g, scale, shift)
    return x, h


def _pre_bwd(xl, dh, dx_in, g, scale, name):
    S, D = xl.shape
    tr = min(S, ROW_TILE)
    nsteps = S // tr
    row = pl.BlockSpec((tr, D), lambda i: (i, 0))
    vec = _vec_spec(D)

    def body(x_ref, dh_ref, dxin_ref, g_ref, sc_ref, dx_ref, dsh_ref, dsc_ref, dg_ref, acc_sh, acc_t):
        i = pl.program_id(0)
        xv = x_ref[...]
        dh = dh_ref[...]
        r = lax.rsqrt(jnp.mean(xv * xv, axis=-1, keepdims=True) + EPS)
        xn = xv * r
        part_sh = jnp.sum(dh.reshape(tr // 8, 8, D), axis=0)
        part_t = jnp.sum((dh * xn).reshape(tr // 8, 8, D), axis=0)

        @pl.when(i == 0)
        def _():
            acc_sh[...] = part_sh
            acc_t[...] = part_t

        @pl.when(i > 0)
        def _():
            acc_sh[...] += part_sh
            acc_t[...] += part_t

        dxn = dh * (g_ref[...] * (1.0 + sc_ref[...]))
        dx_ref[...] = dxin_ref[...] + r * (dxn - xn * jnp.mean(dxn * xn, axis=-1, keepdims=True))

        @pl.when(i == nsteps - 1)
        def _():
            t = jnp.sum(acc_t[...], axis=0, keepdims=True)
            dsh_ref[...] = jnp.sum(acc_sh[...], axis=0, keepdims=True)
            dsc_ref[...] = t * g_ref[...]
            dg_ref[...] = t * (1.0 + sc_ref[...])

    v = jax.ShapeDtypeStruct((1, D), F32)
    return pl.pallas_call(
        body, name=name, grid=(nsteps,),
        in_specs=[row, row, row, vec, vec], out_specs=[row, vec, vec, vec],
        out_shape=[jax.ShapeDtypeStruct((S, D), F32), v, v, v],
        scratch_shapes=[pltpu.VMEM((8, D), F32), pltpu.VMEM((8, D), F32)],
        compiler_params=_cp("arbitrary"),
    )(xl, dh, dx_in, g, scale)


def _post_bwd(dx, out, gate, name):
    S, D = dx.shape
    tr = min(S, ROW_TILE)
    nsteps = S // tr
    row = pl.BlockSpec((tr, D), lambda i: (i, 0))
    vec = _vec_spec(D)

    def body(dx_ref, o_ref, gate_ref, do_ref, dg_ref, acc):
        i = pl.program_id(0)
        dxv = dx_ref[...]
        do_ref[...] = (dxv * gate_ref[...]).astype(BF16)
        part = jnp.sum((dxv * o_ref[...]).reshape(tr // 8, 8, D), axis=0)

        @pl.when(i == 0)
        def _():
            acc[...] = part

        @pl.when(i > 0)
        def _():
            acc[...] += part

        @pl.when(i == nsteps - 1)
        def _():
            dg_ref[...] = jnp.sum(acc[...], axis=0, keepdims=True)

    return pl.pallas_call(
        body, name=name, grid=(nsteps,),
        in_specs=[row, row, vec], out_specs=[row, vec],
        out_shape=[jax.ShapeDtypeStruct((S, D), BF16), jax.ShapeDtypeStruct((1, D), F32)],
        scratch_shapes=[pltpu.VMEM((8, D), F32)],
        compiler_params=_cp("arbitrary"),
    )(dx, out, gate)


def _loss_head(x, res, gate, gf, tgt, name):
    S, D = x.shape
    tr = min(S, ROW_TILE)
    nsteps = S // tr
    row = pl.BlockSpec((tr, D), lambda i: (i, 0))
    vec = _vec_spec(D)

    def body(x_ref, r_ref, gate_ref, gf_ref, t_ref, dx_ref, loss_ref, dgf_ref, acc, lacc):
        i = pl.program_id(0)
        xv = x_ref[...] + gate_ref[...] * r_ref[...]
        r = lax.rsqrt(jnp.mean(xv * xv, axis=-1, keepdims=True) + EPS)
        xn = xv * r
        err = xn * gf_ref[...] - t_ref[...]
        row_loss = jnp.mean(err * err, axis=-1, keepdims=True)
        lpart = 0.5 * jnp.sum(row_loss, axis=0, keepdims=True)
        dy = err * (1.0 / D)
        part = jnp.sum((dy * xn).reshape(tr // 8, 8, D), axis=0)

        @pl.when(i == 0)
        def _():
            acc[...] = part
            lacc[...] = lpart

        @pl.when(i > 0)
        def _():
            acc[...] += part
            lacc[...] += lpart

        dxn = dy * gf_ref[...]
        dx_ref[...] = r * (dxn - xn * jnp.mean(dxn * xn, axis=-1, keepdims=True))

        @pl.when(i == nsteps - 1)
        def _():
            dgf_ref[...] = jnp.sum(acc[...], axis=0, keepdims=True)
            loss_ref[...] = lacc[...]

    return pl.pallas_call(
        body, name=name, grid=(nsteps,),
        in_specs=[row, row, vec, vec, row],
        out_specs=[row, pl.BlockSpec((1, 1), lambda i: (0, 0)), vec],
        out_shape=[jax.ShapeDtypeStruct((S, D), F32), jax.ShapeDtypeStruct((1, 1), F32),
                   jax.ShapeDtypeStruct((1, D), F32)],
        scratch_shapes=[pltpu.VMEM((8, D), F32), pltpu.VMEM((1, 1), F32)],
        compiler_params=_cp("arbitrary"),
    )(x, res, gate, gf, tgt)


NN = (((1,), (0,)), ((), ()))
NT = (((1,), (1,)), ((), ()))
TN = (((0,), (0,)), ((), ()))


def _mm(name, a, b, out_shape, grid, a_spec, b_spec, o_spec, dims, a2d, b2d, k_axis, sem, alias=None, comm=None):
    def body(*refs):
        a_ref, b_ref, o_ref = refs[0], refs[1], refs[-1]
        r = lax.dot_general(a_ref[...].reshape(a2d), b_ref[...].reshape(b2d), dims,
                            preferred_element_type=F32)
        r = r.reshape(o_ref.shape)
        if k_axis is None:
            o_ref[...] = r.astype(o_ref.dtype)
        else:
            k = pl.program_id(k_axis)

            @pl.when(k == 0)
            def _():
                o_ref[...] = r

            @pl.when(k > 0)
            def _():
                o_ref[...] += r

    operands, in_specs, aliases = [a, b], [a_spec, b_spec], {}
    if alias is not None:
        operands.append(alias)
        in_specs.append(ANY)
        aliases = {2: 0}
    res, extra = _hosted_call(body, operands, name=name, grid=grid, in_specs=in_specs, out_specs=o_spec,
                              out_shape=out_shape, sem=sem, aliases=aliases, comm=comm)
    return res if comm is None else (res, extra)


def _tile(n, pref):
    t = min(n, pref)
    while n % t:
        t -= 128
    return t


def _mm_nn_in(a, w, l, name):
    M, K = a.shape
    _, _, _, n = w.shape
    tm, tn = min(M, 512), _tile(n, 1024)
    nb = n // tn
    return _mm(name, a, w, jax.ShapeDtypeStruct((M, NDEV * n), F32), (NDEV * nb, M // tm),
               pl.BlockSpec((tm, K), lambda j, i: (i, 0)),
               pl.BlockSpec((1, 1, K, tn), lambda j, i: (j // nb, l, 0, j % nb)),
               pl.BlockSpec((tm, tn), lambda j, i: (i, j)),
               NN, (tm, K), (K, tn), None, ("parallel", "parallel"))


def _mm_nn_out(a, w, l, name):
    M, K = a.shape
    _, _, kb, N = w.shape
    tm, tn = min(M, 512), _tile(N, 1024)
    return _mm(name, a, w, jax.ShapeDtypeStruct((M, N), F32), (N // tn, M // tm),
               pl.BlockSpec((tm, K), lambda j, i: (i, 0)),
               pl.BlockSpec((NDEV, 1, kb, tn), lambda j, i: (0, l, 0, j)),
               pl.BlockSpec((tm, tn), lambda j, i: (i, j)),
               NN, (tm, K), (K, tn), None, ("parallel", "parallel"))


def _mm_nt_in(a, w, l, name, comm=None):
    M, _ = a.shape
    _, _, K, n = w.shape
    tm, tk, tc = min(M, 1024), _tile(K, 1024), _tile(n, 1024)
    nb = n // tc
    return _mm(name, a, w, jax.ShapeDtypeStruct((M, K), F32), (M // tm, K // tk, NDEV * nb),
               pl.BlockSpec((tm, tc), lambda i, j, k: (i, k)),
               pl.BlockSpec((1, 1, tk, tc), lambda i, j, k: (k // nb, l, j, k % nb)),
               pl.BlockSpec((tm, tk), lambda i, j, k: (i, j)),
               NT, (tm, tc), (tk, tc), 2, ("parallel", "parallel", "arbitrary"), comm=comm)


def _mm_nt_out(a, w, l, name):
    M, N = a.shape
    _, _, kb, _ = w.shape
    K = NDEV * kb
    tm, tk, tc = min(M, 1024), _tile(K, 1024), _tile(N, 1024)
    per = tk // kb
    return _mm(name, a, w, jax.ShapeDtypeStruct((M, K), F32), (M // tm, K // tk, N // tc),
               pl.BlockSpec((tm, tc), lambda i, j, k: (i, k)),
               pl.BlockSpec((per, 1, kb, tc), lambda i, j, k: (j, l, 0, k)),
               pl.BlockSpec((tm, tk), lambda i, j, k: (i, j)),
               NT, (tm, tc), (tk, tc), 2, ("parallel", "parallel", "arbitrary"))


def _mm_tn_in(a, b, l, L, buf, name, comm=None):
    S, K = a.shape
    n = b.shape[1] // NDEV
    ts, tk, tn = min(S, 1024), _tile(K, 1024), _tile(n, 1024)
    nb = n // tn
    return _mm(name, a, b, jax.ShapeDtypeStruct((NDEV, L, K, n), F32), (NDEV * nb, K // tk, S // ts),
               pl.BlockSpec((ts, tk), lambda j, i, s: (s, i)),
               pl.BlockSpec((ts, tn), lambda j, i, s: (s, j)),
               pl.BlockSpec((1, 1, tk, tn), lambda j, i, s: (j // nb, l, i, j % nb)),
               TN, (ts, tk), (ts, tn), 2, ("parallel", "parallel", "arbitrary"), alias=buf, comm=comm)


def _mm_tn_out(a, b, l, L, buf, name):
    S, K = a.shape
    N = b.shape[1]
    kb = K // NDEV
    ts, tk, tn = min(S, 1024), _tile(K, 1024), _tile(N, 1024)
    per = tk // kb
    return _mm(name, a, b, jax.ShapeDtypeStruct((NDEV, L, kb, N), F32), (N // tn, K // tk, S // ts),
               pl.BlockSpec((ts, tk), lambda j, i, s: (s, i)),
               pl.BlockSpec((ts, tn), lambda j, i, s: (s, j)),
               pl.BlockSpec((per, 1, kb, tn), lambda j, i, s: (i, l, 0, j)),
               TN, (ts, tk), (ts, tn), 2, ("parallel", "parallel", "arbitrary"), alias=buf)


def _attn_bias(T):
    reach = max(w // 2 for w, _ in DILATED_PATTERNS)
    hb = -(-reach // T)
    i = np.arange(T)[:, None]
    j = np.arange(T)[None, :]
    tiles = []
    for d in range(-hb, hb + 1):
        rel = j + d * T - i
        mult = np.zeros((T, T), np.float64)
        for window, dil in DILATED_PATTERNS:
            radius = window // (2 * dil)
            mult += (rel % dil == 0) & (np.abs(rel) <= radius * dil)
        tiles.append(np.where(mult > 0, np.log(np.maximum(mult, 1.0)), NEG_INF))
    return jnp.asarray(np.stack(tiles), F32)


def _rope_tables(S):
    half = HEAD_DIM // 2
    pos = jnp.arange(S, dtype=F32)
    inv = ROPE_THETA ** (-jnp.arange(half, dtype=F32) / half)
    ang = pos[:, None] * inv[None, :]
    cos, sin = jnp.cos(ang), jnp.sin(ang)
    return jnp.concatenate([cos, cos], axis=-1), jnp.concatenate([-sin, sin], axis=-1)


def _rope_apply(t, cosf, sinf, heads, sign):
    outs = []
    for hh in range(heads):
        th = t[:, hh * HEAD_DIM:(hh + 1) * HEAD_DIM]
        outs.append(th * cosf + sign * (pltpu.roll(th, HEAD_DIM // 2, 1) * sinf))
    return outs


def _rope_qkv(proj, cosf, sinf, W, name):
    S = proj.shape[0]
    tr = min(S, ROW_TILE)
    heads = W // HEAD_DIM

    def body(q_ref, k_ref, v_ref, c_ref, s_ref, qo_ref, ko_ref, vo_ref):
        cosf_v, sinf_v = c_ref[...], s_ref[...]
        for src, dst in ((q_ref, qo_ref), (k_ref, ko_ref)):
            for hh, val in enumerate(_rope_apply(src[...], cosf_v, sinf_v, heads, 1.0)):
                dst[:, hh * HEAD_DIM:(hh + 1) * HEAD_DIM] = val.astype(BF16)
        vo_ref[...] = v_ref[...].astype(BF16)

    piece = lambda p: pl.BlockSpec((tr, W), lambda i: (i, p))
    tab = pl.BlockSpec((tr, HEAD_DIM), lambda i: (i, 0))
    out = pl.BlockSpec((tr, W), lambda i: (i, 0))
    shp = jax.ShapeDtypeStruct((S, W), BF16)
    return pl.pallas_call(
        body, name=name, grid=(S // tr,),
        in_specs=[piece(0), piece(1), piece(2), tab, tab], out_specs=[out] * 3, out_shape=[shp] * 3,
        compiler_params=_cp("parallel"),
    )(proj, proj, proj, cosf, sinf)


def _attn_fwd(q, k, v, bias, name, comm=None):
    S, W = q.shape
    H = W // HEAD_DIM
    nd, T, _ = bias.shape
    hb, nq = nd // 2, S // T
    scale = HEAD_DIM ** -0.5

    def body(q_ref, k_ref, v_ref, b_ref, o_ref, lse_ref, m_s, l_s, acc_s):
        i, d = pl.program_id(1), pl.program_id(2)
        j = i + d - hb

        @pl.when(d == 0)
        def _():
            m_s[...] = jnp.full(m_s.shape, -jnp.inf, F32)
            l_s[...] = jnp.zeros(l_s.shape, F32)
            acc_s[...] = jnp.zeros(acc_s.shape, F32)

        @pl.when((j >= 0) & (j < nq))
        def _():
            s = lax.dot_general(q_ref[...], k_ref[...], NT, preferred_element_type=F32) * scale + b_ref[d]
            m_old = m_s[...]
            m_new = jnp.maximum(m_old, jnp.max(s, axis=1, keepdims=True))
            p = jnp.exp(s - m_new)
            alpha = jnp.exp(m_old - m_new)
            l_s[...] = alpha * l_s[...] + jnp.sum(p, axis=1, keepdims=True)
            acc_s[...] = alpha * acc_s[...] + jnp.dot(p.astype(BF16), v_ref[...], preferred_element_type=F32)
            m_s[...] = m_new

        @pl.when(d == nd - 1)
        def _():
            o_ref[...] = acc_s[...] / l_s[...]
            lse_ref[0] = m_s[...] + jnp.log(l_s[...])

    kv = pl.BlockSpec((T, HEAD_DIM), lambda h, i, d: (jnp.clip(i + d - hb, 0, nq - 1), h))
    return _hosted_call(
        body, [q, k, v, bias], name=name, grid=(H, nq, nd),
        in_specs=[pl.BlockSpec((T, HEAD_DIM), lambda h, i, d: (i, h)), kv, kv,
                  pl.BlockSpec((nd, T, T), lambda h, i, d: (0, 0, 0))],
        out_specs=[pl.BlockSpec((T, HEAD_DIM), lambda h, i, d: (i, h)),
                   pl.BlockSpec((1, T, 1), lambda h, i, d: (h, i, 0))],
        out_shape=[jax.ShapeDtypeStruct((S, W), F32), jax.ShapeDtypeStruct((H, S, 1), F32)],
        scratch_shapes=[pltpu.VMEM((T, 1), F32), pltpu.VMEM((T, 1), F32), pltpu.VMEM((T, HEAD_DIM), F32)],
        sem=("parallel", "parallel", "arbitrary"), comm=comm)


def _attn_bwd(q, k, v, do, lse, delta, bias, name, comm=None):
    S, W = q.shape
    H = W // HEAD_DIM
    nd, T, _ = bias.shape
    hb, nq = nd // 2, S // T
    scale = HEAD_DIM ** -0.5

    def body(q_ref, do_ref, lse_ref, dl_ref, k_ref, v_ref, b_ref, dq_ref, dk_ref, dv_ref):
        j, d = pl.program_id(1), pl.program_id(2)
        i = j + d - hb

        @pl.when((j == 0) & (d == 0))
        def _():
            dq_ref[...] = jnp.zeros(dq_ref.shape, F32)

        @pl.when(d == 0)
        def _():
            dk_ref[...] = jnp.zeros(dk_ref.shape, F32)
            dv_ref[...] = jnp.zeros(dv_ref.shape, F32)

        @pl.when((i >= 0) & (i < nq))
        def _():
            qv, kv_, dov = q_ref[...], k_ref[...], do_ref[...]
            s = lax.dot_general(qv, kv_, NT, preferred_element_type=F32) * scale + b_ref[nd - 1 - d]
            p = jnp.exp(s - lse_ref[0])
            dp = lax.dot_general(dov, v_ref[...], NT, preferred_element_type=F32)
            ds = (p * (dp - dl_ref[0]) * scale).astype(BF16)
            dv_ref[...] += lax.dot_general(p.astype(BF16), dov, TN, preferred_element_type=F32)
            dk_ref[...] += lax.dot_general(ds, qv, TN, preferred_element_type=F32)
            rows = pl.ds(pl.multiple_of(i * T, T), T)
            dq_ref[rows, :] += jnp.dot(ds, kv_, preferred_element_type=F32)

    qi = lambda h, j, d: (jnp.clip(j + d - hb, 0, nq - 1), h)
    qs = pl.BlockSpec((T, HEAD_DIM), qi)
    col = pl.BlockSpec((1, T, 1), lambda h, j, d: (h, jnp.clip(j + d - hb, 0, nq - 1), 0))
    kv = pl.BlockSpec((T, HEAD_DIM), lambda h, j, d: (j, h))
    shp = jax.ShapeDtypeStruct((S, W), F32)
    return _hosted_call(
        body, [q, do, lse, delta, k, v, bias], name=name, grid=(H, nq, nd),
        in_specs=[qs, qs, col, col, kv, kv, pl.BlockSpec((nd, T, T), lambda h, j, d: (0, 0, 0))],
        out_specs=[pl.BlockSpec((S, HEAD_DIM), lambda h, j, d: (0, h)), kv, kv],
        out_shape=[shp, shp, shp],
        sem=("parallel", "arbitrary", "arbitrary"), comm=comm)


def _halo_specs(S, tr, W, piece):
    per, last = tr // 8, S // 8 - 1
    prev = pl.BlockSpec((8, W), lambda i: (jnp.maximum(i * per - 1, 0), piece))
    nxt = pl.BlockSpec((8, W), lambda i: (jnp.minimum((i + 1) * per, last), piece))
    return prev, nxt


def _shifted(t, before, after, tr):
    rows = lax.broadcasted_iota(jnp.int32, (tr, 1), 0)
    prev = jnp.where(rows == 0, before, pltpu.roll(t, 1, 0))
    nxt = jnp.where(rows == tr - 1, after, pltpu.roll(t, tr - 1, 0))
    return prev, nxt


def _ab_mix(attn, proj, conv_w, W, name):
    S = attn.shape[0]
    tr = min(S, ROW_TILE)
    nsteps = S // tr

    def body(a_ref, za_ref, ub_ref, gb_ref, gc_ref, zb_ref, ubp, ubn, gcp, gcn, w_ref, y_ref):
        i = pl.program_id(0)
        t = gc_ref[...] * ub_ref[...]
        before = jnp.where(i == 0, 0.0, (gcp[...] * ubp[...])[7:8, :])
        after = jnp.where(i == nsteps - 1, 0.0, (gcn[...] * ubn[...])[0:1, :])
        t_prev, t_next = _shifted(t, before, after, tr)
        w = w_ref[...]
        cv = w[0:1, :] * t_prev + w[1:2, :] * t + w[2:3, :] * t_next
        silu_a, _ = _silu_and_grad(za_ref[...])
        silu_b, _ = _silu_and_grad(zb_ref[...])
        y_ref[:, :W] = (a_ref[...] * silu_a).astype(BF16)
        y_ref[:, W:] = (gb_ref[...] * cv * silu_b).astype(BF16)

    piece = lambda p: pl.BlockSpec((tr, W), lambda i: (i, p))
    ubp, ubn = _halo_specs(S, tr, W, 4)
    gcp, gcn = _halo_specs(S, tr, W, 6)
    return pl.pallas_call(
        body, name=name, grid=(nsteps,),
        in_specs=[pl.BlockSpec((tr, W), lambda i: (i, 0)), piece(3), piece(4), piece(5), piece(6), piece(7),
                  ubp, ubn, gcp, gcn, pl.BlockSpec((3, W), lambda i: (0, 0))],
        out_specs=pl.BlockSpec((tr, 2 * W), lambda i: (i, 0)),
        out_shape=jax.ShapeDtypeStruct((S, 2 * W), BF16),
        compiler_params=_cp("parallel"),
    )(attn, proj, proj, proj, proj, proj, proj, proj, proj, proj, conv_w)


def _dattn_prep(dy, proj, attn, W, name):
    S = attn.shape[0]
    H = W // HEAD_DIM
    tr = min(S, ROW_TILE)

    def body(dy_ref, za_ref, a_ref, do_ref, dl_ref):
        silu_a, _ = _silu_and_grad(za_ref[...])
        do = dy_ref[...] * silu_a
        do_ref[...] = do.astype(BF16)
        prod = do * a_ref[...]
        for hh in range(H):
            dl_ref[hh] = jnp.sum(prod[:, hh * HEAD_DIM:(hh + 1) * HEAD_DIM], axis=1, keepdims=True)

    row = pl.BlockSpec((tr, W), lambda i: (i, 0))
    return pl.pallas_call(
        body, name=name, grid=(S // tr,),
        in_specs=[row, pl.BlockSpec((tr, W), lambda i: (i, 3)), row],
        out_specs=[row, pl.BlockSpec((H, tr, 1), lambda i: (0, i, 0))],
        out_shape=[jax.ShapeDtypeStruct((S, W), BF16), jax.ShapeDtypeStruct((H, S, 1), F32)],
        compiler_params=_cp("parallel"),
    )(dy, proj, attn)


def _ab_bwd(dy, attn, proj, dqr, dkr, dv, cosf, sinf, conv_w, W, name):
    S = attn.shape[0]
    tr = min(S, ROW_TILE // 2)
    nsteps = S // tr
    heads = W // HEAD_DIM

    def body(dya_ref, dyb_ref, a_ref, za_ref, ub_ref, gb_ref, gc_ref, zb_ref, dq_ref, dk_ref, dv_ref,
             c_ref, s_ref, w_ref, dybp, dybn, gbp, gbn, zbp, zbn, ubp, ubn, gcp, gcn,
             dp_ref, dw_ref, acc):
        i = pl.program_id(0)
        first, last = i == 0, i == nsteps - 1
        w = w_ref[...]
        w0, w1, w2 = w[0:1, :], w[1:2, :], w[2:3, :]
        ub, gb, gc, zb = ub_ref[...], gb_ref[...], gc_ref[...], zb_ref[...]
        dyb = dyb_ref[...]
        silu_a, dsilu_a = _silu_and_grad(za_ref[...])
        silu_b, dsilu_b = _silu_and_grad(zb)
        t = gc * ub
        t_prev, t_next = _shifted(t, jnp.where(first, 0.0, (gcp[...] * ubp[...])[7:8, :]),
                                  jnp.where(last, 0.0, (gcn[...] * ubn[...])[0:1, :]), tr)
        cv = w0 * t_prev + w1 * t + w2 * t_next
        dcv = dyb * gb * silu_b
        halo_p = dybp[...] * gbp[...] * _silu_and_grad(zbp[...])[0]
        halo_n = dybn[...] * gbn[...] * _silu_and_grad(zbn[...])[0]
        dcv_prev, dcv_next = _shifted(dcv, jnp.where(first, 0.0, halo_p[7:8, :]),
                                      jnp.where(last, 0.0, halo_n[0:1, :]), tr)
        dt = w0 * dcv_next + w1 * dcv + w2 * dcv_prev
        cosf_v, sinf_v = c_ref[...], s_ref[...]
        for src, base in ((dq_ref, 0), (dk_ref, W)):
            for hh, val in enumerate(_rope_apply(src[...], cosf_v, sinf_v, heads, -1.0)):
                dp_ref[:, base + hh * HEAD_DIM:base + (hh + 1) * HEAD_DIM] = val.astype(BF16)
        dp_ref[:, 2 * W:3 * W] = dv_ref[...].astype(BF16)
        dp_ref[:, 3 * W:4 * W] = (dya_ref[...] * a_ref[...] * dsilu_a).astype(BF16)
        dp_ref[:, 4 * W:5 * W] = (dt * gc).astype(BF16)
        dp_ref[:, 5 * W:6 * W] = (dyb * cv * silu_b).astype(BF16)
        dp_ref[:, 6 * W:7 * W] = (dt * ub).astype(BF16)
        dp_ref[:, 7 * W:8 * W] = (dyb * gb * cv * dsilu_b).astype(BF16)
        tap = lax.broadcasted_iota(jnp.int32, (8, 1), 0)
        part = (jnp.where(tap == 0, jnp.sum(dcv * t_prev, axis=0, keepdims=True), 0.0)
                + jnp.where(tap == 1, jnp.sum(dcv * t, axis=0, keepdims=True), 0.0)
                + jnp.where(tap == 2, jnp.sum(dcv * t_next, axis=0, keepdims=True), 0.0))

        @pl.when(first)
        def _():
            acc[...] = part

        @pl.when(i > 0)
        def _():
            acc[...] += part

        @pl.when(last)
        def _():
            dw_ref[...] = acc[...]

    row = pl.BlockSpec((tr, W), lambda i: (i, 0))
    piece = lambda p: pl.BlockSpec((tr, W), lambda i: (i, p))
    tab = pl.BlockSpec((tr, HEAD_DIM), lambda i: (i, 0))
    dybp, dybn = _halo_specs(S, tr, W, 1)
    gbp, gbn = _halo_specs(S, tr, W, 5)
    zbp, zbn = _halo_specs(S, tr, W, 7)
    ubp, ubn = _halo_specs(S, tr, W, 4)
    gcp, gcn = _halo_specs(S, tr, W, 6)
    return pl.pallas_call(
        body, name=name, grid=(nsteps,),
        in_specs=[piece(0), piece(1), row, piece(3), piece(4), piece(5), piece(6), piece(7), row, row, row,
                  tab, tab, pl.BlockSpec((3, W), lambda i: (0, 0)),
                  dybp, dybn, gbp, gbn, zbp, zbn, ubp, ubn, gcp, gcn],
        out_specs=[pl.BlockSpec((tr, 8 * W), lambda i: (i, 0)), pl.BlockSpec((8, W), lambda i: (0, 0))],
        out_shape=[jax.ShapeDtypeStruct((S, 8 * W), BF16), jax.ShapeDtypeStruct((8, W), F32)],
        scratch_shapes=[pltpu.VMEM((8, W), F32)],
        compiler_params=_cp("arbitrary"),
    )(dy, dy, attn, proj, proj, proj, proj, proj, dqr, dkr, dv, cosf, sinf, conv_w,
      dy, dy, proj, proj, proj, proj, proj, proj, proj, proj)


def _sgu_core(p_ref, lng_ref, lnb_ref, ws_ref, bst_ref, Dc):
    gw = Dc // C_GROUPS
    u_raw, v_raw, z = p_ref[:, :Dc], p_ref[:, Dc:2 * Dc], p_ref[:, 2 * Dc:]
    u, du = _gelu_and_grad(u_raw)
    vg, dvg = _gelu_and_grad(v_raw)
    mu = jnp.mean(vg, axis=-1, keepdims=True)
    vc = vg - mu
    rstd = lax.rsqrt(jnp.mean(vc * vc, axis=-1, keepdims=True) + EPS)
    vhat = vc * rstd
    vn = (vhat * lng_ref[...] + lnb_ref[...]).astype(BF16)
    bst = bst_ref[...]
    mixed = jnp.concatenate(
        [jnp.dot(ws_ref[g].astype(BF16), vn[:, g * gw:(g + 1) * gw], preferred_element_type=F32)
         + bst[:, g:g + 1] for g in range(C_GROUPS)], axis=1)
    sz, dsz = _silu_and_grad(z)
    return u, du, dvg, rstd, vhat, vn, mixed, sz, dsz


def _sgu_fwd(proj, ln_g, ln_b, w_s, b_st, name):
    S, Dc3 = proj.shape
    Dc = Dc3 // 3
    vec = pl.BlockSpec((1, Dc), lambda i: (0, 0))

    def body(p_ref, lng_ref, lnb_ref, ws_ref, bst_ref, y_ref):
        u, _, _, _, _, _, mixed, sz, _ = _sgu_core(p_ref, lng_ref, lnb_ref, ws_ref, bst_ref, Dc)
        y_ref[...] = (u * mixed * sz).astype(BF16)

    return pl.pallas_call(
        body, name=name, grid=(S // C_CHUNK,),
        in_specs=[pl.BlockSpec((C_CHUNK, Dc3), lambda i: (i, 0)), vec, vec,
                  pl.BlockSpec((C_GROUPS, C_CHUNK, C_CHUNK), lambda i: (0, 0, 0)),
                  pl.BlockSpec((C_CHUNK, C_GROUPS), lambda i: (0, 0))],
        out_specs=pl.BlockSpec((C_CHUNK, Dc), lambda i: (i, 0)),
        out_shape=jax.ShapeDtypeStruct((S, Dc), BF16),
        compiler_params=_cp("parallel"),
    )(proj, ln_g, ln_b, w_s, b_st)


def _sgu_bwd(proj, dy, ln_g, ln_b, w_s, w_st, b_st, name):
    S, Dc3 = proj.shape
    Dc = Dc3 // 3
    gw = Dc // C_GROUPS
    nsteps = S // C_CHUNK
    vec = pl.BlockSpec((1, Dc), lambda i: (0, 0))
    wspec = pl.BlockSpec((C_GROUPS, C_CHUNK, C_CHUNK), lambda i: (0, 0, 0))

    def body(p_ref, dy_ref, lng_ref, lnb_ref, ws_ref, wst_ref, bst_ref,
             dp_ref, dws_ref, dbs_ref, dlg_ref, dlb_ref, acc_w, acc_b, acc_g, acc_lb):
        i = pl.program_id(0)
        u, du, dvg, rstd, vhat, vn, mixed, sz, dsz = _sgu_core(p_ref, lng_ref, lnb_ref, ws_ref, bst_ref, Dc)
        dy = dy_ref[...]
        dmixed = dy * u * sz
        dmb = dmixed.astype(BF16)

        @pl.when(i == 0)
        def _():
            acc_w[...] = jnp.zeros(acc_w.shape, F32)
            acc_b[...] = jnp.zeros(acc_b.shape, F32)
            acc_g[...] = jnp.zeros(acc_g.shape, F32)
            acc_lb[...] = jnp.zeros(acc_lb.shape, F32)

        dvn_parts = []
        for g in range(C_GROUPS):
            dmg = dmb[:, g * gw:(g + 1) * gw]
            acc_w[g] += lax.dot_general(dmg, vn[:, g * gw:(g + 1) * gw], NT, preferred_element_type=F32)
            acc_b[g] += dmixed[:, g * gw:(g + 1) * gw]
            dvn_parts.append(jnp.dot(wst_ref[g].astype(BF16), dmg, preferred_element_type=F32))
        dvn = jnp.concatenate(dvn_parts, axis=1)
        acc_g[...] += jnp.sum((dvn * vhat).reshape(C_CHUNK // 8, 8, Dc), axis=0)
        acc_lb[...] += jnp.sum(dvn.reshape(C_CHUNK // 8, 8, Dc), axis=0)
        dvh = dvn * lng_ref[...]
        dvgelu = rstd * (dvh - jnp.mean(dvh, axis=-1, keepdims=True)
                         - vhat * jnp.mean(dvh * vhat, axis=-1, keepdims=True))
        dp_ref[:, :Dc] = (dy * mixed * sz * du).astype(BF16)
        dp_ref[:, Dc:2 * Dc] = (dvgelu * dvg).astype(BF16)
        dp_ref[:, 2 * Dc:] = (dy * u * mixed * dsz).astype(BF16)

        @pl.when(i == nsteps - 1)
        def _():
            dws_ref[...] = acc_w[...]
            for g in range(C_GROUPS):
                dbs_ref[g] = jnp.sum(acc_b[g], axis=1, keepdims=True)
            dlg_ref[...] = jnp.sum(acc_g[...], axis=0, keepdims=True)
            dlb_ref[...] = jnp.sum(acc_lb[...], axis=0, keepdims=True)

    v = jax.ShapeDtypeStruct((1, Dc), F32)
    return pl.pallas_call(
        body, name=name, grid=(nsteps,),
        in_specs=[pl.BlockSpec((C_CHUNK, Dc3), lambda i: (i, 0)), pl.BlockSpec((C_CHUNK, Dc), lambda i: (i, 0)),
                  vec, vec, wspec, wspec, pl.BlockSpec((C_CHUNK, C_GROUPS), lambda i: (0, 0))],
        out_specs=[pl.BlockSpec((C_CHUNK, Dc3), lambda i: (i, 0)), wspec,
                   pl.BlockSpec((C_GROUPS, C_CHUNK, 1), lambda i: (0, 0, 0)), vec, vec],
        out_shape=[jax.ShapeDtypeStruct((S, Dc3), BF16),
                   jax.ShapeDtypeStruct((C_GROUPS, C_CHUNK, C_CHUNK), F32),
                   jax.ShapeDtypeStruct((C_GROUPS, C_CHUNK, 1), F32), v, v],
        scratch_shapes=[pltpu.VMEM((C_GROUPS, C_CHUNK, C_CHUNK), F32), pltpu.VMEM((C_GROUPS, C_CHUNK, gw), F32),
                        pltpu.VMEM((8, Dc), F32), pltpu.VMEM((8, Dc), F32)],
        compiler_params=_cp("arbitrary"),
    )(proj, dy, ln_g, ln_b, w_s, w_st, b_st)


PACK_COLS = 1024
PACK_ROWS = 64


def _pack(vectors):
    flat = jnp.concatenate([v.reshape(-1) for v in vectors])
    pad = (-flat.shape[0]) % (PACK_COLS * PACK_ROWS)
    return jnp.pad(flat, (0, pad)).reshape(-1, PACK_COLS)


def _unshard(g, off, shape):
    L, rest = shape[0], shape[1:]
    size = int(np.prod(shape))
    piece = g[:, off:off + size].reshape((NDEV,) + tuple(shape))
    nd = piece.ndim
    perm = tuple(range(1, nd - 1)) + (0, nd - 1)
    full = jnp.transpose(piece, perm)
    return full.reshape(tuple(shape[:-1]) + (NDEV * shape[-1],)), off + size


def kernel(x, c, ab_norm_g, ab_w_mod, ab_b_mod, ab_w_in, ab_conv_w, ab_w_out, sg_norm_g, sg_w_mod, sg_b_mod, sg_w_in, sg_ln_g, sg_ln_b, sg_w_s, sg_b_s, sg_w_out, final_norm_g, loss_target, m_ab_norm_g, m_ab_w_mod, m_ab_b_mod, m_ab_w_in, m_ab_conv_w, m_ab_w_out, m_sg_norm_g, m_sg_w_mod, m_sg_b_mod, m_sg_w_in, m_sg_ln_g, m_sg_ln_b, m_sg_w_s, m_sg_b_s, m_sg_w_out, m_final_norm_g, v_ab_norm_g, v_ab_w_mod, v_ab_b_mod, v_ab_w_in, v_ab_conv_w, v_ab_w_out, v_sg_norm_g, v_sg_w_mod, v_sg_b_mod, v_sg_w_in, v_sg_ln_g, v_sg_ln_b, v_sg_w_s, v_sg_b_s, v_sg_w_out, v_final_norm_g):
    _, S, D = x.shape
    L = ab_norm_g.shape[0]
    W = ab_conv_w.shape[2] * NDEV
    n_ab, n_sg = ab_w_in.shape[2], sg_w_in.shape[2]
    n_mod = ab_w_mod.shape[2]
    kb = ab_w_out.shape[1]
    xi, yi, ci = _position()
    dev = 4 * xi + 2 * yi + ci
    x2, tgt = x.reshape(S, D), loss_target.reshape(S, D)

    small = [c, ab_conv_w, sg_norm_g, sg_ln_g, sg_ln_b]
    (g1,) = _comm_only(_Gather([_pack(small)]), "ag_small")
    g1 = g1.reshape(NDEV, -1)
    c_all = g1[:, :D]
    off = D
    conv_full, off = _unshard(g1, off, ab_conv_w.shape)
    sg_norm_full, off = _unshard(g1, off, sg_norm_g.shape)
    ln_g_full, off = _unshard(g1, off, sg_ln_g.shape)
    ln_b_full, off = _unshard(g1, off, sg_ln_b.shape)

    ab_b_cols = lax.dynamic_slice_in_dim(ab_b_mod, dev * n_mod, n_mod, axis=1)
    m_ab = _mod_fwd(c_all, ab_w_mod, ab_b_cols.reshape(L, 1, n_mod), "mod_fwd_ab")
    m_sg = _mod_fwd(c_all, sg_w_mod, sg_b_mod.reshape(L, 1, n_mod), "mod_fwd_sg")
    m_part = jnp.stack([m_ab, m_sg]).transpose(2, 0, 1, 3).reshape(NDEV, 2 * L * n_mod)
    (g2,) = _comm_only(_Gather([m_part]), "ag_mod")
    mine = lax.dynamic_index_in_dim(g2, dev, axis=1, keepdims=False)
    mods = mine.reshape(NDEV, 2, L, n_mod).transpose(1, 2, 0, 3).reshape(2, L, 3 * D)

    def mod_of(kind, i):
        m = mods[kind, i]
        return m[:D].reshape(1, D), m[D:2 * D].reshape(1, D), m[2 * D:].reshape(1, D)

    big_w = [[(ab_w_in, m_ab_w_in, v_ab_w_in), (ab_w_out, m_ab_w_out, v_ab_w_out)],
             [(sg_w_in, m_sg_w_in, v_sg_w_in), (sg_w_out, m_sg_w_out, v_sg_w_out)]]
    big_names = [["ab_w_in", "ab_w_out"], ["sg_w_in", "sg_w_out"]]
    n_layers = 2 * L
    shards = [[big_w[layer % 2][k][0][layer // 2].astype(BF16) for k in range(2)] for layer in range(n_layers)]
    weights = {}

    def keep_gathered(layers, res):
        for k, layer in enumerate(layers):
            gi, go = res[2 * k], res[2 * k + 1]
            weights[layer] = (gi.reshape(NDEV, 1, D, gi.shape[-1]), go.reshape(NDEV, 1, kb, D))

    keep_gathered([0], _comm_only(_Gather(shards[0]), "ag_w_layer0"))

    cosf, sinf = _rope_tables(S)
    T = min(S, ATTN_TILE)
    bias = _attn_bias(T)
    norm_g = [ab_norm_g, sg_norm_full]
    w_s_t = jnp.swapaxes(sg_w_s, -1, -2)
    b_s_t = jnp.swapaxes(sg_b_s, -1, -2)

    saved = []
    x_cur, res, gate_prev = x2, None, None
    for layer in range(2 * L):
        kind, i = layer % 2, layer // 2
        tag = f"{'ab' if kind == 0 else 'sg'}{i}"
        shift, scale, gate = mod_of(kind, i)
        g = norm_g[kind][i].reshape(1, D)
        xl, h = _pre(x_cur, res, gate_prev, g, scale, shift, f"pre_{tag}")
        proj = _mm_nn_in(h, weights[layer][0], 0, f"proj_{tag}")
        rec = dict(xl=xl, h=h, proj=proj, g=g, scale=scale, gate=gate)
        if kind == 0:
            qr, kr, vb = _rope_qkv(proj, cosf, sinf, W, f"rope_{tag}")
            ahead = [nxt for nxt in (layer + 1, layer + 2) if nxt < n_layers]
            comm = _Gather([s for nxt in ahead for s in shards[nxt]]) if ahead else None
            (attn, lse), got = _attn_fwd(qr, kr, vb, bias, f"attn_{tag}", comm)
            keep_gathered(ahead, got)
            y = _ab_mix(attn, proj, conv_full[i], W, f"mix_{tag}")
            rec.update(qr=qr, kr=kr, vb=vb, attn=attn, lse=lse)
        else:
            y = _sgu_fwd(proj, ln_g_full[i].reshape(1, D), ln_b_full[i].reshape(1, D), sg_w_s[i], b_s_t[i],
                         f"sgu_{tag}")
        out = _mm_nn_out(y, weights[layer][1], 0, f"out_{tag}")
        rec.update(y=y, out=out)
        saved.append(rec)
        x_cur, res, gate_prev = xl, out, gate

    dx, loss_part, d_final_g = _loss_head(x_cur, res, gate_prev, final_norm_g.reshape(1, D), tgt, "loss_head")
    loss = lax.psum(loss_part[0, 0], ("x", "y", "c"))

    c_idx = ci.reshape(1).astype(jnp.int32)
    big_res = {}
    pending = None

    def finish_layer(done, from_chips):
        for k in range(2):
            nm = big_names[done % 2][k]
            w, m, v = big_w[done % 2][k]
            flat = lambda a: a.reshape(L * a.shape[1], a.shape[2])
            big_res[nm] = _sum_adam(from_chips[k], flat(w), flat(m), flat(v), done // 2, big_res.get(nm),
                                    f"adam_{nm}{done // 2}")

    dm = [[None] * L, [None] * L]
    d_norm = [[None] * L, [None] * L]
    d_conv, d_lng, d_lnb, d_ws, d_bs = [None] * L, [None] * L, [None] * L, [None] * L, [None] * L
    for layer in reversed(range(2 * L)):
        kind, i = layer % 2, layer // 2
        tag = f"{'ab' if kind == 0 else 'sg'}{i}"
        rec = saved[layer]
        w_in_l, w_out_l = weights[layer]
        dout, dgate = _post_bwd(dx, rec["out"], rec["gate"], f"post_bwd_{tag}")
        dy = _mm_nt_out(dout, w_out_l, 0, f"dy_{tag}")
        dwo = _mm_tn_out(rec["y"], dout, 0, 1, None, f"dwout_{tag}")
        if kind == 0:
            do, delta = _dattn_prep(dy, rec["proj"], rec["attn"], W, f"dattn_{tag}")
            comm = _ToChips(pending[1]) if pending else None
            (dqr, dkr, dvv), got = _attn_bwd(rec["qr"], rec["kr"], rec["vb"], do, rec["lse"], delta, bias,
                                             f"attn_bwd_{tag}", comm)
            if pending:
                finish_layer(pending[0], got)
            dproj, dcw = _ab_bwd(dy, rec["attn"], rec["proj"], dqr, dkr, dvv, cosf, sinf, conv_full[i], W,
                                 f"mix_bwd_{tag}")
            d_conv[i] = dcw[:3]
            dh = _mm_nt_in(dproj, w_in_l, 0, f"dh_{tag}")
            dwi = _mm_tn_in(rec["h"], dproj, 0, 1, None, f"dwin_{tag}")
        else:
            dproj, d_ws[i], dbs, d_lng[i], d_lnb[i] = _sgu_bwd(
                rec["proj"], dy, ln_g_full[i].reshape(1, D), ln_b_full[i].reshape(1, D),
                sg_w_s[i], w_s_t[i], b_s_t[i], f"sgu_bwd_{tag}")
            d_bs[i] = dbs.reshape(C_GROUPS, C_CHUNK)
            if pending:
                dh, got_in = _mm_nt_in(dproj, w_in_l, 0, f"dh_{tag}", _ToChips(pending[1][:1]))
                dwi, got_out = _mm_tn_in(rec["h"], dproj, 0, 1, None, f"dwin_{tag}", _ToChips(pending[1][1:]))
                finish_layer(pending[0], [got_in[0], got_out[0]])
            else:
                dh = _mm_nt_in(dproj, w_in_l, 0, f"dh_{tag}")
                dwi = _mm_tn_in(rec["h"], dproj, 0, 1, None, f"dwin_{tag}")
        dx, dshift, dscale, d_norm[kind][i] = _pre_bwd(rec["xl"], dh, dx, rec["g"], rec["scale"],
                                                       f"pre_bwd_{tag}")
        dm[kind][i] = jnp.concatenate([dshift, dscale, dgate], axis=1).reshape(3 * D)
        grads = [dwi.reshape(NDEV, D, -1), dwo.reshape(NDEV, kb, D)]
        from_sibling = _comm_only(_ToSibling(grads), f"rs_sibling_{tag}")
        pending = (layer, [_add_sibling(g, r, c_idx, f"rs_add_{big_names[kind][k]}{i}")
                           for k, (g, r) in enumerate(zip(grads, from_sibling))])
    grad_x = dx.reshape(1, S, D)
    finish_layer(pending[0], _comm_only(_ToChips(pending[1]), "rs_chips_last"))
    for kind in range(2):
        for k in range(2):
            nm = big_names[kind][k]
            big_res[nm] = [o.reshape(big_w[kind][k][0].shape) for o in big_res[nm]]

    stack = lambda xs: jnp.stack(xs)
    pack_items = [stack(dm[0]), stack(dm[1]), stack(d_norm[0]).reshape(L, D), stack(d_conv),
                  stack(d_norm[1]).reshape(L, D), stack(d_lng).reshape(L, D), stack(d_lnb).reshape(L, D),
                  stack(d_ws), stack(d_bs), d_final_g]
    (g3,) = _comm_only(_Gather([_pack(pack_items)]), "ag_grads")
    P = g3.shape[1] * g3.shape[2]
    tot = _sum_rows(g3, "sum_small").reshape(P)
    g3 = g3.reshape(NDEV, P)
    sizes = [int(np.prod(p.shape)) for p in pack_items]
    offs = np.concatenate([[0], np.cumsum(sizes)]).tolist()
    seg = lambda k, shape: tot[offs[k]:offs[k + 1]].reshape(shape)

    def shard(full, n):
        return lax.dynamic_slice_in_dim(full, dev * n, n, axis=full.ndim - 1)

    g_ab_b_mod = seg(0, (L, 3 * D))
    g_sg_b_mod = shard(seg(1, (L, 3 * D)), n_mod)
    g_ab_norm = seg(2, (L, D))
    g_conv = shard(seg(3, (L, 3, W)), W // NDEV)
    g_sg_norm = shard(seg(4, (L, D)), kb)
    g_ln_g = shard(seg(5, (L, D)), kb)
    g_ln_b = shard(seg(6, (L, D)), kb)
    g_w_s = seg(7, sg_w_s.shape)
    g_b_s = seg(8, sg_b_s.shape)
    g_final = seg(9, (D,))

    small_w = [("ab_norm_g", g_ab_norm, ab_norm_g, m_ab_norm_g, v_ab_norm_g),
               ("ab_b_mod", g_ab_b_mod, ab_b_mod, m_ab_b_mod, v_ab_b_mod),
               ("ab_conv_w", g_conv, ab_conv_w, m_ab_conv_w, v_ab_conv_w),
               ("sg_norm_g", g_sg_norm, sg_norm_g, m_sg_norm_g, v_sg_norm_g),
               ("sg_b_mod", g_sg_b_mod, sg_b_mod, m_sg_b_mod, v_sg_b_mod),
               ("sg_ln_g", g_ln_g, sg_ln_g, m_sg_ln_g, v_sg_ln_g),
               ("sg_ln_b", g_ln_b, sg_ln_b, m_sg_ln_b, v_sg_ln_b),
               ("sg_w_s", g_w_s, sg_w_s, m_sg_w_s, v_sg_w_s),
               ("sg_b_s", g_b_s, sg_b_s, m_sg_b_s, v_sg_b_s),
               ("final_norm_g", g_final, final_norm_g, m_final_norm_g, v_final_norm_g)]
    packed = [_pack([t[k] for t in small_w]) for k in (1, 2, 3, 4)]
    upd = _adam_only(*packed, "adam_small")
    small_res = {}
    o = 0
    for nm, g, w, _, _ in small_w:
        size = int(np.prod(w.shape))
        small_res[nm] = [g] + [u.reshape(-1)[o:o + size].reshape(w.shape) for u in upd]
        o += size

    KP = 128
    sc_t = jnp.pad((c_all * jax.nn.sigmoid(c_all)).T, ((0, 0), (0, KP - NDEV)))
    mod_res = {}
    for kind, nm, (w, m, v) in ((0, "ab_w_mod", (ab_w_mod, m_ab_w_mod, v_ab_w_mod)),
                                (1, "sg_w_mod", (sg_w_mod, m_sg_w_mod, v_sg_w_mod))):
        dm_all = g3[:, offs[kind]:offs[kind + 1]].reshape(NDEV, L, 3 * D)
        cols = jnp.pad(shard(dm_all, n_mod).transpose(1, 0, 2), ((0, 0), (0, KP - NDEV), (0, 0)))
        mod_res[nm] = _wmod_grad_adam(sc_t, cols, w, m, v, f"adam_{nm}")

    order = ["ab_norm_g", "ab_w_mod", "ab_b_mod", "ab_w_in", "ab_conv_w", "ab_w_out", "sg_norm_g", "sg_w_mod",
             "sg_b_mod", "sg_w_in", "sg_ln_g", "sg_ln_b", "sg_w_s", "sg_b_s", "sg_w_out", "final_norm_g"]
    res = {**big_res, **small_res, **mod_res}
    outs = [loss, grad_x]
    for k in range(4):
        outs += [res[nm][k] for nm in order]
    return tuple(outs)
```

```python
import functools
import math

import numpy as np
import jax
import jax.numpy as jnp
from jax import lax
from jax.experimental import pallas as pl
from jax.experimental.pallas import tpu as pltpu

F32 = jnp.float32
BF16 = jnp.bfloat16

NDEV = 8
NCHIP = 4
EPS = 1e-6
HEAD_DIM = 128
ROPE_THETA = 10000.0
DILATED_PATTERNS = ((128, 1), (512, 4), (2048, 16))
NEG_INF = -1e30
C_CHUNK = 128
C_GROUPS = 8
ADAM_LR = 0.001
ADAM_B1 = 0.9
ADAM_B2 = 0.999
ADAM_EPS = 1e-08
ADAM_WD = 0.01
ADAM_STEP = 10
GELU_K = math.sqrt(2.0 / math.pi)
GELU_C = 0.044715

VMEM_LIMIT_BYTES = 56 * 1024 * 1024
ATTN_TILE = 512
HEADS_PER_STEP = 2
ROW_TILE = 256
MESH = pl.DeviceIdType.MESH
ANY = pl.BlockSpec(memory_space=pl.ANY)


def _cp(*sem):
    return pltpu.CompilerParams(dimension_semantics=sem, vmem_limit_bytes=VMEM_LIMIT_BYTES)


def _sigmoid(z):
    return 1.0 / (1.0 + jnp.exp(-z))


def _silu_and_grad(z):
    s = _sigmoid(z)
    return z * s, s * (1.0 + z * (1.0 - s))


def _gelu_and_grad(x):
    x2 = x * x
    t = jnp.tanh(GELU_K * (x + GELU_C * x2 * x))
    g = 0.5 * x * (1.0 + t)
    dg = 0.5 * (1.0 + t) + 0.5 * x * (1.0 - t * t) * (GELU_K * (1.0 + 3.0 * GELU_C * x2))
    return g, dg


def _position():
    return lax.axis_index("x"), lax.axis_index("y"), lax.axis_index("c")


def _chips(x, y):
    return [(1 - x, y), (x, 1 - y), (1 - x, 1 - y)]


class _Gather:
    def __init__(self, arrs):
        n = len(arrs)
        self.arrs = list(arrs)
        self.out_shape = [jax.ShapeDtypeStruct((NDEV,) + a.shape, a.dtype) for a in arrs]
        self.scratch = [pltpu.SemaphoreType.DMA((n, 7)), pltpu.SemaphoreType.DMA((n, 7)),
                        pltpu.SemaphoreType.DMA((n,))]

    def _copies(self, ins, outs, sems):
        send_sems, recv_sems, local_sems = sems
        x, y, c = _position()

        def copy(a, k, block, to, src=None):
            dst = outs[a].at[4 * block[0] + 2 * block[1] + block[2]]
            return pltpu.make_async_remote_copy(
                src_ref=dst if src is None else src, dst_ref=dst,
                send_sem=send_sems.at[a, k], recv_sem=recv_sems.at[a, k],
                device_id=to, device_id_type=MESH)

        n = len(ins)
        me, sibling = (x, y, c), (x, y, 1 - c)
        mine = [pltpu.make_async_copy(ins[a], outs[a].at[4 * x + 2 * y + c], local_sems.at[a]) for a in range(n)]
        first = []
        for a in range(n):
            first.append(copy(a, 0, me, sibling, src=ins[a]))
            first += [copy(a, 1 + j, me, (*chip, c), src=ins[a]) for j, chip in enumerate(_chips(x, y))]
        return copy, mine, first

    def start(self, ins, outs, sems):
        _, mine, first = self._copies(ins, outs, sems)
        for cp in mine + first:
            cp.start()

    def finish(self, ins, outs, sems):
        copy, mine, first = self._copies(ins, outs, sems)
        x, y, c = _position()
        me, sibling = (x, y, c), (x, y, 1 - c)
        passed = []
        for j, chip in enumerate(_chips(x, y)):
            for a in range(len(ins)):
                copy(a, 1 + j, (*chip, c), me).wait_recv()
                fwd = copy(a, 4 + j, (*chip, c), sibling)
                fwd.start()
                passed.append(fwd)
        for a in range(len(ins)):
            copy(a, 0, sibling, me).wait_recv()
            for j, chip in enumerate(_chips(x, y)):
                copy(a, 4 + j, (*chip, 1 - c), me).wait_recv()
        for cp in first + passed:
            cp.wait_send()
        for cp in mine:
            cp.wait()


class _ToSibling:
    def __init__(self, gs):
        n = len(gs)
        self.arrs = list(gs)
        self.out_shape = [jax.ShapeDtypeStruct((NCHIP,) + g.shape[1:], g.dtype) for g in gs]
        self.scratch = [pltpu.SemaphoreType.DMA((n, NCHIP)), pltpu.SemaphoreType.DMA((n, NCHIP))]

    def _copies(self, ins, outs, sems):
        send_sems, recv_sems = sems
        x, y, c = _position()
        return [pltpu.make_async_remote_copy(
            src_ref=ins[a].at[2 * k + (1 - c)], dst_ref=outs[a].at[k],
            send_sem=send_sems.at[a, k], recv_sem=recv_sems.at[a, k],
            device_id=(x, y, 1 - c), device_id_type=MESH) for a in range(len(ins)) for k in range(NCHIP)]

    def start(self, ins, outs, sems):
        for cp in self._copies(ins, outs, sems):
            cp.start()

    def finish(self, ins, outs, sems):
        copies = self._copies(ins, outs, sems)
        for cp in copies:
            cp.wait_recv()
        for cp in copies:
            cp.wait_send()


class _ToChips:
    def __init__(self, ps):
        n = len(ps)
        self.arrs = list(ps)
        self.out_shape = [jax.ShapeDtypeStruct(p.shape, p.dtype) for p in ps]
        self.scratch = [pltpu.SemaphoreType.DMA((n, 3)), pltpu.SemaphoreType.DMA((n, 3)),
                        pltpu.SemaphoreType.DMA((n,))]

    def _copies(self, ins, outs, sems, arrivals):
        send_sems, recv_sems, local_sems = sems
        x, y, c = _position()
        mychip = 2 * x + y
        n = len(ins)
        mine = [pltpu.make_async_copy(ins[a].at[mychip], outs[a].at[mychip], local_sems.at[a]) for a in range(n)]
        sends, recvs = [], []
        for a in range(n):
            for j, chip in enumerate(_chips(x, y)):
                sends.append(pltpu.make_async_remote_copy(
                    src_ref=ins[a].at[2 * chip[0] + chip[1]], dst_ref=outs[a].at[mychip],
                    send_sem=send_sems.at[a, j], recv_sem=recv_sems.at[a, j],
                    device_id=(*chip, c), device_id_type=MESH))
                if arrivals:
                    slot = outs[a].at[2 * chip[0] + chip[1]]
                    recvs.append(pltpu.make_async_remote_copy(
                        src_ref=slot, dst_ref=slot, send_sem=send_sems.at[a, j], recv_sem=recv_sems.at[a, j],
                        device_id=(*chip, c), device_id_type=MESH))
        return mine, sends, recvs

    def start(self, ins, outs, sems):
        mine, sends, _ = self._copies(ins, outs, sems, False)
        for cp in mine + sends:
            cp.start()

    def finish(self, ins, outs, sems):
        mine, sends, recvs = self._copies(ins, outs, sems, True)
        for cp in recvs:
            cp.wait_recv()
        for cp in sends:
            cp.wait_send()
        for cp in mine:
            cp.wait()


def _comm_only(comm, name):
    n_in, n_out = len(comm.arrs), len(comm.out_shape)

    def body(*refs):
        ins, outs, sems = refs[:n_in], refs[n_in:n_in + n_out], refs[n_in + n_out:]
        comm.start(ins, outs, sems)
        comm.finish(ins, outs, sems)

    return pl.pallas_call(
        body, name=name, out_shape=comm.out_shape, in_specs=[ANY] * n_in, out_specs=[ANY] * n_out,
        scratch_shapes=comm.scratch,
    )(*comm.arrs)


def _hosted_call(body, operands, *, name, grid, in_specs, out_specs, out_shape, scratch_shapes=(), sem=(),
                 aliases=None, comm=None):
    single = not isinstance(out_shape, (list, tuple))
    o_specs = [out_specs] if single else list(out_specs)
    o_shape = [out_shape] if single else list(out_shape)
    n_in, n_out, n_scr = len(in_specs), len(o_shape), len(scratch_shapes)
    if comm is None:
        res = pl.pallas_call(body, name=name, grid=grid, in_specs=list(in_specs), out_specs=o_specs,
                             out_shape=o_shape, scratch_shapes=list(scratch_shapes),
                             input_output_aliases=aliases or {}, compiler_params=_cp(*sem))(*operands)
        return (res[0] if single else res), []
    c_in, c_out = len(comm.arrs), len(comm.out_shape)

    def wrapped(*refs):
        ins, cins = refs[:n_in], refs[n_in:n_in + c_in]
        o0 = n_in + c_in
        outs, couts = refs[o0:o0 + n_out], refs[o0 + n_out:o0 + n_out + c_out]
        s0 = o0 + n_out + c_out
        scr, csems = refs[s0:s0 + n_scr], refs[s0 + n_scr:]
        pids = [pl.program_id(a) for a in range(len(grid))]
        first = functools.reduce(jnp.logical_and, [p == 0 for p in pids])
        last = functools.reduce(jnp.logical_and, [p == g - 1 for p, g in zip(pids, grid)])

        @pl.when(first)
        def _():
            comm.start(cins, couts, csems)

        body(*ins, *outs, *scr)

        @pl.when(last)
        def _():
            comm.finish(cins, couts, csems)

    res = pl.pallas_call(
        wrapped, name=name, grid=grid, in_specs=list(in_specs) + [ANY] * c_in, out_specs=o_specs + [ANY] * c_out,
        out_shape=o_shape + comm.out_shape, scratch_shapes=list(scratch_shapes) + comm.scratch,
        input_output_aliases=aliases or {}, compiler_params=_cp(*(["arbitrary"] * len(grid))),
    )(*operands, *comm.arrs)
    return (res[0] if single else res[:n_out]), res[n_out:]


def _adamw(w, g, m, v):
    m2 = ADAM_B1 * m + (1.0 - ADAM_B1) * g
    v2 = ADAM_B2 * v + (1.0 - ADAM_B2) * (g * g)
    m_hat = m2 / (1.0 - ADAM_B1 ** ADAM_STEP)
    v_hat = v2 / (1.0 - ADAM_B2 ** ADAM_STEP)
    delta = -ADAM_LR * (m_hat / (jnp.sqrt(v_hat) + ADAM_EPS) + ADAM_WD * w)
    return delta, m2, v2


def _add_sibling(g, recv, c_idx, name):
    _, R, C = g.shape
    tr = min(R, 512)

    def body(c_ref, g_ref, r_ref, o_ref):
        o_ref[...] = (g_ref[...] + r_ref[...]).astype(BF16)

    return pl.pallas_call(
        body, name=name,
        grid_spec=pltpu.PrefetchScalarGridSpec(
            num_scalar_prefetch=1, grid=(NCHIP, R // tr),
            in_specs=[pl.BlockSpec((1, tr, C), lambda k, i, c_ref: (2 * k + c_ref[0], i, 0)),
                      pl.BlockSpec((1, tr, C), lambda k, i, c_ref: (k, i, 0))],
            out_specs=pl.BlockSpec((1, tr, C), lambda k, i, c_ref: (k, i, 0))),
        out_shape=jax.ShapeDtypeStruct((NCHIP, R, C), BF16),
        compiler_params=_cp("parallel", "parallel"),
    )(c_idx, g, recv)


def _sum_adam(parts, w, m, v, l, prev, name):
    K, R, C = parts.shape
    LR = w.shape[0]
    tr = min(R, 256)
    nb = R // tr

    def body(p_ref, w_ref, m_ref, v_ref, *rest):
        g_ref, d_ref, m2_ref, v2_ref = rest[-4:]
        g = p_ref[0].astype(F32)
        for k in range(1, K):
            g = g + p_ref[k].astype(F32)
        delta, m2, v2 = _adamw(w_ref[...], g, m_ref[...], v_ref[...])
        g_ref[...] = g
        d_ref[...] = delta
        m2_ref[...] = m2
        v2_ref[...] = v2

    blk = pl.BlockSpec((tr, C), lambda i: (l * nb + i, 0))
    shp = jax.ShapeDtypeStruct((LR, C), F32)
    operands = [parts, w, m, v] + (list(prev) if prev is not None else [])
    return pl.pallas_call(
        body, name=name, grid=(nb,),
        in_specs=[pl.BlockSpec((K, tr, C), lambda i: (0, i, 0)), blk, blk, blk] + [ANY] * (len(operands) - 4),
        out_specs=[blk] * 4, out_shape=[shp] * 4,
        input_output_aliases={4 + k: k for k in range(len(operands) - 4)},
        compiler_params=_cp("parallel"),
    )(*operands)


def _sum_rows(parts, name):
    K, R, C = parts.shape
    tr = min(R, 256)
    while R % tr:
        tr //= 2

    def body(p_ref, o_ref):
        g = p_ref[0]
        for k in range(1, K):
            g = g + p_ref[k]
        o_ref[...] = g

    return pl.pallas_call(
        body, name=name, grid=(R // tr,),
        in_specs=[pl.BlockSpec((K, tr, C), lambda i: (0, i, 0))],
        out_specs=pl.BlockSpec((tr, C), lambda i: (i, 0)),
        out_shape=jax.ShapeDtypeStruct((R, C), F32),
        compiler_params=_cp("parallel"),
    )(parts)


def _adam_only(g, w, m, v, name):
    R, C = g.shape
    tr = min(R, 256)
    while R % tr:
        tr //= 2

    def body(g_ref, w_ref, m_ref, v_ref, d_ref, m2_ref, v2_ref):
        delta, m2, v2 = _adamw(w_ref[...], g_ref[...], m_ref[...], v_ref[...])
        d_ref[...] = delta
        m2_ref[...] = m2
        v2_ref[...] = v2

    blk = pl.BlockSpec((tr, C), lambda i: (i, 0))
    shp = jax.ShapeDtypeStruct((R, C), F32)
    return pl.pallas_call(
        body, name=name, grid=(R // tr,), in_specs=[blk] * 4, out_specs=[blk] * 3,
        out_shape=[shp] * 3, compiler_params=_cp("parallel"),
    )(g, w, m, v)


def _mod_fwd(c_all, w_mod, b_cols, name):
    L, D, n = w_mod.shape
    B = c_all.shape[0]

    def body(c_ref, w_ref, b_ref, o_ref):
        cv = c_ref[...]
        sc = (cv * _sigmoid(cv)).astype(BF16)
        o_ref[0] = jnp.dot(sc, w_ref[0].astype(BF16), preferred_element_type=F32) + b_ref[0]

    return pl.pallas_call(
        body, name=name, grid=(L,),
        in_specs=[pl.BlockSpec((B, D), lambda l: (0, 0)),
                  pl.BlockSpec((1, D, n), lambda l: (l, 0, 0)),
                  pl.BlockSpec((1, 1, n), lambda l: (l, 0, 0))],
        out_specs=pl.BlockSpec((1, B, n), lambda l: (l, 0, 0)),
        out_shape=jax.ShapeDtypeStruct((L, B, n), F32),
        compiler_params=_cp("parallel"),
    )(c_all, w_mod, b_cols)


def _wmod_grad_adam(sc_t, dm, w, m, v, name):
    L, D, n = w.shape
    KP = sc_t.shape[1]
    tr = min(D, 512)

    def body(s_ref, dm_ref, w_ref, m_ref, v_ref, g_ref, d_ref, m2_ref, v2_ref):
        g = jnp.dot(s_ref[...], dm_ref[0], preferred_element_type=F32,
                    precision=lax.Precision.HIGHEST)
        delta, m2, v2 = _adamw(w_ref[0], g, m_ref[0], v_ref[0])
        g_ref[0] = g
        d_ref[0] = delta
        m2_ref[0] = m2
        v2_ref[0] = v2

    blk = pl.BlockSpec((1, tr, n), lambda l, i: (l, i, 0))
    shp = jax.ShapeDtypeStruct((L, D, n), F32)
    return pl.pallas_call(
        body, name=name, grid=(L, D // tr),
        in_specs=[pl.BlockSpec((tr, KP), lambda l, i: (i, 0)),
                  pl.BlockSpec((1, KP, n), lambda l, i: (l, 0, 0)), blk, blk, blk],
        out_specs=[blk] * 4, out_shape=[shp] * 4,
        compiler_params=_cp("parallel", "parallel"),
    )(sc_t, dm, w, m, v)


def _vec_spec(D):
    return pl.BlockSpec((1, D), lambda i: (0, 0))


def _pre(x, res, gate, g, scale, shift, name):
    S, D = x.shape
    tr = min(S, ROW_TILE)
    has_res = res is not None
    row = pl.BlockSpec((tr, D), lambda i: (i, 0))

    def body(*refs):
        if has_res:
            x_ref, r_ref, gate_ref, g_ref, sc_ref, sh_ref, xl_ref, h_ref = refs
            xv = x_ref[...] + gate_ref[...] * r_ref[...]
            xl_ref[...] = xv
        else:
            x_ref, g_ref, sc_ref, sh_ref, h_ref = refs
            xv = x_ref[...]
        r = lax.rsqrt(jnp.mean(xv * xv, axis=-1, keepdims=True) + EPS)
        y = (xv * r) * g_ref[...]
        h_ref[...] = (y * (1.0 + sc_ref[...]) + sh_ref[...]).astype(BF16)

    vec = _vec_spec(D)
    if has_res:
        xl, h = pl.pallas_call(
            body, name=name, grid=(S // tr,),
            in_specs=[row, row, vec, vec, vec, vec], out_specs=[row, row],
            out_shape=[jax.ShapeDtypeStruct((S, D), F32), jax.ShapeDtypeStruct((S, D), BF16)],
            compiler_params=_cp("parallel"),
        )(x, res, gate, g, scale, shift)
        return xl, h
    h = pl.pallas_call(
        body, name=name, grid=(S // tr,),
        in_specs=[row, vec, vec, vec], out_specs=row,
        out_shape=jax.ShapeDtypeStruct((S, D), BF16),
        compiler_params=_cp("parallel"),
    )(x, g, scale, shift)
    return x, h


def _pre_bwd(xl, dh, dx_in, g, scale, name, comm=None):
    S, D = xl.shape
    tr = min(S, ROW_TILE)
    nsteps = S // tr
    row = pl.BlockSpec((tr, D), lambda i: (i, 0))
    vec = _vec_spec(D)

    def body(x_ref, dh_ref, dxin_ref, g_ref, sc_ref, dx_ref, dsh_ref, dsc_ref, dg_ref, acc_sh, acc_t):
        i = pl.program_id(0)
        xv = x_ref[...]
        dh = dh_ref[...]
        r = lax.rsqrt(jnp.mean(xv * xv, axis=-1, keepdims=True) + EPS)
        xn = xv * r
        part_sh = jnp.sum(dh.reshape(tr // 8, 8, D), axis=0)
        part_t = jnp.sum((dh * xn).reshape(tr // 8, 8, D), axis=0)

        @pl.when(i == 0)
        def _():
            acc_sh[...] = part_sh
            acc_t[...] = part_t

        @pl.when(i > 0)
        def _():
            acc_sh[...] += part_sh
            acc_t[...] += part_t

        dxn = dh * (g_ref[...] * (1.0 + sc_ref[...]))
        dx_ref[...] = dxin_ref[...] + r * (dxn - xn * jnp.mean(dxn * xn, axis=-1, keepdims=True))

        @pl.when(i == nsteps - 1)
        def _():
            t = jnp.sum(acc_t[...], axis=0, keepdims=True)
            dsh_ref[...] = jnp.sum(acc_sh[...], axis=0, keepdims=True)
            dsc_ref[...] = t * g_ref[...]
            dg_ref[...] = t * (1.0 + sc_ref[...])

    v = jax.ShapeDtypeStruct((1, D), F32)
    return _hosted_call(
        body, [xl, dh, dx_in, g, scale], name=name, grid=(nsteps,),
        in_specs=[row, row, row, vec, vec], out_specs=[row, vec, vec, vec],
        out_shape=[jax.ShapeDtypeStruct((S, D), F32), v, v, v],
        scratch_shapes=[pltpu.VMEM((8, D), F32), pltpu.VMEM((8, D), F32)],
        sem=("arbitrary",), comm=comm)


def _post_bwd(dx, out, gate, name):
    S, D = dx.shape
    tr = min(S, ROW_TILE)
    nsteps = S // tr
    row = pl.BlockSpec((tr, D), lambda i: (i, 0))
    vec = _vec_spec(D)

    def body(dx_ref, o_ref, gate_ref, do_ref, dg_ref, acc):
        i = pl.program_id(0)
        dxv = dx_ref[...]
        do_ref[...] = (dxv * gate_ref[...]).astype(BF16)
        part = jnp.sum((dxv * o_ref[...]).reshape(tr // 8, 8, D), axis=0)

        @pl.when(i == 0)
        def _():
            acc[...] = part

        @pl.when(i > 0)
        def _():
            acc[...] += part

        @pl.when(i == nsteps - 1)
        def _():
            dg_ref[...] = jnp.sum(acc[...], axis=0, keepdims=True)

    return pl.pallas_call(
        body, name=name, grid=(nsteps,),
        in_specs=[row, row, vec], out_specs=[row, vec],
        out_shape=[jax.ShapeDtypeStruct((S, D), BF16), jax.ShapeDtypeStruct((1, D), F32)],
        scratch_shapes=[pltpu.VMEM((8, D), F32)],
        compiler_params=_cp("arbitrary"),
    )(dx, out, gate)


def _loss_head(x, res, gate, gf, tgt, name):
    S, D = x.shape
    tr = min(S, ROW_TILE)
    nsteps = S // tr
    row = pl.BlockSpec((tr, D), lambda i: (i, 0))
    vec = _vec_spec(D)

    def body(x_ref, r_ref, gate_ref, gf_ref, t_ref, dx_ref, loss_ref, dgf_ref, acc, lacc):
        i = pl.program_id(0)
        xv = x_ref[...] + gate_ref[...] * r_ref[...]
        r = lax.rsqrt(jnp.mean(xv * xv, axis=-1, keepdims=True) + EPS)
        xn = xv * r
        err = xn * gf_ref[...] - t_ref[...]
        row_loss = jnp.mean(err * err, axis=-1, keepdims=True)
        lpart = 0.5 * jnp.sum(row_loss, axis=0, keepdims=True)
        dy = err * (1.0 / D)
        part = jnp.sum((dy * xn).reshape(tr // 8, 8, D), axis=0)

        @pl.when(i == 0)
        def _():
            acc[...] = part
            lacc[...] = lpart

        @pl.when(i > 0)
        def _():
            acc[...] += part
            lacc[...] += lpart

        dxn = dy * gf_ref[...]
        dx_ref[...] = r * (dxn - xn * jnp.mean(dxn * xn, axis=-1, keepdims=True))

        @pl.when(i == nsteps - 1)
        def _():
            dgf_ref[...] = jnp.sum(acc[...], axis=0, keepdims=True)
            loss_ref[...] = lacc[...]

    return pl.pallas_call(
        body, name=name, grid=(nsteps,),
        in_specs=[row, row, vec, vec, row],
        out_specs=[row, pl.BlockSpec((1, 1), lambda i: (0, 0)), vec],
        out_shape=[jax.ShapeDtypeStruct((S, D), F32), jax.ShapeDtypeStruct((1, 1), F32),
                   jax.ShapeDtypeStruct((1, D), F32)],
        scratch_shapes=[pltpu.VMEM((8, D), F32), pltpu.VMEM((1, 1), F32)],
        compiler_params=_cp("arbitrary"),
    )(x, res, gate, gf, tgt)


NN = (((1,), (0,)), ((), ()))
NT = (((1,), (1,)), ((), ()))
TN = (((0,), (0,)), ((), ()))


def _mm(name, a, b, out_shape, grid, a_spec, b_spec, o_spec, dims, a2d, b2d, k_axis, sem, alias=None, comm=None):
    def body(*refs):
        a_ref, b_ref, o_ref = refs[0], refs[1], refs[-1]
        r = lax.dot_general(a_ref[...].reshape(a2d), b_ref[...].reshape(b2d), dims,
                            preferred_element_type=F32)
        r = r.reshape(o_ref.shape)
        if k_axis is None:
            o_ref[...] = r.astype(o_ref.dtype)
        else:
            k = pl.program_id(k_axis)

            @pl.when(k == 0)
            def _():
                o_ref[...] = r

            @pl.when(k > 0)
            def _():
                o_ref[...] += r

    operands, in_specs, aliases = [a, b], [a_spec, b_spec], {}
    if alias is not None:
        operands.append(alias)
        in_specs.append(ANY)
        aliases = {2: 0}
    res, extra = _hosted_call(body, operands, name=name, grid=grid, in_specs=in_specs, out_specs=o_spec,
                              out_shape=out_shape, sem=sem, aliases=aliases, comm=comm)
    return res if comm is None else (res, extra)


def _tile(n, pref):
    t = min(n, pref)
    while n % t:
        t -= 128
    return t


def _mm_nn_in(a, w, l, name, comm=None):
    M, K = a.shape
    _, _, _, n = w.shape
    tm, tn = min(M, 512), _tile(n, 1024)
    nb = n // tn
    return _mm(name, a, w, jax.ShapeDtypeStruct((M, NDEV * n), F32), (NDEV * nb, M // tm),
               pl.BlockSpec((tm, K), lambda j, i: (i, 0)),
               pl.BlockSpec((1, 1, K, tn), lambda j, i: (j // nb, l, 0, j % nb)),
               pl.BlockSpec((tm, tn), lambda j, i: (i, j)),
               NN, (tm, K), (K, tn), None, ("parallel", "parallel"), comm=comm)


def _mm_nn_out(a, w, l, name):
    M, K = a.shape
    _, _, kb, N = w.shape
    tm, tn = min(M, 512), _tile(N, 1024)
    return _mm(name, a, w, jax.ShapeDtypeStruct((M, N), F32), (N // tn, M // tm),
               pl.BlockSpec((tm, K), lambda j, i: (i, 0)),
               pl.BlockSpec((NDEV, 1, kb, tn), lambda j, i: (0, l, 0, j)),
               pl.BlockSpec((tm, tn), lambda j, i: (i, j)),
               NN, (tm, K), (K, tn), None, ("parallel", "parallel"))


def _mm_nt_in(a, w, l, name, comm=None):
    M, _ = a.shape
    _, _, K, n = w.shape
    tm, tk, tc = min(M, 1024), _tile(K, 1024), _tile(n, 1024)
    nb = n // tc
    return _mm(name, a, w, jax.ShapeDtypeStruct((M, K), F32), (M // tm, K // tk, NDEV * nb),
               pl.BlockSpec((tm, tc), lambda i, j, k: (i, k)),
               pl.BlockSpec((1, 1, tk, tc), lambda i, j, k: (k // nb, l, j, k % nb)),
               pl.BlockSpec((tm, tk), lambda i, j, k: (i, j)),
               NT, (tm, tc), (tk, tc), 2, ("parallel", "parallel", "arbitrary"), comm=comm)


def _mm_nt_out(a, w, l, name):
    M, N = a.shape
    _, _, kb, _ = w.shape
    K = NDEV * kb
    tm, tk, tc = min(M, 1024), _tile(K, 1024), _tile(N, 1024)
    per = tk // kb
    return _mm(name, a, w, jax.ShapeDtypeStruct((M, K), F32), (M // tm, K // tk, N // tc),
               pl.BlockSpec((tm, tc), lambda i, j, k: (i, k)),
               pl.BlockSpec((per, 1, kb, tc), lambda i, j, k: (j, l, 0, k)),
               pl.BlockSpec((tm, tk), lambda i, j, k: (i, j)),
               NT, (tm, tc), (tk, tc), 2, ("parallel", "parallel", "arbitrary"))


def _mm_tn_in(a, b, l, L, buf, name, comm=None):
    S, K = a.shape
    n = b.shape[1] // NDEV
    ts, tk, tn = min(S, 1024), _tile(K, 1024), _tile(n, 1024)
    nb = n // tn
    return _mm(name, a, b, jax.ShapeDtypeStruct((NDEV, L, K, n), F32), (NDEV * nb, K // tk, S // ts),
               pl.BlockSpec((ts, tk), lambda j, i, s: (s, i)),
               pl.BlockSpec((ts, tn), lambda j, i, s: (s, j)),
               pl.BlockSpec((1, 1, tk, tn), lambda j, i, s: (j // nb, l, i, j % nb)),
               TN, (ts, tk), (ts, tn), 2, ("parallel", "parallel", "arbitrary"), alias=buf, comm=comm)


def _mm_tn_out(a, b, l, L, buf, name):
    S, K = a.shape
    N = b.shape[1]
    kb = K // NDEV
    ts, tk, tn = min(S, 1024), _tile(K, 1024), _tile(N, 1024)
    per = tk // kb
    return _mm(name, a, b, jax.ShapeDtypeStruct((NDEV, L, kb, N), F32), (N // tn, K // tk, S // ts),
               pl.BlockSpec((ts, tk), lambda j, i, s: (s, i)),
               pl.BlockSpec((ts, tn), lambda j, i, s: (s, j)),
               pl.BlockSpec((per, 1, kb, tn), lambda j, i, s: (i, l, 0, j)),
               TN, (ts, tk), (ts, tn), 2, ("parallel", "parallel", "arbitrary"), alias=buf)


def _attn_bias(T):
    reach = max(w // 2 for w, _ in DILATED_PATTERNS)
    hb = -(-reach // T)
    i = np.arange(T)[:, None]
    j = np.arange(T)[None, :]
    tiles = []
    for d in range(-hb, hb + 1):
        rel = j + d * T - i
        mult = np.zeros((T, T), np.float64)
        for window, dil in DILATED_PATTERNS:
            radius = window // (2 * dil)
            mult += (rel % dil == 0) & (np.abs(rel) <= radius * dil)
        tiles.append(np.where(mult > 0, np.log(np.maximum(mult, 1.0)), NEG_INF))
    return jnp.asarray(np.stack(tiles), F32)


def _rope_tables(S):
    half = HEAD_DIM // 2
    pos = jnp.arange(S, dtype=F32)
    inv = ROPE_THETA ** (-jnp.arange(half, dtype=F32) / half)
    ang = pos[:, None] * inv[None, :]
    cos, sin = jnp.cos(ang), jnp.sin(ang)
    return jnp.concatenate([cos, cos], axis=-1), jnp.concatenate([-sin, sin], axis=-1)


def _rope_apply(t, cosf, sinf, heads, sign):
    outs = []
    for hh in range(heads):
        th = t[:, hh * HEAD_DIM:(hh + 1) * HEAD_DIM]
        outs.append(th * cosf + sign * (pltpu.roll(th, HEAD_DIM // 2, 1) * sinf))
    return outs


def _rope_qkv(proj, cosf, sinf, W, name):
    S = proj.shape[0]
    tr = min(S, ROW_TILE)
    heads = W // HEAD_DIM

    def body(q_ref, k_ref, v_ref, c_ref, s_ref, qo_ref, ko_ref, vo_ref):
        cosf_v, sinf_v = c_ref[...], s_ref[...]
        for src, dst in ((q_ref, qo_ref), (k_ref, ko_ref)):
            for hh, val in enumerate(_rope_apply(src[...], cosf_v, sinf_v, heads, 1.0)):
                dst[:, hh * HEAD_DIM:(hh + 1) * HEAD_DIM] = val.astype(BF16)
        vo_ref[...] = v_ref[...].astype(BF16)

    piece = lambda p: pl.BlockSpec((tr, W), lambda i: (i, p))
    tab = pl.BlockSpec((tr, HEAD_DIM), lambda i: (i, 0))
    out = pl.BlockSpec((tr, W), lambda i: (i, 0))
    shp = jax.ShapeDtypeStruct((S, W), BF16)
    return pl.pallas_call(
        body, name=name, grid=(S // tr,),
        in_specs=[piece(0), piece(1), piece(2), tab, tab], out_specs=[out] * 3, out_shape=[shp] * 3,
        compiler_params=_cp("parallel"),
    )(proj, proj, proj, cosf, sinf)


def _attn_fwd(q, k, v, bias, name, comm=None):
    S, W = q.shape
    H = W // HEAD_DIM
    nd, T, _ = bias.shape
    hb, nq = nd // 2, S // T
    scale = HEAD_DIM ** -0.5
    hp = min(H, HEADS_PER_STEP)
    wp = hp * HEAD_DIM

    def body(q_ref, k_ref, v_ref, b_ref, o_ref, lse_ref, m_s, l_s, acc_s):
        i, d = pl.program_id(1), pl.program_id(2)
        j = i + d - hb

        @pl.when(d == 0)
        def _():
            m_s[...] = jnp.full(m_s.shape, -jnp.inf, F32)
            l_s[...] = jnp.zeros(l_s.shape, F32)
            acc_s[...] = jnp.zeros(acc_s.shape, F32)

        @pl.when((j >= 0) & (j < nq))
        def _():
            b = b_ref[d]
            for hh in range(hp):
                cols = slice(hh * HEAD_DIM, (hh + 1) * HEAD_DIM)
                s = lax.dot_general(q_ref[:, cols], k_ref[:, cols], NT, preferred_element_type=F32) * scale + b
                m_old = m_s[hh]
                m_new = jnp.maximum(m_old, jnp.max(s, axis=1, keepdims=True))
                p = jnp.exp(s - m_new)
                alpha = jnp.exp(m_old - m_new)
                l_s[hh] = alpha * l_s[hh] + jnp.sum(p, axis=1, keepdims=True)
                acc_s[:, cols] = alpha * acc_s[:, cols] + jnp.dot(p.astype(BF16), v_ref[:, cols],
                                                                  preferred_element_type=F32)
                m_s[hh] = m_new

        @pl.when(d == nd - 1)
        def _():
            for hh in range(hp):
                cols = slice(hh * HEAD_DIM, (hh + 1) * HEAD_DIM)
                o_ref[:, cols] = acc_s[:, cols] / l_s[hh]
                lse_ref[hh] = m_s[hh] + jnp.log(l_s[hh])

    kv = pl.BlockSpec((T, wp), lambda h, i, d: (jnp.clip(i + d - hb, 0, nq - 1), h))
    return _hosted_call(
        body, [q, k, v, bias], name=name, grid=(H // hp, nq, nd),
        in_specs=[pl.BlockSpec((T, wp), lambda h, i, d: (i, h)), kv, kv,
                  pl.BlockSpec((nd, T, T), lambda h, i, d: (0, 0, 0))],
        out_specs=[pl.BlockSpec((T, wp), lambda h, i, d: (i, h)),
                   pl.BlockSpec((hp, T, 1), lambda h, i, d: (h, i, 0))],
        out_shape=[jax.ShapeDtypeStruct((S, W), F32), jax.ShapeDtypeStruct((H, S, 1), F32)],
        scratch_shapes=[pltpu.VMEM((hp, T, 1), F32), pltpu.VMEM((hp, T, 1), F32), pltpu.VMEM((T, wp), F32)],
        sem=("parallel", "parallel", "arbitrary"), comm=comm)


def _attn_bwd(q, k, v, do, lse, delta, bias, name, comm=None):
    S, W = q.shape
    H = W // HEAD_DIM
    nd, T, _ = bias.shape
    hb, nq = nd // 2, S // T
    scale = HEAD_DIM ** -0.5
    hp = min(H, HEADS_PER_STEP)
    wp = hp * HEAD_DIM

    def body(q_ref, do_ref, lse_ref, dl_ref, k_ref, v_ref, b_ref, dq_ref, dk_ref, dv_ref):
        j, d = pl.program_id(1), pl.program_id(2)
        i = j + d - hb

        @pl.when((j == 0) & (d == 0))
        def _():
            dq_ref[...] = jnp.zeros(dq_ref.shape, F32)

        @pl.when(d == 0)
        def _():
            dk_ref[...] = jnp.zeros(dk_ref.shape, F32)
            dv_ref[...] = jnp.zeros(dv_ref.shape, F32)

        @pl.when((i >= 0) & (i < nq))
        def _():
            b = b_ref[nd - 1 - d]
            rows = pl.ds(pl.multiple_of(i * T, T), T)
            for hh in range(hp):
                cols = slice(hh * HEAD_DIM, (hh + 1) * HEAD_DIM)
                qv, kv_, dov = q_ref[:, cols], k_ref[:, cols], do_ref[:, cols]
                s = lax.dot_general(qv, kv_, NT, preferred_element_type=F32) * scale + b
                p = jnp.exp(s - lse_ref[hh])
                dp = lax.dot_general(dov, v_ref[:, cols], NT, preferred_element_type=F32)
                ds = (p * (dp - dl_ref[hh]) * scale).astype(BF16)
                dv_ref[:, cols] += lax.dot_general(p.astype(BF16), dov, TN, preferred_element_type=F32)
                dk_ref[:, cols] += lax.dot_general(ds, qv, TN, preferred_element_type=F32)
                dq_ref[rows, cols] += jnp.dot(ds, kv_, preferred_element_type=F32)

    qi = lambda h, j, d: (jnp.clip(j + d - hb, 0, nq - 1), h)
    qs = pl.BlockSpec((T, wp), qi)
    col = pl.BlockSpec((hp, T, 1), lambda h, j, d: (h, jnp.clip(j + d - hb, 0, nq - 1), 0))
    kv = pl.BlockSpec((T, wp), lambda h, j, d: (j, h))
    shp = jax.ShapeDtypeStruct((S, W), F32)
    return _hosted_call(
        body, [q, do, lse, delta, k, v, bias], name=name, grid=(H // hp, nq, nd),
        in_specs=[qs, qs, col, col, kv, kv, pl.BlockSpec((nd, T, T), lambda h, j, d: (0, 0, 0))],
        out_specs=[pl.BlockSpec((S, wp), lambda h, j, d: (0, h)), kv, kv],
        out_shape=[shp, shp, shp],
        sem=("parallel", "arbitrary", "arbitrary"), comm=comm)


def _halo_specs(S, tr, W, piece):
    per, last = tr // 8, S // 8 - 1
    prev = pl.BlockSpec((8, W), lambda i: (jnp.maximum(i * per - 1, 0), piece))
    nxt = pl.BlockSpec((8, W), lambda i: (jnp.minimum((i + 1) * per, last), piece))
    return prev, nxt


def _shifted(t, before, after, tr):
    rows = lax.broadcasted_iota(jnp.int32, (tr, 1), 0)
    prev = jnp.where(rows == 0, before, pltpu.roll(t, 1, 0))
    nxt = jnp.where(rows == tr - 1, after, pltpu.roll(t, tr - 1, 0))
    return prev, nxt


def _ab_mix(attn, proj, conv_w, W, name):
    S = attn.shape[0]
    tr = min(S, ROW_TILE)
    nsteps = S // tr

    def body(a_ref, za_ref, ub_ref, gb_ref, gc_ref, zb_ref, ubp, ubn, gcp, gcn, w_ref, y_ref):
        i = pl.program_id(0)
        t = gc_ref[...] * ub_ref[...]
        before = jnp.where(i == 0, 0.0, (gcp[...] * ubp[...])[7:8, :])
        after = jnp.where(i == nsteps - 1, 0.0, (gcn[...] * ubn[...])[0:1, :])
        t_prev, t_next = _shifted(t, before, after, tr)
        w = w_ref[...]
        cv = w[0:1, :] * t_prev + w[1:2, :] * t + w[2:3, :] * t_next
        silu_a, _ = _silu_and_grad(za_ref[...])
        silu_b, _ = _silu_and_grad(zb_ref[...])
        y_ref[:, :W] = (a_ref[...] * silu_a).astype(BF16)
        y_ref[:, W:] = (gb_ref[...] * cv * silu_b).astype(BF16)

    piece = lambda p: pl.BlockSpec((tr, W), lambda i: (i, p))
    ubp, ubn = _halo_specs(S, tr, W, 4)
    gcp, gcn = _halo_specs(S, tr, W, 6)
    return pl.pallas_call(
        body, name=name, grid=(nsteps,),
        in_specs=[pl.BlockSpec((tr, W), lambda i: (i, 0)), piece(3), piece(4), piece(5), piece(6), piece(7),
                  ubp, ubn, gcp, gcn, pl.BlockSpec((3, W), lambda i: (0, 0))],
        out_specs=pl.BlockSpec((tr, 2 * W), lambda i: (i, 0)),
        out_shape=jax.ShapeDtypeStruct((S, 2 * W), BF16),
        compiler_params=_cp("parallel"),
    )(attn, proj, proj, proj, proj, proj, proj, proj, proj, proj, conv_w)


def _dattn_prep(dy, proj, attn, W, name):
    S = attn.shape[0]
    H = W // HEAD_DIM
    tr = min(S, ROW_TILE)

    def body(dy_ref, za_ref, a_ref, do_ref, dl_ref):
        silu_a, _ = _silu_and_grad(za_ref[...])
        do = dy_ref[...] * silu_a
        do_ref[...] = do.astype(BF16)
        prod = do * a_ref[...]
        for hh in range(H):
            dl_ref[hh] = jnp.sum(prod[:, hh * HEAD_DIM:(hh + 1) * HEAD_DIM], axis=1, keepdims=True)

    row = pl.BlockSpec((tr, W), lambda i: (i, 0))
    return pl.pallas_call(
        body, name=name, grid=(S // tr,),
        in_specs=[row, pl.BlockSpec((tr, W), lambda i: (i, 3)), row],
        out_specs=[row, pl.BlockSpec((H, tr, 1), lambda i: (0, i, 0))],
        out_shape=[jax.ShapeDtypeStruct((S, W), BF16), jax.ShapeDtypeStruct((H, S, 1), F32)],
        compiler_params=_cp("parallel"),
    )(dy, proj, attn)


def _ab_bwd(dy, attn, proj, dqr, dkr, dv, cosf, sinf, conv_w, W, name):
    S = attn.shape[0]
    tr = min(S, ROW_TILE // 2)
    nsteps = S // tr
    heads = W // HEAD_DIM

    def body(dya_ref, dyb_ref, a_ref, za_ref, ub_ref, gb_ref, gc_ref, zb_ref, dq_ref, dk_ref, dv_ref,
             c_ref, s_ref, w_ref, dybp, dybn, gbp, gbn, zbp, zbn, ubp, ubn, gcp, gcn,
             dp_ref, dw_ref, acc):
        i = pl.program_id(0)
        first, last = i == 0, i == nsteps - 1
        w = w_ref[...]
        w0, w1, w2 = w[0:1, :], w[1:2, :], w[2:3, :]
        ub, gb, gc, zb = ub_ref[...], gb_ref[...], gc_ref[...], zb_ref[...]
        dyb = dyb_ref[...]
        silu_a, dsilu_a = _silu_and_grad(za_ref[...])
        silu_b, dsilu_b = _silu_and_grad(zb)
        t = gc * ub
        t_prev, t_next = _shifted(t, jnp.where(first, 0.0, (gcp[...] * ubp[...])[7:8, :]),
                                  jnp.where(last, 0.0, (gcn[...] * ubn[...])[0:1, :]), tr)
        cv = w0 * t_prev + w1 * t + w2 * t_next
        dcv = dyb * gb * silu_b
        halo_p = dybp[...] * gbp[...] * _silu_and_grad(zbp[...])[0]
        halo_n = dybn[...] * gbn[...] * _silu_and_grad(zbn[...])[0]
        dcv_prev, dcv_next = _shifted(dcv, jnp.where(first, 0.0, halo_p[7:8, :]),
                                      jnp.where(last, 0.0, halo_n[0:1, :]), tr)
        dt = w0 * dcv_next + w1 * dcv + w2 * dcv_prev
        cosf_v, sinf_v = c_ref[...], s_ref[...]
        for src, base in ((dq_ref, 0), (dk_ref, W)):
            for hh, val in enumerate(_rope_apply(src[...], cosf_v, sinf_v, heads, -1.0)):
                dp_ref[:, base + hh * HEAD_DIM:base + (hh + 1) * HEAD_DIM] = val.astype(BF16)
        dp_ref[:, 2 * W:3 * W] = dv_ref[...].astype(BF16)
        dp_ref[:, 3 * W:4 * W] = (dya_ref[...] * a_ref[...] * dsilu_a).astype(BF16)
        dp_ref[:, 4 * W:5 * W] = (dt * gc).astype(BF16)
        dp_ref[:, 5 * W:6 * W] = (dyb * cv * silu_b).astype(BF16)
        dp_ref[:, 6 * W:7 * W] = (dt * ub).astype(BF16)
        dp_ref[:, 7 * W:8 * W] = (dyb * gb * cv * dsilu_b).astype(BF16)
        tap = lax.broadcasted_iota(jnp.int32, (8, 1), 0)
        part = (jnp.where(tap == 0, jnp.sum(dcv * t_prev, axis=0, keepdims=True), 0.0)
                + jnp.where(tap == 1, jnp.sum(dcv * t, axis=0, keepdims=True), 0.0)
                + jnp.where(tap == 2, jnp.sum(dcv * t_next, axis=0, keepdims=True), 0.0))

        @pl.when(first)
        def _():
            acc[...] = part

        @pl.when(i > 0)
        def _():
            acc[...] += part

        @pl.when(last)
        def _():
            dw_ref[...] = acc[...]

    row = pl.BlockSpec((tr, W), lambda i: (i, 0))
    piece = lambda p: pl.BlockSpec((tr, W), lambda i: (i, p))
    tab = pl.BlockSpec((tr, HEAD_DIM), lambda i: (i, 0))
    dybp, dybn = _halo_specs(S, tr, W, 1)
    gbp, gbn = _halo_specs(S, tr, W, 5)
    zbp, zbn = _halo_specs(S, tr, W, 7)
    ubp, ubn = _halo_specs(S, tr, W, 4)
    gcp, gcn = _halo_specs(S, tr, W, 6)
    return pl.pallas_call(
        body, name=name, grid=(nsteps,),
        in_specs=[piece(0), piece(1), row, piece(3), piece(4), piece(5), piece(6), piece(7), row, row, row,
                  tab, tab, pl.BlockSpec((3, W), lambda i: (0, 0)),
                  dybp, dybn, gbp, gbn, zbp, zbn, ubp, ubn, gcp, gcn],
        out_specs=[pl.BlockSpec((tr, 8 * W), lambda i: (i, 0)), pl.BlockSpec((8, W), lambda i: (0, 0))],
        out_shape=[jax.ShapeDtypeStruct((S, 8 * W), BF16), jax.ShapeDtypeStruct((8, W), F32)],
        scratch_shapes=[pltpu.VMEM((8, W), F32)],
        compiler_params=_cp("arbitrary"),
    )(dy, dy, attn, proj, proj, proj, proj, proj, dqr, dkr, dv, cosf, sinf, conv_w,
      dy, dy, proj, proj, proj, proj, proj, proj, proj, proj)


def _sgu_core(p_ref, lng_ref, lnb_ref, ws_ref, bst_ref, Dc):
    gw = Dc // C_GROUPS
    u_raw, v_raw, z = p_ref[:, :Dc], p_ref[:, Dc:2 * Dc], p_ref[:, 2 * Dc:]
    u, du = _gelu_and_grad(u_raw)
    vg, dvg = _gelu_and_grad(v_raw)
    mu = jnp.mean(vg, axis=-1, keepdims=True)
    vc = vg - mu
    rstd = lax.rsqrt(jnp.mean(vc * vc, axis=-1, keepdims=True) + EPS)
    vhat = vc * rstd
    vn = (vhat * lng_ref[...] + lnb_ref[...]).astype(BF16)
    bst = bst_ref[...]
    mixed = jnp.concatenate(
        [jnp.dot(ws_ref[g].astype(BF16), vn[:, g * gw:(g + 1) * gw], preferred_element_type=F32)
         + bst[:, g:g + 1] for g in range(C_GROUPS)], axis=1)
    sz, dsz = _silu_and_grad(z)
    return u, du, dvg, rstd, vhat, vn, mixed, sz, dsz


def _sgu_fwd(proj, ln_g, ln_b, w_s, b_st, name):
    S, Dc3 = proj.shape
    Dc = Dc3 // 3
    vec = pl.BlockSpec((1, Dc), lambda i: (0, 0))

    def body(p_ref, lng_ref, lnb_ref, ws_ref, bst_ref, y_ref):
        u, _, _, _, _, _, mixed, sz, _ = _sgu_core(p_ref, lng_ref, lnb_ref, ws_ref, bst_ref, Dc)
        y_ref[...] = (u * mixed * sz).astype(BF16)

    return pl.pallas_call(
        body, name=name, grid=(S // C_CHUNK,),
        in_specs=[pl.BlockSpec((C_CHUNK, Dc3), lambda i: (i, 0)), vec, vec,
                  pl.BlockSpec((C_GROUPS, C_CHUNK, C_CHUNK), lambda i: (0, 0, 0)),
                  pl.BlockSpec((C_CHUNK, C_GROUPS), lambda i: (0, 0))],
        out_specs=pl.BlockSpec((C_CHUNK, Dc), lambda i: (i, 0)),
        out_shape=jax.ShapeDtypeStruct((S, Dc), BF16),
        compiler_params=_cp("parallel"),
    )(proj, ln_g, ln_b, w_s, b_st)


def _sgu_bwd(proj, dy, ln_g, ln_b, w_s, w_st, b_st, name):
    S, Dc3 = proj.shape
    Dc = Dc3 // 3
    gw = Dc // C_GROUPS
    nsteps = S // C_CHUNK
    vec = pl.BlockSpec((1, Dc), lambda i: (0, 0))
    wspec = pl.BlockSpec((C_GROUPS, C_CHUNK, C_CHUNK), lambda i: (0, 0, 0))

    def body(p_ref, dy_ref, lng_ref, lnb_ref, ws_ref, wst_ref, bst_ref,
             dp_ref, dws_ref, dbs_ref, dlg_ref, dlb_ref, acc_w, acc_b, acc_g, acc_lb):
        i = pl.program_id(0)
        u, du, dvg, rstd, vhat, vn, mixed, sz, dsz = _sgu_core(p_ref, lng_ref, lnb_ref, ws_ref, bst_ref, Dc)
        dy = dy_ref[...]
        dmixed = dy * u * sz
        dmb = dmixed.astype(BF16)

        @pl.when(i == 0)
        def _():
            acc_w[...] = jnp.zeros(acc_w.shape, F32)
            acc_b[...] = jnp.zeros(acc_b.shape, F32)
            acc_g[...] = jnp.zeros(acc_g.shape, F32)
            acc_lb[...] = jnp.zeros(acc_lb.shape, F32)

        dvn_parts = []
        for g in range(C_GROUPS):
            dmg = dmb[:, g * gw:(g + 1) * gw]
            acc_w[g] += lax.dot_general(dmg, vn[:, g * gw:(g + 1) * gw], NT, preferred_element_type=F32)
            acc_b[g] += dmixed[:, g * gw:(g + 1) * gw]
            dvn_parts.append(jnp.dot(wst_ref[g].astype(BF16), dmg, preferred_element_type=F32))
        dvn = jnp.concatenate(dvn_parts, axis=1)
        acc_g[...] += jnp.sum((dvn * vhat).reshape(C_CHUNK // 8, 8, Dc), axis=0)
        acc_lb[...] += jnp.sum(dvn.reshape(C_CHUNK // 8, 8, Dc), axis=0)
        dvh = dvn * lng_ref[...]
        dvgelu = rstd * (dvh - jnp.mean(dvh, axis=-1, keepdims=True)
                         - vhat * jnp.mean(dvh * vhat, axis=-1, keepdims=True))
        dp_ref[:, :Dc] = (dy * mixed * sz * du).astype(BF16)
        dp_ref[:, Dc:2 * Dc] = (dvgelu * dvg).astype(BF16)
        dp_ref[:, 2 * Dc:] = (dy * u * mixed * dsz).astype(BF16)

        @pl.when(i == nsteps - 1)
        def _():
            dws_ref[...] = acc_w[...]
            for g in range(C_GROUPS):
                dbs_ref[g] = jnp.sum(acc_b[g], axis=1, keepdims=True)
            dlg_ref[...] = jnp.sum(acc_g[...], axis=0, keepdims=True)
            dlb_ref[...] = jnp.sum(acc_lb[...], axis=0, keepdims=True)

    v = jax.ShapeDtypeStruct((1, Dc), F32)
    return pl.pallas_call(
        body, name=name, grid=(nsteps,),
        in_specs=[pl.BlockSpec((C_CHUNK, Dc3), lambda i: (i, 0)), pl.BlockSpec((C_CHUNK, Dc), lambda i: (i, 0)),
                  vec, vec, wspec, wspec, pl.BlockSpec((C_CHUNK, C_GROUPS), lambda i: (0, 0))],
        out_specs=[pl.BlockSpec((C_CHUNK, Dc3), lambda i: (i, 0)), wspec,
                   pl.BlockSpec((C_GROUPS, C_CHUNK, 1), lambda i: (0, 0, 0)), vec, vec],
        out_shape=[jax.ShapeDtypeStruct((S, Dc3), BF16),
                   jax.ShapeDtypeStruct((C_GROUPS, C_CHUNK, C_CHUNK), F32),
                   jax.ShapeDtypeStruct((C_GROUPS, C_CHUNK, 1), F32), v, v],
        scratch_shapes=[pltpu.VMEM((C_GROUPS, C_CHUNK, C_CHUNK), F32), pltpu.VMEM((C_GROUPS, C_CHUNK, gw), F32),
                        pltpu.VMEM((8, Dc), F32), pltpu.VMEM((8, Dc), F32)],
        compiler_params=_cp("arbitrary"),
    )(proj, dy, ln_g, ln_b, w_s, w_st, b_st)


PACK_COLS = 1024
PACK_ROWS = 64


def _pack(vectors):
    flat = jnp.concatenate([v.reshape(-1) for v in vectors])
    pad = (-flat.shape[0]) % (PACK_COLS * PACK_ROWS)
    return jnp.pad(flat, (0, pad)).reshape(-1, PACK_COLS)


def _unshard(g, off, shape):
    L, rest = shape[0], shape[1:]
    size = int(np.prod(shape))
    piece = g[:, off:off + size].reshape((NDEV,) + tuple(shape))
    nd = piece.ndim
    perm = tuple(range(1, nd - 1)) + (0, nd - 1)
    full = jnp.transpose(piece, perm)
    return full.reshape(tuple(shape[:-1]) + (NDEV * shape[-1],)), off + size


def kernel(x, c, ab_norm_g, ab_w_mod, ab_b_mod, ab_w_in, ab_conv_w, ab_w_out, sg_norm_g, sg_w_mod, sg_b_mod, sg_w_in, sg_ln_g, sg_ln_b, sg_w_s, sg_b_s, sg_w_out, final_norm_g, loss_target, m_ab_norm_g, m_ab_w_mod, m_ab_b_mod, m_ab_w_in, m_ab_conv_w, m_ab_w_out, m_sg_norm_g, m_sg_w_mod, m_sg_b_mod, m_sg_w_in, m_sg_ln_g, m_sg_ln_b, m_sg_w_s, m_sg_b_s, m_sg_w_out, m_final_norm_g, v_ab_norm_g, v_ab_w_mod, v_ab_b_mod, v_ab_w_in, v_ab_conv_w, v_ab_w_out, v_sg_norm_g, v_sg_w_mod, v_sg_b_mod, v_sg_w_in, v_sg_ln_g, v_sg_ln_b, v_sg_w_s, v_sg_b_s, v_sg_w_out, v_final_norm_g):
    _, S, D = x.shape
    L = ab_norm_g.shape[0]
    W = ab_conv_w.shape[2] * NDEV
    n_ab, n_sg = ab_w_in.shape[2], sg_w_in.shape[2]
    n_mod = ab_w_mod.shape[2]
    kb = ab_w_out.shape[1]
    xi, yi, ci = _position()
    dev = 4 * xi + 2 * yi + ci
    x2, tgt = x.reshape(S, D), loss_target.reshape(S, D)

    small = [c, ab_conv_w, sg_norm_g, sg_ln_g, sg_ln_b]
    (g1,) = _comm_only(_Gather([_pack(small)]), "ag_small")
    g1 = g1.reshape(NDEV, -1)
    c_all = g1[:, :D]
    off = D
    conv_full, off = _unshard(g1, off, ab_conv_w.shape)
    sg_norm_full, off = _unshard(g1, off, sg_norm_g.shape)
    ln_g_full, off = _unshard(g1, off, sg_ln_g.shape)
    ln_b_full, off = _unshard(g1, off, sg_ln_b.shape)

    ab_b_cols = lax.dynamic_slice_in_dim(ab_b_mod, dev * n_mod, n_mod, axis=1)
    m_ab = _mod_fwd(c_all, ab_w_mod, ab_b_cols.reshape(L, 1, n_mod), "mod_fwd_ab")
    m_sg = _mod_fwd(c_all, sg_w_mod, sg_b_mod.reshape(L, 1, n_mod), "mod_fwd_sg")
    m_part = jnp.stack([m_ab, m_sg]).transpose(2, 0, 1, 3).reshape(NDEV, 2 * L * n_mod)
    (g2,) = _comm_only(_Gather([m_part]), "ag_mod")
    mine = lax.dynamic_index_in_dim(g2, dev, axis=1, keepdims=False)
    mods = mine.reshape(NDEV, 2, L, n_mod).transpose(1, 2, 0, 3).reshape(2, L, 3 * D)

    def mod_of(kind, i):
        m = mods[kind, i]
        return m[:D].reshape(1, D), m[D:2 * D].reshape(1, D), m[2 * D:].reshape(1, D)

    big_w = [[(ab_w_in, m_ab_w_in, v_ab_w_in), (ab_w_out, m_ab_w_out, v_ab_w_out)],
             [(sg_w_in, m_sg_w_in, v_sg_w_in), (sg_w_out, m_sg_w_out, v_sg_w_out)]]
    big_names = [["ab_w_in", "ab_w_out"], ["sg_w_in", "sg_w_out"]]
    n_layers = 2 * L
    shards = [[big_w[layer % 2][k][0][layer // 2].astype(BF16) for k in range(2)] for layer in range(n_layers)]
    gathered = {}

    def gather_of(keys):
        keys = [key for key in keys if key[0] < n_layers]
        return keys, (_Gather([shards[layer][k] for layer, k in keys]) if keys else None)

    def keep_gathered(keys, res):
        for (layer, k), g in zip(keys, res):
            gathered[(layer, k)] = g.reshape((NDEV, 1, D, g.shape[-1]) if k == 0 else (NDEV, 1, kb, D))

    keys, comm = gather_of([(0, 0)])
    keep_gathered(keys, _comm_only(comm, "ag_w_in_layer0"))

    cosf, sinf = _rope_tables(S)
    T = min(S, ATTN_TILE)
    bias = _attn_bias(T)
    norm_g = [ab_norm_g, sg_norm_full]
    w_s_t = jnp.swapaxes(sg_w_s, -1, -2)
    b_s_t = jnp.swapaxes(sg_b_s, -1, -2)

    saved = []
    x_cur, res, gate_prev = x2, None, None
    for layer in range(2 * L):
        kind, i = layer % 2, layer // 2
        tag = f"{'ab' if kind == 0 else 'sg'}{i}"
        shift, scale, gate = mod_of(kind, i)
        g = norm_g[kind][i].reshape(1, D)
        xl, h = _pre(x_cur, res, gate_prev, g, scale, shift, f"pre_{tag}")
        if kind == 0:
            keys, comm = gather_of(([(0, 1)] if layer == 0 else []) + [(layer + 1, 0)])
            if comm is None:
                proj = _mm_nn_in(h, gathered[(layer, 0)], 0, f"proj_{tag}")
            else:
                proj, got = _mm_nn_in(h, gathered[(layer, 0)], 0, f"proj_{tag}", comm)
                keep_gathered(keys, got)
        else:
            proj = _mm_nn_in(h, gathered[(layer, 0)], 0, f"proj_{tag}")
        rec = dict(xl=xl, h=h, proj=proj, g=g, scale=scale, gate=gate)
        if kind == 0:
            qr, kr, vb = _rope_qkv(proj, cosf, sinf, W, f"rope_{tag}")
            keys, comm = gather_of([(layer + 1, 1), (layer + 2, 0), (layer + 2, 1)])
            (attn, lse), got = _attn_fwd(qr, kr, vb, bias, f"attn_{tag}", comm)
            keep_gathered(keys, got)
            y = _ab_mix(attn, proj, conv_full[i], W, f"mix_{tag}")
            rec.update(qr=qr, kr=kr, vb=vb, attn=attn, lse=lse)
        else:
            y = _sgu_fwd(proj, ln_g_full[i].reshape(1, D), ln_b_full[i].reshape(1, D), sg_w_s[i], b_s_t[i],
                         f"sgu_{tag}")
        out = _mm_nn_out(y, gathered[(layer, 1)], 0, f"out_{tag}")
        rec.update(y=y, out=out)
        saved.append(rec)
        x_cur, res, gate_prev = xl, out, gate

    dx, loss_part, d_final_g = _loss_head(x_cur, res, gate_prev, final_norm_g.reshape(1, D), tgt, "loss_head")
    loss = lax.psum(loss_part[0, 0], ("x", "y", "c"))

    c_idx = ci.reshape(1).astype(jnp.int32)
    big_res = {}
    pending = None

    def finish_layer(done, from_chips):
        for k in range(2):
            nm = big_names[done % 2][k]
            w, m, v = big_w[done % 2][k]
            flat = lambda a: a.reshape(L * a.shape[1], a.shape[2])
            big_res[nm] = _sum_adam(from_chips[k], flat(w), flat(m), flat(v), done // 2, big_res.get(nm),
                                    f"adam_{nm}{done // 2}")

    dm = [[None] * L, [None] * L]
    d_norm = [[None] * L, [None] * L]
    d_conv, d_lng, d_lnb, d_ws, d_bs = [None] * L, [None] * L, [None] * L, [None] * L, [None] * L
    for layer in reversed(range(2 * L)):
        kind, i = layer % 2, layer // 2
        tag = f"{'ab' if kind == 0 else 'sg'}{i}"
        rec = saved[layer]
        w_in_l, w_out_l = gathered[(layer, 0)], gathered[(layer, 1)]
        dout, dgate = _post_bwd(dx, rec["out"], rec["gate"], f"post_bwd_{tag}")
        dy = _mm_nt_out(dout, w_out_l, 0, f"dy_{tag}")
        dwo = _mm_tn_out(rec["y"], dout, 0, 1, None, f"dwout_{tag}")
        if kind == 0:
            do, delta = _dattn_prep(dy, rec["proj"], rec["attn"], W, f"dattn_{tag}")
            comm = _ToChips(pending[1]) if pending else None
            (dqr, dkr, dvv), got = _attn_bwd(rec["qr"], rec["kr"], rec["vb"], do, rec["lse"], delta, bias,
                                             f"attn_bwd_{tag}", comm)
            if pending:
                finish_layer(pending[0], got)
            dproj, dcw = _ab_bwd(dy, rec["attn"], rec["proj"], dqr, dkr, dvv, cosf, sinf, conv_full[i], W,
                                 f"mix_bwd_{tag}")
            d_conv[i] = dcw[:3]
            dh = _mm_nt_in(dproj, w_in_l, 0, f"dh_{tag}")
            dwi = _mm_tn_in(rec["h"], dproj, 0, 1, None, f"dwin_{tag}")
        else:
            dproj, d_ws[i], dbs, d_lng[i], d_lnb[i] = _sgu_bwd(
                rec["proj"], dy, ln_g_full[i].reshape(1, D), ln_b_full[i].reshape(1, D),
                sg_w_s[i], w_s_t[i], b_s_t[i], f"sgu_bwd_{tag}")
            d_bs[i] = dbs.reshape(C_GROUPS, C_CHUNK)
            if pending:
                dh, got_in = _mm_nt_in(dproj, w_in_l, 0, f"dh_{tag}", _ToChips(pending[1][:1]))
                dwi, got_out = _mm_tn_in(rec["h"], dproj, 0, 1, None, f"dwin_{tag}", _ToChips(pending[1][1:]))
                finish_layer(pending[0], [got_in[0], got_out[0]])
            else:
                dh = _mm_nt_in(dproj, w_in_l, 0, f"dh_{tag}")
                dwi = _mm_tn_in(rec["h"], dproj, 0, 1, None, f"dwin_{tag}")
        grads = [dwi.reshape(NDEV, D, -1), dwo.reshape(NDEV, kb, D)]
        (dx, dshift, dscale, d_norm[kind][i]), from_sibling = _pre_bwd(
            rec["xl"], dh, dx, rec["g"], rec["scale"], f"pre_bwd_{tag}", _ToSibling(grads))
        dm[kind][i] = jnp.concatenate([dshift, dscale, dgate], axis=1).reshape(3 * D)
        pending = (layer, [_add_sibling(g, r, c_idx, f"rs_add_{big_names[kind][k]}{i}")
                           for k, (g, r) in enumerate(zip(grads, from_sibling))])
    grad_x = dx.reshape(1, S, D)
    finish_layer(pending[0], _comm_only(_ToChips(pending[1]), "rs_chips_last"))
    for kind in range(2):
        for k in range(2):
            nm = big_names[kind][k]
            big_res[nm] = [o.reshape(big_w[kind][k][0].shape) for o in big_res[nm]]

    stack = lambda xs: jnp.stack(xs)
    pack_items = [stack(dm[0]), stack(dm[1]), stack(d_norm[0]).reshape(L, D), stack(d_conv),
                  stack(d_norm[1]).reshape(L, D), stack(d_lng).reshape(L, D), stack(d_lnb).reshape(L, D),
                  stack(d_ws), stack(d_bs), d_final_g]
    (g3,) = _comm_only(_Gather([_pack(pack_items)]), "ag_grads")
    P = g3.shape[1] * g3.shape[2]
    tot = _sum_rows(g3, "sum_small").reshape(P)
    g3 = g3.reshape(NDEV, P)
    sizes = [int(np.prod(p.shape)) for p in pack_items]
    offs = np.concatenate([[0], np.cumsum(sizes)]).tolist()
    seg = lambda k, shape: tot[offs[k]:offs[k + 1]].reshape(shape)

    def shard(full, n):
        return lax.dynamic_slice_in_dim(full, dev * n, n, axis=full.ndim - 1)

    g_ab_b_mod = seg(0, (L, 3 * D))
    g_sg_b_mod = shard(seg(1, (L, 3 * D)), n_mod)
    g_ab_norm = seg(2, (L, D))
    g_conv = shard(seg(3, (L, 3, W)), W // NDEV)
    g_sg_norm = shard(seg(4, (L, D)), kb)
    g_ln_g = shard(seg(5, (L, D)), kb)
    g_ln_b = shard(seg(6, (L, D)), kb)
    g_w_s = seg(7, sg_w_s.shape)
    g_b_s = seg(8, sg_b_s.shape)
    g_final = seg(9, (D,))

    small_w = [("ab_norm_g", g_ab_norm, ab_norm_g, m_ab_norm_g, v_ab_norm_g),
               ("ab_b_mod", g_ab_b_mod, ab_b_mod, m_ab_b_mod, v_ab_b_mod),
               ("ab_conv_w", g_conv, ab_conv_w, m_ab_conv_w, v_ab_conv_w),
               ("sg_norm_g", g_sg_norm, sg_norm_g, m_sg_norm_g, v_sg_norm_g),
               ("sg_b_mod", g_sg_b_mod, sg_b_mod, m_sg_b_mod, v_sg_b_mod),
               ("sg_ln_g", g_ln_g, sg_ln_g, m_sg_ln_g, v_sg_ln_g),
               ("sg_ln_b", g_ln_b, sg_ln_b, m_sg_ln_b, v_sg_ln_b),
               ("sg_w_s", g_w_s, sg_w_s, m_sg_w_s, v_sg_w_s),
               ("sg_b_s", g_b_s, sg_b_s, m_sg_b_s, v_sg_b_s),
               ("final_norm_g", g_final, final_norm_g, m_final_norm_g, v_final_norm_g)]
    packed = [_pack([t[k] for t in small_w]) for k in (1, 2, 3, 4)]
    upd = _adam_only(*packed, "adam_small")
    small_res = {}
    o = 0
    for nm, g, w, _, _ in small_w:
        size = int(np.prod(w.shape))
        small_res[nm] = [g] + [u.reshape(-1)[o:o + size].reshape(w.shape) for u in upd]
        o += size

    KP = 128
    sc_t = jnp.pad((c_all * jax.nn.sigmoid(c_all)).T, ((0, 0), (0, KP - NDEV)))
    mod_res = {}
    for kind, nm, (w, m, v) in ((0, "ab_w_mod", (ab_w_mod, m_ab_w_mod, v_ab_w_mod)),
                                (1, "sg_w_mod", (sg_w_mod, m_sg_w_mod, v_sg_w_mod))):
        dm_all = g3[:, offs[kind]:offs[kind + 1]].reshape(NDEV, L, 3 * D)
        cols = jnp.pad(shard(dm_all, n_mod).transpose(1, 0, 2), ((0, 0), (0, KP - NDEV), (0, 0)))
        mod_res[nm] = _wmod_grad_adam(sc_t, cols, w, m, v, f"adam_{nm}")

    order = ["ab_norm_g", "ab_w_mod", "ab_b_mod", "ab_w_in", "ab_conv_w", "ab_w_out", "sg_norm_g", "sg_w_mod",
             "sg_b_mod", "sg_w_in", "sg_ln_g", "sg_ln_b", "sg_w_s", "sg_b_s", "sg_w_out", "final_norm_g"]
    res = {**big_res, **small_res, **mod_res}
    outs = [loss, grad_x]
    for k in range(4):
        outs += [res[nm][k] for nm in order]
    return tuple(outs)
```

```python
import functools
import math

import numpy as np
import jax
import jax.numpy as jnp
from jax import lax
from jax.experimental import pallas as pl
from jax.experimental.pallas import tpu as pltpu

F32 = jnp.float32
BF16 = jnp.bfloat16

NDEV = 8
NCHIP = 4
EPS = 1e-6
HEAD_DIM = 128
ROPE_THETA = 10000.0
DILATED_PATTERNS = ((128, 1), (512, 4), (2048, 16))
NEG_INF = -1e30
C_CHUNK = 128
C_GROUPS = 8
ADAM_LR = 0.001
ADAM_B1 = 0.9
ADAM_B2 = 0.999
ADAM_EPS = 1e-08
ADAM_WD = 0.01
ADAM_STEP = 10
GELU_K = math.sqrt(2.0 / math.pi)
GELU_C = 0.044715

VMEM_LIMIT_BYTES = 56 * 1024 * 1024
ATTN_TILE = 512
HEADS_PER_STEP = 4
ATTN_ROW_CHUNK = 256
LANES = 128
ROW_TILE = 256
MESH = pl.DeviceIdType.MESH
ANY = pl.BlockSpec(memory_space=pl.ANY)


def _cp(*sem):
    return pltpu.CompilerParams(dimension_semantics=sem, vmem_limit_bytes=VMEM_LIMIT_BYTES)


def _sigmoid(z):
    return 1.0 / (1.0 + jnp.exp(-z))


def _silu_and_grad(z):
    s = _sigmoid(z)
    return z * s, s * (1.0 + z * (1.0 - s))


def _gelu_and_grad(x):
    x2 = x * x
    t = jnp.tanh(GELU_K * (x + GELU_C * x2 * x))
    g = 0.5 * x * (1.0 + t)
    dg = 0.5 * (1.0 + t) + 0.5 * x * (1.0 - t * t) * (GELU_K * (1.0 + 3.0 * GELU_C * x2))
    return g, dg


def _position():
    return lax.axis_index("x"), lax.axis_index("y"), lax.axis_index("c")


def _chips(x, y):
    return [(1 - x, y), (x, 1 - y), (1 - x, 1 - y)]


class _Gather:
    def __init__(self, arrs):
        n = len(arrs)
        self.arrs = list(arrs)
        self.out_shape = [jax.ShapeDtypeStruct((NDEV,) + a.shape, a.dtype) for a in arrs]
        self.scratch = [pltpu.SemaphoreType.DMA((n, 7)), pltpu.SemaphoreType.DMA((n, 7)),
                        pltpu.SemaphoreType.DMA((n,))]

    def _copies(self, ins, outs, sems):
        send_sems, recv_sems, local_sems = sems
        x, y, c = _position()

        def copy(a, k, block, to, src=None):
            dst = outs[a].at[4 * block[0] + 2 * block[1] + block[2]]
            return pltpu.make_async_remote_copy(
                src_ref=dst if src is None else src, dst_ref=dst,
                send_sem=send_sems.at[a, k], recv_sem=recv_sems.at[a, k],
                device_id=to, device_id_type=MESH)

        n = len(ins)
        me, sibling = (x, y, c), (x, y, 1 - c)
        mine = [pltpu.make_async_copy(ins[a], outs[a].at[4 * x + 2 * y + c], local_sems.at[a]) for a in range(n)]
        first = []
        for a in range(n):
            first.append(copy(a, 0, me, sibling, src=ins[a]))
            first += [copy(a, 1 + j, me, (*chip, c), src=ins[a]) for j, chip in enumerate(_chips(x, y))]
        return copy, mine, first

    def start(self, ins, outs, sems):
        _, mine, first = self._copies(ins, outs, sems)
        for cp in mine + first:
            cp.start()

    def finish(self, ins, outs, sems):
        copy, mine, first = self._copies(ins, outs, sems)
        x, y, c = _position()
        me, sibling = (x, y, c), (x, y, 1 - c)
        passed = []
        for j, chip in enumerate(_chips(x, y)):
            for a in range(len(ins)):
                copy(a, 1 + j, (*chip, c), me).wait_recv()
                fwd = copy(a, 4 + j, (*chip, c), sibling)
                fwd.start()
                passed.append(fwd)
        for a in range(len(ins)):
            copy(a, 0, sibling, me).wait_recv()
            for j, chip in enumerate(_chips(x, y)):
                copy(a, 4 + j, (*chip, 1 - c), me).wait_recv()
        for cp in first + passed:
            cp.wait_send()
        for cp in mine:
            cp.wait()


class _ToSibling:
    def __init__(self, gs):
        n = len(gs)
        self.arrs = list(gs)
        self.out_shape = [jax.ShapeDtypeStruct((NCHIP,) + g.shape[1:], g.dtype) for g in gs]
        self.scratch = [pltpu.SemaphoreType.DMA((n, NCHIP)), pltpu.SemaphoreType.DMA((n, NCHIP))]

    def _copies(self, ins, outs, sems):
        send_sems, recv_sems = sems
        x, y, c = _position()
        return [pltpu.make_async_remote_copy(
            src_ref=ins[a].at[2 * k + (1 - c)], dst_ref=outs[a].at[k],
            send_sem=send_sems.at[a, k], recv_sem=recv_sems.at[a, k],
            device_id=(x, y, 1 - c), device_id_type=MESH) for a in range(len(ins)) for k in range(NCHIP)]

    def start(self, ins, outs, sems):
        for cp in self._copies(ins, outs, sems):
            cp.start()

    def finish(self, ins, outs, sems):
        copies = self._copies(ins, outs, sems)
        for cp in copies:
            cp.wait_recv()
        for cp in copies:
            cp.wait_send()


class _ToChips:
    def __init__(self, ps):
        n = len(ps)
        self.arrs = list(ps)
        self.out_shape = [jax.ShapeDtypeStruct(p.shape, p.dtype) for p in ps]
        self.scratch = [pltpu.SemaphoreType.DMA((n, 3)), pltpu.SemaphoreType.DMA((n, 3)),
                        pltpu.SemaphoreType.DMA((n,))]

    def _copies(self, ins, outs, sems, arrivals):
        send_sems, recv_sems, local_sems = sems
        x, y, c = _position()
        mychip = 2 * x + y
        n = len(ins)
        mine = [pltpu.make_async_copy(ins[a].at[mychip], outs[a].at[mychip], local_sems.at[a]) for a in range(n)]
        sends, recvs = [], []
        for a in range(n):
            for j, chip in enumerate(_chips(x, y)):
                sends.append(pltpu.make_async_remote_copy(
                    src_ref=ins[a].at[2 * chip[0] + chip[1]], dst_ref=outs[a].at[mychip],
                    send_sem=send_sems.at[a, j], recv_sem=recv_sems.at[a, j],
                    device_id=(*chip, c), device_id_type=MESH))
                if arrivals:
                    slot = outs[a].at[2 * chip[0] + chip[1]]
                    recvs.append(pltpu.make_async_remote_copy(
                        src_ref=slot, dst_ref=slot, send_sem=send_sems.at[a, j], recv_sem=recv_sems.at[a, j],
                        device_id=(*chip, c), device_id_type=MESH))
        return mine, sends, recvs

    def start(self, ins, outs, sems):
        mine, sends, _ = self._copies(ins, outs, sems, False)
        for cp in mine + sends:
            cp.start()

    def finish(self, ins, outs, sems):
        mine, sends, recvs = self._copies(ins, outs, sems, True)
        for cp in recvs:
            cp.wait_recv()
        for cp in sends:
            cp.wait_send()
        for cp in mine:
            cp.wait()


def _comm_only(comm, name):
    n_in, n_out = len(comm.arrs), len(comm.out_shape)

    def body(*refs):
        ins, outs, sems = refs[:n_in], refs[n_in:n_in + n_out], refs[n_in + n_out:]
        comm.start(ins, outs, sems)
        comm.finish(ins, outs, sems)

    return pl.pallas_call(
        body, name=name, out_shape=comm.out_shape, in_specs=[ANY] * n_in, out_specs=[ANY] * n_out,
        scratch_shapes=comm.scratch,
    )(*comm.arrs)


def _hosted_call(body, operands, *, name, grid, in_specs, out_specs, out_shape, scratch_shapes=(), sem=(),
                 aliases=None, comm=None):
    single = not isinstance(out_shape, (list, tuple))
    o_specs = [out_specs] if single else list(out_specs)
    o_shape = [out_shape] if single else list(out_shape)
    n_in, n_out, n_scr = len(in_specs), len(o_shape), len(scratch_shapes)
    if comm is None:
        res = pl.pallas_call(body, name=name, grid=grid, in_specs=list(in_specs), out_specs=o_specs,
                             out_shape=o_shape, scratch_shapes=list(scratch_shapes),
                             input_output_aliases=aliases or {}, compiler_params=_cp(*sem))(*operands)
        return (res[0] if single else res), []
    c_in, c_out = len(comm.arrs), len(comm.out_shape)

    def wrapped(*refs):
        ins, cins = refs[:n_in], refs[n_in:n_in + c_in]
        o0 = n_in + c_in
        outs, couts = refs[o0:o0 + n_out], refs[o0 + n_out:o0 + n_out + c_out]
        s0 = o0 + n_out + c_out
        scr, csems = refs[s0:s0 + n_scr], refs[s0 + n_scr:]
        pids = [pl.program_id(a) for a in range(len(grid))]
        first = functools.reduce(jnp.logical_and, [p == 0 for p in pids])
        last = functools.reduce(jnp.logical_and, [p == g - 1 for p, g in zip(pids, grid)])

        @pl.when(first)
        def _():
            comm.start(cins, couts, csems)

        body(*ins, *outs, *scr)

        @pl.when(last)
        def _():
            comm.finish(cins, couts, csems)

    res = pl.pallas_call(
        wrapped, name=name, grid=grid, in_specs=list(in_specs) + [ANY] * c_in, out_specs=o_specs + [ANY] * c_out,
        out_shape=o_shape + comm.out_shape, scratch_shapes=list(scratch_shapes) + comm.scratch,
        input_output_aliases=aliases or {}, compiler_params=_cp(*(["arbitrary"] * len(grid))),
    )(*operands, *comm.arrs)
    return (res[0] if single else res[:n_out]), res[n_out:]


def _adamw(w, g, m, v):
    m2 = ADAM_B1 * m + (1.0 - ADAM_B1) * g
    v2 = ADAM_B2 * v + (1.0 - ADAM_B2) * (g * g)
    m_hat = m2 / (1.0 - ADAM_B1 ** ADAM_STEP)
    v_hat = v2 / (1.0 - ADAM_B2 ** ADAM_STEP)
    delta = -ADAM_LR * (m_hat / (jnp.sqrt(v_hat) + ADAM_EPS) + ADAM_WD * w)
    return delta, m2, v2


def _add_sibling(g, recv, c_idx, name):
    _, R, C = g.shape
    tr = min(R, 512)

    def body(c_ref, g_ref, r_ref, o_ref):
        o_ref[...] = (g_ref[...] + r_ref[...]).astype(BF16)

    return pl.pallas_call(
        body, name=name,
        grid_spec=pltpu.PrefetchScalarGridSpec(
            num_scalar_prefetch=1, grid=(NCHIP, R // tr),
            in_specs=[pl.BlockSpec((1, tr, C), lambda k, i, c_ref: (2 * k + c_ref[0], i, 0)),
                      pl.BlockSpec((1, tr, C), lambda k, i, c_ref: (k, i, 0))],
            out_specs=pl.BlockSpec((1, tr, C), lambda k, i, c_ref: (k, i, 0))),
        out_shape=jax.ShapeDtypeStruct((NCHIP, R, C), BF16),
        compiler_params=_cp("parallel", "parallel"),
    )(c_idx, g, recv)


def _sum_adam(parts, w, m, v, l, prev, name):
    K, R, C = parts.shape
    LR = w.shape[0]
    tr = min(R, 256)
    nb = R // tr

    def body(p_ref, w_ref, m_ref, v_ref, *rest):
        g_ref, d_ref, m2_ref, v2_ref = rest[-4:]
        g = p_ref[0].astype(F32)
        for k in range(1, K):
            g = g + p_ref[k].astype(F32)
        delta, m2, v2 = _adamw(w_ref[...], g, m_ref[...], v_ref[...])
        g_ref[...] = g
        d_ref[...] = delta
        m2_ref[...] = m2
        v2_ref[...] = v2

    blk = pl.BlockSpec((tr, C), lambda i: (l * nb + i, 0))
    shp = jax.ShapeDtypeStruct((LR, C), F32)
    operands = [parts, w, m, v] + (list(prev) if prev is not None else [])
    return pl.pallas_call(
        body, name=name, grid=(nb,),
        in_specs=[pl.BlockSpec((K, tr, C), lambda i: (0, i, 0)), blk, blk, blk] + [ANY] * (len(operands) - 4),
        out_specs=[blk] * 4, out_shape=[shp] * 4,
        input_output_aliases={4 + k: k for k in range(len(operands) - 4)},
        compiler_params=_cp("parallel"),
    )(*operands)


def _sum_rows(parts, name):
    K, R, C = parts.shape
    tr = min(R, 256)
    while R % tr:
        tr //= 2

    def body(p_ref, o_ref):
        g = p_ref[0]
        for k in range(1, K):
            g = g + p_ref[k]
        o_ref[...] = g

    return pl.pallas_call(
        body, name=name, grid=(R // tr,),
        in_specs=[pl.BlockSpec((K, tr, C), lambda i: (0, i, 0))],
        out_specs=pl.BlockSpec((tr, C), lambda i: (i, 0)),
        out_shape=jax.ShapeDtypeStruct((R, C), F32),
        compiler_params=_cp("parallel"),
    )(parts)


def _adam_only(g, w, m, v, name):
    R, C = g.shape
    tr = min(R, 256)
    while R % tr:
        tr //= 2

    def body(g_ref, w_ref, m_ref, v_ref, d_ref, m2_ref, v2_ref):
        delta, m2, v2 = _adamw(w_ref[...], g_ref[...], m_ref[...], v_ref[...])
        d_ref[...] = delta
        m2_ref[...] = m2
        v2_ref[...] = v2

    blk = pl.BlockSpec((tr, C), lambda i: (i, 0))
    shp = jax.ShapeDtypeStruct((R, C), F32)
    return pl.pallas_call(
        body, name=name, grid=(R // tr,), in_specs=[blk] * 4, out_specs=[blk] * 3,
        out_shape=[shp] * 3, compiler_params=_cp("parallel"),
    )(g, w, m, v)


def _mod_fwd(c_all, w_mod, b_cols, name):
    L, D, n = w_mod.shape
    B = c_all.shape[0]

    def body(c_ref, w_ref, b_ref, o_ref):
        cv = c_ref[...]
        sc = (cv * _sigmoid(cv)).astype(BF16)
        o_ref[0] = jnp.dot(sc, w_ref[0].astype(BF16), preferred_element_type=F32) + b_ref[0]

    return pl.pallas_call(
        body, name=name, grid=(L,),
        in_specs=[pl.BlockSpec((B, D), lambda l: (0, 0)),
                  pl.BlockSpec((1, D, n), lambda l: (l, 0, 0)),
                  pl.BlockSpec((1, 1, n), lambda l: (l, 0, 0))],
        out_specs=pl.BlockSpec((1, B, n), lambda l: (l, 0, 0)),
        out_shape=jax.ShapeDtypeStruct((L, B, n), F32),
        compiler_params=_cp("parallel"),
    )(c_all, w_mod, b_cols)


def _wmod_grad_adam(sc_t, dm, w, m, v, name):
    L, D, n = w.shape
    KP = sc_t.shape[1]
    tr = min(D, 512)

    def body(s_ref, dm_ref, w_ref, m_ref, v_ref, g_ref, d_ref, m2_ref, v2_ref):
        g = jnp.dot(s_ref[...], dm_ref[0], preferred_element_type=F32,
                    precision=lax.Precision.HIGHEST)
        delta, m2, v2 = _adamw(w_ref[0], g, m_ref[0], v_ref[0])
        g_ref[0] = g
        d_ref[0] = delta
        m2_ref[0] = m2
        v2_ref[0] = v2

    blk = pl.BlockSpec((1, tr, n), lambda l, i: (l, i, 0))
    shp = jax.ShapeDtypeStruct((L, D, n), F32)
    return pl.pallas_call(
        body, name=name, grid=(L, D // tr),
        in_specs=[pl.BlockSpec((tr, KP), lambda l, i: (i, 0)),
                  pl.BlockSpec((1, KP, n), lambda l, i: (l, 0, 0)), blk, blk, blk],
        out_specs=[blk] * 4, out_shape=[shp] * 4,
        compiler_params=_cp("parallel", "parallel"),
    )(sc_t, dm, w, m, v)


def _vec_spec(D):
    return pl.BlockSpec((1, D), lambda i: (0, 0))


def _pre(x, res, gate, g, scale, shift, name):
    S, D = x.shape
    tr = min(S, ROW_TILE)
    has_res = res is not None
    row = pl.BlockSpec((tr, D), lambda i: (i, 0))

    def body(*refs):
        if has_res:
            x_ref, r_ref, gate_ref, g_ref, sc_ref, sh_ref, xl_ref, h_ref = refs
            xv = x_ref[...] + gate_ref[...] * r_ref[...]
            xl_ref[...] = xv
        else:
            x_ref, g_ref, sc_ref, sh_ref, h_ref = refs
            xv = x_ref[...]
        r = lax.rsqrt(jnp.mean(xv * xv, axis=-1, keepdims=True) + EPS)
        y = (xv * r) * g_ref[...]
        h_ref[...] = (y * (1.0 + sc_ref[...]) + sh_ref[...]).astype(BF16)

    vec = _vec_spec(D)
    if has_res:
        xl, h = pl.pallas_call(
            body, name=name, grid=(S // tr,),
            in_specs=[row, row, vec, vec, vec, vec], out_specs=[row, row],
            out_shape=[jax.ShapeDtypeStruct((S, D), F32), jax.ShapeDtypeStruct((S, D), BF16)],
            compiler_params=_cp("parallel"),
        )(x, res, gate, g, scale, shift)
        return xl, h
    h = pl.pallas_call(
        body, name=name, grid=(S // tr,),
        in_specs=[row, vec, vec, vec], out_specs=row,
        out_shape=jax.ShapeDtypeStruct((S, D), BF16),
        compiler_params=_cp("parallel"),
    )(x, g, scale, shift)
    return x, h


def _pre_bwd(xl, dh, dx_in, g, scale, name, comm=None):
    S, D = xl.shape
    tr = min(S, ROW_TILE)
    nsteps = S // tr
    row = pl.BlockSpec((tr, D), lambda i: (i, 0))
    vec = _vec_spec(D)

    def body(x_ref, dh_ref, dxin_ref, g_ref, sc_ref, dx_ref, dsh_ref, dsc_ref, dg_ref, acc_sh, acc_t):
        i = pl.program_id(0)
        xv = x_ref[...]
        dh = dh_ref[...]
        r = lax.rsqrt(jnp.mean(xv * xv, axis=-1, keepdims=True) + EPS)
        xn = xv * r
        part_sh = jnp.sum(dh.reshape(tr // 8, 8, D), axis=0)
        part_t = jnp.sum((dh * xn).reshape(tr // 8, 8, D), axis=0)

        @pl.when(i == 0)
        def _():
            acc_sh[...] = part_sh
            acc_t[...] = part_t

        @pl.when(i > 0)
        def _():
            acc_sh[...] += part_sh
            acc_t[...] += part_t

        dxn = dh * (g_ref[...] * (1.0 + sc_ref[...]))
        dx_ref[...] = dxin_ref[...] + r * (dxn - xn * jnp.mean(dxn * xn, axis=-1, keepdims=True))

        @pl.when(i == nsteps - 1)
        def _():
            t = jnp.sum(acc_t[...], axis=0, keepdims=True)
            dsh_ref[...] = jnp.sum(acc_sh[...], axis=0, keepdims=True)
            dsc_ref[...] = t * g_ref[...]
            dg_ref[...] = t * (1.0 + sc_ref[...])

    v = jax.ShapeDtypeStruct((1, D), F32)
    return _hosted_call(
        body, [xl, dh, dx_in, g, scale], name=name, grid=(nsteps,),
        in_specs=[row, row, row, vec, vec], out_specs=[row, vec, vec, vec],
        out_shape=[jax.ShapeDtypeStruct((S, D), F32), v, v, v],
        scratch_shapes=[pltpu.VMEM((8, D), F32), pltpu.VMEM((8, D), F32)],
        sem=("arbitrary",), comm=comm)


def _post_bwd(dx, out, gate, name):
    S, D = dx.shape
    tr = min(S, ROW_TILE)
    nsteps = S // tr
    row = pl.BlockSpec((tr, D), lambda i: (i, 0))
    vec = _vec_spec(D)

    def body(dx_ref, o_ref, gate_ref, do_ref, dg_ref, acc):
        i = pl.program_id(0)
        dxv = dx_ref[...]
        do_ref[...] = (dxv * gate_ref[...]).astype(BF16)
        part = jnp.sum((dxv * o_ref[...]).reshape(tr // 8, 8, D), axis=0)

        @pl.when(i == 0)
        def _():
            acc[...] = part

        @pl.when(i > 0)
        def _():
            acc[...] += part

        @pl.when(i == nsteps - 1)
        def _():
            dg_ref[...] = jnp.sum(acc[...], axis=0, keepdims=True)

    return pl.pallas_call(
        body, name=name, grid=(nsteps,),
        in_specs=[row, row, vec], out_specs=[row, vec],
        out_shape=[jax.ShapeDtypeStruct((S, D), BF16), jax.ShapeDtypeStruct((1, D), F32)],
        scratch_shapes=[pltpu.VMEM((8, D), F32)],
        compiler_params=_cp("arbitrary"),
    )(dx, out, gate)


def _loss_head(x, res, gate, gf, tgt, name):
    S, D = x.shape
    tr = min(S, ROW_TILE)
    nsteps = S // tr
    row = pl.BlockSpec((tr, D), lambda i: (i, 0))
    vec = _vec_spec(D)

    def body(x_ref, r_ref, gate_ref, gf_ref, t_ref, dx_ref, loss_ref, dgf_ref, acc, lacc):
        i = pl.program_id(0)
        xv = x_ref[...] + gate_ref[...] * r_ref[...]
        r = lax.rsqrt(jnp.mean(xv * xv, axis=-1, keepdims=True) + EPS)
        xn = xv * r
        err = xn * gf_ref[...] - t_ref[...]
        row_loss = jnp.mean(err * err, axis=-1, keepdims=True)
        lpart = 0.5 * jnp.sum(row_loss, axis=0, keepdims=True)
        dy = err * (1.0 / D)
        part = jnp.sum((dy * xn).reshape(tr // 8, 8, D), axis=0)

        @pl.when(i == 0)
        def _():
            acc[...] = part
            lacc[...] = lpart

        @pl.when(i > 0)
        def _():
            acc[...] += part
            lacc[...] += lpart

        dxn = dy * gf_ref[...]
        dx_ref[...] = r * (dxn - xn * jnp.mean(dxn * xn, axis=-1, keepdims=True))

        @pl.when(i == nsteps - 1)
        def _():
            dgf_ref[...] = jnp.sum(acc[...], axis=0, keepdims=True)
            loss_ref[...] = lacc[...]

    return pl.pallas_call(
        body, name=name, grid=(nsteps,),
        in_specs=[row, row, vec, vec, row],
        out_specs=[row, pl.BlockSpec((1, 1), lambda i: (0, 0)), vec],
        out_shape=[jax.ShapeDtypeStruct((S, D), F32), jax.ShapeDtypeStruct((1, 1), F32),
                   jax.ShapeDtypeStruct((1, D), F32)],
        scratch_shapes=[pltpu.VMEM((8, D), F32), pltpu.VMEM((1, 1), F32)],
        compiler_params=_cp("arbitrary"),
    )(x, res, gate, gf, tgt)


NN = (((1,), (0,)), ((), ()))
NT = (((1,), (1,)), ((), ()))
TN = (((0,), (0,)), ((), ()))


def _mm(name, a, b, out_shape, grid, a_spec, b_spec, o_spec, dims, a2d, b2d, k_axis, sem, alias=None, comm=None):
    def body(*refs):
        a_ref, b_ref, o_ref = refs[0], refs[1], refs[-1]
        r = lax.dot_general(a_ref[...].reshape(a2d), b_ref[...].reshape(b2d), dims,
                            preferred_element_type=F32)
        r = r.reshape(o_ref.shape)
        if k_axis is None:
            o_ref[...] = r.astype(o_ref.dtype)
        else:
            k = pl.program_id(k_axis)

            @pl.when(k == 0)
            def _():
                o_ref[...] = r

            @pl.when(k > 0)
            def _():
                o_ref[...] += r

    operands, in_specs, aliases = [a, b], [a_spec, b_spec], {}
    if alias is not None:
        operands.append(alias)
        in_specs.append(ANY)
        aliases = {2: 0}
    res, extra = _hosted_call(body, operands, name=name, grid=grid, in_specs=in_specs, out_specs=o_spec,
                              out_shape=out_shape, sem=sem, aliases=aliases, comm=comm)
    return res if comm is None else (res, extra)


def _tile(n, pref):
    t = min(n, pref)
    while n % t:
        t -= 128
    return t


def _mm_nn_in(a, w, l, name, comm=None):
    M, K = a.shape
    _, _, _, n = w.shape
    tm, tn = min(M, 512), _tile(n, 1024)
    nb = n // tn
    return _mm(name, a, w, jax.ShapeDtypeStruct((M, NDEV * n), F32), (NDEV * nb, M // tm),
               pl.BlockSpec((tm, K), lambda j, i: (i, 0)),
               pl.BlockSpec((1, 1, K, tn), lambda j, i: (j // nb, l, 0, j % nb)),
               pl.BlockSpec((tm, tn), lambda j, i: (i, j)),
               NN, (tm, K), (K, tn), None, ("parallel", "parallel"), comm=comm)


def _mm_nn_out(a, w, l, name):
    M, K = a.shape
    _, _, kb, N = w.shape
    tm, tn = min(M, 512), _tile(N, 1024)
    return _mm(name, a, w, jax.ShapeDtypeStruct((M, N), F32), (N // tn, M // tm),
               pl.BlockSpec((tm, K), lambda j, i: (i, 0)),
               pl.BlockSpec((NDEV, 1, kb, tn), lambda j, i: (0, l, 0, j)),
               pl.BlockSpec((tm, tn), lambda j, i: (i, j)),
               NN, (tm, K), (K, tn), None, ("parallel", "parallel"))


def _mm_nt_in(a, w, l, name, comm=None):
    M, _ = a.shape
    _, _, K, n = w.shape
    tm, tk, tc = min(M, 1024), _tile(K, 1024), _tile(n, 1024)
    nb = n // tc
    return _mm(name, a, w, jax.ShapeDtypeStruct((M, K), F32), (M // tm, K // tk, NDEV * nb),
               pl.BlockSpec((tm, tc), lambda i, j, k: (i, k)),
               pl.BlockSpec((1, 1, tk, tc), lambda i, j, k: (k // nb, l, j, k % nb)),
               pl.BlockSpec((tm, tk), lambda i, j, k: (i, j)),
               NT, (tm, tc), (tk, tc), 2, ("parallel", "parallel", "arbitrary"), comm=comm)


def _mm_nt_out(a, w, l, name):
    M, N = a.shape
    _, _, kb, _ = w.shape
    K = NDEV * kb
    tm, tk, tc = min(M, 1024), _tile(K, 1024), _tile(N, 1024)
    per = tk // kb
    return _mm(name, a, w, jax.ShapeDtypeStruct((M, K), F32), (M // tm, K // tk, N // tc),
               pl.BlockSpec((tm, tc), lambda i, j, k: (i, k)),
               pl.BlockSpec((per, 1, kb, tc), lambda i, j, k: (j, l, 0, k)),
               pl.BlockSpec((tm, tk), lambda i, j, k: (i, j)),
               NT, (tm, tc), (tk, tc), 2, ("parallel", "parallel", "arbitrary"))


def _mm_tn_in(a, b, l, L, buf, name, comm=None):
    S, K = a.shape
    n = b.shape[1] // NDEV
    ts, tk, tn = min(S, 1024), _tile(K, 1024), _tile(n, 1024)
    nb = n // tn
    return _mm(name, a, b, jax.ShapeDtypeStruct((NDEV, L, K, n), F32), (NDEV * nb, K // tk, S // ts),
               pl.BlockSpec((ts, tk), lambda j, i, s: (s, i)),
               pl.BlockSpec((ts, tn), lambda j, i, s: (s, j)),
               pl.BlockSpec((1, 1, tk, tn), lambda j, i, s: (j // nb, l, i, j % nb)),
               TN, (ts, tk), (ts, tn), 2, ("parallel", "parallel", "arbitrary"), alias=buf, comm=comm)


def _mm_tn_out(a, b, l, L, buf, name):
    S, K = a.shape
    N = b.shape[1]
    kb = K // NDEV
    ts, tk, tn = min(S, 1024), _tile(K, 1024), _tile(N, 1024)
    per = tk // kb
    return _mm(name, a, b, jax.ShapeDtypeStruct((NDEV, L, kb, N), F32), (N // tn, K // tk, S // ts),
               pl.BlockSpec((ts, tk), lambda j, i, s: (s, i)),
               pl.BlockSpec((ts, tn), lambda j, i, s: (s, j)),
               pl.BlockSpec((per, 1, kb, tn), lambda j, i, s: (i, l, 0, j)),
               TN, (ts, tk), (ts, tn), 2, ("parallel", "parallel", "arbitrary"), alias=buf)


def _attn_bias(T):
    reach = max(w // 2 for w, _ in DILATED_PATTERNS)
    hb = -(-reach // T)
    i = np.arange(T)[:, None]
    j = np.arange(T)[None, :]
    tiles = []
    for d in range(-hb, hb + 1):
        rel = j + d * T - i
        mult = np.zeros((T, T), np.float64)
        for window, dil in DILATED_PATTERNS:
            radius = window // (2 * dil)
            mult += (rel % dil == 0) & (np.abs(rel) <= radius * dil)
        tiles.append(np.where(mult > 0, np.log(np.maximum(mult, 1.0)), NEG_INF))
    return jnp.asarray(np.stack(tiles), F32)


def _rope_tables(S):
    half = HEAD_DIM // 2
    pos = jnp.arange(S, dtype=F32)
    inv = ROPE_THETA ** (-jnp.arange(half, dtype=F32) / half)
    ang = pos[:, None] * inv[None, :]
    cos, sin = jnp.cos(ang), jnp.sin(ang)
    return jnp.concatenate([cos, cos], axis=-1), jnp.concatenate([-sin, sin], axis=-1)


def _rope_apply(t, cosf, sinf, heads, sign):
    outs = []
    for hh in range(heads):
        th = t[:, hh * HEAD_DIM:(hh + 1) * HEAD_DIM]
        outs.append(th * cosf + sign * (pltpu.roll(th, HEAD_DIM // 2, 1) * sinf))
    return outs


def _rope_qkv(proj, cosf, sinf, W, name):
    S = proj.shape[0]
    tr = min(S, ROW_TILE)
    heads = W // HEAD_DIM

    def body(q_ref, k_ref, v_ref, c_ref, s_ref, qo_ref, ko_ref, vo_ref):
        cosf_v, sinf_v = c_ref[...], s_ref[...]
        for src, dst, mult in ((q_ref, qo_ref, HEAD_DIM ** -0.5), (k_ref, ko_ref, 1.0)):
            for hh, val in enumerate(_rope_apply(src[...], cosf_v, sinf_v, heads, 1.0)):
                dst[:, hh * HEAD_DIM:(hh + 1) * HEAD_DIM] = (val * mult).astype(BF16)
        vo_ref[...] = v_ref[...].astype(BF16)

    piece = lambda p: pl.BlockSpec((tr, W), lambda i: (i, p))
    tab = pl.BlockSpec((tr, HEAD_DIM), lambda i: (i, 0))
    out = pl.BlockSpec((tr, W), lambda i: (i, 0))
    shp = jax.ShapeDtypeStruct((S, W), BF16)
    return pl.pallas_call(
        body, name=name, grid=(S // tr,),
        in_specs=[piece(0), piece(1), piece(2), tab, tab], out_specs=[out] * 3, out_shape=[shp] * 3,
        compiler_params=_cp("parallel"),
    )(proj, proj, proj, cosf, sinf)


def _attn_fwd(q, k, v, bias, name, comm=None):
    S, W = q.shape
    H = W // HEAD_DIM
    nd, T, _ = bias.shape
    hb, nq = nd // 2, S // T
    scale = HEAD_DIM ** -0.5
    hp = min(H, HEADS_PER_STEP)
    rc = min(T, ATTN_ROW_CHUNK)
    wp = hp * HEAD_DIM

    def body(q_ref, k_ref, v_ref, b_ref, o_ref, lse_ref, m_s, l_s, acc_s):
        i, d = pl.program_id(1), pl.program_id(2)
        j = i + d - hb

        @pl.when(d == 0)
        def _():
            m_s[...] = jnp.full(m_s.shape, -jnp.inf, F32)
            l_s[...] = jnp.zeros(l_s.shape, F32)
            acc_s[...] = jnp.zeros(acc_s.shape, F32)

        @pl.when((j >= 0) & (j < nq))
        def _():
            items = [(hh, c) for hh in range(hp) for c in range(T // rc)]

            def scores(item):
                hh, c = item
                cols, rows = slice(hh * HEAD_DIM, (hh + 1) * HEAD_DIM), slice(c * rc, (c + 1) * rc)
                return (lax.dot_general(q_ref[rows, cols], k_ref[:, cols], NT, preferred_element_type=F32)
                        + b_ref[d, rows, :])

            def weighted_values(item, p, alpha):
                hh, c = item
                cols, rows = slice(hh * HEAD_DIM, (hh + 1) * HEAD_DIM), slice(c * rc, (c + 1) * rc)
                acc_s[rows, cols] = alpha * acc_s[rows, cols] + jnp.dot(p, v_ref[:, cols],
                                                                        preferred_element_type=F32)

            s_next, pending = scores(items[0]), None
            for n, (hh, c) in enumerate(items):
                rows = slice(c * rc, (c + 1) * rc)
                s = s_next
                if n + 1 < len(items):
                    s_next = scores(items[n + 1])
                if pending is not None:
                    weighted_values(*pending)
                parts = [s[:, t * LANES:(t + 1) * LANES] for t in range(T // LANES)]
                m_old = m_s[hh, rows, :]
                m_cur = jnp.max(functools.reduce(jnp.maximum, parts), axis=1, keepdims=True)
                m_new = jnp.maximum(m_old, m_cur)
                alpha = jnp.exp(m_old - m_new)
                ps = [jnp.exp(part - m_new) for part in parts]
                l_s[hh, rows, :] = alpha * l_s[hh, rows, :] + functools.reduce(jnp.add, ps)
                m_s[hh, rows, :] = m_new
                pending = ((hh, c), jnp.concatenate(ps, axis=1).astype(BF16), alpha)
            weighted_values(*pending)

        @pl.when(d == nd - 1)
        def _():
            for hh in range(hp):
                cols = slice(hh * HEAD_DIM, (hh + 1) * HEAD_DIM)
                l = jnp.sum(l_s[hh], axis=1, keepdims=True)
                o_ref[:, cols] = acc_s[:, cols] / l
                lse_ref[hh] = m_s[hh][:, :1] + jnp.log(l)

    kv = pl.BlockSpec((T, wp), lambda h, i, d: (jnp.clip(i + d - hb, 0, nq - 1), h))
    return _hosted_call(
        body, [q, k, v, bias], name=name, grid=(H // hp, nq, nd),
        in_specs=[pl.BlockSpec((T, wp), lambda h, i, d: (i, h)), kv, kv,
                  pl.BlockSpec((nd, T, T), lambda h, i, d: (0, 0, 0))],
        out_specs=[pl.BlockSpec((T, wp), lambda h, i, d: (i, h)),
                   pl.BlockSpec((hp, T, 1), lambda h, i, d: (h, i, 0))],
        out_shape=[jax.ShapeDtypeStruct((S, W), F32), jax.ShapeDtypeStruct((H, S, 1), F32)],
        scratch_shapes=[pltpu.VMEM((hp, T, LANES), F32), pltpu.VMEM((hp, T, LANES), F32),
                        pltpu.VMEM((T, wp), F32)],
        sem=("parallel", "parallel", "arbitrary"), comm=comm)


def _attn_bwd(q, k, v, do, lse, delta, bias, name, comm=None):
    S, W = q.shape
    H = W // HEAD_DIM
    nd, T, _ = bias.shape
    hb, nq = nd // 2, S // T
    scale = HEAD_DIM ** -0.5
    hp = min(H, HEADS_PER_STEP)
    rc = min(T, ATTN_ROW_CHUNK)
    wp = hp * HEAD_DIM

    def body(q_ref, do_ref, lse_ref, dl_ref, k_ref, v_ref, b_ref, dq_ref, dk_ref, dv_ref):
        j, d = pl.program_id(1), pl.program_id(2)
        i = j + d - hb

        @pl.when((j == 0) & (d == 0))
        def _():
            dq_ref[...] = jnp.zeros(dq_ref.shape, F32)

        @pl.when(d == 0)
        def _():
            dk_ref[...] = jnp.zeros(dk_ref.shape, F32)
            dv_ref[...] = jnp.zeros(dv_ref.shape, F32)

        @pl.when((i >= 0) & (i < nq))
        def _():
            items = [(hh, c) for hh in range(hp) for c in range(T // rc)]

            def slices(item):
                hh, c = item
                return slice(hh * HEAD_DIM, (hh + 1) * HEAD_DIM), slice(c * rc, (c + 1) * rc)

            def products(item):
                cols, rows = slices(item)
                s = (lax.dot_general(q_ref[rows, cols], k_ref[:, cols], NT, preferred_element_type=F32)
                     + b_ref[nd - 1 - d, rows, :])
                dp = lax.dot_general(do_ref[rows, cols], v_ref[:, cols], NT, preferred_element_type=F32)
                return s, dp

            def gradients(item, p, ds):
                cols, rows = slices(item)
                dv_ref[:, cols] += lax.dot_general(p, do_ref[rows, cols], TN, preferred_element_type=F32)
                dk_ref[:, cols] += lax.dot_general(ds, q_ref[rows, cols], TN, preferred_element_type=F32)
                q_rows = pl.ds(pl.multiple_of(i * T + item[1] * rc, rc), rc)
                dq_ref[q_rows, cols] += jnp.dot(ds, k_ref[:, cols], preferred_element_type=F32) * scale

            nxt, pending = products(items[0]), None
            for n, item in enumerate(items):
                s, dp = nxt
                if n + 1 < len(items):
                    nxt = products(items[n + 1])
                if pending is not None:
                    gradients(*pending)
                _, rows = slices(item)
                p = jnp.exp(s - lse_ref[item[0], rows, :])
                ds = p * (dp - dl_ref[item[0], rows, :])
                pending = (item, p.astype(BF16), ds.astype(BF16))
            gradients(*pending)

    qi = lambda h, j, d: (jnp.clip(j + d - hb, 0, nq - 1), h)
    qs = pl.BlockSpec((T, wp), qi)
    col = pl.BlockSpec((hp, T, 1), lambda h, j, d: (h, jnp.clip(j + d - hb, 0, nq - 1), 0))
    kv = pl.BlockSpec((T, wp), lambda h, j, d: (j, h))
    shp = jax.ShapeDtypeStruct((S, W), F32)
    return _hosted_call(
        body, [q, do, lse, delta, k, v, bias], name=name, grid=(H // hp, nq, nd),
        in_specs=[qs, qs, col, col, kv, kv, pl.BlockSpec((nd, T, T), lambda h, j, d: (0, 0, 0))],
        out_specs=[pl.BlockSpec((S, wp), lambda h, j, d: (0, h)), kv, kv],
        out_shape=[shp, shp, shp],
        sem=("parallel", "arbitrary", "arbitrary"), comm=comm)


def _halo_specs(S, tr, W, piece):
    per, last = tr // 8, S // 8 - 1
    prev = pl.BlockSpec((8, W), lambda i: (jnp.maximum(i * per - 1, 0), piece))
    nxt = pl.BlockSpec((8, W), lambda i: (jnp.minimum((i + 1) * per, last), piece))
    return prev, nxt


def _shifted(t, before, after, tr):
    rows = lax.broadcasted_iota(jnp.int32, (tr, 1), 0)
    prev = jnp.where(rows == 0, before, pltpu.roll(t, 1, 0))
    nxt = jnp.where(rows == tr - 1, after, pltpu.roll(t, tr - 1, 0))
    return prev, nxt


def _ab_mix(attn, proj, conv_w, W, name):
    S = attn.shape[0]
    tr = min(S, ROW_TILE)
    nsteps = S // tr

    def body(a_ref, za_ref, ub_ref, gb_ref, gc_ref, zb_ref, ubp, ubn, gcp, gcn, w_ref, y_ref):
        i = pl.program_id(0)
        t = gc_ref[...] * ub_ref[...]
        before = jnp.where(i == 0, 0.0, (gcp[...] * ubp[...])[7:8, :])
        after = jnp.where(i == nsteps - 1, 0.0, (gcn[...] * ubn[...])[0:1, :])
        t_prev, t_next = _shifted(t, before, after, tr)
        w = w_ref[...]
        cv = w[0:1, :] * t_prev + w[1:2, :] * t + w[2:3, :] * t_next
        silu_a, _ = _silu_and_grad(za_ref[...])
        silu_b, _ = _silu_and_grad(zb_ref[...])
        y_ref[:, :W] = (a_ref[...] * silu_a).astype(BF16)
        y_ref[:, W:] = (gb_ref[...] * cv * silu_b).astype(BF16)

    piece = lambda p: pl.BlockSpec((tr, W), lambda i: (i, p))
    ubp, ubn = _halo_specs(S, tr, W, 4)
    gcp, gcn = _halo_specs(S, tr, W, 6)
    return pl.pallas_call(
        body, name=name, grid=(nsteps,),
        in_specs=[pl.BlockSpec((tr, W), lambda i: (i, 0)), piece(3), piece(4), piece(5), piece(6), piece(7),
                  ubp, ubn, gcp, gcn, pl.BlockSpec((3, W), lambda i: (0, 0))],
        out_specs=pl.BlockSpec((tr, 2 * W), lambda i: (i, 0)),
        out_shape=jax.ShapeDtypeStruct((S, 2 * W), BF16),
        compiler_params=_cp("parallel"),
    )(attn, proj, proj, proj, proj, proj, proj, proj, proj, proj, conv_w)


def _dattn_prep(dy, proj, attn, W, name):
    S = attn.shape[0]
    H = W // HEAD_DIM
    tr = min(S, ROW_TILE)

    def body(dy_ref, za_ref, a_ref, do_ref, dl_ref):
        silu_a, _ = _silu_and_grad(za_ref[...])
        do = dy_ref[...] * silu_a
        do_ref[...] = do.astype(BF16)
        prod = do * a_ref[...]
        for hh in range(H):
            dl_ref[hh] = jnp.sum(prod[:, hh * HEAD_DIM:(hh + 1) * HEAD_DIM], axis=1, keepdims=True)

    row = pl.BlockSpec((tr, W), lambda i: (i, 0))
    return pl.pallas_call(
        body, name=name, grid=(S // tr,),
        in_specs=[row, pl.BlockSpec((tr, W), lambda i: (i, 3)), row],
        out_specs=[row, pl.BlockSpec((H, tr, 1), lambda i: (0, i, 0))],
        out_shape=[jax.ShapeDtypeStruct((S, W), BF16), jax.ShapeDtypeStruct((H, S, 1), F32)],
        compiler_params=_cp("parallel"),
    )(dy, proj, attn)


def _ab_bwd(dy, attn, proj, dqr, dkr, dv, cosf, sinf, conv_w, W, name):
    S = attn.shape[0]
    tr = min(S, ROW_TILE // 2)
    nsteps = S // tr
    heads = W // HEAD_DIM

    def body(dya_ref, dyb_ref, a_ref, za_ref, ub_ref, gb_ref, gc_ref, zb_ref, dq_ref, dk_ref, dv_ref,
             c_ref, s_ref, w_ref, dybp, dybn, gbp, gbn, zbp, zbn, ubp, ubn, gcp, gcn,
             dp_ref, dw_ref, acc):
        i = pl.program_id(0)
        first, last = i == 0, i == nsteps - 1
        w = w_ref[...]
        w0, w1, w2 = w[0:1, :], w[1:2, :], w[2:3, :]
        ub, gb, gc, zb = ub_ref[...], gb_ref[...], gc_ref[...], zb_ref[...]
        dyb = dyb_ref[...]
        silu_a, dsilu_a = _silu_and_grad(za_ref[...])
        silu_b, dsilu_b = _silu_and_grad(zb)
        t = gc * ub
        t_prev, t_next = _shifted(t, jnp.where(first, 0.0, (gcp[...] * ubp[...])[7:8, :]),
                                  jnp.where(last, 0.0, (gcn[...] * ubn[...])[0:1, :]), tr)
        cv = w0 * t_prev + w1 * t + w2 * t_next
        dcv = dyb * gb * silu_b
        halo_p = dybp[...] * gbp[...] * _silu_and_grad(zbp[...])[0]
        halo_n = dybn[...] * gbn[...] * _silu_and_grad(zbn[...])[0]
        dcv_prev, dcv_next = _shifted(dcv, jnp.where(first, 0.0, halo_p[7:8, :]),
                                      jnp.where(last, 0.0, halo_n[0:1, :]), tr)
        dt = w0 * dcv_next + w1 * dcv + w2 * dcv_prev
        cosf_v, sinf_v = c_ref[...], s_ref[...]
        for src, base in ((dq_ref, 0), (dk_ref, W)):
            for hh, val in enumerate(_rope_apply(src[...], cosf_v, sinf_v, heads, -1.0)):
                dp_ref[:, base + hh * HEAD_DIM:base + (hh + 1) * HEAD_DIM] = val.astype(BF16)
        dp_ref[:, 2 * W:3 * W] = dv_ref[...].astype(BF16)
        dp_ref[:, 3 * W:4 * W] = (dya_ref[...] * a_ref[...] * dsilu_a).astype(BF16)
        dp_ref[:, 4 * W:5 * W] = (dt * gc).astype(BF16)
        dp_ref[:, 5 * W:6 * W] = (dyb * cv * silu_b).astype(BF16)
        dp_ref[:, 6 * W:7 * W] = (dt * ub).astype(BF16)
        dp_ref[:, 7 * W:8 * W] = (dyb * gb * cv * dsilu_b).astype(BF16)
        tap = lax.broadcasted_iota(jnp.int32, (8, 1), 0)
        part = (jnp.where(tap == 0, jnp.sum(dcv * t_prev, axis=0, keepdims=True), 0.0)
                + jnp.where(tap == 1, jnp.sum(dcv * t, axis=0, keepdims=True), 0.0)
                + jnp.where(tap == 2, jnp.sum(dcv * t_next, axis=0, keepdims=True), 0.0))

        @pl.when(first)
        def _():
            acc[...] = part

        @pl.when(i > 0)
        def _():
            acc[...] += part

        @pl.when(last)
        def _():
            dw_ref[...] = acc[...]

    row = pl.BlockSpec((tr, W), lambda i: (i, 0))
    piece = lambda p: pl.BlockSpec((tr, W), lambda i: (i, p))
    tab = pl.BlockSpec((tr, HEAD_DIM), lambda i: (i, 0))
    dybp, dybn = _halo_specs(S, tr, W, 1)
    gbp, gbn = _halo_specs(S, tr, W, 5)
    zbp, zbn = _halo_specs(S, tr, W, 7)
    ubp, ubn = _halo_specs(S, tr, W, 4)
    gcp, gcn = _halo_specs(S, tr, W, 6)
    return pl.pallas_call(
        body, name=name, grid=(nsteps,),
        in_specs=[piece(0), piece(1), row, piece(3), piece(4), piece(5), piece(6), piece(7), row, row, row,
                  tab, tab, pl.BlockSpec((3, W), lambda i: (0, 0)),
                  dybp, dybn, gbp, gbn, zbp, zbn, ubp, ubn, gcp, gcn],
        out_specs=[pl.BlockSpec((tr, 8 * W), lambda i: (i, 0)), pl.BlockSpec((8, W), lambda i: (0, 0))],
        out_shape=[jax.ShapeDtypeStruct((S, 8 * W), BF16), jax.ShapeDtypeStruct((8, W), F32)],
        scratch_shapes=[pltpu.VMEM((8, W), F32)],
        compiler_params=_cp("arbitrary"),
    )(dy, dy, attn, proj, proj, proj, proj, proj, dqr, dkr, dv, cosf, sinf, conv_w,
      dy, dy, proj, proj, proj, proj, proj, proj, proj, proj)


def _sgu_core(p_ref, lng_ref, lnb_ref, ws_ref, bst_ref, Dc):
    gw = Dc // C_GROUPS
    u_raw, v_raw, z = p_ref[:, :Dc], p_ref[:, Dc:2 * Dc], p_ref[:, 2 * Dc:]
    u, du = _gelu_and_grad(u_raw)
    vg, dvg = _gelu_and_grad(v_raw)
    mu = jnp.mean(vg, axis=-1, keepdims=True)
    vc = vg - mu
    rstd = lax.rsqrt(jnp.mean(vc * vc, axis=-1, keepdims=True) + EPS)
    vhat = vc * rstd
    vn = (vhat * lng_ref[...] + lnb_ref[...]).astype(BF16)
    bst = bst_ref[...]
    mixed = jnp.concatenate(
        [jnp.dot(ws_ref[g].astype(BF16), vn[:, g * gw:(g + 1) * gw], preferred_element_type=F32)
         + bst[:, g:g + 1] for g in range(C_GROUPS)], axis=1)
    sz, dsz = _silu_and_grad(z)
    return u, du, dvg, rstd, vhat, vn, mixed, sz, dsz


def _sgu_fwd(proj, ln_g, ln_b, w_s, b_st, name):
    S, Dc3 = proj.shape
    Dc = Dc3 // 3
    vec = pl.BlockSpec((1, Dc), lambda i: (0, 0))

    def body(p_ref, lng_ref, lnb_ref, ws_ref, bst_ref, y_ref):
        u, _, _, _, _, _, mixed, sz, _ = _sgu_core(p_ref, lng_ref, lnb_ref, ws_ref, bst_ref, Dc)
        y_ref[...] = (u * mixed * sz).astype(BF16)

    return pl.pallas_call(
        body, name=name, grid=(S // C_CHUNK,),
        in_specs=[pl.BlockSpec((C_CHUNK, Dc3), lambda i: (i, 0)), vec, vec,
                  pl.BlockSpec((C_GROUPS, C_CHUNK, C_CHUNK), lambda i: (0, 0, 0)),
                  pl.BlockSpec((C_CHUNK, C_GROUPS), lambda i: (0, 0))],
        out_specs=pl.BlockSpec((C_CHUNK, Dc), lambda i: (i, 0)),
        out_shape=jax.ShapeDtypeStruct((S, Dc), BF16),
        compiler_params=_cp("parallel"),
    )(proj, ln_g, ln_b, w_s, b_st)


def _sgu_bwd(proj, dy, ln_g, ln_b, w_s, w_st, b_st, name):
    S, Dc3 = proj.shape
    Dc = Dc3 // 3
    gw = Dc // C_GROUPS
    nsteps = S // C_CHUNK
    vec = pl.BlockSpec((1, Dc), lambda i: (0, 0))
    wspec = pl.BlockSpec((C_GROUPS, C_CHUNK, C_CHUNK), lambda i: (0, 0, 0))

    def body(p_ref, dy_ref, lng_ref, lnb_ref, ws_ref, wst_ref, bst_ref,
             dp_ref, dws_ref, dbs_ref, dlg_ref, dlb_ref, acc_w, acc_b, acc_g, acc_lb):
        i = pl.program_id(0)
        u, du, dvg, rstd, vhat, vn, mixed, sz, dsz = _sgu_core(p_ref, lng_ref, lnb_ref, ws_ref, bst_ref, Dc)
        dy = dy_ref[...]
        dmixed = dy * u * sz
        dmb = dmixed.astype(BF16)

        @pl.when(i == 0)
        def _():
            acc_w[...] = jnp.zeros(acc_w.shape, F32)
            acc_b[...] = jnp.zeros(acc_b.shape, F32)
            acc_g[...] = jnp.zeros(acc_g.shape, F32)
            acc_lb[...] = jnp.zeros(acc_lb.shape, F32)

        dvn_parts = []
        for g in range(C_GROUPS):
            dmg = dmb[:, g * gw:(g + 1) * gw]
            acc_w[g] += lax.dot_general(dmg, vn[:, g * gw:(g + 1) * gw], NT, preferred_element_type=F32)
            acc_b[g] += dmixed[:, g * gw:(g + 1) * gw]
            dvn_parts.append(jnp.dot(wst_ref[g].astype(BF16), dmg, preferred_element_type=F32))
        dvn = jnp.concatenate(dvn_parts, axis=1)
        acc_g[...] += jnp.sum((dvn * vhat).reshape(C_CHUNK // 8, 8, Dc), axis=0)
        acc_lb[...] += jnp.sum(dvn.reshape(C_CHUNK // 8, 8, Dc), axis=0)
        dvh = dvn * lng_ref[...]
        dvgelu = rstd * (dvh - jnp.mean(dvh, axis=-1, keepdims=True)
                         - vhat * jnp.mean(dvh * vhat, axis=-1, keepdims=True))
        dp_ref[:, :Dc] = (dy * mixed * sz * du).astype(BF16)
        dp_ref[:, Dc:2 * Dc] = (dvgelu * dvg).astype(BF16)
        dp_ref[:, 2 * Dc:] = (dy * u * mixed * dsz).astype(BF16)

        @pl.when(i == nsteps - 1)
        def _():
            dws_ref[...] = acc_w[...]
            for g in range(C_GROUPS):
                dbs_ref[g] = jnp.sum(acc_b[g], axis=1, keepdims=True)
            dlg_ref[...] = jnp.sum(acc_g[...], axis=0, keepdims=True)
            dlb_ref[...] = jnp.sum(acc_lb[...], axis=0, keepdims=True)

    v = jax.ShapeDtypeStruct((1, Dc), F32)
    return pl.pallas_call(
        body, name=name, grid=(nsteps,),
        in_specs=[pl.BlockSpec((C_CHUNK, Dc3), lambda i: (i, 0)), pl.BlockSpec((C_CHUNK, Dc), lambda i: (i, 0)),
                  vec, vec, wspec, wspec, pl.BlockSpec((C_CHUNK, C_GROUPS), lambda i: (0, 0))],
        out_specs=[pl.BlockSpec((C_CHUNK, Dc3), lambda i: (i, 0)), wspec,
                   pl.BlockSpec((C_GROUPS, C_CHUNK, 1), lambda i: (0, 0, 0)), vec, vec],
        out_shape=[jax.ShapeDtypeStruct((S, Dc3), BF16),
                   jax.ShapeDtypeStruct((C_GROUPS, C_CHUNK, C_CHUNK), F32),
                   jax.ShapeDtypeStruct((C_GROUPS, C_CHUNK, 1), F32), v, v],
        scratch_shapes=[pltpu.VMEM((C_GROUPS, C_CHUNK, C_CHUNK), F32), pltpu.VMEM((C_GROUPS, C_CHUNK, gw), F32),
                        pltpu.VMEM((8, Dc), F32), pltpu.VMEM((8, Dc), F32)],
        compiler_params=_cp("arbitrary"),
    )(proj, dy, ln_g, ln_b, w_s, w_st, b_st)


PACK_COLS = 1024
PACK_ROWS = 64


def _pack(vectors):
    flat = jnp.concatenate([v.reshape(-1) for v in vectors])
    pad = (-flat.shape[0]) % (PACK_COLS * PACK_ROWS)
    return jnp.pad(flat, (0, pad)).reshape(-1, PACK_COLS)


def _unshard(g, off, shape):
    L, rest = shape[0], shape[1:]
    size = int(np.prod(shape))
    piece = g[:, off:off + size].reshape((NDEV,) + tuple(shape))
    nd = piece.ndim
    perm = tuple(range(1, nd - 1)) + (0, nd - 1)
    full = jnp.transpose(piece, perm)
    return full.reshape(tuple(shape[:-1]) + (NDEV * shape[-1],)), off + size


def kernel(x, c, ab_norm_g, ab_w_mod, ab_b_mod, ab_w_in, ab_conv_w, ab_w_out, sg_norm_g, sg_w_mod, sg_b_mod, sg_w_in, sg_ln_g, sg_ln_b, sg_w_s, sg_b_s, sg_w_out, final_norm_g, loss_target, m_ab_norm_g, m_ab_w_mod, m_ab_b_mod, m_ab_w_in, m_ab_conv_w, m_ab_w_out, m_sg_norm_g, m_sg_w_mod, m_sg_b_mod, m_sg_w_in, m_sg_ln_g, m_sg_ln_b, m_sg_w_s, m_sg_b_s, m_sg_w_out, m_final_norm_g, v_ab_norm_g, v_ab_w_mod, v_ab_b_mod, v_ab_w_in, v_ab_conv_w, v_ab_w_out, v_sg_norm_g, v_sg_w_mod, v_sg_b_mod, v_sg_w_in, v_sg_ln_g, v_sg_ln_b, v_sg_w_s, v_sg_b_s, v_sg_w_out, v_final_norm_g):
    _, S, D = x.shape
    L = ab_norm_g.shape[0]
    W = ab_conv_w.shape[2] * NDEV
    n_ab, n_sg = ab_w_in.shape[2], sg_w_in.shape[2]
    n_mod = ab_w_mod.shape[2]
    kb = ab_w_out.shape[1]
    xi, yi, ci = _position()
    dev = 4 * xi + 2 * yi + ci
    x2, tgt = x.reshape(S, D), loss_target.reshape(S, D)

    small = [c, ab_conv_w, sg_norm_g, sg_ln_g, sg_ln_b]
    (g1,) = _comm_only(_Gather([_pack(small)]), "ag_small")
    g1 = g1.reshape(NDEV, -1)
    c_all = g1[:, :D]
    off = D
    conv_full, off = _unshard(g1, off, ab_conv_w.shape)
    sg_norm_full, off = _unshard(g1, off, sg_norm_g.shape)
    ln_g_full, off = _unshard(g1, off, sg_ln_g.shape)
    ln_b_full, off = _unshard(g1, off, sg_ln_b.shape)

    ab_b_cols = lax.dynamic_slice_in_dim(ab_b_mod, dev * n_mod, n_mod, axis=1)
    m_ab = _mod_fwd(c_all, ab_w_mod, ab_b_cols.reshape(L, 1, n_mod), "mod_fwd_ab")
    m_sg = _mod_fwd(c_all, sg_w_mod, sg_b_mod.reshape(L, 1, n_mod), "mod_fwd_sg")
    m_part = jnp.stack([m_ab, m_sg]).transpose(2, 0, 1, 3).reshape(NDEV, 2 * L * n_mod)
    (g2,) = _comm_only(_Gather([m_part]), "ag_mod")
    mine = lax.dynamic_index_in_dim(g2, dev, axis=1, keepdims=False)
    mods = mine.reshape(NDEV, 2, L, n_mod).transpose(1, 2, 0, 3).reshape(2, L, 3 * D)

    def mod_of(kind, i):
        m = mods[kind, i]
        return m[:D].reshape(1, D), m[D:2 * D].reshape(1, D), m[2 * D:].reshape(1, D)

    big_w = [[(ab_w_in, m_ab_w_in, v_ab_w_in), (ab_w_out, m_ab_w_out, v_ab_w_out)],
             [(sg_w_in, m_sg_w_in, v_sg_w_in), (sg_w_out, m_sg_w_out, v_sg_w_out)]]
    big_names = [["ab_w_in", "ab_w_out"], ["sg_w_in", "sg_w_out"]]
    n_layers = 2 * L
    shards = [[big_w[layer % 2][k][0][layer // 2].astype(BF16) for k in range(2)] for layer in range(n_layers)]
    gathered = {}

    def gather_of(keys):
        keys = [key for key in keys if key[0] < n_layers]
        return keys, (_Gather([shards[layer][k] for layer, k in keys]) if keys else None)

    def keep_gathered(keys, res):
        for (layer, k), g in zip(keys, res):
            gathered[(layer, k)] = g.reshape((NDEV, 1, D, g.shape[-1]) if k == 0 else (NDEV, 1, kb, D))

    keys, comm = gather_of([(0, 0)])
    keep_gathered(keys, _comm_only(comm, "ag_w_in_layer0"))

    cosf, sinf = _rope_tables(S)
    T = min(S, ATTN_TILE)
    bias = _attn_bias(T)
    norm_g = [ab_norm_g, sg_norm_full]
    w_s_t = jnp.swapaxes(sg_w_s, -1, -2)
    b_s_t = jnp.swapaxes(sg_b_s, -1, -2)

    saved = []
    x_cur, res, gate_prev = x2, None, None
    for layer in range(2 * L):
        kind, i = layer % 2, layer // 2
        tag = f"{'ab' if kind == 0 else 'sg'}{i}"
        shift, scale, gate = mod_of(kind, i)
        g = norm_g[kind][i].reshape(1, D)
        xl, h = _pre(x_cur, res, gate_prev, g, scale, shift, f"pre_{tag}")
        if kind == 0:
            keys, comm = gather_of(([(0, 1)] if layer == 0 else []) + [(layer + 1, 0)])
            if comm is None:
                proj = _mm_nn_in(h, gathered[(layer, 0)], 0, f"proj_{tag}")
            else:
                proj, got = _mm_nn_in(h, gathered[(layer, 0)], 0, f"proj_{tag}", comm)
                keep_gathered(keys, got)
        else:
            proj = _mm_nn_in(h, gathered[(layer, 0)], 0, f"proj_{tag}")
        rec = dict(xl=xl, h=h, proj=proj, g=g, scale=scale, gate=gate)
        if kind == 0:
            qr, kr, vb = _rope_qkv(proj, cosf, sinf, W, f"rope_{tag}")
            keys, comm = gather_of([(layer + 1, 1), (layer + 2, 0), (layer + 2, 1)])
            (attn, lse), got = _attn_fwd(qr, kr, vb, bias, f"attn_{tag}", comm)
            keep_gathered(keys, got)
            y = _ab_mix(attn, proj, conv_full[i], W, f"mix_{tag}")
            rec.update(qr=qr, kr=kr, vb=vb, attn=attn, lse=lse)
        else:
            y = _sgu_fwd(proj, ln_g_full[i].reshape(1, D), ln_b_full[i].reshape(1, D), sg_w_s[i], b_s_t[i],
                         f"sgu_{tag}")
        out = _mm_nn_out(y, gathered[(layer, 1)], 0, f"out_{tag}")
        rec.update(y=y, out=out)
        saved.append(rec)
        x_cur, res, gate_prev = xl, out, gate

    dx, loss_part, d_final_g = _loss_head(x_cur, res, gate_prev, final_norm_g.reshape(1, D), tgt, "loss_head")
    loss = lax.psum(loss_part[0, 0], ("x", "y", "c"))

    c_idx = ci.reshape(1).astype(jnp.int32)
    big_res = {}
    pending = None

    def finish_layer(done, from_chips):
        for k in range(2):
            nm = big_names[done % 2][k]
            w, m, v = big_w[done % 2][k]
            flat = lambda a: a.reshape(L * a.shape[1], a.shape[2])
            big_res[nm] = _sum_adam(from_chips[k], flat(w), flat(m), flat(v), done // 2, big_res.get(nm),
                                    f"adam_{nm}{done // 2}")

    dm = [[None] * L, [None] * L]
    d_norm = [[None] * L, [None] * L]
    d_conv, d_lng, d_lnb, d_ws, d_bs = [None] * L, [None] * L, [None] * L, [None] * L, [None] * L
    for layer in reversed(range(2 * L)):
        kind, i = layer % 2, layer // 2
        tag = f"{'ab' if kind == 0 else 'sg'}{i}"
        rec = saved[layer]
        w_in_l, w_out_l = gathered[(layer, 0)], gathered[(layer, 1)]
        dout, dgate = _post_bwd(dx, rec["out"], rec["gate"], f"post_bwd_{tag}")
        dy = _mm_nt_out(dout, w_out_l, 0, f"dy_{tag}")
        dwo = _mm_tn_out(rec["y"], dout, 0, 1, None, f"dwout_{tag}")
        if kind == 0:
            do, delta = _dattn_prep(dy, rec["proj"], rec["attn"], W, f"dattn_{tag}")
            comm = _ToChips(pending[1]) if pending else None
            (dqr, dkr, dvv), got = _attn_bwd(rec["qr"], rec["kr"], rec["vb"], do, rec["lse"], delta, bias,
                                             f"attn_bwd_{tag}", comm)
            if pending:
                finish_layer(pending[0], got)
            dproj, dcw = _ab_bwd(dy, rec["attn"], rec["proj"], dqr, dkr, dvv, cosf, sinf, conv_full[i], W,
                                 f"mix_bwd_{tag}")
            d_conv[i] = dcw[:3]
            dh = _mm_nt_in(dproj, w_in_l, 0, f"dh_{tag}")
            dwi = _mm_tn_in(rec["h"], dproj, 0, 1, None, f"dwin_{tag}")
        else:
            dproj, d_ws[i], dbs, d_lng[i], d_lnb[i] = _sgu_bwd(
                rec["proj"], dy, ln_g_full[i].reshape(1, D), ln_b_full[i].reshape(1, D),
                sg_w_s[i], w_s_t[i], b_s_t[i], f"sgu_bwd_{tag}")
            d_bs[i] = dbs.reshape(C_GROUPS, C_CHUNK)
            if pending:
                dh, got_in = _mm_nt_in(dproj, w_in_l, 0, f"dh_{tag}", _ToChips(pending[1][:1]))
                dwi, got_out = _mm_tn_in(rec["h"], dproj, 0, 1, None, f"dwin_{tag}", _ToChips(pending[1][1:]))
                finish_layer(pending[0], [got_in[0], got_out[0]])
            else:
                dh = _mm_nt_in(dproj, w_in_l, 0, f"dh_{tag}")
                dwi = _mm_tn_in(rec["h"], dproj, 0, 1, None, f"dwin_{tag}")
        grads = [dwi.reshape(NDEV, D, -1), dwo.reshape(NDEV, kb, D)]
        (dx, dshift, dscale, d_norm[kind][i]), from_sibling = _pre_bwd(
            rec["xl"], dh, dx, rec["g"], rec["scale"], f"pre_bwd_{tag}", _ToSibling(grads))
        dm[kind][i] = jnp.concatenate([dshift, dscale, dgate], axis=1).reshape(3 * D)
        pending = (layer, [_add_sibling(g, r, c_idx, f"rs_add_{big_names[kind][k]}{i}")
                           for k, (g, r) in enumerate(zip(grads, from_sibling))])
    grad_x = dx.reshape(1, S, D)
    finish_layer(pending[0], _comm_only(_ToChips(pending[1]), "rs_chips_last"))
    for kind in range(2):
        for k in range(2):
            nm = big_names[kind][k]
            big_res[nm] = [o.reshape(big_w[kind][k][0].shape) for o in big_res[nm]]

    stack = lambda xs: jnp.stack(xs)
    pack_items = [stack(dm[0]), stack(dm[1]), stack(d_norm[0]).reshape(L, D), stack(d_conv),
                  stack(d_norm[1]).reshape(L, D), stack(d_lng).reshape(L, D), stack(d_lnb).reshape(L, D),
                  stack(d_ws), stack(d_bs), d_final_g]
    (g3,) = _comm_only(_Gather([_pack(pack_items)]), "ag_grads")
    P = g3.shape[1] * g3.shape[2]
    tot = _sum_rows(g3, "sum_small").reshape(P)
    g3 = g3.reshape(NDEV, P)
    sizes = [int(np.prod(p.shape)) for p in pack_items]
    offs = np.concatenate([[0], np.cumsum(sizes)]).tolist()
    seg = lambda k, shape: tot[offs[k]:offs[k + 1]].reshape(shape)

    def shard(full, n):
        return lax.dynamic_slice_in_dim(full, dev * n, n, axis=full.ndim - 1)

    g_ab_b_mod = seg(0, (L, 3 * D))
    g_sg_b_mod = shard(seg(1, (L, 3 * D)), n_mod)
    g_ab_norm = seg(2, (L, D))
    g_conv = shard(seg(3, (L, 3, W)), W // NDEV)
    g_sg_norm = shard(seg(4, (L, D)), kb)
    g_ln_g = shard(seg(5, (L, D)), kb)
    g_ln_b = shard(seg(6, (L, D)), kb)
    g_w_s = seg(7, sg_w_s.shape)
    g_b_s = seg(8, sg_b_s.shape)
    g_final = seg(9, (D,))

    small_w = [("ab_norm_g", g_ab_norm, ab_norm_g, m_ab_norm_g, v_ab_norm_g),
               ("ab_b_mod", g_ab_b_mod, ab_b_mod, m_ab_b_mod, v_ab_b_mod),
               ("ab_conv_w", g_conv, ab_conv_w, m_ab_conv_w, v_ab_conv_w),
               ("sg_norm_g", g_sg_norm, sg_norm_g, m_sg_norm_g, v_sg_norm_g),
               ("sg_b_mod", g_sg_b_mod, sg_b_mod, m_sg_b_mod, v_sg_b_mod),
               ("sg_ln_g", g_ln_g, sg_ln_g, m_sg_ln_g, v_sg_ln_g),
               ("sg_ln_b", g_ln_b, sg_ln_b, m_sg_ln_b, v_sg_ln_b),
               ("sg_w_s", g_w_s, sg_w_s, m_sg_w_s, v_sg_w_s),
               ("sg_b_s", g_b_s, sg_b_s, m_sg_b_s, v_sg_b_s),
               ("final_norm_g", g_final, final_norm_g, m_final_norm_g, v_final_norm_g)]
    packed = [_pack([t[k] for t in small_w]) for k in (1, 2, 3, 4)]
    upd = _adam_only(*packed, "adam_small")
    small_res = {}
    o = 0
    for nm, g, w, _, _ in small_w:
        size = int(np.prod(w.shape))
        small_res[nm] = [g] + [u.reshape(-1)[o:o + size].reshape(w.shape) for u in upd]
        o += size

    KP = 128
    sc_t = jnp.pad((c_all * jax.nn.sigmoid(c_all)).T, ((0, 0), (0, KP - NDEV)))
    mod_res = {}
    for kind, nm, (w, m, v) in ((0, "ab_w_mod", (ab_w_mod, m_ab_w_mod, v_ab_w_mod)),
                                (1, "sg_w_mod", (sg_w_mod, m_sg_w_mod, v_sg_w_mod))):
        dm_all = g3[:, offs[kind]:offs[kind + 1]].reshape(NDEV, L, 3 * D)
        cols = jnp.pad(shard(dm_all, n_mod).transpose(1, 0, 2), ((0, 0), (0, KP - NDEV), (0, 0)))
        mod_res[nm] = _wmod_grad_adam(sc_t, cols, w, m, v, f"adam_{nm}")

    order = ["ab_norm_g", "ab_w_mod", "ab_b_mod", "ab_w_in", "ab_conv_w", "ab_w_out", "sg_norm_g", "sg_w_mod",
             "sg_b_mod", "sg_w_in", "sg_ln_g", "sg_ln_b", "sg_w_s", "sg_b_s", "sg_w_out", "final_norm_g"]
    res = {**big_res, **small_res, **mod_res}
    outs = [loss, grad_x]
    for k in range(4):
        outs += [res[nm][k] for nm in order]
    return tuple(outs)
```

```python
import functools
import math

import numpy as np
import jax
import jax.numpy as jnp
from jax import lax
from jax.experimental import pallas as pl
from jax.experimental.pallas import tpu as pltpu

F32 = jnp.float32
BF16 = jnp.bfloat16

NDEV = 8
NCHIP = 4
EPS = 1e-6
HEAD_DIM = 128
ROPE_THETA = 10000.0
DILATED_PATTERNS = ((128, 1), (512, 4), (2048, 16))
NEG_INF = -1e30
C_CHUNK = 128
C_GROUPS = 8
ADAM_LR = 0.001
ADAM_B1 = 0.9
ADAM_B2 = 0.999
ADAM_EPS = 1e-08
ADAM_WD = 0.01
ADAM_STEP = 10
GELU_K = math.sqrt(2.0 / math.pi)
GELU_C = 0.044715

VMEM_LIMIT_BYTES = 56 * 1024 * 1024
ATTN_TILE = 512
HEADS_PER_STEP = 4
ATTN_ROW_CHUNK = 256
LANES = 128
ROW_TILE = 256
MESH = pl.DeviceIdType.MESH
ANY = pl.BlockSpec(memory_space=pl.ANY)


def _cp(*sem):
    return pltpu.CompilerParams(dimension_semantics=sem, vmem_limit_bytes=VMEM_LIMIT_BYTES)


def _sigmoid(z):
    return 0.5 * (jnp.tanh(0.5 * z) + 1.0)


def _silu_and_grad(z):
    s = _sigmoid(z)
    return z * s, s * (1.0 + z * (1.0 - s))


def _gelu_and_grad(x):
    x2 = x * x
    t = jnp.tanh(GELU_K * (x + GELU_C * x2 * x))
    g = 0.5 * x * (1.0 + t)
    dg = 0.5 * (1.0 + t) + 0.5 * x * (1.0 - t * t) * (GELU_K * (1.0 + 3.0 * GELU_C * x2))
    return g, dg


def _position():
    return lax.axis_index("x"), lax.axis_index("y"), lax.axis_index("c")


def _chips(x, y):
    return [(1 - x, y), (x, 1 - y), (1 - x, 1 - y)]


class _Gather:
    def __init__(self, arrs):
        n = len(arrs)
        self.arrs = list(arrs)
        self.out_shape = [jax.ShapeDtypeStruct((NDEV,) + a.shape, a.dtype) for a in arrs]
        self.scratch = [pltpu.SemaphoreType.DMA((n, 7)), pltpu.SemaphoreType.DMA((n, 7)),
                        pltpu.SemaphoreType.DMA((n,))]

    def _copies(self, ins, outs, sems):
        send_sems, recv_sems, local_sems = sems
        x, y, c = _position()

        def copy(a, k, block, to, src=None):
            dst = outs[a].at[4 * block[0] + 2 * block[1] + block[2]]
            return pltpu.make_async_remote_copy(
                src_ref=dst if src is None else src, dst_ref=dst,
                send_sem=send_sems.at[a, k], recv_sem=recv_sems.at[a, k],
                device_id=to, device_id_type=MESH)

        n = len(ins)
        me, sibling = (x, y, c), (x, y, 1 - c)
        mine = [pltpu.make_async_copy(ins[a], outs[a].at[4 * x + 2 * y + c], local_sems.at[a]) for a in range(n)]
        first = []
        for a in range(n):
            first.append(copy(a, 0, me, sibling, src=ins[a]))
            first += [copy(a, 1 + j, me, (*chip, c), src=ins[a]) for j, chip in enumerate(_chips(x, y))]
        return copy, mine, first

    def start(self, ins, outs, sems):
        _, mine, first = self._copies(ins, outs, sems)
        for cp in mine + first:
            cp.start()

    def finish(self, ins, outs, sems):
        copy, mine, first = self._copies(ins, outs, sems)
        x, y, c = _position()
        me, sibling = (x, y, c), (x, y, 1 - c)
        passed = []
        for j, chip in enumerate(_chips(x, y)):
            for a in range(len(ins)):
                copy(a, 1 + j, (*chip, c), me).wait_recv()
                fwd = copy(a, 4 + j, (*chip, c), sibling)
                fwd.start()
                passed.append(fwd)
        for a in range(len(ins)):
            copy(a, 0, sibling, me).wait_recv()
            for j, chip in enumerate(_chips(x, y)):
                copy(a, 4 + j, (*chip, 1 - c), me).wait_recv()
        for cp in first + passed:
            cp.wait_send()
        for cp in mine:
            cp.wait()


class _ToSibling:
    def __init__(self, gs):
        n = len(gs)
        self.arrs = list(gs)
        self.out_shape = [jax.ShapeDtypeStruct((NCHIP,) + g.shape[1:], g.dtype) for g in gs]
        self.scratch = [pltpu.SemaphoreType.DMA((n, NCHIP)), pltpu.SemaphoreType.DMA((n, NCHIP))]

    def _copies(self, ins, outs, sems):
        send_sems, recv_sems = sems
        x, y, c = _position()
        return [pltpu.make_async_remote_copy(
            src_ref=ins[a].at[2 * k + (1 - c)], dst_ref=outs[a].at[k],
            send_sem=send_sems.at[a, k], recv_sem=recv_sems.at[a, k],
            device_id=(x, y, 1 - c), device_id_type=MESH) for a in range(len(ins)) for k in range(NCHIP)]

    def start(self, ins, outs, sems):
        for cp in self._copies(ins, outs, sems):
            cp.start()

    def finish(self, ins, outs, sems):
        copies = self._copies(ins, outs, sems)
        for cp in copies:
            cp.wait_recv()
        for cp in copies:
            cp.wait_send()


class _ToChips:
    def __init__(self, ps):
        n = len(ps)
        self.arrs = list(ps)
        self.out_shape = [jax.ShapeDtypeStruct(p.shape, p.dtype) for p in ps]
        self.scratch = [pltpu.SemaphoreType.DMA((n, 3)), pltpu.SemaphoreType.DMA((n, 3)),
                        pltpu.SemaphoreType.DMA((n,))]

    def _copies(self, ins, outs, sems, arrivals):
        send_sems, recv_sems, local_sems = sems
        x, y, c = _position()
        mychip = 2 * x + y
        n = len(ins)
        mine = [pltpu.make_async_copy(ins[a].at[mychip], outs[a].at[mychip], local_sems.at[a]) for a in range(n)]
        sends, recvs = [], []
        for a in range(n):
            for j, chip in enumerate(_chips(x, y)):
                sends.append(pltpu.make_async_remote_copy(
                    src_ref=ins[a].at[2 * chip[0] + chip[1]], dst_ref=outs[a].at[mychip],
                    send_sem=send_sems.at[a, j], recv_sem=recv_sems.at[a, j],
                    device_id=(*chip, c), device_id_type=MESH))
                if arrivals:
                    slot = outs[a].at[2 * chip[0] + chip[1]]
                    recvs.append(pltpu.make_async_remote_copy(
                        src_ref=slot, dst_ref=slot, send_sem=send_sems.at[a, j], recv_sem=recv_sems.at[a, j],
                        device_id=(*chip, c), device_id_type=MESH))
        return mine, sends, recvs

    def start(self, ins, outs, sems):
        mine, sends, _ = self._copies(ins, outs, sems, False)
        for cp in mine + sends:
            cp.start()

    def finish(self, ins, outs, sems):
        mine, sends, recvs = self._copies(ins, outs, sems, True)
        for cp in recvs:
            cp.wait_recv()
        for cp in sends:
            cp.wait_send()
        for cp in mine:
            cp.wait()


class _Both:
    def __init__(self, first, second):
        self.parts = (first, second)
        self.arrs = first.arrs + second.arrs
        self.out_shape = first.out_shape + second.out_shape
        self.scratch = first.scratch + second.scratch

    def _split(self, ins, outs, sems):
        a, _ = self.parts
        ni, no, ns = len(a.arrs), len(a.out_shape), len(a.scratch)
        return (ins[:ni], outs[:no], sems[:ns]), (ins[ni:], outs[no:], sems[ns:])

    def start(self, ins, outs, sems):
        for part, refs in zip(self.parts, self._split(ins, outs, sems)):
            part.start(*refs)

    def finish(self, ins, outs, sems):
        for part, refs in zip(self.parts, self._split(ins, outs, sems)):
            part.finish(*refs)

    def split_results(self, res):
        no = len(self.parts[0].out_shape)
        return res[:no], res[no:]


def _comm_only(comm, name):
    n_in, n_out = len(comm.arrs), len(comm.out_shape)

    def body(*refs):
        ins, outs, sems = refs[:n_in], refs[n_in:n_in + n_out], refs[n_in + n_out:]
        comm.start(ins, outs, sems)
        comm.finish(ins, outs, sems)

    return pl.pallas_call(
        body, name=name, out_shape=comm.out_shape, in_specs=[ANY] * n_in, out_specs=[ANY] * n_out,
        scratch_shapes=comm.scratch,
    )(*comm.arrs)


def _hosted_call(body, operands, *, name, grid, in_specs, out_specs, out_shape, scratch_shapes=(), sem=(),
                 aliases=None, comm=None):
    single = not isinstance(out_shape, (list, tuple))
    o_specs = [out_specs] if single else list(out_specs)
    o_shape = [out_shape] if single else list(out_shape)
    n_in, n_out, n_scr = len(in_specs), len(o_shape), len(scratch_shapes)
    if comm is None:
        res = pl.pallas_call(body, name=name, grid=grid, in_specs=list(in_specs), out_specs=o_specs,
                             out_shape=o_shape, scratch_shapes=list(scratch_shapes),
                             input_output_aliases=aliases or {}, compiler_params=_cp(*sem))(*operands)
        return (res[0] if single else res), []
    c_in, c_out = len(comm.arrs), len(comm.out_shape)

    def wrapped(*refs):
        ins, cins = refs[:n_in], refs[n_in:n_in + c_in]
        o0 = n_in + c_in
        outs, couts = refs[o0:o0 + n_out], refs[o0 + n_out:o0 + n_out + c_out]
        s0 = o0 + n_out + c_out
        scr, csems = refs[s0:s0 + n_scr], refs[s0 + n_scr:]
        pids = [pl.program_id(a) for a in range(len(grid))]
        first = functools.reduce(jnp.logical_and, [p == 0 for p in pids])
        last = functools.reduce(jnp.logical_and, [p == g - 1 for p, g in zip(pids, grid)])

        @pl.when(first)
        def _():
            comm.start(cins, couts, csems)

        body(*ins, *outs, *scr)

        @pl.when(last)
        def _():
            comm.finish(cins, couts, csems)

    res = pl.pallas_call(
        wrapped, name=name, grid=grid, in_specs=list(in_specs) + [ANY] * c_in, out_specs=o_specs + [ANY] * c_out,
        out_shape=o_shape + comm.out_shape, scratch_shapes=list(scratch_shapes) + comm.scratch,
        input_output_aliases=aliases or {}, compiler_params=_cp(*(["arbitrary"] * len(grid))),
    )(*operands, *comm.arrs)
    return (res[0] if single else res[:n_out]), res[n_out:]


def _adamw(w, g, m, v):
    m2 = ADAM_B1 * m + (1.0 - ADAM_B1) * g
    v2 = ADAM_B2 * v + (1.0 - ADAM_B2) * (g * g)
    m_hat = m2 / (1.0 - ADAM_B1 ** ADAM_STEP)
    v_hat = v2 / (1.0 - ADAM_B2 ** ADAM_STEP)
    delta = -ADAM_LR * (m_hat / (jnp.sqrt(v_hat) + ADAM_EPS) + ADAM_WD * w)
    return delta, m2, v2


def _add_sibling(g, recv, c_idx, name):
    _, R, C = g.shape
    tr = min(R, 512)

    def body(c_ref, g_ref, r_ref, o_ref):
        o_ref[...] = (g_ref[...] + r_ref[...]).astype(BF16)

    return pl.pallas_call(
        body, name=name,
        grid_spec=pltpu.PrefetchScalarGridSpec(
            num_scalar_prefetch=1, grid=(NCHIP, R // tr),
            in_specs=[pl.BlockSpec((1, tr, C), lambda k, i, c_ref: (2 * k + c_ref[0], i, 0)),
                      pl.BlockSpec((1, tr, C), lambda k, i, c_ref: (k, i, 0))],
            out_specs=pl.BlockSpec((1, tr, C), lambda k, i, c_ref: (k, i, 0))),
        out_shape=jax.ShapeDtypeStruct((NCHIP, R, C), BF16),
        compiler_params=_cp("parallel", "parallel"),
    )(c_idx, g, recv)


def _sum_adam(parts, w, m, v, l, prev, name):
    K, R, C = parts.shape
    LR = w.shape[0]
    tr = min(R, 256)
    nb = R // tr

    def body(p_ref, w_ref, m_ref, v_ref, *rest):
        g_ref, d_ref, m2_ref, v2_ref = rest[-4:]
        g = p_ref[0].astype(F32)
        for k in range(1, K):
            g = g + p_ref[k].astype(F32)
        delta, m2, v2 = _adamw(w_ref[...], g, m_ref[...], v_ref[...])
        g_ref[...] = g
        d_ref[...] = delta
        m2_ref[...] = m2
        v2_ref[...] = v2

    blk = pl.BlockSpec((tr, C), lambda i: (l * nb + i, 0))
    shp = jax.ShapeDtypeStruct((LR, C), F32)
    operands = [parts, w, m, v] + (list(prev) if prev is not None else [])
    return pl.pallas_call(
        body, name=name, grid=(nb,),
        in_specs=[pl.BlockSpec((K, tr, C), lambda i: (0, i, 0)), blk, blk, blk] + [ANY] * (len(operands) - 4),
        out_specs=[blk] * 4, out_shape=[shp] * 4,
        input_output_aliases={4 + k: k for k in range(len(operands) - 4)},
        compiler_params=_cp("parallel"),
    )(*operands)


def _sum_rows(parts, name):
    K, R, C = parts.shape
    tr = min(R, 256)
    while R % tr:
        tr //= 2

    def body(p_ref, o_ref):
        g = p_ref[0]
        for k in range(1, K):
            g = g + p_ref[k]
        o_ref[...] = g

    return pl.pallas_call(
        body, name=name, grid=(R // tr,),
        in_specs=[pl.BlockSpec((K, tr, C), lambda i: (0, i, 0))],
        out_specs=pl.BlockSpec((tr, C), lambda i: (i, 0)),
        out_shape=jax.ShapeDtypeStruct((R, C), F32),
        compiler_params=_cp("parallel"),
    )(parts)


def _adam_only(g, w, m, v, name):
    R, C = g.shape
    tr = min(R, 256)
    while R % tr:
        tr //= 2

    def body(g_ref, w_ref, m_ref, v_ref, d_ref, m2_ref, v2_ref):
        delta, m2, v2 = _adamw(w_ref[...], g_ref[...], m_ref[...], v_ref[...])
        d_ref[...] = delta
        m2_ref[...] = m2
        v2_ref[...] = v2

    blk = pl.BlockSpec((tr, C), lambda i: (i, 0))
    shp = jax.ShapeDtypeStruct((R, C), F32)
    return pl.pallas_call(
        body, name=name, grid=(R // tr,), in_specs=[blk] * 4, out_specs=[blk] * 3,
        out_shape=[shp] * 3, compiler_params=_cp("parallel"),
    )(g, w, m, v)


def _mod_fwd(c_all, w_mod, b_cols, name):
    L, D, n = w_mod.shape
    B = c_all.shape[0]

    def body(c_ref, w_ref, b_ref, o_ref):
        cv = c_ref[...]
        sc = (cv * _sigmoid(cv)).astype(BF16)
        o_ref[0] = jnp.dot(sc, w_ref[0].astype(BF16), preferred_element_type=F32) + b_ref[0]

    return pl.pallas_call(
        body, name=name, grid=(L,),
        in_specs=[pl.BlockSpec((B, D), lambda l: (0, 0)),
                  pl.BlockSpec((1, D, n), lambda l: (l, 0, 0)),
                  pl.BlockSpec((1, 1, n), lambda l: (l, 0, 0))],
        out_specs=pl.BlockSpec((1, B, n), lambda l: (l, 0, 0)),
        out_shape=jax.ShapeDtypeStruct((L, B, n), F32),
        compiler_params=_cp("parallel"),
    )(c_all, w_mod, b_cols)


def _wmod_grad_adam(sc_t, dm, w, m, v, name):
    L, D, n = w.shape
    KP = sc_t.shape[1]
    tr = min(D, 512)

    def body(s_ref, dm_ref, w_ref, m_ref, v_ref, g_ref, d_ref, m2_ref, v2_ref):
        g = jnp.dot(s_ref[...], dm_ref[0], preferred_element_type=F32,
                    precision=lax.Precision.HIGHEST)
        delta, m2, v2 = _adamw(w_ref[0], g, m_ref[0], v_ref[0])
        g_ref[0] = g
        d_ref[0] = delta
        m2_ref[0] = m2
        v2_ref[0] = v2

    blk = pl.BlockSpec((1, tr, n), lambda l, i: (l, i, 0))
    shp = jax.ShapeDtypeStruct((L, D, n), F32)
    return pl.pallas_call(
        body, name=name, grid=(L, D // tr),
        in_specs=[pl.BlockSpec((tr, KP), lambda l, i: (i, 0)),
                  pl.BlockSpec((1, KP, n), lambda l, i: (l, 0, 0)), blk, blk, blk],
        out_specs=[blk] * 4, out_shape=[shp] * 4,
        compiler_params=_cp("parallel", "parallel"),
    )(sc_t, dm, w, m, v)


def _vec_spec(D):
    return pl.BlockSpec((1, D), lambda i: (0, 0))


def _pre(x, res, gate, g, scale, shift, name):
    S, D = x.shape
    tr = min(S, ROW_TILE)
    has_res = res is not None
    row = pl.BlockSpec((tr, D), lambda i: (i, 0))

    def body(*refs):
        if has_res:
            x_ref, r_ref, gate_ref, g_ref, sc_ref, sh_ref, xl_ref, h_ref = refs
            xv = x_ref[...] + gate_ref[...] * r_ref[...]
            xl_ref[...] = xv
        else:
            x_ref, g_ref, sc_ref, sh_ref, h_ref = refs
            xv = x_ref[...]
        r = lax.rsqrt(jnp.mean(xv * xv, axis=-1, keepdims=True) + EPS)
        y = (xv * r) * g_ref[...]
        h_ref[...] = (y * (1.0 + sc_ref[...]) + sh_ref[...]).astype(BF16)

    vec = _vec_spec(D)
    if has_res:
        xl, h = pl.pallas_call(
            body, name=name, grid=(S // tr,),
            in_specs=[row, row, vec, vec, vec, vec], out_specs=[row, row],
            out_shape=[jax.ShapeDtypeStruct((S, D), F32), jax.ShapeDtypeStruct((S, D), BF16)],
            compiler_params=_cp("parallel"),
        )(x, res, gate, g, scale, shift)
        return xl, h
    h = pl.pallas_call(
        body, name=name, grid=(S // tr,),
        in_specs=[row, vec, vec, vec], out_specs=row,
        out_shape=jax.ShapeDtypeStruct((S, D), BF16),
        compiler_params=_cp("parallel"),
    )(x, g, scale, shift)
    return x, h


def _pre_bwd(xl, dh, dx_in, g, scale, name, comm=None):
    S, D = xl.shape
    tr = min(S, ROW_TILE)
    nsteps = S // tr
    row = pl.BlockSpec((tr, D), lambda i: (i, 0))
    vec = _vec_spec(D)

    def body(x_ref, dh_ref, dxin_ref, g_ref, sc_ref, dx_ref, dsh_ref, dsc_ref, dg_ref, acc_sh, acc_t):
        i = pl.program_id(0)
        xv = x_ref[...]
        dh = dh_ref[...]
        r = lax.rsqrt(jnp.mean(xv * xv, axis=-1, keepdims=True) + EPS)
        xn = xv * r
        part_sh = jnp.sum(dh.reshape(tr // 8, 8, D), axis=0)
        part_t = jnp.sum((dh * xn).reshape(tr // 8, 8, D), axis=0)

        @pl.when(i == 0)
        def _():
            acc_sh[...] = part_sh
            acc_t[...] = part_t

        @pl.when(i > 0)
        def _():
            acc_sh[...] += part_sh
            acc_t[...] += part_t

        dxn = dh * (g_ref[...] * (1.0 + sc_ref[...]))
        dx_ref[...] = dxin_ref[...] + r * (dxn - xn * jnp.mean(dxn * xn, axis=-1, keepdims=True))

        @pl.when(i == nsteps - 1)
        def _():
            t = jnp.sum(acc_t[...], axis=0, keepdims=True)
            dsh_ref[...] = jnp.sum(acc_sh[...], axis=0, keepdims=True)
            dsc_ref[...] = t * g_ref[...]
            dg_ref[...] = t * (1.0 + sc_ref[...])

    v = jax.ShapeDtypeStruct((1, D), F32)
    return _hosted_call(
        body, [xl, dh, dx_in, g, scale], name=name, grid=(nsteps,),
        in_specs=[row, row, row, vec, vec], out_specs=[row, vec, vec, vec],
        out_shape=[jax.ShapeDtypeStruct((S, D), F32), v, v, v],
        scratch_shapes=[pltpu.VMEM((8, D), F32), pltpu.VMEM((8, D), F32)],
        sem=("arbitrary",), comm=comm)


def _post_bwd(dx, out, gate, name):
    S, D = dx.shape
    tr = min(S, ROW_TILE)
    nsteps = S // tr
    row = pl.BlockSpec((tr, D), lambda i: (i, 0))
    vec = _vec_spec(D)

    def body(dx_ref, o_ref, gate_ref, do_ref, dg_ref, acc):
        i = pl.program_id(0)
        dxv = dx_ref[...]
        do_ref[...] = (dxv * gate_ref[...]).astype(BF16)
        part = jnp.sum((dxv * o_ref[...]).reshape(tr // 8, 8, D), axis=0)

        @pl.when(i == 0)
        def _():
            acc[...] = part

        @pl.when(i > 0)
        def _():
            acc[...] += part

        @pl.when(i == nsteps - 1)
        def _():
            dg_ref[...] = jnp.sum(acc[...], axis=0, keepdims=True)

    return pl.pallas_call(
        body, name=name, grid=(nsteps,),
        in_specs=[row, row, vec], out_specs=[row, vec],
        out_shape=[jax.ShapeDtypeStruct((S, D), BF16), jax.ShapeDtypeStruct((1, D), F32)],
        scratch_shapes=[pltpu.VMEM((8, D), F32)],
        compiler_params=_cp("arbitrary"),
    )(dx, out, gate)


def _loss_head(x, res, gate, gf, tgt, name):
    S, D = x.shape
    tr = min(S, ROW_TILE)
    nsteps = S // tr
    row = pl.BlockSpec((tr, D), lambda i: (i, 0))
    vec = _vec_spec(D)

    def body(x_ref, r_ref, gate_ref, gf_ref, t_ref, dx_ref, loss_ref, dgf_ref, acc, lacc):
        i = pl.program_id(0)
        xv = x_ref[...] + gate_ref[...] * r_ref[...]
        r = lax.rsqrt(jnp.mean(xv * xv, axis=-1, keepdims=True) + EPS)
        xn = xv * r
        err = xn * gf_ref[...] - t_ref[...]
        row_loss = jnp.mean(err * err, axis=-1, keepdims=True)
        lpart = 0.5 * jnp.sum(row_loss, axis=0, keepdims=True)
        dy = err * (1.0 / D)
        part = jnp.sum((dy * xn).reshape(tr // 8, 8, D), axis=0)

        @pl.when(i == 0)
        def _():
            acc[...] = part
            lacc[...] = lpart

        @pl.when(i > 0)
        def _():
            acc[...] += part
            lacc[...] += lpart

        dxn = dy * gf_ref[...]
        dx_ref[...] = r * (dxn - xn * jnp.mean(dxn * xn, axis=-1, keepdims=True))

        @pl.when(i == nsteps - 1)
        def _():
            dgf_ref[...] = jnp.sum(acc[...], axis=0, keepdims=True)
            loss_ref[...] = lacc[...]

    return pl.pallas_call(
        body, name=name, grid=(nsteps,),
        in_specs=[row, row, vec, vec, row],
        out_specs=[row, pl.BlockSpec((1, 1), lambda i: (0, 0)), vec],
        out_shape=[jax.ShapeDtypeStruct((S, D), F32), jax.ShapeDtypeStruct((1, 1), F32),
                   jax.ShapeDtypeStruct((1, D), F32)],
        scratch_shapes=[pltpu.VMEM((8, D), F32), pltpu.VMEM((1, 1), F32)],
        compiler_params=_cp("arbitrary"),
    )(x, res, gate, gf, tgt)


NN = (((1,), (0,)), ((), ()))
NT = (((1,), (1,)), ((), ()))
TN = (((0,), (0,)), ((), ()))


def _mm(name, a, b, out_shape, grid, a_spec, b_spec, o_spec, dims, a2d, b2d, k_axis, sem, alias=None, comm=None):
    def body(*refs):
        a_ref, b_ref, o_ref = refs[0], refs[1], refs[-1]
        r = lax.dot_general(a_ref[...].reshape(a2d), b_ref[...].reshape(b2d), dims,
                            preferred_element_type=F32)
        r = r.reshape(o_ref.shape)
        if k_axis is None:
            o_ref[...] = r.astype(o_ref.dtype)
        else:
            k = pl.program_id(k_axis)

            @pl.when(k == 0)
            def _():
                o_ref[...] = r

            @pl.when(k > 0)
            def _():
                o_ref[...] += r

    operands, in_specs, aliases = [a, b], [a_spec, b_spec], {}
    if alias is not None:
        operands.append(alias)
        in_specs.append(ANY)
        aliases = {2: 0}
    res, extra = _hosted_call(body, operands, name=name, grid=grid, in_specs=in_specs, out_specs=o_spec,
                              out_shape=out_shape, sem=sem, aliases=aliases, comm=comm)
    return res if comm is None else (res, extra)


def _tile(n, pref):
    t = min(n, pref)
    while n % t:
        t -= 128
    return t


def _mm_nn_in(a, w, l, name, comm=None):
    M, K = a.shape
    _, _, _, n = w.shape
    tm, tn = min(M, 512), _tile(n, 1024)
    nb = n // tn
    return _mm(name, a, w, jax.ShapeDtypeStruct((M, NDEV * n), F32), (NDEV * nb, M // tm),
               pl.BlockSpec((tm, K), lambda j, i: (i, 0)),
               pl.BlockSpec((1, 1, K, tn), lambda j, i: (j // nb, l, 0, j % nb)),
               pl.BlockSpec((tm, tn), lambda j, i: (i, j)),
               NN, (tm, K), (K, tn), None, ("parallel", "parallel"), comm=comm)


def _mm_nn_out(a, w, l, name):
    M, K = a.shape
    _, _, kb, N = w.shape
    tm, tn = min(M, 512), _tile(N, 1024)
    return _mm(name, a, w, jax.ShapeDtypeStruct((M, N), F32), (N // tn, M // tm),
               pl.BlockSpec((tm, K), lambda j, i: (i, 0)),
               pl.BlockSpec((NDEV, 1, kb, tn), lambda j, i: (0, l, 0, j)),
               pl.BlockSpec((tm, tn), lambda j, i: (i, j)),
               NN, (tm, K), (K, tn), None, ("parallel", "parallel"))


def _mm_nt_in(a, w, l, name, comm=None):
    M, _ = a.shape
    _, _, K, n = w.shape
    tm, tk = min(M, 1024), _tile(K, 1024)
    gb = 2 if n <= 1024 else 1

    def body(a_ref, w_ref, o_ref):
        k = pl.program_id(2)
        r = lax.dot_general(a_ref[:, :n], w_ref[0, 0], NT, preferred_element_type=F32)
        for g in range(1, gb):
            r = r + lax.dot_general(a_ref[:, g * n:(g + 1) * n], w_ref[g, 0], NT, preferred_element_type=F32)

        @pl.when(k == 0)
        def _():
            o_ref[...] = r

        @pl.when(k > 0)
        def _():
            o_ref[...] += r

    res, extra = _hosted_call(
        body, [a, w], name=name, grid=(M // tm, K // tk, NDEV // gb),
        in_specs=[pl.BlockSpec((tm, gb * n), lambda i, j, k: (i, k)),
                  pl.BlockSpec((gb, 1, tk, n), lambda i, j, k: (k, l, j, 0))],
        out_specs=pl.BlockSpec((tm, tk), lambda i, j, k: (i, j)),
        out_shape=jax.ShapeDtypeStruct((M, K), F32),
        sem=("parallel", "parallel", "arbitrary"), comm=comm)
    return res if comm is None else (res, extra)


def _mm_nt_out(a, w, l, name):
    M, N = a.shape
    _, _, kb, _ = w.shape
    K = NDEV * kb
    tm, tk, tc = min(M, 1024), _tile(K, 1024), _tile(N, 1024)
    per = tk // kb
    return _mm(name, a, w, jax.ShapeDtypeStruct((M, K), F32), (M // tm, K // tk, N // tc),
               pl.BlockSpec((tm, tc), lambda i, j, k: (i, k)),
               pl.BlockSpec((per, 1, kb, tc), lambda i, j, k: (j, l, 0, k)),
               pl.BlockSpec((tm, tk), lambda i, j, k: (i, j)),
               NT, (tm, tc), (tk, tc), 2, ("parallel", "parallel", "arbitrary"))


def _mm_tn_in(a, b, l, L, buf, name, comm=None):
    S, K = a.shape
    n = b.shape[1] // NDEV
    ts, tk, tn = min(S, 2048), _tile(K, 1024), _tile(n, 1024)
    nb = n // tn
    return _mm(name, a, b, jax.ShapeDtypeStruct((NDEV, L, K, n), F32), (NDEV * nb, K // tk, S // ts),
               pl.BlockSpec((ts, tk), lambda j, i, s: (s, i)),
               pl.BlockSpec((ts, tn), lambda j, i, s: (s, j)),
               pl.BlockSpec((1, 1, tk, tn), lambda j, i, s: (j // nb, l, i, j % nb)),
               TN, (ts, tk), (ts, tn), 2, ("parallel", "parallel", "arbitrary"), alias=buf, comm=comm)


def _mm_tn_out(a, b, l, L, buf, name):
    S, K = a.shape
    N = b.shape[1]
    kb = K // NDEV
    ts, tk, tn = min(S, 2048), _tile(K, 1024), _tile(N, 1024)
    per = tk // kb
    return _mm(name, a, b, jax.ShapeDtypeStruct((NDEV, L, kb, N), F32), (N // tn, K // tk, S // ts),
               pl.BlockSpec((ts, tk), lambda j, i, s: (s, i)),
               pl.BlockSpec((ts, tn), lambda j, i, s: (s, j)),
               pl.BlockSpec((per, 1, kb, tn), lambda j, i, s: (i, l, 0, j)),
               TN, (ts, tk), (ts, tn), 2, ("parallel", "parallel", "arbitrary"), alias=buf)


def _attn_bias(T):
    reach = max(w // 2 for w, _ in DILATED_PATTERNS)
    hb = -(-reach // T)
    i = np.arange(T)[:, None]
    j = np.arange(T)[None, :]
    tiles = []
    for d in range(-hb, hb + 1):
        rel = j + d * T - i
        mult = np.zeros((T, T), np.float64)
        for window, dil in DILATED_PATTERNS:
            radius = window // (2 * dil)
            mult += (rel % dil == 0) & (np.abs(rel) <= radius * dil)
        tiles.append(np.where(mult > 0, np.log(np.maximum(mult, 1.0)), NEG_INF))
    return jnp.asarray(np.stack(tiles), F32)


def _rope_tables(S):
    half = HEAD_DIM // 2
    pos = jnp.arange(S, dtype=F32)
    inv = ROPE_THETA ** (-jnp.arange(half, dtype=F32) / half)
    ang = pos[:, None] * inv[None, :]
    cos, sin = jnp.cos(ang), jnp.sin(ang)
    return jnp.concatenate([cos, cos], axis=-1), jnp.concatenate([-sin, sin], axis=-1)


def _rope_apply(t, cosf, sinf, heads, sign):
    outs = []
    for hh in range(heads):
        th = t[:, hh * HEAD_DIM:(hh + 1) * HEAD_DIM]
        outs.append(th * cosf + sign * (pltpu.roll(th, HEAD_DIM // 2, 1) * sinf))
    return outs


def _rope_qkv(proj, cosf, sinf, W, name):
    S = proj.shape[0]
    tr = min(S, ROW_TILE)
    heads = W // HEAD_DIM

    def body(q_ref, k_ref, v_ref, c_ref, s_ref, qo_ref, ko_ref, vo_ref):
        cosf_v, sinf_v = c_ref[...], s_ref[...]
        for src, dst, mult in ((q_ref, qo_ref, HEAD_DIM ** -0.5), (k_ref, ko_ref, 1.0)):
            for hh, val in enumerate(_rope_apply(src[...], cosf_v, sinf_v, heads, 1.0)):
                dst[:, hh * HEAD_DIM:(hh + 1) * HEAD_DIM] = (val * mult).astype(BF16)
        vo_ref[...] = v_ref[...].astype(BF16)

    piece = lambda p: pl.BlockSpec((tr, W), lambda i: (i, p))
    tab = pl.BlockSpec((tr, HEAD_DIM), lambda i: (i, 0))
    out = pl.BlockSpec((tr, W), lambda i: (i, 0))
    shp = jax.ShapeDtypeStruct((S, W), BF16)
    return pl.pallas_call(
        body, name=name, grid=(S // tr,),
        in_specs=[piece(0), piece(1), piece(2), tab, tab], out_specs=[out] * 3, out_shape=[shp] * 3,
        compiler_params=_cp("parallel"),
    )(proj, proj, proj, cosf, sinf)


def _attn_fwd(q, k, v, bias, name, comm=None):
    S, W = q.shape
    H = W // HEAD_DIM
    nd, T, _ = bias.shape
    hb, nq = nd // 2, S // T
    scale = HEAD_DIM ** -0.5
    hp = min(H, HEADS_PER_STEP)
    rc = min(T, ATTN_ROW_CHUNK)
    wp = hp * HEAD_DIM

    def body(q_ref, k_ref, v_ref, b_ref, o_ref, lse_ref, m_s, l_s, acc_s):
        i, d = pl.program_id(1), pl.program_id(2)
        j = i + d - hb

        @pl.when(d == 0)
        def _():
            m_s[...] = jnp.full(m_s.shape, -jnp.inf, F32)
            l_s[...] = jnp.zeros(l_s.shape, F32)
            acc_s[...] = jnp.zeros(acc_s.shape, F32)

        @pl.when((j >= 0) & (j < nq))
        def _():
            items = [(hh, c) for hh in range(hp) for c in range(T // rc)]

            def scores(item):
                hh, c = item
                cols, rows = slice(hh * HEAD_DIM, (hh + 1) * HEAD_DIM), slice(c * rc, (c + 1) * rc)
                return (lax.dot_general(q_ref[rows, cols], k_ref[:, cols], NT, preferred_element_type=F32)
                        + b_ref[d, rows, :])

            def weighted_values(item, p, alpha):
                hh, c = item
                cols, rows = slice(hh * HEAD_DIM, (hh + 1) * HEAD_DIM), slice(c * rc, (c + 1) * rc)
                acc_s[rows, cols] = alpha * acc_s[rows, cols] + jnp.dot(p, v_ref[:, cols],
                                                                        preferred_element_type=F32)

            s_next, pending = scores(items[0]), None
            for n, (hh, c) in enumerate(items):
                rows = slice(c * rc, (c + 1) * rc)
                s = s_next
                if n + 1 < len(items):
                    s_next = scores(items[n + 1])
                if pending is not None:
                    weighted_values(*pending)
                parts = [s[:, t * LANES:(t + 1) * LANES] for t in range(T // LANES)]
                m_old = m_s[hh, rows, :]
                m_cur = jnp.max(functools.reduce(jnp.maximum, parts), axis=1, keepdims=True)
                m_new = jnp.maximum(m_old, m_cur)
                alpha = jnp.exp(m_old - m_new)
                ps = [jnp.exp(part - m_new) for part in parts]
                l_s[hh, rows, :] = alpha * l_s[hh, rows, :] + functools.reduce(jnp.add, ps)
                m_s[hh, rows, :] = m_new
                pending = ((hh, c), jnp.concatenate(ps, axis=1).astype(BF16), alpha)
            weighted_values(*pending)

        @pl.when(d == nd - 1)
        def _():
            for hh in range(hp):
                cols = slice(hh * HEAD_DIM, (hh + 1) * HEAD_DIM)
                l = jnp.sum(l_s[hh], axis=1, keepdims=True)
                o_ref[:, cols] = acc_s[:, cols] / l
                lse_ref[hh] = m_s[hh][:, :1] + jnp.log(l)

    kv = pl.BlockSpec((T, wp), lambda h, i, d: (jnp.clip(i + d - hb, 0, nq - 1), h))
    return _hosted_call(
        body, [q, k, v, bias], name=name, grid=(H // hp, nq, nd),
        in_specs=[pl.BlockSpec((T, wp), lambda h, i, d: (i, h)), kv, kv,
                  pl.BlockSpec((nd, T, T), lambda h, i, d: (0, 0, 0))],
        out_specs=[pl.BlockSpec((T, wp), lambda h, i, d: (i, h)),
                   pl.BlockSpec((hp, T, 1), lambda h, i, d: (h, i, 0))],
        out_shape=[jax.ShapeDtypeStruct((S, W), F32), jax.ShapeDtypeStruct((H, S, 1), F32)],
        scratch_shapes=[pltpu.VMEM((hp, T, LANES), F32), pltpu.VMEM((hp, T, LANES), F32),
                        pltpu.VMEM((T, wp), F32)],
        sem=("parallel", "parallel", "arbitrary"), comm=comm)


def _attn_bwd(q, k, v, do, lse, delta, bias, name, comm=None):
    S, W = q.shape
    H = W // HEAD_DIM
    nd, T, _ = bias.shape
    hb, nq = nd // 2, S // T
    scale = HEAD_DIM ** -0.5
    hp = min(H, HEADS_PER_STEP)
    rc = min(T, ATTN_ROW_CHUNK)
    wp = hp * HEAD_DIM

    def body(q_ref, do_ref, lse_ref, dl_ref, k_ref, v_ref, b_ref, dq_ref, dk_ref, dv_ref):
        j, d = pl.program_id(1), pl.program_id(2)
        i = j + d - hb

        @pl.when((j == 0) & (d == 0))
        def _():
            dq_ref[...] = jnp.zeros(dq_ref.shape, F32)

        @pl.when(d == 0)
        def _():
            dk_ref[...] = jnp.zeros(dk_ref.shape, F32)
            dv_ref[...] = jnp.zeros(dv_ref.shape, F32)

        @pl.when((i >= 0) & (i < nq))
        def _():
            items = [(hh, c) for hh in range(hp) for c in range(T // rc)]

            def slices(item):
                hh, c = item
                return slice(hh * HEAD_DIM, (hh + 1) * HEAD_DIM), slice(c * rc, (c + 1) * rc)

            def products(item):
                cols, rows = slices(item)
                s = (lax.dot_general(q_ref[rows, cols], k_ref[:, cols], NT, preferred_element_type=F32)
                     + b_ref[nd - 1 - d, rows, :])
                dp = lax.dot_general(do_ref[rows, cols], v_ref[:, cols], NT, preferred_element_type=F32)
                return s, dp

            def gradients(item, p, ds):
                cols, rows = slices(item)
                dv_ref[:, cols] += lax.dot_general(p, do_ref[rows, cols], TN, preferred_element_type=F32)
                dk_ref[:, cols] += lax.dot_general(ds, q_ref[rows, cols], TN, preferred_element_type=F32)
                q_rows = pl.ds(pl.multiple_of(i * T + item[1] * rc, rc), rc)
                dq_ref[q_rows, cols] += jnp.dot(ds, k_ref[:, cols], preferred_element_type=F32) * scale

            nxt, pending = products(items[0]), None
            for n, item in enumerate(items):
                s, dp = nxt
                if n + 1 < len(items):
                    nxt = products(items[n + 1])
                if pending is not None:
                    gradients(*pending)
                _, rows = slices(item)
                p = jnp.exp(s - lse_ref[item[0], rows, :])
                ds = p * (dp - dl_ref[item[0], rows, :])
                pending = (item, p.astype(BF16), ds.astype(BF16))
            gradients(*pending)

    qi = lambda h, j, d: (jnp.clip(j + d - hb, 0, nq - 1), h)
    qs = pl.BlockSpec((T, wp), qi)
    col = pl.BlockSpec((hp, T, 1), lambda h, j, d: (h, jnp.clip(j + d - hb, 0, nq - 1), 0))
    kv = pl.BlockSpec((T, wp), lambda h, j, d: (j, h))
    shp = jax.ShapeDtypeStruct((S, W), F32)
    return _hosted_call(
        body, [q, do, lse, delta, k, v, bias], name=name, grid=(H // hp, nq, nd),
        in_specs=[qs, qs, col, col, kv, kv, pl.BlockSpec((nd, T, T), lambda h, j, d: (0, 0, 0))],
        out_specs=[pl.BlockSpec((S, wp), lambda h, j, d: (0, h)), kv, kv],
        out_shape=[shp, shp, shp],
        sem=("parallel", "arbitrary", "arbitrary"), comm=comm)


def _halo_specs(S, tr, W, piece):
    per, last = tr // 8, S // 8 - 1
    prev = pl.BlockSpec((8, W), lambda i: (jnp.maximum(i * per - 1, 0), piece))
    nxt = pl.BlockSpec((8, W), lambda i: (jnp.minimum((i + 1) * per, last), piece))
    return prev, nxt


def _shifted(t, before, after, tr):
    rows = lax.broadcasted_iota(jnp.int32, (tr, 1), 0)
    prev = jnp.where(rows == 0, before, pltpu.roll(t, 1, 0))
    nxt = jnp.where(rows == tr - 1, after, pltpu.roll(t, tr - 1, 0))
    return prev, nxt


def _ab_mix(attn, proj, conv_w, W, name):
    S = attn.shape[0]
    tr = min(S, ROW_TILE)
    nsteps = S // tr

    def body(a_ref, za_ref, ub_ref, gb_ref, gc_ref, zb_ref, ubp, ubn, gcp, gcn, w_ref, y_ref):
        i = pl.program_id(0)
        t = gc_ref[...] * ub_ref[...]
        before = jnp.where(i == 0, 0.0, (gcp[...] * ubp[...])[7:8, :])
        after = jnp.where(i == nsteps - 1, 0.0, (gcn[...] * ubn[...])[0:1, :])
        t_prev, t_next = _shifted(t, before, after, tr)
        w = w_ref[...]
        cv = w[0:1, :] * t_prev + w[1:2, :] * t + w[2:3, :] * t_next
        silu_a, _ = _silu_and_grad(za_ref[...])
        silu_b, _ = _silu_and_grad(zb_ref[...])
        y_ref[:, :W] = (a_ref[...] * silu_a).astype(BF16)
        y_ref[:, W:] = (gb_ref[...] * cv * silu_b).astype(BF16)

    piece = lambda p: pl.BlockSpec((tr, W), lambda i: (i, p))
    ubp, ubn = _halo_specs(S, tr, W, 4)
    gcp, gcn = _halo_specs(S, tr, W, 6)
    return pl.pallas_call(
        body, name=name, grid=(nsteps,),
        in_specs=[pl.BlockSpec((tr, W), lambda i: (i, 0)), piece(3), piece(4), piece(5), piece(6), piece(7),
                  ubp, ubn, gcp, gcn, pl.BlockSpec((3, W), lambda i: (0, 0))],
        out_specs=pl.BlockSpec((tr, 2 * W), lambda i: (i, 0)),
        out_shape=jax.ShapeDtypeStruct((S, 2 * W), BF16),
        compiler_params=_cp("parallel"),
    )(attn, proj, proj, proj, proj, proj, proj, proj, proj, proj, conv_w)


def _dattn_prep(dy, proj, attn, W, name):
    S = attn.shape[0]
    H = W // HEAD_DIM
    tr = min(S, ROW_TILE)

    def body(dy_ref, za_ref, a_ref, do_ref, dl_ref):
        silu_a, _ = _silu_and_grad(za_ref[...])
        do = dy_ref[...] * silu_a
        do_ref[...] = do.astype(BF16)
        prod = do * a_ref[...]
        for hh in range(H):
            dl_ref[hh] = jnp.sum(prod[:, hh * HEAD_DIM:(hh + 1) * HEAD_DIM], axis=1, keepdims=True)

    row = pl.BlockSpec((tr, W), lambda i: (i, 0))
    return pl.pallas_call(
        body, name=name, grid=(S // tr,),
        in_specs=[row, pl.BlockSpec((tr, W), lambda i: (i, 3)), row],
        out_specs=[row, pl.BlockSpec((H, tr, 1), lambda i: (0, i, 0))],
        out_shape=[jax.ShapeDtypeStruct((S, W), BF16), jax.ShapeDtypeStruct((H, S, 1), F32)],
        compiler_params=_cp("parallel"),
    )(dy, proj, attn)


def _ab_bwd(dy, attn, proj, dqr, dkr, dv, cosf, sinf, conv_w, W, name):
    S = attn.shape[0]
    tr = min(S, ROW_TILE // 2)
    nsteps = S // tr
    heads = W // HEAD_DIM

    def body(dya_ref, dyb_ref, a_ref, za_ref, ub_ref, gb_ref, gc_ref, zb_ref, dq_ref, dk_ref, dv_ref,
             c_ref, s_ref, w_ref, dybp, dybn, gbp, gbn, zbp, zbn, ubp, ubn, gcp, gcn,
             dp_ref, dw_ref, acc):
        i = pl.program_id(0)
        first, last = i == 0, i == nsteps - 1
        w = w_ref[...]
        w0, w1, w2 = w[0:1, :], w[1:2, :], w[2:3, :]
        ub, gb, gc, zb = ub_ref[...], gb_ref[...], gc_ref[...], zb_ref[...]
        dyb = dyb_ref[...]
        silu_a, dsilu_a = _silu_and_grad(za_ref[...])
        silu_b, dsilu_b = _silu_and_grad(zb)
        t = gc * ub
        t_prev, t_next = _shifted(t, jnp.where(first, 0.0, (gcp[...] * ubp[...])[7:8, :]),
                                  jnp.where(last, 0.0, (gcn[...] * ubn[...])[0:1, :]), tr)
        cv = w0 * t_prev + w1 * t + w2 * t_next
        dcv = dyb * gb * silu_b
        halo_p = dybp[...] * gbp[...] * _silu_and_grad(zbp[...])[0]
        halo_n = dybn[...] * gbn[...] * _silu_and_grad(zbn[...])[0]
        dcv_prev, dcv_next = _shifted(dcv, jnp.where(first, 0.0, halo_p[7:8, :]),
                                      jnp.where(last, 0.0, halo_n[0:1, :]), tr)
        dt = w0 * dcv_next + w1 * dcv + w2 * dcv_prev
        cosf_v, sinf_v = c_ref[...], s_ref[...]
        for src, base in ((dq_ref, 0), (dk_ref, W)):
            for hh, val in enumerate(_rope_apply(src[...], cosf_v, sinf_v, heads, -1.0)):
                dp_ref[:, base + hh * HEAD_DIM:base + (hh + 1) * HEAD_DIM] = val.astype(BF16)
        dp_ref[:, 2 * W:3 * W] = dv_ref[...].astype(BF16)
        dp_ref[:, 3 * W:4 * W] = (dya_ref[...] * a_ref[...] * dsilu_a).astype(BF16)
        dp_ref[:, 4 * W:5 * W] = (dt * gc).astype(BF16)
        dp_ref[:, 5 * W:6 * W] = (dyb * cv * silu_b).astype(BF16)
        dp_ref[:, 6 * W:7 * W] = (dt * ub).astype(BF16)
        dp_ref[:, 7 * W:8 * W] = (dyb * gb * cv * dsilu_b).astype(BF16)
        tap = lax.broadcasted_iota(jnp.int32, (8, 1), 0)
        part = (jnp.where(tap == 0, jnp.sum(dcv * t_prev, axis=0, keepdims=True), 0.0)
                + jnp.where(tap == 1, jnp.sum(dcv * t, axis=0, keepdims=True), 0.0)
                + jnp.where(tap == 2, jnp.sum(dcv * t_next, axis=0, keepdims=True), 0.0))

        @pl.when(first)
        def _():
            acc[...] = part

        @pl.when(i > 0)
        def _():
            acc[...] += part

        @pl.when(last)
        def _():
            dw_ref[...] = acc[...]

    row = pl.BlockSpec((tr, W), lambda i: (i, 0))
    piece = lambda p: pl.BlockSpec((tr, W), lambda i: (i, p))
    tab = pl.BlockSpec((tr, HEAD_DIM), lambda i: (i, 0))
    dybp, dybn = _halo_specs(S, tr, W, 1)
    gbp, gbn = _halo_specs(S, tr, W, 5)
    zbp, zbn = _halo_specs(S, tr, W, 7)
    ubp, ubn = _halo_specs(S, tr, W, 4)
    gcp, gcn = _halo_specs(S, tr, W, 6)
    return pl.pallas_call(
        body, name=name, grid=(nsteps,),
        in_specs=[piece(0), piece(1), row, piece(3), piece(4), piece(5), piece(6), piece(7), row, row, row,
                  tab, tab, pl.BlockSpec((3, W), lambda i: (0, 0)),
                  dybp, dybn, gbp, gbn, zbp, zbn, ubp, ubn, gcp, gcn],
        out_specs=[pl.BlockSpec((tr, 8 * W), lambda i: (i, 0)), pl.BlockSpec((8, W), lambda i: (0, 0))],
        out_shape=[jax.ShapeDtypeStruct((S, 8 * W), BF16), jax.ShapeDtypeStruct((8, W), F32)],
        scratch_shapes=[pltpu.VMEM((8, W), F32)],
        compiler_params=_cp("arbitrary"),
    )(dy, dy, attn, proj, proj, proj, proj, proj, dqr, dkr, dv, cosf, sinf, conv_w,
      dy, dy, proj, proj, proj, proj, proj, proj, proj, proj)


def _sgu_core(p_ref, lng_ref, lnb_ref, ws_ref, bst_ref, Dc):
    gw = Dc // C_GROUPS
    u_raw, v_raw, z = p_ref[:, :Dc], p_ref[:, Dc:2 * Dc], p_ref[:, 2 * Dc:]
    u, du = _gelu_and_grad(u_raw)
    vg, dvg = _gelu_and_grad(v_raw)
    mu = jnp.mean(vg, axis=-1, keepdims=True)
    vc = vg - mu
    rstd = lax.rsqrt(jnp.mean(vc * vc, axis=-1, keepdims=True) + EPS)
    vhat = vc * rstd
    vn = (vhat * lng_ref[...] + lnb_ref[...]).astype(BF16)
    bst = bst_ref[...]
    mixed = jnp.concatenate(
        [jnp.dot(ws_ref[g].astype(BF16), vn[:, g * gw:(g + 1) * gw], preferred_element_type=F32)
         + bst[:, g:g + 1] for g in range(C_GROUPS)], axis=1)
    sz, dsz = _silu_and_grad(z)
    return u, du, dvg, rstd, vhat, vn, mixed, sz, dsz


def _sgu_fwd(proj, ln_g, ln_b, w_s, b_st, name, comm=None):
    S, Dc3 = proj.shape
    Dc = Dc3 // 3
    vec = pl.BlockSpec((1, Dc), lambda i: (0, 0))

    def body(p_ref, lng_ref, lnb_ref, ws_ref, bst_ref, y_ref):
        u, _, _, _, _, _, mixed, sz, _ = _sgu_core(p_ref, lng_ref, lnb_ref, ws_ref, bst_ref, Dc)
        y_ref[...] = (u * mixed * sz).astype(BF16)

    return _hosted_call(
        body, [proj, ln_g, ln_b, w_s, b_st], name=name, grid=(S // C_CHUNK,),
        in_specs=[pl.BlockSpec((C_CHUNK, Dc3), lambda i: (i, 0)), vec, vec,
                  pl.BlockSpec((C_GROUPS, C_CHUNK, C_CHUNK), lambda i: (0, 0, 0)),
                  pl.BlockSpec((C_CHUNK, C_GROUPS), lambda i: (0, 0))],
        out_specs=pl.BlockSpec((C_CHUNK, Dc), lambda i: (i, 0)),
        out_shape=jax.ShapeDtypeStruct((S, Dc), BF16),
        sem=("parallel",), comm=comm)


def _sgu_bwd(proj, dy, ln_g, ln_b, w_s, w_st, b_st, name):
    S, Dc3 = proj.shape
    Dc = Dc3 // 3
    gw = Dc // C_GROUPS
    nsteps = S // C_CHUNK
    vec = pl.BlockSpec((1, Dc), lambda i: (0, 0))
    wspec = pl.BlockSpec((C_GROUPS, C_CHUNK, C_CHUNK), lambda i: (0, 0, 0))

    def body(p_ref, dy_ref, lng_ref, lnb_ref, ws_ref, wst_ref, bst_ref,
             dp_ref, dws_ref, dbs_ref, dlg_ref, dlb_ref, acc_w, acc_b, acc_g, acc_lb):
        i = pl.program_id(0)
        u, du, dvg, rstd, vhat, vn, mixed, sz, dsz = _sgu_core(p_ref, lng_ref, lnb_ref, ws_ref, bst_ref, Dc)
        dy = dy_ref[...]
        dmixed = dy * u * sz
        dmb = dmixed.astype(BF16)

        @pl.when(i == 0)
        def _():
            acc_w[...] = jnp.zeros(acc_w.shape, F32)
            acc_b[...] = jnp.zeros(acc_b.shape, F32)
            acc_g[...] = jnp.zeros(acc_g.shape, F32)
            acc_lb[...] = jnp.zeros(acc_lb.shape, F32)

        dvn_parts = []
        for g in range(C_GROUPS):
            dmg = dmb[:, g * gw:(g + 1) * gw]
            acc_w[g] += lax.dot_general(dmg, vn[:, g * gw:(g + 1) * gw], NT, preferred_element_type=F32)
            acc_b[g] += dmixed[:, g * gw:(g + 1) * gw]
            dvn_parts.append(jnp.dot(wst_ref[g].astype(BF16), dmg, preferred_element_type=F32))
        dvn = jnp.concatenate(dvn_parts, axis=1)
        acc_g[...] += jnp.sum((dvn * vhat).reshape(C_CHUNK // 8, 8, Dc), axis=0)
        acc_lb[...] += jnp.sum(dvn.reshape(C_CHUNK // 8, 8, Dc), axis=0)
        dvh = dvn * lng_ref[...]
        dvgelu = rstd * (dvh - jnp.mean(dvh, axis=-1, keepdims=True)
                         - vhat * jnp.mean(dvh * vhat, axis=-1, keepdims=True))
        dp_ref[:, :Dc] = (dy * mixed * sz * du).astype(BF16)
        dp_ref[:, Dc:2 * Dc] = (dvgelu * dvg).astype(BF16)
        dp_ref[:, 2 * Dc:] = (dy * u * mixed * dsz).astype(BF16)

        @pl.when(i == nsteps - 1)
        def _():
            dws_ref[...] = acc_w[...]
            for g in range(C_GROUPS):
                dbs_ref[g] = jnp.sum(acc_b[g], axis=1, keepdims=True)
            dlg_ref[...] = jnp.sum(acc_g[...], axis=0, keepdims=True)
            dlb_ref[...] = jnp.sum(acc_lb[...], axis=0, keepdims=True)

    v = jax.ShapeDtypeStruct((1, Dc), F32)
    return pl.pallas_call(
        body, name=name, grid=(nsteps,),
        in_specs=[pl.BlockSpec((C_CHUNK, Dc3), lambda i: (i, 0)), pl.BlockSpec((C_CHUNK, Dc), lambda i: (i, 0)),
                  vec, vec, wspec, wspec, pl.BlockSpec((C_CHUNK, C_GROUPS), lambda i: (0, 0))],
        out_specs=[pl.BlockSpec((C_CHUNK, Dc3), lambda i: (i, 0)), wspec,
                   pl.BlockSpec((C_GROUPS, C_CHUNK, 1), lambda i: (0, 0, 0)), vec, vec],
        out_shape=[jax.ShapeDtypeStruct((S, Dc3), BF16),
                   jax.ShapeDtypeStruct((C_GROUPS, C_CHUNK, C_CHUNK), F32),
                   jax.ShapeDtypeStruct((C_GROUPS, C_CHUNK, 1), F32), v, v],
        scratch_shapes=[pltpu.VMEM((C_GROUPS, C_CHUNK, C_CHUNK), F32), pltpu.VMEM((C_GROUPS, C_CHUNK, gw), F32),
                        pltpu.VMEM((8, Dc), F32), pltpu.VMEM((8, Dc), F32)],
        compiler_params=_cp("arbitrary"),
    )(proj, dy, ln_g, ln_b, w_s, w_st, b_st)


PACK_COLS = 1024
PACK_ROWS = 64


def _pack(vectors):
    flat = jnp.concatenate([v.reshape(-1) for v in vectors])
    pad = (-flat.shape[0]) % (PACK_COLS * PACK_ROWS)
    return jnp.pad(flat, (0, pad)).reshape(-1, PACK_COLS)


def _unshard(g, off, shape):
    L, rest = shape[0], shape[1:]
    size = int(np.prod(shape))
    piece = g[:, off:off + size].reshape((NDEV,) + tuple(shape))
    nd = piece.ndim
    perm = tuple(range(1, nd - 1)) + (0, nd - 1)
    full = jnp.transpose(piece, perm)
    return full.reshape(tuple(shape[:-1]) + (NDEV * shape[-1],)), off + size


def kernel(x, c, ab_norm_g, ab_w_mod, ab_b_mod, ab_w_in, ab_conv_w, ab_w_out, sg_norm_g, sg_w_mod, sg_b_mod, sg_w_in, sg_ln_g, sg_ln_b, sg_w_s, sg_b_s, sg_w_out, final_norm_g, loss_target, m_ab_norm_g, m_ab_w_mod, m_ab_b_mod, m_ab_w_in, m_ab_conv_w, m_ab_w_out, m_sg_norm_g, m_sg_w_mod, m_sg_b_mod, m_sg_w_in, m_sg_ln_g, m_sg_ln_b, m_sg_w_s, m_sg_b_s, m_sg_w_out, m_final_norm_g, v_ab_norm_g, v_ab_w_mod, v_ab_b_mod, v_ab_w_in, v_ab_conv_w, v_ab_w_out, v_sg_norm_g, v_sg_w_mod, v_sg_b_mod, v_sg_w_in, v_sg_ln_g, v_sg_ln_b, v_sg_w_s, v_sg_b_s, v_sg_w_out, v_final_norm_g):
    _, S, D = x.shape
    L = ab_norm_g.shape[0]
    W = ab_conv_w.shape[2] * NDEV
    n_ab, n_sg = ab_w_in.shape[2], sg_w_in.shape[2]
    n_mod = ab_w_mod.shape[2]
    kb = ab_w_out.shape[1]
    xi, yi, ci = _position()
    dev = 4 * xi + 2 * yi + ci
    x2, tgt = x.reshape(S, D), loss_target.reshape(S, D)

    small = [c, ab_conv_w, sg_norm_g, sg_ln_g, sg_ln_b]
    (g1,) = _comm_only(_Gather([_pack(small)]), "ag_small")
    g1 = g1.reshape(NDEV, -1)
    c_all = g1[:, :D]
    off = D
    conv_full, off = _unshard(g1, off, ab_conv_w.shape)
    sg_norm_full, off = _unshard(g1, off, sg_norm_g.shape)
    ln_g_full, off = _unshard(g1, off, sg_ln_g.shape)
    ln_b_full, off = _unshard(g1, off, sg_ln_b.shape)

    ab_b_cols = lax.dynamic_slice_in_dim(ab_b_mod, dev * n_mod, n_mod, axis=1)
    m_ab = _mod_fwd(c_all, ab_w_mod, ab_b_cols.reshape(L, 1, n_mod), "mod_fwd_ab")
    m_sg = _mod_fwd(c_all, sg_w_mod, sg_b_mod.reshape(L, 1, n_mod), "mod_fwd_sg")
    m_part = jnp.stack([m_ab, m_sg]).transpose(2, 0, 1, 3).reshape(NDEV, 2 * L * n_mod)
    (g2,) = _comm_only(_Gather([m_part]), "ag_mod")
    mine = lax.dynamic_index_in_dim(g2, dev, axis=1, keepdims=False)
    mods = mine.reshape(NDEV, 2, L, n_mod).transpose(1, 2, 0, 3).reshape(2, L, 3 * D)

    def mod_of(kind, i):
        m = mods[kind, i]
        return m[:D].reshape(1, D), m[D:2 * D].reshape(1, D), m[2 * D:].reshape(1, D)

    big_w = [[(ab_w_in, m_ab_w_in, v_ab_w_in), (ab_w_out, m_ab_w_out, v_ab_w_out)],
             [(sg_w_in, m_sg_w_in, v_sg_w_in), (sg_w_out, m_sg_w_out, v_sg_w_out)]]
    big_names = [["ab_w_in", "ab_w_out"], ["sg_w_in", "sg_w_out"]]
    n_layers = 2 * L
    shards = [[big_w[layer % 2][k][0][layer // 2].astype(BF16) for k in range(2)] for layer in range(n_layers)]
    gathered = {}

    def gather_of(keys):
        keys = [key for key in keys if key[0] < n_layers]
        return keys, (_Gather([shards[layer][k] for layer, k in keys]) if keys else None)

    def keep_gathered(keys, res):
        for (layer, k), g in zip(keys, res):
            gathered[(layer, k)] = g.reshape((NDEV, 1, D, g.shape[-1]) if k == 0 else (NDEV, 1, kb, D))

    keys, comm = gather_of([(0, 0)])
    keep_gathered(keys, _comm_only(comm, "ag_w_in_layer0"))

    cosf, sinf = _rope_tables(S)
    T = min(S, ATTN_TILE)
    bias = _attn_bias(T)
    norm_g = [ab_norm_g, sg_norm_full]
    w_s_t = jnp.swapaxes(sg_w_s, -1, -2)
    b_s_t = jnp.swapaxes(sg_b_s, -1, -2)

    saved = []
    x_cur, res, gate_prev = x2, None, None
    for layer in range(2 * L):
        kind, i = layer % 2, layer // 2
        tag = f"{'ab' if kind == 0 else 'sg'}{i}"
        shift, scale, gate = mod_of(kind, i)
        g = norm_g[kind][i].reshape(1, D)
        xl, h = _pre(x_cur, res, gate_prev, g, scale, shift, f"pre_{tag}")
        keys, comm = gather_of([(layer + 1, 0)])
        if comm is None:
            proj = _mm_nn_in(h, gathered[(layer, 0)], 0, f"proj_{tag}")
        else:
            proj, got = _mm_nn_in(h, gathered[(layer, 0)], 0, f"proj_{tag}", comm)
            keep_gathered(keys, got)
        rec = dict(xl=xl, h=h, proj=proj, g=g, scale=scale, gate=gate)
        keys, comm = gather_of(([(0, 1)] if layer == 0 else []) + [(layer + 1, 1)])
        if kind == 0:
            qr, kr, vb = _rope_qkv(proj, cosf, sinf, W, f"rope_{tag}")
            (attn, lse), got = _attn_fwd(qr, kr, vb, bias, f"attn_{tag}", comm)
            y = _ab_mix(attn, proj, conv_full[i], W, f"mix_{tag}")
            rec.update(qr=qr, kr=kr, vb=vb, attn=attn, lse=lse)
        else:
            y, got = _sgu_fwd(proj, ln_g_full[i].reshape(1, D), ln_b_full[i].reshape(1, D), sg_w_s[i], b_s_t[i],
                              f"sgu_{tag}", comm)
        keep_gathered(keys, got)
        out = _mm_nn_out(y, gathered[(layer, 1)], 0, f"out_{tag}")
        rec.update(y=y, out=out)
        saved.append(rec)
        x_cur, res, gate_prev = xl, out, gate

    dx, loss_part, d_final_g = _loss_head(x_cur, res, gate_prev, final_norm_g.reshape(1, D), tgt, "loss_head")
    loss = lax.psum(loss_part[0, 0], ("x", "y", "c"))

    c_idx = ci.reshape(1).astype(jnp.int32)
    big_res = {}
    pending = None

    def finish_layer(done, from_chips):
        for k in range(2):
            nm = big_names[done % 2][k]
            w, m, v = big_w[done % 2][k]
            flat = lambda a: a.reshape(L * a.shape[1], a.shape[2])
            big_res[nm] = _sum_adam(from_chips[k], flat(w), flat(m), flat(v), done // 2, big_res.get(nm),
                                    f"adam_{nm}{done // 2}")

    dm = [[None] * L, [None] * L]
    d_norm = [[None] * L, [None] * L]
    d_conv, d_lng, d_lnb, d_ws, d_bs = [None] * L, [None] * L, [None] * L, [None] * L, [None] * L
    for layer in reversed(range(2 * L)):
        kind, i = layer % 2, layer // 2
        tag = f"{'ab' if kind == 0 else 'sg'}{i}"
        rec = saved[layer]
        w_in_l, w_out_l = gathered[(layer, 0)], gathered[(layer, 1)]
        dout, dgate = _post_bwd(dx, rec["out"], rec["gate"], f"post_bwd_{tag}")
        dy = _mm_nt_out(dout, w_out_l, 0, f"dy_{tag}")
        dwo = _mm_tn_out(rec["y"], dout, 0, 1, None, f"dwout_{tag}")
        if kind == 0:
            do, delta = _dattn_prep(dy, rec["proj"], rec["attn"], W, f"dattn_{tag}")
            comm = _ToChips(pending[1]) if pending else None
            (dqr, dkr, dvv), got = _attn_bwd(rec["qr"], rec["kr"], rec["vb"], do, rec["lse"], delta, bias,
                                             f"attn_bwd_{tag}", comm)
            if pending:
                finish_layer(pending[0], got)
            dproj, dcw = _ab_bwd(dy, rec["attn"], rec["proj"], dqr, dkr, dvv, cosf, sinf, conv_full[i], W,
                                 f"mix_bwd_{tag}")
            d_conv[i] = dcw[:3]
            earlier = None
        else:
            dproj, d_ws[i], dbs, d_lng[i], d_lnb[i] = _sgu_bwd(
                rec["proj"], dy, ln_g_full[i].reshape(1, D), ln_b_full[i].reshape(1, D),
                sg_w_s[i], w_s_t[i], b_s_t[i], f"sgu_bwd_{tag}")
            d_bs[i] = dbs.reshape(C_GROUPS, C_CHUNK)
            earlier = pending
        if earlier:
            dwi, got_in = _mm_tn_in(rec["h"], dproj, 0, 1, None, f"dwin_{tag}", _ToChips(earlier[1][:1]))
        else:
            dwi = _mm_tn_in(rec["h"], dproj, 0, 1, None, f"dwin_{tag}")
        grads = [dwi.reshape(NDEV, D, -1), dwo.reshape(NDEV, kb, D)]
        if earlier:
            both = _Both(_ToSibling(grads), _ToChips(earlier[1][1:]))
            dh, got = _mm_nt_in(dproj, w_in_l, 0, f"dh_{tag}", both)
            from_sibling, got_out = both.split_results(got)
            finish_layer(earlier[0], [got_in[0], got_out[0]])
        else:
            dh, from_sibling = _mm_nt_in(dproj, w_in_l, 0, f"dh_{tag}", _ToSibling(grads))
        pending = (layer, [_add_sibling(g, r, c_idx, f"rs_add_{big_names[kind][k]}{i}")
                           for k, (g, r) in enumerate(zip(grads, from_sibling))])
        last = _ToChips(pending[1]) if layer == 0 else None
        (dx, dshift, dscale, d_norm[kind][i]), got = _pre_bwd(
            rec["xl"], dh, dx, rec["g"], rec["scale"], f"pre_bwd_{tag}", last)
        if layer == 0:
            finish_layer(0, got)
        dm[kind][i] = jnp.concatenate([dshift, dscale, dgate], axis=1).reshape(3 * D)
    grad_x = dx.reshape(1, S, D)
    for kind in range(2):
        for k in range(2):
            nm = big_names[kind][k]
            big_res[nm] = [o.reshape(big_w[kind][k][0].shape) for o in big_res[nm]]

    stack = lambda xs: jnp.stack(xs)
    pack_items = [stack(dm[0]), stack(dm[1]), stack(d_norm[0]).reshape(L, D), stack(d_conv),
                  stack(d_norm[1]).reshape(L, D), stack(d_lng).reshape(L, D), stack(d_lnb).reshape(L, D),
                  stack(d_ws), stack(d_bs), d_final_g]
    (g3,) = _comm_only(_Gather([_pack(pack_items)]), "ag_grads")
    P = g3.shape[1] * g3.shape[2]
    tot = _sum_rows(g3, "sum_small").reshape(P)
    g3 = g3.reshape(NDEV, P)
    sizes = [int(np.prod(p.shape)) for p in pack_items]
    offs = np.concatenate([[0], np.cumsum(sizes)]).tolist()
    seg = lambda k, shape: tot[offs[k]:offs[k + 1]].reshape(shape)

    def shard(full, n):
        return lax.dynamic_slice_in_dim(full, dev * n, n, axis=full.ndim - 1)

    g_ab_b_mod = seg(0, (L, 3 * D))
    g_sg_b_mod = shard(seg(1, (L, 3 * D)), n_mod)
    g_ab_norm = seg(2, (L, D))
    g_conv = shard(seg(3, (L, 3, W)), W // NDEV)
    g_sg_norm = shard(seg(4, (L, D)), kb)
    g_ln_g = shard(seg(5, (L, D)), kb)
    g_ln_b = shard(seg(6, (L, D)), kb)
    g_w_s = seg(7, sg_w_s.shape)
    g_b_s = seg(8, sg_b_s.shape)
    g_final = seg(9, (D,))

    small_w = [("ab_norm_g", g_ab_norm, ab_norm_g, m_ab_norm_g, v_ab_norm_g),
               ("ab_b_mod", g_ab_b_mod, ab_b_mod, m_ab_b_mod, v_ab_b_mod),
               ("ab_conv_w", g_conv, ab_conv_w, m_ab_conv_w, v_ab_conv_w),
               ("sg_norm_g", g_sg_norm, sg_norm_g, m_sg_norm_g, v_sg_norm_g),
               ("sg_b_mod", g_sg_b_mod, sg_b_mod, m_sg_b_mod, v_sg_b_mod),
               ("sg_ln_g", g_ln_g, sg_ln_g, m_sg_ln_g, v_sg_ln_g),
               ("sg_ln_b", g_ln_b, sg_ln_b, m_sg_ln_b, v_sg_ln_b),
               ("sg_w_s", g_w_s, sg_w_s, m_sg_w_s, v_sg_w_s),
               ("sg_b_s", g_b_s, sg_b_s, m_sg_b_s, v_sg_b_s),
               ("final_norm_g", g_final, final_norm_g, m_final_norm_g, v_final_norm_g)]
    packed = [_pack([t[k] for t in small_w]) for k in (1, 2, 3, 4)]
    upd = _adam_only(*packed, "adam_small")
    small_res = {}
    o = 0
    for nm, g, w, _, _ in small_w:
        size = int(np.prod(w.shape))
        small_res[nm] = [g] + [u.reshape(-1)[o:o + size].reshape(w.shape) for u in upd]
        o += size

    KP = 128
    sc_t = jnp.pad((c_all * jax.nn.sigmoid(c_all)).T, ((0, 0), (0, KP - NDEV)))
    mod_res = {}
    for kind, nm, (w, m, v) in ((0, "ab_w_mod", (ab_w_mod, m_ab_w_mod, v_ab_w_mod)),
                                (1, "sg_w_mod", (sg_w_mod, m_sg_w_mod, v_sg_w_mod))):
        dm_all = g3[:, offs[kind]:offs[kind + 1]].reshape(NDEV, L, 3 * D)
        cols = jnp.pad(shard(dm_all, n_mod).transpose(1, 0, 2), ((0, 0), (0, KP - NDEV), (0, 0)))
        mod_res[nm] = _wmod_grad_adam(sc_t, cols, w, m, v, f"adam_{nm}")

    order = ["ab_norm_g", "ab_w_mod", "ab_b_mod", "ab_w_in", "ab_conv_w", "ab_w_out", "sg_norm_g", "sg_w_mod",
             "sg_b_mod", "sg_w_in", "sg_ln_g", "sg_ln_b", "sg_w_s", "sg_b_s", "sg_w_out", "final_norm_g"]
    res = {**big_res, **small_res, **mod_res}
    outs = [loss, grad_x]
    for k in range(4):
        outs += [res[nm][k] for nm in order]
    return tuple(outs)
```

```python
import functools
import math

import numpy as np
import jax
import jax.numpy as jnp
from jax import lax
from jax.experimental import pallas as pl
from jax.experimental.pallas import tpu as pltpu

F32 = jnp.float32
BF16 = jnp.bfloat16

NDEV = 8
NCHIP = 4
EPS = 1e-6
HEAD_DIM = 128
ROPE_THETA = 10000.0
DILATED_PATTERNS = ((128, 1), (512, 4), (2048, 16))
NEG_INF = -1e30
C_CHUNK = 128
C_GROUPS = 8
ADAM_LR = 0.001
ADAM_B1 = 0.9
ADAM_B2 = 0.999
ADAM_EPS = 1e-08
ADAM_WD = 0.01
ADAM_STEP = 10
GELU_K = math.sqrt(2.0 / math.pi)
GELU_C = 0.044715

VMEM_LIMIT_BYTES = 56 * 1024 * 1024
ATTN_TILE = 512
HEADS_PER_STEP = 4
ATTN_ROW_CHUNK = 256
LANES = 128
ROW_TILE = 256
MESH = pl.DeviceIdType.MESH
ANY = pl.BlockSpec(memory_space=pl.ANY)


def _cp(*sem):
    return pltpu.CompilerParams(dimension_semantics=sem, vmem_limit_bytes=VMEM_LIMIT_BYTES)


def _sigmoid(z):
    return 0.5 * (jnp.tanh(0.5 * z) + 1.0)


def _silu_and_grad(z):
    s = _sigmoid(z)
    return z * s, s * (1.0 + z * (1.0 - s))


def _gelu_and_grad(x):
    x2 = x * x
    t = jnp.tanh(GELU_K * (x + GELU_C * x2 * x))
    g = 0.5 * x * (1.0 + t)
    dg = 0.5 * (1.0 + t) + 0.5 * x * (1.0 - t * t) * (GELU_K * (1.0 + 3.0 * GELU_C * x2))
    return g, dg


def _position():
    return lax.axis_index("x"), lax.axis_index("y"), lax.axis_index("c")


def _chips(x, y):
    return [(1 - x, y), (x, 1 - y), (1 - x, 1 - y)]


class _Gather:
    def __init__(self, arrs):
        n = len(arrs)
        self.arrs = list(arrs)
        self.out_shape = [jax.ShapeDtypeStruct((NDEV,) + a.shape, a.dtype) for a in arrs]
        self.scratch = [pltpu.SemaphoreType.DMA((n, 7)), pltpu.SemaphoreType.DMA((n, 7)),
                        pltpu.SemaphoreType.DMA((n,))]

    def _copies(self, ins, outs, sems):
        send_sems, recv_sems, local_sems = sems
        x, y, c = _position()

        def copy(a, k, block, to, src=None):
            dst = outs[a].at[4 * block[0] + 2 * block[1] + block[2]]
            return pltpu.make_async_remote_copy(
                src_ref=dst if src is None else src, dst_ref=dst,
                send_sem=send_sems.at[a, k], recv_sem=recv_sems.at[a, k],
                device_id=to, device_id_type=MESH)

        n = len(ins)
        me, sibling = (x, y, c), (x, y, 1 - c)
        mine = [pltpu.make_async_copy(ins[a], outs[a].at[4 * x + 2 * y + c], local_sems.at[a]) for a in range(n)]
        first = []
        for a in range(n):
            first.append(copy(a, 0, me, sibling, src=ins[a]))
            first += [copy(a, 1 + j, me, (*chip, c), src=ins[a]) for j, chip in enumerate(_chips(x, y))]
        return copy, mine, first

    def start(self, ins, outs, sems):
        _, mine, first = self._copies(ins, outs, sems)
        for cp in mine + first:
            cp.start()

    def finish(self, ins, outs, sems):
        copy, mine, first = self._copies(ins, outs, sems)
        x, y, c = _position()
        me, sibling = (x, y, c), (x, y, 1 - c)
        passed = []
        for j, chip in enumerate(_chips(x, y)):
            for a in range(len(ins)):
                copy(a, 1 + j, (*chip, c), me).wait_recv()
                fwd = copy(a, 4 + j, (*chip, c), sibling)
                fwd.start()
                passed.append(fwd)
        for a in range(len(ins)):
            copy(a, 0, sibling, me).wait_recv()
            for j, chip in enumerate(_chips(x, y)):
                copy(a, 4 + j, (*chip, 1 - c), me).wait_recv()
        for cp in first + passed:
            cp.wait_send()
        for cp in mine:
            cp.wait()


class _ToSibling:
    def __init__(self, gs):
        n = len(gs)
        self.arrs = list(gs)
        self.out_shape = [jax.ShapeDtypeStruct((NCHIP,) + g.shape[1:], g.dtype) for g in gs]
        self.scratch = [pltpu.SemaphoreType.DMA((n, NCHIP)), pltpu.SemaphoreType.DMA((n, NCHIP))]

    def _copies(self, ins, outs, sems):
        send_sems, recv_sems = sems
        x, y, c = _position()
        return [pltpu.make_async_remote_copy(
            src_ref=ins[a].at[2 * k + (1 - c)], dst_ref=outs[a].at[k],
            send_sem=send_sems.at[a, k], recv_sem=recv_sems.at[a, k],
            device_id=(x, y, 1 - c), device_id_type=MESH) for a in range(len(ins)) for k in range(NCHIP)]

    def start(self, ins, outs, sems):
        for cp in self._copies(ins, outs, sems):
            cp.start()

    def finish(self, ins, outs, sems):
        copies = self._copies(ins, outs, sems)
        for cp in copies:
            cp.wait_recv()
        for cp in copies:
            cp.wait_send()


class _ToChips:
    def __init__(self, ps):
        n = len(ps)
        self.arrs = list(ps)
        self.out_shape = [jax.ShapeDtypeStruct(p.shape, p.dtype) for p in ps]
        self.scratch = [pltpu.SemaphoreType.DMA((n, 3)), pltpu.SemaphoreType.DMA((n, 3)),
                        pltpu.SemaphoreType.DMA((n,))]

    def _copies(self, ins, outs, sems, arrivals):
        send_sems, recv_sems, local_sems = sems
        x, y, c = _position()
        mychip = 2 * x + y
        n = len(ins)
        mine = [pltpu.make_async_copy(ins[a].at[mychip], outs[a].at[mychip], local_sems.at[a]) for a in range(n)]
        sends, recvs = [], []
        for a in range(n):
            for j, chip in enumerate(_chips(x, y)):
                sends.append(pltpu.make_async_remote_copy(
                    src_ref=ins[a].at[2 * chip[0] + chip[1]], dst_ref=outs[a].at[mychip],
                    send_sem=send_sems.at[a, j], recv_sem=recv_sems.at[a, j],
                    device_id=(*chip, c), device_id_type=MESH))
                if arrivals:
                    slot = outs[a].at[2 * chip[0] + chip[1]]
                    recvs.append(pltpu.make_async_remote_copy(
                        src_ref=slot, dst_ref=slot, send_sem=send_sems.at[a, j], recv_sem=recv_sems.at[a, j],
                        device_id=(*chip, c), device_id_type=MESH))
        return mine, sends, recvs

    def start(self, ins, outs, sems):
        mine, sends, _ = self._copies(ins, outs, sems, False)
        for cp in mine + sends:
            cp.start()

    def finish(self, ins, outs, sems):
        mine, sends, recvs = self._copies(ins, outs, sems, True)
        for cp in recvs:
            cp.wait_recv()
        for cp in sends:
            cp.wait_send()
        for cp in mine:
            cp.wait()


class _Both:
    def __init__(self, first, second):
        self.parts = (first, second)
        self.arrs = first.arrs + second.arrs
        self.out_shape = first.out_shape + second.out_shape
        self.scratch = first.scratch + second.scratch

    def _split(self, ins, outs, sems):
        a, _ = self.parts
        ni, no, ns = len(a.arrs), len(a.out_shape), len(a.scratch)
        return (ins[:ni], outs[:no], sems[:ns]), (ins[ni:], outs[no:], sems[ns:])

    def start(self, ins, outs, sems):
        for part, refs in zip(self.parts, self._split(ins, outs, sems)):
            part.start(*refs)

    def finish(self, ins, outs, sems):
        for part, refs in zip(self.parts, self._split(ins, outs, sems)):
            part.finish(*refs)

    def split_results(self, res):
        no = len(self.parts[0].out_shape)
        return res[:no], res[no:]


def _comm_only(comm, name):
    n_in, n_out = len(comm.arrs), len(comm.out_shape)

    def body(*refs):
        ins, outs, sems = refs[:n_in], refs[n_in:n_in + n_out], refs[n_in + n_out:]
        comm.start(ins, outs, sems)
        comm.finish(ins, outs, sems)

    return pl.pallas_call(
        body, name=name, out_shape=comm.out_shape, in_specs=[ANY] * n_in, out_specs=[ANY] * n_out,
        scratch_shapes=comm.scratch,
    )(*comm.arrs)


def _hosted_call(body, operands, *, name, grid, in_specs, out_specs, out_shape, scratch_shapes=(), sem=(),
                 aliases=None, comm=None):
    single = not isinstance(out_shape, (list, tuple))
    o_specs = [out_specs] if single else list(out_specs)
    o_shape = [out_shape] if single else list(out_shape)
    n_in, n_out, n_scr = len(in_specs), len(o_shape), len(scratch_shapes)
    if comm is None:
        res = pl.pallas_call(body, name=name, grid=grid, in_specs=list(in_specs), out_specs=o_specs,
                             out_shape=o_shape, scratch_shapes=list(scratch_shapes),
                             input_output_aliases=aliases or {}, compiler_params=_cp(*sem))(*operands)
        return (res[0] if single else res), []
    c_in, c_out = len(comm.arrs), len(comm.out_shape)

    def wrapped(*refs):
        ins, cins = refs[:n_in], refs[n_in:n_in + c_in]
        o0 = n_in + c_in
        outs, couts = refs[o0:o0 + n_out], refs[o0 + n_out:o0 + n_out + c_out]
        s0 = o0 + n_out + c_out
        scr, csems = refs[s0:s0 + n_scr], refs[s0 + n_scr:]
        pids = [pl.program_id(a) for a in range(len(grid))]
        first = functools.reduce(jnp.logical_and, [p == 0 for p in pids])
        last = functools.reduce(jnp.logical_and, [p == g - 1 for p, g in zip(pids, grid)])

        @pl.when(first)
        def _():
            comm.start(cins, couts, csems)

        body(*ins, *outs, *scr)

        @pl.when(last)
        def _():
            comm.finish(cins, couts, csems)

    res = pl.pallas_call(
        wrapped, name=name, grid=grid, in_specs=list(in_specs) + [ANY] * c_in, out_specs=o_specs + [ANY] * c_out,
        out_shape=o_shape + comm.out_shape, scratch_shapes=list(scratch_shapes) + comm.scratch,
        input_output_aliases=aliases or {}, compiler_params=_cp(*(["arbitrary"] * len(grid))),
    )(*operands, *comm.arrs)
    return (res[0] if single else res[:n_out]), res[n_out:]


def _adamw(w, g, m, v):
    m2 = ADAM_B1 * m + (1.0 - ADAM_B1) * g
    v2 = ADAM_B2 * v + (1.0 - ADAM_B2) * (g * g)
    m_hat = m2 / (1.0 - ADAM_B1 ** ADAM_STEP)
    v_hat = v2 / (1.0 - ADAM_B2 ** ADAM_STEP)
    delta = -ADAM_LR * (m_hat / (jnp.sqrt(v_hat) + ADAM_EPS) + ADAM_WD * w)
    return delta, m2, v2


def _add_sibling(g, recv, c_idx, name):
    _, R, C = g.shape
    tr = min(R, 512)

    def body(c_ref, g_ref, r_ref, o_ref):
        o_ref[...] = (g_ref[...] + r_ref[...]).astype(BF16)

    return pl.pallas_call(
        body, name=name,
        grid_spec=pltpu.PrefetchScalarGridSpec(
            num_scalar_prefetch=1, grid=(NCHIP, R // tr),
            in_specs=[pl.BlockSpec((1, tr, C), lambda k, i, c_ref: (2 * k + c_ref[0], i, 0)),
                      pl.BlockSpec((1, tr, C), lambda k, i, c_ref: (k, i, 0))],
            out_specs=pl.BlockSpec((1, tr, C), lambda k, i, c_ref: (k, i, 0))),
        out_shape=jax.ShapeDtypeStruct((NCHIP, R, C), BF16),
        compiler_params=_cp("parallel", "parallel"),
    )(c_idx, g, recv)


def _sum_adam(parts, w, m, v, row0, prev, name):
    K, R, C = parts.shape
    LR = w.shape[0]
    tr = min(R, 256)
    nb = R // tr
    first = row0 // tr

    def body(p_ref, w_ref, m_ref, v_ref, *rest):
        g_ref, d_ref, m2_ref, v2_ref = rest[-4:]
        g = p_ref[0].astype(F32)
        for k in range(1, K):
            g = g + p_ref[k].astype(F32)
        delta, m2, v2 = _adamw(w_ref[...], g, m_ref[...], v_ref[...])
        g_ref[...] = g
        d_ref[...] = delta
        m2_ref[...] = m2
        v2_ref[...] = v2

    blk = pl.BlockSpec((tr, C), lambda i: (first + i, 0))
    shp = jax.ShapeDtypeStruct((LR, C), F32)
    operands = [parts, w, m, v] + (list(prev) if prev is not None else [])
    return pl.pallas_call(
        body, name=name, grid=(nb,),
        in_specs=[pl.BlockSpec((K, tr, C), lambda i: (0, i, 0)), blk, blk, blk] + [ANY] * (len(operands) - 4),
        out_specs=[blk] * 4, out_shape=[shp] * 4,
        input_output_aliases={4 + k: k for k in range(len(operands) - 4)},
        compiler_params=_cp("parallel"),
    )(*operands)


def _sum_rows(parts, name):
    K, R, C = parts.shape
    tr = min(R, 256)
    while R % tr:
        tr //= 2

    def body(p_ref, o_ref):
        g = p_ref[0]
        for k in range(1, K):
            g = g + p_ref[k]
        o_ref[...] = g

    return pl.pallas_call(
        body, name=name, grid=(R // tr,),
        in_specs=[pl.BlockSpec((K, tr, C), lambda i: (0, i, 0))],
        out_specs=pl.BlockSpec((tr, C), lambda i: (i, 0)),
        out_shape=jax.ShapeDtypeStruct((R, C), F32),
        compiler_params=_cp("parallel"),
    )(parts)


def _adam_only(g, w, m, v, name):
    R, C = g.shape
    tr = min(R, 256)
    while R % tr:
        tr //= 2

    def body(g_ref, w_ref, m_ref, v_ref, d_ref, m2_ref, v2_ref):
        delta, m2, v2 = _adamw(w_ref[...], g_ref[...], m_ref[...], v_ref[...])
        d_ref[...] = delta
        m2_ref[...] = m2
        v2_ref[...] = v2

    blk = pl.BlockSpec((tr, C), lambda i: (i, 0))
    shp = jax.ShapeDtypeStruct((R, C), F32)
    return pl.pallas_call(
        body, name=name, grid=(R // tr,), in_specs=[blk] * 4, out_specs=[blk] * 3,
        out_shape=[shp] * 3, compiler_params=_cp("parallel"),
    )(g, w, m, v)


def _mod_fwd(c_all, w_mod, b_cols, name):
    L, D, n = w_mod.shape
    B = c_all.shape[0]

    def body(c_ref, w_ref, b_ref, o_ref):
        cv = c_ref[...]
        sc = (cv * _sigmoid(cv)).astype(BF16)
        o_ref[0] = jnp.dot(sc, w_ref[0].astype(BF16), preferred_element_type=F32) + b_ref[0]

    return pl.pallas_call(
        body, name=name, grid=(L,),
        in_specs=[pl.BlockSpec((B, D), lambda l: (0, 0)),
                  pl.BlockSpec((1, D, n), lambda l: (l, 0, 0)),
                  pl.BlockSpec((1, 1, n), lambda l: (l, 0, 0))],
        out_specs=pl.BlockSpec((1, B, n), lambda l: (l, 0, 0)),
        out_shape=jax.ShapeDtypeStruct((L, B, n), F32),
        compiler_params=_cp("parallel"),
    )(c_all, w_mod, b_cols)


def _wmod_grad_adam(sc_t, dm, w, m, v, name):
    L, D, n = w.shape
    KP = sc_t.shape[1]
    tr = min(D, 512)

    def body(s_ref, dm_ref, w_ref, m_ref, v_ref, g_ref, d_ref, m2_ref, v2_ref):
        g = jnp.dot(s_ref[...], dm_ref[0], preferred_element_type=F32,
                    precision=lax.Precision.HIGHEST)
        delta, m2, v2 = _adamw(w_ref[0], g, m_ref[0], v_ref[0])
        g_ref[0] = g
        d_ref[0] = delta
        m2_ref[0] = m2
        v2_ref[0] = v2

    blk = pl.BlockSpec((1, tr, n), lambda l, i: (l, i, 0))
    shp = jax.ShapeDtypeStruct((L, D, n), F32)
    return pl.pallas_call(
        body, name=name, grid=(L, D // tr),
        in_specs=[pl.BlockSpec((tr, KP), lambda l, i: (i, 0)),
                  pl.BlockSpec((1, KP, n), lambda l, i: (l, 0, 0)), blk, blk, blk],
        out_specs=[blk] * 4, out_shape=[shp] * 4,
        compiler_params=_cp("parallel", "parallel"),
    )(sc_t, dm, w, m, v)


def _vec_spec(D):
    return pl.BlockSpec((1, D), lambda i: (0, 0))


def _pre(x, res, gate, g, scale, shift, name):
    S, D = x.shape
    tr = min(S, ROW_TILE)
    has_res = res is not None
    row = pl.BlockSpec((tr, D), lambda i: (i, 0))

    def body(*refs):
        if has_res:
            x_ref, r_ref, gate_ref, g_ref, sc_ref, sh_ref, xl_ref, h_ref = refs
            xv = x_ref[...] + gate_ref[...] * r_ref[...]
            xl_ref[...] = xv
        else:
            x_ref, g_ref, sc_ref, sh_ref, h_ref = refs
            xv = x_ref[...]
        r = lax.rsqrt(jnp.mean(xv * xv, axis=-1, keepdims=True) + EPS)
        y = (xv * r) * g_ref[...]
        h_ref[...] = (y * (1.0 + sc_ref[...]) + sh_ref[...]).astype(BF16)

    vec = _vec_spec(D)
    if has_res:
        xl, h = pl.pallas_call(
            body, name=name, grid=(S // tr,),
            in_specs=[row, row, vec, vec, vec, vec], out_specs=[row, row],
            out_shape=[jax.ShapeDtypeStruct((S, D), F32), jax.ShapeDtypeStruct((S, D), BF16)],
            compiler_params=_cp("parallel"),
        )(x, res, gate, g, scale, shift)
        return xl, h
    h = pl.pallas_call(
        body, name=name, grid=(S // tr,),
        in_specs=[row, vec, vec, vec], out_specs=row,
        out_shape=jax.ShapeDtypeStruct((S, D), BF16),
        compiler_params=_cp("parallel"),
    )(x, g, scale, shift)
    return x, h


def _pre_bwd(xl, dh, dx_in, g, scale, name, comm=None):
    S, D = xl.shape
    tr = min(S, ROW_TILE)
    nsteps = S // tr
    row = pl.BlockSpec((tr, D), lambda i: (i, 0))
    vec = _vec_spec(D)

    def body(x_ref, dh_ref, dxin_ref, g_ref, sc_ref, dx_ref, dsh_ref, dsc_ref, dg_ref, acc_sh, acc_t):
        i = pl.program_id(0)
        xv = x_ref[...]
        dh = dh_ref[...]
        r = lax.rsqrt(jnp.mean(xv * xv, axis=-1, keepdims=True) + EPS)
        xn = xv * r
        part_sh = jnp.sum(dh.reshape(tr // 8, 8, D), axis=0)
        part_t = jnp.sum((dh * xn).reshape(tr // 8, 8, D), axis=0)

        @pl.when(i == 0)
        def _():
            acc_sh[...] = part_sh
            acc_t[...] = part_t

        @pl.when(i > 0)
        def _():
            acc_sh[...] += part_sh
            acc_t[...] += part_t

        dxn = dh * (g_ref[...] * (1.0 + sc_ref[...]))
        dx_ref[...] = dxin_ref[...] + r * (dxn - xn * jnp.mean(dxn * xn, axis=-1, keepdims=True))

        @pl.when(i == nsteps - 1)
        def _():
            t = jnp.sum(acc_t[...], axis=0, keepdims=True)
            dsh_ref[...] = jnp.sum(acc_sh[...], axis=0, keepdims=True)
            dsc_ref[...] = t * g_ref[...]
            dg_ref[...] = t * (1.0 + sc_ref[...])

    v = jax.ShapeDtypeStruct((1, D), F32)
    return _hosted_call(
        body, [xl, dh, dx_in, g, scale], name=name, grid=(nsteps,),
        in_specs=[row, row, row, vec, vec], out_specs=[row, vec, vec, vec],
        out_shape=[jax.ShapeDtypeStruct((S, D), F32), v, v, v],
        scratch_shapes=[pltpu.VMEM((8, D), F32), pltpu.VMEM((8, D), F32)],
        sem=("arbitrary",), comm=comm)


def _post_bwd(dx, out, gate, name):
    S, D = dx.shape
    tr = min(S, ROW_TILE)
    nsteps = S // tr
    row = pl.BlockSpec((tr, D), lambda i: (i, 0))
    vec = _vec_spec(D)

    def body(dx_ref, o_ref, gate_ref, do_ref, dg_ref, acc):
        i = pl.program_id(0)
        dxv = dx_ref[...]
        do_ref[...] = (dxv * gate_ref[...]).astype(BF16)
        part = jnp.sum((dxv * o_ref[...]).reshape(tr // 8, 8, D), axis=0)

        @pl.when(i == 0)
        def _():
            acc[...] = part

        @pl.when(i > 0)
        def _():
            acc[...] += part

        @pl.when(i == nsteps - 1)
        def _():
            dg_ref[...] = jnp.sum(acc[...], axis=0, keepdims=True)

    return pl.pallas_call(
        body, name=name, grid=(nsteps,),
        in_specs=[row, row, vec], out_specs=[row, vec],
        out_shape=[jax.ShapeDtypeStruct((S, D), BF16), jax.ShapeDtypeStruct((1, D), F32)],
        scratch_shapes=[pltpu.VMEM((8, D), F32)],
        compiler_params=_cp("arbitrary"),
    )(dx, out, gate)


def _loss_head(x, res, gate, gf, tgt, name):
    S, D = x.shape
    tr = min(S, ROW_TILE)
    nsteps = S // tr
    row = pl.BlockSpec((tr, D), lambda i: (i, 0))
    vec = _vec_spec(D)

    def body(x_ref, r_ref, gate_ref, gf_ref, t_ref, dx_ref, loss_ref, dgf_ref, acc, lacc):
        i = pl.program_id(0)
        xv = x_ref[...] + gate_ref[...] * r_ref[...]
        r = lax.rsqrt(jnp.mean(xv * xv, axis=-1, keepdims=True) + EPS)
        xn = xv * r
        err = xn * gf_ref[...] - t_ref[...]
        row_loss = jnp.mean(err * err, axis=-1, keepdims=True)
        lpart = 0.5 * jnp.sum(row_loss, axis=0, keepdims=True)
        dy = err * (1.0 / D)
        part = jnp.sum((dy * xn).reshape(tr // 8, 8, D), axis=0)

        @pl.when(i == 0)
        def _():
            acc[...] = part
            lacc[...] = lpart

        @pl.when(i > 0)
        def _():
            acc[...] += part
            lacc[...] += lpart

        dxn = dy * gf_ref[...]
        dx_ref[...] = r * (dxn - xn * jnp.mean(dxn * xn, axis=-1, keepdims=True))

        @pl.when(i == nsteps - 1)
        def _():
            dgf_ref[...] = jnp.sum(acc[...], axis=0, keepdims=True)
            loss_ref[...] = lacc[...]

    return pl.pallas_call(
        body, name=name, grid=(nsteps,),
        in_specs=[row, row, vec, vec, row],
        out_specs=[row, pl.BlockSpec((1, 1), lambda i: (0, 0)), vec],
        out_shape=[jax.ShapeDtypeStruct((S, D), F32), jax.ShapeDtypeStruct((1, 1), F32),
                   jax.ShapeDtypeStruct((1, D), F32)],
        scratch_shapes=[pltpu.VMEM((8, D), F32), pltpu.VMEM((1, 1), F32)],
        compiler_params=_cp("arbitrary"),
    )(x, res, gate, gf, tgt)


NN = (((1,), (0,)), ((), ()))
NT = (((1,), (1,)), ((), ()))
TN = (((0,), (0,)), ((), ()))


def _mm(name, a, b, out_shape, grid, a_spec, b_spec, o_spec, dims, a2d, b2d, k_axis, sem, alias=None, comm=None):
    def body(*refs):
        a_ref, b_ref, o_ref = refs[0], refs[1], refs[-1]
        r = lax.dot_general(a_ref[...].reshape(a2d), b_ref[...].reshape(b2d), dims,
                            preferred_element_type=F32)
        r = r.reshape(o_ref.shape)
        if k_axis is None:
            o_ref[...] = r.astype(o_ref.dtype)
        else:
            k = pl.program_id(k_axis)

            @pl.when(k == 0)
            def _():
                o_ref[...] = r

            @pl.when(k > 0)
            def _():
                o_ref[...] += r

    operands, in_specs, aliases = [a, b], [a_spec, b_spec], {}
    if alias is not None:
        operands.append(alias)
        in_specs.append(ANY)
        aliases = {2: 0}
    res, extra = _hosted_call(body, operands, name=name, grid=grid, in_specs=in_specs, out_specs=o_spec,
                              out_shape=out_shape, sem=sem, aliases=aliases, comm=comm)
    return res if comm is None else (res, extra)


def _tile(n, pref):
    t = min(n, pref)
    while n % t:
        t -= 128
    return t


def _mm_nn_in(a, w, l, name, comm=None):
    M, K = a.shape
    _, _, _, n = w.shape
    tm, tn = min(M, 512), _tile(n, 1024)
    nb = n // tn
    return _mm(name, a, w, jax.ShapeDtypeStruct((M, NDEV * n), F32), (NDEV * nb, M // tm),
               pl.BlockSpec((tm, K), lambda j, i: (i, 0)),
               pl.BlockSpec((1, 1, K, tn), lambda j, i: (j // nb, l, 0, j % nb)),
               pl.BlockSpec((tm, tn), lambda j, i: (i, j)),
               NN, (tm, K), (K, tn), None, ("parallel", "parallel"), comm=comm)


def _mm_nn_out(a, w, l, name):
    M, K = a.shape
    _, _, kb, N = w.shape
    tm, tn = min(M, 512), _tile(N, 1024)
    return _mm(name, a, w, jax.ShapeDtypeStruct((M, N), F32), (N // tn, M // tm),
               pl.BlockSpec((tm, K), lambda j, i: (i, 0)),
               pl.BlockSpec((NDEV, 1, kb, tn), lambda j, i: (0, l, 0, j)),
               pl.BlockSpec((tm, tn), lambda j, i: (i, j)),
               NN, (tm, K), (K, tn), None, ("parallel", "parallel"))


def _mm_nt_in(a, w, l, name, comm=None):
    M, _ = a.shape
    _, _, K, n = w.shape
    tm, tk = min(M, 1024), _tile(K, 1024)
    gb = 2 if n <= 1024 else 1

    def body(a_ref, w_ref, o_ref):
        k = pl.program_id(2)
        r = lax.dot_general(a_ref[:, :n], w_ref[0, 0], NT, preferred_element_type=F32)
        for g in range(1, gb):
            r = r + lax.dot_general(a_ref[:, g * n:(g + 1) * n], w_ref[g, 0], NT, preferred_element_type=F32)

        @pl.when(k == 0)
        def _():
            o_ref[...] = r

        @pl.when(k > 0)
        def _():
            o_ref[...] += r

    res, extra = _hosted_call(
        body, [a, w], name=name, grid=(M // tm, K // tk, NDEV // gb),
        in_specs=[pl.BlockSpec((tm, gb * n), lambda i, j, k: (i, k)),
                  pl.BlockSpec((gb, 1, tk, n), lambda i, j, k: (k, l, j, 0))],
        out_specs=pl.BlockSpec((tm, tk), lambda i, j, k: (i, j)),
        out_shape=jax.ShapeDtypeStruct((M, K), F32),
        sem=("parallel", "parallel", "arbitrary"), comm=comm)
    return res if comm is None else (res, extra)


def _mm_nt_out(a, w, l, name):
    M, N = a.shape
    _, _, kb, _ = w.shape
    K = NDEV * kb
    tm, tk, tc = min(M, 1024), _tile(K, 1024), _tile(N, 1024)
    per = tk // kb
    return _mm(name, a, w, jax.ShapeDtypeStruct((M, K), F32), (M // tm, K // tk, N // tc),
               pl.BlockSpec((tm, tc), lambda i, j, k: (i, k)),
               pl.BlockSpec((per, 1, kb, tc), lambda i, j, k: (j, l, 0, k)),
               pl.BlockSpec((tm, tk), lambda i, j, k: (i, j)),
               NT, (tm, tc), (tk, tc), 2, ("parallel", "parallel", "arbitrary"))


def _mm_tn_in(a, b, l, L, buf, name, comm=None, part=(0, 1)):
    S, K = a.shape
    K = K // part[1]
    n = b.shape[1] // NDEV
    ts, tk, tn = min(S, 2048), _tile(K, 1024), _tile(n, 1024)
    nb = n // tn
    first = part[0] * (K // tk)
    return _mm(name, a, b, jax.ShapeDtypeStruct((NDEV, L, K, n), F32), (NDEV * nb, K // tk, S // ts),
               pl.BlockSpec((ts, tk), lambda j, i, s: (s, first + i)),
               pl.BlockSpec((ts, tn), lambda j, i, s: (s, j)),
               pl.BlockSpec((1, 1, tk, tn), lambda j, i, s: (j // nb, l, i, j % nb)),
               TN, (ts, tk), (ts, tn), 2, ("parallel", "parallel", "arbitrary"), alias=buf, comm=comm)


def _mm_tn_out(a, b, l, L, buf, name):
    S, K = a.shape
    N = b.shape[1]
    kb = K // NDEV
    ts, tk, tn = min(S, 2048), _tile(K, 1024), _tile(N, 1024)
    per = tk // kb
    return _mm(name, a, b, jax.ShapeDtypeStruct((NDEV, L, kb, N), F32), (N // tn, K // tk, S // ts),
               pl.BlockSpec((ts, tk), lambda j, i, s: (s, i)),
               pl.BlockSpec((ts, tn), lambda j, i, s: (s, j)),
               pl.BlockSpec((per, 1, kb, tn), lambda j, i, s: (i, l, 0, j)),
               TN, (ts, tk), (ts, tn), 2, ("parallel", "parallel", "arbitrary"), alias=buf)


def _attn_bias(T):
    reach = max(w // 2 for w, _ in DILATED_PATTERNS)
    hb = -(-reach // T)
    i = np.arange(T)[:, None]
    j = np.arange(T)[None, :]
    tiles = []
    for d in range(-hb, hb + 1):
        rel = j + d * T - i
        mult = np.zeros((T, T), np.float64)
        for window, dil in DILATED_PATTERNS:
            radius = window // (2 * dil)
            mult += (rel % dil == 0) & (np.abs(rel) <= radius * dil)
        tiles.append(np.where(mult > 0, np.log(np.maximum(mult, 1.0)), NEG_INF))
    return jnp.asarray(np.stack(tiles), F32)


def _rope_tables(S):
    half = HEAD_DIM // 2
    pos = jnp.arange(S, dtype=F32)
    inv = ROPE_THETA ** (-jnp.arange(half, dtype=F32) / half)
    ang = pos[:, None] * inv[None, :]
    cos, sin = jnp.cos(ang), jnp.sin(ang)
    return jnp.concatenate([cos, cos], axis=-1), jnp.concatenate([-sin, sin], axis=-1)


def _rope_apply(t, cosf, sinf, heads, sign):
    outs = []
    for hh in range(heads):
        th = t[:, hh * HEAD_DIM:(hh + 1) * HEAD_DIM]
        outs.append(th * cosf + sign * (pltpu.roll(th, HEAD_DIM // 2, 1) * sinf))
    return outs


def _rope_qkv(proj, cosf, sinf, W, name):
    S = proj.shape[0]
    tr = min(S, ROW_TILE)
    heads = W // HEAD_DIM

    def body(q_ref, k_ref, v_ref, c_ref, s_ref, qo_ref, ko_ref, vo_ref):
        cosf_v, sinf_v = c_ref[...], s_ref[...]
        for src, dst, mult in ((q_ref, qo_ref, HEAD_DIM ** -0.5), (k_ref, ko_ref, 1.0)):
            for hh, val in enumerate(_rope_apply(src[...], cosf_v, sinf_v, heads, 1.0)):
                dst[:, hh * HEAD_DIM:(hh + 1) * HEAD_DIM] = (val * mult).astype(BF16)
        vo_ref[...] = v_ref[...].astype(BF16)

    piece = lambda p: pl.BlockSpec((tr, W), lambda i: (i, p))
    tab = pl.BlockSpec((tr, HEAD_DIM), lambda i: (i, 0))
    out = pl.BlockSpec((tr, W), lambda i: (i, 0))
    shp = jax.ShapeDtypeStruct((S, W), BF16)
    return pl.pallas_call(
        body, name=name, grid=(S // tr,),
        in_specs=[piece(0), piece(1), piece(2), tab, tab], out_specs=[out] * 3, out_shape=[shp] * 3,
        compiler_params=_cp("parallel"),
    )(proj, proj, proj, cosf, sinf)


def _attn_fwd(q, k, v, bias, name, comm=None):
    S, W = q.shape
    H = W // HEAD_DIM
    nd, T, _ = bias.shape
    hb, nq = nd // 2, S // T
    scale = HEAD_DIM ** -0.5
    hp = min(H, HEADS_PER_STEP)
    rc = min(T, ATTN_ROW_CHUNK)
    wp = hp * HEAD_DIM

    def body(q_ref, k_ref, v_ref, b_ref, o_ref, lse_ref, m_s, l_s, acc_s):
        i, d = pl.program_id(1), pl.program_id(2)
        j = i + d - hb

        @pl.when(d == 0)
        def _():
            m_s[...] = jnp.full(m_s.shape, -jnp.inf, F32)
            l_s[...] = jnp.zeros(l_s.shape, F32)
            acc_s[...] = jnp.zeros(acc_s.shape, F32)

        @pl.when((j >= 0) & (j < nq))
        def _():
            items = [(hh, c) for hh in range(hp) for c in range(T // rc)]

            def scores(item):
                hh, c = item
                cols, rows = slice(hh * HEAD_DIM, (hh + 1) * HEAD_DIM), slice(c * rc, (c + 1) * rc)
                return (lax.dot_general(q_ref[rows, cols], k_ref[:, cols], NT, preferred_element_type=F32)
                        + b_ref[d, rows, :])

            def weighted_values(item, p, alpha):
                hh, c = item
                cols, rows = slice(hh * HEAD_DIM, (hh + 1) * HEAD_DIM), slice(c * rc, (c + 1) * rc)
                acc_s[rows, cols] = alpha * acc_s[rows, cols] + jnp.dot(p, v_ref[:, cols],
                                                                        preferred_element_type=F32)

            s_next, pending = scores(items[0]), None
            for n, (hh, c) in enumerate(items):
                rows = slice(c * rc, (c + 1) * rc)
                s = s_next
                if n + 1 < len(items):
                    s_next = scores(items[n + 1])
                if pending is not None:
                    weighted_values(*pending)
                parts = [s[:, t * LANES:(t + 1) * LANES] for t in range(T // LANES)]
                m_old = m_s[hh, rows, :]
                m_cur = jnp.max(functools.reduce(jnp.maximum, parts), axis=1, keepdims=True)
                m_new = jnp.maximum(m_old, m_cur)
                alpha = jnp.exp(m_old - m_new)
                ps = [jnp.exp(part - m_new) for part in parts]
                l_s[hh, rows, :] = alpha * l_s[hh, rows, :] + functools.reduce(jnp.add, ps)
                m_s[hh, rows, :] = m_new
                pending = ((hh, c), jnp.concatenate(ps, axis=1).astype(BF16), alpha)
            weighted_values(*pending)

        @pl.when(d == nd - 1)
        def _():
            for hh in range(hp):
                cols = slice(hh * HEAD_DIM, (hh + 1) * HEAD_DIM)
                l = jnp.sum(l_s[hh], axis=1, keepdims=True)
                o_ref[:, cols] = acc_s[:, cols] / l
                lse_ref[hh] = m_s[hh][:, :1] + jnp.log(l)

    kv = pl.BlockSpec((T, wp), lambda h, i, d: (jnp.clip(i + d - hb, 0, nq - 1), h))
    return _hosted_call(
        body, [q, k, v, bias], name=name, grid=(H // hp, nq, nd),
        in_specs=[pl.BlockSpec((T, wp), lambda h, i, d: (i, h)), kv, kv,
                  pl.BlockSpec((nd, T, T), lambda h, i, d: (0, 0, 0))],
        out_specs=[pl.BlockSpec((T, wp), lambda h, i, d: (i, h)),
                   pl.BlockSpec((hp, T, 1), lambda h, i, d: (h, i, 0))],
        out_shape=[jax.ShapeDtypeStruct((S, W), F32), jax.ShapeDtypeStruct((H, S, 1), F32)],
        scratch_shapes=[pltpu.VMEM((hp, T, LANES), F32), pltpu.VMEM((hp, T, LANES), F32),
                        pltpu.VMEM((T, wp), F32)],
        sem=("parallel", "parallel", "arbitrary"), comm=comm)


def _attn_bwd(q, k, v, do, lse, delta, bias, name, comm=None):
    S, W = q.shape
    H = W // HEAD_DIM
    nd, T, _ = bias.shape
    hb, nq = nd // 2, S // T
    scale = HEAD_DIM ** -0.5
    hp = min(H, HEADS_PER_STEP)
    rc = min(T, ATTN_ROW_CHUNK)
    wp = hp * HEAD_DIM

    def body(q_ref, do_ref, lse_ref, dl_ref, k_ref, v_ref, b_ref, dq_ref, dk_ref, dv_ref):
        j, d = pl.program_id(1), pl.program_id(2)
        i = j + d - hb

        @pl.when((j == 0) & (d == 0))
        def _():
            dq_ref[...] = jnp.zeros(dq_ref.shape, F32)

        @pl.when(d == 0)
        def _():
            dk_ref[...] = jnp.zeros(dk_ref.shape, F32)
            dv_ref[...] = jnp.zeros(dv_ref.shape, F32)

        @pl.when((i >= 0) & (i < nq))
        def _():
            items = [(hh, c) for hh in range(hp) for c in range(T // rc)]

            def slices(item):
                hh, c = item
                return slice(hh * HEAD_DIM, (hh + 1) * HEAD_DIM), slice(c * rc, (c + 1) * rc)

            def products(item):
                cols, rows = slices(item)
                s = (lax.dot_general(q_ref[rows, cols], k_ref[:, cols], NT, preferred_element_type=F32)
                     + b_ref[nd - 1 - d, rows, :])
                dp = lax.dot_general(do_ref[rows, cols], v_ref[:, cols], NT, preferred_element_type=F32)
                return s, dp

            def gradients(item, p, ds):
                cols, rows = slices(item)
                dv_ref[:, cols] += lax.dot_general(p, do_ref[rows, cols], TN, preferred_element_type=F32)
                dk_ref[:, cols] += lax.dot_general(ds, q_ref[rows, cols], TN, preferred_element_type=F32)
                q_rows = pl.ds(pl.multiple_of(i * T + item[1] * rc, rc), rc)
                dq_ref[q_rows, cols] += jnp.dot(ds, k_ref[:, cols], preferred_element_type=F32) * scale

            nxt, pending = products(items[0]), None
            for n, item in enumerate(items):
                s, dp = nxt
                if n + 1 < len(items):
                    nxt = products(items[n + 1])
                if pending is not None:
                    gradients(*pending)
                _, rows = slices(item)
                p = jnp.exp(s - lse_ref[item[0], rows, :])
                ds = p * (dp - dl_ref[item[0], rows, :])
                pending = (item, p.astype(BF16), ds.astype(BF16))
            gradients(*pending)

    qi = lambda h, j, d: (jnp.clip(j + d - hb, 0, nq - 1), h)
    qs = pl.BlockSpec((T, wp), qi)
    col = pl.BlockSpec((hp, T, 1), lambda h, j, d: (h, jnp.clip(j + d - hb, 0, nq - 1), 0))
    kv = pl.BlockSpec((T, wp), lambda h, j, d: (j, h))
    shp = jax.ShapeDtypeStruct((S, W), F32)
    return _hosted_call(
        body, [q, do, lse, delta, k, v, bias], name=name, grid=(H // hp, nq, nd),
        in_specs=[qs, qs, col, col, kv, kv, pl.BlockSpec((nd, T, T), lambda h, j, d: (0, 0, 0))],
        out_specs=[pl.BlockSpec((S, wp), lambda h, j, d: (0, h)), kv, kv],
        out_shape=[shp, shp, shp],
        sem=("parallel", "arbitrary", "arbitrary"), comm=comm)


def _halo_specs(S, tr, W, piece):
    per, last = tr // 8, S // 8 - 1
    prev = pl.BlockSpec((8, W), lambda i: (jnp.maximum(i * per - 1, 0), piece))
    nxt = pl.BlockSpec((8, W), lambda i: (jnp.minimum((i + 1) * per, last), piece))
    return prev, nxt


def _shifted(t, before, after, tr):
    rows = lax.broadcasted_iota(jnp.int32, (tr, 1), 0)
    prev = jnp.where(rows == 0, before, pltpu.roll(t, 1, 0))
    nxt = jnp.where(rows == tr - 1, after, pltpu.roll(t, tr - 1, 0))
    return prev, nxt


def _ab_mix(attn, proj, conv_w, W, name):
    S = attn.shape[0]
    tr = min(S, ROW_TILE)
    nsteps = S // tr

    def body(a_ref, za_ref, ub_ref, gb_ref, gc_ref, zb_ref, ubp, ubn, gcp, gcn, w_ref, y_ref):
        i = pl.program_id(0)
        t = gc_ref[...] * ub_ref[...]
        before = jnp.where(i == 0, 0.0, (gcp[...] * ubp[...])[7:8, :])
        after = jnp.where(i == nsteps - 1, 0.0, (gcn[...] * ubn[...])[0:1, :])
        t_prev, t_next = _shifted(t, before, after, tr)
        w = w_ref[...]
        cv = w[0:1, :] * t_prev + w[1:2, :] * t + w[2:3, :] * t_next
        silu_a, _ = _silu_and_grad(za_ref[...])
        silu_b, _ = _silu_and_grad(zb_ref[...])
        y_ref[:, :W] = (a_ref[...] * silu_a).astype(BF16)
        y_ref[:, W:] = (gb_ref[...] * cv * silu_b).astype(BF16)

    piece = lambda p: pl.BlockSpec((tr, W), lambda i: (i, p))
    ubp, ubn = _halo_specs(S, tr, W, 4)
    gcp, gcn = _halo_specs(S, tr, W, 6)
    return pl.pallas_call(
        body, name=name, grid=(nsteps,),
        in_specs=[pl.BlockSpec((tr, W), lambda i: (i, 0)), piece(3), piece(4), piece(5), piece(6), piece(7),
                  ubp, ubn, gcp, gcn, pl.BlockSpec((3, W), lambda i: (0, 0))],
        out_specs=pl.BlockSpec((tr, 2 * W), lambda i: (i, 0)),
        out_shape=jax.ShapeDtypeStruct((S, 2 * W), BF16),
        compiler_params=_cp("parallel"),
    )(attn, proj, proj, proj, proj, proj, proj, proj, proj, proj, conv_w)


def _dattn_prep(dy, proj, attn, W, name):
    S = attn.shape[0]
    H = W // HEAD_DIM
    tr = min(S, ROW_TILE)

    def body(dy_ref, za_ref, a_ref, do_ref, dl_ref):
        silu_a, _ = _silu_and_grad(za_ref[...])
        do = dy_ref[...] * silu_a
        do_ref[...] = do.astype(BF16)
        prod = do * a_ref[...]
        for hh in range(H):
            dl_ref[hh] = jnp.sum(prod[:, hh * HEAD_DIM:(hh + 1) * HEAD_DIM], axis=1, keepdims=True)

    row = pl.BlockSpec((tr, W), lambda i: (i, 0))
    return pl.pallas_call(
        body, name=name, grid=(S // tr,),
        in_specs=[row, pl.BlockSpec((tr, W), lambda i: (i, 3)), row],
        out_specs=[row, pl.BlockSpec((H, tr, 1), lambda i: (0, i, 0))],
        out_shape=[jax.ShapeDtypeStruct((S, W), BF16), jax.ShapeDtypeStruct((H, S, 1), F32)],
        compiler_params=_cp("parallel"),
    )(dy, proj, attn)


def _ab_bwd(dy, attn, proj, dqr, dkr, dv, cosf, sinf, conv_w, W, name):
    S = attn.shape[0]
    tr = min(S, ROW_TILE // 2)
    nsteps = S // tr
    heads = W // HEAD_DIM

    def body(dya_ref, dyb_ref, a_ref, za_ref, ub_ref, gb_ref, gc_ref, zb_ref, dq_ref, dk_ref, dv_ref,
             c_ref, s_ref, w_ref, dybp, dybn, gbp, gbn, zbp, zbn, ubp, ubn, gcp, gcn,
             dp_ref, dw_ref, acc):
        i = pl.program_id(0)
        first, last = i == 0, i == nsteps - 1
        w = w_ref[...]
        w0, w1, w2 = w[0:1, :], w[1:2, :], w[2:3, :]
        ub, gb, gc, zb = ub_ref[...], gb_ref[...], gc_ref[...], zb_ref[...]
        dyb = dyb_ref[...]
        silu_a, dsilu_a = _silu_and_grad(za_ref[...])
        silu_b, dsilu_b = _silu_and_grad(zb)
        t = gc * ub
        t_prev, t_next = _shifted(t, jnp.where(first, 0.0, (gcp[...] * ubp[...])[7:8, :]),
                                  jnp.where(last, 0.0, (gcn[...] * ubn[...])[0:1, :]), tr)
        cv = w0 * t_prev + w1 * t + w2 * t_next
        dcv = dyb * gb * silu_b
        halo_p = dybp[...] * gbp[...] * _silu_and_grad(zbp[...])[0]
        halo_n = dybn[...] * gbn[...] * _silu_and_grad(zbn[...])[0]
        dcv_prev, dcv_next = _shifted(dcv, jnp.where(first, 0.0, halo_p[7:8, :]),
                                      jnp.where(last, 0.0, halo_n[0:1, :]), tr)
        dt = w0 * dcv_next + w1 * dcv + w2 * dcv_prev
        cosf_v, sinf_v = c_ref[...], s_ref[...]
        for src, base in ((dq_ref, 0), (dk_ref, W)):
            for hh, val in enumerate(_rope_apply(src[...], cosf_v, sinf_v, heads, -1.0)):
                dp_ref[:, base + hh * HEAD_DIM:base + (hh + 1) * HEAD_DIM] = val.astype(BF16)
        dp_ref[:, 2 * W:3 * W] = dv_ref[...].astype(BF16)
        dp_ref[:, 3 * W:4 * W] = (dya_ref[...] * a_ref[...] * dsilu_a).astype(BF16)
        dp_ref[:, 4 * W:5 * W] = (dt * gc).astype(BF16)
        dp_ref[:, 5 * W:6 * W] = (dyb * cv * silu_b).astype(BF16)
        dp_ref[:, 6 * W:7 * W] = (dt * ub).astype(BF16)
        dp_ref[:, 7 * W:8 * W] = (dyb * gb * cv * dsilu_b).astype(BF16)
        tap = lax.broadcasted_iota(jnp.int32, (8, 1), 0)
        part = (jnp.where(tap == 0, jnp.sum(dcv * t_prev, axis=0, keepdims=True), 0.0)
                + jnp.where(tap == 1, jnp.sum(dcv * t, axis=0, keepdims=True), 0.0)
                + jnp.where(tap == 2, jnp.sum(dcv * t_next, axis=0, keepdims=True), 0.0))

        @pl.when(first)
        def _():
            acc[...] = part

        @pl.when(i > 0)
        def _():
            acc[...] += part

        @pl.when(last)
        def _():
            dw_ref[...] = acc[...]

    row = pl.BlockSpec((tr, W), lambda i: (i, 0))
    piece = lambda p: pl.BlockSpec((tr, W), lambda i: (i, p))
    tab = pl.BlockSpec((tr, HEAD_DIM), lambda i: (i, 0))
    dybp, dybn = _halo_specs(S, tr, W, 1)
    gbp, gbn = _halo_specs(S, tr, W, 5)
    zbp, zbn = _halo_specs(S, tr, W, 7)
    ubp, ubn = _halo_specs(S, tr, W, 4)
    gcp, gcn = _halo_specs(S, tr, W, 6)
    return pl.pallas_call(
        body, name=name, grid=(nsteps,),
        in_specs=[piece(0), piece(1), row, piece(3), piece(4), piece(5), piece(6), piece(7), row, row, row,
                  tab, tab, pl.BlockSpec((3, W), lambda i: (0, 0)),
                  dybp, dybn, gbp, gbn, zbp, zbn, ubp, ubn, gcp, gcn],
        out_specs=[pl.BlockSpec((tr, 8 * W), lambda i: (i, 0)), pl.BlockSpec((8, W), lambda i: (0, 0))],
        out_shape=[jax.ShapeDtypeStruct((S, 8 * W), BF16), jax.ShapeDtypeStruct((8, W), F32)],
        scratch_shapes=[pltpu.VMEM((8, W), F32)],
        compiler_params=_cp("arbitrary"),
    )(dy, dy, attn, proj, proj, proj, proj, proj, dqr, dkr, dv, cosf, sinf, conv_w,
      dy, dy, proj, proj, proj, proj, proj, proj, proj, proj)


def _sgu_core(p_ref, lng_ref, lnb_ref, ws_ref, bst_ref, Dc):
    gw = Dc // C_GROUPS
    u_raw, v_raw, z = p_ref[:, :Dc], p_ref[:, Dc:2 * Dc], p_ref[:, 2 * Dc:]
    u, du = _gelu_and_grad(u_raw)
    vg, dvg = _gelu_and_grad(v_raw)
    mu = jnp.mean(vg, axis=-1, keepdims=True)
    vc = vg - mu
    rstd = lax.rsqrt(jnp.mean(vc * vc, axis=-1, keepdims=True) + EPS)
    vhat = vc * rstd
    vn = (vhat * lng_ref[...] + lnb_ref[...]).astype(BF16)
    bst = bst_ref[...]
    mixed = jnp.concatenate(
        [jnp.dot(ws_ref[g].astype(BF16), vn[:, g * gw:(g + 1) * gw], preferred_element_type=F32)
         + bst[:, g:g + 1] for g in range(C_GROUPS)], axis=1)
    sz, dsz = _silu_and_grad(z)
    return u, du, dvg, rstd, vhat, vn, mixed, sz, dsz


def _sgu_fwd(proj, ln_g, ln_b, w_s, b_st, name, comm=None):
    S, Dc3 = proj.shape
    Dc = Dc3 // 3
    vec = pl.BlockSpec((1, Dc), lambda i: (0, 0))

    def body(p_ref, lng_ref, lnb_ref, ws_ref, bst_ref, y_ref):
        u, _, _, _, _, _, mixed, sz, _ = _sgu_core(p_ref, lng_ref, lnb_ref, ws_ref, bst_ref, Dc)
        y_ref[...] = (u * mixed * sz).astype(BF16)

    return _hosted_call(
        body, [proj, ln_g, ln_b, w_s, b_st], name=name, grid=(S // C_CHUNK,),
        in_specs=[pl.BlockSpec((C_CHUNK, Dc3), lambda i: (i, 0)), vec, vec,
                  pl.BlockSpec((C_GROUPS, C_CHUNK, C_CHUNK), lambda i: (0, 0, 0)),
                  pl.BlockSpec((C_CHUNK, C_GROUPS), lambda i: (0, 0))],
        out_specs=pl.BlockSpec((C_CHUNK, Dc), lambda i: (i, 0)),
        out_shape=jax.ShapeDtypeStruct((S, Dc), BF16),
        sem=("parallel",), comm=comm)


def _sgu_bwd(proj, dy, ln_g, ln_b, w_s, w_st, b_st, name):
    S, Dc3 = proj.shape
    Dc = Dc3 // 3
    gw = Dc // C_GROUPS
    nsteps = S // C_CHUNK
    vec = pl.BlockSpec((1, Dc), lambda i: (0, 0))
    wspec = pl.BlockSpec((C_GROUPS, C_CHUNK, C_CHUNK), lambda i: (0, 0, 0))

    def body(p_ref, dy_ref, lng_ref, lnb_ref, ws_ref, wst_ref, bst_ref,
             dp_ref, dws_ref, dbs_ref, dlg_ref, dlb_ref, acc_w, acc_b, acc_g, acc_lb):
        i = pl.program_id(0)
        u, du, dvg, rstd, vhat, vn, mixed, sz, dsz = _sgu_core(p_ref, lng_ref, lnb_ref, ws_ref, bst_ref, Dc)
        dy = dy_ref[...]
        dmixed = dy * u * sz
        dmb = dmixed.astype(BF16)

        @pl.when(i == 0)
        def _():
            acc_w[...] = jnp.zeros(acc_w.shape, F32)
            acc_b[...] = jnp.zeros(acc_b.shape, F32)
            acc_g[...] = jnp.zeros(acc_g.shape, F32)
            acc_lb[...] = jnp.zeros(acc_lb.shape, F32)

        dvn_parts = []
        for g in range(C_GROUPS):
            dmg = dmb[:, g * gw:(g + 1) * gw]
            acc_w[g] += lax.dot_general(dmg, vn[:, g * gw:(g + 1) * gw], NT, preferred_element_type=F32)
            acc_b[g] += dmixed[:, g * gw:(g + 1) * gw]
            dvn_parts.append(jnp.dot(wst_ref[g].astype(BF16), dmg, preferred_element_type=F32))
        dvn = jnp.concatenate(dvn_parts, axis=1)
        acc_g[...] += jnp.sum((dvn * vhat).reshape(C_CHUNK // 8, 8, Dc), axis=0)
        acc_lb[...] += jnp.sum(dvn.reshape(C_CHUNK // 8, 8, Dc), axis=0)
        dvh = dvn * lng_ref[...]
        dvgelu = rstd * (dvh - jnp.mean(dvh, axis=-1, keepdims=True)
                         - vhat * jnp.mean(dvh * vhat, axis=-1, keepdims=True))
        dp_ref[:, :Dc] = (dy * mixed * sz * du).astype(BF16)
        dp_ref[:, Dc:2 * Dc] = (dvgelu * dvg).astype(BF16)
        dp_ref[:, 2 * Dc:] = (dy * u * mixed * dsz).astype(BF16)

        @pl.when(i == nsteps - 1)
        def _():
            dws_ref[...] = acc_w[...]
            for g in range(C_GROUPS):
                dbs_ref[g] = jnp.sum(acc_b[g], axis=1, keepdims=True)
            dlg_ref[...] = jnp.sum(acc_g[...], axis=0, keepdims=True)
            dlb_ref[...] = jnp.sum(acc_lb[...], axis=0, keepdims=True)

    v = jax.ShapeDtypeStruct((1, Dc), F32)
    return pl.pallas_call(
        body, name=name, grid=(nsteps,),
        in_specs=[pl.BlockSpec((C_CHUNK, Dc3), lambda i: (i, 0)), pl.BlockSpec((C_CHUNK, Dc), lambda i: (i, 0)),
                  vec, vec, wspec, wspec, pl.BlockSpec((C_CHUNK, C_GROUPS), lambda i: (0, 0))],
        out_specs=[pl.BlockSpec((C_CHUNK, Dc3), lambda i: (i, 0)), wspec,
                   pl.BlockSpec((C_GROUPS, C_CHUNK, 1), lambda i: (0, 0, 0)), vec, vec],
        out_shape=[jax.ShapeDtypeStruct((S, Dc3), BF16),
                   jax.ShapeDtypeStruct((C_GROUPS, C_CHUNK, C_CHUNK), F32),
                   jax.ShapeDtypeStruct((C_GROUPS, C_CHUNK, 1), F32), v, v],
        scratch_shapes=[pltpu.VMEM((C_GROUPS, C_CHUNK, C_CHUNK), F32), pltpu.VMEM((C_GROUPS, C_CHUNK, gw), F32),
                        pltpu.VMEM((8, Dc), F32), pltpu.VMEM((8, Dc), F32)],
        compiler_params=_cp("arbitrary"),
    )(proj, dy, ln_g, ln_b, w_s, w_st, b_st)


PACK_COLS = 1024
PACK_ROWS = 64


def _pack(vectors):
    flat = jnp.concatenate([v.reshape(-1) for v in vectors])
    pad = (-flat.shape[0]) % (PACK_COLS * PACK_ROWS)
    return jnp.pad(flat, (0, pad)).reshape(-1, PACK_COLS)


def _unshard(g, off, shape):
    L, rest = shape[0], shape[1:]
    size = int(np.prod(shape))
    piece = g[:, off:off + size].reshape((NDEV,) + tuple(shape))
    nd = piece.ndim
    perm = tuple(range(1, nd - 1)) + (0, nd - 1)
    full = jnp.transpose(piece, perm)
    return full.reshape(tuple(shape[:-1]) + (NDEV * shape[-1],)), off + size


def kernel(x, c, ab_norm_g, ab_w_mod, ab_b_mod, ab_w_in, ab_conv_w, ab_w_out, sg_norm_g, sg_w_mod, sg_b_mod, sg_w_in, sg_ln_g, sg_ln_b, sg_w_s, sg_b_s, sg_w_out, final_norm_g, loss_target, m_ab_norm_g, m_ab_w_mod, m_ab_b_mod, m_ab_w_in, m_ab_conv_w, m_ab_w_out, m_sg_norm_g, m_sg_w_mod, m_sg_b_mod, m_sg_w_in, m_sg_ln_g, m_sg_ln_b, m_sg_w_s, m_sg_b_s, m_sg_w_out, m_final_norm_g, v_ab_norm_g, v_ab_w_mod, v_ab_b_mod, v_ab_w_in, v_ab_conv_w, v_ab_w_out, v_sg_norm_g, v_sg_w_mod, v_sg_b_mod, v_sg_w_in, v_sg_ln_g, v_sg_ln_b, v_sg_w_s, v_sg_b_s, v_sg_w_out, v_final_norm_g):
    _, S, D = x.shape
    L = ab_norm_g.shape[0]
    W = ab_conv_w.shape[2] * NDEV
    n_ab, n_sg = ab_w_in.shape[2], sg_w_in.shape[2]
    n_mod = ab_w_mod.shape[2]
    kb = ab_w_out.shape[1]
    xi, yi, ci = _position()
    dev = 4 * xi + 2 * yi + ci
    x2, tgt = x.reshape(S, D), loss_target.reshape(S, D)

    small = [c, ab_conv_w, sg_norm_g, sg_ln_g, sg_ln_b]
    (g1,) = _comm_only(_Gather([_pack(small)]), "ag_small")
    g1 = g1.reshape(NDEV, -1)
    c_all = g1[:, :D]
    off = D
    conv_full, off = _unshard(g1, off, ab_conv_w.shape)
    sg_norm_full, off = _unshard(g1, off, sg_norm_g.shape)
    ln_g_full, off = _unshard(g1, off, sg_ln_g.shape)
    ln_b_full, off = _unshard(g1, off, sg_ln_b.shape)

    ab_b_cols = lax.dynamic_slice_in_dim(ab_b_mod, dev * n_mod, n_mod, axis=1)
    m_ab = _mod_fwd(c_all, ab_w_mod, ab_b_cols.reshape(L, 1, n_mod), "mod_fwd_ab")
    m_sg = _mod_fwd(c_all, sg_w_mod, sg_b_mod.reshape(L, 1, n_mod), "mod_fwd_sg")
    m_part = jnp.stack([m_ab, m_sg]).transpose(2, 0, 1, 3).reshape(NDEV, 2 * L * n_mod)
    (g2,) = _comm_only(_Gather([m_part]), "ag_mod")
    mine = lax.dynamic_index_in_dim(g2, dev, axis=1, keepdims=False)
    mods = mine.reshape(NDEV, 2, L, n_mod).transpose(1, 2, 0, 3).reshape(2, L, 3 * D)

    def mod_of(kind, i):
        m = mods[kind, i]
        return m[:D].reshape(1, D), m[D:2 * D].reshape(1, D), m[2 * D:].reshape(1, D)

    big_w = [[(ab_w_in, m_ab_w_in, v_ab_w_in), (ab_w_out, m_ab_w_out, v_ab_w_out)],
             [(sg_w_in, m_sg_w_in, v_sg_w_in), (sg_w_out, m_sg_w_out, v_sg_w_out)]]
    big_names = [["ab_w_in", "ab_w_out"], ["sg_w_in", "sg_w_out"]]
    n_layers = 2 * L
    shards = [[big_w[layer % 2][k][0][layer // 2].astype(BF16) for k in range(2)] for layer in range(n_layers)]
    gathered = {}

    def gather_of(keys):
        keys = [key for key in keys if key[0] < n_layers]
        return keys, (_Gather([shards[layer][k] for layer, k in keys]) if keys else None)

    def keep_gathered(keys, res):
        for (layer, k), g in zip(keys, res):
            gathered[(layer, k)] = g.reshape((NDEV, 1, D, g.shape[-1]) if k == 0 else (NDEV, 1, kb, D))

    keys, comm = gather_of([(0, 0)])
    keep_gathered(keys, _comm_only(comm, "ag_w_in_layer0"))

    cosf, sinf = _rope_tables(S)
    T = min(S, ATTN_TILE)
    bias = _attn_bias(T)
    norm_g = [ab_norm_g, sg_norm_full]
    w_s_t = jnp.swapaxes(sg_w_s, -1, -2)
    b_s_t = jnp.swapaxes(sg_b_s, -1, -2)

    saved = []
    x_cur, res, gate_prev = x2, None, None
    for layer in range(2 * L):
        kind, i = layer % 2, layer // 2
        tag = f"{'ab' if kind == 0 else 'sg'}{i}"
        shift, scale, gate = mod_of(kind, i)
        g = norm_g[kind][i].reshape(1, D)
        xl, h = _pre(x_cur, res, gate_prev, g, scale, shift, f"pre_{tag}")
        keys, comm = gather_of([(layer + 1, 0)])
        if comm is None:
            proj = _mm_nn_in(h, gathered[(layer, 0)], 0, f"proj_{tag}")
        else:
            proj, got = _mm_nn_in(h, gathered[(layer, 0)], 0, f"proj_{tag}", comm)
            keep_gathered(keys, got)
        rec = dict(xl=xl, h=h, proj=proj, g=g, scale=scale, gate=gate)
        keys, comm = gather_of(([(0, 1)] if layer == 0 else []) + [(layer + 1, 1)])
        if kind == 0:
            qr, kr, vb = _rope_qkv(proj, cosf, sinf, W, f"rope_{tag}")
            (attn, lse), got = _attn_fwd(qr, kr, vb, bias, f"attn_{tag}", comm)
            y = _ab_mix(attn, proj, conv_full[i], W, f"mix_{tag}")
            rec.update(qr=qr, kr=kr, vb=vb, attn=attn, lse=lse)
        else:
            y, got = _sgu_fwd(proj, ln_g_full[i].reshape(1, D), ln_b_full[i].reshape(1, D), sg_w_s[i], b_s_t[i],
                              f"sgu_{tag}", comm)
        keep_gathered(keys, got)
        out = _mm_nn_out(y, gathered[(layer, 1)], 0, f"out_{tag}")
        rec.update(y=y, out=out)
        saved.append(rec)
        x_cur, res, gate_prev = xl, out, gate

    dx, loss_part, d_final_g = _loss_head(x_cur, res, gate_prev, final_norm_g.reshape(1, D), tgt, "loss_head")
    loss = lax.psum(loss_part[0, 0], ("x", "y", "c"))

    c_idx = ci.reshape(1).astype(jnp.int32)
    big_res = {}
    pending = None

    def sum_and_update(done, k, from_chips, part=(0, 1)):
        nm = big_names[done % 2][k]
        w, m, v = big_w[done % 2][k]
        flat = lambda a: a.reshape(L * a.shape[1], a.shape[2])
        row0 = (done // 2) * w.shape[1] + part[0] * (w.shape[1] // part[1])
        big_res[nm] = _sum_adam(from_chips, flat(w), flat(m), flat(v), row0, big_res.get(nm),
                                f"adam_{nm}{done // 2}_{part[0]}")

    def finish_layer(done, from_chips):
        for k in range(2):
            sum_and_update(done, k, from_chips[k])

    dm = [[None] * L, [None] * L]
    d_norm = [[None] * L, [None] * L]
    d_conv, d_lng, d_lnb, d_ws, d_bs = [None] * L, [None] * L, [None] * L, [None] * L, [None] * L
    for layer in reversed(range(2 * L)):
        kind, i = layer % 2, layer // 2
        tag = f"{'ab' if kind == 0 else 'sg'}{i}"
        rec = saved[layer]
        w_in_l, w_out_l = gathered[(layer, 0)], gathered[(layer, 1)]
        dout, dgate = _post_bwd(dx, rec["out"], rec["gate"], f"post_bwd_{tag}")
        dy = _mm_nt_out(dout, w_out_l, 0, f"dy_{tag}")
        dwo = _mm_tn_out(rec["y"], dout, 0, 1, None, f"dwout_{tag}")
        if kind == 0:
            do, delta = _dattn_prep(dy, rec["proj"], rec["attn"], W, f"dattn_{tag}")
            comm = _ToChips(pending[1]) if pending else None
            (dqr, dkr, dvv), got = _attn_bwd(rec["qr"], rec["kr"], rec["vb"], do, rec["lse"], delta, bias,
                                             f"attn_bwd_{tag}", comm)
            if pending:
                finish_layer(pending[0], got)
            dproj, dcw = _ab_bwd(dy, rec["attn"], rec["proj"], dqr, dkr, dvv, cosf, sinf, conv_full[i], W,
                                 f"mix_bwd_{tag}")
            d_conv[i] = dcw[:3]
            earlier = None
        else:
            dproj, d_ws[i], dbs, d_lng[i], d_lnb[i] = _sgu_bwd(
                rec["proj"], dy, ln_g_full[i].reshape(1, D), ln_b_full[i].reshape(1, D),
                sg_w_s[i], w_s_t[i], b_s_t[i], f"sgu_bwd_{tag}")
            d_bs[i] = dbs.reshape(C_GROUPS, C_CHUNK)
            earlier = pending
        if layer == 0:
            h_l = rec["h"]
            dwi_a = _mm_tn_in(h_l, dproj, 0, 1, None, f"dwin_a_{tag}", part=(0, 2))
            g_a = [dwi_a.reshape(NDEV, D // 2, -1), dwo.reshape(NDEV, kb, D)]
            dwi_b, sib_a = _mm_tn_in(h_l, dproj, 0, 1, None, f"dwin_b_{tag}", _ToSibling(g_a), part=(1, 2))
            p_a = [_add_sibling(g, r, c_idx, f"rs_add_a{k}_{tag}") for k, (g, r) in enumerate(zip(g_a, sib_a))]
            g_b = [dwi_b.reshape(NDEV, D // 2, -1)]
            both = _Both(_ToSibling(g_b), _ToChips(p_a))
            dh, got = _mm_nt_in(dproj, w_in_l, 0, f"dh_{tag}", both)
            sib_b, chips_a = both.split_results(got)
            p_b = [_add_sibling(g_b[0], sib_b[0], c_idx, f"rs_add_b_{tag}")]
            (dx, dshift, dscale, d_norm[kind][i]), chips_b = _pre_bwd(
                rec["xl"], dh, dx, rec["g"], rec["scale"], f"pre_bwd_{tag}", _ToChips(p_b))
            sum_and_update(0, 0, chips_a[0], (0, 2))
            sum_and_update(0, 1, chips_a[1])
            sum_and_update(0, 0, chips_b[0], (1, 2))
            dm[kind][i] = jnp.concatenate([dshift, dscale, dgate], axis=1).reshape(3 * D)
            continue
        if earlier:
            dwi, got_in = _mm_tn_in(rec["h"], dproj, 0, 1, None, f"dwin_{tag}", _ToChips(earlier[1][:1]))
        else:
            dwi = _mm_tn_in(rec["h"], dproj, 0, 1, None, f"dwin_{tag}")
        grads = [dwi.reshape(NDEV, D, -1), dwo.reshape(NDEV, kb, D)]
        if earlier:
            both = _Both(_ToSibling(grads), _ToChips(earlier[1][1:]))
            dh, got = _mm_nt_in(dproj, w_in_l, 0, f"dh_{tag}", both)
            from_sibling, got_out = both.split_results(got)
            finish_layer(earlier[0], [got_in[0], got_out[0]])
        else:
            dh, from_sibling = _mm_nt_in(dproj, w_in_l, 0, f"dh_{tag}", _ToSibling(grads))
        pending = (layer, [_add_sibling(g, r, c_idx, f"rs_add_{big_names[kind][k]}{i}")
                           for k, (g, r) in enumerate(zip(grads, from_sibling))])
        (dx, dshift, dscale, d_norm[kind][i]), _ = _pre_bwd(
            rec["xl"], dh, dx, rec["g"], rec["scale"], f"pre_bwd_{tag}")
        dm[kind][i] = jnp.concatenate([dshift, dscale, dgate], axis=1).reshape(3 * D)
    grad_x = dx.reshape(1, S, D)
    for kind in range(2):
        for k in range(2):
            nm = big_names[kind][k]
            big_res[nm] = [o.reshape(big_w[kind][k][0].shape) for o in big_res[nm]]

    stack = lambda xs: jnp.stack(xs)
    pack_items = [stack(dm[0]), stack(dm[1]), stack(d_norm[0]).reshape(L, D), stack(d_conv),
                  stack(d_norm[1]).reshape(L, D), stack(d_lng).reshape(L, D), stack(d_lnb).reshape(L, D),
                  stack(d_ws), stack(d_bs), d_final_g]
    (g3,) = _comm_only(_Gather([_pack(pack_items)]), "ag_grads")
    P = g3.shape[1] * g3.shape[2]
    tot = _sum_rows(g3, "sum_small").reshape(P)
    g3 = g3.reshape(NDEV, P)
    sizes = [int(np.prod(p.shape)) for p in pack_items]
    offs = np.concatenate([[0], np.cumsum(sizes)]).tolist()
    seg = lambda k, shape: tot[offs[k]:offs[k + 1]].reshape(shape)

    def shard(full, n):
        return lax.dynamic_slice_in_dim(full, dev * n, n, axis=full.ndim - 1)

    g_ab_b_mod = seg(0, (L, 3 * D))
    g_sg_b_mod = shard(seg(1, (L, 3 * D)), n_mod)
    g_ab_norm = seg(2, (L, D))
    g_conv = shard(seg(3, (L, 3, W)), W // NDEV)
    g_sg_norm = shard(seg(4, (L, D)), kb)
    g_ln_g = shard(seg(5, (L, D)), kb)
    g_ln_b = shard(seg(6, (L, D)), kb)
    g_w_s = seg(7, sg_w_s.shape)
    g_b_s = seg(8, sg_b_s.shape)
    g_final = seg(9, (D,))

    small_w = [("ab_norm_g", g_ab_norm, ab_norm_g, m_ab_norm_g, v_ab_norm_g),
               ("ab_b_mod", g_ab_b_mod, ab_b_mod, m_ab_b_mod, v_ab_b_mod),
               ("ab_conv_w", g_conv, ab_conv_w, m_ab_conv_w, v_ab_conv_w),
               ("sg_norm_g", g_sg_norm, sg_norm_g, m_sg_norm_g, v_sg_norm_g),
               ("sg_b_mod", g_sg_b_mod, sg_b_mod, m_sg_b_mod, v_sg_b_mod),
               ("sg_ln_g", g_ln_g, sg_ln_g, m_sg_ln_g, v_sg_ln_g),
               ("sg_ln_b", g_ln_b, sg_ln_b, m_sg_ln_b, v_sg_ln_b),
               ("sg_w_s", g_w_s, sg_w_s, m_sg_w_s, v_sg_w_s),
               ("sg_b_s", g_b_s, sg_b_s, m_sg_b_s, v_sg_b_s),
               ("final_norm_g", g_final, final_norm_g, m_final_norm_g, v_final_norm_g)]
    packed = [_pack([t[k] for t in small_w]) for k in (1, 2, 3, 4)]
    upd = _adam_only(*packed, "adam_small")
    small_res = {}
    o = 0
    for nm, g, w, _, _ in small_w:
        size = int(np.prod(w.shape))
        small_res[nm] = [g] + [u.reshape(-1)[o:o + size].reshape(w.shape) for u in upd]
        o += size

    KP = 128
    sc_t = jnp.pad((c_all * jax.nn.sigmoid(c_all)).T, ((0, 0), (0, KP - NDEV)))
    mod_res = {}
    for kind, nm, (w, m, v) in ((0, "ab_w_mod", (ab_w_mod, m_ab_w_mod, v_ab_w_mod)),
                                (1, "sg_w_mod", (sg_w_mod, m_sg_w_mod, v_sg_w_mod))):
        dm_all = g3[:, offs[kind]:offs[kind + 1]].reshape(NDEV, L, 3 * D)
        cols = jnp.pad(shard(dm_all, n_mod).transpose(1, 0, 2), ((0, 0), (0, KP - NDEV), (0, 0)))
        mod_res[nm] = _wmod_grad_adam(sc_t, cols, w, m, v, f"adam_{nm}")

    order = ["ab_norm_g", "ab_w_mod", "ab_b_mod", "ab_w_in", "ab_conv_w", "ab_w_out", "sg_norm_g", "sg_w_mod",
             "sg_b_mod", "sg_w_in", "sg_ln_g", "sg_ln_b", "sg_w_s", "sg_b_s", "sg_w_out", "final_norm_g"]
    res = {**big_res, **small_res, **mod_res}
    outs = [loss, grad_x]
    for k in range(4):
        outs += [res[nm][k] for nm in order]
    return tuple(outs)
```

```python
import functools
import math

import numpy as np
import jax
import jax.numpy as jnp
from jax import lax
from jax.experimental import pallas as pl
from jax.experimental.pallas import tpu as pltpu

F32 = jnp.float32
BF16 = jnp.bfloat16

NDEV = 8
NCHIP = 4
EPS = 1e-6
HEAD_DIM = 128
ROPE_THETA = 10000.0
DILATED_PATTERNS = ((128, 1), (512, 4), (2048, 16))
NEG_INF = -1e30
C_CHUNK = 128
C_GROUPS = 8
ADAM_LR = 0.001
ADAM_B1 = 0.9
ADAM_B2 = 0.999
ADAM_EPS = 1e-08
ADAM_WD = 0.01
ADAM_STEP = 10
GELU_K = math.sqrt(2.0 / math.pi)
GELU_C = 0.044715

VMEM_LIMIT_BYTES = 56 * 1024 * 1024
ATTN_TILE = 512
HEADS_PER_STEP = 4
ATTN_ROW_CHUNK = 256
LANES = 128
ROW_TILE = 256
MESH = pl.DeviceIdType.MESH
ANY = pl.BlockSpec(memory_space=pl.ANY)


def _cp(*sem):
    return pltpu.CompilerParams(dimension_semantics=sem, vmem_limit_bytes=VMEM_LIMIT_BYTES)


def _sigmoid(z):
    return 0.5 * (jnp.tanh(0.5 * z) + 1.0)


def _silu_and_grad(z):
    s = _sigmoid(z)
    return z * s, s * (1.0 + z * (1.0 - s))


def _gelu_and_grad(x):
    x2 = x * x
    t = jnp.tanh(GELU_K * (x + GELU_C * x2 * x))
    g = 0.5 * x * (1.0 + t)
    dg = 0.5 * (1.0 + t) + 0.5 * x * (1.0 - t * t) * (GELU_K * (1.0 + 3.0 * GELU_C * x2))
    return g, dg


def _position():
    return lax.axis_index("x"), lax.axis_index("y"), lax.axis_index("c")


def _chips(x, y):
    return [(1 - x, y), (x, 1 - y), (1 - x, 1 - y)]


class _Gather:
    def __init__(self, arrs):
        n = len(arrs)
        self.arrs = list(arrs)
        self.out_shape = [jax.ShapeDtypeStruct((NDEV,) + a.shape, a.dtype) for a in arrs]
        self.scratch = [pltpu.SemaphoreType.DMA((n, 7)), pltpu.SemaphoreType.DMA((n, 7)),
                        pltpu.SemaphoreType.DMA((n,))]

    def _copies(self, ins, outs, sems):
        send_sems, recv_sems, local_sems = sems
        x, y, c = _position()

        def copy(a, k, block, to, src=None):
            dst = outs[a].at[4 * block[0] + 2 * block[1] + block[2]]
            return pltpu.make_async_remote_copy(
                src_ref=dst if src is None else src, dst_ref=dst,
                send_sem=send_sems.at[a, k], recv_sem=recv_sems.at[a, k],
                device_id=to, device_id_type=MESH)

        n = len(ins)
        me, sibling = (x, y, c), (x, y, 1 - c)
        mine = [pltpu.make_async_copy(ins[a], outs[a].at[4 * x + 2 * y + c], local_sems.at[a]) for a in range(n)]
        first = []
        for a in range(n):
            first.append(copy(a, 0, me, sibling, src=ins[a]))
            first += [copy(a, 1 + j, me, (*chip, c), src=ins[a]) for j, chip in enumerate(_chips(x, y))]
        return copy, mine, first

    def start(self, ins, outs, sems):
        _, mine, first = self._copies(ins, outs, sems)
        for cp in mine + first:
            cp.start()

    def finish(self, ins, outs, sems):
        copy, mine, first = self._copies(ins, outs, sems)
        x, y, c = _position()
        me, sibling = (x, y, c), (x, y, 1 - c)
        passed = []
        for j, chip in enumerate(_chips(x, y)):
            for a in range(len(ins)):
                copy(a, 1 + j, (*chip, c), me).wait_recv()
                fwd = copy(a, 4 + j, (*chip, c), sibling)
                fwd.start()
                passed.append(fwd)
        for a in range(len(ins)):
            copy(a, 0, sibling, me).wait_recv()
            for j, chip in enumerate(_chips(x, y)):
                copy(a, 4 + j, (*chip, 1 - c), me).wait_recv()
        for cp in first + passed:
            cp.wait_send()
        for cp in mine:
            cp.wait()


class _ToSibling:
    def __init__(self, gs):
        n = len(gs)
        self.arrs = list(gs)
        self.out_shape = [jax.ShapeDtypeStruct((NCHIP,) + g.shape[1:], g.dtype) for g in gs]
        self.scratch = [pltpu.SemaphoreType.DMA((n, NCHIP)), pltpu.SemaphoreType.DMA((n, NCHIP))]

    def _copies(self, ins, outs, sems):
        send_sems, recv_sems = sems
        x, y, c = _position()
        return [pltpu.make_async_remote_copy(
            src_ref=ins[a].at[2 * k + (1 - c)], dst_ref=outs[a].at[k],
            send_sem=send_sems.at[a, k], recv_sem=recv_sems.at[a, k],
            device_id=(x, y, 1 - c), device_id_type=MESH) for a in range(len(ins)) for k in range(NCHIP)]

    def start(self, ins, outs, sems):
        for cp in self._copies(ins, outs, sems):
            cp.start()

    def finish(self, ins, outs, sems):
        copies = self._copies(ins, outs, sems)
        for cp in copies:
            cp.wait_recv()
        for cp in copies:
            cp.wait_send()


class _ToChips:
    def __init__(self, ps):
        n = len(ps)
        self.arrs = list(ps)
        self.out_shape = [jax.ShapeDtypeStruct(p.shape, p.dtype) for p in ps]
        self.scratch = [pltpu.SemaphoreType.DMA((n, 3)), pltpu.SemaphoreType.DMA((n, 3)),
                        pltpu.SemaphoreType.DMA((n,))]

    def _copies(self, ins, outs, sems, arrivals):
        send_sems, recv_sems, local_sems = sems
        x, y, c = _position()
        mychip = 2 * x + y
        n = len(ins)
        mine = [pltpu.make_async_copy(ins[a].at[mychip], outs[a].at[mychip], local_sems.at[a]) for a in range(n)]
        sends, recvs = [], []
        for a in range(n):
            for j, chip in enumerate(_chips(x, y)):
                sends.append(pltpu.make_async_remote_copy(
                    src_ref=ins[a].at[2 * chip[0] + chip[1]], dst_ref=outs[a].at[mychip],
                    send_sem=send_sems.at[a, j], recv_sem=recv_sems.at[a, j],
                    device_id=(*chip, c), device_id_type=MESH))
                if arrivals:
                    slot = outs[a].at[2 * chip[0] + chip[1]]
                    recvs.append(pltpu.make_async_remote_copy(
                        src_ref=slot, dst_ref=slot, send_sem=send_sems.at[a, j], recv_sem=recv_sems.at[a, j],
                        device_id=(*chip, c), device_id_type=MESH))
        return mine, sends, recvs

    def start(self, ins, outs, sems):
        mine, sends, _ = self._copies(ins, outs, sems, False)
        for cp in mine + sends:
            cp.start()

    def finish(self, ins, outs, sems):
        mine, sends, recvs = self._copies(ins, outs, sems, True)
        for cp in recvs:
            cp.wait_recv()
        for cp in sends:
            cp.wait_send()
        for cp in mine:
            cp.wait()


class _Both:
    def __init__(self, first, second):
        self.parts = (first, second)
        self.arrs = first.arrs + second.arrs
        self.out_shape = first.out_shape + second.out_shape
        self.scratch = first.scratch + second.scratch

    def _split(self, ins, outs, sems):
        a, _ = self.parts
        ni, no, ns = len(a.arrs), len(a.out_shape), len(a.scratch)
        return (ins[:ni], outs[:no], sems[:ns]), (ins[ni:], outs[no:], sems[ns:])

    def start(self, ins, outs, sems):
        for part, refs in zip(self.parts, self._split(ins, outs, sems)):
            part.start(*refs)

    def finish(self, ins, outs, sems):
        for part, refs in zip(self.parts, self._split(ins, outs, sems)):
            part.finish(*refs)

    def split_results(self, res):
        no = len(self.parts[0].out_shape)
        return res[:no], res[no:]


def _comm_only(comm, name):
    n_in, n_out = len(comm.arrs), len(comm.out_shape)

    def body(*refs):
        ins, outs, sems = refs[:n_in], refs[n_in:n_in + n_out], refs[n_in + n_out:]
        comm.start(ins, outs, sems)
        comm.finish(ins, outs, sems)

    return pl.pallas_call(
        body, name=name, out_shape=comm.out_shape, in_specs=[ANY] * n_in, out_specs=[ANY] * n_out,
        scratch_shapes=comm.scratch,
    )(*comm.arrs)


def _hosted_call(body, operands, *, name, grid, in_specs, out_specs, out_shape, scratch_shapes=(), sem=(),
                 aliases=None, comm=None):
    single = not isinstance(out_shape, (list, tuple))
    o_specs = [out_specs] if single else list(out_specs)
    o_shape = [out_shape] if single else list(out_shape)
    n_in, n_out, n_scr = len(in_specs), len(o_shape), len(scratch_shapes)
    if comm is None:
        res = pl.pallas_call(body, name=name, grid=grid, in_specs=list(in_specs), out_specs=o_specs,
                             out_shape=o_shape, scratch_shapes=list(scratch_shapes),
                             input_output_aliases=aliases or {}, compiler_params=_cp(*sem))(*operands)
        return (res[0] if single else res), []
    c_in, c_out = len(comm.arrs), len(comm.out_shape)

    def wrapped(*refs):
        ins, cins = refs[:n_in], refs[n_in:n_in + c_in]
        o0 = n_in + c_in
        outs, couts = refs[o0:o0 + n_out], refs[o0 + n_out:o0 + n_out + c_out]
        s0 = o0 + n_out + c_out
        scr, csems = refs[s0:s0 + n_scr], refs[s0 + n_scr:]
        pids = [pl.program_id(a) for a in range(len(grid))]
        first = functools.reduce(jnp.logical_and, [p == 0 for p in pids])
        last = functools.reduce(jnp.logical_and, [p == g - 1 for p, g in zip(pids, grid)])

        @pl.when(first)
        def _():
            comm.start(cins, couts, csems)

        body(*ins, *outs, *scr)

        @pl.when(last)
        def _():
            comm.finish(cins, couts, csems)

    res = pl.pallas_call(
        wrapped, name=name, grid=grid, in_specs=list(in_specs) + [ANY] * c_in, out_specs=o_specs + [ANY] * c_out,
        out_shape=o_shape + comm.out_shape, scratch_shapes=list(scratch_shapes) + comm.scratch,
        input_output_aliases=aliases or {}, compiler_params=_cp(*(["arbitrary"] * len(grid))),
    )(*operands, *comm.arrs)
    return (res[0] if single else res[:n_out]), res[n_out:]


def _adamw(w, g, m, v):
    m2 = ADAM_B1 * m + (1.0 - ADAM_B1) * g
    v2 = ADAM_B2 * v + (1.0 - ADAM_B2) * (g * g)
    m_hat = m2 / (1.0 - ADAM_B1 ** ADAM_STEP)
    v_hat = v2 / (1.0 - ADAM_B2 ** ADAM_STEP)
    delta = -ADAM_LR * (m_hat / (jnp.sqrt(v_hat) + ADAM_EPS) + ADAM_WD * w)
    return delta, m2, v2


def _add_sibling(g, recv, c_idx, name):
    _, R, C = g.shape
    tr = min(R, 512)

    def body(c_ref, g_ref, r_ref, o_ref):
        o_ref[...] = (g_ref[...] + r_ref[...]).astype(BF16)

    return pl.pallas_call(
        body, name=name,
        grid_spec=pltpu.PrefetchScalarGridSpec(
            num_scalar_prefetch=1, grid=(NCHIP, R // tr),
            in_specs=[pl.BlockSpec((1, tr, C), lambda k, i, c_ref: (2 * k + c_ref[0], i, 0)),
                      pl.BlockSpec((1, tr, C), lambda k, i, c_ref: (k, i, 0))],
            out_specs=pl.BlockSpec((1, tr, C), lambda k, i, c_ref: (k, i, 0))),
        out_shape=jax.ShapeDtypeStruct((NCHIP, R, C), BF16),
        compiler_params=_cp("parallel", "parallel"),
    )(c_idx, g, recv)


def _sum_adam(parts, w, m, v, row0, prev, name):
    K, R, C = parts.shape
    LR = w.shape[0]
    tr = min(R, 256)
    nb = R // tr
    first = row0 // tr

    def body(p_ref, w_ref, m_ref, v_ref, *rest):
        g_ref, d_ref, m2_ref, v2_ref = rest[-4:]
        g = p_ref[0].astype(F32)
        for k in range(1, K):
            g = g + p_ref[k].astype(F32)
        delta, m2, v2 = _adamw(w_ref[...], g, m_ref[...], v_ref[...])
        g_ref[...] = g
        d_ref[...] = delta
        m2_ref[...] = m2
        v2_ref[...] = v2

    blk = pl.BlockSpec((tr, C), lambda i: (first + i, 0))
    shp = jax.ShapeDtypeStruct((LR, C), F32)
    operands = [parts, w, m, v] + (list(prev) if prev is not None else [])
    return pl.pallas_call(
        body, name=name, grid=(nb,),
        in_specs=[pl.BlockSpec((K, tr, C), lambda i: (0, i, 0)), blk, blk, blk] + [ANY] * (len(operands) - 4),
        out_specs=[blk] * 4, out_shape=[shp] * 4,
        input_output_aliases={4 + k: k for k in range(len(operands) - 4)},
        compiler_params=_cp("parallel"),
    )(*operands)


def _sum_rows(parts, name):
    K, R, C = parts.shape
    tr = min(R, 256)
    while R % tr:
        tr //= 2

    def body(p_ref, o_ref):
        g = p_ref[0]
        for k in range(1, K):
            g = g + p_ref[k]
        o_ref[...] = g

    return pl.pallas_call(
        body, name=name, grid=(R // tr,),
        in_specs=[pl.BlockSpec((K, tr, C), lambda i: (0, i, 0))],
        out_specs=pl.BlockSpec((tr, C), lambda i: (i, 0)),
        out_shape=jax.ShapeDtypeStruct((R, C), F32),
        compiler_params=_cp("parallel"),
    )(parts)


def _adam_only(g, w, m, v, name):
    R, C = g.shape
    tr = min(R, 256)
    while R % tr:
        tr //= 2

    def body(g_ref, w_ref, m_ref, v_ref, d_ref, m2_ref, v2_ref):
        delta, m2, v2 = _adamw(w_ref[...], g_ref[...], m_ref[...], v_ref[...])
        d_ref[...] = delta
        m2_ref[...] = m2
        v2_ref[...] = v2

    blk = pl.BlockSpec((tr, C), lambda i: (i, 0))
    shp = jax.ShapeDtypeStruct((R, C), F32)
    return pl.pallas_call(
        body, name=name, grid=(R // tr,), in_specs=[blk] * 4, out_specs=[blk] * 3,
        out_shape=[shp] * 3, compiler_params=_cp("parallel"),
    )(g, w, m, v)


def _mod_fwd(c_all, w_mod, b_cols, name):
    L, D, n = w_mod.shape
    B = c_all.shape[0]

    def body(c_ref, w_ref, b_ref, o_ref):
        cv = c_ref[...]
        sc = (cv * _sigmoid(cv)).astype(BF16)
        o_ref[0] = jnp.dot(sc, w_ref[0].astype(BF16), preferred_element_type=F32) + b_ref[0]

    return pl.pallas_call(
        body, name=name, grid=(L,),
        in_specs=[pl.BlockSpec((B, D), lambda l: (0, 0)),
                  pl.BlockSpec((1, D, n), lambda l: (l, 0, 0)),
                  pl.BlockSpec((1, 1, n), lambda l: (l, 0, 0))],
        out_specs=pl.BlockSpec((1, B, n), lambda l: (l, 0, 0)),
        out_shape=jax.ShapeDtypeStruct((L, B, n), F32),
        compiler_params=_cp("parallel"),
    )(c_all, w_mod, b_cols)


def _wmod_grad_adam(sc_t, dm, w, m, v, name):
    L, D, n = w.shape
    KP = sc_t.shape[1]
    tr = min(D, 512)

    def body(s_ref, dm_ref, w_ref, m_ref, v_ref, g_ref, d_ref, m2_ref, v2_ref):
        g = jnp.dot(s_ref[...], dm_ref[0], preferred_element_type=F32,
                    precision=lax.Precision.HIGHEST)
        delta, m2, v2 = _adamw(w_ref[0], g, m_ref[0], v_ref[0])
        g_ref[0] = g
        d_ref[0] = delta
        m2_ref[0] = m2
        v2_ref[0] = v2

    blk = pl.BlockSpec((1, tr, n), lambda l, i: (l, i, 0))
    shp = jax.ShapeDtypeStruct((L, D, n), F32)
    return pl.pallas_call(
        body, name=name, grid=(L, D // tr),
        in_specs=[pl.BlockSpec((tr, KP), lambda l, i: (i, 0)),
                  pl.BlockSpec((1, KP, n), lambda l, i: (l, 0, 0)), blk, blk, blk],
        out_specs=[blk] * 4, out_shape=[shp] * 4,
        compiler_params=_cp("parallel", "parallel"),
    )(sc_t, dm, w, m, v)


def _vec_spec(D):
    return pl.BlockSpec((1, D), lambda i: (0, 0))


def _pre(x, res, gate, g, scale, shift, name):
    S, D = x.shape
    tr = min(S, ROW_TILE)
    has_res = res is not None
    row = pl.BlockSpec((tr, D), lambda i: (i, 0))

    def body(*refs):
        if has_res:
            x_ref, r_ref, gate_ref, g_ref, sc_ref, sh_ref, xl_ref, h_ref = refs
            xv = x_ref[...] + gate_ref[...] * r_ref[...]
            xl_ref[...] = xv
        else:
            x_ref, g_ref, sc_ref, sh_ref, h_ref = refs
            xv = x_ref[...]
        r = lax.rsqrt(jnp.mean(xv * xv, axis=-1, keepdims=True) + EPS)
        y = (xv * r) * g_ref[...]
        h_ref[...] = (y * (1.0 + sc_ref[...]) + sh_ref[...]).astype(BF16)

    vec = _vec_spec(D)
    if has_res:
        xl, h = pl.pallas_call(
            body, name=name, grid=(S // tr,),
            in_specs=[row, row, vec, vec, vec, vec], out_specs=[row, row],
            out_shape=[jax.ShapeDtypeStruct((S, D), F32), jax.ShapeDtypeStruct((S, D), BF16)],
            compiler_params=_cp("parallel"),
        )(x, res, gate, g, scale, shift)
        return xl, h
    h = pl.pallas_call(
        body, name=name, grid=(S // tr,),
        in_specs=[row, vec, vec, vec], out_specs=row,
        out_shape=jax.ShapeDtypeStruct((S, D), BF16),
        compiler_params=_cp("parallel"),
    )(x, g, scale, shift)
    return x, h


def _pre_bwd(xl, dh, dx_in, g, scale, name, comm=None):
    S, D = xl.shape
    tr = min(S, ROW_TILE)
    nsteps = S // tr
    row = pl.BlockSpec((tr, D), lambda i: (i, 0))
    vec = _vec_spec(D)

    def body(x_ref, dh_ref, dxin_ref, g_ref, sc_ref, dx_ref, dsh_ref, dsc_ref, dg_ref, acc_sh, acc_t):
        i = pl.program_id(0)
        xv = x_ref[...]
        dh = dh_ref[...]
        r = lax.rsqrt(jnp.mean(xv * xv, axis=-1, keepdims=True) + EPS)
        xn = xv * r
        part_sh = jnp.sum(dh.reshape(tr // 8, 8, D), axis=0)
        part_t = jnp.sum((dh * xn).reshape(tr // 8, 8, D), axis=0)

        @pl.when(i == 0)
        def _():
            acc_sh[...] = part_sh
            acc_t[...] = part_t

        @pl.when(i > 0)
        def _():
            acc_sh[...] += part_sh
            acc_t[...] += part_t

        dxn = dh * (g_ref[...] * (1.0 + sc_ref[...]))
        dx_ref[...] = dxin_ref[...] + r * (dxn - xn * jnp.mean(dxn * xn, axis=-1, keepdims=True))

        @pl.when(i == nsteps - 1)
        def _():
            t = jnp.sum(acc_t[...], axis=0, keepdims=True)
            dsh_ref[...] = jnp.sum(acc_sh[...], axis=0, keepdims=True)
            dsc_ref[...] = t * g_ref[...]
            dg_ref[...] = t * (1.0 + sc_ref[...])

    v = jax.ShapeDtypeStruct((1, D), F32)
    return _hosted_call(
        body, [xl, dh, dx_in, g, scale], name=name, grid=(nsteps,),
        in_specs=[row, row, row, vec, vec], out_specs=[row, vec, vec, vec],
        out_shape=[jax.ShapeDtypeStruct((S, D), F32), v, v, v],
        scratch_shapes=[pltpu.VMEM((8, D), F32), pltpu.VMEM((8, D), F32)],
        sem=("arbitrary",), comm=comm)


def _post_bwd(dx, out, gate, name):
    S, D = dx.shape
    tr = min(S, ROW_TILE)
    nsteps = S // tr
    row = pl.BlockSpec((tr, D), lambda i: (i, 0))
    vec = _vec_spec(D)

    def body(dx_ref, o_ref, gate_ref, do_ref, dg_ref, acc):
        i = pl.program_id(0)
        dxv = dx_ref[...]
        do_ref[...] = (dxv * gate_ref[...]).astype(BF16)
        part = jnp.sum((dxv * o_ref[...]).reshape(tr // 8, 8, D), axis=0)

        @pl.when(i == 0)
        def _():
            acc[...] = part

        @pl.when(i > 0)
        def _():
            acc[...] += part

        @pl.when(i == nsteps - 1)
        def _():
            dg_ref[...] = jnp.sum(acc[...], axis=0, keepdims=True)

    return pl.pallas_call(
        body, name=name, grid=(nsteps,),
        in_specs=[row, row, vec], out_specs=[row, vec],
        out_shape=[jax.ShapeDtypeStruct((S, D), BF16), jax.ShapeDtypeStruct((1, D), F32)],
        scratch_shapes=[pltpu.VMEM((8, D), F32)],
        compiler_params=_cp("arbitrary"),
    )(dx, out, gate)


def _loss_head(x, res, gate, gf, tgt, name):
    S, D = x.shape
    tr = min(S, ROW_TILE)
    nsteps = S // tr
    row = pl.BlockSpec((tr, D), lambda i: (i, 0))
    vec = _vec_spec(D)

    def body(x_ref, r_ref, gate_ref, gf_ref, t_ref, dx_ref, loss_ref, dgf_ref, acc, lacc):
        i = pl.program_id(0)
        xv = x_ref[...] + gate_ref[...] * r_ref[...]
        r = lax.rsqrt(jnp.mean(xv * xv, axis=-1, keepdims=True) + EPS)
        xn = xv * r
        err = xn * gf_ref[...] - t_ref[...]
        row_loss = jnp.mean(err * err, axis=-1, keepdims=True)
        lpart = 0.5 * jnp.sum(row_loss, axis=0, keepdims=True)
        dy = err * (1.0 / D)
        part = jnp.sum((dy * xn).reshape(tr // 8, 8, D), axis=0)

        @pl.when(i == 0)
        def _():
            acc[...] = part
            lacc[...] = lpart

        @pl.when(i > 0)
        def _():
            acc[...] += part
            lacc[...] += lpart

        dxn = dy * gf_ref[...]
        dx_ref[...] = r * (dxn - xn * jnp.mean(dxn * xn, axis=-1, keepdims=True))

        @pl.when(i == nsteps - 1)
        def _():
            dgf_ref[...] = jnp.sum(acc[...], axis=0, keepdims=True)
            loss_ref[...] = lacc[...]

    return pl.pallas_call(
        body, name=name, grid=(nsteps,),
        in_specs=[row, row, vec, vec, row],
        out_specs=[row, pl.BlockSpec((1, 1), lambda i: (0, 0)), vec],
        out_shape=[jax.ShapeDtypeStruct((S, D), F32), jax.ShapeDtypeStruct((1, 1), F32),
                   jax.ShapeDtypeStruct((1, D), F32)],
        scratch_shapes=[pltpu.VMEM((8, D), F32), pltpu.VMEM((1, 1), F32)],
        compiler_params=_cp("arbitrary"),
    )(x, res, gate, gf, tgt)


NN = (((1,), (0,)), ((), ()))
NT = (((1,), (1,)), ((), ()))
TN = (((0,), (0,)), ((), ()))


def _mm(name, a, b, out_shape, grid, a_spec, b_spec, o_spec, dims, a2d, b2d, k_axis, sem, alias=None, comm=None):
    def body(*refs):
        a_ref, b_ref, o_ref = refs[0], refs[1], refs[-1]
        r = lax.dot_general(a_ref[...].reshape(a2d), b_ref[...].reshape(b2d), dims,
                            preferred_element_type=F32)
        r = r.reshape(o_ref.shape)
        if k_axis is None:
            o_ref[...] = r.astype(o_ref.dtype)
        else:
            k = pl.program_id(k_axis)

            @pl.when(k == 0)
            def _():
                o_ref[...] = r

            @pl.when(k > 0)
            def _():
                o_ref[...] += r

    operands, in_specs, aliases = [a, b], [a_spec, b_spec], {}
    if alias is not None:
        operands.append(alias)
        in_specs.append(ANY)
        aliases = {2: 0}
    res, extra = _hosted_call(body, operands, name=name, grid=grid, in_specs=in_specs, out_specs=o_spec,
                              out_shape=out_shape, sem=sem, aliases=aliases, comm=comm)
    return res if comm is None else (res, extra)


def _tile(n, pref):
    t = min(n, pref)
    while n % t:
        t -= 128
    return t


def _mm_nn_in(a, w, l, name, comm=None):
    M, K = a.shape
    _, _, _, n = w.shape
    tm, tn = min(M, 512), _tile(n, 1024)
    nb = n // tn
    return _mm(name, a, w, jax.ShapeDtypeStruct((M, NDEV * n), F32), (NDEV * nb, M // tm),
               pl.BlockSpec((tm, K), lambda j, i: (i, 0)),
               pl.BlockSpec((1, 1, K, tn), lambda j, i: (j // nb, l, 0, j % nb)),
               pl.BlockSpec((tm, tn), lambda j, i: (i, j)),
               NN, (tm, K), (K, tn), None, ("parallel", "parallel"), comm=comm)


def _mm_nn_out(a, w, l, name):
    M, K = a.shape
    _, _, kb, N = w.shape
    tm, tn = min(M, 512), _tile(N, 1024)
    return _mm(name, a, w, jax.ShapeDtypeStruct((M, N), F32), (N // tn, M // tm),
               pl.BlockSpec((tm, K), lambda j, i: (i, 0)),
               pl.BlockSpec((NDEV, 1, kb, tn), lambda j, i: (0, l, 0, j)),
               pl.BlockSpec((tm, tn), lambda j, i: (i, j)),
               NN, (tm, K), (K, tn), None, ("parallel", "parallel"))


def _mm_nt_in(a, w, l, name, comm=None):
    M, _ = a.shape
    _, _, K, n = w.shape
    tm, tk = min(M, 1024), _tile(K, 1024)
    gb = 2 if n <= 1024 else 1

    def body(a_ref, w_ref, o_ref):
        k = pl.program_id(2)
        r = lax.dot_general(a_ref[:, :n], w_ref[0, 0], NT, preferred_element_type=F32)
        for g in range(1, gb):
            r = r + lax.dot_general(a_ref[:, g * n:(g + 1) * n], w_ref[g, 0], NT, preferred_element_type=F32)

        @pl.when(k == 0)
        def _():
            o_ref[...] = r

        @pl.when(k > 0)
        def _():
            o_ref[...] += r

    res, extra = _hosted_call(
        body, [a, w], name=name, grid=(M // tm, K // tk, NDEV // gb),
        in_specs=[pl.BlockSpec((tm, gb * n), lambda i, j, k: (i, k)),
                  pl.BlockSpec((gb, 1, tk, n), lambda i, j, k: (k, l, j, 0))],
        out_specs=pl.BlockSpec((tm, tk), lambda i, j, k: (i, j)),
        out_shape=jax.ShapeDtypeStruct((M, K), F32),
        sem=("parallel", "parallel", "arbitrary"), comm=comm)
    return res if comm is None else (res, extra)


def _mm_nt_out(a, w, l, name):
    M, N = a.shape
    _, _, kb, _ = w.shape
    K = NDEV * kb
    tm, tk, tc = min(M, 1024), _tile(K, 1024), _tile(N, 1024)
    per = tk // kb
    return _mm(name, a, w, jax.ShapeDtypeStruct((M, K), F32), (M // tm, K // tk, N // tc),
               pl.BlockSpec((tm, tc), lambda i, j, k: (i, k)),
               pl.BlockSpec((per, 1, kb, tc), lambda i, j, k: (j, l, 0, k)),
               pl.BlockSpec((tm, tk), lambda i, j, k: (i, j)),
               NT, (tm, tc), (tk, tc), 2, ("parallel", "parallel", "arbitrary"))


def _mm_tn_in(a, b, l, L, buf, name, comm=None, part=(0, 1)):
    S, K = a.shape
    K = K // part[1]
    n = b.shape[1] // NDEV
    ts, tk, tn = min(S, 2048), _tile(K, 1024), _tile(n, 1024)
    nb = n // tn
    first = part[0] * (K // tk)
    return _mm(name, a, b, jax.ShapeDtypeStruct((NDEV, L, K, n), F32), (NDEV * nb, K // tk, S // ts),
               pl.BlockSpec((ts, tk), lambda j, i, s: (s, first + i)),
               pl.BlockSpec((ts, tn), lambda j, i, s: (s, j)),
               pl.BlockSpec((1, 1, tk, tn), lambda j, i, s: (j // nb, l, i, j % nb)),
               TN, (ts, tk), (ts, tn), 2, ("parallel", "parallel", "arbitrary"), alias=buf, comm=comm)


def _mm_tn_out(a, b, l, L, buf, name):
    S, K = a.shape
    N = b.shape[1]
    kb = K // NDEV
    ts, tk, tn = min(S, 2048), _tile(K, 1024), _tile(N, 1024)
    per = tk // kb
    return _mm(name, a, b, jax.ShapeDtypeStruct((NDEV, L, kb, N), F32), (N // tn, K // tk, S // ts),
               pl.BlockSpec((ts, tk), lambda j, i, s: (s, i)),
               pl.BlockSpec((ts, tn), lambda j, i, s: (s, j)),
               pl.BlockSpec((per, 1, kb, tn), lambda j, i, s: (i, l, 0, j)),
               TN, (ts, tk), (ts, tn), 2, ("parallel", "parallel", "arbitrary"), alias=buf)


def _attn_bias(T):
    reach = max(w // 2 for w, _ in DILATED_PATTERNS)
    hb = -(-reach // T)
    i = np.arange(T)[:, None]
    j = np.arange(T)[None, :]
    tiles = []
    for d in range(-hb, hb + 1):
        rel = j + d * T - i
        mult = np.zeros((T, T), np.float64)
        for window, dil in DILATED_PATTERNS:
            radius = window // (2 * dil)
            mult += (rel % dil == 0) & (np.abs(rel) <= radius * dil)
        tiles.append(np.where(mult > 0, np.log(np.maximum(mult, 1.0)), NEG_INF))
    return jnp.asarray(np.stack(tiles), F32)


def _rope_tables(S):
    half = HEAD_DIM // 2
    pos = jnp.arange(S, dtype=F32)
    inv = ROPE_THETA ** (-jnp.arange(half, dtype=F32) / half)
    ang = pos[:, None] * inv[None, :]
    cos, sin = jnp.cos(ang), jnp.sin(ang)
    return jnp.concatenate([cos, cos], axis=-1), jnp.concatenate([-sin, sin], axis=-1)


def _rope_apply(t, cosf, sinf, heads, sign):
    outs = []
    for hh in range(heads):
        th = t[:, hh * HEAD_DIM:(hh + 1) * HEAD_DIM]
        outs.append(th * cosf + sign * (pltpu.roll(th, HEAD_DIM // 2, 1) * sinf))
    return outs


def _rope_qkv(proj, cosf, sinf, W, name):
    S = proj.shape[0]
    tr = min(S, ROW_TILE)
    heads = W // HEAD_DIM

    def body(q_ref, k_ref, v_ref, c_ref, s_ref, qo_ref, ko_ref, vo_ref):
        cosf_v, sinf_v = c_ref[...], s_ref[...]
        for src, dst, mult in ((q_ref, qo_ref, HEAD_DIM ** -0.5), (k_ref, ko_ref, 1.0)):
            for hh, val in enumerate(_rope_apply(src[...], cosf_v, sinf_v, heads, 1.0)):
                dst[:, hh * HEAD_DIM:(hh + 1) * HEAD_DIM] = (val * mult).astype(BF16)
        vo_ref[...] = v_ref[...].astype(BF16)

    piece = lambda p: pl.BlockSpec((tr, W), lambda i: (i, p))
    tab = pl.BlockSpec((tr, HEAD_DIM), lambda i: (i, 0))
    out = pl.BlockSpec((tr, W), lambda i: (i, 0))
    shp = jax.ShapeDtypeStruct((S, W), BF16)
    return pl.pallas_call(
        body, name=name, grid=(S // tr,),
        in_specs=[piece(0), piece(1), piece(2), tab, tab], out_specs=[out] * 3, out_shape=[shp] * 3,
        compiler_params=_cp("parallel"),
    )(proj, proj, proj, cosf, sinf)


def _attn_fwd(q, k, v, bias, name, comm=None):
    S, W = q.shape
    H = W // HEAD_DIM
    nd, T, _ = bias.shape
    hb, nq = nd // 2, S // T
    scale = HEAD_DIM ** -0.5
    hp = min(H, HEADS_PER_STEP)
    rc = min(T, ATTN_ROW_CHUNK)
    wp = hp * HEAD_DIM

    def body(q_ref, k_ref, v_ref, b_ref, o_ref, lse_ref, m_s, l_s, acc_s):
        i, d = pl.program_id(1), pl.program_id(2)
        j = i + d - hb

        @pl.when(d == 0)
        def _():
            m_s[...] = jnp.full(m_s.shape, -jnp.inf, F32)
            l_s[...] = jnp.zeros(l_s.shape, F32)
            acc_s[...] = jnp.zeros(acc_s.shape, F32)

        @pl.when((j >= 0) & (j < nq))
        def _():
            items = [(hh, c) for hh in range(hp) for c in range(T // rc)]

            def scores(item):
                hh, c = item
                cols, rows = slice(hh * HEAD_DIM, (hh + 1) * HEAD_DIM), slice(c * rc, (c + 1) * rc)
                return (lax.dot_general(q_ref[rows, cols], k_ref[:, cols], NT, preferred_element_type=F32)
                        + b_ref[d, rows, :])

            def weighted_values(item, p, alpha):
                hh, c = item
                cols, rows = slice(hh * HEAD_DIM, (hh + 1) * HEAD_DIM), slice(c * rc, (c + 1) * rc)
                acc_s[rows, cols] = alpha * acc_s[rows, cols] + jnp.dot(p, v_ref[:, cols],
                                                                        preferred_element_type=F32)

            s_next, pending = scores(items[0]), None
            for n, (hh, c) in enumerate(items):
                rows = slice(c * rc, (c + 1) * rc)
                s = s_next
                if n + 1 < len(items):
                    s_next = scores(items[n + 1])
                if pending is not None:
                    weighted_values(*pending)
                parts = [s[:, t * LANES:(t + 1) * LANES] for t in range(T // LANES)]
                m_old = m_s[hh, rows, :]
                m_cur = jnp.max(functools.reduce(jnp.maximum, parts), axis=1, keepdims=True)
                m_new = jnp.maximum(m_old, m_cur)
                alpha = jnp.exp(m_old - m_new)
                ps = [jnp.exp(part - m_new) for part in parts]
                l_s[hh, rows, :] = alpha * l_s[hh, rows, :] + functools.reduce(jnp.add, ps)
                m_s[hh, rows, :] = m_new
                pending = ((hh, c), jnp.concatenate(ps, axis=1).astype(BF16), alpha)
            weighted_values(*pending)

        @pl.when(d == nd - 1)
        def _():
            for hh in range(hp):
                cols = slice(hh * HEAD_DIM, (hh + 1) * HEAD_DIM)
                l = jnp.sum(l_s[hh], axis=1, keepdims=True)
                o_ref[:, cols] = acc_s[:, cols] / l
                lse_ref[hh] = m_s[hh][:, :1] + jnp.log(l)

    kv = pl.BlockSpec((T, wp), lambda h, i, d: (jnp.clip(i + d - hb, 0, nq - 1), h))
    return _hosted_call(
        body, [q, k, v, bias], name=name, grid=(H // hp, nq, nd),
        in_specs=[pl.BlockSpec((T, wp), lambda h, i, d: (i, h)), kv, kv,
                  pl.BlockSpec((nd, T, T), lambda h, i, d: (0, 0, 0))],
        out_specs=[pl.BlockSpec((T, wp), lambda h, i, d: (i, h)),
                   pl.BlockSpec((hp, T, 1), lambda h, i, d: (h, i, 0))],
        out_shape=[jax.ShapeDtypeStruct((S, W), F32), jax.ShapeDtypeStruct((H, S, 1), F32)],
        scratch_shapes=[pltpu.VMEM((hp, T, LANES), F32), pltpu.VMEM((hp, T, LANES), F32),
                        pltpu.VMEM((T, wp), F32)],
        sem=("parallel", "parallel", "arbitrary"), comm=comm)


def _attn_bwd(q, k, v, do, lse, delta, bias, name, comm=None):
    S, W = q.shape
    H = W // HEAD_DIM
    nd, T, _ = bias.shape
    hb, nq = nd // 2, S // T
    scale = HEAD_DIM ** -0.5
    hp = min(H, HEADS_PER_STEP)
    rc = min(T, ATTN_ROW_CHUNK)
    wp = hp * HEAD_DIM

    def body(q_ref, do_ref, lse_ref, dl_ref, k_ref, v_ref, b_ref, dq_ref, dk_ref, dv_ref):
        j, d = pl.program_id(1), pl.program_id(2)
        i = j + d - hb

        @pl.when((j == 0) & (d == 0))
        def _():
            dq_ref[...] = jnp.zeros(dq_ref.shape, F32)

        @pl.when(d == 0)
        def _():
            dk_ref[...] = jnp.zeros(dk_ref.shape, F32)
            dv_ref[...] = jnp.zeros(dv_ref.shape, F32)

        @pl.when((i >= 0) & (i < nq))
        def _():
            items = [(hh, c) for hh in range(hp) for c in range(T // rc)]

            def slices(item):
                hh, c = item
                return slice(hh * HEAD_DIM, (hh + 1) * HEAD_DIM), slice(c * rc, (c + 1) * rc)

            def products(item):
                cols, rows = slices(item)
                s = (lax.dot_general(q_ref[rows, cols], k_ref[:, cols], NT, preferred_element_type=F32)
                     + b_ref[nd - 1 - d, rows, :])
                dp = lax.dot_general(do_ref[rows, cols], v_ref[:, cols], NT, preferred_element_type=F32)
                return s, dp

            def gradients(item, p, ds):
                cols, rows = slices(item)
                dv_ref[:, cols] += lax.dot_general(p, do_ref[rows, cols], TN, preferred_element_type=F32)
                dk_ref[:, cols] += lax.dot_general(ds, q_ref[rows, cols], TN, preferred_element_type=F32)
                q_rows = pl.ds(pl.multiple_of(i * T + item[1] * rc, rc), rc)
                dq_ref[q_rows, cols] += jnp.dot(ds, k_ref[:, cols], preferred_element_type=F32) * scale

            nxt, pending = products(items[0]), None
            for n, item in enumerate(items):
                s, dp = nxt
                if n + 1 < len(items):
                    nxt = products(items[n + 1])
                if pending is not None:
                    gradients(*pending)
                _, rows = slices(item)
                p = jnp.exp(s - lse_ref[item[0], rows, :])
                ds = p * (dp - dl_ref[item[0], rows, :])
                pending = (item, p.astype(BF16), ds.astype(BF16))
            gradients(*pending)

    qi = lambda h, j, d: (jnp.clip(j + d - hb, 0, nq - 1), h)
    qs = pl.BlockSpec((T, wp), qi)
    col = pl.BlockSpec((hp, T, 1), lambda h, j, d: (h, jnp.clip(j + d - hb, 0, nq - 1), 0))
    kv = pl.BlockSpec((T, wp), lambda h, j, d: (j, h))
    shp = jax.ShapeDtypeStruct((S, W), F32)
    return _hosted_call(
        body, [q, do, lse, delta, k, v, bias], name=name, grid=(H // hp, nq, nd),
        in_specs=[qs, qs, col, col, kv, kv, pl.BlockSpec((nd, T, T), lambda h, j, d: (0, 0, 0))],
        out_specs=[pl.BlockSpec((S, wp), lambda h, j, d: (0, h)), kv, kv],
        out_shape=[shp, shp, shp],
        sem=("parallel", "arbitrary", "arbitrary"), comm=comm)


def _halo_specs(S, tr, W, piece):
    per, last = tr // 8, S // 8 - 1
    prev = pl.BlockSpec((8, W), lambda i: (jnp.maximum(i * per - 1, 0), piece))
    nxt = pl.BlockSpec((8, W), lambda i: (jnp.minimum((i + 1) * per, last), piece))
    return prev, nxt


def _shifted(t, before, after, tr):
    rows = lax.broadcasted_iota(jnp.int32, (tr, 1), 0)
    prev = jnp.where(rows == 0, before, pltpu.roll(t, 1, 0))
    nxt = jnp.where(rows == tr - 1, after, pltpu.roll(t, tr - 1, 0))
    return prev, nxt


def _ab_mix(attn, proj, conv_w, W, name):
    S = attn.shape[0]
    tr = min(S, ROW_TILE)
    nsteps = S // tr

    def body(a_ref, za_ref, ub_ref, gb_ref, gc_ref, zb_ref, ubp, ubn, gcp, gcn, w_ref, y_ref):
        i = pl.program_id(0)
        t = gc_ref[...] * ub_ref[...]
        before = jnp.where(i == 0, 0.0, (gcp[...] * ubp[...])[7:8, :])
        after = jnp.where(i == nsteps - 1, 0.0, (gcn[...] * ubn[...])[0:1, :])
        t_prev, t_next = _shifted(t, before, after, tr)
        w = w_ref[...]
        cv = w[0:1, :] * t_prev + w[1:2, :] * t + w[2:3, :] * t_next
        silu_a, _ = _silu_and_grad(za_ref[...])
        silu_b, _ = _silu_and_grad(zb_ref[...])
        y_ref[:, :W] = (a_ref[...] * silu_a).astype(BF16)
        y_ref[:, W:] = (gb_ref[...] * cv * silu_b).astype(BF16)

    piece = lambda p: pl.BlockSpec((tr, W), lambda i: (i, p))
    ubp, ubn = _halo_specs(S, tr, W, 4)
    gcp, gcn = _halo_specs(S, tr, W, 6)
    return pl.pallas_call(
        body, name=name, grid=(nsteps,),
        in_specs=[pl.BlockSpec((tr, W), lambda i: (i, 0)), piece(3), piece(4), piece(5), piece(6), piece(7),
                  ubp, ubn, gcp, gcn, pl.BlockSpec((3, W), lambda i: (0, 0))],
        out_specs=pl.BlockSpec((tr, 2 * W), lambda i: (i, 0)),
        out_shape=jax.ShapeDtypeStruct((S, 2 * W), BF16),
        compiler_params=_cp("parallel"),
    )(attn, proj, proj, proj, proj, proj, proj, proj, proj, proj, conv_w)


def _dattn_prep(dy, proj, attn, W, name):
    S = attn.shape[0]
    H = W // HEAD_DIM
    tr = min(S, ROW_TILE)

    def body(dy_ref, za_ref, a_ref, do_ref, dl_ref):
        silu_a, _ = _silu_and_grad(za_ref[...])
        do = dy_ref[...] * silu_a
        do_ref[...] = do.astype(BF16)
        prod = do * a_ref[...]
        for hh in range(H):
            dl_ref[hh] = jnp.sum(prod[:, hh * HEAD_DIM:(hh + 1) * HEAD_DIM], axis=1, keepdims=True)

    row = pl.BlockSpec((tr, W), lambda i: (i, 0))
    return pl.pallas_call(
        body, name=name, grid=(S // tr,),
        in_specs=[row, pl.BlockSpec((tr, W), lambda i: (i, 3)), row],
        out_specs=[row, pl.BlockSpec((H, tr, 1), lambda i: (0, i, 0))],
        out_shape=[jax.ShapeDtypeStruct((S, W), BF16), jax.ShapeDtypeStruct((H, S, 1), F32)],
        compiler_params=_cp("parallel"),
    )(dy, proj, attn)


def _ab_bwd(dy, attn, proj, dqr, dkr, dv, cosf, sinf, conv_w, W, name):
    S = attn.shape[0]
    tr = min(S, ROW_TILE // 2)
    nsteps = S // tr
    heads = W // HEAD_DIM

    def body(dya_ref, dyb_ref, a_ref, za_ref, ub_ref, gb_ref, gc_ref, zb_ref, dq_ref, dk_ref, dv_ref,
             c_ref, s_ref, w_ref, dybp, dybn, gbp, gbn, zbp, zbn, ubp, ubn, gcp, gcn,
             dp_ref, dw_ref, acc):
        i = pl.program_id(0)
        first, last = i == 0, i == nsteps - 1
        w = w_ref[...]
        w0, w1, w2 = w[0:1, :], w[1:2, :], w[2:3, :]
        ub, gb, gc, zb = ub_ref[...], gb_ref[...], gc_ref[...], zb_ref[...]
        dyb = dyb_ref[...]
        silu_a, dsilu_a = _silu_and_grad(za_ref[...])
        silu_b, dsilu_b = _silu_and_grad(zb)
        t = gc * ub
        t_prev, t_next = _shifted(t, jnp.where(first, 0.0, (gcp[...] * ubp[...])[7:8, :]),
                                  jnp.where(last, 0.0, (gcn[...] * ubn[...])[0:1, :]), tr)
        cv = w0 * t_prev + w1 * t + w2 * t_next
        dcv = dyb * gb * silu_b
        halo_p = dybp[...] * gbp[...] * _silu_and_grad(zbp[...])[0]
        halo_n = dybn[...] * gbn[...] * _silu_and_grad(zbn[...])[0]
        dcv_prev, dcv_next = _shifted(dcv, jnp.where(first, 0.0, halo_p[7:8, :]),
                                      jnp.where(last, 0.0, halo_n[0:1, :]), tr)
        dt = w0 * dcv_next + w1 * dcv + w2 * dcv_prev
        cosf_v, sinf_v = c_ref[...], s_ref[...]
        for src, base in ((dq_ref, 0), (dk_ref, W)):
            for hh, val in enumerate(_rope_apply(src[...], cosf_v, sinf_v, heads, -1.0)):
                dp_ref[:, base + hh * HEAD_DIM:base + (hh + 1) * HEAD_DIM] = val.astype(BF16)
        dp_ref[:, 2 * W:3 * W] = dv_ref[...].astype(BF16)
        dp_ref[:, 3 * W:4 * W] = (dya_ref[...] * a_ref[...] * dsilu_a).astype(BF16)
        dp_ref[:, 4 * W:5 * W] = (dt * gc).astype(BF16)
        dp_ref[:, 5 * W:6 * W] = (dyb * cv * silu_b).astype(BF16)
        dp_ref[:, 6 * W:7 * W] = (dt * ub).astype(BF16)
        dp_ref[:, 7 * W:8 * W] = (dyb * gb * cv * dsilu_b).astype(BF16)
        tap = lax.broadcasted_iota(jnp.int32, (8, 1), 0)
        part = (jnp.where(tap == 0, jnp.sum(dcv * t_prev, axis=0, keepdims=True), 0.0)
                + jnp.where(tap == 1, jnp.sum(dcv * t, axis=0, keepdims=True), 0.0)
                + jnp.where(tap == 2, jnp.sum(dcv * t_next, axis=0, keepdims=True), 0.0))

        @pl.when(first)
        def _():
            acc[...] = part

        @pl.when(i > 0)
        def _():
            acc[...] += part

        @pl.when(last)
        def _():
            dw_ref[...] = acc[...]

    row = pl.BlockSpec((tr, W), lambda i: (i, 0))
    piece = lambda p: pl.BlockSpec((tr, W), lambda i: (i, p))
    tab = pl.BlockSpec((tr, HEAD_DIM), lambda i: (i, 0))
    dybp, dybn = _halo_specs(S, tr, W, 1)
    gbp, gbn = _halo_specs(S, tr, W, 5)
    zbp, zbn = _halo_specs(S, tr, W, 7)
    ubp, ubn = _halo_specs(S, tr, W, 4)
    gcp, gcn = _halo_specs(S, tr, W, 6)
    return pl.pallas_call(
        body, name=name, grid=(nsteps,),
        in_specs=[piece(0), piece(1), row, piece(3), piece(4), piece(5), piece(6), piece(7), row, row, row,
                  tab, tab, pl.BlockSpec((3, W), lambda i: (0, 0)),
                  dybp, dybn, gbp, gbn, zbp, zbn, ubp, ubn, gcp, gcn],
        out_specs=[pl.BlockSpec((tr, 8 * W), lambda i: (i, 0)), pl.BlockSpec((8, W), lambda i: (0, 0))],
        out_shape=[jax.ShapeDtypeStruct((S, 8 * W), BF16), jax.ShapeDtypeStruct((8, W), F32)],
        scratch_shapes=[pltpu.VMEM((8, W), F32)],
        compiler_params=_cp("arbitrary"),
    )(dy, dy, attn, proj, proj, proj, proj, proj, dqr, dkr, dv, cosf, sinf, conv_w,
      dy, dy, proj, proj, proj, proj, proj, proj, proj, proj)


def _sgu_centre(p_ref, vc_s, dvg_s, Dc):
    gw = Dc // C_GROUPS
    total = None
    for g in range(C_GROUPS):
        cs = slice(g * gw, (g + 1) * gw)
        vg, dvg = _gelu_and_grad(p_ref[:, Dc + g * gw:Dc + (g + 1) * gw])
        vc_s[:, cs] = vg
        if dvg_s is not None:
            dvg_s[:, cs] = dvg
        total = vg if total is None else total + vg
    mu = jnp.sum(total, axis=1, keepdims=True) * (1.0 / Dc)
    total = None
    for g in range(C_GROUPS):
        cs = slice(g * gw, (g + 1) * gw)
        vc = vc_s[:, cs] - mu
        vc_s[:, cs] = vc
        total = vc * vc if total is None else total + vc * vc
    return lax.rsqrt(jnp.sum(total, axis=1, keepdims=True) * (1.0 / Dc) + EPS)


def _sgu_fwd(proj, ln_g, ln_b, w_s, b_st, name, comm=None):
    S, Dc3 = proj.shape
    Dc = Dc3 // 3
    gw = Dc // C_GROUPS
    vec = pl.BlockSpec((1, Dc), lambda i: (0, 0))

    def body(p_ref, lng_ref, lnb_ref, ws_ref, bst_ref, y_ref, vc_s):
        rstd = _sgu_centre(p_ref, vc_s, None, Dc)
        bst = bst_ref[...]
        for g in range(C_GROUPS):
            cs = slice(g * gw, (g + 1) * gw)
            vn = (vc_s[:, cs] * rstd * lng_ref[:, cs] + lnb_ref[:, cs]).astype(BF16)
            mixed = jnp.dot(ws_ref[g].astype(BF16), vn, preferred_element_type=F32) + bst[:, g:g + 1]
            u, _ = _gelu_and_grad(p_ref[:, cs])
            sz, _ = _silu_and_grad(p_ref[:, 2 * Dc + g * gw:2 * Dc + (g + 1) * gw])
            y_ref[:, cs] = (u * mixed * sz).astype(BF16)

    return _hosted_call(
        body, [proj, ln_g, ln_b, w_s, b_st], name=name, grid=(S // C_CHUNK,),
        in_specs=[pl.BlockSpec((C_CHUNK, Dc3), lambda i: (i, 0)), vec, vec,
                  pl.BlockSpec((C_GROUPS, C_CHUNK, C_CHUNK), lambda i: (0, 0, 0)),
                  pl.BlockSpec((C_CHUNK, C_GROUPS), lambda i: (0, 0))],
        out_specs=pl.BlockSpec((C_CHUNK, Dc), lambda i: (i, 0)),
        out_shape=jax.ShapeDtypeStruct((S, Dc), BF16),
        scratch_shapes=[pltpu.VMEM((C_CHUNK, Dc), F32)],
        sem=("parallel",), comm=comm)


def _sgu_bwd(proj, dy, ln_g, ln_b, w_s, w_st, b_st, name):
    S, Dc3 = proj.shape
    Dc = Dc3 // 3
    gw = Dc // C_GROUPS
    nsteps = S // C_CHUNK
    vec = pl.BlockSpec((1, Dc), lambda i: (0, 0))
    wspec = pl.BlockSpec((C_GROUPS, C_CHUNK, C_CHUNK), lambda i: (0, 0, 0))

    def body(p_ref, dy_ref, lng_ref, lnb_ref, ws_ref, wst_ref, bst_ref,
             dp_ref, dws_ref, dbs_ref, dlg_ref, dlb_ref, acc_w, acc_b, acc_g, acc_lb, vc_s, dvg_s, dvh_s):
        i = pl.program_id(0)

        @pl.when(i == 0)
        def _():
            acc_w[...] = jnp.zeros(acc_w.shape, F32)
            acc_b[...] = jnp.zeros(acc_b.shape, F32)
            acc_g[...] = jnp.zeros(acc_g.shape, F32)
            acc_lb[...] = jnp.zeros(acc_lb.shape, F32)

        rstd = _sgu_centre(p_ref, vc_s, dvg_s, Dc)
        bst = bst_ref[...]
        octets = lambda t: jnp.sum(t.reshape(C_CHUNK // 8, 8, gw), axis=0)
        t1, t2 = None, None
        for g in range(C_GROUPS):
            cs = slice(g * gw, (g + 1) * gw)
            zs = slice(2 * Dc + g * gw, 2 * Dc + (g + 1) * gw)
            vhat = vc_s[:, cs] * rstd
            vn = (vhat * lng_ref[:, cs] + lnb_ref[:, cs]).astype(BF16)
            mixed = jnp.dot(ws_ref[g].astype(BF16), vn, preferred_element_type=F32) + bst[:, g:g + 1]
            u, du = _gelu_and_grad(p_ref[:, cs])
            sz, dsz = _silu_and_grad(p_ref[:, zs])
            dy = dy_ref[:, cs]
            dmixed = dy * u * sz
            dmb = dmixed.astype(BF16)
            acc_w[g] += lax.dot_general(dmb, vn, NT, preferred_element_type=F32)
            acc_b[g] += dmixed
            dvn = jnp.dot(wst_ref[g].astype(BF16), dmb, preferred_element_type=F32)
            acc_g[:, cs] += octets(dvn * vhat)
            acc_lb[:, cs] += octets(dvn)
            dvh = dvn * lng_ref[:, cs]
            dvh_s[:, cs] = dvh
            t1 = dvh if t1 is None else t1 + dvh
            t2 = dvh * vhat if t2 is None else t2 + dvh * vhat
            dp_ref[:, cs] = (dy * mixed * sz * du).astype(BF16)
            dp_ref[:, zs] = (dy * u * mixed * dsz).astype(BF16)
        m1 = jnp.sum(t1, axis=1, keepdims=True) * (1.0 / Dc)
        m2 = jnp.sum(t2, axis=1, keepdims=True) * (1.0 / Dc)
        for g in range(C_GROUPS):
            cs = slice(g * gw, (g + 1) * gw)
            dvgelu = rstd * (dvh_s[:, cs] - m1 - (vc_s[:, cs] * rstd) * m2)
            dp_ref[:, Dc + g * gw:Dc + (g + 1) * gw] = (dvgelu * dvg_s[:, cs]).astype(BF16)

        @pl.when(i == nsteps - 1)
        def _():
            dws_ref[...] = acc_w[...]
            for g in range(C_GROUPS):
                dbs_ref[g] = jnp.sum(acc_b[g], axis=1, keepdims=True)
            dlg_ref[...] = jnp.sum(acc_g[...], axis=0, keepdims=True)
            dlb_ref[...] = jnp.sum(acc_lb[...], axis=0, keepdims=True)

    v = jax.ShapeDtypeStruct((1, Dc), F32)
    return pl.pallas_call(
        body, name=name, grid=(nsteps,),
        in_specs=[pl.BlockSpec((C_CHUNK, Dc3), lambda i: (i, 0)), pl.BlockSpec((C_CHUNK, Dc), lambda i: (i, 0)),
                  vec, vec, wspec, wspec, pl.BlockSpec((C_CHUNK, C_GROUPS), lambda i: (0, 0))],
        out_specs=[pl.BlockSpec((C_CHUNK, Dc3), lambda i: (i, 0)), wspec,
                   pl.BlockSpec((C_GROUPS, C_CHUNK, 1), lambda i: (0, 0, 0)), vec, vec],
        out_shape=[jax.ShapeDtypeStruct((S, Dc3), BF16),
                   jax.ShapeDtypeStruct((C_GROUPS, C_CHUNK, C_CHUNK), F32),
                   jax.ShapeDtypeStruct((C_GROUPS, C_CHUNK, 1), F32), v, v],
        scratch_shapes=[pltpu.VMEM((C_GROUPS, C_CHUNK, C_CHUNK), F32), pltpu.VMEM((C_GROUPS, C_CHUNK, gw), F32),
                        pltpu.VMEM((8, Dc), F32), pltpu.VMEM((8, Dc), F32)]
                       + [pltpu.VMEM((C_CHUNK, Dc), F32)] * 3,
        compiler_params=_cp("arbitrary"),
    )(proj, dy, ln_g, ln_b, w_s, w_st, b_st)


PACK_COLS = 1024
PACK_ROWS = 64


def _pack(vectors):
    flat = jnp.concatenate([v.reshape(-1) for v in vectors])
    pad = (-flat.shape[0]) % (PACK_COLS * PACK_ROWS)
    return jnp.pad(flat, (0, pad)).reshape(-1, PACK_COLS)


def _unshard(g, off, shape):
    L, rest = shape[0], shape[1:]
    size = int(np.prod(shape))
    piece = g[:, off:off + size].reshape((NDEV,) + tuple(shape))
    nd = piece.ndim
    perm = tuple(range(1, nd - 1)) + (0, nd - 1)
    full = jnp.transpose(piece, perm)
    return full.reshape(tuple(shape[:-1]) + (NDEV * shape[-1],)), off + size


def kernel(x, c, ab_norm_g, ab_w_mod, ab_b_mod, ab_w_in, ab_conv_w, ab_w_out, sg_norm_g, sg_w_mod, sg_b_mod, sg_w_in, sg_ln_g, sg_ln_b, sg_w_s, sg_b_s, sg_w_out, final_norm_g, loss_target, m_ab_norm_g, m_ab_w_mod, m_ab_b_mod, m_ab_w_in, m_ab_conv_w, m_ab_w_out, m_sg_norm_g, m_sg_w_mod, m_sg_b_mod, m_sg_w_in, m_sg_ln_g, m_sg_ln_b, m_sg_w_s, m_sg_b_s, m_sg_w_out, m_final_norm_g, v_ab_norm_g, v_ab_w_mod, v_ab_b_mod, v_ab_w_in, v_ab_conv_w, v_ab_w_out, v_sg_norm_g, v_sg_w_mod, v_sg_b_mod, v_sg_w_in, v_sg_ln_g, v_sg_ln_b, v_sg_w_s, v_sg_b_s, v_sg_w_out, v_final_norm_g):
    _, S, D = x.shape
    L = ab_norm_g.shape[0]
    W = ab_conv_w.shape[2] * NDEV
    n_ab, n_sg = ab_w_in.shape[2], sg_w_in.shape[2]
    n_mod = ab_w_mod.shape[2]
    kb = ab_w_out.shape[1]
    xi, yi, ci = _position()
    dev = 4 * xi + 2 * yi + ci
    x2, tgt = x.reshape(S, D), loss_target.reshape(S, D)

    small = [c, ab_conv_w, sg_norm_g, sg_ln_g, sg_ln_b]
    (g1,) = _comm_only(_Gather([_pack(small)]), "ag_small")
    g1 = g1.reshape(NDEV, -1)
    c_all = g1[:, :D]
    off = D
    conv_full, off = _unshard(g1, off, ab_conv_w.shape)
    sg_norm_full, off = _unshard(g1, off, sg_norm_g.shape)
    ln_g_full, off = _unshard(g1, off, sg_ln_g.shape)
    ln_b_full, off = _unshard(g1, off, sg_ln_b.shape)

    ab_b_cols = lax.dynamic_slice_in_dim(ab_b_mod, dev * n_mod, n_mod, axis=1)
    m_ab = _mod_fwd(c_all, ab_w_mod, ab_b_cols.reshape(L, 1, n_mod), "mod_fwd_ab")
    m_sg = _mod_fwd(c_all, sg_w_mod, sg_b_mod.reshape(L, 1, n_mod), "mod_fwd_sg")
    m_part = jnp.stack([m_ab, m_sg]).transpose(2, 0, 1, 3).reshape(NDEV, 2 * L * n_mod)
    (g2,) = _comm_only(_Gather([m_part]), "ag_mod")
    mine = lax.dynamic_index_in_dim(g2, dev, axis=1, keepdims=False)
    mods = mine.reshape(NDEV, 2, L, n_mod).transpose(1, 2, 0, 3).reshape(2, L, 3 * D)

    def mod_of(kind, i):
        m = mods[kind, i]
        return m[:D].reshape(1, D), m[D:2 * D].reshape(1, D), m[2 * D:].reshape(1, D)

    big_w = [[(ab_w_in, m_ab_w_in, v_ab_w_in), (ab_w_out, m_ab_w_out, v_ab_w_out)],
             [(sg_w_in, m_sg_w_in, v_sg_w_in), (sg_w_out, m_sg_w_out, v_sg_w_out)]]
    big_names = [["ab_w_in", "ab_w_out"], ["sg_w_in", "sg_w_out"]]
    n_layers = 2 * L
    shards = [[big_w[layer % 2][k][0][layer // 2].astype(BF16) for k in range(2)] for layer in range(n_layers)]
    gathered = {}

    def gather_of(keys):
        keys = [key for key in keys if key[0] < n_layers]
        return keys, (_Gather([shards[layer][k] for layer, k in keys]) if keys else None)

    def keep_gathered(keys, res):
        for (layer, k), g in zip(keys, res):
            gathered[(layer, k)] = g.reshape((NDEV, 1, D, g.shape[-1]) if k == 0 else (NDEV, 1, kb, D))

    keys, comm = gather_of([(0, 0)])
    keep_gathered(keys, _comm_only(comm, "ag_w_in_layer0"))

    cosf, sinf = _rope_tables(S)
    T = min(S, ATTN_TILE)
    bias = _attn_bias(T)
    norm_g = [ab_norm_g, sg_norm_full]
    w_s_t = jnp.swapaxes(sg_w_s, -1, -2)
    b_s_t = jnp.swapaxes(sg_b_s, -1, -2)

    saved = []
    x_cur, res, gate_prev = x2, None, None
    for layer in range(2 * L):
        kind, i = layer % 2, layer // 2
        tag = f"{'ab' if kind == 0 else 'sg'}{i}"
        shift, scale, gate = mod_of(kind, i)
        g = norm_g[kind][i].reshape(1, D)
        xl, h = _pre(x_cur, res, gate_prev, g, scale, shift, f"pre_{tag}")
        keys, comm = gather_of([(layer + 1, 0)])
        if comm is None:
            proj = _mm_nn_in(h, gathered[(layer, 0)], 0, f"proj_{tag}")
        else:
            proj, got = _mm_nn_in(h, gathered[(layer, 0)], 0, f"proj_{tag}", comm)
            keep_gathered(keys, got)
        rec = dict(xl=xl, h=h, proj=proj, g=g, scale=scale, gate=gate)
        keys, comm = gather_of(([(0, 1)] if layer == 0 else []) + [(layer + 1, 1)])
        if kind == 0:
            qr, kr, vb = _rope_qkv(proj, cosf, sinf, W, f"rope_{tag}")
            (attn, lse), got = _attn_fwd(qr, kr, vb, bias, f"attn_{tag}", comm)
            y = _ab_mix(attn, proj, conv_full[i], W, f"mix_{tag}")
            rec.update(qr=qr, kr=kr, vb=vb, attn=attn, lse=lse)
        else:
            y, got = _sgu_fwd(proj, ln_g_full[i].reshape(1, D), ln_b_full[i].reshape(1, D), sg_w_s[i], b_s_t[i],
                              f"sgu_{tag}", comm)
        keep_gathered(keys, got)
        out = _mm_nn_out(y, gathered[(layer, 1)], 0, f"out_{tag}")
        rec.update(y=y, out=out)
        saved.append(rec)
        x_cur, res, gate_prev = xl, out, gate

    dx, loss_part, d_final_g = _loss_head(x_cur, res, gate_prev, final_norm_g.reshape(1, D), tgt, "loss_head")
    loss = lax.psum(loss_part[0, 0], ("x", "y", "c"))

    c_idx = ci.reshape(1).astype(jnp.int32)
    big_res = {}
    pending = None

    def sum_and_update(done, k, from_chips, part=(0, 1)):
        nm = big_names[done % 2][k]
        w, m, v = big_w[done % 2][k]
        flat = lambda a: a.reshape(L * a.shape[1], a.shape[2])
        row0 = (done // 2) * w.shape[1] + part[0] * (w.shape[1] // part[1])
        big_res[nm] = _sum_adam(from_chips, flat(w), flat(m), flat(v), row0, big_res.get(nm),
                                f"adam_{nm}{done // 2}_{part[0]}")

    def finish_layer(done, from_chips):
        for k in range(2):
            sum_and_update(done, k, from_chips[k])

    dm = [[None] * L, [None] * L]
    d_norm = [[None] * L, [None] * L]
    d_conv, d_lng, d_lnb, d_ws, d_bs = [None] * L, [None] * L, [None] * L, [None] * L, [None] * L
    for layer in reversed(range(2 * L)):
        kind, i = layer % 2, layer // 2
        tag = f"{'ab' if kind == 0 else 'sg'}{i}"
        rec = saved[layer]
        w_in_l, w_out_l = gathered[(layer, 0)], gathered[(layer, 1)]
        dout, dgate = _post_bwd(dx, rec["out"], rec["gate"], f"post_bwd_{tag}")
        dy = _mm_nt_out(dout, w_out_l, 0, f"dy_{tag}")
        dwo = _mm_tn_out(rec["y"], dout, 0, 1, None, f"dwout_{tag}")
        if kind == 0:
            do, delta = _dattn_prep(dy, rec["proj"], rec["attn"], W, f"dattn_{tag}")
            comm = _ToChips(pending[1]) if pending else None
            (dqr, dkr, dvv), got = _attn_bwd(rec["qr"], rec["kr"], rec["vb"], do, rec["lse"], delta, bias,
                                             f"attn_bwd_{tag}", comm)
            if pending:
                finish_layer(pending[0], got)
            dproj, dcw = _ab_bwd(dy, rec["attn"], rec["proj"], dqr, dkr, dvv, cosf, sinf, conv_full[i], W,
                                 f"mix_bwd_{tag}")
            d_conv[i] = dcw[:3]
            earlier = None
        else:
            dproj, d_ws[i], dbs, d_lng[i], d_lnb[i] = _sgu_bwd(
                rec["proj"], dy, ln_g_full[i].reshape(1, D), ln_b_full[i].reshape(1, D),
                sg_w_s[i], w_s_t[i], b_s_t[i], f"sgu_bwd_{tag}")
            d_bs[i] = dbs.reshape(C_GROUPS, C_CHUNK)
            earlier = pending
        if layer == 0:
            h_l = rec["h"]
            dwi_a = _mm_tn_in(h_l, dproj, 0, 1, None, f"dwin_a_{tag}", part=(0, 2))
            g_a = [dwi_a.reshape(NDEV, D // 2, -1), dwo.reshape(NDEV, kb, D)]
            dwi_b, sib_a = _mm_tn_in(h_l, dproj, 0, 1, None, f"dwin_b_{tag}", _ToSibling(g_a), part=(1, 2))
            p_a = [_add_sibling(g, r, c_idx, f"rs_add_a{k}_{tag}") for k, (g, r) in enumerate(zip(g_a, sib_a))]
            g_b = [dwi_b.reshape(NDEV, D // 2, -1)]
            both = _Both(_ToSibling(g_b), _ToChips(p_a))
            dh, got = _mm_nt_in(dproj, w_in_l, 0, f"dh_{tag}", both)
            sib_b, chips_a = both.split_results(got)
            p_b = [_add_sibling(g_b[0], sib_b[0], c_idx, f"rs_add_b_{tag}")]
            (dx, dshift, dscale, d_norm[kind][i]), chips_b = _pre_bwd(
                rec["xl"], dh, dx, rec["g"], rec["scale"], f"pre_bwd_{tag}", _ToChips(p_b))
            sum_and_update(0, 0, chips_a[0], (0, 2))
            sum_and_update(0, 1, chips_a[1])
            sum_and_update(0, 0, chips_b[0], (1, 2))
            dm[kind][i] = jnp.concatenate([dshift, dscale, dgate], axis=1).reshape(3 * D)
            continue
        if earlier:
            dwi, got_in = _mm_tn_in(rec["h"], dproj, 0, 1, None, f"dwin_{tag}", _ToChips(earlier[1][:1]))
        else:
            dwi = _mm_tn_in(rec["h"], dproj, 0, 1, None, f"dwin_{tag}")
        grads = [dwi.reshape(NDEV, D, -1), dwo.reshape(NDEV, kb, D)]
        if earlier:
            both = _Both(_ToSibling(grads), _ToChips(earlier[1][1:]))
            dh, got = _mm_nt_in(dproj, w_in_l, 0, f"dh_{tag}", both)
            from_sibling, got_out = both.split_results(got)
            finish_layer(earlier[0], [got_in[0], got_out[0]])
        else:
            dh, from_sibling = _mm_nt_in(dproj, w_in_l, 0, f"dh_{tag}", _ToSibling(grads))
        pending = (layer, [_add_sibling(g, r, c_idx, f"rs_add_{big_names[kind][k]}{i}")
                           for k, (g, r) in enumerate(zip(grads, from_sibling))])
        (dx, dshift, dscale, d_norm[kind][i]), _ = _pre_bwd(
            rec["xl"], dh, dx, rec["g"], rec["scale"], f"pre_bwd_{tag}")
        dm[kind][i] = jnp.concatenate([dshift, dscale, dgate], axis=1).reshape(3 * D)
    grad_x = dx.reshape(1, S, D)
    for kind in range(2):
        for k in range(2):
            nm = big_names[kind][k]
            big_res[nm] = [o.reshape(big_w[kind][k][0].shape) for o in big_res[nm]]

    stack = lambda xs: jnp.stack(xs)
    pack_items = [stack(dm[0]), stack(dm[1]), stack(d_norm[0]).reshape(L, D), stack(d_conv),
                  stack(d_norm[1]).reshape(L, D), stack(d_lng).reshape(L, D), stack(d_lnb).reshape(L, D),
                  stack(d_ws), stack(d_bs), d_final_g]
    (g3,) = _comm_only(_Gather([_pack(pack_items)]), "ag_grads")
    P = g3.shape[1] * g3.shape[2]
    tot = _sum_rows(g3, "sum_small").reshape(P)
    g3 = g3.reshape(NDEV, P)
    sizes = [int(np.prod(p.shape)) for p in pack_items]
    offs = np.concatenate([[0], np.cumsum(sizes)]).tolist()
    seg = lambda k, shape: tot[offs[k]:offs[k + 1]].reshape(shape)

    def shard(full, n):
        return lax.dynamic_slice_in_dim(full, dev * n, n, axis=full.ndim - 1)

    g_ab_b_mod = seg(0, (L, 3 * D))
    g_sg_b_mod = shard(seg(1, (L, 3 * D)), n_mod)
    g_ab_norm = seg(2, (L, D))
    g_conv = shard(seg(3, (L, 3, W)), W // NDEV)
    g_sg_norm = shard(seg(4, (L, D)), kb)
    g_ln_g = shard(seg(5, (L, D)), kb)
    g_ln_b = shard(seg(6, (L, D)), kb)
    g_w_s = seg(7, sg_w_s.shape)
    g_b_s = seg(8, sg_b_s.shape)
    g_final = seg(9, (D,))

    small_w = [("ab_norm_g", g_ab_norm, ab_norm_g, m_ab_norm_g, v_ab_norm_g),
               ("ab_b_mod", g_ab_b_mod, ab_b_mod, m_ab_b_mod, v_ab_b_mod),
               ("ab_conv_w", g_conv, ab_conv_w, m_ab_conv_w, v_ab_conv_w),
               ("sg_norm_g", g_sg_norm, sg_norm_g, m_sg_norm_g, v_sg_norm_g),
               ("sg_b_mod", g_sg_b_mod, sg_b_mod, m_sg_b_mod, v_sg_b_mod),
               ("sg_ln_g", g_ln_g, sg_ln_g, m_sg_ln_g, v_sg_ln_g),
               ("sg_ln_b", g_ln_b, sg_ln_b, m_sg_ln_b, v_sg_ln_b),
               ("sg_w_s", g_w_s, sg_w_s, m_sg_w_s, v_sg_w_s),
               ("sg_b_s", g_b_s, sg_b_s, m_sg_b_s, v_sg_b_s),
               ("final_norm_g", g_final, final_norm_g, m_final_norm_g, v_final_norm_g)]
    packed = [_pack([t[k] for t in small_w]) for k in (1, 2, 3, 4)]
    upd = _adam_only(*packed, "adam_small")
    small_res = {}
    o = 0
    for nm, g, w, _, _ in small_w:
        size = int(np.prod(w.shape))
        small_res[nm] = [g] + [u.reshape(-1)[o:o + size].reshape(w.shape) for u in upd]
        o += size

    KP = 128
    sc_t = jnp.pad((c_all * jax.nn.sigmoid(c_all)).T, ((0, 0), (0, KP - NDEV)))
    mod_res = {}
    for kind, nm, (w, m, v) in ((0, "ab_w_mod", (ab_w_mod, m_ab_w_mod, v_ab_w_mod)),
                                (1, "sg_w_mod", (sg_w_mod, m_sg_w_mod, v_sg_w_mod))):
        dm_all = g3[:, offs[kind]:offs[kind + 1]].reshape(NDEV, L, 3 * D)
        cols = jnp.pad(shard(dm_all, n_mod).transpose(1, 0, 2), ((0, 0), (0, KP - NDEV), (0, 0)))
        mod_res[nm] = _wmod_grad_adam(sc_t, cols, w, m, v, f"adam_{nm}")

    order = ["ab_norm_g", "ab_w_mod", "ab_b_mod", "ab_w_in", "ab_conv_w", "ab_w_out", "sg_norm_g", "sg_w_mod",
             "sg_b_mod", "sg_w_in", "sg_ln_g", "sg_ln_b", "sg_w_s", "sg_b_s", "sg_w_out", "final_norm_g"]
    res = {**big_res, **small_res, **mod_res}
    outs = [loss, grad_x]
    for k in range(4):
        outs += [res[nm][k] for nm in order]
    return tuple(outs)
```

```python
import functools
import math

import numpy as np
import jax
import jax.numpy as jnp
from jax import lax
from jax.experimental import pallas as pl
from jax.experimental.pallas import tpu as pltpu

F32 = jnp.float32
BF16 = jnp.bfloat16

NDEV = 8
NCHIP = 4
EPS = 1e-6
HEAD_DIM = 128
ROPE_THETA = 10000.0
DILATED_PATTERNS = ((128, 1), (512, 4), (2048, 16))
NEG_INF = -1e30
C_CHUNK = 128
C_GROUPS = 8
ADAM_LR = 0.001
ADAM_B1 = 0.9
ADAM_B2 = 0.999
ADAM_EPS = 1e-08
ADAM_WD = 0.01
ADAM_STEP = 10
GELU_K = math.sqrt(2.0 / math.pi)
GELU_C = 0.044715

VMEM_LIMIT_BYTES = 56 * 1024 * 1024
ATTN_TILE = 512
HEADS_PER_STEP = 4
ATTN_ROW_CHUNK = 256
LANES = 128
ROW_TILE = 256
MESH = pl.DeviceIdType.MESH
ANY = pl.BlockSpec(memory_space=pl.ANY)


def _cp(*sem):
    return pltpu.CompilerParams(dimension_semantics=sem, vmem_limit_bytes=VMEM_LIMIT_BYTES)


def _sigmoid(z):
    return 0.5 * (jnp.tanh(0.5 * z) + 1.0)


def _silu_and_grad(z):
    s = _sigmoid(z)
    return z * s, s * (1.0 + z * (1.0 - s))


def _gelu_and_grad(x):
    x2 = x * x
    t = jnp.tanh(GELU_K * (x + GELU_C * x2 * x))
    g = 0.5 * x * (1.0 + t)
    dg = 0.5 * (1.0 + t) + 0.5 * x * (1.0 - t * t) * (GELU_K * (1.0 + 3.0 * GELU_C * x2))
    return g, dg


def _position():
    return lax.axis_index("x"), lax.axis_index("y"), lax.axis_index("c")


def _chips(x, y):
    return [(1 - x, y), (x, 1 - y), (1 - x, 1 - y)]


class _Gather:
    def __init__(self, arrs):
        n = len(arrs)
        self.arrs = list(arrs)
        self.out_shape = [jax.ShapeDtypeStruct((NDEV,) + a.shape, a.dtype) for a in arrs]
        self.scratch = [pltpu.SemaphoreType.DMA((n, 7)), pltpu.SemaphoreType.DMA((n, 7)),
                        pltpu.SemaphoreType.DMA((n,))]

    def _copies(self, ins, outs, sems, own=True):
        send_sems, recv_sems, local_sems = sems
        x, y, c = _position()

        def copy(a, k, block, to, src=None):
            dst = outs[a].at[4 * block[0] + 2 * block[1] + block[2]]
            return pltpu.make_async_remote_copy(
                src_ref=dst if src is None else src, dst_ref=dst,
                send_sem=send_sems.at[a, k], recv_sem=recv_sems.at[a, k],
                device_id=to, device_id_type=MESH)

        n = len(ins)
        me, sibling = (x, y, c), (x, y, 1 - c)
        mine, first = [], []
        if own:
            mine = [pltpu.make_async_copy(ins[a], outs[a].at[4 * x + 2 * y + c], local_sems.at[a])
                    for a in range(n)]
            for a in range(n):
                first.append(copy(a, 0, me, sibling, src=ins[a]))
                first += [copy(a, 1 + j, me, (*chip, c), src=ins[a]) for j, chip in enumerate(_chips(x, y))]
        return copy, mine, first

    def start(self, ins, outs, sems):
        _, mine, first = self._copies(ins, outs, sems)
        for cp in mine + first:
            cp.start()

    def middle(self, ins, outs, sems):
        copy = self._copies(ins, outs, sems, own=False)[0]
        x, y, c = _position()
        me, sibling = (x, y, c), (x, y, 1 - c)
        for j, chip in enumerate(_chips(x, y)):
            for a in range(len(ins)):
                copy(a, 1 + j, (*chip, c), me).wait_recv()
                copy(a, 4 + j, (*chip, c), sibling).start()

    def finish(self, ins, outs, sems):
        copy, mine, first = self._copies(ins, outs, sems)
        x, y, c = _position()
        me, sibling = (x, y, c), (x, y, 1 - c)
        passed = [copy(a, 4 + j, (*chip, c), sibling)
                  for j, chip in enumerate(_chips(x, y)) for a in range(len(ins))]
        for a in range(len(ins)):
            copy(a, 0, sibling, me).wait_recv()
            for j, chip in enumerate(_chips(x, y)):
                copy(a, 4 + j, (*chip, 1 - c), me).wait_recv()
        for cp in first + passed:
            cp.wait_send()
        for cp in mine:
            cp.wait()


class _GatherDirect:
    def __init__(self, arrs):
        n = len(arrs)
        self.arrs = list(arrs)
        self.out_shape = [jax.ShapeDtypeStruct((NDEV,) + a.shape, a.dtype) for a in arrs]
        self.scratch = [pltpu.SemaphoreType.DMA((n, 7)), pltpu.SemaphoreType.DMA((n, 7)),
                        pltpu.SemaphoreType.DMA((n,))]

    def _copies(self, ins, outs, sems, arrivals):
        send_sems, recv_sems, local_sems = sems
        x, y, c = _position()
        mine = [pltpu.make_async_copy(ins[a], outs[a].at[4 * x + 2 * y + c], local_sems.at[a])
                for a in range(len(ins))]
        sends, recvs = [], []
        for a in range(len(ins)):
            for k in range(1, NDEV):
                px = 1 - x if k & 4 else x
                py = 1 - y if k & 2 else y
                pc = 1 - c if k & 1 else c
                sends.append(pltpu.make_async_remote_copy(
                    src_ref=ins[a], dst_ref=outs[a].at[4 * x + 2 * y + c],
                    send_sem=send_sems.at[a, k - 1], recv_sem=recv_sems.at[a, k - 1],
                    device_id=(px, py, pc), device_id_type=MESH))
                if arrivals:
                    slot = outs[a].at[4 * px + 2 * py + pc]
                    recvs.append(pltpu.make_async_remote_copy(
                        src_ref=slot, dst_ref=slot, send_sem=send_sems.at[a, k - 1], recv_sem=recv_sems.at[a, k - 1],
                        device_id=(px, py, pc), device_id_type=MESH))
        return mine, sends, recvs

    def start(self, ins, outs, sems):
        mine, sends, _ = self._copies(ins, outs, sems, False)
        for cp in mine + sends:
            cp.start()

    def finish(self, ins, outs, sems):
        mine, sends, recvs = self._copies(ins, outs, sems, True)
        for cp in recvs:
            cp.wait_recv()
        for cp in sends:
            cp.wait_send()
        for cp in mine:
            cp.wait()


class _ToSibling:
    def __init__(self, gs):
        n = len(gs)
        self.arrs = list(gs)
        self.out_shape = [jax.ShapeDtypeStruct((NCHIP,) + g.shape[1:], g.dtype) for g in gs]
        self.scratch = [pltpu.SemaphoreType.DMA((n, NCHIP)), pltpu.SemaphoreType.DMA((n, NCHIP))]

    def _copies(self, ins, outs, sems):
        send_sems, recv_sems = sems
        x, y, c = _position()
        return [pltpu.make_async_remote_copy(
            src_ref=ins[a].at[2 * k + (1 - c)], dst_ref=outs[a].at[k],
            send_sem=send_sems.at[a, k], recv_sem=recv_sems.at[a, k],
            device_id=(x, y, 1 - c), device_id_type=MESH) for a in range(len(ins)) for k in range(NCHIP)]

    def start(self, ins, outs, sems):
        for cp in self._copies(ins, outs, sems):
            cp.start()

    def finish(self, ins, outs, sems):
        copies = self._copies(ins, outs, sems)
        for cp in copies:
            cp.wait_recv()
        for cp in copies:
            cp.wait_send()


class _ToChips:
    def __init__(self, ps):
        n = len(ps)
        self.arrs = list(ps)
        self.out_shape = [jax.ShapeDtypeStruct(p.shape, p.dtype) for p in ps]
        self.scratch = [pltpu.SemaphoreType.DMA((n, 3)), pltpu.SemaphoreType.DMA((n, 3)),
                        pltpu.SemaphoreType.DMA((n,))]

    def _copies(self, ins, outs, sems, arrivals):
        send_sems, recv_sems, local_sems = sems
        x, y, c = _position()
        mychip = 2 * x + y
        n = len(ins)
        mine = [pltpu.make_async_copy(ins[a].at[mychip], outs[a].at[mychip], local_sems.at[a]) for a in range(n)]
        sends, recvs = [], []
        for a in range(n):
            for j, chip in enumerate(_chips(x, y)):
                sends.append(pltpu.make_async_remote_copy(
                    src_ref=ins[a].at[2 * chip[0] + chip[1]], dst_ref=outs[a].at[mychip],
                    send_sem=send_sems.at[a, j], recv_sem=recv_sems.at[a, j],
                    device_id=(*chip, c), device_id_type=MESH))
                if arrivals:
                    slot = outs[a].at[2 * chip[0] + chip[1]]
                    recvs.append(pltpu.make_async_remote_copy(
                        src_ref=slot, dst_ref=slot, send_sem=send_sems.at[a, j], recv_sem=recv_sems.at[a, j],
                        device_id=(*chip, c), device_id_type=MESH))
        return mine, sends, recvs

    def start(self, ins, outs, sems):
        mine, sends, _ = self._copies(ins, outs, sems, False)
        for cp in mine + sends:
            cp.start()

    def finish(self, ins, outs, sems):
        mine, sends, recvs = self._copies(ins, outs, sems, True)
        for cp in recvs:
            cp.wait_recv()
        for cp in sends:
            cp.wait_send()
        for cp in mine:
            cp.wait()


HOSTED_MIDDLE_AT = 0.8


def _middle_of(comm, ins, outs, sems):
    if hasattr(comm, "middle"):
        comm.middle(ins, outs, sems)


class _Both:
    def __init__(self, first, second):
        self.parts = (first, second)
        self.arrs = first.arrs + second.arrs
        self.out_shape = first.out_shape + second.out_shape
        self.scratch = first.scratch + second.scratch

    def _split(self, ins, outs, sems):
        a, _ = self.parts
        ni, no, ns = len(a.arrs), len(a.out_shape), len(a.scratch)
        return (ins[:ni], outs[:no], sems[:ns]), (ins[ni:], outs[no:], sems[ns:])

    def start(self, ins, outs, sems):
        for part, refs in zip(self.parts, self._split(ins, outs, sems)):
            part.start(*refs)

    def middle(self, ins, outs, sems):
        for part, refs in zip(self.parts, self._split(ins, outs, sems)):
            _middle_of(part, *refs)

    def finish(self, ins, outs, sems):
        for part, refs in zip(self.parts, self._split(ins, outs, sems)):
            part.finish(*refs)

    def split_results(self, res):
        no = len(self.parts[0].out_shape)
        return res[:no], res[no:]


def _comm_only(comm, name):
    n_in, n_out = len(comm.arrs), len(comm.out_shape)

    def body(*refs):
        ins, outs, sems = refs[:n_in], refs[n_in:n_in + n_out], refs[n_in + n_out:]
        comm.start(ins, outs, sems)
        _middle_of(comm, ins, outs, sems)
        comm.finish(ins, outs, sems)

    return pl.pallas_call(
        body, name=name, out_shape=comm.out_shape, in_specs=[ANY] * n_in, out_specs=[ANY] * n_out,
        scratch_shapes=comm.scratch,
    )(*comm.arrs)


def _hosted_call(body, operands, *, name, grid, in_specs, out_specs, out_shape, scratch_shapes=(), sem=(),
                 aliases=None, comm=None):
    single = not isinstance(out_shape, (list, tuple))
    o_specs = [out_specs] if single else list(out_specs)
    o_shape = [out_shape] if single else list(out_shape)
    n_in, n_out, n_scr = len(in_specs), len(o_shape), len(scratch_shapes)
    if comm is None:
        res = pl.pallas_call(body, name=name, grid=grid, in_specs=list(in_specs), out_specs=o_specs,
                             out_shape=o_shape, scratch_shapes=list(scratch_shapes),
                             input_output_aliases=aliases or {}, compiler_params=_cp(*sem))(*operands)
        return (res[0] if single else res), []
    c_in, c_out = len(comm.arrs), len(comm.out_shape)

    def wrapped(*refs):
        ins, cins = refs[:n_in], refs[n_in:n_in + c_in]
        o0 = n_in + c_in
        outs, couts = refs[o0:o0 + n_out], refs[o0 + n_out:o0 + n_out + c_out]
        s0 = o0 + n_out + c_out
        scr, csems = refs[s0:s0 + n_scr], refs[s0 + n_scr:]
        pids = [pl.program_id(a) for a in range(len(grid))]
        step = functools.reduce(lambda acc, pg: acc * pg[1] + pg[0], zip(pids, grid), 0)
        total = int(np.prod(grid))
        late = min(total - 1, max(1, int(total * HOSTED_MIDDLE_AT)))

        @pl.when(step == 0)
        def _():
            comm.start(cins, couts, csems)

        @pl.when(step == late)
        def _():
            _middle_of(comm, cins, couts, csems)

        body(*ins, *outs, *scr)

        @pl.when(step == total - 1)
        def _():
            comm.finish(cins, couts, csems)

    res = pl.pallas_call(
        wrapped, name=name, grid=grid, in_specs=list(in_specs) + [ANY] * c_in, out_specs=o_specs + [ANY] * c_out,
        out_shape=o_shape + comm.out_shape, scratch_shapes=list(scratch_shapes) + comm.scratch,
        input_output_aliases=aliases or {}, compiler_params=_cp(*(["arbitrary"] * len(grid))),
    )(*operands, *comm.arrs)
    return (res[0] if single else res[:n_out]), res[n_out:]


def _adamw(w, g, m, v):
    m2 = ADAM_B1 * m + (1.0 - ADAM_B1) * g
    v2 = ADAM_B2 * v + (1.0 - ADAM_B2) * (g * g)
    m_hat = m2 / (1.0 - ADAM_B1 ** ADAM_STEP)
    v_hat = v2 / (1.0 - ADAM_B2 ** ADAM_STEP)
    delta = -ADAM_LR * (m_hat / (jnp.sqrt(v_hat) + ADAM_EPS) + ADAM_WD * w)
    return delta, m2, v2


def _add_sibling(g, recv, c_idx, name):
    _, R, C = g.shape
    tr = min(R, 512)

    def body(c_ref, g_ref, r_ref, o_ref):
        o_ref[...] = (g_ref[...] + r_ref[...]).astype(BF16)

    return pl.pallas_call(
        body, name=name,
        grid_spec=pltpu.PrefetchScalarGridSpec(
            num_scalar_prefetch=1, grid=(NCHIP, R // tr),
            in_specs=[pl.BlockSpec((1, tr, C), lambda k, i, c_ref: (2 * k + c_ref[0], i, 0)),
                      pl.BlockSpec((1, tr, C), lambda k, i, c_ref: (k, i, 0))],
            out_specs=pl.BlockSpec((1, tr, C), lambda k, i, c_ref: (k, i, 0))),
        out_shape=jax.ShapeDtypeStruct((NCHIP, R, C), BF16),
        compiler_params=_cp("parallel", "parallel"),
    )(c_idx, g, recv)


def _sum_adam(parts, w, m, v, row0, prev, name):
    K, R, C = parts.shape
    LR = w.shape[0]
    tr = min(R, 256)
    nb = R // tr
    first = row0 // tr

    def body(p_ref, w_ref, m_ref, v_ref, *rest):
        g_ref, d_ref, m2_ref, v2_ref = rest[-4:]
        g = p_ref[0].astype(F32)
        for k in range(1, K):
            g = g + p_ref[k].astype(F32)
        delta, m2, v2 = _adamw(w_ref[...], g, m_ref[...], v_ref[...])
        g_ref[...] = g
        d_ref[...] = delta
        m2_ref[...] = m2
        v2_ref[...] = v2

    blk = pl.BlockSpec((tr, C), lambda i: (first + i, 0))
    shp = jax.ShapeDtypeStruct((LR, C), F32)
    operands = [parts, w, m, v] + (list(prev) if prev is not None else [])
    return pl.pallas_call(
        body, name=name, grid=(nb,),
        in_specs=[pl.BlockSpec((K, tr, C), lambda i: (0, i, 0)), blk, blk, blk] + [ANY] * (len(operands) - 4),
        out_specs=[blk] * 4, out_shape=[shp] * 4,
        input_output_aliases={4 + k: k for k in range(len(operands) - 4)},
        compiler_params=_cp("parallel"),
    )(*operands)


def _sum_rows(parts, name):
    K, R, C = parts.shape
    tr = min(R, 256)
    while R % tr:
        tr //= 2

    def body(p_ref, o_ref):
        g = p_ref[0]
        for k in range(1, K):
            g = g + p_ref[k]
        o_ref[...] = g

    return pl.pallas_call(
        body, name=name, grid=(R // tr,),
        in_specs=[pl.BlockSpec((K, tr, C), lambda i: (0, i, 0))],
        out_specs=pl.BlockSpec((tr, C), lambda i: (i, 0)),
        out_shape=jax.ShapeDtypeStruct((R, C), F32),
        compiler_params=_cp("parallel"),
    )(parts)


def _adam_only(g, w, m, v, name):
    R, C = g.shape
    tr = min(R, 256)
    while R % tr:
        tr //= 2

    def body(g_ref, w_ref, m_ref, v_ref, d_ref, m2_ref, v2_ref):
        delta, m2, v2 = _adamw(w_ref[...], g_ref[...], m_ref[...], v_ref[...])
        d_ref[...] = delta
        m2_ref[...] = m2
        v2_ref[...] = v2

    blk = pl.BlockSpec((tr, C), lambda i: (i, 0))
    shp = jax.ShapeDtypeStruct((R, C), F32)
    return pl.pallas_call(
        body, name=name, grid=(R // tr,), in_specs=[blk] * 4, out_specs=[blk] * 3,
        out_shape=[shp] * 3, compiler_params=_cp("parallel"),
    )(g, w, m, v)


def _mod_fwd(c_all, w_mod, b_cols, name):
    L, D, n = w_mod.shape
    B = c_all.shape[0]

    def body(c_ref, w_ref, b_ref, o_ref):
        cv = c_ref[...]
        sc = (cv * _sigmoid(cv)).astype(BF16)
        o_ref[0] = jnp.dot(sc, w_ref[0].astype(BF16), preferred_element_type=F32) + b_ref[0]

    return pl.pallas_call(
        body, name=name, grid=(L,),
        in_specs=[pl.BlockSpec((B, D), lambda l: (0, 0)),
                  pl.BlockSpec((1, D, n), lambda l: (l, 0, 0)),
                  pl.BlockSpec((1, 1, n), lambda l: (l, 0, 0))],
        out_specs=pl.BlockSpec((1, B, n), lambda l: (l, 0, 0)),
        out_shape=jax.ShapeDtypeStruct((L, B, n), F32),
        compiler_params=_cp("parallel"),
    )(c_all, w_mod, b_cols)


def _wmod_grad_adam(sc_t, dm, w, m, v, name):
    L, D, n = w.shape
    KP = sc_t.shape[1]
    tr = min(D, 512)

    def body(s_ref, dm_ref, w_ref, m_ref, v_ref, g_ref, d_ref, m2_ref, v2_ref):
        g = jnp.dot(s_ref[...], dm_ref[0], preferred_element_type=F32,
                    precision=lax.Precision.HIGHEST)
        delta, m2, v2 = _adamw(w_ref[0], g, m_ref[0], v_ref[0])
        g_ref[0] = g
        d_ref[0] = delta
        m2_ref[0] = m2
        v2_ref[0] = v2

    blk = pl.BlockSpec((1, tr, n), lambda l, i: (l, i, 0))
    shp = jax.ShapeDtypeStruct((L, D, n), F32)
    return pl.pallas_call(
        body, name=name, grid=(L, D // tr),
        in_specs=[pl.BlockSpec((tr, KP), lambda l, i: (i, 0)),
                  pl.BlockSpec((1, KP, n), lambda l, i: (l, 0, 0)), blk, blk, blk],
        out_specs=[blk] * 4, out_shape=[shp] * 4,
        compiler_params=_cp("parallel", "parallel"),
    )(sc_t, dm, w, m, v)


def _vec_spec(D):
    return pl.BlockSpec((1, D), lambda i: (0, 0))


def _pre(x, res, gate, g, scale, shift, name):
    S, D = x.shape
    tr = min(S, ROW_TILE)
    has_res = res is not None
    row = pl.BlockSpec((tr, D), lambda i: (i, 0))

    def body(*refs):
        if has_res:
            x_ref, r_ref, gate_ref, g_ref, sc_ref, sh_ref, xl_ref, h_ref = refs
            xv = x_ref[...] + gate_ref[...] * r_ref[...]
            xl_ref[...] = xv
        else:
            x_ref, g_ref, sc_ref, sh_ref, h_ref = refs
            xv = x_ref[...]
        r = lax.rsqrt(jnp.mean(xv * xv, axis=-1, keepdims=True) + EPS)
        y = (xv * r) * g_ref[...]
        h_ref[...] = (y * (1.0 + sc_ref[...]) + sh_ref[...]).astype(BF16)

    vec = _vec_spec(D)
    if has_res:
        xl, h = pl.pallas_call(
            body, name=name, grid=(S // tr,),
            in_specs=[row, row, vec, vec, vec, vec], out_specs=[row, row],
            out_shape=[jax.ShapeDtypeStruct((S, D), F32), jax.ShapeDtypeStruct((S, D), BF16)],
            compiler_params=_cp("parallel"),
        )(x, res, gate, g, scale, shift)
        return xl, h
    h = pl.pallas_call(
        body, name=name, grid=(S // tr,),
        in_specs=[row, vec, vec, vec], out_specs=row,
        out_shape=jax.ShapeDtypeStruct((S, D), BF16),
        compiler_params=_cp("parallel"),
    )(x, g, scale, shift)
    return x, h


def _pre_bwd(xl, dh, dx_in, g, scale, name, comm=None):
    S, D = xl.shape
    tr = min(S, ROW_TILE)
    nsteps = S // tr
    row = pl.BlockSpec((tr, D), lambda i: (i, 0))
    vec = _vec_spec(D)

    def body(x_ref, dh_ref, dxin_ref, g_ref, sc_ref, dx_ref, dsh_ref, dsc_ref, dg_ref, acc_sh, acc_t):
        i = pl.program_id(0)
        xv = x_ref[...]
        dh = dh_ref[...]
        r = lax.rsqrt(jnp.mean(xv * xv, axis=-1, keepdims=True) + EPS)
        xn = xv * r
        part_sh = jnp.sum(dh.reshape(tr // 8, 8, D), axis=0)
        part_t = jnp.sum((dh * xn).reshape(tr // 8, 8, D), axis=0)

        @pl.when(i == 0)
        def _():
            acc_sh[...] = part_sh
            acc_t[...] = part_t

        @pl.when(i > 0)
        def _():
            acc_sh[...] += part_sh
            acc_t[...] += part_t

        dxn = dh * (g_ref[...] * (1.0 + sc_ref[...]))
        dx_ref[...] = dxin_ref[...] + r * (dxn - xn * jnp.mean(dxn * xn, axis=-1, keepdims=True))

        @pl.when(i == nsteps - 1)
        def _():
            t = jnp.sum(acc_t[...], axis=0, keepdims=True)
            dsh_ref[...] = jnp.sum(acc_sh[...], axis=0, keepdims=True)
            dsc_ref[...] = t * g_ref[...]
            dg_ref[...] = t * (1.0 + sc_ref[...])

    v = jax.ShapeDtypeStruct((1, D), F32)
    return _hosted_call(
        body, [xl, dh, dx_in, g, scale], name=name, grid=(nsteps,),
        in_specs=[row, row, row, vec, vec], out_specs=[row, vec, vec, vec],
        out_shape=[jax.ShapeDtypeStruct((S, D), F32), v, v, v],
        scratch_shapes=[pltpu.VMEM((8, D), F32), pltpu.VMEM((8, D), F32)],
        sem=("arbitrary",), comm=comm)


def _post_bwd(dx, out, gate, name):
    S, D = dx.shape
    tr = min(S, ROW_TILE)
    nsteps = S // tr
    row = pl.BlockSpec((tr, D), lambda i: (i, 0))
    vec = _vec_spec(D)

    def body(dx_ref, o_ref, gate_ref, do_ref, dg_ref, acc):
        i = pl.program_id(0)
        dxv = dx_ref[...]
        do_ref[...] = (dxv * gate_ref[...]).astype(BF16)
        part = jnp.sum((dxv * o_ref[...]).reshape(tr // 8, 8, D), axis=0)

        @pl.when(i == 0)
        def _():
            acc[...] = part

        @pl.when(i > 0)
        def _():
            acc[...] += part

        @pl.when(i == nsteps - 1)
        def _():
            dg_ref[...] = jnp.sum(acc[...], axis=0, keepdims=True)

    return pl.pallas_call(
        body, name=name, grid=(nsteps,),
        in_specs=[row, row, vec], out_specs=[row, vec],
        out_shape=[jax.ShapeDtypeStruct((S, D), BF16), jax.ShapeDtypeStruct((1, D), F32)],
        scratch_shapes=[pltpu.VMEM((8, D), F32)],
        compiler_params=_cp("arbitrary"),
    )(dx, out, gate)


def _loss_head(x, res, gate, gf, tgt, name):
    S, D = x.shape
    tr = min(S, ROW_TILE)
    nsteps = S // tr
    row = pl.BlockSpec((tr, D), lambda i: (i, 0))
    vec = _vec_spec(D)

    def body(x_ref, r_ref, gate_ref, gf_ref, t_ref, dx_ref, loss_ref, dgf_ref, acc, lacc):
        i = pl.program_id(0)
        xv = x_ref[...] + gate_ref[...] * r_ref[...]
        r = lax.rsqrt(jnp.mean(xv * xv, axis=-1, keepdims=True) + EPS)
        xn = xv * r
        err = xn * gf_ref[...] - t_ref[...]
        row_loss = jnp.mean(err * err, axis=-1, keepdims=True)
        lpart = 0.5 * jnp.sum(row_loss, axis=0, keepdims=True)
        dy = err * (1.0 / D)
        part = jnp.sum((dy * xn).reshape(tr // 8, 8, D), axis=0)

        @pl.when(i == 0)
        def _():
            acc[...] = part
            lacc[...] = lpart

        @pl.when(i > 0)
        def _():
            acc[...] += part
            lacc[...] += lpart

        dxn = dy * gf_ref[...]
        dx_ref[...] = r * (dxn - xn * jnp.mean(dxn * xn, axis=-1, keepdims=True))

        @pl.when(i == nsteps - 1)
        def _():
            dgf_ref[...] = jnp.sum(acc[...], axis=0, keepdims=True)
            loss_ref[...] = lacc[...]

    return pl.pallas_call(
        body, name=name, grid=(nsteps,),
        in_specs=[row, row, vec, vec, row],
        out_specs=[row, pl.BlockSpec((1, 1), lambda i: (0, 0)), vec],
        out_shape=[jax.ShapeDtypeStruct((S, D), F32), jax.ShapeDtypeStruct((1, 1), F32),
                   jax.ShapeDtypeStruct((1, D), F32)],
        scratch_shapes=[pltpu.VMEM((8, D), F32), pltpu.VMEM((1, 1), F32)],
        compiler_params=_cp("arbitrary"),
    )(x, res, gate, gf, tgt)


NN = (((1,), (0,)), ((), ()))
NT = (((1,), (1,)), ((), ()))
TN = (((0,), (0,)), ((), ()))


def _mm(name, a, b, out_shape, grid, a_spec, b_spec, o_spec, dims, a2d, b2d, k_axis, sem, alias=None, comm=None):
    def body(*refs):
        a_ref, b_ref, o_ref = refs[0], refs[1], refs[-1]
        r = lax.dot_general(a_ref[...].reshape(a2d), b_ref[...].reshape(b2d), dims,
                            preferred_element_type=F32)
        r = r.reshape(o_ref.shape)
        if k_axis is None:
            o_ref[...] = r.astype(o_ref.dtype)
        else:
            k = pl.program_id(k_axis)

            @pl.when(k == 0)
            def _():
                o_ref[...] = r

            @pl.when(k > 0)
            def _():
                o_ref[...] += r

    operands, in_specs, aliases = [a, b], [a_spec, b_spec], {}
    if alias is not None:
        operands.append(alias)
        in_specs.append(ANY)
        aliases = {2: 0}
    res, extra = _hosted_call(body, operands, name=name, grid=grid, in_specs=in_specs, out_specs=o_spec,
                              out_shape=out_shape, sem=sem, aliases=aliases, comm=comm)
    return res if comm is None else (res, extra)


def _tile(n, pref):
    t = min(n, pref)
    while n % t:
        t -= 128
    return t


def _mm_nn_in(a, w, l, name, comm=None):
    M, K = a.shape
    _, _, _, n = w.shape
    tm, tn = min(M, 512), _tile(n, 1024)
    nb = n // tn
    return _mm(name, a, w, jax.ShapeDtypeStruct((M, NDEV * n), F32), (NDEV * nb, M // tm),
               pl.BlockSpec((tm, K), lambda j, i: (i, 0)),
               pl.BlockSpec((1, 1, K, tn), lambda j, i: (j // nb, l, 0, j % nb)),
               pl.BlockSpec((tm, tn), lambda j, i: (i, j)),
               NN, (tm, K), (K, tn), None, ("parallel", "parallel"), comm=comm)


def _mm_nn_out(a, w, l, name):
    M, K = a.shape
    _, _, kb, N = w.shape
    tm, tn = min(M, 512), _tile(N, 1024)
    return _mm(name, a, w, jax.ShapeDtypeStruct((M, N), F32), (N // tn, M // tm),
               pl.BlockSpec((tm, K), lambda j, i: (i, 0)),
               pl.BlockSpec((NDEV, 1, kb, tn), lambda j, i: (0, l, 0, j)),
               pl.BlockSpec((tm, tn), lambda j, i: (i, j)),
               NN, (tm, K), (K, tn), None, ("parallel", "parallel"))


def _mm_nt_in(a, w, l, name, comm=None):
    M, _ = a.shape
    _, _, K, n = w.shape
    tm, tk = min(M, 1024), _tile(K, 1024)
    gb = 2 if n <= 1024 else 1

    def body(a_ref, w_ref, o_ref):
        k = pl.program_id(2)
        r = lax.dot_general(a_ref[:, :n], w_ref[0, 0], NT, preferred_element_type=F32)
        for g in range(1, gb):
            r = r + lax.dot_general(a_ref[:, g * n:(g + 1) * n], w_ref[g, 0], NT, preferred_element_type=F32)

        @pl.when(k == 0)
        def _():
            o_ref[...] = r

        @pl.when(k > 0)
        def _():
            o_ref[...] += r

    res, extra = _hosted_call(
        body, [a, w], name=name, grid=(M // tm, K // tk, NDEV // gb),
        in_specs=[pl.BlockSpec((tm, gb * n), lambda i, j, k: (i, k)),
                  pl.BlockSpec((gb, 1, tk, n), lambda i, j, k: (k, l, j, 0))],
        out_specs=pl.BlockSpec((tm, tk), lambda i, j, k: (i, j)),
        out_shape=jax.ShapeDtypeStruct((M, K), F32),
        sem=("parallel", "parallel", "arbitrary"), comm=comm)
    return res if comm is None else (res, extra)


def _mm_nt_out(a, w, l, name):
    M, N = a.shape
    _, _, kb, _ = w.shape
    K = NDEV * kb
    tm, tk, tc = min(M, 1024), _tile(K, 1024), _tile(N, 1024)
    per = tk // kb
    return _mm(name, a, w, jax.ShapeDtypeStruct((M, K), F32), (M // tm, K // tk, N // tc),
               pl.BlockSpec((tm, tc), lambda i, j, k: (i, k)),
               pl.BlockSpec((per, 1, kb, tc), lambda i, j, k: (j, l, 0, k)),
               pl.BlockSpec((tm, tk), lambda i, j, k: (i, j)),
               NT, (tm, tc), (tk, tc), 2, ("parallel", "parallel", "arbitrary"))


def _mm_tn_in(a, b, l, L, buf, name, comm=None, part=(0, 1)):
    S, K = a.shape
    K = K // part[1]
    n = b.shape[1] // NDEV
    ts, tk, tn = min(S, 2048), _tile(K, 1024), _tile(n, 1024)
    nb = n // tn
    first = part[0] * (K // tk)
    return _mm(name, a, b, jax.ShapeDtypeStruct((NDEV, L, K, n), F32), (NDEV * nb, K // tk, S // ts),
               pl.BlockSpec((ts, tk), lambda j, i, s: (s, first + i)),
               pl.BlockSpec((ts, tn), lambda j, i, s: (s, j)),
               pl.BlockSpec((1, 1, tk, tn), lambda j, i, s: (j // nb, l, i, j % nb)),
               TN, (ts, tk), (ts, tn), 2, ("parallel", "parallel", "arbitrary"), alias=buf, comm=comm)


def _mm_tn_out(a, b, l, L, buf, name):
    S, K = a.shape
    N = b.shape[1]
    kb = K // NDEV
    ts, tk, tn = min(S, 2048), _tile(K, 1024), _tile(N, 1024)
    per = tk // kb
    return _mm(name, a, b, jax.ShapeDtypeStruct((NDEV, L, kb, N), F32), (N // tn, K // tk, S // ts),
               pl.BlockSpec((ts, tk), lambda j, i, s: (s, i)),
               pl.BlockSpec((ts, tn), lambda j, i, s: (s, j)),
               pl.BlockSpec((per, 1, kb, tn), lambda j, i, s: (i, l, 0, j)),
               TN, (ts, tk), (ts, tn), 2, ("parallel", "parallel", "arbitrary"), alias=buf)


def _attn_bias(T):
    reach = max(w // 2 for w, _ in DILATED_PATTERNS)
    hb = -(-reach // T)
    i = np.arange(T)[:, None]
    j = np.arange(T)[None, :]
    tiles = []
    for d in range(-hb, hb + 1):
        rel = j + d * T - i
        mult = np.zeros((T, T), np.float64)
        for window, dil in DILATED_PATTERNS:
            radius = window // (2 * dil)
            mult += (rel % dil == 0) & (np.abs(rel) <= radius * dil)
        tiles.append(np.where(mult > 0, np.log(np.maximum(mult, 1.0)), NEG_INF))
    return jnp.asarray(np.stack(tiles), F32)


def _rope_tables(S):
    half = HEAD_DIM // 2
    pos = jnp.arange(S, dtype=F32)
    inv = ROPE_THETA ** (-jnp.arange(half, dtype=F32) / half)
    ang = pos[:, None] * inv[None, :]
    cos, sin = jnp.cos(ang), jnp.sin(ang)
    return jnp.concatenate([cos, cos], axis=-1), jnp.concatenate([-sin, sin], axis=-1)


def _rope_apply(t, cosf, sinf, heads, sign):
    outs = []
    for hh in range(heads):
        th = t[:, hh * HEAD_DIM:(hh + 1) * HEAD_DIM]
        outs.append(th * cosf + sign * (pltpu.roll(th, HEAD_DIM // 2, 1) * sinf))
    return outs


def _rope_qkv(proj, cosf, sinf, W, name):
    S = proj.shape[0]
    tr = min(S, ROW_TILE)
    heads = W // HEAD_DIM

    def body(q_ref, k_ref, v_ref, c_ref, s_ref, qo_ref, ko_ref, vo_ref):
        cosf_v, sinf_v = c_ref[...], s_ref[...]
        for src, dst, mult in ((q_ref, qo_ref, HEAD_DIM ** -0.5), (k_ref, ko_ref, 1.0)):
            for hh, val in enumerate(_rope_apply(src[...], cosf_v, sinf_v, heads, 1.0)):
                dst[:, hh * HEAD_DIM:(hh + 1) * HEAD_DIM] = (val * mult).astype(BF16)
        vo_ref[...] = v_ref[...].astype(BF16)

    piece = lambda p: pl.BlockSpec((tr, W), lambda i: (i, p))
    tab = pl.BlockSpec((tr, HEAD_DIM), lambda i: (i, 0))
    out = pl.BlockSpec((tr, W), lambda i: (i, 0))
    shp = jax.ShapeDtypeStruct((S, W), BF16)
    return pl.pallas_call(
        body, name=name, grid=(S // tr,),
        in_specs=[piece(0), piece(1), piece(2), tab, tab], out_specs=[out] * 3, out_shape=[shp] * 3,
        compiler_params=_cp("parallel"),
    )(proj, proj, proj, cosf, sinf)


def _attn_fwd(q, k, v, bias, name, comm=None):
    S, W = q.shape
    H = W // HEAD_DIM
    nd, T, _ = bias.shape
    hb, nq = nd // 2, S // T
    scale = HEAD_DIM ** -0.5
    hp = min(H, HEADS_PER_STEP)
    rc = min(T, ATTN_ROW_CHUNK)
    wp = hp * HEAD_DIM

    def body(q_ref, k_ref, v_ref, b_ref, o_ref, lse_ref, m_s, l_s, acc_s):
        i, d = pl.program_id(1), pl.program_id(2)
        j = i + d - hb

        @pl.when(d == 0)
        def _():
            m_s[...] = jnp.full(m_s.shape, -jnp.inf, F32)
            l_s[...] = jnp.zeros(l_s.shape, F32)
            acc_s[...] = jnp.zeros(acc_s.shape, F32)

        @pl.when((j >= 0) & (j < nq))
        def _():
            items = [(hh, c) for hh in range(hp) for c in range(T // rc)]

            def scores(item):
                hh, c = item
                cols, rows = slice(hh * HEAD_DIM, (hh + 1) * HEAD_DIM), slice(c * rc, (c + 1) * rc)
                return (lax.dot_general(q_ref[rows, cols], k_ref[:, cols], NT, preferred_element_type=F32)
                        + b_ref[d, rows, :])

            def weighted_values(item, p, alpha):
                hh, c = item
                cols, rows = slice(hh * HEAD_DIM, (hh + 1) * HEAD_DIM), slice(c * rc, (c + 1) * rc)
                acc_s[rows, cols] = alpha * acc_s[rows, cols] + jnp.dot(p, v_ref[:, cols],
                                                                        preferred_element_type=F32)

            s_next, pending = scores(items[0]), None
            for n, (hh, c) in enumerate(items):
                rows = slice(c * rc, (c + 1) * rc)
                s = s_next
                if n + 1 < len(items):
                    s_next = scores(items[n + 1])
                if pending is not None:
                    weighted_values(*pending)
                parts = [s[:, t * LANES:(t + 1) * LANES] for t in range(T // LANES)]
                m_old = m_s[hh, rows, :]
                m_cur = jnp.max(functools.reduce(jnp.maximum, parts), axis=1, keepdims=True)
                m_new = jnp.maximum(m_old, m_cur)
                alpha = jnp.exp(m_old - m_new)
                ps = [jnp.exp(part - m_new) for part in parts]
                l_s[hh, rows, :] = alpha * l_s[hh, rows, :] + functools.reduce(jnp.add, ps)
                m_s[hh, rows, :] = m_new
                pending = ((hh, c), jnp.concatenate(ps, axis=1).astype(BF16), alpha)
            weighted_values(*pending)

        @pl.when(d == nd - 1)
        def _():
            for hh in range(hp):
                cols = slice(hh * HEAD_DIM, (hh + 1) * HEAD_DIM)
                l = jnp.sum(l_s[hh], axis=1, keepdims=True)
                o_ref[:, cols] = acc_s[:, cols] / l
                lse_ref[hh] = m_s[hh][:, :1] + jnp.log(l)

    kv = pl.BlockSpec((T, wp), lambda h, i, d: (jnp.clip(i + d - hb, 0, nq - 1), h))
    return _hosted_call(
        body, [q, k, v, bias], name=name, grid=(H // hp, nq, nd),
        in_specs=[pl.BlockSpec((T, wp), lambda h, i, d: (i, h)), kv, kv,
                  pl.BlockSpec((nd, T, T), lambda h, i, d: (0, 0, 0))],
        out_specs=[pl.BlockSpec((T, wp), lambda h, i, d: (i, h)),
                   pl.BlockSpec((hp, T, 1), lambda h, i, d: (h, i, 0))],
        out_shape=[jax.ShapeDtypeStruct((S, W), F32), jax.ShapeDtypeStruct((H, S, 1), F32)],
        scratch_shapes=[pltpu.VMEM((hp, T, LANES), F32), pltpu.VMEM((hp, T, LANES), F32),
                        pltpu.VMEM((T, wp), F32)],
        sem=("parallel", "parallel", "arbitrary"), comm=comm)


def _attn_bwd(q, k, v, do, lse, delta, bias, name, comm=None):
    S, W = q.shape
    H = W // HEAD_DIM
    nd, T, _ = bias.shape
    hb, nq = nd // 2, S // T
    scale = HEAD_DIM ** -0.5
    hp = min(H, HEADS_PER_STEP)
    rc = min(T, ATTN_ROW_CHUNK)
    wp = hp * HEAD_DIM

    def body(q_ref, do_ref, lse_ref, dl_ref, k_ref, v_ref, b_ref, dq_ref, dk_ref, dv_ref):
        j, d = pl.program_id(1), pl.program_id(2)
        i = j + d - hb

        @pl.when((j == 0) & (d == 0))
        def _():
            dq_ref[...] = jnp.zeros(dq_ref.shape, F32)

        @pl.when(d == 0)
        def _():
            dk_ref[...] = jnp.zeros(dk_ref.shape, F32)
            dv_ref[...] = jnp.zeros(dv_ref.shape, F32)

        @pl.when((i >= 0) & (i < nq))
        def _():
            items = [(hh, c) for hh in range(hp) for c in range(T // rc)]

            def slices(item):
                hh, c = item
                return slice(hh * HEAD_DIM, (hh + 1) * HEAD_DIM), slice(c * rc, (c + 1) * rc)

            def products(item):
                cols, rows = slices(item)
                s = (lax.dot_general(q_ref[rows, cols], k_ref[:, cols], NT, preferred_element_type=F32)
                     + b_ref[nd - 1 - d, rows, :])
                dp = lax.dot_general(do_ref[rows, cols], v_ref[:, cols], NT, preferred_element_type=F32)
                return s, dp

            def gradients(item, p, ds):
                cols, rows = slices(item)
                dv_ref[:, cols] += lax.dot_general(p, do_ref[rows, cols], TN, preferred_element_type=F32)
                dk_ref[:, cols] += lax.dot_general(ds, q_ref[rows, cols], TN, preferred_element_type=F32)
                q_rows = pl.ds(pl.multiple_of(i * T + item[1] * rc, rc), rc)
                dq_ref[q_rows, cols] += jnp.dot(ds, k_ref[:, cols], preferred_element_type=F32) * scale

            nxt, pending = products(items[0]), None
            for n, item in enumerate(items):
                s, dp = nxt
                if n + 1 < len(items):
                    nxt = products(items[n + 1])
                if pending is not None:
                    gradients(*pending)
                _, rows = slices(item)
                p = jnp.exp(s - lse_ref[item[0], rows, :])
                ds = p * (dp - dl_ref[item[0], rows, :])
                pending = (item, p.astype(BF16), ds.astype(BF16))
            gradients(*pending)

    qi = lambda h, j, d: (jnp.clip(j + d - hb, 0, nq - 1), h)
    qs = pl.BlockSpec((T, wp), qi)
    col = pl.BlockSpec((hp, T, 1), lambda h, j, d: (h, jnp.clip(j + d - hb, 0, nq - 1), 0))
    kv = pl.BlockSpec((T, wp), lambda h, j, d: (j, h))
    shp = jax.ShapeDtypeStruct((S, W), F32)
    return _hosted_call(
        body, [q, do, lse, delta, k, v, bias], name=name, grid=(H // hp, nq, nd),
        in_specs=[qs, qs, col, col, kv, kv, pl.BlockSpec((nd, T, T), lambda h, j, d: (0, 0, 0))],
        out_specs=[pl.BlockSpec((S, wp), lambda h, j, d: (0, h)), kv, kv],
        out_shape=[shp, shp, shp],
        sem=("parallel", "arbitrary", "arbitrary"), comm=comm)


def _halo_specs(S, tr, W, piece):
    per, last = tr // 8, S // 8 - 1
    prev = pl.BlockSpec((8, W), lambda i: (jnp.maximum(i * per - 1, 0), piece))
    nxt = pl.BlockSpec((8, W), lambda i: (jnp.minimum((i + 1) * per, last), piece))
    return prev, nxt


def _shifted(t, before, after, tr):
    rows = lax.broadcasted_iota(jnp.int32, (tr, 1), 0)
    prev = jnp.where(rows == 0, before, pltpu.roll(t, 1, 0))
    nxt = jnp.where(rows == tr - 1, after, pltpu.roll(t, tr - 1, 0))
    return prev, nxt


def _ab_mix(attn, proj, conv_w, W, name):
    S = attn.shape[0]
    tr = min(S, ROW_TILE)
    nsteps = S // tr

    def body(a_ref, za_ref, ub_ref, gb_ref, gc_ref, zb_ref, ubp, ubn, gcp, gcn, w_ref, y_ref):
        i = pl.program_id(0)
        t = gc_ref[...] * ub_ref[...]
        before = jnp.where(i == 0, 0.0, (gcp[...] * ubp[...])[7:8, :])
        after = jnp.where(i == nsteps - 1, 0.0, (gcn[...] * ubn[...])[0:1, :])
        t_prev, t_next = _shifted(t, before, after, tr)
        w = w_ref[...]
        cv = w[0:1, :] * t_prev + w[1:2, :] * t + w[2:3, :] * t_next
        silu_a, _ = _silu_and_grad(za_ref[...])
        silu_b, _ = _silu_and_grad(zb_ref[...])
        y_ref[:, :W] = (a_ref[...] * silu_a).astype(BF16)
        y_ref[:, W:] = (gb_ref[...] * cv * silu_b).astype(BF16)

    piece = lambda p: pl.BlockSpec((tr, W), lambda i: (i, p))
    ubp, ubn = _halo_specs(S, tr, W, 4)
    gcp, gcn = _halo_specs(S, tr, W, 6)
    return pl.pallas_call(
        body, name=name, grid=(nsteps,),
        in_specs=[pl.BlockSpec((tr, W), lambda i: (i, 0)), piece(3), piece(4), piece(5), piece(6), piece(7),
                  ubp, ubn, gcp, gcn, pl.BlockSpec((3, W), lambda i: (0, 0))],
        out_specs=pl.BlockSpec((tr, 2 * W), lambda i: (i, 0)),
        out_shape=jax.ShapeDtypeStruct((S, 2 * W), BF16),
        compiler_params=_cp("parallel"),
    )(attn, proj, proj, proj, proj, proj, proj, proj, proj, proj, conv_w)


def _dattn_prep(dy, proj, attn, W, name):
    S = attn.shape[0]
    H = W // HEAD_DIM
    tr = min(S, ROW_TILE)

    def body(dy_ref, za_ref, a_ref, do_ref, dl_ref):
        silu_a, _ = _silu_and_grad(za_ref[...])
        do = dy_ref[...] * silu_a
        do_ref[...] = do.astype(BF16)
        prod = do * a_ref[...]
        for hh in range(H):
            dl_ref[hh] = jnp.sum(prod[:, hh * HEAD_DIM:(hh + 1) * HEAD_DIM], axis=1, keepdims=True)

    row = pl.BlockSpec((tr, W), lambda i: (i, 0))
    return pl.pallas_call(
        body, name=name, grid=(S // tr,),
        in_specs=[row, pl.BlockSpec((tr, W), lambda i: (i, 3)), row],
        out_specs=[row, pl.BlockSpec((H, tr, 1), lambda i: (0, i, 0))],
        out_shape=[jax.ShapeDtypeStruct((S, W), BF16), jax.ShapeDtypeStruct((H, S, 1), F32)],
        compiler_params=_cp("parallel"),
    )(dy, proj, attn)


def _ab_bwd(dy, attn, proj, dqr, dkr, dv, cosf, sinf, conv_w, W, name):
    S = attn.shape[0]
    tr = min(S, ROW_TILE // 2)
    nsteps = S // tr
    heads = W // HEAD_DIM

    def body(dya_ref, dyb_ref, a_ref, za_ref, ub_ref, gb_ref, gc_ref, zb_ref, dq_ref, dk_ref, dv_ref,
             c_ref, s_ref, w_ref, dybp, dybn, gbp, gbn, zbp, zbn, ubp, ubn, gcp, gcn,
             dp_ref, dw_ref, acc):
        i = pl.program_id(0)
        first, last = i == 0, i == nsteps - 1
        w = w_ref[...]
        w0, w1, w2 = w[0:1, :], w[1:2, :], w[2:3, :]
        ub, gb, gc, zb = ub_ref[...], gb_ref[...], gc_ref[...], zb_ref[...]
        dyb = dyb_ref[...]
        silu_a, dsilu_a = _silu_and_grad(za_ref[...])
        silu_b, dsilu_b = _silu_and_grad(zb)
        t = gc * ub
        t_prev, t_next = _shifted(t, jnp.where(first, 0.0, (gcp[...] * ubp[...])[7:8, :]),
                                  jnp.where(last, 0.0, (gcn[...] * ubn[...])[0:1, :]), tr)
        cv = w0 * t_prev + w1 * t + w2 * t_next
        dcv = dyb * gb * silu_b
        halo_p = dybp[...] * gbp[...] * _silu_and_grad(zbp[...])[0]
        halo_n = dybn[...] * gbn[...] * _silu_and_grad(zbn[...])[0]
        dcv_prev, dcv_next = _shifted(dcv, jnp.where(first, 0.0, halo_p[7:8, :]),
                                      jnp.where(last, 0.0, halo_n[0:1, :]), tr)
        dt = w0 * dcv_next + w1 * dcv + w2 * dcv_prev
        cosf_v, sinf_v = c_ref[...], s_ref[...]
        for src, base in ((dq_ref, 0), (dk_ref, W)):
            for hh, val in enumerate(_rope_apply(src[...], cosf_v, sinf_v, heads, -1.0)):
                dp_ref[:, base + hh * HEAD_DIM:base + (hh + 1) * HEAD_DIM] = val.astype(BF16)
        dp_ref[:, 2 * W:3 * W] = dv_ref[...].astype(BF16)
        dp_ref[:, 3 * W:4 * W] = (dya_ref[...] * a_ref[...] * dsilu_a).astype(BF16)
        dp_ref[:, 4 * W:5 * W] = (dt * gc).astype(BF16)
        dp_ref[:, 5 * W:6 * W] = (dyb * cv * silu_b).astype(BF16)
        dp_ref[:, 6 * W:7 * W] = (dt * ub).astype(BF16)
        dp_ref[:, 7 * W:8 * W] = (dyb * gb * cv * dsilu_b).astype(BF16)
        tap = lax.broadcasted_iota(jnp.int32, (8, 1), 0)
        part = (jnp.where(tap == 0, jnp.sum(dcv * t_prev, axis=0, keepdims=True), 0.0)
                + jnp.where(tap == 1, jnp.sum(dcv * t, axis=0, keepdims=True), 0.0)
                + jnp.where(tap == 2, jnp.sum(dcv * t_next, axis=0, keepdims=True), 0.0))

        @pl.when(first)
        def _():
            acc[...] = part

        @pl.when(i > 0)
        def _():
            acc[...] += part

        @pl.when(last)
        def _():
            dw_ref[...] = acc[...]

    row = pl.BlockSpec((tr, W), lambda i: (i, 0))
    piece = lambda p: pl.BlockSpec((tr, W), lambda i: (i, p))
    tab = pl.BlockSpec((tr, HEAD_DIM), lambda i: (i, 0))
    dybp, dybn = _halo_specs(S, tr, W, 1)
    gbp, gbn = _halo_specs(S, tr, W, 5)
    zbp, zbn = _halo_specs(S, tr, W, 7)
    ubp, ubn = _halo_specs(S, tr, W, 4)
    gcp, gcn = _halo_specs(S, tr, W, 6)
    return pl.pallas_call(
        body, name=name, grid=(nsteps,),
        in_specs=[piece(0), piece(1), row, piece(3), piece(4), piece(5), piece(6), piece(7), row, row, row,
                  tab, tab, pl.BlockSpec((3, W), lambda i: (0, 0)),
                  dybp, dybn, gbp, gbn, zbp, zbn, ubp, ubn, gcp, gcn],
        out_specs=[pl.BlockSpec((tr, 8 * W), lambda i: (i, 0)), pl.BlockSpec((8, W), lambda i: (0, 0))],
        out_shape=[jax.ShapeDtypeStruct((S, 8 * W), BF16), jax.ShapeDtypeStruct((8, W), F32)],
        scratch_shapes=[pltpu.VMEM((8, W), F32)],
        compiler_params=_cp("arbitrary"),
    )(dy, dy, attn, proj, proj, proj, proj, proj, dqr, dkr, dv, cosf, sinf, conv_w,
      dy, dy, proj, proj, proj, proj, proj, proj, proj, proj)


def _sgu_centre(p_ref, vc_s, dvg_s, Dc):
    gw = Dc // C_GROUPS
    total = None
    for g in range(C_GROUPS):
        cs = slice(g * gw, (g + 1) * gw)
        vg, dvg = _gelu_and_grad(p_ref[:, Dc + g * gw:Dc + (g + 1) * gw])
        vc_s[:, cs] = vg
        if dvg_s is not None:
            dvg_s[:, cs] = dvg
        total = vg if total is None else total + vg
    mu = jnp.sum(total, axis=1, keepdims=True) * (1.0 / Dc)
    total = None
    for g in range(C_GROUPS):
        cs = slice(g * gw, (g + 1) * gw)
        vc = vc_s[:, cs] - mu
        vc_s[:, cs] = vc
        total = vc * vc if total is None else total + vc * vc
    return lax.rsqrt(jnp.sum(total, axis=1, keepdims=True) * (1.0 / Dc) + EPS)


def _sgu_fwd(proj, ln_g, ln_b, w_s, b_st, name, comm=None):
    S, Dc3 = proj.shape
    Dc = Dc3 // 3
    gw = Dc // C_GROUPS
    vec = pl.BlockSpec((1, Dc), lambda i: (0, 0))

    def body(p_ref, lng_ref, lnb_ref, ws_ref, bst_ref, y_ref, vc_s):
        rstd = _sgu_centre(p_ref, vc_s, None, Dc)
        bst = bst_ref[...]
        for g in range(C_GROUPS):
            cs = slice(g * gw, (g + 1) * gw)
            vn = (vc_s[:, cs] * rstd * lng_ref[:, cs] + lnb_ref[:, cs]).astype(BF16)
            mixed = jnp.dot(ws_ref[g].astype(BF16), vn, preferred_element_type=F32) + bst[:, g:g + 1]
            u, _ = _gelu_and_grad(p_ref[:, cs])
            sz, _ = _silu_and_grad(p_ref[:, 2 * Dc + g * gw:2 * Dc + (g + 1) * gw])
            y_ref[:, cs] = (u * mixed * sz).astype(BF16)

    return _hosted_call(
        body, [proj, ln_g, ln_b, w_s, b_st], name=name, grid=(S // C_CHUNK,),
        in_specs=[pl.BlockSpec((C_CHUNK, Dc3), lambda i: (i, 0)), vec, vec,
                  pl.BlockSpec((C_GROUPS, C_CHUNK, C_CHUNK), lambda i: (0, 0, 0)),
                  pl.BlockSpec((C_CHUNK, C_GROUPS), lambda i: (0, 0))],
        out_specs=pl.BlockSpec((C_CHUNK, Dc), lambda i: (i, 0)),
        out_shape=jax.ShapeDtypeStruct((S, Dc), BF16),
        scratch_shapes=[pltpu.VMEM((C_CHUNK, Dc), F32)],
        sem=("parallel",), comm=comm)


def _sgu_bwd(proj, dy, ln_g, ln_b, w_s, w_st, b_st, name):
    S, Dc3 = proj.shape
    Dc = Dc3 // 3
    gw = Dc // C_GROUPS
    nsteps = S // C_CHUNK
    vec = pl.BlockSpec((1, Dc), lambda i: (0, 0))
    wspec = pl.BlockSpec((C_GROUPS, C_CHUNK, C_CHUNK), lambda i: (0, 0, 0))

    def body(p_ref, dy_ref, lng_ref, lnb_ref, ws_ref, wst_ref, bst_ref,
             dp_ref, dws_ref, dbs_ref, dlg_ref, dlb_ref, acc_w, acc_b, acc_g, acc_lb, vc_s, dvg_s, dvh_s):
        i = pl.program_id(0)

        @pl.when(i == 0)
        def _():
            acc_w[...] = jnp.zeros(acc_w.shape, F32)
            acc_b[...] = jnp.zeros(acc_b.shape, F32)
            acc_g[...] = jnp.zeros(acc_g.shape, F32)
            acc_lb[...] = jnp.zeros(acc_lb.shape, F32)

        rstd = _sgu_centre(p_ref, vc_s, dvg_s, Dc)
        bst = bst_ref[...]
        octets = lambda t: jnp.sum(t.reshape(C_CHUNK // 8, 8, gw), axis=0)
        t1, t2 = None, None
        for g in range(C_GROUPS):
            cs = slice(g * gw, (g + 1) * gw)
            zs = slice(2 * Dc + g * gw, 2 * Dc + (g + 1) * gw)
            vhat = vc_s[:, cs] * rstd
            vn = (vhat * lng_ref[:, cs] + lnb_ref[:, cs]).astype(BF16)
            mixed = jnp.dot(ws_ref[g].astype(BF16), vn, preferred_element_type=F32) + bst[:, g:g + 1]
            u, du = _gelu_and_grad(p_ref[:, cs])
            sz, dsz = _silu_and_grad(p_ref[:, zs])
            dy = dy_ref[:, cs]
            dmixed = dy * u * sz
            dmb = dmixed.astype(BF16)
            acc_w[g] += lax.dot_general(dmb, vn, NT, preferred_element_type=F32)
            acc_b[g] += dmixed
            dvn = jnp.dot(wst_ref[g].astype(BF16), dmb, preferred_element_type=F32)
            acc_g[:, cs] += octets(dvn * vhat)
            acc_lb[:, cs] += octets(dvn)
            dvh = dvn * lng_ref[:, cs]
            dvh_s[:, cs] = dvh
            t1 = dvh if t1 is None else t1 + dvh
            t2 = dvh * vhat if t2 is None else t2 + dvh * vhat
            dp_ref[:, cs] = (dy * mixed * sz * du).astype(BF16)
            dp_ref[:, zs] = (dy * u * mixed * dsz).astype(BF16)
        m1 = jnp.sum(t1, axis=1, keepdims=True) * (1.0 / Dc)
        m2 = jnp.sum(t2, axis=1, keepdims=True) * (1.0 / Dc)
        for g in range(C_GROUPS):
            cs = slice(g * gw, (g + 1) * gw)
            dvgelu = rstd * (dvh_s[:, cs] - m1 - (vc_s[:, cs] * rstd) * m2)
            dp_ref[:, Dc + g * gw:Dc + (g + 1) * gw] = (dvgelu * dvg_s[:, cs]).astype(BF16)

        @pl.when(i == nsteps - 1)
        def _():
            dws_ref[...] = acc_w[...]
            for g in range(C_GROUPS):
                dbs_ref[g] = jnp.sum(acc_b[g], axis=1, keepdims=True)
            dlg_ref[...] = jnp.sum(acc_g[...], axis=0, keepdims=True)
            dlb_ref[...] = jnp.sum(acc_lb[...], axis=0, keepdims=True)

    v = jax.ShapeDtypeStruct((1, Dc), F32)
    return pl.pallas_call(
        body, name=name, grid=(nsteps,),
        in_specs=[pl.BlockSpec((C_CHUNK, Dc3), lambda i: (i, 0)), pl.BlockSpec((C_CHUNK, Dc), lambda i: (i, 0)),
                  vec, vec, wspec, wspec, pl.BlockSpec((C_CHUNK, C_GROUPS), lambda i: (0, 0))],
        out_specs=[pl.BlockSpec((C_CHUNK, Dc3), lambda i: (i, 0)), wspec,
                   pl.BlockSpec((C_GROUPS, C_CHUNK, 1), lambda i: (0, 0, 0)), vec, vec],
        out_shape=[jax.ShapeDtypeStruct((S, Dc3), BF16),
                   jax.ShapeDtypeStruct((C_GROUPS, C_CHUNK, C_CHUNK), F32),
                   jax.ShapeDtypeStruct((C_GROUPS, C_CHUNK, 1), F32), v, v],
        scratch_shapes=[pltpu.VMEM((C_GROUPS, C_CHUNK, C_CHUNK), F32), pltpu.VMEM((C_GROUPS, C_CHUNK, gw), F32),
                        pltpu.VMEM((8, Dc), F32), pltpu.VMEM((8, Dc), F32)]
                       + [pltpu.VMEM((C_CHUNK, Dc), F32)] * 3,
        compiler_params=_cp("arbitrary"),
    )(proj, dy, ln_g, ln_b, w_s, w_st, b_st)


PACK_COLS = 1024
PACK_ROWS = 64


def _pack(vectors, rows=PACK_ROWS):
    flat = jnp.concatenate([v.reshape(-1) for v in vectors])
    pad = (-flat.shape[0]) % (PACK_COLS * rows)
    return jnp.pad(flat, (0, pad)).reshape(-1, PACK_COLS)


def _unshard(g, off, shape):
    L, rest = shape[0], shape[1:]
    size = int(np.prod(shape))
    piece = g[:, off:off + size].reshape((NDEV,) + tuple(shape))
    nd = piece.ndim
    perm = tuple(range(1, nd - 1)) + (0, nd - 1)
    full = jnp.transpose(piece, perm)
    return full.reshape(tuple(shape[:-1]) + (NDEV * shape[-1],)), off + size


def kernel(x, c, ab_norm_g, ab_w_mod, ab_b_mod, ab_w_in, ab_conv_w, ab_w_out, sg_norm_g, sg_w_mod, sg_b_mod, sg_w_in, sg_ln_g, sg_ln_b, sg_w_s, sg_b_s, sg_w_out, final_norm_g, loss_target, m_ab_norm_g, m_ab_w_mod, m_ab_b_mod, m_ab_w_in, m_ab_conv_w, m_ab_w_out, m_sg_norm_g, m_sg_w_mod, m_sg_b_mod, m_sg_w_in, m_sg_ln_g, m_sg_ln_b, m_sg_w_s, m_sg_b_s, m_sg_w_out, m_final_norm_g, v_ab_norm_g, v_ab_w_mod, v_ab_b_mod, v_ab_w_in, v_ab_conv_w, v_ab_w_out, v_sg_norm_g, v_sg_w_mod, v_sg_b_mod, v_sg_w_in, v_sg_ln_g, v_sg_ln_b, v_sg_w_s, v_sg_b_s, v_sg_w_out, v_final_norm_g):
    _, S, D = x.shape
    L = ab_norm_g.shape[0]
    W = ab_conv_w.shape[2] * NDEV
    n_ab, n_sg = ab_w_in.shape[2], sg_w_in.shape[2]
    n_mod = ab_w_mod.shape[2]
    kb = ab_w_out.shape[1]
    xi, yi, ci = _position()
    dev = 4 * xi + 2 * yi + ci
    x2, tgt = x.reshape(S, D), loss_target.reshape(S, D)

    small = [c, ab_conv_w, sg_norm_g, sg_ln_g, sg_ln_b]
    (g1,) = _comm_only(_GatherDirect([_pack(small, 8)]), "ag_small")
    g1 = g1.reshape(NDEV, -1)
    c_all = g1[:, :D]
    off = D
    conv_full, off = _unshard(g1, off, ab_conv_w.shape)
    sg_norm_full, off = _unshard(g1, off, sg_norm_g.shape)
    ln_g_full, off = _unshard(g1, off, sg_ln_g.shape)
    ln_b_full, off = _unshard(g1, off, sg_ln_b.shape)

    ab_b_cols = lax.dynamic_slice_in_dim(ab_b_mod, dev * n_mod, n_mod, axis=1)
    m_ab = _mod_fwd(c_all, ab_w_mod, ab_b_cols.reshape(L, 1, n_mod), "mod_fwd_ab")
    m_sg = _mod_fwd(c_all, sg_w_mod, sg_b_mod.reshape(L, 1, n_mod), "mod_fwd_sg")
    m_part = jnp.stack([m_ab, m_sg]).transpose(2, 0, 1, 3).reshape(NDEV, 2 * L * n_mod)
    (g2,) = _comm_only(_GatherDirect([m_part]), "ag_mod")
    mine = lax.dynamic_index_in_dim(g2, dev, axis=1, keepdims=False)
    mods = mine.reshape(NDEV, 2, L, n_mod).transpose(1, 2, 0, 3).reshape(2, L, 3 * D)

    def mod_of(kind, i):
        m = mods[kind, i]
        return m[:D].reshape(1, D), m[D:2 * D].reshape(1, D), m[2 * D:].reshape(1, D)

    big_w = [[(ab_w_in, m_ab_w_in, v_ab_w_in), (ab_w_out, m_ab_w_out, v_ab_w_out)],
             [(sg_w_in, m_sg_w_in, v_sg_w_in), (sg_w_out, m_sg_w_out, v_sg_w_out)]]
    big_names = [["ab_w_in", "ab_w_out"], ["sg_w_in", "sg_w_out"]]
    n_layers = 2 * L
    shards = [[big_w[layer % 2][k][0][layer // 2].astype(BF16) for k in range(2)] for layer in range(n_layers)]
    gathered = {}

    def gather_of(keys):
        keys = [key for key in keys if key[0] < n_layers]
        return keys, (_Gather([shards[layer][k] for layer, k in keys]) if keys else None)

    def keep_gathered(keys, res):
        for (layer, k), g in zip(keys, res):
            gathered[(layer, k)] = g.reshape((NDEV, 1, D, g.shape[-1]) if k == 0 else (NDEV, 1, kb, D))

    keys, comm = gather_of([(0, 0)])
    keep_gathered(keys, _comm_only(comm, "ag_w_in_layer0"))

    cosf, sinf = _rope_tables(S)
    T = min(S, ATTN_TILE)
    bias = _attn_bias(T)
    norm_g = [ab_norm_g, sg_norm_full]
    w_s_t = jnp.swapaxes(sg_w_s, -1, -2)
    b_s_t = jnp.swapaxes(sg_b_s, -1, -2)

    saved = []
    x_cur, res, gate_prev = x2, None, None
    for layer in range(2 * L):
        kind, i = layer % 2, layer // 2
        tag = f"{'ab' if kind == 0 else 'sg'}{i}"
        shift, scale, gate = mod_of(kind, i)
        g = norm_g[kind][i].reshape(1, D)
        xl, h = _pre(x_cur, res, gate_prev, g, scale, shift, f"pre_{tag}")
        keys, comm = gather_of([(layer + 1, 0)])
        if comm is None:
            proj = _mm_nn_in(h, gathered[(layer, 0)], 0, f"proj_{tag}")
        else:
            proj, got = _mm_nn_in(h, gathered[(layer, 0)], 0, f"proj_{tag}", comm)
            keep_gathered(keys, got)
        rec = dict(xl=xl, h=h, proj=proj, g=g, scale=scale, gate=gate)
        keys, comm = gather_of(([(0, 1)] if layer == 0 else []) + [(layer + 1, 1)])
        if kind == 0:
            qr, kr, vb = _rope_qkv(proj, cosf, sinf, W, f"rope_{tag}")
            (attn, lse), got = _attn_fwd(qr, kr, vb, bias, f"attn_{tag}", comm)
            y = _ab_mix(attn, proj, conv_full[i], W, f"mix_{tag}")
            rec.update(qr=qr, kr=kr, vb=vb, attn=attn, lse=lse)
        else:
            y, got = _sgu_fwd(proj, ln_g_full[i].reshape(1, D), ln_b_full[i].reshape(1, D), sg_w_s[i], b_s_t[i],
                              f"sgu_{tag}", comm)
        keep_gathered(keys, got)
        out = _mm_nn_out(y, gathered[(layer, 1)], 0, f"out_{tag}")
        rec.update(y=y, out=out)
        saved.append(rec)
        x_cur, res, gate_prev = xl, out, gate

    dx, loss_part, d_final_g = _loss_head(x_cur, res, gate_prev, final_norm_g.reshape(1, D), tgt, "loss_head")
    loss = lax.psum(loss_part[0, 0], ("x", "y", "c"))

    c_idx = ci.reshape(1).astype(jnp.int32)
    big_res = {}
    pending = None

    def sum_and_update(done, k, from_chips, part=(0, 1)):
        nm = big_names[done % 2][k]
        w, m, v = big_w[done % 2][k]
        flat = lambda a: a.reshape(L * a.shape[1], a.shape[2])
        row0 = (done // 2) * w.shape[1] + part[0] * (w.shape[1] // part[1])
        big_res[nm] = _sum_adam(from_chips, flat(w), flat(m), flat(v), row0, big_res.get(nm),
                                f"adam_{nm}{done // 2}_{part[0]}")

    def finish_layer(done, from_chips):
        for k in range(2):
            sum_and_update(done, k, from_chips[k])

    dm = [[None] * L, [None] * L]
    d_norm = [[None] * L, [None] * L]
    d_conv, d_lng, d_lnb, d_ws, d_bs = [None] * L, [None] * L, [None] * L, [None] * L, [None] * L
    for layer in reversed(range(2 * L)):
        kind, i = layer % 2, layer // 2
        tag = f"{'ab' if kind == 0 else 'sg'}{i}"
        rec = saved[layer]
        w_in_l, w_out_l = gathered[(layer, 0)], gathered[(layer, 1)]
        dout, dgate = _post_bwd(dx, rec["out"], rec["gate"], f"post_bwd_{tag}")
        dy = _mm_nt_out(dout, w_out_l, 0, f"dy_{tag}")
        dwo = _mm_tn_out(rec["y"], dout, 0, 1, None, f"dwout_{tag}")
        if kind == 0:
            do, delta = _dattn_prep(dy, rec["proj"], rec["attn"], W, f"dattn_{tag}")
            comm = _ToChips(pending[1]) if pending else None
            (dqr, dkr, dvv), got = _attn_bwd(rec["qr"], rec["kr"], rec["vb"], do, rec["lse"], delta, bias,
                                             f"attn_bwd_{tag}", comm)
            if pending:
                finish_layer(pending[0], got)
            dproj, dcw = _ab_bwd(dy, rec["attn"], rec["proj"], dqr, dkr, dvv, cosf, sinf, conv_full[i], W,
                                 f"mix_bwd_{tag}")
            d_conv[i] = dcw[:3]
            earlier = None
        else:
            dproj, d_ws[i], dbs, d_lng[i], d_lnb[i] = _sgu_bwd(
                rec["proj"], dy, ln_g_full[i].reshape(1, D), ln_b_full[i].reshape(1, D),
                sg_w_s[i], w_s_t[i], b_s_t[i], f"sgu_bwd_{tag}")
            d_bs[i] = dbs.reshape(C_GROUPS, C_CHUNK)
            earlier = pending
        if layer == 0:
            h_l = rec["h"]
            dwi_a = _mm_tn_in(h_l, dproj, 0, 1, None, f"dwin_a_{tag}", part=(0, 2))
            g_a = [dwi_a.reshape(NDEV, D // 2, -1), dwo.reshape(NDEV, kb, D)]
            dwi_b, sib_a = _mm_tn_in(h_l, dproj, 0, 1, None, f"dwin_b_{tag}", _ToSibling(g_a), part=(1, 2))
            p_a = [_add_sibling(g, r, c_idx, f"rs_add_a{k}_{tag}") for k, (g, r) in enumerate(zip(g_a, sib_a))]
            g_b = [dwi_b.reshape(NDEV, D // 2, -1)]
            both = _Both(_ToSibling(g_b), _ToChips(p_a))
            dh, got = _mm_nt_in(dproj, w_in_l, 0, f"dh_{tag}", both)
            sib_b, chips_a = both.split_results(got)
            p_b = [_add_sibling(g_b[0], sib_b[0], c_idx, f"rs_add_b_{tag}")]
            (dx, dshift, dscale, d_norm[kind][i]), chips_b = _pre_bwd(
                rec["xl"], dh, dx, rec["g"], rec["scale"], f"pre_bwd_{tag}", _ToChips(p_b))
            sum_and_update(0, 0, chips_a[0], (0, 2))
            sum_and_update(0, 1, chips_a[1])
            sum_and_update(0, 0, chips_b[0], (1, 2))
            dm[kind][i] = jnp.concatenate([dshift, dscale, dgate], axis=1).reshape(3 * D)
            continue
        if earlier:
            dwi, got_in = _mm_tn_in(rec["h"], dproj, 0, 1, None, f"dwin_{tag}", _ToChips(earlier[1][:1]))
        else:
            dwi = _mm_tn_in(rec["h"], dproj, 0, 1, None, f"dwin_{tag}")
        grads = [dwi.reshape(NDEV, D, -1), dwo.reshape(NDEV, kb, D)]
        if earlier:
            both = _Both(_ToSibling(grads), _ToChips(earlier[1][1:]))
            dh, got = _mm_nt_in(dproj, w_in_l, 0, f"dh_{tag}", both)
            from_sibling, got_out = both.split_results(got)
            finish_layer(earlier[0], [got_in[0], got_out[0]])
        else:
            dh, from_sibling = _mm_nt_in(dproj, w_in_l, 0, f"dh_{tag}", _ToSibling(grads))
        pending = (layer, [_add_sibling(g, r, c_idx, f"rs_add_{big_names[kind][k]}{i}")
                           for k, (g, r) in enumerate(zip(grads, from_sibling))])
        (dx, dshift, dscale, d_norm[kind][i]), _ = _pre_bwd(
            rec["xl"], dh, dx, rec["g"], rec["scale"], f"pre_bwd_{tag}")
        dm[kind][i] = jnp.concatenate([dshift, dscale, dgate], axis=1).reshape(3 * D)
    grad_x = dx.reshape(1, S, D)
    for kind in range(2):
        for k in range(2):
            nm = big_names[kind][k]
            big_res[nm] = [o.reshape(big_w[kind][k][0].shape) for o in big_res[nm]]

    stack = lambda xs: jnp.stack(xs)
    pack_items = [stack(dm[0]), stack(dm[1]), stack(d_norm[0]).reshape(L, D), stack(d_conv),
                  stack(d_norm[1]).reshape(L, D), stack(d_lng).reshape(L, D), stack(d_lnb).reshape(L, D),
                  stack(d_ws), stack(d_bs), d_final_g]
    (g3,) = _comm_only(_Gather([_pack(pack_items)]), "ag_grads")
    P = g3.shape[1] * g3.shape[2]
    tot = _sum_rows(g3, "sum_small").reshape(P)
    g3 = g3.reshape(NDEV, P)
    sizes = [int(np.prod(p.shape)) for p in pack_items]
    offs = np.concatenate([[0], np.cumsum(sizes)]).tolist()
    seg = lambda k, shape: tot[offs[k]:offs[k + 1]].reshape(shape)

    def shard(full, n):
        return lax.dynamic_slice_in_dim(full, dev * n, n, axis=full.ndim - 1)

    g_ab_b_mod = seg(0, (L, 3 * D))
    g_sg_b_mod = shard(seg(1, (L, 3 * D)), n_mod)
    g_ab_norm = seg(2, (L, D))
    g_conv = shard(seg(3, (L, 3, W)), W // NDEV)
    g_sg_norm = shard(seg(4, (L, D)), kb)
    g_ln_g = shard(seg(5, (L, D)), kb)
    g_ln_b = shard(seg(6, (L, D)), kb)
    g_w_s = seg(7, sg_w_s.shape)
    g_b_s = seg(8, sg_b_s.shape)
    g_final = seg(9, (D,))

    small_w = [("ab_norm_g", g_ab_norm, ab_norm_g, m_ab_norm_g, v_ab_norm_g),
               ("ab_b_mod", g_ab_b_mod, ab_b_mod, m_ab_b_mod, v_ab_b_mod),
               ("ab_conv_w", g_conv, ab_conv_w, m_ab_conv_w, v_ab_conv_w),
               ("sg_norm_g", g_sg_norm, sg_norm_g, m_sg_norm_g, v_sg_norm_g),
               ("sg_b_mod", g_sg_b_mod, sg_b_mod, m_sg_b_mod, v_sg_b_mod),
               ("sg_ln_g", g_ln_g, sg_ln_g, m_sg_ln_g, v_sg_ln_g),
               ("sg_ln_b", g_ln_b, sg_ln_b, m_sg_ln_b, v_sg_ln_b),
               ("sg_w_s", g_w_s, sg_w_s, m_sg_w_s, v_sg_w_s),
               ("sg_b_s", g_b_s, sg_b_s, m_sg_b_s, v_sg_b_s),
               ("final_norm_g", g_final, final_norm_g, m_final_norm_g, v_final_norm_g)]
    packed = [_pack([t[k] for t in small_w]) for k in (1, 2, 3, 4)]
    upd = _adam_only(*packed, "adam_small")
    small_res = {}
    o = 0
    for nm, g, w, _, _ in small_w:
        size = int(np.prod(w.shape))
        small_res[nm] = [g] + [u.reshape(-1)[o:o + size].reshape(w.shape) for u in upd]
        o += size

    KP = 128
    sc_t = jnp.pad((c_all * jax.nn.sigmoid(c_all)).T, ((0, 0), (0, KP - NDEV)))
    mod_res = {}
    for kind, nm, (w, m, v) in ((0, "ab_w_mod", (ab_w_mod, m_ab_w_mod, v_ab_w_mod)),
                                (1, "sg_w_mod", (sg_w_mod, m_sg_w_mod, v_sg_w_mod))):
        dm_all = g3[:, offs[kind]:offs[kind + 1]].reshape(NDEV, L, 3 * D)
        cols = jnp.pad(shard(dm_all, n_mod).transpose(1, 0, 2), ((0, 0), (0, KP - NDEV), (0, 0)))
        mod_res[nm] = _wmod_grad_adam(sc_t, cols, w, m, v, f"adam_{nm}")

    order = ["ab_norm_g", "ab_w_mod", "ab_b_mod", "ab_w_in", "ab_conv_w", "ab_w_out", "sg_norm_g", "sg_w_mod",
             "sg_b_mod", "sg_w_in", "sg_ln_g", "sg_ln_b", "sg_w_s", "sg_b_s", "sg_w_out", "final_norm_g"]
    res = {**big_res, **small_res, **mod_res}
    outs = [loss, grad_x]
    for k in range(4):
        outs += [res[nm][k] for nm in order]
    return tuple(outs)
```

```python
import functools
import math

import numpy as np
import jax
import jax.numpy as jnp
from jax import lax
from jax.experimental import pallas as pl
from jax.experimental.pallas import tpu as pltpu

F32 = jnp.float32
BF16 = jnp.bfloat16

NDEV = 8
NCHIP = 4
EPS = 1e-6
HEAD_DIM = 128
ROPE_THETA = 10000.0
DILATED_PATTERNS = ((128, 1), (512, 4), (2048, 16))
NEG_INF = -1e30
C_CHUNK = 128
C_GROUPS = 8
ADAM_LR = 0.001
ADAM_B1 = 0.9
ADAM_B2 = 0.999
ADAM_EPS = 1e-08
ADAM_WD = 0.01
ADAM_STEP = 10
GELU_K = math.sqrt(2.0 / math.pi)
GELU_C = 0.044715

VMEM_LIMIT_BYTES = 56 * 1024 * 1024
ATTN_TILE = 512
HEADS_PER_STEP = 4
ATTN_ROW_CHUNK = 256
LANES = 128
ROW_TILE = 256
MESH = pl.DeviceIdType.MESH
ANY = pl.BlockSpec(memory_space=pl.ANY)


def _cp(*sem):
    return pltpu.CompilerParams(dimension_semantics=sem, vmem_limit_bytes=VMEM_LIMIT_BYTES)


def _sigmoid(z):
    return 0.5 * (jnp.tanh(0.5 * z) + 1.0)


def _silu_and_grad(z):
    s = _sigmoid(z)
    return z * s, s * (1.0 + z * (1.0 - s))


def _gelu_and_grad(x):
    x2 = x * x
    t = jnp.tanh(GELU_K * (x + GELU_C * x2 * x))
    g = 0.5 * x * (1.0 + t)
    dg = 0.5 * (1.0 + t) + 0.5 * x * (1.0 - t * t) * (GELU_K * (1.0 + 3.0 * GELU_C * x2))
    return g, dg


def _position():
    return lax.axis_index("x"), lax.axis_index("y"), lax.axis_index("c")


def _chips(x, y):
    return [(1 - x, y), (x, 1 - y), (1 - x, 1 - y)]


class _Gather:
    def __init__(self, arrs):
        n = len(arrs)
        self.arrs = list(arrs)
        self.out_shape = [jax.ShapeDtypeStruct((NDEV,) + a.shape, a.dtype) for a in arrs]
        self.scratch = [pltpu.SemaphoreType.DMA((n, 7)), pltpu.SemaphoreType.DMA((n, 7)),
                        pltpu.SemaphoreType.DMA((n,))]

    def _copies(self, ins, outs, sems, own=True):
        send_sems, recv_sems, local_sems = sems
        x, y, c = _position()

        def copy(a, k, block, to, src=None):
            dst = outs[a].at[4 * block[0] + 2 * block[1] + block[2]]
            return pltpu.make_async_remote_copy(
                src_ref=dst if src is None else src, dst_ref=dst,
                send_sem=send_sems.at[a, k], recv_sem=recv_sems.at[a, k],
                device_id=to, device_id_type=MESH)

        n = len(ins)
        me, sibling = (x, y, c), (x, y, 1 - c)
        mine, first = [], []
        if own:
            mine = [pltpu.make_async_copy(ins[a], outs[a].at[4 * x + 2 * y + c], local_sems.at[a])
                    for a in range(n)]
            for a in range(n):
                first.append(copy(a, 0, me, sibling, src=ins[a]))
                first += [copy(a, 1 + j, me, (*chip, c), src=ins[a]) for j, chip in enumerate(_chips(x, y))]
        return copy, mine, first

    def start(self, ins, outs, sems):
        _, mine, first = self._copies(ins, outs, sems)
        for cp in mine + first:
            cp.start()

    def middle(self, ins, outs, sems):
        copy = self._copies(ins, outs, sems, own=False)[0]
        x, y, c = _position()
        me, sibling = (x, y, c), (x, y, 1 - c)
        for j, chip in enumerate(_chips(x, y)):
            for a in range(len(ins)):
                copy(a, 1 + j, (*chip, c), me).wait_recv()
                copy(a, 4 + j, (*chip, c), sibling).start()

    def finish(self, ins, outs, sems):
        copy, mine, first = self._copies(ins, outs, sems)
        x, y, c = _position()
        me, sibling = (x, y, c), (x, y, 1 - c)
        passed = [copy(a, 4 + j, (*chip, c), sibling)
                  for j, chip in enumerate(_chips(x, y)) for a in range(len(ins))]
        for a in range(len(ins)):
            copy(a, 0, sibling, me).wait_recv()
            for j, chip in enumerate(_chips(x, y)):
                copy(a, 4 + j, (*chip, 1 - c), me).wait_recv()
        for cp in first + passed:
            cp.wait_send()
        for cp in mine:
            cp.wait()


class _GatherDirect:
    def __init__(self, arrs):
        n = len(arrs)
        self.arrs = list(arrs)
        self.out_shape = [jax.ShapeDtypeStruct((NDEV,) + a.shape, a.dtype) for a in arrs]
        self.scratch = [pltpu.SemaphoreType.DMA((n, 7)), pltpu.SemaphoreType.DMA((n, 7)),
                        pltpu.SemaphoreType.DMA((n,))]

    def _copies(self, ins, outs, sems, arrivals):
        send_sems, recv_sems, local_sems = sems
        x, y, c = _position()
        mine = [pltpu.make_async_copy(ins[a], outs[a].at[4 * x + 2 * y + c], local_sems.at[a])
                for a in range(len(ins))]
        sends, recvs = [], []
        for a in range(len(ins)):
            for k in range(1, NDEV):
                px = 1 - x if k & 4 else x
                py = 1 - y if k & 2 else y
                pc = 1 - c if k & 1 else c
                sends.append(pltpu.make_async_remote_copy(
                    src_ref=ins[a], dst_ref=outs[a].at[4 * x + 2 * y + c],
                    send_sem=send_sems.at[a, k - 1], recv_sem=recv_sems.at[a, k - 1],
                    device_id=(px, py, pc), device_id_type=MESH))
                if arrivals:
                    slot = outs[a].at[4 * px + 2 * py + pc]
                    recvs.append(pltpu.make_async_remote_copy(
                        src_ref=slot, dst_ref=slot, send_sem=send_sems.at[a, k - 1], recv_sem=recv_sems.at[a, k - 1],
                        device_id=(px, py, pc), device_id_type=MESH))
        return mine, sends, recvs

    def start(self, ins, outs, sems):
        mine, sends, _ = self._copies(ins, outs, sems, False)
        for cp in mine + sends:
            cp.start()

    def finish(self, ins, outs, sems):
        mine, sends, recvs = self._copies(ins, outs, sems, True)
        for cp in recvs:
            cp.wait_recv()
        for cp in sends:
            cp.wait_send()
        for cp in mine:
            cp.wait()


class _ToSibling:
    def __init__(self, gs):
        n = len(gs)
        self.arrs = list(gs)
        self.out_shape = [jax.ShapeDtypeStruct((NCHIP,) + g.shape[1:], g.dtype) for g in gs]
        self.scratch = [pltpu.SemaphoreType.DMA((n, NCHIP)), pltpu.SemaphoreType.DMA((n, NCHIP))]

    def _copies(self, ins, outs, sems):
        send_sems, recv_sems = sems
        x, y, c = _position()
        return [pltpu.make_async_remote_copy(
            src_ref=ins[a].at[2 * k + (1 - c)], dst_ref=outs[a].at[k],
            send_sem=send_sems.at[a, k], recv_sem=recv_sems.at[a, k],
            device_id=(x, y, 1 - c), device_id_type=MESH) for a in range(len(ins)) for k in range(NCHIP)]

    def start(self, ins, outs, sems):
        for cp in self._copies(ins, outs, sems):
            cp.start()

    def finish(self, ins, outs, sems):
        copies = self._copies(ins, outs, sems)
        for cp in copies:
            cp.wait_recv()
        for cp in copies:
            cp.wait_send()


class _ToChips:
    def __init__(self, ps):
        n = len(ps)
        self.arrs = list(ps)
        self.out_shape = [jax.ShapeDtypeStruct(p.shape, p.dtype) for p in ps]
        self.scratch = [pltpu.SemaphoreType.DMA((n, 3)), pltpu.SemaphoreType.DMA((n, 3)),
                        pltpu.SemaphoreType.DMA((n,))]

    def _copies(self, ins, outs, sems, arrivals):
        send_sems, recv_sems, local_sems = sems
        x, y, c = _position()
        mychip = 2 * x + y
        n = len(ins)
        mine = [pltpu.make_async_copy(ins[a].at[mychip], outs[a].at[mychip], local_sems.at[a]) for a in range(n)]
        sends, recvs = [], []
        for a in range(n):
            for j, chip in enumerate(_chips(x, y)):
                sends.append(pltpu.make_async_remote_copy(
                    src_ref=ins[a].at[2 * chip[0] + chip[1]], dst_ref=outs[a].at[mychip],
                    send_sem=send_sems.at[a, j], recv_sem=recv_sems.at[a, j],
                    device_id=(*chip, c), device_id_type=MESH))
                if arrivals:
                    slot = outs[a].at[2 * chip[0] + chip[1]]
                    recvs.append(pltpu.make_async_remote_copy(
                        src_ref=slot, dst_ref=slot, send_sem=send_sems.at[a, j], recv_sem=recv_sems.at[a, j],
                        device_id=(*chip, c), device_id_type=MESH))
        return mine, sends, recvs

    def start(self, ins, outs, sems):
        mine, sends, _ = self._copies(ins, outs, sems, False)
        for cp in mine + sends:
            cp.start()

    def finish(self, ins, outs, sems):
        mine, sends, recvs = self._copies(ins, outs, sems, True)
        for cp in recvs:
            cp.wait_recv()
        for cp in sends:
            cp.wait_send()
        for cp in mine:
            cp.wait()


HOSTED_MIDDLE_AT = 0.8


def _middle_of(comm, ins, outs, sems):
    if hasattr(comm, "middle"):
        comm.middle(ins, outs, sems)


class _Both:
    def __init__(self, first, second):
        self.parts = (first, second)
        self.arrs = first.arrs + second.arrs
        self.out_shape = first.out_shape + second.out_shape
        self.scratch = first.scratch + second.scratch

    def _split(self, ins, outs, sems):
        a, _ = self.parts
        ni, no, ns = len(a.arrs), len(a.out_shape), len(a.scratch)
        return (ins[:ni], outs[:no], sems[:ns]), (ins[ni:], outs[no:], sems[ns:])

    def start(self, ins, outs, sems):
        for part, refs in zip(self.parts, self._split(ins, outs, sems)):
            part.start(*refs)

    def middle(self, ins, outs, sems):
        for part, refs in zip(self.parts, self._split(ins, outs, sems)):
            _middle_of(part, *refs)

    def finish(self, ins, outs, sems):
        for part, refs in zip(self.parts, self._split(ins, outs, sems)):
            part.finish(*refs)

    def split_results(self, res):
        no = len(self.parts[0].out_shape)
        return res[:no], res[no:]


def _comm_only(comm, name):
    n_in, n_out = len(comm.arrs), len(comm.out_shape)

    def body(*refs):
        ins, outs, sems = refs[:n_in], refs[n_in:n_in + n_out], refs[n_in + n_out:]
        comm.start(ins, outs, sems)
        _middle_of(comm, ins, outs, sems)
        comm.finish(ins, outs, sems)

    return pl.pallas_call(
        body, name=name, out_shape=comm.out_shape, in_specs=[ANY] * n_in, out_specs=[ANY] * n_out,
        scratch_shapes=comm.scratch,
    )(*comm.arrs)


def _hosted_call(body, operands, *, name, grid, in_specs, out_specs, out_shape, scratch_shapes=(), sem=(),
                 aliases=None, comm=None):
    single = not isinstance(out_shape, (list, tuple))
    o_specs = [out_specs] if single else list(out_specs)
    o_shape = [out_shape] if single else list(out_shape)
    n_in, n_out, n_scr = len(in_specs), len(o_shape), len(scratch_shapes)
    if comm is None:
        res = pl.pallas_call(body, name=name, grid=grid, in_specs=list(in_specs), out_specs=o_specs,
                             out_shape=o_shape, scratch_shapes=list(scratch_shapes),
                             input_output_aliases=aliases or {}, compiler_params=_cp(*sem))(*operands)
        return (res[0] if single else res), []
    c_in, c_out = len(comm.arrs), len(comm.out_shape)

    def wrapped(*refs):
        ins, cins = refs[:n_in], refs[n_in:n_in + c_in]
        o0 = n_in + c_in
        outs, couts = refs[o0:o0 + n_out], refs[o0 + n_out:o0 + n_out + c_out]
        s0 = o0 + n_out + c_out
        scr, csems = refs[s0:s0 + n_scr], refs[s0 + n_scr:]
        pids = [pl.program_id(a) for a in range(len(grid))]
        step = functools.reduce(lambda acc, pg: acc * pg[1] + pg[0], zip(pids, grid), 0)
        total = int(np.prod(grid))
        late = min(total - 1, max(1, int(total * HOSTED_MIDDLE_AT)))

        @pl.when(step == 0)
        def _():
            comm.start(cins, couts, csems)

        @pl.when(step == late)
        def _():
            _middle_of(comm, cins, couts, csems)

        body(*ins, *outs, *scr)

        @pl.when(step == total - 1)
        def _():
            comm.finish(cins, couts, csems)

    res = pl.pallas_call(
        wrapped, name=name, grid=grid, in_specs=list(in_specs) + [ANY] * c_in, out_specs=o_specs + [ANY] * c_out,
        out_shape=o_shape + comm.out_shape, scratch_shapes=list(scratch_shapes) + comm.scratch,
        input_output_aliases=aliases or {}, compiler_params=_cp(*(["arbitrary"] * len(grid))),
    )(*operands, *comm.arrs)
    return (res[0] if single else res[:n_out]), res[n_out:]


def _adamw(w, g, m, v):
    m2 = ADAM_B1 * m + (1.0 - ADAM_B1) * g
    v2 = ADAM_B2 * v + (1.0 - ADAM_B2) * (g * g)
    m_hat = m2 / (1.0 - ADAM_B1 ** ADAM_STEP)
    v_hat = v2 / (1.0 - ADAM_B2 ** ADAM_STEP)
    delta = -ADAM_LR * (m_hat / (jnp.sqrt(v_hat) + ADAM_EPS) + ADAM_WD * w)
    return delta, m2, v2


def _add_sibling(g, recv, c_idx, name, part=(0, 1)):
    _, R, C = g.shape
    R = R // part[1]
    tr = min(R, 512)
    first = part[0] * (R // tr)

    def body(c_ref, g_ref, r_ref, o_ref):
        o_ref[...] = (g_ref[...] + r_ref[...]).astype(BF16)

    return pl.pallas_call(
        body, name=name,
        grid_spec=pltpu.PrefetchScalarGridSpec(
            num_scalar_prefetch=1, grid=(NCHIP, R // tr),
            in_specs=[pl.BlockSpec((1, tr, C), lambda k, i, c_ref: (2 * k + c_ref[0], first + i, 0)),
                      pl.BlockSpec((1, tr, C), lambda k, i, c_ref: (k, first + i, 0))],
            out_specs=pl.BlockSpec((1, tr, C), lambda k, i, c_ref: (k, i, 0))),
        out_shape=jax.ShapeDtypeStruct((NCHIP, R, C), BF16),
        compiler_params=_cp("parallel", "parallel"),
    )(c_idx, g, recv)


def _sum_adam(parts, w, m, v, row0, prev, name):
    K, R, C = parts.shape
    LR = w.shape[0]
    tr = min(R, 256)
    nb = R // tr
    first = row0 // tr

    def body(p_ref, w_ref, m_ref, v_ref, *rest):
        g_ref, d_ref, m2_ref, v2_ref = rest[-4:]
        g = p_ref[0].astype(F32)
        for k in range(1, K):
            g = g + p_ref[k].astype(F32)
        delta, m2, v2 = _adamw(w_ref[...], g, m_ref[...], v_ref[...])
        g_ref[...] = g
        d_ref[...] = delta
        m2_ref[...] = m2
        v2_ref[...] = v2

    blk = pl.BlockSpec((tr, C), lambda i: (first + i, 0))
    shp = jax.ShapeDtypeStruct((LR, C), F32)
    operands = [parts, w, m, v] + (list(prev) if prev is not None else [])
    return pl.pallas_call(
        body, name=name, grid=(nb,),
        in_specs=[pl.BlockSpec((K, tr, C), lambda i: (0, i, 0)), blk, blk, blk] + [ANY] * (len(operands) - 4),
        out_specs=[blk] * 4, out_shape=[shp] * 4,
        input_output_aliases={4 + k: k for k in range(len(operands) - 4)},
        compiler_params=_cp("parallel"),
    )(*operands)


def _sum_rows(parts, name):
    K, R, C = parts.shape
    tr = min(R, 256)
    while R % tr:
        tr //= 2

    def body(p_ref, o_ref):
        g = p_ref[0]
        for k in range(1, K):
            g = g + p_ref[k]
        o_ref[...] = g

    return pl.pallas_call(
        body, name=name, grid=(R // tr,),
        in_specs=[pl.BlockSpec((K, tr, C), lambda i: (0, i, 0))],
        out_specs=pl.BlockSpec((tr, C), lambda i: (i, 0)),
        out_shape=jax.ShapeDtypeStruct((R, C), F32),
        compiler_params=_cp("parallel"),
    )(parts)


def _adam_only(g, w, m, v, name):
    R, C = g.shape
    tr = min(R, 256)
    while R % tr:
        tr //= 2

    def body(g_ref, w_ref, m_ref, v_ref, d_ref, m2_ref, v2_ref):
        delta, m2, v2 = _adamw(w_ref[...], g_ref[...], m_ref[...], v_ref[...])
        d_ref[...] = delta
        m2_ref[...] = m2
        v2_ref[...] = v2

    blk = pl.BlockSpec((tr, C), lambda i: (i, 0))
    shp = jax.ShapeDtypeStruct((R, C), F32)
    return pl.pallas_call(
        body, name=name, grid=(R // tr,), in_specs=[blk] * 4, out_specs=[blk] * 3,
        out_shape=[shp] * 3, compiler_params=_cp("parallel"),
    )(g, w, m, v)


def _mod_fwd(c_all, w_mod, b_cols, name):
    L, D, n = w_mod.shape
    B = c_all.shape[0]

    def body(c_ref, w_ref, b_ref, o_ref):
        cv = c_ref[...]
        sc = (cv * _sigmoid(cv)).astype(BF16)
        o_ref[0] = jnp.dot(sc, w_ref[0].astype(BF16), preferred_element_type=F32) + b_ref[0]

    return pl.pallas_call(
        body, name=name, grid=(L,),
        in_specs=[pl.BlockSpec((B, D), lambda l: (0, 0)),
                  pl.BlockSpec((1, D, n), lambda l: (l, 0, 0)),
                  pl.BlockSpec((1, 1, n), lambda l: (l, 0, 0))],
        out_specs=pl.BlockSpec((1, B, n), lambda l: (l, 0, 0)),
        out_shape=jax.ShapeDtypeStruct((L, B, n), F32),
        compiler_params=_cp("parallel"),
    )(c_all, w_mod, b_cols)


def _wmod_grad_adam(sc_t, dm, w, m, v, name):
    L, D, n = w.shape
    KP = sc_t.shape[1]
    tr = min(D, 512)

    def body(s_ref, dm_ref, w_ref, m_ref, v_ref, g_ref, d_ref, m2_ref, v2_ref):
        g = jnp.dot(s_ref[...], dm_ref[0], preferred_element_type=F32,
                    precision=lax.Precision.HIGHEST)
        delta, m2, v2 = _adamw(w_ref[0], g, m_ref[0], v_ref[0])
        g_ref[0] = g
        d_ref[0] = delta
        m2_ref[0] = m2
        v2_ref[0] = v2

    blk = pl.BlockSpec((1, tr, n), lambda l, i: (l, i, 0))
    shp = jax.ShapeDtypeStruct((L, D, n), F32)
    return pl.pallas_call(
        body, name=name, grid=(L, D // tr),
        in_specs=[pl.BlockSpec((tr, KP), lambda l, i: (i, 0)),
                  pl.BlockSpec((1, KP, n), lambda l, i: (l, 0, 0)), blk, blk, blk],
        out_specs=[blk] * 4, out_shape=[shp] * 4,
        compiler_params=_cp("parallel", "parallel"),
    )(sc_t, dm, w, m, v)


def _vec_spec(D):
    return pl.BlockSpec((1, D), lambda i: (0, 0))


def _pre(x, res, gate, g, scale, shift, name):
    S, D = x.shape
    tr = min(S, ROW_TILE)
    has_res = res is not None
    row = pl.BlockSpec((tr, D), lambda i: (i, 0))

    def body(*refs):
        if has_res:
            x_ref, r_ref, gate_ref, g_ref, sc_ref, sh_ref, xl_ref, h_ref = refs
            xv = x_ref[...] + gate_ref[...] * r_ref[...]
            xl_ref[...] = xv
        else:
            x_ref, g_ref, sc_ref, sh_ref, h_ref = refs
            xv = x_ref[...]
        r = lax.rsqrt(jnp.mean(xv * xv, axis=-1, keepdims=True) + EPS)
        y = (xv * r) * g_ref[...]
        h_ref[...] = (y * (1.0 + sc_ref[...]) + sh_ref[...]).astype(BF16)

    vec = _vec_spec(D)
    if has_res:
        xl, h = pl.pallas_call(
            body, name=name, grid=(S // tr,),
            in_specs=[row, row, vec, vec, vec, vec], out_specs=[row, row],
            out_shape=[jax.ShapeDtypeStruct((S, D), F32), jax.ShapeDtypeStruct((S, D), BF16)],
            compiler_params=_cp("parallel"),
        )(x, res, gate, g, scale, shift)
        return xl, h
    h = pl.pallas_call(
        body, name=name, grid=(S // tr,),
        in_specs=[row, vec, vec, vec], out_specs=row,
        out_shape=jax.ShapeDtypeStruct((S, D), BF16),
        compiler_params=_cp("parallel"),
    )(x, g, scale, shift)
    return x, h


def _pre_bwd(xl, dh, dx_in, g, scale, name, comm=None):
    S, D = xl.shape
    tr = min(S, ROW_TILE)
    nsteps = S // tr
    row = pl.BlockSpec((tr, D), lambda i: (i, 0))
    vec = _vec_spec(D)

    def body(x_ref, dh_ref, dxin_ref, g_ref, sc_ref, dx_ref, dsh_ref, dsc_ref, dg_ref, acc_sh, acc_t):
        i = pl.program_id(0)
        xv = x_ref[...]
        dh = dh_ref[...]
        r = lax.rsqrt(jnp.mean(xv * xv, axis=-1, keepdims=True) + EPS)
        xn = xv * r
        part_sh = jnp.sum(dh.reshape(tr // 8, 8, D), axis=0)
        part_t = jnp.sum((dh * xn).reshape(tr // 8, 8, D), axis=0)

        @pl.when(i == 0)
        def _():
            acc_sh[...] = part_sh
            acc_t[...] = part_t

        @pl.when(i > 0)
        def _():
            acc_sh[...] += part_sh
            acc_t[...] += part_t

        dxn = dh * (g_ref[...] * (1.0 + sc_ref[...]))
        dx_ref[...] = dxin_ref[...] + r * (dxn - xn * jnp.mean(dxn * xn, axis=-1, keepdims=True))

        @pl.when(i == nsteps - 1)
        def _():
            t = jnp.sum(acc_t[...], axis=0, keepdims=True)
            dsh_ref[...] = jnp.sum(acc_sh[...], axis=0, keepdims=True)
            dsc_ref[...] = t * g_ref[...]
            dg_ref[...] = t * (1.0 + sc_ref[...])

    v = jax.ShapeDtypeStruct((1, D), F32)
    return _hosted_call(
        body, [xl, dh, dx_in, g, scale], name=name, grid=(nsteps,),
        in_specs=[row, row, row, vec, vec], out_specs=[row, vec, vec, vec],
        out_shape=[jax.ShapeDtypeStruct((S, D), F32), v, v, v],
        scratch_shapes=[pltpu.VMEM((8, D), F32), pltpu.VMEM((8, D), F32)],
        sem=("arbitrary",), comm=comm)


def _post_bwd(dx, out, gate, name):
    S, D = dx.shape
    tr = min(S, ROW_TILE)
    nsteps = S // tr
    row = pl.BlockSpec((tr, D), lambda i: (i, 0))
    vec = _vec_spec(D)

    def body(dx_ref, o_ref, gate_ref, do_ref, dg_ref, acc):
        i = pl.program_id(0)
        dxv = dx_ref[...]
        do_ref[...] = (dxv * gate_ref[...]).astype(BF16)
        part = jnp.sum((dxv * o_ref[...]).reshape(tr // 8, 8, D), axis=0)

        @pl.when(i == 0)
        def _():
            acc[...] = part

        @pl.when(i > 0)
        def _():
            acc[...] += part

        @pl.when(i == nsteps - 1)
        def _():
            dg_ref[...] = jnp.sum(acc[...], axis=0, keepdims=True)

    return pl.pallas_call(
        body, name=name, grid=(nsteps,),
        in_specs=[row, row, vec], out_specs=[row, vec],
        out_shape=[jax.ShapeDtypeStruct((S, D), BF16), jax.ShapeDtypeStruct((1, D), F32)],
        scratch_shapes=[pltpu.VMEM((8, D), F32)],
        compiler_params=_cp("arbitrary"),
    )(dx, out, gate)


def _loss_head(x, res, gate, gf, tgt, name):
    S, D = x.shape
    tr = min(S, ROW_TILE)
    nsteps = S // tr
    row = pl.BlockSpec((tr, D), lambda i: (i, 0))
    vec = _vec_spec(D)

    def body(x_ref, r_ref, gate_ref, gf_ref, t_ref, dx_ref, loss_ref, dgf_ref, acc, lacc):
        i = pl.program_id(0)
        xv = x_ref[...] + gate_ref[...] * r_ref[...]
        r = lax.rsqrt(jnp.mean(xv * xv, axis=-1, keepdims=True) + EPS)
        xn = xv * r
        err = xn * gf_ref[...] - t_ref[...]
        row_loss = jnp.mean(err * err, axis=-1, keepdims=True)
        lpart = 0.5 * jnp.sum(row_loss, axis=0, keepdims=True)
        dy = err * (1.0 / D)
        part = jnp.sum((dy * xn).reshape(tr // 8, 8, D), axis=0)

        @pl.when(i == 0)
        def _():
            acc[...] = part
            lacc[...] = lpart

        @pl.when(i > 0)
        def _():
            acc[...] += part
            lacc[...] += lpart

        dxn = dy * gf_ref[...]
        dx_ref[...] = r * (dxn - xn * jnp.mean(dxn * xn, axis=-1, keepdims=True))

        @pl.when(i == nsteps - 1)
        def _():
            dgf_ref[...] = jnp.sum(acc[...], axis=0, keepdims=True)
            loss_ref[...] = lacc[...]

    return pl.pallas_call(
        body, name=name, grid=(nsteps,),
        in_specs=[row, row, vec, vec, row],
        out_specs=[row, pl.BlockSpec((1, 1), lambda i: (0, 0)), vec],
        out_shape=[jax.ShapeDtypeStruct((S, D), F32), jax.ShapeDtypeStruct((1, 1), F32),
                   jax.ShapeDtypeStruct((1, D), F32)],
        scratch_shapes=[pltpu.VMEM((8, D), F32), pltpu.VMEM((1, 1), F32)],
        compiler_params=_cp("arbitrary"),
    )(x, res, gate, gf, tgt)


NN = (((1,), (0,)), ((), ()))
NT = (((1,), (1,)), ((), ()))
TN = (((0,), (0,)), ((), ()))


def _mm(name, a, b, out_shape, grid, a_spec, b_spec, o_spec, dims, a2d, b2d, k_axis, sem, alias=None, comm=None):
    def body(*refs):
        a_ref, b_ref, o_ref = refs[0], refs[1], refs[-1]
        r = lax.dot_general(a_ref[...].reshape(a2d), b_ref[...].reshape(b2d), dims,
                            preferred_element_type=F32)
        r = r.reshape(o_ref.shape)
        if k_axis is None:
            o_ref[...] = r.astype(o_ref.dtype)
        else:
            k = pl.program_id(k_axis)

            @pl.when(k == 0)
            def _():
                o_ref[...] = r

            @pl.when(k > 0)
            def _():
                o_ref[...] += r

    operands, in_specs, aliases = [a, b], [a_spec, b_spec], {}
    if alias is not None:
        operands.append(alias)
        in_specs.append(ANY)
        aliases = {2: 0}
    res, extra = _hosted_call(body, operands, name=name, grid=grid, in_specs=in_specs, out_specs=o_spec,
                              out_shape=out_shape, sem=sem, aliases=aliases, comm=comm)
    return res if comm is None else (res, extra)


def _tile(n, pref):
    t = min(n, pref)
    while n % t:
        t -= 128
    return t


def _mm_nn_in(a, w, l, name, comm=None):
    M, K = a.shape
    _, _, _, n = w.shape
    tm, tn = min(M, 512), _tile(n, 1024)
    nb = n // tn
    return _mm(name, a, w, jax.ShapeDtypeStruct((M, NDEV * n), F32), (NDEV * nb, M // tm),
               pl.BlockSpec((tm, K), lambda j, i: (i, 0)),
               pl.BlockSpec((1, 1, K, tn), lambda j, i: (j // nb, l, 0, j % nb)),
               pl.BlockSpec((tm, tn), lambda j, i: (i, j)),
               NN, (tm, K), (K, tn), None, ("parallel", "parallel"), comm=comm)


def _mm_nn_out(a, w, l, name):
    M, K = a.shape
    _, _, kb, N = w.shape
    tm, tn = min(M, 512), _tile(N, 1024)
    return _mm(name, a, w, jax.ShapeDtypeStruct((M, N), F32), (N // tn, M // tm),
               pl.BlockSpec((tm, K), lambda j, i: (i, 0)),
               pl.BlockSpec((NDEV, 1, kb, tn), lambda j, i: (0, l, 0, j)),
               pl.BlockSpec((tm, tn), lambda j, i: (i, j)),
               NN, (tm, K), (K, tn), None, ("parallel", "parallel"))


def _mm_nt_in(a, w, l, name, comm=None):
    M, _ = a.shape
    _, _, K, n = w.shape
    tm, tk = min(M, 1024), _tile(K, 1024)
    gb = 2 if n <= 1024 else 1

    def body(a_ref, w_ref, o_ref):
        k = pl.program_id(2)
        r = lax.dot_general(a_ref[:, :n], w_ref[0, 0], NT, preferred_element_type=F32)
        for g in range(1, gb):
            r = r + lax.dot_general(a_ref[:, g * n:(g + 1) * n], w_ref[g, 0], NT, preferred_element_type=F32)

        @pl.when(k == 0)
        def _():
            o_ref[...] = r

        @pl.when(k > 0)
        def _():
            o_ref[...] += r

    res, extra = _hosted_call(
        body, [a, w], name=name, grid=(M // tm, K // tk, NDEV // gb),
        in_specs=[pl.BlockSpec((tm, gb * n), lambda i, j, k: (i, k)),
                  pl.BlockSpec((gb, 1, tk, n), lambda i, j, k: (k, l, j, 0))],
        out_specs=pl.BlockSpec((tm, tk), lambda i, j, k: (i, j)),
        out_shape=jax.ShapeDtypeStruct((M, K), F32),
        sem=("parallel", "parallel", "arbitrary"), comm=comm)
    return res if comm is None else (res, extra)


def _mm_nt_out(a, w, l, name):
    M, N = a.shape
    _, _, kb, _ = w.shape
    K = NDEV * kb
    tm, tk, tc = min(M, 1024), _tile(K, 1024), _tile(N, 1024)
    per = tk // kb
    return _mm(name, a, w, jax.ShapeDtypeStruct((M, K), F32), (M // tm, K // tk, N // tc),
               pl.BlockSpec((tm, tc), lambda i, j, k: (i, k)),
               pl.BlockSpec((per, 1, kb, tc), lambda i, j, k: (j, l, 0, k)),
               pl.BlockSpec((tm, tk), lambda i, j, k: (i, j)),
               NT, (tm, tc), (tk, tc), 2, ("parallel", "parallel", "arbitrary"))


def _mm_tn_in(a, b, l, L, buf, name, comm=None, part=(0, 1)):
    S, K = a.shape
    K = K // part[1]
    n = b.shape[1] // NDEV
    ts, tk, tn = min(S, 2048), _tile(K, 1024), _tile(n, 1024)
    nb = n // tn
    first = part[0] * (K // tk)
    return _mm(name, a, b, jax.ShapeDtypeStruct((NDEV, L, K, n), F32), (NDEV * nb, K // tk, S // ts),
               pl.BlockSpec((ts, tk), lambda j, i, s: (s, first + i)),
               pl.BlockSpec((ts, tn), lambda j, i, s: (s, j)),
               pl.BlockSpec((1, 1, tk, tn), lambda j, i, s: (j // nb, l, i, j % nb)),
               TN, (ts, tk), (ts, tn), 2, ("parallel", "parallel", "arbitrary"), alias=buf, comm=comm)


def _mm_tn_out(a, b, l, L, buf, name):
    S, K = a.shape
    N = b.shape[1]
    kb = K // NDEV
    ts, tk, tn = min(S, 2048), _tile(K, 1024), _tile(N, 1024)
    per = tk // kb
    return _mm(name, a, b, jax.ShapeDtypeStruct((NDEV, L, kb, N), F32), (N // tn, K // tk, S // ts),
               pl.BlockSpec((ts, tk), lambda j, i, s: (s, i)),
               pl.BlockSpec((ts, tn), lambda j, i, s: (s, j)),
               pl.BlockSpec((per, 1, kb, tn), lambda j, i, s: (i, l, 0, j)),
               TN, (ts, tk), (ts, tn), 2, ("parallel", "parallel", "arbitrary"), alias=buf)


def _attn_bias(T):
    reach = max(w // 2 for w, _ in DILATED_PATTERNS)
    hb = -(-reach // T)
    i = np.arange(T)[:, None]
    j = np.arange(T)[None, :]
    tiles = []
    for d in range(-hb, hb + 1):
        rel = j + d * T - i
        mult = np.zeros((T, T), np.float64)
        for window, dil in DILATED_PATTERNS:
            radius = window // (2 * dil)
            mult += (rel % dil == 0) & (np.abs(rel) <= radius * dil)
        tiles.append(np.where(mult > 0, np.log(np.maximum(mult, 1.0)), NEG_INF))
    return jnp.asarray(np.stack(tiles), F32)


def _rope_tables(S):
    half = HEAD_DIM // 2
    pos = jnp.arange(S, dtype=F32)
    inv = ROPE_THETA ** (-jnp.arange(half, dtype=F32) / half)
    ang = pos[:, None] * inv[None, :]
    cos, sin = jnp.cos(ang), jnp.sin(ang)
    return jnp.concatenate([cos, cos], axis=-1), jnp.concatenate([-sin, sin], axis=-1)


def _rope_apply(t, cosf, sinf, heads, sign):
    outs = []
    for hh in range(heads):
        th = t[:, hh * HEAD_DIM:(hh + 1) * HEAD_DIM]
        outs.append(th * cosf + sign * (pltpu.roll(th, HEAD_DIM // 2, 1) * sinf))
    return outs


def _rope_qkv(proj, cosf, sinf, W, name):
    S = proj.shape[0]
    tr = min(S, ROW_TILE)
    heads = W // HEAD_DIM

    def body(q_ref, k_ref, v_ref, c_ref, s_ref, qo_ref, ko_ref, vo_ref):
        cosf_v, sinf_v = c_ref[...], s_ref[...]
        for src, dst, mult in ((q_ref, qo_ref, HEAD_DIM ** -0.5), (k_ref, ko_ref, 1.0)):
            for hh, val in enumerate(_rope_apply(src[...], cosf_v, sinf_v, heads, 1.0)):
                dst[:, hh * HEAD_DIM:(hh + 1) * HEAD_DIM] = (val * mult).astype(BF16)
        vo_ref[...] = v_ref[...].astype(BF16)

    piece = lambda p: pl.BlockSpec((tr, W), lambda i: (i, p))
    tab = pl.BlockSpec((tr, HEAD_DIM), lambda i: (i, 0))
    out = pl.BlockSpec((tr, W), lambda i: (i, 0))
    shp = jax.ShapeDtypeStruct((S, W), BF16)
    return pl.pallas_call(
        body, name=name, grid=(S // tr,),
        in_specs=[piece(0), piece(1), piece(2), tab, tab], out_specs=[out] * 3, out_shape=[shp] * 3,
        compiler_params=_cp("parallel"),
    )(proj, proj, proj, cosf, sinf)


def _attn_fwd(q, k, v, bias, name, comm=None):
    S, W = q.shape
    H = W // HEAD_DIM
    nd, T, _ = bias.shape
    hb, nq = nd // 2, S // T
    scale = HEAD_DIM ** -0.5
    hp = min(H, HEADS_PER_STEP)
    rc = min(T, ATTN_ROW_CHUNK)
    wp = hp * HEAD_DIM

    def body(q_ref, k_ref, v_ref, b_ref, o_ref, lse_ref, m_s, l_s, acc_s):
        i, d = pl.program_id(1), pl.program_id(2)
        j = i + d - hb

        @pl.when(d == 0)
        def _():
            m_s[...] = jnp.full(m_s.shape, -jnp.inf, F32)
            l_s[...] = jnp.zeros(l_s.shape, F32)
            acc_s[...] = jnp.zeros(acc_s.shape, F32)

        @pl.when((j >= 0) & (j < nq))
        def _():
            items = [(hh, c) for hh in range(hp) for c in range(T // rc)]

            def scores(item):
                hh, c = item
                cols, rows = slice(hh * HEAD_DIM, (hh + 1) * HEAD_DIM), slice(c * rc, (c + 1) * rc)
                return (lax.dot_general(q_ref[rows, cols], k_ref[:, cols], NT, preferred_element_type=F32)
                        + b_ref[d, rows, :])

            def weighted_values(item, p, alpha):
                hh, c = item
                cols, rows = slice(hh * HEAD_DIM, (hh + 1) * HEAD_DIM), slice(c * rc, (c + 1) * rc)
                acc_s[rows, cols] = alpha * acc_s[rows, cols] + jnp.dot(p, v_ref[:, cols],
                                                                        preferred_element_type=F32)

            s_next, pending = scores(items[0]), None
            for n, (hh, c) in enumerate(items):
                rows = slice(c * rc, (c + 1) * rc)
                s = s_next
                if n + 1 < len(items):
                    s_next = scores(items[n + 1])
                if pending is not None:
                    weighted_values(*pending)
                parts = [s[:, t * LANES:(t + 1) * LANES] for t in range(T // LANES)]
                m_old = m_s[hh, rows, :]
                m_cur = jnp.max(functools.reduce(jnp.maximum, parts), axis=1, keepdims=True)
                m_new = jnp.maximum(m_old, m_cur)
                alpha = jnp.exp(m_old - m_new)
                ps = [jnp.exp(part - m_new) for part in parts]
                l_s[hh, rows, :] = alpha * l_s[hh, rows, :] + functools.reduce(jnp.add, ps)
                m_s[hh, rows, :] = m_new
                pending = ((hh, c), jnp.concatenate(ps, axis=1).astype(BF16), alpha)
            weighted_values(*pending)

        @pl.when(d == nd - 1)
        def _():
            for hh in range(hp):
                cols = slice(hh * HEAD_DIM, (hh + 1) * HEAD_DIM)
                l = jnp.sum(l_s[hh], axis=1, keepdims=True)
                o_ref[:, cols] = acc_s[:, cols] / l
                lse_ref[hh] = m_s[hh][:, :1] + jnp.log(l)

    kv = pl.BlockSpec((T, wp), lambda h, i, d: (jnp.clip(i + d - hb, 0, nq - 1), h))
    return _hosted_call(
        body, [q, k, v, bias], name=name, grid=(H // hp, nq, nd),
        in_specs=[pl.BlockSpec((T, wp), lambda h, i, d: (i, h)), kv, kv,
                  pl.BlockSpec((nd, T, T), lambda h, i, d: (0, 0, 0))],
        out_specs=[pl.BlockSpec((T, wp), lambda h, i, d: (i, h)),
                   pl.BlockSpec((hp, T, 1), lambda h, i, d: (h, i, 0))],
        out_shape=[jax.ShapeDtypeStruct((S, W), F32), jax.ShapeDtypeStruct((H, S, 1), F32)],
        scratch_shapes=[pltpu.VMEM((hp, T, LANES), F32), pltpu.VMEM((hp, T, LANES), F32),
                        pltpu.VMEM((T, wp), F32)],
        sem=("parallel", "parallel", "arbitrary"), comm=comm)


def _attn_bwd(q, k, v, do, lse, delta, bias, name, comm=None):
    S, W = q.shape
    H = W // HEAD_DIM
    nd, T, _ = bias.shape
    hb, nq = nd // 2, S // T
    scale = HEAD_DIM ** -0.5
    hp = min(H, HEADS_PER_STEP)
    rc = min(T, ATTN_ROW_CHUNK)
    wp = hp * HEAD_DIM

    def body(q_ref, do_ref, lse_ref, dl_ref, k_ref, v_ref, b_ref, dq_ref, dk_ref, dv_ref):
        j, d = pl.program_id(1), pl.program_id(2)
        i = j + d - hb

        @pl.when((j == 0) & (d == 0))
        def _():
            dq_ref[...] = jnp.zeros(dq_ref.shape, F32)

        @pl.when(d == 0)
        def _():
            dk_ref[...] = jnp.zeros(dk_ref.shape, F32)
            dv_ref[...] = jnp.zeros(dv_ref.shape, F32)

        @pl.when((i >= 0) & (i < nq))
        def _():
            items = [(hh, c) for hh in range(hp) for c in range(T // rc)]

            def slices(item):
                hh, c = item
                return slice(hh * HEAD_DIM, (hh + 1) * HEAD_DIM), slice(c * rc, (c + 1) * rc)

            def products(item):
                cols, rows = slices(item)
                s = (lax.dot_general(q_ref[rows, cols], k_ref[:, cols], NT, preferred_element_type=F32)
                     + b_ref[nd - 1 - d, rows, :])
                dp = lax.dot_general(do_ref[rows, cols], v_ref[:, cols], NT, preferred_element_type=F32)
                return s, dp

            def gradients(item, p, ds):
                cols, rows = slices(item)
                dv_ref[:, cols] += lax.dot_general(p, do_ref[rows, cols], TN, preferred_element_type=F32)
                dk_ref[:, cols] += lax.dot_general(ds, q_ref[rows, cols], TN, preferred_element_type=F32)
                q_rows = pl.ds(pl.multiple_of(i * T + item[1] * rc, rc), rc)
                dq_ref[q_rows, cols] += jnp.dot(ds, k_ref[:, cols], preferred_element_type=F32) * scale

            nxt, pending = products(items[0]), None
            for n, item in enumerate(items):
                s, dp = nxt
                if n + 1 < len(items):
                    nxt = products(items[n + 1])
                if pending is not None:
                    gradients(*pending)
                _, rows = slices(item)
                p = jnp.exp(s - lse_ref[item[0], rows, :])
                ds = p * (dp - dl_ref[item[0], rows, :])
                pending = (item, p.astype(BF16), ds.astype(BF16))
            gradients(*pending)

    qi = lambda h, j, d: (jnp.clip(j + d - hb, 0, nq - 1), h)
    qs = pl.BlockSpec((T, wp), qi)
    col = pl.BlockSpec((hp, T, 1), lambda h, j, d: (h, jnp.clip(j + d - hb, 0, nq - 1), 0))
    kv = pl.BlockSpec((T, wp), lambda h, j, d: (j, h))
    shp = jax.ShapeDtypeStruct((S, W), F32)
    return _hosted_call(
        body, [q, do, lse, delta, k, v, bias], name=name, grid=(H // hp, nq, nd),
        in_specs=[qs, qs, col, col, kv, kv, pl.BlockSpec((nd, T, T), lambda h, j, d: (0, 0, 0))],
        out_specs=[pl.BlockSpec((S, wp), lambda h, j, d: (0, h)), kv, kv],
        out_shape=[shp, shp, shp],
        sem=("parallel", "arbitrary", "arbitrary"), comm=comm)


def _halo_specs(S, tr, W, piece):
    per, last = tr // 8, S // 8 - 1
    prev = pl.BlockSpec((8, W), lambda i: (jnp.maximum(i * per - 1, 0), piece))
    nxt = pl.BlockSpec((8, W), lambda i: (jnp.minimum((i + 1) * per, last), piece))
    return prev, nxt


def _shifted(t, before, after, tr):
    rows = lax.broadcasted_iota(jnp.int32, (tr, 1), 0)
    prev = jnp.where(rows == 0, before, pltpu.roll(t, 1, 0))
    nxt = jnp.where(rows == tr - 1, after, pltpu.roll(t, tr - 1, 0))
    return prev, nxt


def _ab_mix(attn, proj, conv_w, W, name):
    S = attn.shape[0]
    tr = min(S, ROW_TILE)
    nsteps = S // tr

    def body(a_ref, za_ref, ub_ref, gb_ref, gc_ref, zb_ref, ubp, ubn, gcp, gcn, w_ref, y_ref):
        i = pl.program_id(0)
        t = gc_ref[...] * ub_ref[...]
        before = jnp.where(i == 0, 0.0, (gcp[...] * ubp[...])[7:8, :])
        after = jnp.where(i == nsteps - 1, 0.0, (gcn[...] * ubn[...])[0:1, :])
        t_prev, t_next = _shifted(t, before, after, tr)
        w = w_ref[...]
        cv = w[0:1, :] * t_prev + w[1:2, :] * t + w[2:3, :] * t_next
        silu_a, _ = _silu_and_grad(za_ref[...])
        silu_b, _ = _silu_and_grad(zb_ref[...])
        y_ref[:, :W] = (a_ref[...] * silu_a).astype(BF16)
        y_ref[:, W:] = (gb_ref[...] * cv * silu_b).astype(BF16)

    piece = lambda p: pl.BlockSpec((tr, W), lambda i: (i, p))
    ubp, ubn = _halo_specs(S, tr, W, 4)
    gcp, gcn = _halo_specs(S, tr, W, 6)
    return pl.pallas_call(
        body, name=name, grid=(nsteps,),
        in_specs=[pl.BlockSpec((tr, W), lambda i: (i, 0)), piece(3), piece(4), piece(5), piece(6), piece(7),
                  ubp, ubn, gcp, gcn, pl.BlockSpec((3, W), lambda i: (0, 0))],
        out_specs=pl.BlockSpec((tr, 2 * W), lambda i: (i, 0)),
        out_shape=jax.ShapeDtypeStruct((S, 2 * W), BF16),
        compiler_params=_cp("parallel"),
    )(attn, proj, proj, proj, proj, proj, proj, proj, proj, proj, conv_w)


def _dattn_prep(dy, proj, attn, W, name):
    S = attn.shape[0]
    H = W // HEAD_DIM
    tr = min(S, ROW_TILE)

    def body(dy_ref, za_ref, a_ref, do_ref, dl_ref):
        silu_a, _ = _silu_and_grad(za_ref[...])
        do = dy_ref[...] * silu_a
        do_ref[...] = do.astype(BF16)
        prod = do * a_ref[...]
        for hh in range(H):
            dl_ref[hh] = jnp.sum(prod[:, hh * HEAD_DIM:(hh + 1) * HEAD_DIM], axis=1, keepdims=True)

    row = pl.BlockSpec((tr, W), lambda i: (i, 0))
    return pl.pallas_call(
        body, name=name, grid=(S // tr,),
        in_specs=[row, pl.BlockSpec((tr, W), lambda i: (i, 3)), row],
        out_specs=[row, pl.BlockSpec((H, tr, 1), lambda i: (0, i, 0))],
        out_shape=[jax.ShapeDtypeStruct((S, W), BF16), jax.ShapeDtypeStruct((H, S, 1), F32)],
        compiler_params=_cp("parallel"),
    )(dy, proj, attn)


def _ab_bwd(dy, attn, proj, dqr, dkr, dv, cosf, sinf, conv_w, W, name):
    S = attn.shape[0]
    tr = min(S, ROW_TILE // 2)
    nsteps = S // tr
    heads = W // HEAD_DIM

    def body(dya_ref, dyb_ref, a_ref, za_ref, ub_ref, gb_ref, gc_ref, zb_ref, dq_ref, dk_ref, dv_ref,
             c_ref, s_ref, w_ref, dybp, dybn, gbp, gbn, zbp, zbn, ubp, ubn, gcp, gcn,
             dp_ref, dw_ref, acc):
        i = pl.program_id(0)
        first, last = i == 0, i == nsteps - 1
        w = w_ref[...]
        w0, w1, w2 = w[0:1, :], w[1:2, :], w[2:3, :]
        ub, gb, gc, zb = ub_ref[...], gb_ref[...], gc_ref[...], zb_ref[...]
        dyb = dyb_ref[...]
        silu_a, dsilu_a = _silu_and_grad(za_ref[...])
        silu_b, dsilu_b = _silu_and_grad(zb)
        t = gc * ub
        t_prev, t_next = _shifted(t, jnp.where(first, 0.0, (gcp[...] * ubp[...])[7:8, :]),
                                  jnp.where(last, 0.0, (gcn[...] * ubn[...])[0:1, :]), tr)
        cv = w0 * t_prev + w1 * t + w2 * t_next
        dcv = dyb * gb * silu_b
        halo_p = dybp[...] * gbp[...] * _silu_and_grad(zbp[...])[0]
        halo_n = dybn[...] * gbn[...] * _silu_and_grad(zbn[...])[0]
        dcv_prev, dcv_next = _shifted(dcv, jnp.where(first, 0.0, halo_p[7:8, :]),
                                      jnp.where(last, 0.0, halo_n[0:1, :]), tr)
        dt = w0 * dcv_next + w1 * dcv + w2 * dcv_prev
        cosf_v, sinf_v = c_ref[...], s_ref[...]
        for src, base in ((dq_ref, 0), (dk_ref, W)):
            for hh, val in enumerate(_rope_apply(src[...], cosf_v, sinf_v, heads, -1.0)):
                dp_ref[:, base + hh * HEAD_DIM:base + (hh + 1) * HEAD_DIM] = val.astype(BF16)
        dp_ref[:, 2 * W:3 * W] = dv_ref[...].astype(BF16)
        dp_ref[:, 3 * W:4 * W] = (dya_ref[...] * a_ref[...] * dsilu_a).astype(BF16)
        dp_ref[:, 4 * W:5 * W] = (dt * gc).astype(BF16)
        dp_ref[:, 5 * W:6 * W] = (dyb * cv * silu_b).astype(BF16)
        dp_ref[:, 6 * W:7 * W] = (dt * ub).astype(BF16)
        dp_ref[:, 7 * W:8 * W] = (dyb * gb * cv * dsilu_b).astype(BF16)
        tap = lax.broadcasted_iota(jnp.int32, (8, 1), 0)
        part = (jnp.where(tap == 0, jnp.sum(dcv * t_prev, axis=0, keepdims=True), 0.0)
                + jnp.where(tap == 1, jnp.sum(dcv * t, axis=0, keepdims=True), 0.0)
                + jnp.where(tap == 2, jnp.sum(dcv * t_next, axis=0, keepdims=True), 0.0))

        @pl.when(first)
        def _():
            acc[...] = part

        @pl.when(i > 0)
        def _():
            acc[...] += part

        @pl.when(last)
        def _():
            dw_ref[...] = acc[...]

    row = pl.BlockSpec((tr, W), lambda i: (i, 0))
    piece = lambda p: pl.BlockSpec((tr, W), lambda i: (i, p))
    tab = pl.BlockSpec((tr, HEAD_DIM), lambda i: (i, 0))
    dybp, dybn = _halo_specs(S, tr, W, 1)
    gbp, gbn = _halo_specs(S, tr, W, 5)
    zbp, zbn = _halo_specs(S, tr, W, 7)
    ubp, ubn = _halo_specs(S, tr, W, 4)
    gcp, gcn = _halo_specs(S, tr, W, 6)
    return pl.pallas_call(
        body, name=name, grid=(nsteps,),
        in_specs=[piece(0), piece(1), row, piece(3), piece(4), piece(5), piece(6), piece(7), row, row, row,
                  tab, tab, pl.BlockSpec((3, W), lambda i: (0, 0)),
                  dybp, dybn, gbp, gbn, zbp, zbn, ubp, ubn, gcp, gcn],
        out_specs=[pl.BlockSpec((tr, 8 * W), lambda i: (i, 0)), pl.BlockSpec((8, W), lambda i: (0, 0))],
        out_shape=[jax.ShapeDtypeStruct((S, 8 * W), BF16), jax.ShapeDtypeStruct((8, W), F32)],
        scratch_shapes=[pltpu.VMEM((8, W), F32)],
        compiler_params=_cp("arbitrary"),
    )(dy, dy, attn, proj, proj, proj, proj, proj, dqr, dkr, dv, cosf, sinf, conv_w,
      dy, dy, proj, proj, proj, proj, proj, proj, proj, proj)


def _sgu_centre(p_ref, vc_s, dvg_s, Dc):
    gw = Dc // C_GROUPS
    total = None
    for g in range(C_GROUPS):
        cs = slice(g * gw, (g + 1) * gw)
        vg, dvg = _gelu_and_grad(p_ref[:, Dc + g * gw:Dc + (g + 1) * gw])
        vc_s[:, cs] = vg
        if dvg_s is not None:
            dvg_s[:, cs] = dvg
        total = vg if total is None else total + vg
    mu = jnp.sum(total, axis=1, keepdims=True) * (1.0 / Dc)
    total = None
    for g in range(C_GROUPS):
        cs = slice(g * gw, (g + 1) * gw)
        vc = vc_s[:, cs] - mu
        vc_s[:, cs] = vc
        total = vc * vc if total is None else total + vc * vc
    return lax.rsqrt(jnp.sum(total, axis=1, keepdims=True) * (1.0 / Dc) + EPS)


def _sgu_fwd(proj, ln_g, ln_b, w_s, b_st, name, comm=None):
    S, Dc3 = proj.shape
    Dc = Dc3 // 3
    gw = Dc // C_GROUPS
    vec = pl.BlockSpec((1, Dc), lambda i: (0, 0))

    def body(p_ref, lng_ref, lnb_ref, ws_ref, bst_ref, y_ref, vc_s):
        rstd = _sgu_centre(p_ref, vc_s, None, Dc)
        bst = bst_ref[...]
        for g in range(C_GROUPS):
            cs = slice(g * gw, (g + 1) * gw)
            vn = (vc_s[:, cs] * rstd * lng_ref[:, cs] + lnb_ref[:, cs]).astype(BF16)
            mixed = jnp.dot(ws_ref[g].astype(BF16), vn, preferred_element_type=F32) + bst[:, g:g + 1]
            u, _ = _gelu_and_grad(p_ref[:, cs])
            sz, _ = _silu_and_grad(p_ref[:, 2 * Dc + g * gw:2 * Dc + (g + 1) * gw])
            y_ref[:, cs] = (u * mixed * sz).astype(BF16)

    return _hosted_call(
        body, [proj, ln_g, ln_b, w_s, b_st], name=name, grid=(S // C_CHUNK,),
        in_specs=[pl.BlockSpec((C_CHUNK, Dc3), lambda i: (i, 0)), vec, vec,
                  pl.BlockSpec((C_GROUPS, C_CHUNK, C_CHUNK), lambda i: (0, 0, 0)),
                  pl.BlockSpec((C_CHUNK, C_GROUPS), lambda i: (0, 0))],
        out_specs=pl.BlockSpec((C_CHUNK, Dc), lambda i: (i, 0)),
        out_shape=jax.ShapeDtypeStruct((S, Dc), BF16),
        scratch_shapes=[pltpu.VMEM((C_CHUNK, Dc), F32)],
        sem=("parallel",), comm=comm)


def _sgu_bwd(proj, dy, ln_g, ln_b, w_s, w_st, b_st, name):
    S, Dc3 = proj.shape
    Dc = Dc3 // 3
    gw = Dc // C_GROUPS
    nsteps = S // C_CHUNK
    vec = pl.BlockSpec((1, Dc), lambda i: (0, 0))
    wspec = pl.BlockSpec((C_GROUPS, C_CHUNK, C_CHUNK), lambda i: (0, 0, 0))

    def body(p_ref, dy_ref, lng_ref, lnb_ref, ws_ref, wst_ref, bst_ref,
             dp_ref, dws_ref, dbs_ref, dlg_ref, dlb_ref, acc_w, acc_b, acc_g, acc_lb, vc_s, dvg_s, dvh_s):
        i = pl.program_id(0)

        @pl.when(i == 0)
        def _():
            acc_w[...] = jnp.zeros(acc_w.shape, F32)
            acc_b[...] = jnp.zeros(acc_b.shape, F32)
            acc_g[...] = jnp.zeros(acc_g.shape, F32)
            acc_lb[...] = jnp.zeros(acc_lb.shape, F32)

        rstd = _sgu_centre(p_ref, vc_s, dvg_s, Dc)
        bst = bst_ref[...]
        octets = lambda t: jnp.sum(t.reshape(C_CHUNK // 8, 8, gw), axis=0)
        t1, t2 = None, None
        for g in range(C_GROUPS):
            cs = slice(g * gw, (g + 1) * gw)
            zs = slice(2 * Dc + g * gw, 2 * Dc + (g + 1) * gw)
            vhat = vc_s[:, cs] * rstd
            vn = (vhat * lng_ref[:, cs] + lnb_ref[:, cs]).astype(BF16)
            mixed = jnp.dot(ws_ref[g].astype(BF16), vn, preferred_element_type=F32) + bst[:, g:g + 1]
            u, du = _gelu_and_grad(p_ref[:, cs])
            sz, dsz = _silu_and_grad(p_ref[:, zs])
            dy = dy_ref[:, cs]
            dmixed = dy * u * sz
            dmb = dmixed.astype(BF16)
            acc_w[g] += lax.dot_general(dmb, vn, NT, preferred_element_type=F32)
            acc_b[g] += dmixed
            dvn = jnp.dot(wst_ref[g].astype(BF16), dmb, preferred_element_type=F32)
            acc_g[:, cs] += octets(dvn * vhat)
            acc_lb[:, cs] += octets(dvn)
            dvh = dvn * lng_ref[:, cs]
            dvh_s[:, cs] = dvh
            t1 = dvh if t1 is None else t1 + dvh
            t2 = dvh * vhat if t2 is None else t2 + dvh * vhat
            dp_ref[:, cs] = (dy * mixed * sz * du).astype(BF16)
            dp_ref[:, zs] = (dy * u * mixed * dsz).astype(BF16)
        m1 = jnp.sum(t1, axis=1, keepdims=True) * (1.0 / Dc)
        m2 = jnp.sum(t2, axis=1, keepdims=True) * (1.0 / Dc)
        for g in range(C_GROUPS):
            cs = slice(g * gw, (g + 1) * gw)
            dvgelu = rstd * (dvh_s[:, cs] - m1 - (vc_s[:, cs] * rstd) * m2)
            dp_ref[:, Dc + g * gw:Dc + (g + 1) * gw] = (dvgelu * dvg_s[:, cs]).astype(BF16)

        @pl.when(i == nsteps - 1)
        def _():
            dws_ref[...] = acc_w[...]
            for g in range(C_GROUPS):
                dbs_ref[g] = jnp.sum(acc_b[g], axis=1, keepdims=True)
            dlg_ref[...] = jnp.sum(acc_g[...], axis=0, keepdims=True)
            dlb_ref[...] = jnp.sum(acc_lb[...], axis=0, keepdims=True)

    v = jax.ShapeDtypeStruct((1, Dc), F32)
    return pl.pallas_call(
        body, name=name, grid=(nsteps,),
        in_specs=[pl.BlockSpec((C_CHUNK, Dc3), lambda i: (i, 0)), pl.BlockSpec((C_CHUNK, Dc), lambda i: (i, 0)),
                  vec, vec, wspec, wspec, pl.BlockSpec((C_CHUNK, C_GROUPS), lambda i: (0, 0))],
        out_specs=[pl.BlockSpec((C_CHUNK, Dc3), lambda i: (i, 0)), wspec,
                   pl.BlockSpec((C_GROUPS, C_CHUNK, 1), lambda i: (0, 0, 0)), vec, vec],
        out_shape=[jax.ShapeDtypeStruct((S, Dc3), BF16),
                   jax.ShapeDtypeStruct((C_GROUPS, C_CHUNK, C_CHUNK), F32),
                   jax.ShapeDtypeStruct((C_GROUPS, C_CHUNK, 1), F32), v, v],
        scratch_shapes=[pltpu.VMEM((C_GROUPS, C_CHUNK, C_CHUNK), F32), pltpu.VMEM((C_GROUPS, C_CHUNK, gw), F32),
                        pltpu.VMEM((8, Dc), F32), pltpu.VMEM((8, Dc), F32)]
                       + [pltpu.VMEM((C_CHUNK, Dc), F32)] * 3,
        compiler_params=_cp("arbitrary"),
    )(proj, dy, ln_g, ln_b, w_s, w_st, b_st)


PACK_COLS = 1024
PACK_ROWS = 64


def _pack(vectors, rows=PACK_ROWS):
    flat = jnp.concatenate([v.reshape(-1) for v in vectors])
    pad = (-flat.shape[0]) % (PACK_COLS * rows)
    return jnp.pad(flat, (0, pad)).reshape(-1, PACK_COLS)


def _unshard(g, off, shape):
    L, rest = shape[0], shape[1:]
    size = int(np.prod(shape))
    piece = g[:, off:off + size].reshape((NDEV,) + tuple(shape))
    nd = piece.ndim
    perm = tuple(range(1, nd - 1)) + (0, nd - 1)
    full = jnp.transpose(piece, perm)
    return full.reshape(tuple(shape[:-1]) + (NDEV * shape[-1],)), off + size


def kernel(x, c, ab_norm_g, ab_w_mod, ab_b_mod, ab_w_in, ab_conv_w, ab_w_out, sg_norm_g, sg_w_mod, sg_b_mod, sg_w_in, sg_ln_g, sg_ln_b, sg_w_s, sg_b_s, sg_w_out, final_norm_g, loss_target, m_ab_norm_g, m_ab_w_mod, m_ab_b_mod, m_ab_w_in, m_ab_conv_w, m_ab_w_out, m_sg_norm_g, m_sg_w_mod, m_sg_b_mod, m_sg_w_in, m_sg_ln_g, m_sg_ln_b, m_sg_w_s, m_sg_b_s, m_sg_w_out, m_final_norm_g, v_ab_norm_g, v_ab_w_mod, v_ab_b_mod, v_ab_w_in, v_ab_conv_w, v_ab_w_out, v_sg_norm_g, v_sg_w_mod, v_sg_b_mod, v_sg_w_in, v_sg_ln_g, v_sg_ln_b, v_sg_w_s, v_sg_b_s, v_sg_w_out, v_final_norm_g):
    _, S, D = x.shape
    L = ab_norm_g.shape[0]
    W = ab_conv_w.shape[2] * NDEV
    n_ab, n_sg = ab_w_in.shape[2], sg_w_in.shape[2]
    n_mod = ab_w_mod.shape[2]
    kb = ab_w_out.shape[1]
    xi, yi, ci = _position()
    dev = 4 * xi + 2 * yi + ci
    x2, tgt = x.reshape(S, D), loss_target.reshape(S, D)

    small = [c, ab_conv_w, sg_norm_g, sg_ln_g, sg_ln_b]
    (g1,) = _comm_only(_GatherDirect([_pack(small, 8)]), "ag_small")
    g1 = g1.reshape(NDEV, -1)
    c_all = g1[:, :D]
    off = D
    conv_full, off = _unshard(g1, off, ab_conv_w.shape)
    sg_norm_full, off = _unshard(g1, off, sg_norm_g.shape)
    ln_g_full, off = _unshard(g1, off, sg_ln_g.shape)
    ln_b_full, off = _unshard(g1, off, sg_ln_b.shape)

    ab_b_cols = lax.dynamic_slice_in_dim(ab_b_mod, dev * n_mod, n_mod, axis=1)
    m_ab = _mod_fwd(c_all, ab_w_mod, ab_b_cols.reshape(L, 1, n_mod), "mod_fwd_ab")
    m_sg = _mod_fwd(c_all, sg_w_mod, sg_b_mod.reshape(L, 1, n_mod), "mod_fwd_sg")
    m_part = jnp.stack([m_ab, m_sg]).transpose(2, 0, 1, 3).reshape(NDEV, 2 * L * n_mod)
    (g2,) = _comm_only(_GatherDirect([m_part]), "ag_mod")
    mine = lax.dynamic_index_in_dim(g2, dev, axis=1, keepdims=False)
    mods = mine.reshape(NDEV, 2, L, n_mod).transpose(1, 2, 0, 3).reshape(2, L, 3 * D)

    def mod_of(kind, i):
        m = mods[kind, i]
        return m[:D].reshape(1, D), m[D:2 * D].reshape(1, D), m[2 * D:].reshape(1, D)

    big_w = [[(ab_w_in, m_ab_w_in, v_ab_w_in), (ab_w_out, m_ab_w_out, v_ab_w_out)],
             [(sg_w_in, m_sg_w_in, v_sg_w_in), (sg_w_out, m_sg_w_out, v_sg_w_out)]]
    big_names = [["ab_w_in", "ab_w_out"], ["sg_w_in", "sg_w_out"]]
    n_layers = 2 * L
    shards = [[big_w[layer % 2][k][0][layer // 2].astype(BF16) for k in range(2)] for layer in range(n_layers)]
    gathered = {}

    def gather_of(keys):
        keys = [key for key in keys if key[0] < n_layers and key not in gathered]
        return keys, (_Gather([shards[layer][k] for layer, k in keys]) if keys else None)

    def keep_gathered(keys, res):
        for (layer, k), g in zip(keys, res):
            gathered[(layer, k)] = g.reshape((NDEV, 1, D, g.shape[-1]) if k == 0 else (NDEV, 1, kb, D))

    keys, comm = gather_of([(0, 0)])
    keep_gathered(keys, _comm_only(comm, "ag_w_in_layer0"))

    cosf, sinf = _rope_tables(S)
    T = min(S, ATTN_TILE)
    bias = _attn_bias(T)
    norm_g = [ab_norm_g, sg_norm_full]
    w_s_t = jnp.swapaxes(sg_w_s, -1, -2)
    b_s_t = jnp.swapaxes(sg_b_s, -1, -2)

    saved = []
    x_cur, res, gate_prev = x2, None, None
    for layer in range(2 * L):
        kind, i = layer % 2, layer // 2
        tag = f"{'ab' if kind == 0 else 'sg'}{i}"
        shift, scale, gate = mod_of(kind, i)
        g = norm_g[kind][i].reshape(1, D)
        xl, h = _pre(x_cur, res, gate_prev, g, scale, shift, f"pre_{tag}")
        if kind == 0:
            keys, comm = gather_of([(layer, 1), (layer + 1, 0)])
        else:
            keys, comm = gather_of([(layer, 1), (layer + 1, 1)])
        if comm is None:
            proj = _mm_nn_in(h, gathered[(layer, 0)], 0, f"proj_{tag}")
        else:
            proj, got = _mm_nn_in(h, gathered[(layer, 0)], 0, f"proj_{tag}", comm)
            keep_gathered(keys, got)
        rec = dict(xl=xl, h=h, proj=proj, g=g, scale=scale, gate=gate)
        ahead = [(layer + 2, 0)] if layer + 2 < n_layers else [(layer + 1, 1)]
        keys, comm = gather_of(ahead if kind == 0 else [])
        if kind == 0:
            qr, kr, vb = _rope_qkv(proj, cosf, sinf, W, f"rope_{tag}")
            (attn, lse), got = _attn_fwd(qr, kr, vb, bias, f"attn_{tag}", comm)
            y = _ab_mix(attn, proj, conv_full[i], W, f"mix_{tag}")
            rec.update(qr=qr, kr=kr, vb=vb, attn=attn, lse=lse)
        else:
            y, got = _sgu_fwd(proj, ln_g_full[i].reshape(1, D), ln_b_full[i].reshape(1, D), sg_w_s[i], b_s_t[i],
                              f"sgu_{tag}", comm)
        keep_gathered(keys, got)
        out = _mm_nn_out(y, gathered[(layer, 1)], 0, f"out_{tag}")
        rec.update(y=y, out=out)
        saved.append(rec)
        x_cur, res, gate_prev = xl, out, gate

    dx, loss_part, d_final_g = _loss_head(x_cur, res, gate_prev, final_norm_g.reshape(1, D), tgt, "loss_head")
    loss = lax.psum(loss_part[0, 0], ("x", "y", "c"))

    c_idx = ci.reshape(1).astype(jnp.int32)
    big_res = {}
    pending = None

    def sum_and_update(done, k, from_chips, part=(0, 1)):
        nm = big_names[done % 2][k]
        w, m, v = big_w[done % 2][k]
        flat = lambda a: a.reshape(L * a.shape[1], a.shape[2])
        row0 = (done // 2) * w.shape[1] + part[0] * (w.shape[1] // part[1])
        big_res[nm] = _sum_adam(from_chips, flat(w), flat(m), flat(v), row0, big_res.get(nm),
                                f"adam_{nm}{done // 2}_{part[0]}")

    def finish_layer(done, from_chips):
        for k in range(2):
            sum_and_update(done, k, from_chips[k])

    dm = [[None] * L, [None] * L]
    d_norm = [[None] * L, [None] * L]
    d_conv, d_lng, d_lnb, d_ws, d_bs = [None] * L, [None] * L, [None] * L, [None] * L, [None] * L
    for layer in reversed(range(2 * L)):
        kind, i = layer % 2, layer // 2
        tag = f"{'ab' if kind == 0 else 'sg'}{i}"
        rec = saved[layer]
        w_in_l, w_out_l = gathered[(layer, 0)], gathered[(layer, 1)]
        dout, dgate = _post_bwd(dx, rec["out"], rec["gate"], f"post_bwd_{tag}")
        dy = _mm_nt_out(dout, w_out_l, 0, f"dy_{tag}")
        dwo = _mm_tn_out(rec["y"], dout, 0, 1, None, f"dwout_{tag}")
        if kind == 0:
            do, delta = _dattn_prep(dy, rec["proj"], rec["attn"], W, f"dattn_{tag}")
            comm = _ToChips(pending[1]) if pending else None
            (dqr, dkr, dvv), got = _attn_bwd(rec["qr"], rec["kr"], rec["vb"], do, rec["lse"], delta, bias,
                                             f"attn_bwd_{tag}", comm)
            if pending:
                finish_layer(pending[0], got)
            dproj, dcw = _ab_bwd(dy, rec["attn"], rec["proj"], dqr, dkr, dvv, cosf, sinf, conv_full[i], W,
                                 f"mix_bwd_{tag}")
            d_conv[i] = dcw[:3]
            earlier = None
        else:
            dproj, d_ws[i], dbs, d_lng[i], d_lnb[i] = _sgu_bwd(
                rec["proj"], dy, ln_g_full[i].reshape(1, D), ln_b_full[i].reshape(1, D),
                sg_w_s[i], w_s_t[i], b_s_t[i], f"sgu_bwd_{tag}")
            d_bs[i] = dbs.reshape(C_GROUPS, C_CHUNK)
            earlier = pending
        if layer == 0:
            h_l = rec["h"]
            dwi_a = _mm_tn_in(h_l, dproj, 0, 1, None, f"dwin_a_{tag}", part=(0, 2))
            g_a = [dwi_a.reshape(NDEV, D // 2, -1), dwo.reshape(NDEV, kb, D)]
            dwi_b, sib_a = _mm_tn_in(h_l, dproj, 0, 1, None, f"dwin_b_{tag}", _ToSibling(g_a), part=(1, 2))
            p_a = [_add_sibling(g, r, c_idx, f"rs_add_a{k}_{tag}") for k, (g, r) in enumerate(zip(g_a, sib_a))]
            g_b = [dwi_b.reshape(NDEV, D // 2, -1)]
            both = _Both(_ToSibling(g_b), _ToChips(p_a))
            dh, got = _mm_nt_in(dproj, w_in_l, 0, f"dh_{tag}", both)
            sib_b, chips_a = both.split_results(got)
            p_b = [_add_sibling(g_b[0], sib_b[0], c_idx, f"rs_add_b_{tag}")]
            (dx, dshift, dscale, d_norm[kind][i]), chips_b = _pre_bwd(
                rec["xl"], dh, dx, rec["g"], rec["scale"], f"pre_bwd_{tag}", _ToChips(p_b))
            sum_and_update(0, 0, chips_a[0], (0, 2))
            sum_and_update(0, 1, chips_a[1])
            sum_and_update(0, 0, chips_b[0], (1, 2))
            dm[kind][i] = jnp.concatenate([dshift, dscale, dgate], axis=1).reshape(3 * D)
            continue
        if earlier:
            dwi, got_in = _mm_tn_in(rec["h"], dproj, 0, 1, None, f"dwin_{tag}", _ToChips(earlier[1][:1]))
        else:
            dwi = _mm_tn_in(rec["h"], dproj, 0, 1, None, f"dwin_{tag}")
        grads = [dwi.reshape(NDEV, D, -1), dwo.reshape(NDEV, kb, D)]
        if earlier:
            both = _Both(_ToSibling(grads), _ToChips(earlier[1][1:]))
            dh, got = _mm_nt_in(dproj, w_in_l, 0, f"dh_{tag}", both)
            from_sibling, got_rest = both.split_results(got)
            sum_and_update(earlier[0], 0, got_in[0], (0, 2))
            sum_and_update(earlier[0], 0, got_rest[0], (1, 2))
            sum_and_update(earlier[0], 1, got_rest[1])
        else:
            dh, from_sibling = _mm_nt_in(dproj, w_in_l, 0, f"dh_{tag}", _ToSibling(grads))
        name_in, name_out = (f"rs_add_{nm}{i}" for nm in big_names[kind])
        if kind == 0:
            parts = [_add_sibling(grads[0], from_sibling[0], c_idx, f"{name_in}_{p}", (p, 2)) for p in range(2)]
        else:
            parts = [_add_sibling(grads[0], from_sibling[0], c_idx, name_in)]
        pending = (layer, parts + [_add_sibling(grads[1], from_sibling[1], c_idx, name_out)])
        (dx, dshift, dscale, d_norm[kind][i]), _ = _pre_bwd(
            rec["xl"], dh, dx, rec["g"], rec["scale"], f"pre_bwd_{tag}")
        dm[kind][i] = jnp.concatenate([dshift, dscale, dgate], axis=1).reshape(3 * D)
    grad_x = dx.reshape(1, S, D)
    for kind in range(2):
        for k in range(2):
            nm = big_names[kind][k]
            big_res[nm] = [o.reshape(big_w[kind][k][0].shape) for o in big_res[nm]]

    stack = lambda xs: jnp.stack(xs)
    pack_items = [stack(dm[0]), stack(dm[1]), stack(d_norm[0]).reshape(L, D), stack(d_conv),
                  stack(d_norm[1]).reshape(L, D), stack(d_lng).reshape(L, D), stack(d_lnb).reshape(L, D),
                  stack(d_ws), stack(d_bs), d_final_g]
    (g3,) = _comm_only(_Gather([_pack(pack_items)]), "ag_grads")
    P = g3.shape[1] * g3.shape[2]
    tot = _sum_rows(g3, "sum_small").reshape(P)
    g3 = g3.reshape(NDEV, P)
    sizes = [int(np.prod(p.shape)) for p in pack_items]
    offs = np.concatenate([[0], np.cumsum(sizes)]).tolist()
    seg = lambda k, shape: tot[offs[k]:offs[k + 1]].reshape(shape)

    def shard(full, n):
        return lax.dynamic_slice_in_dim(full, dev * n, n, axis=full.ndim - 1)

    g_ab_b_mod = seg(0, (L, 3 * D))
    g_sg_b_mod = shard(seg(1, (L, 3 * D)), n_mod)
    g_ab_norm = seg(2, (L, D))
    g_conv = shard(seg(3, (L, 3, W)), W // NDEV)
    g_sg_norm = shard(seg(4, (L, D)), kb)
    g_ln_g = shard(seg(5, (L, D)), kb)
    g_ln_b = shard(seg(6, (L, D)), kb)
    g_w_s = seg(7, sg_w_s.shape)
    g_b_s = seg(8, sg_b_s.shape)
    g_final = seg(9, (D,))

    small_w = [("ab_norm_g", g_ab_norm, ab_norm_g, m_ab_norm_g, v_ab_norm_g),
               ("ab_b_mod", g_ab_b_mod, ab_b_mod, m_ab_b_mod, v_ab_b_mod),
               ("ab_conv_w", g_conv, ab_conv_w, m_ab_conv_w, v_ab_conv_w),
               ("sg_norm_g", g_sg_norm, sg_norm_g, m_sg_norm_g, v_sg_norm_g),
               ("sg_b_mod", g_sg_b_mod, sg_b_mod, m_sg_b_mod, v_sg_b_mod),
               ("sg_ln_g", g_ln_g, sg_ln_g, m_sg_ln_g, v_sg_ln_g),
               ("sg_ln_b", g_ln_b, sg_ln_b, m_sg_ln_b, v_sg_ln_b),
               ("sg_w_s", g_w_s, sg_w_s, m_sg_w_s, v_sg_w_s),
               ("sg_b_s", g_b_s, sg_b_s, m_sg_b_s, v_sg_b_s),
               ("final_norm_g", g_final, final_norm_g, m_final_norm_g, v_final_norm_g)]
    packed = [_pack([t[k] for t in small_w]) for k in (1, 2, 3, 4)]
    upd = _adam_only(*packed, "adam_small")
    small_res = {}
    o = 0
    for nm, g, w, _, _ in small_w:
        size = int(np.prod(w.shape))
        small_res[nm] = [g] + [u.reshape(-1)[o:o + size].reshape(w.shape) for u in upd]
        o += size

    KP = 128
    sc_t = jnp.pad((c_all * jax.nn.sigmoid(c_all)).T, ((0, 0), (0, KP - NDEV)))
    mod_res = {}
    for kind, nm, (w, m, v) in ((0, "ab_w_mod", (ab_w_mod, m_ab_w_mod, v_ab_w_mod)),
                                (1, "sg_w_mod", (sg_w_mod, m_sg_w_mod, v_sg_w_mod))):
        dm_all = g3[:, offs[kind]:offs[kind + 1]].reshape(NDEV, L, 3 * D)
        cols = jnp.pad(shard(dm_all, n_mod).transpose(1, 0, 2), ((0, 0), (0, KP - NDEV), (0, 0)))
        mod_res[nm] = _wmod_grad_adam(sc_t, cols, w, m, v, f"adam_{nm}")

    order = ["ab_norm_g", "ab_w_mod", "ab_b_mod", "ab_w_in", "ab_conv_w", "ab_w_out", "sg_norm_g", "sg_w_mod",
             "sg_b_mod", "sg_w_in", "sg_ln_g", "sg_ln_b", "sg_w_s", "sg_b_s", "sg_w_out", "final_norm_g"]
    res = {**big_res, **small_res, **mod_res}
    outs = [loss, grad_x]
    for k in range(4):
        outs += [res[nm][k] for nm in order]
    return tuple(outs)
```

```python
import functools
import math

import numpy as np
import jax
import jax.numpy as jnp
from jax import lax
from jax.experimental import pallas as pl
from jax.experimental.pallas import tpu as pltpu

F32 = jnp.float32
BF16 = jnp.bfloat16

NDEV = 8
NCHIP = 4
EPS = 1e-6
HEAD_DIM = 128
ROPE_THETA = 10000.0
DILATED_PATTERNS = ((128, 1), (512, 4), (2048, 16))
NEG_INF = -1e30
C_CHUNK = 128
C_GROUPS = 8
ADAM_LR = 0.001
ADAM_B1 = 0.9
ADAM_B2 = 0.999
ADAM_EPS = 1e-08
ADAM_WD = 0.01
ADAM_STEP = 10
GELU_K = math.sqrt(2.0 / math.pi)
GELU_C = 0.044715

VMEM_LIMIT_BYTES = 56 * 1024 * 1024
ATTN_TILE = 512
HEADS_PER_STEP = 4
ATTN_ROW_CHUNK = 256
LANES = 128
ROW_TILE = 256
MESH = pl.DeviceIdType.MESH
ANY = pl.BlockSpec(memory_space=pl.ANY)


def _cp(*sem):
    return pltpu.CompilerParams(dimension_semantics=sem, vmem_limit_bytes=VMEM_LIMIT_BYTES)


def _sigmoid(z):
    return 0.5 * (jnp.tanh(0.5 * z) + 1.0)


def _silu_and_grad(z):
    s = _sigmoid(z)
    return z * s, s * (1.0 + z * (1.0 - s))


def _gelu_and_grad(x):
    x2 = x * x
    t = jnp.tanh(GELU_K * (x + GELU_C * x2 * x))
    g = 0.5 * x * (1.0 + t)
    dg = 0.5 * (1.0 + t) + 0.5 * x * (1.0 - t * t) * (GELU_K * (1.0 + 3.0 * GELU_C * x2))
    return g, dg


def _position():
    return lax.axis_index("x"), lax.axis_index("y"), lax.axis_index("c")


def _chips(x, y):
    return [(1 - x, y), (x, 1 - y), (1 - x, 1 - y)]


class _Gather:
    def __init__(self, arrs):
        n = len(arrs)
        self.arrs = list(arrs)
        self.out_shape = [jax.ShapeDtypeStruct((NDEV,) + a.shape, a.dtype) for a in arrs]
        self.scratch = [pltpu.SemaphoreType.DMA((n, 7)), pltpu.SemaphoreType.DMA((n, 7)),
                        pltpu.SemaphoreType.DMA((n,))]

    def _copies(self, ins, outs, sems, own=True):
        send_sems, recv_sems, local_sems = sems
        x, y, c = _position()

        def copy(a, k, block, to, src=None):
            dst = outs[a].at[4 * block[0] + 2 * block[1] + block[2]]
            return pltpu.make_async_remote_copy(
                src_ref=dst if src is None else src, dst_ref=dst,
                send_sem=send_sems.at[a, k], recv_sem=recv_sems.at[a, k],
                device_id=to, device_id_type=MESH)

        n = len(ins)
        me, sibling = (x, y, c), (x, y, 1 - c)
        mine, first = [], []
        if own:
            mine = [pltpu.make_async_copy(ins[a], outs[a].at[4 * x + 2 * y + c], local_sems.at[a])
                    for a in range(n)]
            for a in range(n):
                first.append(copy(a, 0, me, sibling, src=ins[a]))
                first += [copy(a, 1 + j, me, (*chip, c), src=ins[a]) for j, chip in enumerate(_chips(x, y))]
        return copy, mine, first

    def start(self, ins, outs, sems):
        _, mine, first = self._copies(ins, outs, sems)
        for cp in mine + first:
            cp.start()

    def middle(self, ins, outs, sems):
        copy = self._copies(ins, outs, sems, own=False)[0]
        x, y, c = _position()
        me, sibling = (x, y, c), (x, y, 1 - c)
        for j, chip in enumerate(_chips(x, y)):
            for a in range(len(ins)):
                copy(a, 1 + j, (*chip, c), me).wait_recv()
                copy(a, 4 + j, (*chip, c), sibling).start()

    def finish(self, ins, outs, sems):
        copy, mine, first = self._copies(ins, outs, sems)
        x, y, c = _position()
        me, sibling = (x, y, c), (x, y, 1 - c)
        passed = [copy(a, 4 + j, (*chip, c), sibling)
                  for j, chip in enumerate(_chips(x, y)) for a in range(len(ins))]
        for a in range(len(ins)):
            copy(a, 0, sibling, me).wait_recv()
            for j, chip in enumerate(_chips(x, y)):
                copy(a, 4 + j, (*chip, 1 - c), me).wait_recv()
        for cp in first + passed:
            cp.wait_send()
        for cp in mine:
            cp.wait()


class _GatherDirect:
    def __init__(self, arrs):
        n = len(arrs)
        self.arrs = list(arrs)
        self.out_shape = [jax.ShapeDtypeStruct((NDEV,) + a.shape, a.dtype) for a in arrs]
        self.scratch = [pltpu.SemaphoreType.DMA((n, 7)), pltpu.SemaphoreType.DMA((n, 7)),
                        pltpu.SemaphoreType.DMA((n,))]

    def _copies(self, ins, outs, sems, arrivals):
        send_sems, recv_sems, local_sems = sems
        x, y, c = _position()
        mine = [pltpu.make_async_copy(ins[a], outs[a].at[4 * x + 2 * y + c], local_sems.at[a])
                for a in range(len(ins))]
        sends, recvs = [], []
        for a in range(len(ins)):
            for k in range(1, NDEV):
                px = 1 - x if k & 4 else x
                py = 1 - y if k & 2 else y
                pc = 1 - c if k & 1 else c
                sends.append(pltpu.make_async_remote_copy(
                    src_ref=ins[a], dst_ref=outs[a].at[4 * x + 2 * y + c],
                    send_sem=send_sems.at[a, k - 1], recv_sem=recv_sems.at[a, k - 1],
                    device_id=(px, py, pc), device_id_type=MESH))
                if arrivals:
                    slot = outs[a].at[4 * px + 2 * py + pc]
                    recvs.append(pltpu.make_async_remote_copy(
                        src_ref=slot, dst_ref=slot, send_sem=send_sems.at[a, k - 1], recv_sem=recv_sems.at[a, k - 1],
                        device_id=(px, py, pc), device_id_type=MESH))
        return mine, sends, recvs

    def start(self, ins, outs, sems):
        mine, sends, _ = self._copies(ins, outs, sems, False)
        for cp in mine + sends:
            cp.start()

    def finish(self, ins, outs, sems):
        mine, sends, recvs = self._copies(ins, outs, sems, True)
        for cp in recvs:
            cp.wait_recv()
        for cp in sends:
            cp.wait_send()
        for cp in mine:
            cp.wait()


class _ToSibling:
    def __init__(self, gs):
        n = len(gs)
        self.arrs = list(gs)
        self.out_shape = [jax.ShapeDtypeStruct((NCHIP,) + g.shape[1:], g.dtype) for g in gs]
        self.scratch = [pltpu.SemaphoreType.DMA((n, NCHIP)), pltpu.SemaphoreType.DMA((n, NCHIP))]

    def _copies(self, ins, outs, sems):
        send_sems, recv_sems = sems
        x, y, c = _position()
        return [pltpu.make_async_remote_copy(
            src_ref=ins[a].at[2 * k + (1 - c)], dst_ref=outs[a].at[k],
            send_sem=send_sems.at[a, k], recv_sem=recv_sems.at[a, k],
            device_id=(x, y, 1 - c), device_id_type=MESH) for a in range(len(ins)) for k in range(NCHIP)]

    def start(self, ins, outs, sems):
        for cp in self._copies(ins, outs, sems):
            cp.start()

    def finish(self, ins, outs, sems):
        copies = self._copies(ins, outs, sems)
        for cp in copies:
            cp.wait_recv()
        for cp in copies:
            cp.wait_send()


class _ToChips:
    def __init__(self, ps):
        n = len(ps)
        self.arrs = list(ps)
        self.out_shape = [jax.ShapeDtypeStruct(p.shape, p.dtype) for p in ps]
        self.scratch = [pltpu.SemaphoreType.DMA((n, 3)), pltpu.SemaphoreType.DMA((n, 3)),
                        pltpu.SemaphoreType.DMA((n,))]

    def _copies(self, ins, outs, sems, arrivals):
        send_sems, recv_sems, local_sems = sems
        x, y, c = _position()
        mychip = 2 * x + y
        n = len(ins)
        mine = [pltpu.make_async_copy(ins[a].at[mychip], outs[a].at[mychip], local_sems.at[a]) for a in range(n)]
        sends, recvs = [], []
        for a in range(n):
            for j, chip in enumerate(_chips(x, y)):
                sends.append(pltpu.make_async_remote_copy(
                    src_ref=ins[a].at[2 * chip[0] + chip[1]], dst_ref=outs[a].at[mychip],
                    send_sem=send_sems.at[a, j], recv_sem=recv_sems.at[a, j],
                    device_id=(*chip, c), device_id_type=MESH))
                if arrivals:
                    slot = outs[a].at[2 * chip[0] + chip[1]]
                    recvs.append(pltpu.make_async_remote_copy(
                        src_ref=slot, dst_ref=slot, send_sem=send_sems.at[a, j], recv_sem=recv_sems.at[a, j],
                        device_id=(*chip, c), device_id_type=MESH))
        return mine, sends, recvs

    def start(self, ins, outs, sems):
        mine, sends, _ = self._copies(ins, outs, sems, False)
        for cp in mine + sends:
            cp.start()

    def finish(self, ins, outs, sems):
        mine, sends, recvs = self._copies(ins, outs, sems, True)
        for cp in recvs:
            cp.wait_recv()
        for cp in sends:
            cp.wait_send()
        for cp in mine:
            cp.wait()


HOSTED_MIDDLE_AT = 0.95


def _middle_of(comm, ins, outs, sems):
    if hasattr(comm, "middle"):
        comm.middle(ins, outs, sems)


class _Both:
    def __init__(self, first, second):
        self.parts = (first, second)
        self.arrs = first.arrs + second.arrs
        self.out_shape = first.out_shape + second.out_shape
        self.scratch = first.scratch + second.scratch

    def _split(self, ins, outs, sems):
        a, _ = self.parts
        ni, no, ns = len(a.arrs), len(a.out_shape), len(a.scratch)
        return (ins[:ni], outs[:no], sems[:ns]), (ins[ni:], outs[no:], sems[ns:])

    def start(self, ins, outs, sems):
        for part, refs in zip(self.parts, self._split(ins, outs, sems)):
            part.start(*refs)

    def middle(self, ins, outs, sems):
        for part, refs in zip(self.parts, self._split(ins, outs, sems)):
            _middle_of(part, *refs)

    def finish(self, ins, outs, sems):
        for part, refs in zip(self.parts, self._split(ins, outs, sems)):
            part.finish(*refs)

    def split_results(self, res):
        no = len(self.parts[0].out_shape)
        return res[:no], res[no:]


def _comm_only(comm, name):
    n_in, n_out = len(comm.arrs), len(comm.out_shape)

    def body(*refs):
        ins, outs, sems = refs[:n_in], refs[n_in:n_in + n_out], refs[n_in + n_out:]
        comm.start(ins, outs, sems)
        _middle_of(comm, ins, outs, sems)
        comm.finish(ins, outs, sems)

    return pl.pallas_call(
        body, name=name, out_shape=comm.out_shape, in_specs=[ANY] * n_in, out_specs=[ANY] * n_out,
        scratch_shapes=comm.scratch,
    )(*comm.arrs)


def _hosted_call(body, operands, *, name, grid, in_specs, out_specs, out_shape, scratch_shapes=(), sem=(),
                 aliases=None, comm=None):
    single = not isinstance(out_shape, (list, tuple))
    o_specs = [out_specs] if single else list(out_specs)
    o_shape = [out_shape] if single else list(out_shape)
    n_in, n_out, n_scr = len(in_specs), len(o_shape), len(scratch_shapes)
    if comm is None:
        res = pl.pallas_call(body, name=name, grid=grid, in_specs=list(in_specs), out_specs=o_specs,
                             out_shape=o_shape, scratch_shapes=list(scratch_shapes),
                             input_output_aliases=aliases or {}, compiler_params=_cp(*sem))(*operands)
        return (res[0] if single else res), []
    c_in, c_out = len(comm.arrs), len(comm.out_shape)

    def wrapped(*refs):
        ins, cins = refs[:n_in], refs[n_in:n_in + c_in]
        o0 = n_in + c_in
        outs, couts = refs[o0:o0 + n_out], refs[o0 + n_out:o0 + n_out + c_out]
        s0 = o0 + n_out + c_out
        scr, csems = refs[s0:s0 + n_scr], refs[s0 + n_scr:]
        pids = [pl.program_id(a) for a in range(len(grid))]
        step = functools.reduce(lambda acc, pg: acc * pg[1] + pg[0], zip(pids, grid), 0)
        total = int(np.prod(grid))
        late = min(total - 1, max(1, int(total * HOSTED_MIDDLE_AT)))

        @pl.when(step == 0)
        def _():
            comm.start(cins, couts, csems)

        @pl.when(step == late)
        def _():
            _middle_of(comm, cins, couts, csems)

        body(*ins, *outs, *scr)

        @pl.when(step == total - 1)
        def _():
            comm.finish(cins, couts, csems)

    res = pl.pallas_call(
        wrapped, name=name, grid=grid, in_specs=list(in_specs) + [ANY] * c_in, out_specs=o_specs + [ANY] * c_out,
        out_shape=o_shape + comm.out_shape, scratch_shapes=list(scratch_shapes) + comm.scratch,
        input_output_aliases=aliases or {}, compiler_params=_cp(*(["arbitrary"] * len(grid))),
    )(*operands, *comm.arrs)
    return (res[0] if single else res[:n_out]), res[n_out:]


def _adamw(w, g, m, v):
    m2 = ADAM_B1 * m + (1.0 - ADAM_B1) * g
    v2 = ADAM_B2 * v + (1.0 - ADAM_B2) * (g * g)
    m_hat = m2 / (1.0 - ADAM_B1 ** ADAM_STEP)
    v_hat = v2 / (1.0 - ADAM_B2 ** ADAM_STEP)
    delta = -ADAM_LR * (m_hat / (jnp.sqrt(v_hat) + ADAM_EPS) + ADAM_WD * w)
    return delta, m2, v2


def _add_sibling(g, recv, c_idx, name, part=(0, 1)):
    _, R, C = g.shape
    R = R // part[1]
    tr = min(R, 512)
    first = part[0] * (R // tr)

    def body(c_ref, g_ref, r_ref, o_ref):
        o_ref[...] = (g_ref[...] + r_ref[...]).astype(BF16)

    return pl.pallas_call(
        body, name=name,
        grid_spec=pltpu.PrefetchScalarGridSpec(
            num_scalar_prefetch=1, grid=(NCHIP, R // tr),
            in_specs=[pl.BlockSpec((1, tr, C), lambda k, i, c_ref: (2 * k + c_ref[0], first + i, 0)),
                      pl.BlockSpec((1, tr, C), lambda k, i, c_ref: (k, first + i, 0))],
            out_specs=pl.BlockSpec((1, tr, C), lambda k, i, c_ref: (k, i, 0))),
        out_shape=jax.ShapeDtypeStruct((NCHIP, R, C), BF16),
        compiler_params=_cp("parallel", "parallel"),
    )(c_idx, g, recv)


def _sum_adam(parts, w, m, v, row0, prev, name):
    K, R, C = parts.shape
    LR = w.shape[0]
    tr = min(R, 256)
    nb = R // tr
    first = row0 // tr

    def body(p_ref, w_ref, m_ref, v_ref, *rest):
        g_ref, d_ref, m2_ref, v2_ref = rest[-4:]
        g = p_ref[0].astype(F32)
        for k in range(1, K):
            g = g + p_ref[k].astype(F32)
        delta, m2, v2 = _adamw(w_ref[...], g, m_ref[...], v_ref[...])
        g_ref[...] = g
        d_ref[...] = delta
        m2_ref[...] = m2
        v2_ref[...] = v2

    blk = pl.BlockSpec((tr, C), lambda i: (first + i, 0))
    shp = jax.ShapeDtypeStruct((LR, C), F32)
    operands = [parts, w, m, v] + (list(prev) if prev is not None else [])
    return pl.pallas_call(
        body, name=name, grid=(nb,),
        in_specs=[pl.BlockSpec((K, tr, C), lambda i: (0, i, 0)), blk, blk, blk] + [ANY] * (len(operands) - 4),
        out_specs=[blk] * 4, out_shape=[shp] * 4,
        input_output_aliases={4 + k: k for k in range(len(operands) - 4)},
        compiler_params=_cp("parallel"),
    )(*operands)


def _sum_rows(parts, name):
    K, R, C = parts.shape
    tr = min(R, 256)
    while R % tr:
        tr //= 2

    def body(p_ref, o_ref):
        g = p_ref[0]
        for k in range(1, K):
            g = g + p_ref[k]
        o_ref[...] = g

    return pl.pallas_call(
        body, name=name, grid=(R // tr,),
        in_specs=[pl.BlockSpec((K, tr, C), lambda i: (0, i, 0))],
        out_specs=pl.BlockSpec((tr, C), lambda i: (i, 0)),
        out_shape=jax.ShapeDtypeStruct((R, C), F32),
        compiler_params=_cp("parallel"),
    )(parts)


def _adam_only(g, w, m, v, name):
    R, C = g.shape
    tr = min(R, 256)
    while R % tr:
        tr //= 2

    def body(g_ref, w_ref, m_ref, v_ref, d_ref, m2_ref, v2_ref):
        delta, m2, v2 = _adamw(w_ref[...], g_ref[...], m_ref[...], v_ref[...])
        d_ref[...] = delta
        m2_ref[...] = m2
        v2_ref[...] = v2

    blk = pl.BlockSpec((tr, C), lambda i: (i, 0))
    shp = jax.ShapeDtypeStruct((R, C), F32)
    return pl.pallas_call(
        body, name=name, grid=(R // tr,), in_specs=[blk] * 4, out_specs=[blk] * 3,
        out_shape=[shp] * 3, compiler_params=_cp("parallel"),
    )(g, w, m, v)


def _mod_fwd(c_all, w_mod, b_cols, name):
    L, D, n = w_mod.shape
    B = c_all.shape[0]

    def body(c_ref, w_ref, b_ref, o_ref):
        cv = c_ref[...]
        sc = (cv * _sigmoid(cv)).astype(BF16)
        o_ref[0] = jnp.dot(sc, w_ref[0].astype(BF16), preferred_element_type=F32) + b_ref[0]

    return pl.pallas_call(
        body, name=name, grid=(L,),
        in_specs=[pl.BlockSpec((B, D), lambda l: (0, 0)),
                  pl.BlockSpec((1, D, n), lambda l: (l, 0, 0)),
                  pl.BlockSpec((1, 1, n), lambda l: (l, 0, 0))],
        out_specs=pl.BlockSpec((1, B, n), lambda l: (l, 0, 0)),
        out_shape=jax.ShapeDtypeStruct((L, B, n), F32),
        compiler_params=_cp("parallel"),
    )(c_all, w_mod, b_cols)


def _wmod_grad_adam(sc_t, dm, w, m, v, name):
    L, D, n = w.shape
    KP = sc_t.shape[1]
    tr = min(D, 512)

    def body(s_ref, dm_ref, w_ref, m_ref, v_ref, g_ref, d_ref, m2_ref, v2_ref):
        g = jnp.dot(s_ref[...], dm_ref[0], preferred_element_type=F32,
                    precision=lax.Precision.HIGHEST)
        delta, m2, v2 = _adamw(w_ref[0], g, m_ref[0], v_ref[0])
        g_ref[0] = g
        d_ref[0] = delta
        m2_ref[0] = m2
        v2_ref[0] = v2

    blk = pl.BlockSpec((1, tr, n), lambda l, i: (l, i, 0))
    shp = jax.ShapeDtypeStruct((L, D, n), F32)
    return pl.pallas_call(
        body, name=name, grid=(L, D // tr),
        in_specs=[pl.BlockSpec((tr, KP), lambda l, i: (i, 0)),
                  pl.BlockSpec((1, KP, n), lambda l, i: (l, 0, 0)), blk, blk, blk],
        out_specs=[blk] * 4, out_shape=[shp] * 4,
        compiler_params=_cp("parallel", "parallel"),
    )(sc_t, dm, w, m, v)


def _vec_spec(D):
    return pl.BlockSpec((1, D), lambda i: (0, 0))


def _pre(x, res, gate, g, scale, shift, name):
    S, D = x.shape
    tr = min(S, ROW_TILE)
    has_res = res is not None
    row = pl.BlockSpec((tr, D), lambda i: (i, 0))

    def body(*refs):
        if has_res:
            x_ref, r_ref, gate_ref, g_ref, sc_ref, sh_ref, xl_ref, h_ref = refs
            xv = x_ref[...] + gate_ref[...] * r_ref[...]
            xl_ref[...] = xv
        else:
            x_ref, g_ref, sc_ref, sh_ref, h_ref = refs
            xv = x_ref[...]
        r = lax.rsqrt(jnp.mean(xv * xv, axis=-1, keepdims=True) + EPS)
        y = (xv * r) * g_ref[...]
        h_ref[...] = (y * (1.0 + sc_ref[...]) + sh_ref[...]).astype(BF16)

    vec = _vec_spec(D)
    if has_res:
        xl, h = pl.pallas_call(
            body, name=name, grid=(S // tr,),
            in_specs=[row, row, vec, vec, vec, vec], out_specs=[row, row],
            out_shape=[jax.ShapeDtypeStruct((S, D), F32), jax.ShapeDtypeStruct((S, D), BF16)],
            compiler_params=_cp("parallel"),
        )(x, res, gate, g, scale, shift)
        return xl, h
    h = pl.pallas_call(
        body, name=name, grid=(S // tr,),
        in_specs=[row, vec, vec, vec], out_specs=row,
        out_shape=jax.ShapeDtypeStruct((S, D), BF16),
        compiler_params=_cp("parallel"),
    )(x, g, scale, shift)
    return x, h


def _pre_bwd(xl, dh, dx_in, g, scale, name, comm=None):
    S, D = xl.shape
    tr = min(S, ROW_TILE)
    nsteps = S // tr
    row = pl.BlockSpec((tr, D), lambda i: (i, 0))
    vec = _vec_spec(D)

    def body(x_ref, dh_ref, dxin_ref, g_ref, sc_ref, dx_ref, dsh_ref, dsc_ref, dg_ref, acc_sh, acc_t):
        i = pl.program_id(0)
        xv = x_ref[...]
        dh = dh_ref[...]
        r = lax.rsqrt(jnp.mean(xv * xv, axis=-1, keepdims=True) + EPS)
        xn = xv * r
        part_sh = jnp.sum(dh.reshape(tr // 8, 8, D), axis=0)
        part_t = jnp.sum((dh * xn).reshape(tr // 8, 8, D), axis=0)

        @pl.when(i == 0)
        def _():
            acc_sh[...] = part_sh
            acc_t[...] = part_t

        @pl.when(i > 0)
        def _():
            acc_sh[...] += part_sh
            acc_t[...] += part_t

        dxn = dh * (g_ref[...] * (1.0 + sc_ref[...]))
        dx_ref[...] = dxin_ref[...] + r * (dxn - xn * jnp.mean(dxn * xn, axis=-1, keepdims=True))

        @pl.when(i == nsteps - 1)
        def _():
            t = jnp.sum(acc_t[...], axis=0, keepdims=True)
            dsh_ref[...] = jnp.sum(acc_sh[...], axis=0, keepdims=True)
            dsc_ref[...] = t * g_ref[...]
            dg_ref[...] = t * (1.0 + sc_ref[...])

    v = jax.ShapeDtypeStruct((1, D), F32)
    return _hosted_call(
        body, [xl, dh, dx_in, g, scale], name=name, grid=(nsteps,),
        in_specs=[row, row, row, vec, vec], out_specs=[row, vec, vec, vec],
        out_shape=[jax.ShapeDtypeStruct((S, D), F32), v, v, v],
        scratch_shapes=[pltpu.VMEM((8, D), F32), pltpu.VMEM((8, D), F32)],
        sem=("arbitrary",), comm=comm)


def _post_bwd(dx, out, gate, name):
    S, D = dx.shape
    tr = min(S, ROW_TILE)
    nsteps = S // tr
    row = pl.BlockSpec((tr, D), lambda i: (i, 0))
    vec = _vec_spec(D)

    def body(dx_ref, o_ref, gate_ref, do_ref, dg_ref, acc):
        i = pl.program_id(0)
        dxv = dx_ref[...]
        do_ref[...] = (dxv * gate_ref[...]).astype(BF16)
        part = jnp.sum((dxv * o_ref[...]).reshape(tr // 8, 8, D), axis=0)

        @pl.when(i == 0)
        def _():
            acc[...] = part

        @pl.when(i > 0)
        def _():
            acc[...] += part

        @pl.when(i == nsteps - 1)
        def _():
            dg_ref[...] = jnp.sum(acc[...], axis=0, keepdims=True)

    return pl.pallas_call(
        body, name=name, grid=(nsteps,),
        in_specs=[row, row, vec], out_specs=[row, vec],
        out_shape=[jax.ShapeDtypeStruct((S, D), BF16), jax.ShapeDtypeStruct((1, D), F32)],
        scratch_shapes=[pltpu.VMEM((8, D), F32)],
        compiler_params=_cp("arbitrary"),
    )(dx, out, gate)


def _loss_head(x, res, gate, gf, tgt, name):
    S, D = x.shape
    tr = min(S, ROW_TILE)
    nsteps = S // tr
    row = pl.BlockSpec((tr, D), lambda i: (i, 0))
    vec = _vec_spec(D)

    def body(x_ref, r_ref, gate_ref, gf_ref, t_ref, dx_ref, loss_ref, dgf_ref, acc, lacc):
        i = pl.program_id(0)
        xv = x_ref[...] + gate_ref[...] * r_ref[...]
        r = lax.rsqrt(jnp.mean(xv * xv, axis=-1, keepdims=True) + EPS)
        xn = xv * r
        err = xn * gf_ref[...] - t_ref[...]
        row_loss = jnp.mean(err * err, axis=-1, keepdims=True)
        lpart = 0.5 * jnp.sum(row_loss, axis=0, keepdims=True)
        dy = err * (1.0 / D)
        part = jnp.sum((dy * xn).reshape(tr // 8, 8, D), axis=0)

        @pl.when(i == 0)
        def _():
            acc[...] = part
            lacc[...] = lpart

        @pl.when(i > 0)
        def _():
            acc[...] += part
            lacc[...] += lpart

        dxn = dy * gf_ref[...]
        dx_ref[...] = r * (dxn - xn * jnp.mean(dxn * xn, axis=-1, keepdims=True))

        @pl.when(i == nsteps - 1)
        def _():
            dgf_ref[...] = jnp.sum(acc[...], axis=0, keepdims=True)
            loss_ref[...] = lacc[...]

    return pl.pallas_call(
        body, name=name, grid=(nsteps,),
        in_specs=[row, row, vec, vec, row],
        out_specs=[row, pl.BlockSpec((1, 1), lambda i: (0, 0)), vec],
        out_shape=[jax.ShapeDtypeStruct((S, D), F32), jax.ShapeDtypeStruct((1, 1), F32),
                   jax.ShapeDtypeStruct((1, D), F32)],
        scratch_shapes=[pltpu.VMEM((8, D), F32), pltpu.VMEM((1, 1), F32)],
        compiler_params=_cp("arbitrary"),
    )(x, res, gate, gf, tgt)


NN = (((1,), (0,)), ((), ()))
NT = (((1,), (1,)), ((), ()))
TN = (((0,), (0,)), ((), ()))


def _mm(name, a, b, out_shape, grid, a_spec, b_spec, o_spec, dims, a2d, b2d, k_axis, sem, alias=None, comm=None):
    def body(*refs):
        a_ref, b_ref, o_ref = refs[0], refs[1], refs[-1]
        r = lax.dot_general(a_ref[...].reshape(a2d), b_ref[...].reshape(b2d), dims,
                            preferred_element_type=F32)
        r = r.reshape(o_ref.shape)
        if k_axis is None:
            o_ref[...] = r.astype(o_ref.dtype)
        else:
            k = pl.program_id(k_axis)

            @pl.when(k == 0)
            def _():
                o_ref[...] = r

            @pl.when(k > 0)
            def _():
                o_ref[...] += r

    operands, in_specs, aliases = [a, b], [a_spec, b_spec], {}
    if alias is not None:
        operands.append(alias)
        in_specs.append(ANY)
        aliases = {2: 0}
    res, extra = _hosted_call(body, operands, name=name, grid=grid, in_specs=in_specs, out_specs=o_spec,
                              out_shape=out_shape, sem=sem, aliases=aliases, comm=comm)
    return res if comm is None else (res, extra)


def _tile(n, pref):
    t = min(n, pref)
    while n % t:
        t -= 128
    return t


def _mm_nn_in(a, w, l, name, comm=None):
    M, K = a.shape
    _, _, _, n = w.shape
    tm, tn = min(M, 512), _tile(n, 1024)
    nb = n // tn
    return _mm(name, a, w, jax.ShapeDtypeStruct((M, NDEV * n), F32), (NDEV * nb, M // tm),
               pl.BlockSpec((tm, K), lambda j, i: (i, 0)),
               pl.BlockSpec((1, 1, K, tn), lambda j, i: (j // nb, l, 0, j % nb)),
               pl.BlockSpec((tm, tn), lambda j, i: (i, j)),
               NN, (tm, K), (K, tn), None, ("parallel", "parallel"), comm=comm)


def _mm_nn_out(a, w, l, name):
    M, K = a.shape
    _, _, kb, N = w.shape
    tm, tn = min(M, 512), _tile(N, 1024)
    return _mm(name, a, w, jax.ShapeDtypeStruct((M, N), F32), (N // tn, M // tm),
               pl.BlockSpec((tm, K), lambda j, i: (i, 0)),
               pl.BlockSpec((NDEV, 1, kb, tn), lambda j, i: (0, l, 0, j)),
               pl.BlockSpec((tm, tn), lambda j, i: (i, j)),
               NN, (tm, K), (K, tn), None, ("parallel", "parallel"))


def _mm_nt_in(a, w, l, name, comm=None):
    M, _ = a.shape
    _, _, K, n = w.shape
    tm, tk = min(M, 1024), _tile(K, 1024)
    gb = 2 if n <= 1024 else 1

    def body(a_ref, w_ref, o_ref):
        k = pl.program_id(2)
        r = lax.dot_general(a_ref[:, :n], w_ref[0, 0], NT, preferred_element_type=F32)
        for g in range(1, gb):
            r = r + lax.dot_general(a_ref[:, g * n:(g + 1) * n], w_ref[g, 0], NT, preferred_element_type=F32)

        @pl.when(k == 0)
        def _():
            o_ref[...] = r

        @pl.when(k > 0)
        def _():
            o_ref[...] += r

    res, extra = _hosted_call(
        body, [a, w], name=name, grid=(M // tm, K // tk, NDEV // gb),
        in_specs=[pl.BlockSpec((tm, gb * n), lambda i, j, k: (i, k)),
                  pl.BlockSpec((gb, 1, tk, n), lambda i, j, k: (k, l, j, 0))],
        out_specs=pl.BlockSpec((tm, tk), lambda i, j, k: (i, j)),
        out_shape=jax.ShapeDtypeStruct((M, K), F32),
        sem=("parallel", "parallel", "arbitrary"), comm=comm)
    return res if comm is None else (res, extra)


def _mm_nt_out(a, w, l, name):
    M, N = a.shape
    _, _, kb, _ = w.shape
    K = NDEV * kb
    tm, tk, tc = min(M, 1024), _tile(K, 1024), _tile(N, 1024)
    per = tk // kb
    return _mm(name, a, w, jax.ShapeDtypeStruct((M, K), F32), (M // tm, K // tk, N // tc),
               pl.BlockSpec((tm, tc), lambda i, j, k: (i, k)),
               pl.BlockSpec((per, 1, kb, tc), lambda i, j, k: (j, l, 0, k)),
               pl.BlockSpec((tm, tk), lambda i, j, k: (i, j)),
               NT, (tm, tc), (tk, tc), 2, ("parallel", "parallel", "arbitrary"))


def _mm_tn_in(a, b, l, L, buf, name, comm=None, part=(0, 1)):
    S, K = a.shape
    K = K // part[1]
    n = b.shape[1] // NDEV
    ts, tk, tn = min(S, 2048), _tile(K, 1024), _tile(n, 1024)
    nb = n // tn
    first = part[0] * (K // tk)
    return _mm(name, a, b, jax.ShapeDtypeStruct((NDEV, L, K, n), F32), (NDEV * nb, K // tk, S // ts),
               pl.BlockSpec((ts, tk), lambda j, i, s: (s, first + i)),
               pl.BlockSpec((ts, tn), lambda j, i, s: (s, j)),
               pl.BlockSpec((1, 1, tk, tn), lambda j, i, s: (j // nb, l, i, j % nb)),
               TN, (ts, tk), (ts, tn), 2, ("parallel", "parallel", "arbitrary"), alias=buf, comm=comm)


def _mm_tn_out(a, b, l, L, buf, name):
    S, K = a.shape
    N = b.shape[1]
    kb = K // NDEV
    ts, tk, tn = min(S, 2048), _tile(K, 1024), _tile(N, 1024)
    per = tk // kb
    return _mm(name, a, b, jax.ShapeDtypeStruct((NDEV, L, kb, N), F32), (N // tn, K // tk, S // ts),
               pl.BlockSpec((ts, tk), lambda j, i, s: (s, i)),
               pl.BlockSpec((ts, tn), lambda j, i, s: (s, j)),
               pl.BlockSpec((per, 1, kb, tn), lambda j, i, s: (i, l, 0, j)),
               TN, (ts, tk), (ts, tn), 2, ("parallel", "parallel", "arbitrary"), alias=buf)


def _attn_bias(T):
    reach = max(w // 2 for w, _ in DILATED_PATTERNS)
    hb = -(-reach // T)
    i = np.arange(T)[:, None]
    j = np.arange(T)[None, :]
    tiles = []
    for d in range(-hb, hb + 1):
        rel = j + d * T - i
        mult = np.zeros((T, T), np.float64)
        for window, dil in DILATED_PATTERNS:
            radius = window // (2 * dil)
            mult += (rel % dil == 0) & (np.abs(rel) <= radius * dil)
        tiles.append(np.where(mult > 0, np.log(np.maximum(mult, 1.0)), NEG_INF))
    return jnp.asarray(np.stack(tiles), F32)


def _rope_tables(S):
    half = HEAD_DIM // 2
    pos = jnp.arange(S, dtype=F32)
    inv = ROPE_THETA ** (-jnp.arange(half, dtype=F32) / half)
    ang = pos[:, None] * inv[None, :]
    cos, sin = jnp.cos(ang), jnp.sin(ang)
    return jnp.concatenate([cos, cos], axis=-1), jnp.concatenate([-sin, sin], axis=-1)


def _rope_apply(t, cosf, sinf, heads, sign):
    outs = []
    for hh in range(heads):
        th = t[:, hh * HEAD_DIM:(hh + 1) * HEAD_DIM]
        outs.append(th * cosf + sign * (pltpu.roll(th, HEAD_DIM // 2, 1) * sinf))
    return outs


def _rope_qkv(proj, cosf, sinf, W, name):
    S = proj.shape[0]
    tr = min(S, ROW_TILE)
    heads = W // HEAD_DIM

    def body(q_ref, k_ref, v_ref, c_ref, s_ref, qo_ref, ko_ref, vo_ref):
        cosf_v, sinf_v = c_ref[...], s_ref[...]
        for src, dst, mult in ((q_ref, qo_ref, HEAD_DIM ** -0.5), (k_ref, ko_ref, 1.0)):
            for hh, val in enumerate(_rope_apply(src[...], cosf_v, sinf_v, heads, 1.0)):
                dst[:, hh * HEAD_DIM:(hh + 1) * HEAD_DIM] = (val * mult).astype(BF16)
        vo_ref[...] = v_ref[...].astype(BF16)

    piece = lambda p: pl.BlockSpec((tr, W), lambda i: (i, p))
    tab = pl.BlockSpec((tr, HEAD_DIM), lambda i: (i, 0))
    out = pl.BlockSpec((tr, W), lambda i: (i, 0))
    shp = jax.ShapeDtypeStruct((S, W), BF16)
    return pl.pallas_call(
        body, name=name, grid=(S // tr,),
        in_specs=[piece(0), piece(1), piece(2), tab, tab], out_specs=[out] * 3, out_shape=[shp] * 3,
        compiler_params=_cp("parallel"),
    )(proj, proj, proj, cosf, sinf)


def _attn_fwd(q, k, v, bias, name, comm=None):
    S, W = q.shape
    H = W // HEAD_DIM
    nd, T, _ = bias.shape
    hb, nq = nd // 2, S // T
    scale = HEAD_DIM ** -0.5
    hp = min(H, HEADS_PER_STEP)
    rc = min(T, ATTN_ROW_CHUNK)
    wp = hp * HEAD_DIM

    def body(q_ref, k_ref, v_ref, b_ref, o_ref, lse_ref, m_s, l_s, acc_s):
        i, d = pl.program_id(1), pl.program_id(2)
        j = i + d - hb

        @pl.when(d == 0)
        def _():
            m_s[...] = jnp.full(m_s.shape, -jnp.inf, F32)
            l_s[...] = jnp.zeros(l_s.shape, F32)
            acc_s[...] = jnp.zeros(acc_s.shape, F32)

        @pl.when((j >= 0) & (j < nq))
        def _():
            items = [(hh, c) for hh in range(hp) for c in range(T // rc)]

            def scores(item):
                hh, c = item
                cols, rows = slice(hh * HEAD_DIM, (hh + 1) * HEAD_DIM), slice(c * rc, (c + 1) * rc)
                return (lax.dot_general(q_ref[rows, cols], k_ref[:, cols], NT, preferred_element_type=F32)
                        + b_ref[d, rows, :])

            def weighted_values(item, p, alpha):
                hh, c = item
                cols, rows = slice(hh * HEAD_DIM, (hh + 1) * HEAD_DIM), slice(c * rc, (c + 1) * rc)
                acc_s[rows, cols] = alpha * acc_s[rows, cols] + jnp.dot(p, v_ref[:, cols],
                                                                        preferred_element_type=F32)

            s_next, pending = scores(items[0]), None
            for n, (hh, c) in enumerate(items):
                rows = slice(c * rc, (c + 1) * rc)
                s = s_next
                if n + 1 < len(items):
                    s_next = scores(items[n + 1])
                if pending is not None:
                    weighted_values(*pending)
                parts = [s[:, t * LANES:(t + 1) * LANES] for t in range(T // LANES)]
                m_old = m_s[hh, rows, :]
                m_cur = jnp.max(functools.reduce(jnp.maximum, parts), axis=1, keepdims=True)
                m_new = jnp.maximum(m_old, m_cur)
                alpha = jnp.exp(m_old - m_new)
                ps = [jnp.exp(part - m_new) for part in parts]
                l_s[hh, rows, :] = alpha * l_s[hh, rows, :] + functools.reduce(jnp.add, ps)
                m_s[hh, rows, :] = m_new
                pending = ((hh, c), jnp.concatenate(ps, axis=1).astype(BF16), alpha)
            weighted_values(*pending)

        @pl.when(d == nd - 1)
        def _():
            for hh in range(hp):
                cols = slice(hh * HEAD_DIM, (hh + 1) * HEAD_DIM)
                l = jnp.sum(l_s[hh], axis=1, keepdims=True)
                o_ref[:, cols] = acc_s[:, cols] / l
                lse_ref[hh] = m_s[hh][:, :1] + jnp.log(l)

    kv = pl.BlockSpec((T, wp), lambda h, i, d: (jnp.clip(i + d - hb, 0, nq - 1), h))
    return _hosted_call(
        body, [q, k, v, bias], name=name, grid=(H // hp, nq, nd),
        in_specs=[pl.BlockSpec((T, wp), lambda h, i, d: (i, h)), kv, kv,
                  pl.BlockSpec((nd, T, T), lambda h, i, d: (0, 0, 0))],
        out_specs=[pl.BlockSpec((T, wp), lambda h, i, d: (i, h)),
                   pl.BlockSpec((hp, T, 1), lambda h, i, d: (h, i, 0))],
        out_shape=[jax.ShapeDtypeStruct((S, W), F32), jax.ShapeDtypeStruct((H, S, 1), F32)],
        scratch_shapes=[pltpu.VMEM((hp, T, LANES), F32), pltpu.VMEM((hp, T, LANES), F32),
                        pltpu.VMEM((T, wp), F32)],
        sem=("parallel", "parallel", "arbitrary"), comm=comm)


def _attn_bwd(q, k, v, do, lse, delta, bias, name, comm=None):
    S, W = q.shape
    H = W // HEAD_DIM
    nd, T, _ = bias.shape
    hb, nq = nd // 2, S // T
    scale = HEAD_DIM ** -0.5
    hp = min(H, HEADS_PER_STEP)
    rc = min(T, ATTN_ROW_CHUNK)
    wp = hp * HEAD_DIM

    def body(q_ref, do_ref, lse_ref, dl_ref, k_ref, v_ref, b_ref, dq_ref, dk_ref, dv_ref):
        j, d = pl.program_id(1), pl.program_id(2)
        i = j + d - hb

        @pl.when((j == 0) & (d == 0))
        def _():
            dq_ref[...] = jnp.zeros(dq_ref.shape, F32)

        @pl.when(d == 0)
        def _():
            dk_ref[...] = jnp.zeros(dk_ref.shape, F32)
            dv_ref[...] = jnp.zeros(dv_ref.shape, F32)

        @pl.when((i >= 0) & (i < nq))
        def _():
            items = [(hh, c) for hh in range(hp) for c in range(T // rc)]

            def slices(item):
                hh, c = item
                return slice(hh * HEAD_DIM, (hh + 1) * HEAD_DIM), slice(c * rc, (c + 1) * rc)

            def products(item):
                cols, rows = slices(item)
                s = (lax.dot_general(q_ref[rows, cols], k_ref[:, cols], NT, preferred_element_type=F32)
                     + b_ref[nd - 1 - d, rows, :])
                dp = lax.dot_general(do_ref[rows, cols], v_ref[:, cols], NT, preferred_element_type=F32)
                return s, dp

            def gradients(item, p, ds):
                cols, rows = slices(item)
                dv_ref[:, cols] += lax.dot_general(p, do_ref[rows, cols], TN, preferred_element_type=F32)
                dk_ref[:, cols] += lax.dot_general(ds, q_ref[rows, cols], TN, preferred_element_type=F32)
                q_rows = pl.ds(pl.multiple_of(i * T + item[1] * rc, rc), rc)
                dq_ref[q_rows, cols] += jnp.dot(ds, k_ref[:, cols], preferred_element_type=F32) * scale

            nxt, pending = products(items[0]), None
            for n, item in enumerate(items):
                s, dp = nxt
                if n + 1 < len(items):
                    nxt = products(items[n + 1])
                if pending is not None:
                    gradients(*pending)
                _, rows = slices(item)
                p = jnp.exp(s - lse_ref[item[0], rows, :])
                ds = p * (dp - dl_ref[item[0], rows, :])
                pending = (item, p.astype(BF16), ds.astype(BF16))
            gradients(*pending)

    qi = lambda h, j, d: (jnp.clip(j + d - hb, 0, nq - 1), h)
    qs = pl.BlockSpec((T, wp), qi)
    col = pl.BlockSpec((hp, T, 1), lambda h, j, d: (h, jnp.clip(j + d - hb, 0, nq - 1), 0))
    kv = pl.BlockSpec((T, wp), lambda h, j, d: (j, h))
    shp = jax.ShapeDtypeStruct((S, W), F32)
    return _hosted_call(
        body, [q, do, lse, delta, k, v, bias], name=name, grid=(H // hp, nq, nd),
        in_specs=[qs, qs, col, col, kv, kv, pl.BlockSpec((nd, T, T), lambda h, j, d: (0, 0, 0))],
        out_specs=[pl.BlockSpec((S, wp), lambda h, j, d: (0, h)), kv, kv],
        out_shape=[shp, shp, shp],
        sem=("parallel", "arbitrary", "arbitrary"), comm=comm)


def _halo_specs(S, tr, W, piece):
    per, last = tr // 8, S // 8 - 1
    prev = pl.BlockSpec((8, W), lambda i: (jnp.maximum(i * per - 1, 0), piece))
    nxt = pl.BlockSpec((8, W), lambda i: (jnp.minimum((i + 1) * per, last), piece))
    return prev, nxt


def _shifted(t, before, after, tr):
    rows = lax.broadcasted_iota(jnp.int32, (tr, 1), 0)
    prev = jnp.where(rows == 0, before, pltpu.roll(t, 1, 0))
    nxt = jnp.where(rows == tr - 1, after, pltpu.roll(t, tr - 1, 0))
    return prev, nxt


def _ab_mix(attn, proj, conv_w, W, name):
    S = attn.shape[0]
    tr = min(S, ROW_TILE)
    nsteps = S // tr

    def body(a_ref, za_ref, ub_ref, gb_ref, gc_ref, zb_ref, ubp, ubn, gcp, gcn, w_ref, y_ref):
        i = pl.program_id(0)
        t = gc_ref[...] * ub_ref[...]
        before = jnp.where(i == 0, 0.0, (gcp[...] * ubp[...])[7:8, :])
        after = jnp.where(i == nsteps - 1, 0.0, (gcn[...] * ubn[...])[0:1, :])
        t_prev, t_next = _shifted(t, before, after, tr)
        w = w_ref[...]
        cv = w[0:1, :] * t_prev + w[1:2, :] * t + w[2:3, :] * t_next
        silu_a, _ = _silu_and_grad(za_ref[...])
        silu_b, _ = _silu_and_grad(zb_ref[...])
        y_ref[:, :W] = (a_ref[...] * silu_a).astype(BF16)
        y_ref[:, W:] = (gb_ref[...] * cv * silu_b).astype(BF16)

    piece = lambda p: pl.BlockSpec((tr, W), lambda i: (i, p))
    ubp, ubn = _halo_specs(S, tr, W, 4)
    gcp, gcn = _halo_specs(S, tr, W, 6)
    return pl.pallas_call(
        body, name=name, grid=(nsteps,),
        in_specs=[pl.BlockSpec((tr, W), lambda i: (i, 0)), piece(3), piece(4), piece(5), piece(6), piece(7),
                  ubp, ubn, gcp, gcn, pl.BlockSpec((3, W), lambda i: (0, 0))],
        out_specs=pl.BlockSpec((tr, 2 * W), lambda i: (i, 0)),
        out_shape=jax.ShapeDtypeStruct((S, 2 * W), BF16),
        compiler_params=_cp("parallel"),
    )(attn, proj, proj, proj, proj, proj, proj, proj, proj, proj, conv_w)


def _dattn_prep(dy, proj, attn, W, name):
    S = attn.shape[0]
    H = W // HEAD_DIM
    tr = min(S, ROW_TILE)

    def body(dy_ref, za_ref, a_ref, do_ref, dl_ref):
        silu_a, _ = _silu_and_grad(za_ref[...])
        do = dy_ref[...] * silu_a
        do_ref[...] = do.astype(BF16)
        prod = do * a_ref[...]
        for hh in range(H):
            dl_ref[hh] = jnp.sum(prod[:, hh * HEAD_DIM:(hh + 1) * HEAD_DIM], axis=1, keepdims=True)

    row = pl.BlockSpec((tr, W), lambda i: (i, 0))
    return pl.pallas_call(
        body, name=name, grid=(S // tr,),
        in_specs=[row, pl.BlockSpec((tr, W), lambda i: (i, 3)), row],
        out_specs=[row, pl.BlockSpec((H, tr, 1), lambda i: (0, i, 0))],
        out_shape=[jax.ShapeDtypeStruct((S, W), BF16), jax.ShapeDtypeStruct((H, S, 1), F32)],
        compiler_params=_cp("parallel"),
    )(dy, proj, attn)


def _ab_bwd(dy, attn, proj, dqr, dkr, dv, cosf, sinf, conv_w, W, name):
    S = attn.shape[0]
    tr = min(S, ROW_TILE // 2)
    nsteps = S // tr
    heads = W // HEAD_DIM

    def body(dya_ref, dyb_ref, a_ref, za_ref, ub_ref, gb_ref, gc_ref, zb_ref, dq_ref, dk_ref, dv_ref,
             c_ref, s_ref, w_ref, dybp, dybn, gbp, gbn, zbp, zbn, ubp, ubn, gcp, gcn,
             dp_ref, dw_ref, acc):
        i = pl.program_id(0)
        first, last = i == 0, i == nsteps - 1
        w = w_ref[...]
        w0, w1, w2 = w[0:1, :], w[1:2, :], w[2:3, :]
        ub, gb, gc, zb = ub_ref[...], gb_ref[...], gc_ref[...], zb_ref[...]
        dyb = dyb_ref[...]
        silu_a, dsilu_a = _silu_and_grad(za_ref[...])
        silu_b, dsilu_b = _silu_and_grad(zb)
        t = gc * ub
        t_prev, t_next = _shifted(t, jnp.where(first, 0.0, (gcp[...] * ubp[...])[7:8, :]),
                                  jnp.where(last, 0.0, (gcn[...] * ubn[...])[0:1, :]), tr)
        cv = w0 * t_prev + w1 * t + w2 * t_next
        dcv = dyb * gb * silu_b
        halo_p = dybp[...] * gbp[...] * _silu_and_grad(zbp[...])[0]
        halo_n = dybn[...] * gbn[...] * _silu_and_grad(zbn[...])[0]
        dcv_prev, dcv_next = _shifted(dcv, jnp.where(first, 0.0, halo_p[7:8, :]),
                                      jnp.where(last, 0.0, halo_n[0:1, :]), tr)
        dt = w0 * dcv_next + w1 * dcv + w2 * dcv_prev
        cosf_v, sinf_v = c_ref[...], s_ref[...]
        for src, base in ((dq_ref, 0), (dk_ref, W)):
            for hh, val in enumerate(_rope_apply(src[...], cosf_v, sinf_v, heads, -1.0)):
                dp_ref[:, base + hh * HEAD_DIM:base + (hh + 1) * HEAD_DIM] = val.astype(BF16)
        dp_ref[:, 2 * W:3 * W] = dv_ref[...].astype(BF16)
        dp_ref[:, 3 * W:4 * W] = (dya_ref[...] * a_ref[...] * dsilu_a).astype(BF16)
        dp_ref[:, 4 * W:5 * W] = (dt * gc).astype(BF16)
        dp_ref[:, 5 * W:6 * W] = (dyb * cv * silu_b).astype(BF16)
        dp_ref[:, 6 * W:7 * W] = (dt * ub).astype(BF16)
        dp_ref[:, 7 * W:8 * W] = (dyb * gb * cv * dsilu_b).astype(BF16)
        tap = lax.broadcasted_iota(jnp.int32, (8, 1), 0)
        part = (jnp.where(tap == 0, jnp.sum(dcv * t_prev, axis=0, keepdims=True), 0.0)
                + jnp.where(tap == 1, jnp.sum(dcv * t, axis=0, keepdims=True), 0.0)
                + jnp.where(tap == 2, jnp.sum(dcv * t_next, axis=0, keepdims=True), 0.0))

        @pl.when(first)
        def _():
            acc[...] = part

        @pl.when(i > 0)
        def _():
            acc[...] += part

        @pl.when(last)
        def _():
            dw_ref[...] = acc[...]

    row = pl.BlockSpec((tr, W), lambda i: (i, 0))
    piece = lambda p: pl.BlockSpec((tr, W), lambda i: (i, p))
    tab = pl.BlockSpec((tr, HEAD_DIM), lambda i: (i, 0))
    dybp, dybn = _halo_specs(S, tr, W, 1)
    gbp, gbn = _halo_specs(S, tr, W, 5)
    zbp, zbn = _halo_specs(S, tr, W, 7)
    ubp, ubn = _halo_specs(S, tr, W, 4)
    gcp, gcn = _halo_specs(S, tr, W, 6)
    return pl.pallas_call(
        body, name=name, grid=(nsteps,),
        in_specs=[piece(0), piece(1), row, piece(3), piece(4), piece(5), piece(6), piece(7), row, row, row,
                  tab, tab, pl.BlockSpec((3, W), lambda i: (0, 0)),
                  dybp, dybn, gbp, gbn, zbp, zbn, ubp, ubn, gcp, gcn],
        out_specs=[pl.BlockSpec((tr, 8 * W), lambda i: (i, 0)), pl.BlockSpec((8, W), lambda i: (0, 0))],
        out_shape=[jax.ShapeDtypeStruct((S, 8 * W), BF16), jax.ShapeDtypeStruct((8, W), F32)],
        scratch_shapes=[pltpu.VMEM((8, W), F32)],
        compiler_params=_cp("arbitrary"),
    )(dy, dy, attn, proj, proj, proj, proj, proj, dqr, dkr, dv, cosf, sinf, conv_w,
      dy, dy, proj, proj, proj, proj, proj, proj, proj, proj)


def _sgu_centre(p_ref, vc_s, dvg_s, Dc):
    gw = Dc // C_GROUPS
    total = None
    for g in range(C_GROUPS):
        cs = slice(g * gw, (g + 1) * gw)
        vg, dvg = _gelu_and_grad(p_ref[:, Dc + g * gw:Dc + (g + 1) * gw])
        vc_s[:, cs] = vg
        if dvg_s is not None:
            dvg_s[:, cs] = dvg
        total = vg if total is None else total + vg
    mu = jnp.sum(total, axis=1, keepdims=True) * (1.0 / Dc)
    total = None
    for g in range(C_GROUPS):
        cs = slice(g * gw, (g + 1) * gw)
        vc = vc_s[:, cs] - mu
        vc_s[:, cs] = vc
        total = vc * vc if total is None else total + vc * vc
    return lax.rsqrt(jnp.sum(total, axis=1, keepdims=True) * (1.0 / Dc) + EPS)


def _sgu_fwd(proj, ln_g, ln_b, w_s, b_st, name, comm=None):
    S, Dc3 = proj.shape
    Dc = Dc3 // 3
    gw = Dc // C_GROUPS
    vec = pl.BlockSpec((1, Dc), lambda i: (0, 0))

    def body(p_ref, lng_ref, lnb_ref, ws_ref, bst_ref, y_ref, vc_s):
        rstd = _sgu_centre(p_ref, vc_s, None, Dc)
        bst = bst_ref[...]
        for g in range(C_GROUPS):
            cs = slice(g * gw, (g + 1) * gw)
            vn = (vc_s[:, cs] * rstd * lng_ref[:, cs] + lnb_ref[:, cs]).astype(BF16)
            mixed = jnp.dot(ws_ref[g].astype(BF16), vn, preferred_element_type=F32) + bst[:, g:g + 1]
            u, _ = _gelu_and_grad(p_ref[:, cs])
            sz, _ = _silu_and_grad(p_ref[:, 2 * Dc + g * gw:2 * Dc + (g + 1) * gw])
            y_ref[:, cs] = (u * mixed * sz).astype(BF16)

    return _hosted_call(
        body, [proj, ln_g, ln_b, w_s, b_st], name=name, grid=(S // C_CHUNK,),
        in_specs=[pl.BlockSpec((C_CHUNK, Dc3), lambda i: (i, 0)), vec, vec,
                  pl.BlockSpec((C_GROUPS, C_CHUNK, C_CHUNK), lambda i: (0, 0, 0)),
                  pl.BlockSpec((C_CHUNK, C_GROUPS), lambda i: (0, 0))],
        out_specs=pl.BlockSpec((C_CHUNK, Dc), lambda i: (i, 0)),
        out_shape=jax.ShapeDtypeStruct((S, Dc), BF16),
        scratch_shapes=[pltpu.VMEM((C_CHUNK, Dc), F32)],
        sem=("parallel",), comm=comm)


def _sgu_bwd(proj, dy, ln_g, ln_b, w_s, w_st, b_st, name):
    S, Dc3 = proj.shape
    Dc = Dc3 // 3
    gw = Dc // C_GROUPS
    nsteps = S // C_CHUNK
    vec = pl.BlockSpec((1, Dc), lambda i: (0, 0))
    wspec = pl.BlockSpec((C_GROUPS, C_CHUNK, C_CHUNK), lambda i: (0, 0, 0))

    def body(p_ref, dy_ref, lng_ref, lnb_ref, ws_ref, wst_ref, bst_ref,
             dp_ref, dws_ref, dbs_ref, dlg_ref, dlb_ref, acc_w, acc_b, acc_g, acc_lb, vc_s, dvg_s, dvh_s):
        i = pl.program_id(0)

        @pl.when(i == 0)
        def _():
            acc_w[...] = jnp.zeros(acc_w.shape, F32)
            acc_b[...] = jnp.zeros(acc_b.shape, F32)
            acc_g[...] = jnp.zeros(acc_g.shape, F32)
            acc_lb[...] = jnp.zeros(acc_lb.shape, F32)

        rstd = _sgu_centre(p_ref, vc_s, dvg_s, Dc)
        bst = bst_ref[...]
        octets = lambda t: jnp.sum(t.reshape(C_CHUNK // 8, 8, gw), axis=0)
        t1, t2 = None, None
        for g in range(C_GROUPS):
            cs = slice(g * gw, (g + 1) * gw)
            zs = slice(2 * Dc + g * gw, 2 * Dc + (g + 1) * gw)
            vhat = vc_s[:, cs] * rstd
            vn = (vhat * lng_ref[:, cs] + lnb_ref[:, cs]).astype(BF16)
            mixed = jnp.dot(ws_ref[g].astype(BF16), vn, preferred_element_type=F32) + bst[:, g:g + 1]
            u, du = _gelu_and_grad(p_ref[:, cs])
            sz, dsz = _silu_and_grad(p_ref[:, zs])
            dy = dy_ref[:, cs]
            dmixed = dy * u * sz
            dmb = dmixed.astype(BF16)
            acc_w[g] += lax.dot_general(dmb, vn, NT, preferred_element_type=F32)
            acc_b[g] += dmixed
            dvn = jnp.dot(wst_ref[g].astype(BF16), dmb, preferred_element_type=F32)
            acc_g[:, cs] += octets(dvn * vhat)
            acc_lb[:, cs] += octets(dvn)
            dvh = dvn * lng_ref[:, cs]
            dvh_s[:, cs] = dvh
            t1 = dvh if t1 is None else t1 + dvh
            t2 = dvh * vhat if t2 is None else t2 + dvh * vhat
            dp_ref[:, cs] = (dy * mixed * sz * du).astype(BF16)
            dp_ref[:, zs] = (dy * u * mixed * dsz).astype(BF16)
        m1 = jnp.sum(t1, axis=1, keepdims=True) * (1.0 / Dc)
        m2 = jnp.sum(t2, axis=1, keepdims=True) * (1.0 / Dc)
        for g in range(C_GROUPS):
            cs = slice(g * gw, (g + 1) * gw)
            dvgelu = rstd * (dvh_s[:, cs] - m1 - (vc_s[:, cs] * rstd) * m2)
            dp_ref[:, Dc + g * gw:Dc + (g + 1) * gw] = (dvgelu * dvg_s[:, cs]).astype(BF16)

        @pl.when(i == nsteps - 1)
        def _():
            dws_ref[...] = acc_w[...]
            for g in range(C_GROUPS):
                dbs_ref[g] = jnp.sum(acc_b[g], axis=1, keepdims=True)
            dlg_ref[...] = jnp.sum(acc_g[...], axis=0, keepdims=True)
            dlb_ref[...] = jnp.sum(acc_lb[...], axis=0, keepdims=True)

    v = jax.ShapeDtypeStruct((1, Dc), F32)
    return pl.pallas_call(
        body, name=name, grid=(nsteps,),
        in_specs=[pl.BlockSpec((C_CHUNK, Dc3), lambda i: (i, 0)), pl.BlockSpec((C_CHUNK, Dc), lambda i: (i, 0)),
                  vec, vec, wspec, wspec, pl.BlockSpec((C_CHUNK, C_GROUPS), lambda i: (0, 0))],
        out_specs=[pl.BlockSpec((C_CHUNK, Dc3), lambda i: (i, 0)), wspec,
                   pl.BlockSpec((C_GROUPS, C_CHUNK, 1), lambda i: (0, 0, 0)), vec, vec],
        out_shape=[jax.ShapeDtypeStruct((S, Dc3), BF16),
                   jax.ShapeDtypeStruct((C_GROUPS, C_CHUNK, C_CHUNK), F32),
                   jax.ShapeDtypeStruct((C_GROUPS, C_CHUNK, 1), F32), v, v],
        scratch_shapes=[pltpu.VMEM((C_GROUPS, C_CHUNK, C_CHUNK), F32), pltpu.VMEM((C_GROUPS, C_CHUNK, gw), F32),
                        pltpu.VMEM((8, Dc), F32), pltpu.VMEM((8, Dc), F32)]
                       + [pltpu.VMEM((C_CHUNK, Dc), F32)] * 3,
        compiler_params=_cp("arbitrary"),
    )(proj, dy, ln_g, ln_b, w_s, w_st, b_st)


PACK_COLS = 1024
PACK_ROWS = 64


def _pack(vectors, rows=PACK_ROWS):
    flat = jnp.concatenate([v.reshape(-1) for v in vectors])
    pad = (-flat.shape[0]) % (PACK_COLS * rows)
    return jnp.pad(flat, (0, pad)).reshape(-1, PACK_COLS)


def _unshard(g, off, shape):
    L, rest = shape[0], shape[1:]
    size = int(np.prod(shape))
    piece = g[:, off:off + size].reshape((NDEV,) + tuple(shape))
    nd = piece.ndim
    perm = tuple(range(1, nd - 1)) + (0, nd - 1)
    full = jnp.transpose(piece, perm)
    return full.reshape(tuple(shape[:-1]) + (NDEV * shape[-1],)), off + size


def kernel(x, c, ab_norm_g, ab_w_mod, ab_b_mod, ab_w_in, ab_conv_w, ab_w_out, sg_norm_g, sg_w_mod, sg_b_mod, sg_w_in, sg_ln_g, sg_ln_b, sg_w_s, sg_b_s, sg_w_out, final_norm_g, loss_target, m_ab_norm_g, m_ab_w_mod, m_ab_b_mod, m_ab_w_in, m_ab_conv_w, m_ab_w_out, m_sg_norm_g, m_sg_w_mod, m_sg_b_mod, m_sg_w_in, m_sg_ln_g, m_sg_ln_b, m_sg_w_s, m_sg_b_s, m_sg_w_out, m_final_norm_g, v_ab_norm_g, v_ab_w_mod, v_ab_b_mod, v_ab_w_in, v_ab_conv_w, v_ab_w_out, v_sg_norm_g, v_sg_w_mod, v_sg_b_mod, v_sg_w_in, v_sg_ln_g, v_sg_ln_b, v_sg_w_s, v_sg_b_s, v_sg_w_out, v_final_norm_g):
    _, S, D = x.shape
    L = ab_norm_g.shape[0]
    W = ab_conv_w.shape[2] * NDEV
    n_ab, n_sg = ab_w_in.shape[2], sg_w_in.shape[2]
    n_mod = ab_w_mod.shape[2]
    kb = ab_w_out.shape[1]
    xi, yi, ci = _position()
    dev = 4 * xi + 2 * yi + ci
    x2, tgt = x.reshape(S, D), loss_target.reshape(S, D)

    small = [c, ab_conv_w, sg_norm_g, sg_ln_g, sg_ln_b]
    (g1,) = _comm_only(_GatherDirect([_pack(small, 8)]), "ag_small")
    g1 = g1.reshape(NDEV, -1)
    c_all = g1[:, :D]
    off = D
    conv_full, off = _unshard(g1, off, ab_conv_w.shape)
    sg_norm_full, off = _unshard(g1, off, sg_norm_g.shape)
    ln_g_full, off = _unshard(g1, off, sg_ln_g.shape)
    ln_b_full, off = _unshard(g1, off, sg_ln_b.shape)

    ab_b_cols = lax.dynamic_slice_in_dim(ab_b_mod, dev * n_mod, n_mod, axis=1)
    m_ab = _mod_fwd(c_all, ab_w_mod, ab_b_cols.reshape(L, 1, n_mod), "mod_fwd_ab")
    m_sg = _mod_fwd(c_all, sg_w_mod, sg_b_mod.reshape(L, 1, n_mod), "mod_fwd_sg")
    m_part = jnp.stack([m_ab, m_sg]).transpose(2, 0, 1, 3).reshape(NDEV, 2 * L * n_mod)
    (g2,) = _comm_only(_GatherDirect([m_part]), "ag_mod")
    mine = lax.dynamic_index_in_dim(g2, dev, axis=1, keepdims=False)
    mods = mine.reshape(NDEV, 2, L, n_mod).transpose(1, 2, 0, 3).reshape(2, L, 3 * D)

    def mod_of(kind, i):
        m = mods[kind, i]
        return m[:D].reshape(1, D), m[D:2 * D].reshape(1, D), m[2 * D:].reshape(1, D)

    big_w = [[(ab_w_in, m_ab_w_in, v_ab_w_in), (ab_w_out, m_ab_w_out, v_ab_w_out)],
             [(sg_w_in, m_sg_w_in, v_sg_w_in), (sg_w_out, m_sg_w_out, v_sg_w_out)]]
    big_names = [["ab_w_in", "ab_w_out"], ["sg_w_in", "sg_w_out"]]
    n_layers = 2 * L
    shards = [[big_w[layer % 2][k][0][layer // 2].astype(BF16) for k in range(2)] for layer in range(n_layers)]
    gathered = {}

    def gather_of(keys):
        keys = [key for key in keys if key[0] < n_layers and key not in gathered]
        return keys, (_Gather([shards[layer][k] for layer, k in keys]) if keys else None)

    def keep_gathered(keys, res):
        for (layer, k), g in zip(keys, res):
            gathered[(layer, k)] = g.reshape((NDEV, 1, D, g.shape[-1]) if k == 0 else (NDEV, 1, kb, D))

    keys, comm = gather_of([(0, 0)])
    keep_gathered(keys, _comm_only(comm, "ag_w_in_layer0"))

    cosf, sinf = _rope_tables(S)
    T = min(S, ATTN_TILE)
    bias = _attn_bias(T)
    norm_g = [ab_norm_g, sg_norm_full]
    w_s_t = jnp.swapaxes(sg_w_s, -1, -2)
    b_s_t = jnp.swapaxes(sg_b_s, -1, -2)

    saved = []
    x_cur, res, gate_prev = x2, None, None
    for layer in range(2 * L):
        kind, i = layer % 2, layer // 2
        tag = f"{'ab' if kind == 0 else 'sg'}{i}"
        shift, scale, gate = mod_of(kind, i)
        g = norm_g[kind][i].reshape(1, D)
        xl, h = _pre(x_cur, res, gate_prev, g, scale, shift, f"pre_{tag}")
        keys, comm = gather_of([(layer + 1, 0)])
        if comm is None:
            proj = _mm_nn_in(h, gathered[(layer, 0)], 0, f"proj_{tag}")
        else:
            proj, got = _mm_nn_in(h, gathered[(layer, 0)], 0, f"proj_{tag}", comm)
            keep_gathered(keys, got)
        rec = dict(xl=xl, h=h, proj=proj, g=g, scale=scale, gate=gate)
        keys, comm = gather_of([(layer, 1), (layer + 1, 1), (layer + 2, 1)] if kind == 0 else [])
        if kind == 0:
            qr, kr, vb = _rope_qkv(proj, cosf, sinf, W, f"rope_{tag}")
            (attn, lse), got = _attn_fwd(qr, kr, vb, bias, f"attn_{tag}", comm)
            y = _ab_mix(attn, proj, conv_full[i], W, f"mix_{tag}")
            rec.update(qr=qr, kr=kr, vb=vb, attn=attn, lse=lse)
        else:
            y, got = _sgu_fwd(proj, ln_g_full[i].reshape(1, D), ln_b_full[i].reshape(1, D), sg_w_s[i], b_s_t[i],
                              f"sgu_{tag}", comm)
        keep_gathered(keys, got)
        out = _mm_nn_out(y, gathered[(layer, 1)], 0, f"out_{tag}")
        rec.update(y=y, out=out)
        saved.append(rec)
        x_cur, res, gate_prev = xl, out, gate

    dx, loss_part, d_final_g = _loss_head(x_cur, res, gate_prev, final_norm_g.reshape(1, D), tgt, "loss_head")
    loss = lax.psum(loss_part[0, 0], ("x", "y", "c"))

    c_idx = ci.reshape(1).astype(jnp.int32)
    big_res = {}
    pending = None

    def sum_and_update(done, k, from_chips, part=(0, 1)):
        nm = big_names[done % 2][k]
        w, m, v = big_w[done % 2][k]
        flat = lambda a: a.reshape(L * a.shape[1], a.shape[2])
        row0 = (done // 2) * w.shape[1] + part[0] * (w.shape[1] // part[1])
        big_res[nm] = _sum_adam(from_chips, flat(w), flat(m), flat(v), row0, big_res.get(nm),
                                f"adam_{nm}{done // 2}_{part[0]}")

    def finish_layer(done, from_chips):
        for k in range(2):
            sum_and_update(done, k, from_chips[k])

    dm = [[None] * L, [None] * L]
    d_norm = [[None] * L, [None] * L]
    d_conv, d_lng, d_lnb, d_ws, d_bs = [None] * L, [None] * L, [None] * L, [None] * L, [None] * L
    for layer in reversed(range(2 * L)):
        kind, i = layer % 2, layer // 2
        tag = f"{'ab' if kind == 0 else 'sg'}{i}"
        rec = saved[layer]
        w_in_l, w_out_l = gathered[(layer, 0)], gathered[(layer, 1)]
        dout, dgate = _post_bwd(dx, rec["out"], rec["gate"], f"post_bwd_{tag}")
        dy = _mm_nt_out(dout, w_out_l, 0, f"dy_{tag}")
        dwo = _mm_tn_out(rec["y"], dout, 0, 1, None, f"dwout_{tag}")
        if kind == 0:
            do, delta = _dattn_prep(dy, rec["proj"], rec["attn"], W, f"dattn_{tag}")
            comm = _ToChips(pending[1]) if pending else None
            (dqr, dkr, dvv), got = _attn_bwd(rec["qr"], rec["kr"], rec["vb"], do, rec["lse"], delta, bias,
                                             f"attn_bwd_{tag}", comm)
            if pending:
                finish_layer(pending[0], got)
            dproj, dcw = _ab_bwd(dy, rec["attn"], rec["proj"], dqr, dkr, dvv, cosf, sinf, conv_full[i], W,
                                 f"mix_bwd_{tag}")
            d_conv[i] = dcw[:3]
            earlier = None
        else:
            dproj, d_ws[i], dbs, d_lng[i], d_lnb[i] = _sgu_bwd(
                rec["proj"], dy, ln_g_full[i].reshape(1, D), ln_b_full[i].reshape(1, D),
                sg_w_s[i], w_s_t[i], b_s_t[i], f"sgu_bwd_{tag}")
            d_bs[i] = dbs.reshape(C_GROUPS, C_CHUNK)
            earlier = pending
        if layer == 0:
            h_l = rec["h"]
            dwi_a = _mm_tn_in(h_l, dproj, 0, 1, None, f"dwin_a_{tag}", part=(0, 2))
            g_a = [dwi_a.reshape(NDEV, D // 2, -1), dwo.reshape(NDEV, kb, D)]
            dwi_b, sib_a = _mm_tn_in(h_l, dproj, 0, 1, None, f"dwin_b_{tag}", _ToSibling(g_a), part=(1, 2))
            p_a = [_add_sibling(g, r, c_idx, f"rs_add_a{k}_{tag}") for k, (g, r) in enumerate(zip(g_a, sib_a))]
            g_b = [dwi_b.reshape(NDEV, D // 2, -1)]
            both = _Both(_ToSibling(g_b), _ToChips(p_a))
            dh, got = _mm_nt_in(dproj, w_in_l, 0, f"dh_{tag}", both)
            sib_b, chips_a = both.split_results(got)
            p_b = [_add_sibling(g_b[0], sib_b[0], c_idx, f"rs_add_b_{tag}")]
            (dx, dshift, dscale, d_norm[kind][i]), chips_b = _pre_bwd(
                rec["xl"], dh, dx, rec["g"], rec["scale"], f"pre_bwd_{tag}", _ToChips(p_b))
            sum_and_update(0, 0, chips_a[0], (0, 2))
            sum_and_update(0, 1, chips_a[1])
            sum_and_update(0, 0, chips_b[0], (1, 2))
            dm[kind][i] = jnp.concatenate([dshift, dscale, dgate], axis=1).reshape(3 * D)
            continue
        if earlier:
            dwi, got_in = _mm_tn_in(rec["h"], dproj, 0, 1, None, f"dwin_{tag}", _ToChips(earlier[1][:1]))
        else:
            dwi = _mm_tn_in(rec["h"], dproj, 0, 1, None, f"dwin_{tag}")
        grads = [dwi.reshape(NDEV, D, -1), dwo.reshape(NDEV, kb, D)]
        if earlier:
            both = _Both(_ToSibling(grads), _ToChips(earlier[1][1:]))
            dh, got = _mm_nt_in(dproj, w_in_l, 0, f"dh_{tag}", both)
            from_sibling, got_rest = both.split_results(got)
            sum_and_update(earlier[0], 0, got_in[0], (0, 2))
            sum_and_update(earlier[0], 0, got_rest[0], (1, 2))
            sum_and_update(earlier[0], 1, got_rest[1])
        else:
            dh, from_sibling = _mm_nt_in(dproj, w_in_l, 0, f"dh_{tag}", _ToSibling(grads))
        name_in, name_out = (f"rs_add_{nm}{i}" for nm in big_names[kind])
        if kind == 0:
            parts = [_add_sibling(grads[0], from_sibling[0], c_idx, f"{name_in}_{p}", (p, 2)) for p in range(2)]
        else:
            parts = [_add_sibling(grads[0], from_sibling[0], c_idx, name_in)]
        pending = (layer, parts + [_add_sibling(grads[1], from_sibling[1], c_idx, name_out)])
        (dx, dshift, dscale, d_norm[kind][i]), _ = _pre_bwd(
            rec["xl"], dh, dx, rec["g"], rec["scale"], f"pre_bwd_{tag}")
        dm[kind][i] = jnp.concatenate([dshift, dscale, dgate], axis=1).reshape(3 * D)
    grad_x = dx.reshape(1, S, D)
    for kind in range(2):
        for k in range(2):
            nm = big_names[kind][k]
            big_res[nm] = [o.reshape(big_w[kind][k][0].shape) for o in big_res[nm]]

    stack = lambda xs: jnp.stack(xs)
    pack_items = [stack(dm[0]), stack(dm[1]), stack(d_norm[0]).reshape(L, D), stack(d_conv),
                  stack(d_norm[1]).reshape(L, D), stack(d_lng).reshape(L, D), stack(d_lnb).reshape(L, D),
                  stack(d_ws), stack(d_bs), d_final_g]
    (g3,) = _comm_only(_Gather([_pack(pack_items)]), "ag_grads")
    P = g3.shape[1] * g3.shape[2]
    tot = _sum_rows(g3, "sum_small").reshape(P)
    g3 = g3.reshape(NDEV, P)
    sizes = [int(np.prod(p.shape)) for p in pack_items]
    offs = np.concatenate([[0], np.cumsum(sizes)]).tolist()
    seg = lambda k, shape: tot[offs[k]:offs[k + 1]].reshape(shape)

    def shard(full, n):
        return lax.dynamic_slice_in_dim(full, dev * n, n, axis=full.ndim - 1)

    g_ab_b_mod = seg(0, (L, 3 * D))
    g_sg_b_mod = shard(seg(1, (L, 3 * D)), n_mod)
    g_ab_norm = seg(2, (L, D))
    g_conv = shard(seg(3, (L, 3, W)), W // NDEV)
    g_sg_norm = shard(seg(4, (L, D)), kb)
    g_ln_g = shard(seg(5, (L, D)), kb)
    g_ln_b = shard(seg(6, (L, D)), kb)
    g_w_s = seg(7, sg_w_s.shape)
    g_b_s = seg(8, sg_b_s.shape)
    g_final = seg(9, (D,))

    small_w = [("ab_norm_g", g_ab_norm, ab_norm_g, m_ab_norm_g, v_ab_norm_g),
               ("ab_b_mod", g_ab_b_mod, ab_b_mod, m_ab_b_mod, v_ab_b_mod),
               ("ab_conv_w", g_conv, ab_conv_w, m_ab_conv_w, v_ab_conv_w),
               ("sg_norm_g", g_sg_norm, sg_norm_g, m_sg_norm_g, v_sg_norm_g),
               ("sg_b_mod", g_sg_b_mod, sg_b_mod, m_sg_b_mod, v_sg_b_mod),
               ("sg_ln_g", g_ln_g, sg_ln_g, m_sg_ln_g, v_sg_ln_g),
               ("sg_ln_b", g_ln_b, sg_ln_b, m_sg_ln_b, v_sg_ln_b),
               ("sg_w_s", g_w_s, sg_w_s, m_sg_w_s, v_sg_w_s),
               ("sg_b_s", g_b_s, sg_b_s, m_sg_b_s, v_sg_b_s),
               ("final_norm_g", g_final, final_norm_g, m_final_norm_g, v_final_norm_g)]
    packed = [_pack([t[k] for t in small_w]) for k in (1, 2, 3, 4)]
    upd = _adam_only(*packed, "adam_small")
    small_res = {}
    o = 0
    for nm, g, w, _, _ in small_w:
        size = int(np.prod(w.shape))
        small_res[nm] = [g] + [u.reshape(-1)[o:o + size].reshape(w.shape) for u in upd]
        o += size

    KP = 128
    sc_t = jnp.pad((c_all * jax.nn.sigmoid(c_all)).T, ((0, 0), (0, KP - NDEV)))
    mod_res = {}
    for kind, nm, (w, m, v) in ((0, "ab_w_mod", (ab_w_mod, m_ab_w_mod, v_ab_w_mod)),
                                (1, "sg_w_mod", (sg_w_mod, m_sg_w_mod, v_sg_w_mod))):
        dm_all = g3[:, offs[kind]:offs[kind + 1]].reshape(NDEV, L, 3 * D)
        cols = jnp.pad(shard(dm_all, n_mod).transpose(1, 0, 2), ((0, 0), (0, KP - NDEV), (0, 0)))
        mod_res[nm] = _wmod_grad_adam(sc_t, cols, w, m, v, f"adam_{nm}")

    order = ["ab_norm_g", "ab_w_mod", "ab_b_mod", "ab_w_in", "ab_conv_w", "ab_w_out", "sg_norm_g", "sg_w_mod",
             "sg_b_mod", "sg_w_in", "sg_ln_g", "sg_ln_b", "sg_w_s", "sg_b_s", "sg_w_out", "final_norm_g"]
    res = {**big_res, **small_res, **mod_res}
    outs = [loss, grad_x]
    for k in range(4):
        outs += [res[nm][k] for nm in order]
    return tuple(outs)
```

```python
import functools
import math

import numpy as np
import jax
import jax.numpy as jnp
from jax import lax
from jax.experimental import pallas as pl
from jax.experimental.pallas import tpu as pltpu

F32 = jnp.float32
BF16 = jnp.bfloat16

NDEV = 8
NCHIP = 4
EPS = 1e-6
HEAD_DIM = 128
ROPE_THETA = 10000.0
DILATED_PATTERNS = ((128, 1), (512, 4), (2048, 16))
NEG_INF = -1e30
C_CHUNK = 128
C_GROUPS = 8
ADAM_LR = 0.001
ADAM_B1 = 0.9
ADAM_B2 = 0.999
ADAM_EPS = 1e-08
ADAM_WD = 0.01
ADAM_STEP = 10
GELU_K = math.sqrt(2.0 / math.pi)
GELU_C = 0.044715

VMEM_LIMIT_BYTES = 56 * 1024 * 1024
ATTN_TILE = 512
HEADS_PER_STEP = 4
ATTN_ROW_CHUNK = 256
LANES = 128
ROW_TILE = 256
MESH = pl.DeviceIdType.MESH
ANY = pl.BlockSpec(memory_space=pl.ANY)


def _cp(*sem):
    return pltpu.CompilerParams(dimension_semantics=sem, vmem_limit_bytes=VMEM_LIMIT_BYTES)


def _sigmoid(z):
    return 0.5 * (jnp.tanh(0.5 * z) + 1.0)


def _silu_and_grad(z):
    s = _sigmoid(z)
    return z * s, s * (1.0 + z * (1.0 - s))


def _gelu_and_grad(x):
    x2 = x * x
    t = jnp.tanh(GELU_K * (x + GELU_C * x2 * x))
    g = 0.5 * x * (1.0 + t)
    dg = 0.5 * (1.0 + t) + 0.5 * x * (1.0 - t * t) * (GELU_K * (1.0 + 3.0 * GELU_C * x2))
    return g, dg


def _position():
    return lax.axis_index("x"), lax.axis_index("y"), lax.axis_index("c")


def _chips(x, y):
    return [(1 - x, y), (x, 1 - y), (1 - x, 1 - y)]


class _Gather:
    def __init__(self, arrs):
        n = len(arrs)
        self.arrs = list(arrs)
        self.out_shape = [jax.ShapeDtypeStruct((NDEV,) + a.shape, a.dtype) for a in arrs]
        self.scratch = [pltpu.SemaphoreType.DMA((n, 7)), pltpu.SemaphoreType.DMA((n, 7)),
                        pltpu.SemaphoreType.DMA((n,))]

    def _copies(self, ins, outs, sems, own=True):
        send_sems, recv_sems, local_sems = sems
        x, y, c = _position()

        def copy(a, k, block, to, src=None):
            dst = outs[a].at[4 * block[0] + 2 * block[1] + block[2]]
            return pltpu.make_async_remote_copy(
                src_ref=dst if src is None else src, dst_ref=dst,
                send_sem=send_sems.at[a, k], recv_sem=recv_sems.at[a, k],
                device_id=to, device_id_type=MESH)

        n = len(ins)
        me, sibling = (x, y, c), (x, y, 1 - c)
        mine, first = [], []
        if own:
            mine = [pltpu.make_async_copy(ins[a], outs[a].at[4 * x + 2 * y + c], local_sems.at[a])
                    for a in range(n)]
            for a in range(n):
                first.append(copy(a, 0, me, sibling, src=ins[a]))
                first += [copy(a, 1 + j, me, (*chip, c), src=ins[a]) for j, chip in enumerate(_chips(x, y))]
        return copy, mine, first

    def start(self, ins, outs, sems):
        _, mine, first = self._copies(ins, outs, sems)
        for cp in mine + first:
            cp.start()

    def middle(self, ins, outs, sems):
        copy = self._copies(ins, outs, sems, own=False)[0]
        x, y, c = _position()
        me, sibling = (x, y, c), (x, y, 1 - c)
        for j, chip in enumerate(_chips(x, y)):
            for a in range(len(ins)):
                copy(a, 1 + j, (*chip, c), me).wait_recv()
                copy(a, 4 + j, (*chip, c), sibling).start()

    def finish(self, ins, outs, sems):
        copy, mine, first = self._copies(ins, outs, sems)
        x, y, c = _position()
        me, sibling = (x, y, c), (x, y, 1 - c)
        passed = [copy(a, 4 + j, (*chip, c), sibling)
                  for j, chip in enumerate(_chips(x, y)) for a in range(len(ins))]
        for a in range(len(ins)):
            copy(a, 0, sibling, me).wait_recv()
            for j, chip in enumerate(_chips(x, y)):
                copy(a, 4 + j, (*chip, 1 - c), me).wait_recv()
        for cp in first + passed:
            cp.wait_send()
        for cp in mine:
            cp.wait()


class _GatherDirect:
    def __init__(self, arrs):
        n = len(arrs)
        self.arrs = list(arrs)
        self.out_shape = [jax.ShapeDtypeStruct((NDEV,) + a.shape, a.dtype) for a in arrs]
        self.scratch = [pltpu.SemaphoreType.DMA((n, 7)), pltpu.SemaphoreType.DMA((n, 7)),
                        pltpu.SemaphoreType.DMA((n,))]

    def _copies(self, ins, outs, sems, arrivals):
        send_sems, recv_sems, local_sems = sems
        x, y, c = _position()
        mine = [pltpu.make_async_copy(ins[a], outs[a].at[4 * x + 2 * y + c], local_sems.at[a])
                for a in range(len(ins))]
        sends, recvs = [], []
        for a in range(len(ins)):
            for k in range(1, NDEV):
                px = 1 - x if k & 4 else x
                py = 1 - y if k & 2 else y
                pc = 1 - c if k & 1 else c
                sends.append(pltpu.make_async_remote_copy(
                    src_ref=ins[a], dst_ref=outs[a].at[4 * x + 2 * y + c],
                    send_sem=send_sems.at[a, k - 1], recv_sem=recv_sems.at[a, k - 1],
                    device_id=(px, py, pc), device_id_type=MESH))
                if arrivals:
                    slot = outs[a].at[4 * px + 2 * py + pc]
                    recvs.append(pltpu.make_async_remote_copy(
                        src_ref=slot, dst_ref=slot, send_sem=send_sems.at[a, k - 1], recv_sem=recv_sems.at[a, k - 1],
                        device_id=(px, py, pc), device_id_type=MESH))
        return mine, sends, recvs

    def start(self, ins, outs, sems):
        mine, sends, _ = self._copies(ins, outs, sems, False)
        for cp in mine + sends:
            cp.start()

    def finish(self, ins, outs, sems):
        mine, sends, recvs = self._copies(ins, outs, sems, True)
        for cp in recvs:
            cp.wait_recv()
        for cp in sends:
            cp.wait_send()
        for cp in mine:
            cp.wait()


class _ToSibling:
    def __init__(self, gs):
        n = len(gs)
        self.arrs = list(gs)
        self.out_shape = [jax.ShapeDtypeStruct((NCHIP,) + g.shape[1:], g.dtype) for g in gs]
        self.scratch = [pltpu.SemaphoreType.DMA((n, NCHIP)), pltpu.SemaphoreType.DMA((n, NCHIP))]

    def _copies(self, ins, outs, sems):
        send_sems, recv_sems = sems
        x, y, c = _position()
        return [pltpu.make_async_remote_copy(
            src_ref=ins[a].at[2 * k + (1 - c)], dst_ref=outs[a].at[k],
            send_sem=send_sems.at[a, k], recv_sem=recv_sems.at[a, k],
            device_id=(x, y, 1 - c), device_id_type=MESH) for a in range(len(ins)) for k in range(NCHIP)]

    def start(self, ins, outs, sems):
        for cp in self._copies(ins, outs, sems):
            cp.start()

    def finish(self, ins, outs, sems):
        copies = self._copies(ins, outs, sems)
        for cp in copies:
            cp.wait_recv()
        for cp in copies:
            cp.wait_send()


class _ToChips:
    def __init__(self, ps):
        n = len(ps)
        self.arrs = list(ps)
        self.out_shape = [jax.ShapeDtypeStruct(p.shape, p.dtype) for p in ps]
        self.scratch = [pltpu.SemaphoreType.DMA((n, 3)), pltpu.SemaphoreType.DMA((n, 3)),
                        pltpu.SemaphoreType.DMA((n,))]

    def _copies(self, ins, outs, sems, arrivals):
        send_sems, recv_sems, local_sems = sems
        x, y, c = _position()
        mychip = 2 * x + y
        n = len(ins)
        mine = [pltpu.make_async_copy(ins[a].at[mychip], outs[a].at[mychip], local_sems.at[a]) for a in range(n)]
        sends, recvs = [], []
        for a in range(n):
            for j, chip in enumerate(_chips(x, y)):
                sends.append(pltpu.make_async_remote_copy(
                    src_ref=ins[a].at[2 * chip[0] + chip[1]], dst_ref=outs[a].at[mychip],
                    send_sem=send_sems.at[a, j], recv_sem=recv_sems.at[a, j],
                    device_id=(*chip, c), device_id_type=MESH))
                if arrivals:
                    slot = outs[a].at[2 * chip[0] + chip[1]]
                    recvs.append(pltpu.make_async_remote_copy(
                        src_ref=slot, dst_ref=slot, send_sem=send_sems.at[a, j], recv_sem=recv_sems.at[a, j],
                        device_id=(*chip, c), device_id_type=MESH))
        return mine, sends, recvs

    def start(self, ins, outs, sems):
        mine, sends, _ = self._copies(ins, outs, sems, False)
        for cp in mine + sends:
            cp.start()

    def finish(self, ins, outs, sems):
        mine, sends, recvs = self._copies(ins, outs, sems, True)
        for cp in recvs:
            cp.wait_recv()
        for cp in sends:
            cp.wait_send()
        for cp in mine:
            cp.wait()


HOSTED_MIDDLE_AT = 0.95


def _middle_of(comm, ins, outs, sems):
    if hasattr(comm, "middle"):
        comm.middle(ins, outs, sems)


class _Both:
    def __init__(self, first, second):
        self.parts = (first, second)
        self.arrs = first.arrs + second.arrs
        self.out_shape = first.out_shape + second.out_shape
        self.scratch = first.scratch + second.scratch

    def _split(self, ins, outs, sems):
        a, _ = self.parts
        ni, no, ns = len(a.arrs), len(a.out_shape), len(a.scratch)
        return (ins[:ni], outs[:no], sems[:ns]), (ins[ni:], outs[no:], sems[ns:])

    def start(self, ins, outs, sems):
        for part, refs in zip(self.parts, self._split(ins, outs, sems)):
            part.start(*refs)

    def middle(self, ins, outs, sems):
        for part, refs in zip(self.parts, self._split(ins, outs, sems)):
            _middle_of(part, *refs)

    def finish(self, ins, outs, sems):
        for part, refs in zip(self.parts, self._split(ins, outs, sems)):
            part.finish(*refs)

    def split_results(self, res):
        no = len(self.parts[0].out_shape)
        return res[:no], res[no:]


def _comm_only(comm, name):
    n_in, n_out = len(comm.arrs), len(comm.out_shape)

    def body(*refs):
        ins, outs, sems = refs[:n_in], refs[n_in:n_in + n_out], refs[n_in + n_out:]
        comm.start(ins, outs, sems)
        _middle_of(comm, ins, outs, sems)
        comm.finish(ins, outs, sems)

    return pl.pallas_call(
        body, name=name, out_shape=comm.out_shape, in_specs=[ANY] * n_in, out_specs=[ANY] * n_out,
        scratch_shapes=comm.scratch,
    )(*comm.arrs)


def _hosted_call(body, operands, *, name, grid, in_specs, out_specs, out_shape, scratch_shapes=(), sem=(),
                 aliases=None, comm=None):
    single = not isinstance(out_shape, (list, tuple))
    o_specs = [out_specs] if single else list(out_specs)
    o_shape = [out_shape] if single else list(out_shape)
    n_in, n_out, n_scr = len(in_specs), len(o_shape), len(scratch_shapes)
    if comm is None:
        res = pl.pallas_call(body, name=name, grid=grid, in_specs=list(in_specs), out_specs=o_specs,
                             out_shape=o_shape, scratch_shapes=list(scratch_shapes),
                             input_output_aliases=aliases or {}, compiler_params=_cp(*sem))(*operands)
        return (res[0] if single else res), []
    c_in, c_out = len(comm.arrs), len(comm.out_shape)

    def wrapped(*refs):
        ins, cins = refs[:n_in], refs[n_in:n_in + c_in]
        o0 = n_in + c_in
        outs, couts = refs[o0:o0 + n_out], refs[o0 + n_out:o0 + n_out + c_out]
        s0 = o0 + n_out + c_out
        scr, csems = refs[s0:s0 + n_scr], refs[s0 + n_scr:]
        pids = [pl.program_id(a) for a in range(len(grid))]
        step = functools.reduce(lambda acc, pg: acc * pg[1] + pg[0], zip(pids, grid), 0)
        total = int(np.prod(grid))
        late = min(total - 1, max(1, int(total * HOSTED_MIDDLE_AT)))

        @pl.when(step == 0)
        def _():
            comm.start(cins, couts, csems)

        @pl.when(step == late)
        def _():
            _middle_of(comm, cins, couts, csems)

        body(*ins, *outs, *scr)

        @pl.when(step == total - 1)
        def _():
            comm.finish(cins, couts, csems)

    res = pl.pallas_call(
        wrapped, name=name, grid=grid, in_specs=list(in_specs) + [ANY] * c_in, out_specs=o_specs + [ANY] * c_out,
        out_shape=o_shape + comm.out_shape, scratch_shapes=list(scratch_shapes) + comm.scratch,
        input_output_aliases=aliases or {}, compiler_params=_cp(*(["arbitrary"] * len(grid))),
    )(*operands, *comm.arrs)
    return (res[0] if single else res[:n_out]), res[n_out:]


def _adamw(w, g, m, v):
    m2 = ADAM_B1 * m + (1.0 - ADAM_B1) * g
    v2 = ADAM_B2 * v + (1.0 - ADAM_B2) * (g * g)
    m_hat = m2 / (1.0 - ADAM_B1 ** ADAM_STEP)
    v_hat = v2 / (1.0 - ADAM_B2 ** ADAM_STEP)
    delta = -ADAM_LR * (m_hat / (jnp.sqrt(v_hat) + ADAM_EPS) + ADAM_WD * w)
    return delta, m2, v2


def _add_sibling(g, recv, c_idx, name, part=(0, 1)):
    _, R, C = g.shape
    R = R // part[1]
    tr = min(R, 512)
    first = part[0] * (R // tr)

    def body(c_ref, g_ref, r_ref, o_ref):
        o_ref[...] = (g_ref[...] + r_ref[...]).astype(BF16)

    return pl.pallas_call(
        body, name=name,
        grid_spec=pltpu.PrefetchScalarGridSpec(
            num_scalar_prefetch=1, grid=(NCHIP, R // tr),
            in_specs=[pl.BlockSpec((1, tr, C), lambda k, i, c_ref: (2 * k + c_ref[0], first + i, 0)),
                      pl.BlockSpec((1, tr, C), lambda k, i, c_ref: (k, first + i, 0))],
            out_specs=pl.BlockSpec((1, tr, C), lambda k, i, c_ref: (k, i, 0))),
        out_shape=jax.ShapeDtypeStruct((NCHIP, R, C), BF16),
        compiler_params=_cp("parallel", "parallel"),
    )(c_idx, g, recv)


def _sum_adam(parts, w, m, v, row0, prev, name):
    K, R, C = parts.shape
    LR = w.shape[0]
    tr = min(R, 256)
    nb = R // tr
    first = row0 // tr

    def body(p_ref, w_ref, m_ref, v_ref, *rest):
        g_ref, d_ref, m2_ref, v2_ref = rest[-4:]
        g = p_ref[0].astype(F32)
        for k in range(1, K):
            g = g + p_ref[k].astype(F32)
        delta, m2, v2 = _adamw(w_ref[...], g, m_ref[...], v_ref[...])
        g_ref[...] = g
        d_ref[...] = delta
        m2_ref[...] = m2
        v2_ref[...] = v2

    blk = pl.BlockSpec((tr, C), lambda i: (first + i, 0))
    shp = jax.ShapeDtypeStruct((LR, C), F32)
    operands = [parts, w, m, v] + (list(prev) if prev is not None else [])
    return pl.pallas_call(
        body, name=name, grid=(nb,),
        in_specs=[pl.BlockSpec((K, tr, C), lambda i: (0, i, 0)), blk, blk, blk] + [ANY] * (len(operands) - 4),
        out_specs=[blk] * 4, out_shape=[shp] * 4,
        input_output_aliases={4 + k: k for k in range(len(operands) - 4)},
        compiler_params=_cp("parallel"),
    )(*operands)


def _sum_rows(parts, name):
    K, R, C = parts.shape
    tr = min(R, 256)
    while R % tr:
        tr //= 2

    def body(p_ref, o_ref):
        g = p_ref[0]
        for k in range(1, K):
            g = g + p_ref[k]
        o_ref[...] = g

    return pl.pallas_call(
        body, name=name, grid=(R // tr,),
        in_specs=[pl.BlockSpec((K, tr, C), lambda i: (0, i, 0))],
        out_specs=pl.BlockSpec((tr, C), lambda i: (i, 0)),
        out_shape=jax.ShapeDtypeStruct((R, C), F32),
        compiler_params=_cp("parallel"),
    )(parts)


def _adam_only(g, w, m, v, name):
    R, C = g.shape
    tr = min(R, 256)
    while R % tr:
        tr //= 2

    def body(g_ref, w_ref, m_ref, v_ref, d_ref, m2_ref, v2_ref):
        delta, m2, v2 = _adamw(w_ref[...], g_ref[...], m_ref[...], v_ref[...])
        d_ref[...] = delta
        m2_ref[...] = m2
        v2_ref[...] = v2

    blk = pl.BlockSpec((tr, C), lambda i: (i, 0))
    shp = jax.ShapeDtypeStruct((R, C), F32)
    return pl.pallas_call(
        body, name=name, grid=(R // tr,), in_specs=[blk] * 4, out_specs=[blk] * 3,
        out_shape=[shp] * 3, compiler_params=_cp("parallel"),
    )(g, w, m, v)


def _mod_fwd(c_all, w_mod, b_cols, name):
    L, D, n = w_mod.shape
    B = c_all.shape[0]

    def body(c_ref, w_ref, b_ref, o_ref):
        cv = c_ref[...]
        sc = (cv * _sigmoid(cv)).astype(BF16)
        o_ref[0] = jnp.dot(sc, w_ref[0].astype(BF16), preferred_element_type=F32) + b_ref[0]

    return pl.pallas_call(
        body, name=name, grid=(L,),
        in_specs=[pl.BlockSpec((B, D), lambda l: (0, 0)),
                  pl.BlockSpec((1, D, n), lambda l: (l, 0, 0)),
                  pl.BlockSpec((1, 1, n), lambda l: (l, 0, 0))],
        out_specs=pl.BlockSpec((1, B, n), lambda l: (l, 0, 0)),
        out_shape=jax.ShapeDtypeStruct((L, B, n), F32),
        compiler_params=_cp("parallel"),
    )(c_all, w_mod, b_cols)


def _wmod_grad_adam(sc_t, dm, w, m, v, name):
    L, D, n = w.shape
    KP = sc_t.shape[1]
    tr = min(D, 512)

    def body(s_ref, dm_ref, w_ref, m_ref, v_ref, g_ref, d_ref, m2_ref, v2_ref):
        g = jnp.dot(s_ref[...], dm_ref[0], preferred_element_type=F32,
                    precision=lax.Precision.HIGHEST)
        delta, m2, v2 = _adamw(w_ref[0], g, m_ref[0], v_ref[0])
        g_ref[0] = g
        d_ref[0] = delta
        m2_ref[0] = m2
        v2_ref[0] = v2

    blk = pl.BlockSpec((1, tr, n), lambda l, i: (l, i, 0))
    shp = jax.ShapeDtypeStruct((L, D, n), F32)
    return pl.pallas_call(
        body, name=name, grid=(L, D // tr),
        in_specs=[pl.BlockSpec((tr, KP), lambda l, i: (i, 0)),
                  pl.BlockSpec((1, KP, n), lambda l, i: (l, 0, 0)), blk, blk, blk],
        out_specs=[blk] * 4, out_shape=[shp] * 4,
        compiler_params=_cp("parallel", "parallel"),
    )(sc_t, dm, w, m, v)


def _vec_spec(D):
    return pl.BlockSpec((1, D), lambda i: (0, 0))


def _pre(x, res, gate, g, scale, shift, name):
    S, D = x.shape
    tr = min(S, ROW_TILE)
    has_res = res is not None
    row = pl.BlockSpec((tr, D), lambda i: (i, 0))

    def body(*refs):
        if has_res:
            x_ref, r_ref, gate_ref, g_ref, sc_ref, sh_ref, xl_ref, h_ref = refs
            xv = x_ref[...] + gate_ref[...] * r_ref[...]
            xl_ref[...] = xv
        else:
            x_ref, g_ref, sc_ref, sh_ref, h_ref = refs
            xv = x_ref[...]
        r = lax.rsqrt(jnp.mean(xv * xv, axis=-1, keepdims=True) + EPS)
        y = (xv * r) * g_ref[...]
        h_ref[...] = (y * (1.0 + sc_ref[...]) + sh_ref[...]).astype(BF16)

    vec = _vec_spec(D)
    if has_res:
        xl, h = pl.pallas_call(
            body, name=name, grid=(S // tr,),
            in_specs=[row, row, vec, vec, vec, vec], out_specs=[row, row],
            out_shape=[jax.ShapeDtypeStruct((S, D), F32), jax.ShapeDtypeStruct((S, D), BF16)],
            compiler_params=_cp("parallel"),
        )(x, res, gate, g, scale, shift)
        return xl, h
    h = pl.pallas_call(
        body, name=name, grid=(S // tr,),
        in_specs=[row, vec, vec, vec], out_specs=row,
        out_shape=jax.ShapeDtypeStruct((S, D), BF16),
        compiler_params=_cp("parallel"),
    )(x, g, scale, shift)
    return x, h


def _pre_bwd(xl, dh, dx_in, g, scale, name, comm=None):
    S, D = xl.shape
    tr = min(S, ROW_TILE)
    nsteps = S // tr
    row = pl.BlockSpec((tr, D), lambda i: (i, 0))
    vec = _vec_spec(D)

    def body(x_ref, dh_ref, dxin_ref, g_ref, sc_ref, dx_ref, dsh_ref, dsc_ref, dg_ref, acc_sh, acc_t):
        i = pl.program_id(0)
        xv = x_ref[...]
        dh = dh_ref[...]
        r = lax.rsqrt(jnp.mean(xv * xv, axis=-1, keepdims=True) + EPS)
        xn = xv * r
        part_sh = jnp.sum(dh.reshape(tr // 8, 8, D), axis=0)
        part_t = jnp.sum((dh * xn).reshape(tr // 8, 8, D), axis=0)

        @pl.when(i == 0)
        def _():
            acc_sh[...] = part_sh
            acc_t[...] = part_t

        @pl.when(i > 0)
        def _():
            acc_sh[...] += part_sh
            acc_t[...] += part_t

        dxn = dh * (g_ref[...] * (1.0 + sc_ref[...]))
        dx_ref[...] = dxin_ref[...] + r * (dxn - xn * jnp.mean(dxn * xn, axis=-1, keepdims=True))

        @pl.when(i == nsteps - 1)
        def _():
            t = jnp.sum(acc_t[...], axis=0, keepdims=True)
            dsh_ref[...] = jnp.sum(acc_sh[...], axis=0, keepdims=True)
            dsc_ref[...] = t * g_ref[...]
            dg_ref[...] = t * (1.0 + sc_ref[...])

    v = jax.ShapeDtypeStruct((1, D), F32)
    return _hosted_call(
        body, [xl, dh, dx_in, g, scale], name=name, grid=(nsteps,),
        in_specs=[row, row, row, vec, vec], out_specs=[row, vec, vec, vec],
        out_shape=[jax.ShapeDtypeStruct((S, D), F32), v, v, v],
        scratch_shapes=[pltpu.VMEM((8, D), F32), pltpu.VMEM((8, D), F32)],
        sem=("arbitrary",), comm=comm)


def _post_bwd(dx, out, gate, name):
    S, D = dx.shape
    tr = min(S, ROW_TILE)
    nsteps = S // tr
    row = pl.BlockSpec((tr, D), lambda i: (i, 0))
    vec = _vec_spec(D)

    def body(dx_ref, o_ref, gate_ref, do_ref, dg_ref, acc):
        i = pl.program_id(0)
        dxv = dx_ref[...]
        do_ref[...] = (dxv * gate_ref[...]).astype(BF16)
        part = jnp.sum((dxv * o_ref[...]).reshape(tr // 8, 8, D), axis=0)

        @pl.when(i == 0)
        def _():
            acc[...] = part

        @pl.when(i > 0)
        def _():
            acc[...] += part

        @pl.when(i == nsteps - 1)
        def _():
            dg_ref[...] = jnp.sum(acc[...], axis=0, keepdims=True)

    return pl.pallas_call(
        body, name=name, grid=(nsteps,),
        in_specs=[row, row, vec], out_specs=[row, vec],
        out_shape=[jax.ShapeDtypeStruct((S, D), BF16), jax.ShapeDtypeStruct((1, D), F32)],
        scratch_shapes=[pltpu.VMEM((8, D), F32)],
        compiler_params=_cp("arbitrary"),
    )(dx, out, gate)


def _loss_head(x, res, gate, gf, tgt, name):
    S, D = x.shape
    tr = min(S, ROW_TILE)
    nsteps = S // tr
    row = pl.BlockSpec((tr, D), lambda i: (i, 0))
    vec = _vec_spec(D)

    def body(x_ref, r_ref, gate_ref, gf_ref, t_ref, dx_ref, loss_ref, dgf_ref, acc, lacc):
        i = pl.program_id(0)
        xv = x_ref[...] + gate_ref[...] * r_ref[...]
        r = lax.rsqrt(jnp.mean(xv * xv, axis=-1, keepdims=True) + EPS)
        xn = xv * r
        err = xn * gf_ref[...] - t_ref[...]
        row_loss = jnp.mean(err * err, axis=-1, keepdims=True)
        lpart = 0.5 * jnp.sum(row_loss, axis=0, keepdims=True)
        dy = err * (1.0 / D)
        part = jnp.sum((dy * xn).reshape(tr // 8, 8, D), axis=0)

        @pl.when(i == 0)
        def _():
            acc[...] = part
            lacc[...] = lpart

        @pl.when(i > 0)
        def _():
            acc[...] += part
            lacc[...] += lpart

        dxn = dy * gf_ref[...]
        dx_ref[...] = r * (dxn - xn * jnp.mean(dxn * xn, axis=-1, keepdims=True))

        @pl.when(i == nsteps - 1)
        def _():
            dgf_ref[...] = jnp.sum(acc[...], axis=0, keepdims=True)
            loss_ref[...] = lacc[...]

    return pl.pallas_call(
        body, name=name, grid=(nsteps,),
        in_specs=[row, row, vec, vec, row],
        out_specs=[row, pl.BlockSpec((1, 1), lambda i: (0, 0)), vec],
        out_shape=[jax.ShapeDtypeStruct((S, D), F32), jax.ShapeDtypeStruct((1, 1), F32),
                   jax.ShapeDtypeStruct((1, D), F32)],
        scratch_shapes=[pltpu.VMEM((8, D), F32), pltpu.VMEM((1, 1), F32)],
        compiler_params=_cp("arbitrary"),
    )(x, res, gate, gf, tgt)


NN = (((1,), (0,)), ((), ()))
NT = (((1,), (1,)), ((), ()))
TN = (((0,), (0,)), ((), ()))


def _mm(name, a, b, out_shape, grid, a_spec, b_spec, o_spec, dims, a2d, b2d, k_axis, sem, alias=None, comm=None):
    def body(*refs):
        a_ref, b_ref, o_ref = refs[0], refs[1], refs[-1]
        r = lax.dot_general(a_ref[...].reshape(a2d), b_ref[...].reshape(b2d), dims,
                            preferred_element_type=F32)
        r = r.reshape(o_ref.shape)
        if k_axis is None:
            o_ref[...] = r.astype(o_ref.dtype)
        else:
            k = pl.program_id(k_axis)

            @pl.when(k == 0)
            def _():
                o_ref[...] = r

            @pl.when(k > 0)
            def _():
                o_ref[...] += r

    operands, in_specs, aliases = [a, b], [a_spec, b_spec], {}
    if alias is not None:
        operands.append(alias)
        in_specs.append(ANY)
        aliases = {2: 0}
    res, extra = _hosted_call(body, operands, name=name, grid=grid, in_specs=in_specs, out_specs=o_spec,
                              out_shape=out_shape, sem=sem, aliases=aliases, comm=comm)
    return res if comm is None else (res, extra)


def _tile(n, pref):
    t = min(n, pref)
    while n % t:
        t -= 128
    return t


def _mm_nn_in(a, w, l, name, comm=None):
    M, K = a.shape
    _, _, _, n = w.shape
    tm, tn = min(M, 512), _tile(n, 1024)
    nb = n // tn
    return _mm(name, a, w, jax.ShapeDtypeStruct((M, NDEV * n), F32), (NDEV * nb, M // tm),
               pl.BlockSpec((tm, K), lambda j, i: (i, 0)),
               pl.BlockSpec((1, 1, K, tn), lambda j, i: (j // nb, l, 0, j % nb)),
               pl.BlockSpec((tm, tn), lambda j, i: (i, j)),
               NN, (tm, K), (K, tn), None, ("parallel", "parallel"), comm=comm)


def _mm_nn_out(a, w, l, name):
    M, K = a.shape
    _, _, kb, N = w.shape
    tm, tn = min(M, 512), _tile(N, 1024)
    return _mm(name, a, w, jax.ShapeDtypeStruct((M, N), F32), (N // tn, M // tm),
               pl.BlockSpec((tm, K), lambda j, i: (i, 0)),
               pl.BlockSpec((NDEV, 1, kb, tn), lambda j, i: (0, l, 0, j)),
               pl.BlockSpec((tm, tn), lambda j, i: (i, j)),
               NN, (tm, K), (K, tn), None, ("parallel", "parallel"))


def _mm_nt_in(a, w, l, name, comm=None):
    M, _ = a.shape
    _, _, K, n = w.shape
    tm, tk = min(M, 1024), _tile(K, 1024)
    gb = 2 if n <= 1024 else 1

    def body(a_ref, w_ref, o_ref):
        k = pl.program_id(2)
        r = lax.dot_general(a_ref[:, :n], w_ref[0, 0], NT, preferred_element_type=F32)
        for g in range(1, gb):
            r = r + lax.dot_general(a_ref[:, g * n:(g + 1) * n], w_ref[g, 0], NT, preferred_element_type=F32)

        @pl.when(k == 0)
        def _():
            o_ref[...] = r

        @pl.when(k > 0)
        def _():
            o_ref[...] += r

    res, extra = _hosted_call(
        body, [a, w], name=name, grid=(M // tm, K // tk, NDEV // gb),
        in_specs=[pl.BlockSpec((tm, gb * n), lambda i, j, k: (i, k)),
                  pl.BlockSpec((gb, 1, tk, n), lambda i, j, k: (k, l, j, 0))],
        out_specs=pl.BlockSpec((tm, tk), lambda i, j, k: (i, j)),
        out_shape=jax.ShapeDtypeStruct((M, K), F32),
        sem=("parallel", "parallel", "arbitrary"), comm=comm)
    return res if comm is None else (res, extra)


def _mm_nt_out(a, w, l, name):
    M, N = a.shape
    _, _, kb, _ = w.shape
    K = NDEV * kb
    tm, tk, tc = min(M, 1024), _tile(K, 1024), _tile(N, 2048)
    per = tk // kb
    return _mm(name, a, w, jax.ShapeDtypeStruct((M, K), F32), (M // tm, K // tk, N // tc),
               pl.BlockSpec((tm, tc), lambda i, j, k: (i, k)),
               pl.BlockSpec((per, 1, kb, tc), lambda i, j, k: (j, l, 0, k)),
               pl.BlockSpec((tm, tk), lambda i, j, k: (i, j)),
               NT, (tm, tc), (tk, tc), 2, ("parallel", "parallel", "arbitrary"))


def _mm_tn_in(a, b, l, L, buf, name, comm=None, part=(0, 1)):
    S, K = a.shape
    K = K // part[1]
    n = b.shape[1] // NDEV
    ts, tk, tn = min(S, 2048), _tile(K, 1024), _tile(n, 1024)
    nb = n // tn
    first = part[0] * (K // tk)
    return _mm(name, a, b, jax.ShapeDtypeStruct((NDEV, L, K, n), F32), (NDEV * nb, K // tk, S // ts),
               pl.BlockSpec((ts, tk), lambda j, i, s: (s, first + i)),
               pl.BlockSpec((ts, tn), lambda j, i, s: (s, j)),
               pl.BlockSpec((1, 1, tk, tn), lambda j, i, s: (j // nb, l, i, j % nb)),
               TN, (ts, tk), (ts, tn), 2, ("parallel", "parallel", "arbitrary"), alias=buf, comm=comm)


def _mm_tn_out(a, b, l, L, buf, name):
    S, K = a.shape
    N = b.shape[1]
    kb = K // NDEV
    ts, tk, tn = min(S, 2048), _tile(K, 1024), _tile(N, 1024)
    per = tk // kb
    return _mm(name, a, b, jax.ShapeDtypeStruct((NDEV, L, kb, N), F32), (N // tn, K // tk, S // ts),
               pl.BlockSpec((ts, tk), lambda j, i, s: (s, i)),
               pl.BlockSpec((ts, tn), lambda j, i, s: (s, j)),
               pl.BlockSpec((per, 1, kb, tn), lambda j, i, s: (i, l, 0, j)),
               TN, (ts, tk), (ts, tn), 2, ("parallel", "parallel", "arbitrary"), alias=buf)


def _attn_bias(T):
    reach = max(w // 2 for w, _ in DILATED_PATTERNS)
    hb = -(-reach // T)
    i = np.arange(T)[:, None]
    j = np.arange(T)[None, :]
    tiles = []
    for d in range(-hb, hb + 1):
        rel = j + d * T - i
        mult = np.zeros((T, T), np.float64)
        for window, dil in DILATED_PATTERNS:
            radius = window // (2 * dil)
            mult += (rel % dil == 0) & (np.abs(rel) <= radius * dil)
        tiles.append(np.where(mult > 0, np.log(np.maximum(mult, 1.0)), NEG_INF))
    return jnp.asarray(np.stack(tiles), F32)


def _rope_tables(S):
    half = HEAD_DIM // 2
    pos = jnp.arange(S, dtype=F32)
    inv = ROPE_THETA ** (-jnp.arange(half, dtype=F32) / half)
    ang = pos[:, None] * inv[None, :]
    cos, sin = jnp.cos(ang), jnp.sin(ang)
    return jnp.concatenate([cos, cos], axis=-1), jnp.concatenate([-sin, sin], axis=-1)


def _rope_apply(t, cosf, sinf, heads, sign):
    outs = []
    for hh in range(heads):
        th = t[:, hh * HEAD_DIM:(hh + 1) * HEAD_DIM]
        outs.append(th * cosf + sign * (pltpu.roll(th, HEAD_DIM // 2, 1) * sinf))
    return outs


def _rope_qkv(proj, cosf, sinf, W, name):
    S = proj.shape[0]
    tr = min(S, ROW_TILE)
    heads = W // HEAD_DIM

    def body(q_ref, k_ref, v_ref, c_ref, s_ref, qo_ref, ko_ref, vo_ref):
        cosf_v, sinf_v = c_ref[...], s_ref[...]
        for src, dst, mult in ((q_ref, qo_ref, HEAD_DIM ** -0.5), (k_ref, ko_ref, 1.0)):
            for hh, val in enumerate(_rope_apply(src[...], cosf_v, sinf_v, heads, 1.0)):
                dst[:, hh * HEAD_DIM:(hh + 1) * HEAD_DIM] = (val * mult).astype(BF16)
        vo_ref[...] = v_ref[...].astype(BF16)

    piece = lambda p: pl.BlockSpec((tr, W), lambda i: (i, p))
    tab = pl.BlockSpec((tr, HEAD_DIM), lambda i: (i, 0))
    out = pl.BlockSpec((tr, W), lambda i: (i, 0))
    shp = jax.ShapeDtypeStruct((S, W), BF16)
    return pl.pallas_call(
        body, name=name, grid=(S // tr,),
        in_specs=[piece(0), piece(1), piece(2), tab, tab], out_specs=[out] * 3, out_shape=[shp] * 3,
        compiler_params=_cp("parallel"),
    )(proj, proj, proj, cosf, sinf)


def _attn_fwd(q, k, v, bias, name, comm=None):
    S, W = q.shape
    H = W // HEAD_DIM
    nd, T, _ = bias.shape
    hb, nq = nd // 2, S // T
    scale = HEAD_DIM ** -0.5
    hp = min(H, HEADS_PER_STEP)
    rc = min(T, ATTN_ROW_CHUNK)
    wp = hp * HEAD_DIM

    def body(q_ref, k_ref, v_ref, b_ref, o_ref, lse_ref, m_s, l_s, acc_s):
        i, d = pl.program_id(1), pl.program_id(2)
        j = i + d - hb

        @pl.when(d == 0)
        def _():
            m_s[...] = jnp.full(m_s.shape, -jnp.inf, F32)
            l_s[...] = jnp.zeros(l_s.shape, F32)
            acc_s[...] = jnp.zeros(acc_s.shape, F32)

        @pl.when((j >= 0) & (j < nq))
        def _():
            items = [(hh, c) for hh in range(hp) for c in range(T // rc)]

            def scores(item):
                hh, c = item
                cols, rows = slice(hh * HEAD_DIM, (hh + 1) * HEAD_DIM), slice(c * rc, (c + 1) * rc)
                return (lax.dot_general(q_ref[rows, cols], k_ref[:, cols], NT, preferred_element_type=F32)
                        + b_ref[d, rows, :])

            def weighted_values(item, p, alpha):
                hh, c = item
                cols, rows = slice(hh * HEAD_DIM, (hh + 1) * HEAD_DIM), slice(c * rc, (c + 1) * rc)
                acc_s[rows, cols] = alpha * acc_s[rows, cols] + jnp.dot(p, v_ref[:, cols],
                                                                        preferred_element_type=F32)

            s_next, pending = scores(items[0]), None
            for n, (hh, c) in enumerate(items):
                rows = slice(c * rc, (c + 1) * rc)
                s = s_next
                if n + 1 < len(items):
                    s_next = scores(items[n + 1])
                if pending is not None:
                    weighted_values(*pending)
                parts = [s[:, t * LANES:(t + 1) * LANES] for t in range(T // LANES)]
                m_old = m_s[hh, rows, :]
                m_cur = jnp.max(functools.reduce(jnp.maximum, parts), axis=1, keepdims=True)
                m_new = jnp.maximum(m_old, m_cur)
                alpha = jnp.exp(m_old - m_new)
                ps = [jnp.exp(part - m_new) for part in parts]
                l_s[hh, rows, :] = alpha * l_s[hh, rows, :] + functools.reduce(jnp.add, ps)
                m_s[hh, rows, :] = m_new
                pending = ((hh, c), jnp.concatenate(ps, axis=1).astype(BF16), alpha)
            weighted_values(*pending)

        @pl.when(d == nd - 1)
        def _():
            for hh in range(hp):
                cols = slice(hh * HEAD_DIM, (hh + 1) * HEAD_DIM)
                l = jnp.sum(l_s[hh], axis=1, keepdims=True)
                o_ref[:, cols] = acc_s[:, cols] / l
                lse_ref[hh] = m_s[hh][:, :1] + jnp.log(l)

    kv = pl.BlockSpec((T, wp), lambda h, i, d: (jnp.clip(i + d - hb, 0, nq - 1), h))
    return _hosted_call(
        body, [q, k, v, bias], name=name, grid=(H // hp, nq, nd),
        in_specs=[pl.BlockSpec((T, wp), lambda h, i, d: (i, h)), kv, kv,
                  pl.BlockSpec((nd, T, T), lambda h, i, d: (0, 0, 0))],
        out_specs=[pl.BlockSpec((T, wp), lambda h, i, d: (i, h)),
                   pl.BlockSpec((hp, T, 1), lambda h, i, d: (h, i, 0))],
        out_shape=[jax.ShapeDtypeStruct((S, W), F32), jax.ShapeDtypeStruct((H, S, 1), F32)],
        scratch_shapes=[pltpu.VMEM((hp, T, LANES), F32), pltpu.VMEM((hp, T, LANES), F32),
                        pltpu.VMEM((T, wp), F32)],
        sem=("parallel", "parallel", "arbitrary"), comm=comm)


def _attn_bwd(q, k, v, do, lse, delta, bias, name, comm=None):
    S, W = q.shape
    H = W // HEAD_DIM
    nd, T, _ = bias.shape
    hb, nq = nd // 2, S // T
    scale = HEAD_DIM ** -0.5
    hp = min(H, HEADS_PER_STEP)
    rc = min(T, ATTN_ROW_CHUNK)
    wp = hp * HEAD_DIM

    def body(q_ref, do_ref, lse_ref, dl_ref, k_ref, v_ref, b_ref, dq_ref, dk_ref, dv_ref):
        j, d = pl.program_id(1), pl.program_id(2)
        i = j + d - hb

        @pl.when((j == 0) & (d == 0))
        def _():
            dq_ref[...] = jnp.zeros(dq_ref.shape, F32)

        @pl.when(d == 0)
        def _():
            dk_ref[...] = jnp.zeros(dk_ref.shape, F32)
            dv_ref[...] = jnp.zeros(dv_ref.shape, F32)

        @pl.when((i >= 0) & (i < nq))
        def _():
            items = [(hh, c) for hh in range(hp) for c in range(T // rc)]

            def slices(item):
                hh, c = item
                return slice(hh * HEAD_DIM, (hh + 1) * HEAD_DIM), slice(c * rc, (c + 1) * rc)

            def products(item):
                cols, rows = slices(item)
                s = (lax.dot_general(q_ref[rows, cols], k_ref[:, cols], NT, preferred_element_type=F32)
                     + b_ref[nd - 1 - d, rows, :])
                dp = lax.dot_general(do_ref[rows, cols], v_ref[:, cols], NT, preferred_element_type=F32)
                return s, dp

            def gradients(item, p, ds):
                cols, rows = slices(item)
                dv_ref[:, cols] += lax.dot_general(p, do_ref[rows, cols], TN, preferred_element_type=F32)
                dk_ref[:, cols] += lax.dot_general(ds, q_ref[rows, cols], TN, preferred_element_type=F32)
                q_rows = pl.ds(pl.multiple_of(i * T + item[1] * rc, rc), rc)
                dq_ref[q_rows, cols] += jnp.dot(ds, k_ref[:, cols], preferred_element_type=F32) * scale

            nxt, pending = products(items[0]), None
            for n, item in enumerate(items):
                s, dp = nxt
                if n + 1 < len(items):
                    nxt = products(items[n + 1])
                if pending is not None:
                    gradients(*pending)
                _, rows = slices(item)
                p = jnp.exp(s - lse_ref[item[0], rows, :])
                ds = p * (dp - dl_ref[item[0], rows, :])
                pending = (item, p.astype(BF16), ds.astype(BF16))
            gradients(*pending)

    qi = lambda h, j, d: (jnp.clip(j + d - hb, 0, nq - 1), h)
    qs = pl.BlockSpec((T, wp), qi)
    col = pl.BlockSpec((hp, T, 1), lambda h, j, d: (h, jnp.clip(j + d - hb, 0, nq - 1), 0))
    kv = pl.BlockSpec((T, wp), lambda h, j, d: (j, h))
    shp = jax.ShapeDtypeStruct((S, W), F32)
    return _hosted_call(
        body, [q, do, lse, delta, k, v, bias], name=name, grid=(H // hp, nq, nd),
        in_specs=[qs, qs, col, col, kv, kv, pl.BlockSpec((nd, T, T), lambda h, j, d: (0, 0, 0))],
        out_specs=[pl.BlockSpec((S, wp), lambda h, j, d: (0, h)), kv, kv],
        out_shape=[shp, shp, shp],
        sem=("parallel", "arbitrary", "arbitrary"), comm=comm)


def _halo_specs(S, tr, W, piece):
    per, last = tr // 8, S // 8 - 1
    prev = pl.BlockSpec((8, W), lambda i: (jnp.maximum(i * per - 1, 0), piece))
    nxt = pl.BlockSpec((8, W), lambda i: (jnp.minimum((i + 1) * per, last), piece))
    return prev, nxt


def _shifted(t, before, after, tr):
    rows = lax.broadcasted_iota(jnp.int32, (tr, 1), 0)
    prev = jnp.where(rows == 0, before, pltpu.roll(t, 1, 0))
    nxt = jnp.where(rows == tr - 1, after, pltpu.roll(t, tr - 1, 0))
    return prev, nxt


def _ab_mix(attn, proj, conv_w, W, name):
    S = attn.shape[0]
    tr = min(S, ROW_TILE)
    nsteps = S // tr

    def body(a_ref, za_ref, ub_ref, gb_ref, gc_ref, zb_ref, ubp, ubn, gcp, gcn, w_ref, y_ref):
        i = pl.program_id(0)
        t = gc_ref[...] * ub_ref[...]
        before = jnp.where(i == 0, 0.0, (gcp[...] * ubp[...])[7:8, :])
        after = jnp.where(i == nsteps - 1, 0.0, (gcn[...] * ubn[...])[0:1, :])
        t_prev, t_next = _shifted(t, before, after, tr)
        w = w_ref[...]
        cv = w[0:1, :] * t_prev + w[1:2, :] * t + w[2:3, :] * t_next
        silu_a, _ = _silu_and_grad(za_ref[...])
        silu_b, _ = _silu_and_grad(zb_ref[...])
        y_ref[:, :W] = (a_ref[...] * silu_a).astype(BF16)
        y_ref[:, W:] = (gb_ref[...] * cv * silu_b).astype(BF16)

    piece = lambda p: pl.BlockSpec((tr, W), lambda i: (i, p))
    ubp, ubn = _halo_specs(S, tr, W, 4)
    gcp, gcn = _halo_specs(S, tr, W, 6)
    return pl.pallas_call(
        body, name=name, grid=(nsteps,),
        in_specs=[pl.BlockSpec((tr, W), lambda i: (i, 0)), piece(3), piece(4), piece(5), piece(6), piece(7),
                  ubp, ubn, gcp, gcn, pl.BlockSpec((3, W), lambda i: (0, 0))],
        out_specs=pl.BlockSpec((tr, 2 * W), lambda i: (i, 0)),
        out_shape=jax.ShapeDtypeStruct((S, 2 * W), BF16),
        compiler_params=_cp("parallel"),
    )(attn, proj, proj, proj, proj, proj, proj, proj, proj, proj, conv_w)


def _dattn_prep(dy, proj, attn, W, name):
    S = attn.shape[0]
    H = W // HEAD_DIM
    tr = min(S, ROW_TILE)

    def body(dy_ref, za_ref, a_ref, do_ref, dl_ref):
        silu_a, _ = _silu_and_grad(za_ref[...])
        do = dy_ref[...] * silu_a
        do_ref[...] = do.astype(BF16)
        prod = do * a_ref[...]
        for hh in range(H):
            dl_ref[hh] = jnp.sum(prod[:, hh * HEAD_DIM:(hh + 1) * HEAD_DIM], axis=1, keepdims=True)

    row = pl.BlockSpec((tr, W), lambda i: (i, 0))
    return pl.pallas_call(
        body, name=name, grid=(S // tr,),
        in_specs=[row, pl.BlockSpec((tr, W), lambda i: (i, 3)), row],
        out_specs=[row, pl.BlockSpec((H, tr, 1), lambda i: (0, i, 0))],
        out_shape=[jax.ShapeDtypeStruct((S, W), BF16), jax.ShapeDtypeStruct((H, S, 1), F32)],
        compiler_params=_cp("parallel"),
    )(dy, proj, attn)


def _ab_bwd(dy, attn, proj, dqr, dkr, dv, cosf, sinf, conv_w, W, name):
    S = attn.shape[0]
    tr = min(S, ROW_TILE // 2)
    nsteps = S // tr
    heads = W // HEAD_DIM

    def body(dya_ref, dyb_ref, a_ref, za_ref, ub_ref, gb_ref, gc_ref, zb_ref, dq_ref, dk_ref, dv_ref,
             c_ref, s_ref, w_ref, dybp, dybn, gbp, gbn, zbp, zbn, ubp, ubn, gcp, gcn,
             dp_ref, dw_ref, acc):
        i = pl.program_id(0)
        first, last = i == 0, i == nsteps - 1
        w = w_ref[...]
        w0, w1, w2 = w[0:1, :], w[1:2, :], w[2:3, :]
        ub, gb, gc, zb = ub_ref[...], gb_ref[...], gc_ref[...], zb_ref[...]
        dyb = dyb_ref[...]
        silu_a, dsilu_a = _silu_and_grad(za_ref[...])
        silu_b, dsilu_b = _silu_and_grad(zb)
        t = gc * ub
        t_prev, t_next = _shifted(t, jnp.where(first, 0.0, (gcp[...] * ubp[...])[7:8, :]),
                                  jnp.where(last, 0.0, (gcn[...] * ubn[...])[0:1, :]), tr)
        cv = w0 * t_prev + w1 * t + w2 * t_next
        dcv = dyb * gb * silu_b
        halo_p = dybp[...] * gbp[...] * _silu_and_grad(zbp[...])[0]
        halo_n = dybn[...] * gbn[...] * _silu_and_grad(zbn[...])[0]
        dcv_prev, dcv_next = _shifted(dcv, jnp.where(first, 0.0, halo_p[7:8, :]),
                                      jnp.where(last, 0.0, halo_n[0:1, :]), tr)
        dt = w0 * dcv_next + w1 * dcv + w2 * dcv_prev
        cosf_v, sinf_v = c_ref[...], s_ref[...]
        for src, base in ((dq_ref, 0), (dk_ref, W)):
            for hh, val in enumerate(_rope_apply(src[...], cosf_v, sinf_v, heads, -1.0)):
                dp_ref[:, base + hh * HEAD_DIM:base + (hh + 1) * HEAD_DIM] = val.astype(BF16)
        dp_ref[:, 2 * W:3 * W] = dv_ref[...].astype(BF16)
        dp_ref[:, 3 * W:4 * W] = (dya_ref[...] * a_ref[...] * dsilu_a).astype(BF16)
        dp_ref[:, 4 * W:5 * W] = (dt * gc).astype(BF16)
        dp_ref[:, 5 * W:6 * W] = (dyb * cv * silu_b).astype(BF16)
        dp_ref[:, 6 * W:7 * W] = (dt * ub).astype(BF16)
        dp_ref[:, 7 * W:8 * W] = (dyb * gb * cv * dsilu_b).astype(BF16)
        tap = lax.broadcasted_iota(jnp.int32, (8, 1), 0)
        part = (jnp.where(tap == 0, jnp.sum(dcv * t_prev, axis=0, keepdims=True), 0.0)
                + jnp.where(tap == 1, jnp.sum(dcv * t, axis=0, keepdims=True), 0.0)
                + jnp.where(tap == 2, jnp.sum(dcv * t_next, axis=0, keepdims=True), 0.0))

        @pl.when(first)
        def _():
            acc[...] = part

        @pl.when(i > 0)
        def _():
            acc[...] += part

        @pl.when(last)
        def _():
            dw_ref[...] = acc[...]

    row = pl.BlockSpec((tr, W), lambda i: (i, 0))
    piece = lambda p: pl.BlockSpec((tr, W), lambda i: (i, p))
    tab = pl.BlockSpec((tr, HEAD_DIM), lambda i: (i, 0))
    dybp, dybn = _halo_specs(S, tr, W, 1)
    gbp, gbn = _halo_specs(S, tr, W, 5)
    zbp, zbn = _halo_specs(S, tr, W, 7)
    ubp, ubn = _halo_specs(S, tr, W, 4)
    gcp, gcn = _halo_specs(S, tr, W, 6)
    return pl.pallas_call(
        body, name=name, grid=(nsteps,),
        in_specs=[piece(0), piece(1), row, piece(3), piece(4), piece(5), piece(6), piece(7), row, row, row,
                  tab, tab, pl.BlockSpec((3, W), lambda i: (0, 0)),
                  dybp, dybn, gbp, gbn, zbp, zbn, ubp, ubn, gcp, gcn],
        out_specs=[pl.BlockSpec((tr, 8 * W), lambda i: (i, 0)), pl.BlockSpec((8, W), lambda i: (0, 0))],
        out_shape=[jax.ShapeDtypeStruct((S, 8 * W), BF16), jax.ShapeDtypeStruct((8, W), F32)],
        scratch_shapes=[pltpu.VMEM((8, W), F32)],
        compiler_params=_cp("arbitrary"),
    )(dy, dy, attn, proj, proj, proj, proj, proj, dqr, dkr, dv, cosf, sinf, conv_w,
      dy, dy, proj, proj, proj, proj, proj, proj, proj, proj)


def _sgu_centre(p_ref, vc_s, dvg_s, Dc):
    gw = Dc // C_GROUPS
    total = None
    for g in range(C_GROUPS):
        cs = slice(g * gw, (g + 1) * gw)
        vg, dvg = _gelu_and_grad(p_ref[:, Dc + g * gw:Dc + (g + 1) * gw])
        vc_s[:, cs] = vg
        if dvg_s is not None:
            dvg_s[:, cs] = dvg
        total = vg if total is None else total + vg
    mu = jnp.sum(total, axis=1, keepdims=True) * (1.0 / Dc)
    total = None
    for g in range(C_GROUPS):
        cs = slice(g * gw, (g + 1) * gw)
        vc = vc_s[:, cs] - mu
        vc_s[:, cs] = vc
        total = vc * vc if total is None else total + vc * vc
    return lax.rsqrt(jnp.sum(total, axis=1, keepdims=True) * (1.0 / Dc) + EPS)


def _sgu_fwd(proj, ln_g, ln_b, w_s, b_st, name, comm=None):
    S, Dc3 = proj.shape
    Dc = Dc3 // 3
    gw = Dc // C_GROUPS
    vec = pl.BlockSpec((1, Dc), lambda i: (0, 0))

    def body(p_ref, lng_ref, lnb_ref, ws_ref, bst_ref, y_ref, vc_s):
        rstd = _sgu_centre(p_ref, vc_s, None, Dc)
        bst = bst_ref[...]
        for g in range(C_GROUPS):
            cs = slice(g * gw, (g + 1) * gw)
            vn = (vc_s[:, cs] * rstd * lng_ref[:, cs] + lnb_ref[:, cs]).astype(BF16)
            mixed = jnp.dot(ws_ref[g].astype(BF16), vn, preferred_element_type=F32) + bst[:, g:g + 1]
            u, _ = _gelu_and_grad(p_ref[:, cs])
            sz, _ = _silu_and_grad(p_ref[:, 2 * Dc + g * gw:2 * Dc + (g + 1) * gw])
            y_ref[:, cs] = (u * mixed * sz).astype(BF16)

    return _hosted_call(
        body, [proj, ln_g, ln_b, w_s, b_st], name=name, grid=(S // C_CHUNK,),
        in_specs=[pl.BlockSpec((C_CHUNK, Dc3), lambda i: (i, 0)), vec, vec,
                  pl.BlockSpec((C_GROUPS, C_CHUNK, C_CHUNK), lambda i: (0, 0, 0)),
                  pl.BlockSpec((C_CHUNK, C_GROUPS), lambda i: (0, 0))],
        out_specs=pl.BlockSpec((C_CHUNK, Dc), lambda i: (i, 0)),
        out_shape=jax.ShapeDtypeStruct((S, Dc), BF16),
        scratch_shapes=[pltpu.VMEM((C_CHUNK, Dc), F32)],
        sem=("parallel",), comm=comm)


def _sgu_bwd(proj, dy, ln_g, ln_b, w_s, w_st, b_st, name):
    S, Dc3 = proj.shape
    Dc = Dc3 // 3
    gw = Dc // C_GROUPS
    nsteps = S // C_CHUNK
    vec = pl.BlockSpec((1, Dc), lambda i: (0, 0))
    wspec = pl.BlockSpec((C_GROUPS, C_CHUNK, C_CHUNK), lambda i: (0, 0, 0))

    def body(p_ref, dy_ref, lng_ref, lnb_ref, ws_ref, wst_ref, bst_ref,
             dp_ref, dws_ref, dbs_ref, dlg_ref, dlb_ref, acc_w, acc_b, acc_g, acc_lb, vc_s, dvg_s, dvh_s):
        i = pl.program_id(0)

        @pl.when(i == 0)
        def _():
            acc_w[...] = jnp.zeros(acc_w.shape, F32)
            acc_b[...] = jnp.zeros(acc_b.shape, F32)
            acc_g[...] = jnp.zeros(acc_g.shape, F32)
            acc_lb[...] = jnp.zeros(acc_lb.shape, F32)

        rstd = _sgu_centre(p_ref, vc_s, dvg_s, Dc)
        bst = bst_ref[...]
        octets = lambda t: jnp.sum(t.reshape(C_CHUNK // 8, 8, gw), axis=0)
        t1, t2 = None, None
        for g in range(C_GROUPS):
            cs = slice(g * gw, (g + 1) * gw)
            zs = slice(2 * Dc + g * gw, 2 * Dc + (g + 1) * gw)
            vhat = vc_s[:, cs] * rstd
            vn = (vhat * lng_ref[:, cs] + lnb_ref[:, cs]).astype(BF16)
            mixed = jnp.dot(ws_ref[g].astype(BF16), vn, preferred_element_type=F32) + bst[:, g:g + 1]
            u, du = _gelu_and_grad(p_ref[:, cs])
            sz, dsz = _silu_and_grad(p_ref[:, zs])
            dy = dy_ref[:, cs]
            dmixed = dy * u * sz
            dmb = dmixed.astype(BF16)
            acc_w[g] += lax.dot_general(dmb, vn, NT, preferred_element_type=F32)
            acc_b[g] += dmixed
            dvn = jnp.dot(wst_ref[g].astype(BF16), dmb, preferred_element_type=F32)
            acc_g[:, cs] += octets(dvn * vhat)
            acc_lb[:, cs] += octets(dvn)
            dvh = dvn * lng_ref[:, cs]
            dvh_s[:, cs] = dvh
            t1 = dvh if t1 is None else t1 + dvh
            t2 = dvh * vhat if t2 is None else t2 + dvh * vhat
            dp_ref[:, cs] = (dy * mixed * sz * du).astype(BF16)
            dp_ref[:, zs] = (dy * u * mixed * dsz).astype(BF16)
        m1 = jnp.sum(t1, axis=1, keepdims=True) * (1.0 / Dc)
        m2 = jnp.sum(t2, axis=1, keepdims=True) * (1.0 / Dc)
        for g in range(C_GROUPS):
            cs = slice(g * gw, (g + 1) * gw)
            dvgelu = rstd * (dvh_s[:, cs] - m1 - (vc_s[:, cs] * rstd) * m2)
            dp_ref[:, Dc + g * gw:Dc + (g + 1) * gw] = (dvgelu * dvg_s[:, cs]).astype(BF16)

        @pl.when(i == nsteps - 1)
        def _():
            dws_ref[...] = acc_w[...]
            for g in range(C_GROUPS):
                dbs_ref[g] = jnp.sum(acc_b[g], axis=1, keepdims=True)
            dlg_ref[...] = jnp.sum(acc_g[...], axis=0, keepdims=True)
            dlb_ref[...] = jnp.sum(acc_lb[...], axis=0, keepdims=True)

    v = jax.ShapeDtypeStruct((1, Dc), F32)
    return pl.pallas_call(
        body, name=name, grid=(nsteps,),
        in_specs=[pl.BlockSpec((C_CHUNK, Dc3), lambda i: (i, 0)), pl.BlockSpec((C_CHUNK, Dc), lambda i: (i, 0)),
                  vec, vec, wspec, wspec, pl.BlockSpec((C_CHUNK, C_GROUPS), lambda i: (0, 0))],
        out_specs=[pl.BlockSpec((C_CHUNK, Dc3), lambda i: (i, 0)), wspec,
                   pl.BlockSpec((C_GROUPS, C_CHUNK, 1), lambda i: (0, 0, 0)), vec, vec],
        out_shape=[jax.ShapeDtypeStruct((S, Dc3), BF16),
                   jax.ShapeDtypeStruct((C_GROUPS, C_CHUNK, C_CHUNK), F32),
                   jax.ShapeDtypeStruct((C_GROUPS, C_CHUNK, 1), F32), v, v],
        scratch_shapes=[pltpu.VMEM((C_GROUPS, C_CHUNK, C_CHUNK), F32), pltpu.VMEM((C_GROUPS, C_CHUNK, gw), F32),
                        pltpu.VMEM((8, Dc), F32), pltpu.VMEM((8, Dc), F32)]
                       + [pltpu.VMEM((C_CHUNK, Dc), F32)] * 3,
        compiler_params=_cp("arbitrary"),
    )(proj, dy, ln_g, ln_b, w_s, w_st, b_st)


PACK_COLS = 1024
PACK_ROWS = 64


def _pack(vectors, rows=PACK_ROWS):
    flat = jnp.concatenate([v.reshape(-1) for v in vectors])
    pad = (-flat.shape[0]) % (PACK_COLS * rows)
    return jnp.pad(flat, (0, pad)).reshape(-1, PACK_COLS)


def _unshard(g, off, shape):
    L, rest = shape[0], shape[1:]
    size = int(np.prod(shape))
    piece = g[:, off:off + size].reshape((NDEV,) + tuple(shape))
    nd = piece.ndim
    perm = tuple(range(1, nd - 1)) + (0, nd - 1)
    full = jnp.transpose(piece, perm)
    return full.reshape(tuple(shape[:-1]) + (NDEV * shape[-1],)), off + size


def kernel(x, c, ab_norm_g, ab_w_mod, ab_b_mod, ab_w_in, ab_conv_w, ab_w_out, sg_norm_g, sg_w_mod, sg_b_mod, sg_w_in, sg_ln_g, sg_ln_b, sg_w_s, sg_b_s, sg_w_out, final_norm_g, loss_target, m_ab_norm_g, m_ab_w_mod, m_ab_b_mod, m_ab_w_in, m_ab_conv_w, m_ab_w_out, m_sg_norm_g, m_sg_w_mod, m_sg_b_mod, m_sg_w_in, m_sg_ln_g, m_sg_ln_b, m_sg_w_s, m_sg_b_s, m_sg_w_out, m_final_norm_g, v_ab_norm_g, v_ab_w_mod, v_ab_b_mod, v_ab_w_in, v_ab_conv_w, v_ab_w_out, v_sg_norm_g, v_sg_w_mod, v_sg_b_mod, v_sg_w_in, v_sg_ln_g, v_sg_ln_b, v_sg_w_s, v_sg_b_s, v_sg_w_out, v_final_norm_g):
    _, S, D = x.shape
    L = ab_norm_g.shape[0]
    W = ab_conv_w.shape[2] * NDEV
    n_ab, n_sg = ab_w_in.shape[2], sg_w_in.shape[2]
    n_mod = ab_w_mod.shape[2]
    kb = ab_w_out.shape[1]
    xi, yi, ci = _position()
    dev = 4 * xi + 2 * yi + ci
    x2, tgt = x.reshape(S, D), loss_target.reshape(S, D)

    small = [c, ab_conv_w, sg_norm_g, sg_ln_g, sg_ln_b]
    (g1,) = _comm_only(_GatherDirect([_pack(small, 8)]), "ag_small")
    g1 = g1.reshape(NDEV, -1)
    c_all = g1[:, :D]
    off = D
    conv_full, off = _unshard(g1, off, ab_conv_w.shape)
    sg_norm_full, off = _unshard(g1, off, sg_norm_g.shape)
    ln_g_full, off = _unshard(g1, off, sg_ln_g.shape)
    ln_b_full, off = _unshard(g1, off, sg_ln_b.shape)

    ab_b_cols = lax.dynamic_slice_in_dim(ab_b_mod, dev * n_mod, n_mod, axis=1)
    m_ab = _mod_fwd(c_all, ab_w_mod, ab_b_cols.reshape(L, 1, n_mod), "mod_fwd_ab")
    m_sg = _mod_fwd(c_all, sg_w_mod, sg_b_mod.reshape(L, 1, n_mod), "mod_fwd_sg")
    m_part = jnp.stack([m_ab, m_sg]).transpose(2, 0, 1, 3).reshape(NDEV, 2 * L * n_mod)
    (g2,) = _comm_only(_GatherDirect([m_part]), "ag_mod")
    mine = lax.dynamic_index_in_dim(g2, dev, axis=1, keepdims=False)
    mods = mine.reshape(NDEV, 2, L, n_mod).transpose(1, 2, 0, 3).reshape(2, L, 3 * D)

    def mod_of(kind, i):
        m = mods[kind, i]
        return m[:D].reshape(1, D), m[D:2 * D].reshape(1, D), m[2 * D:].reshape(1, D)

    big_w = [[(ab_w_in, m_ab_w_in, v_ab_w_in), (ab_w_out, m_ab_w_out, v_ab_w_out)],
             [(sg_w_in, m_sg_w_in, v_sg_w_in), (sg_w_out, m_sg_w_out, v_sg_w_out)]]
    big_names = [["ab_w_in", "ab_w_out"], ["sg_w_in", "sg_w_out"]]
    n_layers = 2 * L
    shards = [[big_w[layer % 2][k][0][layer // 2].astype(BF16) for k in range(2)] for layer in range(n_layers)]
    gathered = {}

    def gather_of(keys):
        keys = [key for key in keys if key[0] < n_layers and key not in gathered]
        return keys, (_Gather([shards[layer][k] for layer, k in keys]) if keys else None)

    def keep_gathered(keys, res):
        for (layer, k), g in zip(keys, res):
            gathered[(layer, k)] = g.reshape((NDEV, 1, D, g.shape[-1]) if k == 0 else (NDEV, 1, kb, D))

    keys, comm = gather_of([(0, 0)])
    keep_gathered(keys, _comm_only(comm, "ag_w_in_layer0"))

    cosf, sinf = _rope_tables(S)
    T = min(S, ATTN_TILE)
    bias = _attn_bias(T)
    norm_g = [ab_norm_g, sg_norm_full]
    w_s_t = jnp.swapaxes(sg_w_s, -1, -2)
    b_s_t = jnp.swapaxes(sg_b_s, -1, -2)

    saved = []
    x_cur, res, gate_prev = x2, None, None
    for layer in range(2 * L):
        kind, i = layer % 2, layer // 2
        tag = f"{'ab' if kind == 0 else 'sg'}{i}"
        shift, scale, gate = mod_of(kind, i)
        g = norm_g[kind][i].reshape(1, D)
        xl, h = _pre(x_cur, res, gate_prev, g, scale, shift, f"pre_{tag}")
        keys, comm = gather_of([(layer + 1, 0)])
        if comm is None:
            proj = _mm_nn_in(h, gathered[(layer, 0)], 0, f"proj_{tag}")
        else:
            proj, got = _mm_nn_in(h, gathered[(layer, 0)], 0, f"proj_{tag}", comm)
            keep_gathered(keys, got)
        rec = dict(xl=xl, h=h, proj=proj, g=g, scale=scale, gate=gate)
        keys, comm = gather_of([(layer, 1), (layer + 1, 1), (layer + 2, 1)] if kind == 0 else [])
        if kind == 0:
            qr, kr, vb = _rope_qkv(proj, cosf, sinf, W, f"rope_{tag}")
            (attn, lse), got = _attn_fwd(qr, kr, vb, bias, f"attn_{tag}", comm)
            y = _ab_mix(attn, proj, conv_full[i], W, f"mix_{tag}")
            rec.update(qr=qr, kr=kr, vb=vb, attn=attn, lse=lse)
        else:
            y, got = _sgu_fwd(proj, ln_g_full[i].reshape(1, D), ln_b_full[i].reshape(1, D), sg_w_s[i], b_s_t[i],
                              f"sgu_{tag}", comm)
        keep_gathered(keys, got)
        out = _mm_nn_out(y, gathered[(layer, 1)], 0, f"out_{tag}")
        rec.update(y=y, out=out)
        saved.append(rec)
        x_cur, res, gate_prev = xl, out, gate

    dx, loss_part, d_final_g = _loss_head(x_cur, res, gate_prev, final_norm_g.reshape(1, D), tgt, "loss_head")
    loss = lax.psum(loss_part[0, 0], ("x", "y", "c"))

    c_idx = ci.reshape(1).astype(jnp.int32)
    big_res = {}
    pending = None

    def sum_and_update(done, k, from_chips, part=(0, 1)):
        nm = big_names[done % 2][k]
        w, m, v = big_w[done % 2][k]
        flat = lambda a: a.reshape(L * a.shape[1], a.shape[2])
        row0 = (done // 2) * w.shape[1] + part[0] * (w.shape[1] // part[1])
        big_res[nm] = _sum_adam(from_chips, flat(w), flat(m), flat(v), row0, big_res.get(nm),
                                f"adam_{nm}{done // 2}_{part[0]}")

    def finish_layer(done, from_chips):
        for k in range(2):
            sum_and_update(done, k, from_chips[k])

    dm = [[None] * L, [None] * L]
    d_norm = [[None] * L, [None] * L]
    d_conv, d_lng, d_lnb, d_ws, d_bs = [None] * L, [None] * L, [None] * L, [None] * L, [None] * L
    for layer in reversed(range(2 * L)):
        kind, i = layer % 2, layer // 2
        tag = f"{'ab' if kind == 0 else 'sg'}{i}"
        rec = saved[layer]
        w_in_l, w_out_l = gathered[(layer, 0)], gathered[(layer, 1)]
        dout, dgate = _post_bwd(dx, rec["out"], rec["gate"], f"post_bwd_{tag}")
        dy = _mm_nt_out(dout, w_out_l, 0, f"dy_{tag}")
        dwo = _mm_tn_out(rec["y"], dout, 0, 1, None, f"dwout_{tag}")
        if kind == 0:
            do, delta = _dattn_prep(dy, rec["proj"], rec["attn"], W, f"dattn_{tag}")
            comm = _ToChips(pending[1]) if pending else None
            (dqr, dkr, dvv), got = _attn_bwd(rec["qr"], rec["kr"], rec["vb"], do, rec["lse"], delta, bias,
                                             f"attn_bwd_{tag}", comm)
            if pending:
                finish_layer(pending[0], got)
            dproj, dcw = _ab_bwd(dy, rec["attn"], rec["proj"], dqr, dkr, dvv, cosf, sinf, conv_full[i], W,
                                 f"mix_bwd_{tag}")
            d_conv[i] = dcw[:3]
            earlier = None
        else:
            dproj, d_ws[i], dbs, d_lng[i], d_lnb[i] = _sgu_bwd(
                rec["proj"], dy, ln_g_full[i].reshape(1, D), ln_b_full[i].reshape(1, D),
                sg_w_s[i], w_s_t[i], b_s_t[i], f"sgu_bwd_{tag}")
            d_bs[i] = dbs.reshape(C_GROUPS, C_CHUNK)
            earlier = pending
        if layer == 0:
            h_l = rec["h"]
            dwi_a = _mm_tn_in(h_l, dproj, 0, 1, None, f"dwin_a_{tag}", part=(0, 2))
            g_a = [dwi_a.reshape(NDEV, D // 2, -1), dwo.reshape(NDEV, kb, D)]
            dwi_b, sib_a = _mm_tn_in(h_l, dproj, 0, 1, None, f"dwin_b_{tag}", _ToSibling(g_a), part=(1, 2))
            p_a = [_add_sibling(g, r, c_idx, f"rs_add_a{k}_{tag}") for k, (g, r) in enumerate(zip(g_a, sib_a))]
            g_b = [dwi_b.reshape(NDEV, D // 2, -1)]
            both = _Both(_ToSibling(g_b), _ToChips(p_a))
            dh, got = _mm_nt_in(dproj, w_in_l, 0, f"dh_{tag}", both)
            sib_b, chips_a = both.split_results(got)
            p_b = [_add_sibling(g_b[0], sib_b[0], c_idx, f"rs_add_b_{tag}")]
            (dx, dshift, dscale, d_norm[kind][i]), chips_b = _pre_bwd(
                rec["xl"], dh, dx, rec["g"], rec["scale"], f"pre_bwd_{tag}", _ToChips(p_b))
            sum_and_update(0, 0, chips_a[0], (0, 2))
            sum_and_update(0, 1, chips_a[1])
            sum_and_update(0, 0, chips_b[0], (1, 2))
            dm[kind][i] = jnp.concatenate([dshift, dscale, dgate], axis=1).reshape(3 * D)
            continue
        if earlier:
            dwi, got_in = _mm_tn_in(rec["h"], dproj, 0, 1, None, f"dwin_{tag}", _ToChips(earlier[1][:1]))
        else:
            dwi = _mm_tn_in(rec["h"], dproj, 0, 1, None, f"dwin_{tag}")
        grads = [dwi.reshape(NDEV, D, -1), dwo.reshape(NDEV, kb, D)]
        if earlier:
            both = _Both(_ToSibling(grads), _ToChips(earlier[1][1:]))
            dh, got = _mm_nt_in(dproj, w_in_l, 0, f"dh_{tag}", both)
            from_sibling, got_rest = both.split_results(got)
            sum_and_update(earlier[0], 0, got_in[0], (0, 2))
            sum_and_update(earlier[0], 0, got_rest[0], (1, 2))
            sum_and_update(earlier[0], 1, got_rest[1])
        else:
            dh, from_sibling = _mm_nt_in(dproj, w_in_l, 0, f"dh_{tag}", _ToSibling(grads))
        name_in, name_out = (f"rs_add_{nm}{i}" for nm in big_names[kind])
        if kind == 0:
            parts = [_add_sibling(grads[0], from_sibling[0], c_idx, f"{name_in}_{p}", (p, 2)) for p in range(2)]
        else:
            parts = [_add_sibling(grads[0], from_sibling[0], c_idx, name_in)]
        pending = (layer, parts + [_add_sibling(grads[1], from_sibling[1], c_idx, name_out)])
        (dx, dshift, dscale, d_norm[kind][i]), _ = _pre_bwd(
            rec["xl"], dh, dx, rec["g"], rec["scale"], f"pre_bwd_{tag}")
        dm[kind][i] = jnp.concatenate([dshift, dscale, dgate], axis=1).reshape(3 * D)
    grad_x = dx.reshape(1, S, D)
    for kind in range(2):
        for k in range(2):
            nm = big_names[kind][k]
            big_res[nm] = [o.reshape(big_w[kind][k][0].shape) for o in big_res[nm]]

    stack = lambda xs: jnp.stack(xs)
    pack_items = [stack(dm[0]), stack(dm[1]), stack(d_norm[0]).reshape(L, D), stack(d_conv),
                  stack(d_norm[1]).reshape(L, D), stack(d_lng).reshape(L, D), stack(d_lnb).reshape(L, D),
                  stack(d_ws), stack(d_bs), d_final_g]
    (g3,) = _comm_only(_Gather([_pack(pack_items)]), "ag_grads")
    P = g3.shape[1] * g3.shape[2]
    tot = _sum_rows(g3, "sum_small").reshape(P)
    g3 = g3.reshape(NDEV, P)
    sizes = [int(np.prod(p.shape)) for p in pack_items]
    offs = np.concatenate([[0], np.cumsum(sizes)]).tolist()
    seg = lambda k, shape: tot[offs[k]:offs[k + 1]].reshape(shape)

    def shard(full, n):
        return lax.dynamic_slice_in_dim(full, dev * n, n, axis=full.ndim - 1)

    g_ab_b_mod = seg(0, (L, 3 * D))
    g_sg_b_mod = shard(seg(1, (L, 3 * D)), n_mod)
    g_ab_norm = seg(2, (L, D))
    g_conv = shard(seg(3, (L, 3, W)), W // NDEV)
    g_sg_norm = shard(seg(4, (L, D)), kb)
    g_ln_g = shard(seg(5, (L, D)), kb)
    g_ln_b = shard(seg(6, (L, D)), kb)
    g_w_s = seg(7, sg_w_s.shape)
    g_b_s = seg(8, sg_b_s.shape)
    g_final = seg(9, (D,))

    small_w = [("ab_norm_g", g_ab_norm, ab_norm_g, m_ab_norm_g, v_ab_norm_g),
               ("ab_b_mod", g_ab_b_mod, ab_b_mod, m_ab_b_mod, v_ab_b_mod),
               ("ab_conv_w", g_conv, ab_conv_w, m_ab_conv_w, v_ab_conv_w),
               ("sg_norm_g", g_sg_norm, sg_norm_g, m_sg_norm_g, v_sg_norm_g),
               ("sg_b_mod", g_sg_b_mod, sg_b_mod, m_sg_b_mod, v_sg_b_mod),
               ("sg_ln_g", g_ln_g, sg_ln_g, m_sg_ln_g, v_sg_ln_g),
               ("sg_ln_b", g_ln_b, sg_ln_b, m_sg_ln_b, v_sg_ln_b),
               ("sg_w_s", g_w_s, sg_w_s, m_sg_w_s, v_sg_w_s),
               ("sg_b_s", g_b_s, sg_b_s, m_sg_b_s, v_sg_b_s),
               ("final_norm_g", g_final, final_norm_g, m_final_norm_g, v_final_norm_g)]
    packed = [_pack([t[k] for t in small_w]) for k in (1, 2, 3, 4)]
    upd = _adam_only(*packed, "adam_small")
    small_res = {}
    o = 0
    for nm, g, w, _, _ in small_w:
        size = int(np.prod(w.shape))
        small_res[nm] = [g] + [u.reshape(-1)[o:o + size].reshape(w.shape) for u in upd]
        o += size

    KP = 128
    sc_t = jnp.pad((c_all * jax.nn.sigmoid(c_all)).T, ((0, 0), (0, KP - NDEV)))
    mod_res = {}
    for kind, nm, (w, m, v) in ((0, "ab_w_mod", (ab_w_mod, m_ab_w_mod, v_ab_w_mod)),
                                (1, "sg_w_mod", (sg_w_mod, m_sg_w_mod, v_sg_w_mod))):
        dm_all = g3[:, offs[kind]:offs[kind + 1]].reshape(NDEV, L, 3 * D)
        cols = jnp.pad(shard(dm_all, n_mod).transpose(1, 0, 2), ((0, 0), (0, KP - NDEV), (0, 0)))
        mod_res[nm] = _wmod_grad_adam(sc_t, cols, w, m, v, f"adam_{nm}")

    order = ["ab_norm_g", "ab_w_mod", "ab_b_mod", "ab_w_in", "ab_conv_w", "ab_w_out", "sg_norm_g", "sg_w_mod",
             "sg_b_mod", "sg_w_in", "sg_ln_g", "sg_ln_b", "sg_w_s", "sg_b_s", "sg_w_out", "final_norm_g"]
    res = {**big_res, **small_res, **mod_res}
    outs = [loss, grad_x]
    for k in range(4):
        outs += [res[nm][k] for nm in order]
    return tuple(outs)
```

```python
import functools
import math

import numpy as np
import jax
import jax.numpy as jnp
from jax import lax
from jax.experimental import pallas as pl
from jax.experimental.pallas import tpu as pltpu

F32 = jnp.float32
BF16 = jnp.bfloat16

NDEV = 8
NCHIP = 4
EPS = 1e-6
HEAD_DIM = 128
ROPE_THETA = 10000.0
DILATED_PATTERNS = ((128, 1), (512, 4), (2048, 16))
NEG_INF = -1e30
C_CHUNK = 128
C_GROUPS = 8
ADAM_LR = 0.001
ADAM_B1 = 0.9
ADAM_B2 = 0.999
ADAM_EPS = 1e-08
ADAM_WD = 0.01
ADAM_STEP = 10
GELU_K = math.sqrt(2.0 / math.pi)
GELU_C = 0.044715

VMEM_LIMIT_BYTES = 56 * 1024 * 1024
ATTN_TILE = 512
HEADS_PER_STEP = 4
ATTN_ROW_CHUNK = 256
LANES = 128
ROW_TILE = 256
MESH = pl.DeviceIdType.MESH
ANY = pl.BlockSpec(memory_space=pl.ANY)


def _cp(*sem):
    return pltpu.CompilerParams(dimension_semantics=sem, vmem_limit_bytes=VMEM_LIMIT_BYTES)


def _sigmoid(z):
    return 0.5 * (jnp.tanh(0.5 * z) + 1.0)


def _silu_and_grad(z):
    s = _sigmoid(z)
    return z * s, s * (1.0 + z * (1.0 - s))


def _gelu_and_grad(x):
    x2 = x * x
    t = jnp.tanh(GELU_K * (x + GELU_C * x2 * x))
    g = 0.5 * x * (1.0 + t)
    dg = 0.5 * (1.0 + t) + 0.5 * x * (1.0 - t * t) * (GELU_K * (1.0 + 3.0 * GELU_C * x2))
    return g, dg


def _position():
    return lax.axis_index("x"), lax.axis_index("y"), lax.axis_index("c")


def _chips(x, y):
    return [(1 - x, y), (x, 1 - y), (1 - x, 1 - y)]


class _Gather:
    def __init__(self, arrs):
        n = len(arrs)
        self.arrs = list(arrs)
        self.out_shape = [jax.ShapeDtypeStruct((NDEV,) + a.shape, a.dtype) for a in arrs]
        self.scratch = [pltpu.SemaphoreType.DMA((n, 7)), pltpu.SemaphoreType.DMA((n, 7)),
                        pltpu.SemaphoreType.DMA((n,))]

    def _copies(self, ins, outs, sems, own=True):
        send_sems, recv_sems, local_sems = sems
        x, y, c = _position()

        def copy(a, k, block, to, src=None):
            dst = outs[a].at[4 * block[0] + 2 * block[1] + block[2]]
            return pltpu.make_async_remote_copy(
                src_ref=dst if src is None else src, dst_ref=dst,
                send_sem=send_sems.at[a, k], recv_sem=recv_sems.at[a, k],
                device_id=to, device_id_type=MESH)

        n = len(ins)
        me, sibling = (x, y, c), (x, y, 1 - c)
        mine, first = [], []
        if own:
            mine = [pltpu.make_async_copy(ins[a], outs[a].at[4 * x + 2 * y + c], local_sems.at[a])
                    for a in range(n)]
            for a in range(n):
                first.append(copy(a, 0, me, sibling, src=ins[a]))
                first += [copy(a, 1 + j, me, (*chip, c), src=ins[a]) for j, chip in enumerate(_chips(x, y))]
        return copy, mine, first

    def start(self, ins, outs, sems):
        _, mine, first = self._copies(ins, outs, sems)
        for cp in mine + first:
            cp.start()

    def middle(self, ins, outs, sems):
        copy = self._copies(ins, outs, sems, own=False)[0]
        x, y, c = _position()
        me, sibling = (x, y, c), (x, y, 1 - c)
        for j, chip in enumerate(_chips(x, y)):
            for a in range(len(ins)):
                copy(a, 1 + j, (*chip, c), me).wait_recv()
                copy(a, 4 + j, (*chip, c), sibling).start()

    def finish(self, ins, outs, sems):
        copy, mine, first = self._copies(ins, outs, sems)
        x, y, c = _position()
        me, sibling = (x, y, c), (x, y, 1 - c)
        passed = [copy(a, 4 + j, (*chip, c), sibling)
                  for j, chip in enumerate(_chips(x, y)) for a in range(len(ins))]
        for a in range(len(ins)):
            copy(a, 0, sibling, me).wait_recv()
            for j, chip in enumerate(_chips(x, y)):
                copy(a, 4 + j, (*chip, 1 - c), me).wait_recv()
        for cp in first + passed:
            cp.wait_send()
        for cp in mine:
            cp.wait()


class _GatherDirect:
    def __init__(self, arrs):
        n = len(arrs)
        self.arrs = list(arrs)
        self.out_shape = [jax.ShapeDtypeStruct((NDEV,) + a.shape, a.dtype) for a in arrs]
        self.scratch = [pltpu.SemaphoreType.DMA((n, 7)), pltpu.SemaphoreType.DMA((n, 7)),
                        pltpu.SemaphoreType.DMA((n,))]

    def _copies(self, ins, outs, sems, arrivals):
        send_sems, recv_sems, local_sems = sems
        x, y, c = _position()
        mine = [pltpu.make_async_copy(ins[a], outs[a].at[4 * x + 2 * y + c], local_sems.at[a])
                for a in range(len(ins))]
        sends, recvs = [], []
        for a in range(len(ins)):
            for k in range(1, NDEV):
                px = 1 - x if k & 4 else x
                py = 1 - y if k & 2 else y
                pc = 1 - c if k & 1 else c
                sends.append(pltpu.make_async_remote_copy(
                    src_ref=ins[a], dst_ref=outs[a].at[4 * x + 2 * y + c],
                    send_sem=send_sems.at[a, k - 1], recv_sem=recv_sems.at[a, k - 1],
                    device_id=(px, py, pc), device_id_type=MESH))
                if arrivals:
                    slot = outs[a].at[4 * px + 2 * py + pc]
                    recvs.append(pltpu.make_async_remote_copy(
                        src_ref=slot, dst_ref=slot, send_sem=send_sems.at[a, k - 1], recv_sem=recv_sems.at[a, k - 1],
                        device_id=(px, py, pc), device_id_type=MESH))
        return mine, sends, recvs

    def start(self, ins, outs, sems):
        mine, sends, _ = self._copies(ins, outs, sems, False)
        for cp in mine + sends:
            cp.start()

    def finish(self, ins, outs, sems):
        mine, sends, recvs = self._copies(ins, outs, sems, True)
        for cp in recvs:
            cp.wait_recv()
        for cp in sends:
            cp.wait_send()
        for cp in mine:
            cp.wait()


class _ToSibling:
    def __init__(self, gs):
        n = len(gs)
        self.arrs = list(gs)
        self.out_shape = [jax.ShapeDtypeStruct((NCHIP,) + g.shape[1:], g.dtype) for g in gs]
        self.scratch = [pltpu.SemaphoreType.DMA((n, NCHIP)), pltpu.SemaphoreType.DMA((n, NCHIP))]

    def _copies(self, ins, outs, sems):
        send_sems, recv_sems = sems
        x, y, c = _position()
        return [pltpu.make_async_remote_copy(
            src_ref=ins[a].at[2 * k + (1 - c)], dst_ref=outs[a].at[k],
            send_sem=send_sems.at[a, k], recv_sem=recv_sems.at[a, k],
            device_id=(x, y, 1 - c), device_id_type=MESH) for a in range(len(ins)) for k in range(NCHIP)]

    def start(self, ins, outs, sems):
        for cp in self._copies(ins, outs, sems):
            cp.start()

    def finish(self, ins, outs, sems):
        copies = self._copies(ins, outs, sems)
        for cp in copies:
            cp.wait_recv()
        for cp in copies:
            cp.wait_send()


class _ToChips:
    def __init__(self, ps):
        n = len(ps)
        self.arrs = list(ps)
        self.out_shape = [jax.ShapeDtypeStruct(p.shape, p.dtype) for p in ps]
        self.scratch = [pltpu.SemaphoreType.DMA((n, 3)), pltpu.SemaphoreType.DMA((n, 3)),
                        pltpu.SemaphoreType.DMA((n,))]

    def _copies(self, ins, outs, sems, arrivals):
        send_sems, recv_sems, local_sems = sems
        x, y, c = _position()
        mychip = 2 * x + y
        n = len(ins)
        mine = [pltpu.make_async_copy(ins[a].at[mychip], outs[a].at[mychip], local_sems.at[a]) for a in range(n)]
        sends, recvs = [], []
        for a in range(n):
            for j, chip in enumerate(_chips(x, y)):
                sends.append(pltpu.make_async_remote_copy(
                    src_ref=ins[a].at[2 * chip[0] + chip[1]], dst_ref=outs[a].at[mychip],
                    send_sem=send_sems.at[a, j], recv_sem=recv_sems.at[a, j],
                    device_id=(*chip, c), device_id_type=MESH))
                if arrivals:
                    slot = outs[a].at[2 * chip[0] + chip[1]]
                    recvs.append(pltpu.make_async_remote_copy(
                        src_ref=slot, dst_ref=slot, send_sem=send_sems.at[a, j], recv_sem=recv_sems.at[a, j],
                        device_id=(*chip, c), device_id_type=MESH))
        return mine, sends, recvs

    def start(self, ins, outs, sems):
        mine, sends, _ = self._copies(ins, outs, sems, False)
        for cp in mine + sends:
            cp.start()

    def finish(self, ins, outs, sems):
        mine, sends, recvs = self._copies(ins, outs, sems, True)
        for cp in recvs:
            cp.wait_recv()
        for cp in sends:
            cp.wait_send()
        for cp in mine:
            cp.wait()


HOSTED_MIDDLE_AT = 0.95


def _middle_of(comm, ins, outs, sems):
    if hasattr(comm, "middle"):
        comm.middle(ins, outs, sems)


class _Both:
    def __init__(self, first, second):
        self.parts = (first, second)
        self.arrs = first.arrs + second.arrs
        self.out_shape = first.out_shape + second.out_shape
        self.scratch = first.scratch + second.scratch

    def _split(self, ins, outs, sems):
        a, _ = self.parts
        ni, no, ns = len(a.arrs), len(a.out_shape), len(a.scratch)
        return (ins[:ni], outs[:no], sems[:ns]), (ins[ni:], outs[no:], sems[ns:])

    def start(self, ins, outs, sems):
        for part, refs in zip(self.parts, self._split(ins, outs, sems)):
            part.start(*refs)

    def middle(self, ins, outs, sems):
        for part, refs in zip(self.parts, self._split(ins, outs, sems)):
            _middle_of(part, *refs)

    def finish(self, ins, outs, sems):
        for part, refs in zip(self.parts, self._split(ins, outs, sems)):
            part.finish(*refs)

    def split_results(self, res):
        no = len(self.parts[0].out_shape)
        return res[:no], res[no:]


def _comm_only(comm, name):
    n_in, n_out = len(comm.arrs), len(comm.out_shape)

    def body(*refs):
        ins, outs, sems = refs[:n_in], refs[n_in:n_in + n_out], refs[n_in + n_out:]
        comm.start(ins, outs, sems)
        _middle_of(comm, ins, outs, sems)
        comm.finish(ins, outs, sems)

    return pl.pallas_call(
        body, name=name, out_shape=comm.out_shape, in_specs=[ANY] * n_in, out_specs=[ANY] * n_out,
        scratch_shapes=comm.scratch,
    )(*comm.arrs)


def _hosted_call(body, operands, *, name, grid, in_specs, out_specs, out_shape, scratch_shapes=(), sem=(),
                 aliases=None, comm=None):
    single = not isinstance(out_shape, (list, tuple))
    o_specs = [out_specs] if single else list(out_specs)
    o_shape = [out_shape] if single else list(out_shape)
    n_in, n_out, n_scr = len(in_specs), len(o_shape), len(scratch_shapes)
    if comm is None:
        res = pl.pallas_call(body, name=name, grid=grid, in_specs=list(in_specs), out_specs=o_specs,
                             out_shape=o_shape, scratch_shapes=list(scratch_shapes),
                             input_output_aliases=aliases or {}, compiler_params=_cp(*sem))(*operands)
        return (res[0] if single else res), []
    c_in, c_out = len(comm.arrs), len(comm.out_shape)

    def wrapped(*refs):
        ins, cins = refs[:n_in], refs[n_in:n_in + c_in]
        o0 = n_in + c_in
        outs, couts = refs[o0:o0 + n_out], refs[o0 + n_out:o0 + n_out + c_out]
        s0 = o0 + n_out + c_out
        scr, csems = refs[s0:s0 + n_scr], refs[s0 + n_scr:]
        pids = [pl.program_id(a) for a in range(len(grid))]
        step = functools.reduce(lambda acc, pg: acc * pg[1] + pg[0], zip(pids, grid), 0)
        total = int(np.prod(grid))
        late = min(total - 1, max(1, int(total * HOSTED_MIDDLE_AT)))

        @pl.when(step == 0)
        def _():
            comm.start(cins, couts, csems)

        @pl.when(step == late)
        def _():
            _middle_of(comm, cins, couts, csems)

        body(*ins, *outs, *scr)

        @pl.when(step == total - 1)
        def _():
            comm.finish(cins, couts, csems)

    res = pl.pallas_call(
        wrapped, name=name, grid=grid, in_specs=list(in_specs) + [ANY] * c_in, out_specs=o_specs + [ANY] * c_out,
        out_shape=o_shape + comm.out_shape, scratch_shapes=list(scratch_shapes) + comm.scratch,
        input_output_aliases=aliases or {}, compiler_params=_cp(*(["arbitrary"] * len(grid))),
    )(*operands, *comm.arrs)
    return (res[0] if single else res[:n_out]), res[n_out:]


def _adamw(w, g, m, v):
    m2 = ADAM_B1 * m + (1.0 - ADAM_B1) * g
    v2 = ADAM_B2 * v + (1.0 - ADAM_B2) * (g * g)
    m_hat = m2 / (1.0 - ADAM_B1 ** ADAM_STEP)
    v_hat = v2 / (1.0 - ADAM_B2 ** ADAM_STEP)
    delta = -ADAM_LR * (m_hat / (jnp.sqrt(v_hat) + ADAM_EPS) + ADAM_WD * w)
    return delta, m2, v2


def _add_sibling(g, recv, c_idx, name, part=(0, 1)):
    _, R, C = g.shape
    R = R // part[1]
    tr = min(R, 512)
    first = part[0] * (R // tr)

    def body(c_ref, g_ref, r_ref, o_ref):
        o_ref[...] = (g_ref[...] + r_ref[...]).astype(BF16)

    return pl.pallas_call(
        body, name=name,
        grid_spec=pltpu.PrefetchScalarGridSpec(
            num_scalar_prefetch=1, grid=(NCHIP, R // tr),
            in_specs=[pl.BlockSpec((1, tr, C), lambda k, i, c_ref: (2 * k + c_ref[0], first + i, 0)),
                      pl.BlockSpec((1, tr, C), lambda k, i, c_ref: (k, first + i, 0))],
            out_specs=pl.BlockSpec((1, tr, C), lambda k, i, c_ref: (k, i, 0))),
        out_shape=jax.ShapeDtypeStruct((NCHIP, R, C), BF16),
        compiler_params=_cp("parallel", "parallel"),
    )(c_idx, g, recv)


def _sum_adam(parts, w, m, v, row0, prev, name):
    K, R, C = parts.shape
    LR = w.shape[0]
    tr = min(R, 256)
    nb = R // tr
    first = row0 // tr

    def body(p_ref, w_ref, m_ref, v_ref, *rest):
        g_ref, d_ref, m2_ref, v2_ref = rest[-4:]
        g = p_ref[0].astype(F32)
        for k in range(1, K):
            g = g + p_ref[k].astype(F32)
        delta, m2, v2 = _adamw(w_ref[...], g, m_ref[...], v_ref[...])
        g_ref[...] = g
        d_ref[...] = delta
        m2_ref[...] = m2
        v2_ref[...] = v2

    blk = pl.BlockSpec((tr, C), lambda i: (first + i, 0))
    shp = jax.ShapeDtypeStruct((LR, C), F32)
    operands = [parts, w, m, v] + (list(prev) if prev is not None else [])
    return pl.pallas_call(
        body, name=name, grid=(nb,),
        in_specs=[pl.BlockSpec((K, tr, C), lambda i: (0, i, 0)), blk, blk, blk] + [ANY] * (len(operands) - 4),
        out_specs=[blk] * 4, out_shape=[shp] * 4,
        input_output_aliases={4 + k: k for k in range(len(operands) - 4)},
        compiler_params=_cp("parallel"),
    )(*operands)


def _sum_rows(parts, name):
    K, R, C = parts.shape
    tr = min(R, 256)
    while R % tr:
        tr //= 2

    def body(p_ref, o_ref):
        g = p_ref[0]
        for k in range(1, K):
            g = g + p_ref[k]
        o_ref[...] = g

    return pl.pallas_call(
        body, name=name, grid=(R // tr,),
        in_specs=[pl.BlockSpec((K, tr, C), lambda i: (0, i, 0))],
        out_specs=pl.BlockSpec((tr, C), lambda i: (i, 0)),
        out_shape=jax.ShapeDtypeStruct((R, C), F32),
        compiler_params=_cp("parallel"),
    )(parts)


def _adam_only(g, w, m, v, name):
    R, C = g.shape
    tr = min(R, 256)
    while R % tr:
        tr //= 2

    def body(g_ref, w_ref, m_ref, v_ref, d_ref, m2_ref, v2_ref):
        delta, m2, v2 = _adamw(w_ref[...], g_ref[...], m_ref[...], v_ref[...])
        d_ref[...] = delta
        m2_ref[...] = m2
        v2_ref[...] = v2

    blk = pl.BlockSpec((tr, C), lambda i: (i, 0))
    shp = jax.ShapeDtypeStruct((R, C), F32)
    return pl.pallas_call(
        body, name=name, grid=(R // tr,), in_specs=[blk] * 4, out_specs=[blk] * 3,
        out_shape=[shp] * 3, compiler_params=_cp("parallel"),
    )(g, w, m, v)


def _mod_fwd(c_all, w_mod, b_cols, name):
    L, D, n = w_mod.shape
    B = c_all.shape[0]

    def body(c_ref, w_ref, b_ref, o_ref):
        cv = c_ref[...]
        sc = (cv * _sigmoid(cv)).astype(BF16)
        o_ref[0] = jnp.dot(sc, w_ref[0].astype(BF16), preferred_element_type=F32) + b_ref[0]

    return pl.pallas_call(
        body, name=name, grid=(L,),
        in_specs=[pl.BlockSpec((B, D), lambda l: (0, 0)),
                  pl.BlockSpec((1, D, n), lambda l: (l, 0, 0)),
                  pl.BlockSpec((1, 1, n), lambda l: (l, 0, 0))],
        out_specs=pl.BlockSpec((1, B, n), lambda l: (l, 0, 0)),
        out_shape=jax.ShapeDtypeStruct((L, B, n), F32),
        compiler_params=_cp("parallel"),
    )(c_all, w_mod, b_cols)


def _wmod_grad_adam(sc_t, dm, w, m, v, name):
    L, D, n = w.shape
    KP = sc_t.shape[1]
    tr = min(D, 512)

    def body(s_ref, dm_ref, w_ref, m_ref, v_ref, g_ref, d_ref, m2_ref, v2_ref):
        g = jnp.dot(s_ref[...], dm_ref[0], preferred_element_type=F32,
                    precision=lax.Precision.HIGHEST)
        delta, m2, v2 = _adamw(w_ref[0], g, m_ref[0], v_ref[0])
        g_ref[0] = g
        d_ref[0] = delta
        m2_ref[0] = m2
        v2_ref[0] = v2

    blk = pl.BlockSpec((1, tr, n), lambda l, i: (l, i, 0))
    shp = jax.ShapeDtypeStruct((L, D, n), F32)
    return pl.pallas_call(
        body, name=name, grid=(L, D // tr),
        in_specs=[pl.BlockSpec((tr, KP), lambda l, i: (i, 0)),
                  pl.BlockSpec((1, KP, n), lambda l, i: (l, 0, 0)), blk, blk, blk],
        out_specs=[blk] * 4, out_shape=[shp] * 4,
        compiler_params=_cp("parallel", "parallel"),
    )(sc_t, dm, w, m, v)


def _vec_spec(D):
    return pl.BlockSpec((1, D), lambda i: (0, 0))


def _pre(x, res, gate, g, scale, shift, name):
    S, D = x.shape
    tr = min(S, ROW_TILE)
    has_res = res is not None
    row = pl.BlockSpec((tr, D), lambda i: (i, 0))

    def body(*refs):
        if has_res:
            x_ref, r_ref, gate_ref, g_ref, sc_ref, sh_ref, xl_ref, h_ref = refs
            xv = x_ref[...] + gate_ref[...] * r_ref[...]
            xl_ref[...] = xv
        else:
            x_ref, g_ref, sc_ref, sh_ref, h_ref = refs
            xv = x_ref[...]
        r = lax.rsqrt(jnp.mean(xv * xv, axis=-1, keepdims=True) + EPS)
        y = (xv * r) * g_ref[...]
        h_ref[...] = (y * (1.0 + sc_ref[...]) + sh_ref[...]).astype(BF16)

    vec = _vec_spec(D)
    if has_res:
        xl, h = pl.pallas_call(
            body, name=name, grid=(S // tr,),
            in_specs=[row, row, vec, vec, vec, vec], out_specs=[row, row],
            out_shape=[jax.ShapeDtypeStruct((S, D), F32), jax.ShapeDtypeStruct((S, D), BF16)],
            compiler_params=_cp("parallel"),
        )(x, res, gate, g, scale, shift)
        return xl, h
    h = pl.pallas_call(
        body, name=name, grid=(S // tr,),
        in_specs=[row, vec, vec, vec], out_specs=row,
        out_shape=jax.ShapeDtypeStruct((S, D), BF16),
        compiler_params=_cp("parallel"),
    )(x, g, scale, shift)
    return x, h


def _pre_bwd(xl, dh, dx_in, g, scale, name, comm=None, below=None):
    S, D = xl.shape
    tr = min(S, ROW_TILE)
    nsteps = S // tr
    row = pl.BlockSpec((tr, D), lambda i: (i, 0))
    vec = _vec_spec(D)

    def body(*refs):
        if below is None:
            x_ref, dh_ref, dxin_ref, g_ref, sc_ref, dx_ref, dsh_ref, dsc_ref, dg_ref, acc_sh, acc_t = refs
        else:
            (x_ref, dh_ref, dxin_ref, g_ref, sc_ref, o2_ref, gate2_ref,
             dx_ref, dsh_ref, dsc_ref, dg_ref, do2_ref, dgate2_ref, acc_sh, acc_t, acc_g2) = refs
        i = pl.program_id(0)
        xv = x_ref[...]
        dh = dh_ref[...]
        r = lax.rsqrt(jnp.mean(xv * xv, axis=-1, keepdims=True) + EPS)
        xn = xv * r
        part_sh = jnp.sum(dh.reshape(tr // 8, 8, D), axis=0)
        part_t = jnp.sum((dh * xn).reshape(tr // 8, 8, D), axis=0)

        @pl.when(i == 0)
        def _():
            acc_sh[...] = part_sh
            acc_t[...] = part_t

        @pl.when(i > 0)
        def _():
            acc_sh[...] += part_sh
            acc_t[...] += part_t

        dxn = dh * (g_ref[...] * (1.0 + sc_ref[...]))
        dxv = dxin_ref[...] + r * (dxn - xn * jnp.mean(dxn * xn, axis=-1, keepdims=True))
        dx_ref[...] = dxv
        if below is not None:
            do2_ref[...] = (dxv * gate2_ref[...]).astype(BF16)
            part_g2 = jnp.sum((dxv * o2_ref[...]).reshape(tr // 8, 8, D), axis=0)

            @pl.when(i == 0)
            def _():
                acc_g2[...] = part_g2

            @pl.when(i > 0)
            def _():
                acc_g2[...] += part_g2

        @pl.when(i == nsteps - 1)
        def _():
            t = jnp.sum(acc_t[...], axis=0, keepdims=True)
            dsh_ref[...] = jnp.sum(acc_sh[...], axis=0, keepdims=True)
            dsc_ref[...] = t * g_ref[...]
            dg_ref[...] = t * (1.0 + sc_ref[...])
            if below is not None:
                dgate2_ref[...] = jnp.sum(acc_g2[...], axis=0, keepdims=True)

    v = jax.ShapeDtypeStruct((1, D), F32)
    operands, in_specs = [xl, dh, dx_in, g, scale], [row, row, row, vec, vec]
    out_specs, out_shape = [row, vec, vec, vec], [jax.ShapeDtypeStruct((S, D), F32), v, v, v]
    scratch = [pltpu.VMEM((8, D), F32), pltpu.VMEM((8, D), F32)]
    if below is not None:
        operands, in_specs = operands + list(below), in_specs + [row, vec]
        out_specs, out_shape = out_specs + [row, vec], out_shape + [jax.ShapeDtypeStruct((S, D), BF16), v]
        scratch = scratch + [pltpu.VMEM((8, D), F32)]
    return _hosted_call(
        body, operands, name=name, grid=(nsteps,), in_specs=in_specs, out_specs=out_specs,
        out_shape=out_shape, scratch_shapes=scratch, sem=("arbitrary",), comm=comm)


def _post_bwd(dx, out, gate, name):
    S, D = dx.shape
    tr = min(S, ROW_TILE)
    nsteps = S // tr
    row = pl.BlockSpec((tr, D), lambda i: (i, 0))
    vec = _vec_spec(D)

    def body(dx_ref, o_ref, gate_ref, do_ref, dg_ref, acc):
        i = pl.program_id(0)
        dxv = dx_ref[...]
        do_ref[...] = (dxv * gate_ref[...]).astype(BF16)
        part = jnp.sum((dxv * o_ref[...]).reshape(tr // 8, 8, D), axis=0)

        @pl.when(i == 0)
        def _():
            acc[...] = part

        @pl.when(i > 0)
        def _():
            acc[...] += part

        @pl.when(i == nsteps - 1)
        def _():
            dg_ref[...] = jnp.sum(acc[...], axis=0, keepdims=True)

    return pl.pallas_call(
        body, name=name, grid=(nsteps,),
        in_specs=[row, row, vec], out_specs=[row, vec],
        out_shape=[jax.ShapeDtypeStruct((S, D), BF16), jax.ShapeDtypeStruct((1, D), F32)],
        scratch_shapes=[pltpu.VMEM((8, D), F32)],
        compiler_params=_cp("arbitrary"),
    )(dx, out, gate)


def _loss_head(x, res, gate, gf, tgt, name):
    S, D = x.shape
    tr = min(S, ROW_TILE)
    nsteps = S // tr
    row = pl.BlockSpec((tr, D), lambda i: (i, 0))
    vec = _vec_spec(D)

    def body(x_ref, r_ref, gate_ref, gf_ref, t_ref, dx_ref, loss_ref, dgf_ref, acc, lacc):
        i = pl.program_id(0)
        xv = x_ref[...] + gate_ref[...] * r_ref[...]
        r = lax.rsqrt(jnp.mean(xv * xv, axis=-1, keepdims=True) + EPS)
        xn = xv * r
        err = xn * gf_ref[...] - t_ref[...]
        row_loss = jnp.mean(err * err, axis=-1, keepdims=True)
        lpart = 0.5 * jnp.sum(row_loss, axis=0, keepdims=True)
        dy = err * (1.0 / D)
        part = jnp.sum((dy * xn).reshape(tr // 8, 8, D), axis=0)

        @pl.when(i == 0)
        def _():
            acc[...] = part
            lacc[...] = lpart

        @pl.when(i > 0)
        def _():
            acc[...] += part
            lacc[...] += lpart

        dxn = dy * gf_ref[...]
        dx_ref[...] = r * (dxn - xn * jnp.mean(dxn * xn, axis=-1, keepdims=True))

        @pl.when(i == nsteps - 1)
        def _():
            dgf_ref[...] = jnp.sum(acc[...], axis=0, keepdims=True)
            loss_ref[...] = lacc[...]

    return pl.pallas_call(
        body, name=name, grid=(nsteps,),
        in_specs=[row, row, vec, vec, row],
        out_specs=[row, pl.BlockSpec((1, 1), lambda i: (0, 0)), vec],
        out_shape=[jax.ShapeDtypeStruct((S, D), F32), jax.ShapeDtypeStruct((1, 1), F32),
                   jax.ShapeDtypeStruct((1, D), F32)],
        scratch_shapes=[pltpu.VMEM((8, D), F32), pltpu.VMEM((1, 1), F32)],
        compiler_params=_cp("arbitrary"),
    )(x, res, gate, gf, tgt)


NN = (((1,), (0,)), ((), ()))
NT = (((1,), (1,)), ((), ()))
TN = (((0,), (0,)), ((), ()))


def _mm(name, a, b, out_shape, grid, a_spec, b_spec, o_spec, dims, a2d, b2d, k_axis, sem, alias=None, comm=None):
    def body(*refs):
        a_ref, b_ref, o_ref = refs[0], refs[1], refs[-1]
        r = lax.dot_general(a_ref[...].reshape(a2d), b_ref[...].reshape(b2d), dims,
                            preferred_element_type=F32)
        r = r.reshape(o_ref.shape)
        if k_axis is None:
            o_ref[...] = r.astype(o_ref.dtype)
        else:
            k = pl.program_id(k_axis)

            @pl.when(k == 0)
            def _():
                o_ref[...] = r

            @pl.when(k > 0)
            def _():
                o_ref[...] += r

    operands, in_specs, aliases = [a, b], [a_spec, b_spec], {}
    if alias is not None:
        operands.append(alias)
        in_specs.append(ANY)
        aliases = {2: 0}
    res, extra = _hosted_call(body, operands, name=name, grid=grid, in_specs=in_specs, out_specs=o_spec,
                              out_shape=out_shape, sem=sem, aliases=aliases, comm=comm)
    return res if comm is None else (res, extra)


def _tile(n, pref):
    t = min(n, pref)
    while n % t:
        t -= 128
    return t


def _mm_nn_in(a, w, l, name, comm=None):
    M, K = a.shape
    _, _, _, n = w.shape
    tm, tn = min(M, 512), _tile(n, 1024)
    nb = n // tn
    return _mm(name, a, w, jax.ShapeDtypeStruct((M, NDEV * n), F32), (NDEV * nb, M // tm),
               pl.BlockSpec((tm, K), lambda j, i: (i, 0)),
               pl.BlockSpec((1, 1, K, tn), lambda j, i: (j // nb, l, 0, j % nb)),
               pl.BlockSpec((tm, tn), lambda j, i: (i, j)),
               NN, (tm, K), (K, tn), None, ("parallel", "parallel"), comm=comm)


def _mm_nn_out(a, w, l, name):
    M, K = a.shape
    _, _, kb, N = w.shape
    tm, tn = min(M, 512), _tile(N, 1024)
    return _mm(name, a, w, jax.ShapeDtypeStruct((M, N), F32), (N // tn, M // tm),
               pl.BlockSpec((tm, K), lambda j, i: (i, 0)),
               pl.BlockSpec((NDEV, 1, kb, tn), lambda j, i: (0, l, 0, j)),
               pl.BlockSpec((tm, tn), lambda j, i: (i, j)),
               NN, (tm, K), (K, tn), None, ("parallel", "parallel"))


def _mm_nt_in(a, w, l, name, comm=None):
    M, _ = a.shape
    _, _, K, n = w.shape
    tm, tk = min(M, 1024), _tile(K, 1024)
    gb = 2 if n <= 1024 else 1

    def body(a_ref, w_ref, o_ref):
        k = pl.program_id(2)
        r = lax.dot_general(a_ref[:, :n], w_ref[0, 0], NT, preferred_element_type=F32)
        for g in range(1, gb):
            r = r + lax.dot_general(a_ref[:, g * n:(g + 1) * n], w_ref[g, 0], NT, preferred_element_type=F32)

        @pl.when(k == 0)
        def _():
            o_ref[...] = r

        @pl.when(k > 0)
        def _():
            o_ref[...] += r

    res, extra = _hosted_call(
        body, [a, w], name=name, grid=(M // tm, K // tk, NDEV // gb),
        in_specs=[pl.BlockSpec((tm, gb * n), lambda i, j, k: (i, k)),
                  pl.BlockSpec((gb, 1, tk, n), lambda i, j, k: (k, l, j, 0))],
        out_specs=pl.BlockSpec((tm, tk), lambda i, j, k: (i, j)),
        out_shape=jax.ShapeDtypeStruct((M, K), F32),
        sem=("parallel", "parallel", "arbitrary"), comm=comm)
    return res if comm is None else (res, extra)


def _mm_nt_out(a, w, l, name):
    M, N = a.shape
    _, _, kb, _ = w.shape
    K = NDEV * kb
    tm, tk, tc = min(M, 1024), _tile(K, 1024), _tile(N, 2048)
    per = tk // kb
    return _mm(name, a, w, jax.ShapeDtypeStruct((M, K), F32), (M // tm, K // tk, N // tc),
               pl.BlockSpec((tm, tc), lambda i, j, k: (i, k)),
               pl.BlockSpec((per, 1, kb, tc), lambda i, j, k: (j, l, 0, k)),
               pl.BlockSpec((tm, tk), lambda i, j, k: (i, j)),
               NT, (tm, tc), (tk, tc), 2, ("parallel", "parallel", "arbitrary"))


def _mm_tn_in(a, b, l, L, buf, name, comm=None, part=(0, 1)):
    S, K = a.shape
    K = K // part[1]
    n = b.shape[1] // NDEV
    ts, tk, tn = min(S, 2048), _tile(K, 1024), _tile(n, 1024)
    nb = n // tn
    first = part[0] * (K // tk)
    return _mm(name, a, b, jax.ShapeDtypeStruct((NDEV, L, K, n), F32), (NDEV * nb, K // tk, S // ts),
               pl.BlockSpec((ts, tk), lambda j, i, s: (s, first + i)),
               pl.BlockSpec((ts, tn), lambda j, i, s: (s, j)),
               pl.BlockSpec((1, 1, tk, tn), lambda j, i, s: (j // nb, l, i, j % nb)),
               TN, (ts, tk), (ts, tn), 2, ("parallel", "parallel", "arbitrary"), alias=buf, comm=comm)


def _mm_tn_out(a, b, l, L, buf, name):
    S, K = a.shape
    N = b.shape[1]
    kb = K // NDEV
    ts, tk, tn = min(S, 2048), _tile(K, 1024), _tile(N, 1024)
    per = tk // kb
    return _mm(name, a, b, jax.ShapeDtypeStruct((NDEV, L, kb, N), F32), (N // tn, K // tk, S // ts),
               pl.BlockSpec((ts, tk), lambda j, i, s: (s, i)),
               pl.BlockSpec((ts, tn), lambda j, i, s: (s, j)),
               pl.BlockSpec((per, 1, kb, tn), lambda j, i, s: (i, l, 0, j)),
               TN, (ts, tk), (ts, tn), 2, ("parallel", "parallel", "arbitrary"), alias=buf)


def _attn_bias(T):
    reach = max(w // 2 for w, _ in DILATED_PATTERNS)
    hb = -(-reach // T)
    i = np.arange(T)[:, None]
    j = np.arange(T)[None, :]
    tiles = []
    for d in range(-hb, hb + 1):
        rel = j + d * T - i
        mult = np.zeros((T, T), np.float64)
        for window, dil in DILATED_PATTERNS:
            radius = window // (2 * dil)
            mult += (rel % dil == 0) & (np.abs(rel) <= radius * dil)
        tiles.append(np.where(mult > 0, np.log(np.maximum(mult, 1.0)), NEG_INF))
    return jnp.asarray(np.stack(tiles), F32)


def _rope_tables(S):
    half = HEAD_DIM // 2
    pos = jnp.arange(S, dtype=F32)
    inv = ROPE_THETA ** (-jnp.arange(half, dtype=F32) / half)
    ang = pos[:, None] * inv[None, :]
    cos, sin = jnp.cos(ang), jnp.sin(ang)
    return jnp.concatenate([cos, cos], axis=-1), jnp.concatenate([-sin, sin], axis=-1)


def _rope_apply(t, cosf, sinf, heads, sign):
    outs = []
    for hh in range(heads):
        th = t[:, hh * HEAD_DIM:(hh + 1) * HEAD_DIM]
        outs.append(th * cosf + sign * (pltpu.roll(th, HEAD_DIM // 2, 1) * sinf))
    return outs


def _rope_qkv(proj, cosf, sinf, W, name):
    S = proj.shape[0]
    tr = min(S, ROW_TILE)
    heads = W // HEAD_DIM

    def body(q_ref, k_ref, v_ref, c_ref, s_ref, qo_ref, ko_ref, vo_ref):
        cosf_v, sinf_v = c_ref[...], s_ref[...]
        for src, dst, mult in ((q_ref, qo_ref, HEAD_DIM ** -0.5), (k_ref, ko_ref, 1.0)):
            for hh, val in enumerate(_rope_apply(src[...], cosf_v, sinf_v, heads, 1.0)):
                dst[:, hh * HEAD_DIM:(hh + 1) * HEAD_DIM] = (val * mult).astype(BF16)
        vo_ref[...] = v_ref[...].astype(BF16)

    piece = lambda p: pl.BlockSpec((tr, W), lambda i: (i, p))
    tab = pl.BlockSpec((tr, HEAD_DIM), lambda i: (i, 0))
    out = pl.BlockSpec((tr, W), lambda i: (i, 0))
    shp = jax.ShapeDtypeStruct((S, W), BF16)
    return pl.pallas_call(
        body, name=name, grid=(S // tr,),
        in_specs=[piece(0), piece(1), piece(2), tab, tab], out_specs=[out] * 3, out_shape=[shp] * 3,
        compiler_params=_cp("parallel"),
    )(proj, proj, proj, cosf, sinf)


def _attn_fwd(q, k, v, bias, name, comm=None):
    S, W = q.shape
    H = W // HEAD_DIM
    nd, T, _ = bias.shape
    hb, nq = nd // 2, S // T
    scale = HEAD_DIM ** -0.5
    hp = min(H, HEADS_PER_STEP)
    rc = min(T, ATTN_ROW_CHUNK)
    wp = hp * HEAD_DIM

    def body(q_ref, k_ref, v_ref, b_ref, o_ref, lse_ref, m_s, l_s, acc_s):
        i, d = pl.program_id(1), pl.program_id(2)
        j = i + d - hb

        @pl.when(d == 0)
        def _():
            m_s[...] = jnp.full(m_s.shape, -jnp.inf, F32)
            l_s[...] = jnp.zeros(l_s.shape, F32)
            acc_s[...] = jnp.zeros(acc_s.shape, F32)

        @pl.when((j >= 0) & (j < nq))
        def _():
            items = [(hh, c) for hh in range(hp) for c in range(T // rc)]

            def scores(item):
                hh, c = item
                cols, rows = slice(hh * HEAD_DIM, (hh + 1) * HEAD_DIM), slice(c * rc, (c + 1) * rc)
                return (lax.dot_general(q_ref[rows, cols], k_ref[:, cols], NT, preferred_element_type=F32)
                        + b_ref[d, rows, :])

            def weighted_values(item, p, alpha):
                hh, c = item
                cols, rows = slice(hh * HEAD_DIM, (hh + 1) * HEAD_DIM), slice(c * rc, (c + 1) * rc)
                acc_s[rows, cols] = alpha * acc_s[rows, cols] + jnp.dot(p, v_ref[:, cols],
                                                                        preferred_element_type=F32)

            s_next, pending = scores(items[0]), None
            for n, (hh, c) in enumerate(items):
                rows = slice(c * rc, (c + 1) * rc)
                s = s_next
                if n + 1 < len(items):
                    s_next = scores(items[n + 1])
                if pending is not None:
                    weighted_values(*pending)
                parts = [s[:, t * LANES:(t + 1) * LANES] for t in range(T // LANES)]
                m_old = m_s[hh, rows, :]
                m_cur = jnp.max(functools.reduce(jnp.maximum, parts), axis=1, keepdims=True)
                m_new = jnp.maximum(m_old, m_cur)
                alpha = jnp.exp(m_old - m_new)
                ps = [jnp.exp(part - m_new) for part in parts]
                l_s[hh, rows, :] = alpha * l_s[hh, rows, :] + functools.reduce(jnp.add, ps)
                m_s[hh, rows, :] = m_new
                pending = ((hh, c), jnp.concatenate(ps, axis=1).astype(BF16), alpha)
            weighted_values(*pending)

        @pl.when(d == nd - 1)
        def _():
            for hh in range(hp):
                cols = slice(hh * HEAD_DIM, (hh + 1) * HEAD_DIM)
                l = jnp.sum(l_s[hh], axis=1, keepdims=True)
                o_ref[:, cols] = acc_s[:, cols] / l
                lse_ref[hh] = m_s[hh][:, :1] + jnp.log(l)

    kv = pl.BlockSpec((T, wp), lambda h, i, d: (jnp.clip(i + d - hb, 0, nq - 1), h))
    return _hosted_call(
        body, [q, k, v, bias], name=name, grid=(H // hp, nq, nd),
        in_specs=[pl.BlockSpec((T, wp), lambda h, i, d: (i, h)), kv, kv,
                  pl.BlockSpec((nd, T, T), lambda h, i, d: (0, 0, 0))],
        out_specs=[pl.BlockSpec((T, wp), lambda h, i, d: (i, h)),
                   pl.BlockSpec((hp, T, 1), lambda h, i, d: (h, i, 0))],
        out_shape=[jax.ShapeDtypeStruct((S, W), F32), jax.ShapeDtypeStruct((H, S, 1), F32)],
        scratch_shapes=[pltpu.VMEM((hp, T, LANES), F32), pltpu.VMEM((hp, T, LANES), F32),
                        pltpu.VMEM((T, wp), F32)],
        sem=("parallel", "parallel", "arbitrary"), comm=comm)


def _attn_bwd(q, k, v, do, lse, delta, bias, name, comm=None):
    S, W = q.shape
    H = W // HEAD_DIM
    nd, T, _ = bias.shape
    hb, nq = nd // 2, S // T
    scale = HEAD_DIM ** -0.5
    hp = min(H, HEADS_PER_STEP)
    rc = min(T, ATTN_ROW_CHUNK)
    wp = hp * HEAD_DIM

    def body(q_ref, do_ref, lse_ref, dl_ref, k_ref, v_ref, b_ref, dq_ref, dk_ref, dv_ref):
        j, d = pl.program_id(1), pl.program_id(2)
        i = j + d - hb

        @pl.when((j == 0) & (d == 0))
        def _():
            dq_ref[...] = jnp.zeros(dq_ref.shape, F32)

        @pl.when(d == 0)
        def _():
            dk_ref[...] = jnp.zeros(dk_ref.shape, F32)
            dv_ref[...] = jnp.zeros(dv_ref.shape, F32)

        @pl.when((i >= 0) & (i < nq))
        def _():
            items = [(hh, c) for hh in range(hp) for c in range(T // rc)]

            def slices(item):
                hh, c = item
                return slice(hh * HEAD_DIM, (hh + 1) * HEAD_DIM), slice(c * rc, (c + 1) * rc)

            def products(item):
                cols, rows = slices(item)
                s = (lax.dot_general(q_ref[rows, cols], k_ref[:, cols], NT, preferred_element_type=F32)
                     + b_ref[nd - 1 - d, rows, :])
                dp = lax.dot_general(do_ref[rows, cols], v_ref[:, cols], NT, preferred_element_type=F32)
                return s, dp

            def gradients(item, p, ds):
                cols, rows = slices(item)
                dv_ref[:, cols] += lax.dot_general(p, do_ref[rows, cols], TN, preferred_element_type=F32)
                dk_ref[:, cols] += lax.dot_general(ds, q_ref[rows, cols], TN, preferred_element_type=F32)
                q_rows = pl.ds(pl.multiple_of(i * T + item[1] * rc, rc), rc)
                dq_ref[q_rows, cols] += jnp.dot(ds, k_ref[:, cols], preferred_element_type=F32) * scale

            nxt, pending = products(items[0]), None
            for n, item in enumerate(items):
                s, dp = nxt
                if n + 1 < len(items):
                    nxt = products(items[n + 1])
                if pending is not None:
                    gradients(*pending)
                _, rows = slices(item)
                p = jnp.exp(s - lse_ref[item[0], rows, :])
                ds = p * (dp - dl_ref[item[0], rows, :])
                pending = (item, p.astype(BF16), ds.astype(BF16))
            gradients(*pending)

    qi = lambda h, j, d: (jnp.clip(j + d - hb, 0, nq - 1), h)
    qs = pl.BlockSpec((T, wp), qi)
    col = pl.BlockSpec((hp, T, 1), lambda h, j, d: (h, jnp.clip(j + d - hb, 0, nq - 1), 0))
    kv = pl.BlockSpec((T, wp), lambda h, j, d: (j, h))
    shp = jax.ShapeDtypeStruct((S, W), F32)
    return _hosted_call(
        body, [q, do, lse, delta, k, v, bias], name=name, grid=(H // hp, nq, nd),
        in_specs=[qs, qs, col, col, kv, kv, pl.BlockSpec((nd, T, T), lambda h, j, d: (0, 0, 0))],
        out_specs=[pl.BlockSpec((S, wp), lambda h, j, d: (0, h)), kv, kv],
        out_shape=[shp, shp, shp],
        sem=("parallel", "arbitrary", "arbitrary"), comm=comm)


def _halo_specs(S, tr, W, piece):
    per, last = tr // 8, S // 8 - 1
    prev = pl.BlockSpec((8, W), lambda i: (jnp.maximum(i * per - 1, 0), piece))
    nxt = pl.BlockSpec((8, W), lambda i: (jnp.minimum((i + 1) * per, last), piece))
    return prev, nxt


def _shifted(t, before, after, tr):
    rows = lax.broadcasted_iota(jnp.int32, (tr, 1), 0)
    prev = jnp.where(rows == 0, before, pltpu.roll(t, 1, 0))
    nxt = jnp.where(rows == tr - 1, after, pltpu.roll(t, tr - 1, 0))
    return prev, nxt


def _ab_mix(attn, proj, conv_w, W, name):
    S = attn.shape[0]
    tr = min(S, ROW_TILE)
    nsteps = S // tr

    def body(a_ref, za_ref, ub_ref, gb_ref, gc_ref, zb_ref, ubp, ubn, gcp, gcn, w_ref, y_ref):
        i = pl.program_id(0)
        t = gc_ref[...] * ub_ref[...]
        before = jnp.where(i == 0, 0.0, (gcp[...] * ubp[...])[7:8, :])
        after = jnp.where(i == nsteps - 1, 0.0, (gcn[...] * ubn[...])[0:1, :])
        t_prev, t_next = _shifted(t, before, after, tr)
        w = w_ref[...]
        cv = w[0:1, :] * t_prev + w[1:2, :] * t + w[2:3, :] * t_next
        silu_a, _ = _silu_and_grad(za_ref[...])
        silu_b, _ = _silu_and_grad(zb_ref[...])
        y_ref[:, :W] = (a_ref[...] * silu_a).astype(BF16)
        y_ref[:, W:] = (gb_ref[...] * cv * silu_b).astype(BF16)

    piece = lambda p: pl.BlockSpec((tr, W), lambda i: (i, p))
    ubp, ubn = _halo_specs(S, tr, W, 4)
    gcp, gcn = _halo_specs(S, tr, W, 6)
    return pl.pallas_call(
        body, name=name, grid=(nsteps,),
        in_specs=[pl.BlockSpec((tr, W), lambda i: (i, 0)), piece(3), piece(4), piece(5), piece(6), piece(7),
                  ubp, ubn, gcp, gcn, pl.BlockSpec((3, W), lambda i: (0, 0))],
        out_specs=pl.BlockSpec((tr, 2 * W), lambda i: (i, 0)),
        out_shape=jax.ShapeDtypeStruct((S, 2 * W), BF16),
        compiler_params=_cp("parallel"),
    )(attn, proj, proj, proj, proj, proj, proj, proj, proj, proj, conv_w)


def _dattn_prep(dy, proj, attn, W, name):
    S = attn.shape[0]
    H = W // HEAD_DIM
    tr = min(S, ROW_TILE)

    def body(dy_ref, za_ref, a_ref, do_ref, dl_ref):
        silu_a, _ = _silu_and_grad(za_ref[...])
        do = dy_ref[...] * silu_a
        do_ref[...] = do.astype(BF16)
        prod = do * a_ref[...]
        for hh in range(H):
            dl_ref[hh] = jnp.sum(prod[:, hh * HEAD_DIM:(hh + 1) * HEAD_DIM], axis=1, keepdims=True)

    row = pl.BlockSpec((tr, W), lambda i: (i, 0))
    return pl.pallas_call(
        body, name=name, grid=(S // tr,),
        in_specs=[row, pl.BlockSpec((tr, W), lambda i: (i, 3)), row],
        out_specs=[row, pl.BlockSpec((H, tr, 1), lambda i: (0, i, 0))],
        out_shape=[jax.ShapeDtypeStruct((S, W), BF16), jax.ShapeDtypeStruct((H, S, 1), F32)],
        compiler_params=_cp("parallel"),
    )(dy, proj, attn)


def _ab_bwd(dy, attn, proj, dqr, dkr, dv, cosf, sinf, conv_w, W, name):
    S = attn.shape[0]
    tr = min(S, ROW_TILE // 2)
    nsteps = S // tr
    heads = W // HEAD_DIM

    def body(dya_ref, dyb_ref, a_ref, za_ref, ub_ref, gb_ref, gc_ref, zb_ref, dq_ref, dk_ref, dv_ref,
             c_ref, s_ref, w_ref, dybp, dybn, gbp, gbn, zbp, zbn, ubp, ubn, gcp, gcn,
             dp_ref, dw_ref, acc):
        i = pl.program_id(0)
        first, last = i == 0, i == nsteps - 1
        w = w_ref[...]
        w0, w1, w2 = w[0:1, :], w[1:2, :], w[2:3, :]
        ub, gb, gc, zb = ub_ref[...], gb_ref[...], gc_ref[...], zb_ref[...]
        dyb = dyb_ref[...]
        silu_a, dsilu_a = _silu_and_grad(za_ref[...])
        silu_b, dsilu_b = _silu_and_grad(zb)
        t = gc * ub
        t_prev, t_next = _shifted(t, jnp.where(first, 0.0, (gcp[...] * ubp[...])[7:8, :]),
                                  jnp.where(last, 0.0, (gcn[...] * ubn[...])[0:1, :]), tr)
        cv = w0 * t_prev + w1 * t + w2 * t_next
        dcv = dyb * gb * silu_b
        halo_p = dybp[...] * gbp[...] * _silu_and_grad(zbp[...])[0]
        halo_n = dybn[...] * gbn[...] * _silu_and_grad(zbn[...])[0]
        dcv_prev, dcv_next = _shifted(dcv, jnp.where(first, 0.0, halo_p[7:8, :]),
                                      jnp.where(last, 0.0, halo_n[0:1, :]), tr)
        dt = w0 * dcv_next + w1 * dcv + w2 * dcv_prev
        cosf_v, sinf_v = c_ref[...], s_ref[...]
        for src, base in ((dq_ref, 0), (dk_ref, W)):
            for hh, val in enumerate(_rope_apply(src[...], cosf_v, sinf_v, heads, -1.0)):
                dp_ref[:, base + hh * HEAD_DIM:base + (hh + 1) * HEAD_DIM] = val.astype(BF16)
        dp_ref[:, 2 * W:3 * W] = dv_ref[...].astype(BF16)
        dp_ref[:, 3 * W:4 * W] = (dya_ref[...] * a_ref[...] * dsilu_a).astype(BF16)
        dp_ref[:, 4 * W:5 * W] = (dt * gc).astype(BF16)
        dp_ref[:, 5 * W:6 * W] = (dyb * cv * silu_b).astype(BF16)
        dp_ref[:, 6 * W:7 * W] = (dt * ub).astype(BF16)
        dp_ref[:, 7 * W:8 * W] = (dyb * gb * cv * dsilu_b).astype(BF16)
        tap = lax.broadcasted_iota(jnp.int32, (8, 1), 0)
        part = (jnp.where(tap == 0, jnp.sum(dcv * t_prev, axis=0, keepdims=True), 0.0)
                + jnp.where(tap == 1, jnp.sum(dcv * t, axis=0, keepdims=True), 0.0)
                + jnp.where(tap == 2, jnp.sum(dcv * t_next, axis=0, keepdims=True), 0.0))

        @pl.when(first)
        def _():
            acc[...] = part

        @pl.when(i > 0)
        def _():
            acc[...] += part

        @pl.when(last)
        def _():
            dw_ref[...] = acc[...]

    row = pl.BlockSpec((tr, W), lambda i: (i, 0))
    piece = lambda p: pl.BlockSpec((tr, W), lambda i: (i, p))
    tab = pl.BlockSpec((tr, HEAD_DIM), lambda i: (i, 0))
    dybp, dybn = _halo_specs(S, tr, W, 1)
    gbp, gbn = _halo_specs(S, tr, W, 5)
    zbp, zbn = _halo_specs(S, tr, W, 7)
    ubp, ubn = _halo_specs(S, tr, W, 4)
    gcp, gcn = _halo_specs(S, tr, W, 6)
    return pl.pallas_call(
        body, name=name, grid=(nsteps,),
        in_specs=[piece(0), piece(1), row, piece(3), piece(4), piece(5), piece(6), piece(7), row, row, row,
                  tab, tab, pl.BlockSpec((3, W), lambda i: (0, 0)),
                  dybp, dybn, gbp, gbn, zbp, zbn, ubp, ubn, gcp, gcn],
        out_specs=[pl.BlockSpec((tr, 8 * W), lambda i: (i, 0)), pl.BlockSpec((8, W), lambda i: (0, 0))],
        out_shape=[jax.ShapeDtypeStruct((S, 8 * W), BF16), jax.ShapeDtypeStruct((8, W), F32)],
        scratch_shapes=[pltpu.VMEM((8, W), F32)],
        compiler_params=_cp("arbitrary"),
    )(dy, dy, attn, proj, proj, proj, proj, proj, dqr, dkr, dv, cosf, sinf, conv_w,
      dy, dy, proj, proj, proj, proj, proj, proj, proj, proj)


def _sgu_centre(p_ref, vc_s, dvg_s, Dc):
    gw = Dc // C_GROUPS
    total = None
    for g in range(C_GROUPS):
        cs = slice(g * gw, (g + 1) * gw)
        vg, dvg = _gelu_and_grad(p_ref[:, Dc + g * gw:Dc + (g + 1) * gw])
        vc_s[:, cs] = vg
        if dvg_s is not None:
            dvg_s[:, cs] = dvg
        total = vg if total is None else total + vg
    mu = jnp.sum(total, axis=1, keepdims=True) * (1.0 / Dc)
    total = None
    for g in range(C_GROUPS):
        cs = slice(g * gw, (g + 1) * gw)
        vc = vc_s[:, cs] - mu
        vc_s[:, cs] = vc
        total = vc * vc if total is None else total + vc * vc
    return lax.rsqrt(jnp.sum(total, axis=1, keepdims=True) * (1.0 / Dc) + EPS)


def _sgu_fwd(proj, ln_g, ln_b, w_s, b_st, name, comm=None):
    S, Dc3 = proj.shape
    Dc = Dc3 // 3
    gw = Dc // C_GROUPS
    vec = pl.BlockSpec((1, Dc), lambda i: (0, 0))

    def body(p_ref, lng_ref, lnb_ref, ws_ref, bst_ref, y_ref, vc_s):
        rstd = _sgu_centre(p_ref, vc_s, None, Dc)
        bst = bst_ref[...]
        for g in range(C_GROUPS):
            cs = slice(g * gw, (g + 1) * gw)
            vn = (vc_s[:, cs] * rstd * lng_ref[:, cs] + lnb_ref[:, cs]).astype(BF16)
            mixed = jnp.dot(ws_ref[g].astype(BF16), vn, preferred_element_type=F32) + bst[:, g:g + 1]
            u, _ = _gelu_and_grad(p_ref[:, cs])
            sz, _ = _silu_and_grad(p_ref[:, 2 * Dc + g * gw:2 * Dc + (g + 1) * gw])
            y_ref[:, cs] = (u * mixed * sz).astype(BF16)

    return _hosted_call(
        body, [proj, ln_g, ln_b, w_s, b_st], name=name, grid=(S // C_CHUNK,),
        in_specs=[pl.BlockSpec((C_CHUNK, Dc3), lambda i: (i, 0)), vec, vec,
                  pl.BlockSpec((C_GROUPS, C_CHUNK, C_CHUNK), lambda i: (0, 0, 0)),
                  pl.BlockSpec((C_CHUNK, C_GROUPS), lambda i: (0, 0))],
        out_specs=pl.BlockSpec((C_CHUNK, Dc), lambda i: (i, 0)),
        out_shape=jax.ShapeDtypeStruct((S, Dc), BF16),
        scratch_shapes=[pltpu.VMEM((C_CHUNK, Dc), F32)],
        sem=("parallel",), comm=comm)


def _sgu_bwd(proj, dy, ln_g, ln_b, w_s, w_st, b_st, name):
    S, Dc3 = proj.shape
    Dc = Dc3 // 3
    gw = Dc // C_GROUPS
    nsteps = S // C_CHUNK
    vec = pl.BlockSpec((1, Dc), lambda i: (0, 0))
    wspec = pl.BlockSpec((C_GROUPS, C_CHUNK, C_CHUNK), lambda i: (0, 0, 0))

    def body(p_ref, dy_ref, lng_ref, lnb_ref, ws_ref, wst_ref, bst_ref,
             dp_ref, dws_ref, dbs_ref, dlg_ref, dlb_ref, acc_w, acc_b, acc_g, acc_lb, vc_s, dvg_s, dvh_s):
        i = pl.program_id(0)

        @pl.when(i == 0)
        def _():
            acc_w[...] = jnp.zeros(acc_w.shape, F32)
            acc_b[...] = jnp.zeros(acc_b.shape, F32)
            acc_g[...] = jnp.zeros(acc_g.shape, F32)
            acc_lb[...] = jnp.zeros(acc_lb.shape, F32)

        rstd = _sgu_centre(p_ref, vc_s, dvg_s, Dc)
        bst = bst_ref[...]
        octets = lambda t: jnp.sum(t.reshape(C_CHUNK // 8, 8, gw), axis=0)
        t1, t2 = None, None
        for g in range(C_GROUPS):
            cs = slice(g * gw, (g + 1) * gw)
            zs = slice(2 * Dc + g * gw, 2 * Dc + (g + 1) * gw)
            vhat = vc_s[:, cs] * rstd
            vn = (vhat * lng_ref[:, cs] + lnb_ref[:, cs]).astype(BF16)
            mixed = jnp.dot(ws_ref[g].astype(BF16), vn, preferred_element_type=F32) + bst[:, g:g + 1]
            u, du = _gelu_and_grad(p_ref[:, cs])
            sz, dsz = _silu_and_grad(p_ref[:, zs])
            dy = dy_ref[:, cs]
            dmixed = dy * u * sz
            dmb = dmixed.astype(BF16)
            acc_w[g] += lax.dot_general(dmb, vn, NT, preferred_element_type=F32)
            acc_b[g] += dmixed
            dvn = jnp.dot(wst_ref[g].astype(BF16), dmb, preferred_element_type=F32)
            acc_g[:, cs] += octets(dvn * vhat)
            acc_lb[:, cs] += octets(dvn)
            dvh = dvn * lng_ref[:, cs]
            dvh_s[:, cs] = dvh
            t1 = dvh if t1 is None else t1 + dvh
            t2 = dvh * vhat if t2 is None else t2 + dvh * vhat
            dp_ref[:, cs] = (dy * mixed * sz * du).astype(BF16)
            dp_ref[:, zs] = (dy * u * mixed * dsz).astype(BF16)
        m1 = jnp.sum(t1, axis=1, keepdims=True) * (1.0 / Dc)
        m2 = jnp.sum(t2, axis=1, keepdims=True) * (1.0 / Dc)
        for g in range(C_GROUPS):
            cs = slice(g * gw, (g + 1) * gw)
            dvgelu = rstd * (dvh_s[:, cs] - m1 - (vc_s[:, cs] * rstd) * m2)
            dp_ref[:, Dc + g * gw:Dc + (g + 1) * gw] = (dvgelu * dvg_s[:, cs]).astype(BF16)

        @pl.when(i == nsteps - 1)
        def _():
            dws_ref[...] = acc_w[...]
            for g in range(C_GROUPS):
                dbs_ref[g] = jnp.sum(acc_b[g], axis=1, keepdims=True)
            dlg_ref[...] = jnp.sum(acc_g[...], axis=0, keepdims=True)
            dlb_ref[...] = jnp.sum(acc_lb[...], axis=0, keepdims=True)

    v = jax.ShapeDtypeStruct((1, Dc), F32)
    return pl.pallas_call(
        body, name=name, grid=(nsteps,),
        in_specs=[pl.BlockSpec((C_CHUNK, Dc3), lambda i: (i, 0)), pl.BlockSpec((C_CHUNK, Dc), lambda i: (i, 0)),
                  vec, vec, wspec, wspec, pl.BlockSpec((C_CHUNK, C_GROUPS), lambda i: (0, 0))],
        out_specs=[pl.BlockSpec((C_CHUNK, Dc3), lambda i: (i, 0)), wspec,
                   pl.BlockSpec((C_GROUPS, C_CHUNK, 1), lambda i: (0, 0, 0)), vec, vec],
        out_shape=[jax.ShapeDtypeStruct((S, Dc3), BF16),
                   jax.ShapeDtypeStruct((C_GROUPS, C_CHUNK, C_CHUNK), F32),
                   jax.ShapeDtypeStruct((C_GROUPS, C_CHUNK, 1), F32), v, v],
        scratch_shapes=[pltpu.VMEM((C_GROUPS, C_CHUNK, C_CHUNK), F32), pltpu.VMEM((C_GROUPS, C_CHUNK, gw), F32),
                        pltpu.VMEM((8, Dc), F32), pltpu.VMEM((8, Dc), F32)]
                       + [pltpu.VMEM((C_CHUNK, Dc), F32)] * 3,
        compiler_params=_cp("arbitrary"),
    )(proj, dy, ln_g, ln_b, w_s, w_st, b_st)


PACK_COLS = 1024
PACK_ROWS = 64


def _pack(vectors, rows=PACK_ROWS):
    flat = jnp.concatenate([v.reshape(-1) for v in vectors])
    pad = (-flat.shape[0]) % (PACK_COLS * rows)
    return jnp.pad(flat, (0, pad)).reshape(-1, PACK_COLS)


def _unshard(g, off, shape):
    L, rest = shape[0], shape[1:]
    size = int(np.prod(shape))
    piece = g[:, off:off + size].reshape((NDEV,) + tuple(shape))
    nd = piece.ndim
    perm = tuple(range(1, nd - 1)) + (0, nd - 1)
    full = jnp.transpose(piece, perm)
    return full.reshape(tuple(shape[:-1]) + (NDEV * shape[-1],)), off + size


def kernel(x, c, ab_norm_g, ab_w_mod, ab_b_mod, ab_w_in, ab_conv_w, ab_w_out, sg_norm_g, sg_w_mod, sg_b_mod, sg_w_in, sg_ln_g, sg_ln_b, sg_w_s, sg_b_s, sg_w_out, final_norm_g, loss_target, m_ab_norm_g, m_ab_w_mod, m_ab_b_mod, m_ab_w_in, m_ab_conv_w, m_ab_w_out, m_sg_norm_g, m_sg_w_mod, m_sg_b_mod, m_sg_w_in, m_sg_ln_g, m_sg_ln_b, m_sg_w_s, m_sg_b_s, m_sg_w_out, m_final_norm_g, v_ab_norm_g, v_ab_w_mod, v_ab_b_mod, v_ab_w_in, v_ab_conv_w, v_ab_w_out, v_sg_norm_g, v_sg_w_mod, v_sg_b_mod, v_sg_w_in, v_sg_ln_g, v_sg_ln_b, v_sg_w_s, v_sg_b_s, v_sg_w_out, v_final_norm_g):
    _, S, D = x.shape
    L = ab_norm_g.shape[0]
    W = ab_conv_w.shape[2] * NDEV
    n_ab, n_sg = ab_w_in.shape[2], sg_w_in.shape[2]
    n_mod = ab_w_mod.shape[2]
    kb = ab_w_out.shape[1]
    xi, yi, ci = _position()
    dev = 4 * xi + 2 * yi + ci
    x2, tgt = x.reshape(S, D), loss_target.reshape(S, D)

    small = [c, ab_conv_w, sg_norm_g, sg_ln_g, sg_ln_b]
    (g1,) = _comm_only(_GatherDirect([_pack(small, 8)]), "ag_small")
    g1 = g1.reshape(NDEV, -1)
    c_all = g1[:, :D]
    off = D
    conv_full, off = _unshard(g1, off, ab_conv_w.shape)
    sg_norm_full, off = _unshard(g1, off, sg_norm_g.shape)
    ln_g_full, off = _unshard(g1, off, sg_ln_g.shape)
    ln_b_full, off = _unshard(g1, off, sg_ln_b.shape)

    ab_b_cols = lax.dynamic_slice_in_dim(ab_b_mod, dev * n_mod, n_mod, axis=1)
    m_ab = _mod_fwd(c_all, ab_w_mod, ab_b_cols.reshape(L, 1, n_mod), "mod_fwd_ab")
    m_sg = _mod_fwd(c_all, sg_w_mod, sg_b_mod.reshape(L, 1, n_mod), "mod_fwd_sg")
    m_part = jnp.stack([m_ab, m_sg]).transpose(2, 0, 1, 3).reshape(NDEV, 2 * L * n_mod)
    (g2,) = _comm_only(_GatherDirect([m_part]), "ag_mod")
    mine = lax.dynamic_index_in_dim(g2, dev, axis=1, keepdims=False)
    mods = mine.reshape(NDEV, 2, L, n_mod).transpose(1, 2, 0, 3).reshape(2, L, 3 * D)

    def mod_of(kind, i):
        m = mods[kind, i]
        return m[:D].reshape(1, D), m[D:2 * D].reshape(1, D), m[2 * D:].reshape(1, D)

    big_w = [[(ab_w_in, m_ab_w_in, v_ab_w_in), (ab_w_out, m_ab_w_out, v_ab_w_out)],
             [(sg_w_in, m_sg_w_in, v_sg_w_in), (sg_w_out, m_sg_w_out, v_sg_w_out)]]
    big_names = [["ab_w_in", "ab_w_out"], ["sg_w_in", "sg_w_out"]]
    n_layers = 2 * L
    shards = [[big_w[layer % 2][k][0][layer // 2].astype(BF16) for k in range(2)] for layer in range(n_layers)]
    gathered = {}

    def gather_of(keys):
        keys = [key for key in keys if key[0] < n_layers and key not in gathered]
        return keys, (_Gather([shards[layer][k] for layer, k in keys]) if keys else None)

    def keep_gathered(keys, res):
        for (layer, k), g in zip(keys, res):
            gathered[(layer, k)] = g.reshape((NDEV, 1, D, g.shape[-1]) if k == 0 else (NDEV, 1, kb, D))

    keys, comm = gather_of([(0, 0)])
    keep_gathered(keys, _comm_only(comm, "ag_w_in_layer0"))

    cosf, sinf = _rope_tables(S)
    T = min(S, ATTN_TILE)
    bias = _attn_bias(T)
    norm_g = [ab_norm_g, sg_norm_full]
    w_s_t = jnp.swapaxes(sg_w_s, -1, -2)
    b_s_t = jnp.swapaxes(sg_b_s, -1, -2)

    saved = []
    x_cur, res, gate_prev = x2, None, None
    for layer in range(2 * L):
        kind, i = layer % 2, layer // 2
        tag = f"{'ab' if kind == 0 else 'sg'}{i}"
        shift, scale, gate = mod_of(kind, i)
        g = norm_g[kind][i].reshape(1, D)
        xl, h = _pre(x_cur, res, gate_prev, g, scale, shift, f"pre_{tag}")
        keys, comm = gather_of([(layer + 1, 0)])
        if comm is None:
            proj = _mm_nn_in(h, gathered[(layer, 0)], 0, f"proj_{tag}")
        else:
            proj, got = _mm_nn_in(h, gathered[(layer, 0)], 0, f"proj_{tag}", comm)
            keep_gathered(keys, got)
        rec = dict(xl=xl, h=h, proj=proj, g=g, scale=scale, gate=gate)
        keys, comm = gather_of([(layer, 1), (layer + 1, 1), (layer + 2, 1)] if kind == 0 else [])
        if kind == 0:
            qr, kr, vb = _rope_qkv(proj, cosf, sinf, W, f"rope_{tag}")
            (attn, lse), got = _attn_fwd(qr, kr, vb, bias, f"attn_{tag}", comm)
            y = _ab_mix(attn, proj, conv_full[i], W, f"mix_{tag}")
            rec.update(qr=qr, kr=kr, vb=vb, attn=attn, lse=lse)
        else:
            y, got = _sgu_fwd(proj, ln_g_full[i].reshape(1, D), ln_b_full[i].reshape(1, D), sg_w_s[i], b_s_t[i],
                              f"sgu_{tag}", comm)
        keep_gathered(keys, got)
        out = _mm_nn_out(y, gathered[(layer, 1)], 0, f"out_{tag}")
        rec.update(y=y, out=out)
        saved.append(rec)
        x_cur, res, gate_prev = xl, out, gate

    dx, loss_part, d_final_g = _loss_head(x_cur, res, gate_prev, final_norm_g.reshape(1, D), tgt, "loss_head")
    loss = lax.psum(loss_part[0, 0], ("x", "y", "c"))

    c_idx = ci.reshape(1).astype(jnp.int32)
    big_res = {}
    pending = None
    carried = None

    def sum_and_update(done, k, from_chips, part=(0, 1)):
        nm = big_names[done % 2][k]
        w, m, v = big_w[done % 2][k]
        flat = lambda a: a.reshape(L * a.shape[1], a.shape[2])
        row0 = (done // 2) * w.shape[1] + part[0] * (w.shape[1] // part[1])
        big_res[nm] = _sum_adam(from_chips, flat(w), flat(m), flat(v), row0, big_res.get(nm),
                                f"adam_{nm}{done // 2}_{part[0]}")

    def finish_layer(done, from_chips):
        for k in range(2):
            sum_and_update(done, k, from_chips[k])

    dm = [[None] * L, [None] * L]
    d_norm = [[None] * L, [None] * L]
    d_conv, d_lng, d_lnb, d_ws, d_bs = [None] * L, [None] * L, [None] * L, [None] * L, [None] * L
    for layer in reversed(range(2 * L)):
        kind, i = layer % 2, layer // 2
        tag = f"{'ab' if kind == 0 else 'sg'}{i}"
        rec = saved[layer]
        w_in_l, w_out_l = gathered[(layer, 0)], gathered[(layer, 1)]
        if carried is None:
            dout, dgate = _post_bwd(dx, rec["out"], rec["gate"], f"post_bwd_{tag}")
        else:
            dout, dgate = carried
        dy = _mm_nt_out(dout, w_out_l, 0, f"dy_{tag}")
        dwo = _mm_tn_out(rec["y"], dout, 0, 1, None, f"dwout_{tag}")
        if kind == 0:
            do, delta = _dattn_prep(dy, rec["proj"], rec["attn"], W, f"dattn_{tag}")
            comm = _ToChips(pending[1]) if pending else None
            (dqr, dkr, dvv), got = _attn_bwd(rec["qr"], rec["kr"], rec["vb"], do, rec["lse"], delta, bias,
                                             f"attn_bwd_{tag}", comm)
            if pending:
                finish_layer(pending[0], got)
            dproj, dcw = _ab_bwd(dy, rec["attn"], rec["proj"], dqr, dkr, dvv, cosf, sinf, conv_full[i], W,
                                 f"mix_bwd_{tag}")
            d_conv[i] = dcw[:3]
            earlier = None
        else:
            dproj, d_ws[i], dbs, d_lng[i], d_lnb[i] = _sgu_bwd(
                rec["proj"], dy, ln_g_full[i].reshape(1, D), ln_b_full[i].reshape(1, D),
                sg_w_s[i], w_s_t[i], b_s_t[i], f"sgu_bwd_{tag}")
            d_bs[i] = dbs.reshape(C_GROUPS, C_CHUNK)
            earlier = pending
        if layer == 0:
            h_l = rec["h"]
            dwi_a = _mm_tn_in(h_l, dproj, 0, 1, None, f"dwin_a_{tag}", part=(0, 2))
            g_a = [dwi_a.reshape(NDEV, D // 2, -1), dwo.reshape(NDEV, kb, D)]
            dwi_b, sib_a = _mm_tn_in(h_l, dproj, 0, 1, None, f"dwin_b_{tag}", _ToSibling(g_a), part=(1, 2))
            p_a = [_add_sibling(g, r, c_idx, f"rs_add_a{k}_{tag}") for k, (g, r) in enumerate(zip(g_a, sib_a))]
            g_b = [dwi_b.reshape(NDEV, D // 2, -1)]
            both = _Both(_ToSibling(g_b), _ToChips(p_a))
            dh, got = _mm_nt_in(dproj, w_in_l, 0, f"dh_{tag}", both)
            sib_b, chips_a = both.split_results(got)
            p_b = [_add_sibling(g_b[0], sib_b[0], c_idx, f"rs_add_b_{tag}")]
            (dx, dshift, dscale, d_norm[kind][i]), chips_b = _pre_bwd(
                rec["xl"], dh, dx, rec["g"], rec["scale"], f"pre_bwd_{tag}", _ToChips(p_b))
            sum_and_update(0, 0, chips_a[0], (0, 2))
            sum_and_update(0, 1, chips_a[1])
            sum_and_update(0, 0, chips_b[0], (1, 2))
            dm[kind][i] = jnp.concatenate([dshift, dscale, dgate], axis=1).reshape(3 * D)
            continue
        if earlier:
            dwi, got_in = _mm_tn_in(rec["h"], dproj, 0, 1, None, f"dwin_{tag}", _ToChips(earlier[1][:1]))
        else:
            dwi = _mm_tn_in(rec["h"], dproj, 0, 1, None, f"dwin_{tag}")
        grads = [dwi.reshape(NDEV, D, -1), dwo.reshape(NDEV, kb, D)]
        if earlier:
            both = _Both(_ToSibling(grads), _ToChips(earlier[1][1:]))
            dh, got = _mm_nt_in(dproj, w_in_l, 0, f"dh_{tag}", both)
            from_sibling, got_rest = both.split_results(got)
            sum_and_update(earlier[0], 0, got_in[0], (0, 2))
            sum_and_update(earlier[0], 0, got_rest[0], (1, 2))
            sum_and_update(earlier[0], 1, got_rest[1])
        else:
            dh, from_sibling = _mm_nt_in(dproj, w_in_l, 0, f"dh_{tag}", _ToSibling(grads))
        name_in, name_out = (f"rs_add_{nm}{i}" for nm in big_names[kind])
        if kind == 0:
            parts = [_add_sibling(grads[0], from_sibling[0], c_idx, f"{name_in}_{p}", (p, 2)) for p in range(2)]
        else:
            parts = [_add_sibling(grads[0], from_sibling[0], c_idx, name_in)]
        pending = (layer, parts + [_add_sibling(grads[1], from_sibling[1], c_idx, name_out)])
        under = saved[layer - 1]
        (dx, dshift, dscale, d_norm[kind][i], *carried), _ = _pre_bwd(
            rec["xl"], dh, dx, rec["g"], rec["scale"], f"pre_bwd_{tag}", below=(under["out"], under["gate"]))
        dm[kind][i] = jnp.concatenate([dshift, dscale, dgate], axis=1).reshape(3 * D)
    grad_x = dx.reshape(1, S, D)
    for kind in range(2):
        for k in range(2):
            nm = big_names[kind][k]
            big_res[nm] = [o.reshape(big_w[kind][k][0].shape) for o in big_res[nm]]

    stack = lambda xs: jnp.stack(xs)
    pack_items = [stack(dm[0]), stack(dm[1]), stack(d_norm[0]).reshape(L, D), stack(d_conv),
                  stack(d_norm[1]).reshape(L, D), stack(d_lng).reshape(L, D), stack(d_lnb).reshape(L, D),
                  stack(d_ws), stack(d_bs), d_final_g]
    (g3,) = _comm_only(_Gather([_pack(pack_items)]), "ag_grads")
    P = g3.shape[1] * g3.shape[2]
    tot = _sum_rows(g3, "sum_small").reshape(P)
    g3 = g3.reshape(NDEV, P)
    sizes = [int(np.prod(p.shape)) for p in pack_items]
    offs = np.concatenate([[0], np.cumsum(sizes)]).tolist()
    seg = lambda k, shape: tot[offs[k]:offs[k + 1]].reshape(shape)

    def shard(full, n):
        return lax.dynamic_slice_in_dim(full, dev * n, n, axis=full.ndim - 1)

    g_ab_b_mod = seg(0, (L, 3 * D))
    g_sg_b_mod = shard(seg(1, (L, 3 * D)), n_mod)
    g_ab_norm = seg(2, (L, D))
    g_conv = shard(seg(3, (L, 3, W)), W // NDEV)
    g_sg_norm = shard(seg(4, (L, D)), kb)
    g_ln_g = shard(seg(5, (L, D)), kb)
    g_ln_b = shard(seg(6, (L, D)), kb)
    g_w_s = seg(7, sg_w_s.shape)
    g_b_s = seg(8, sg_b_s.shape)
    g_final = seg(9, (D,))

    small_w = [("ab_norm_g", g_ab_norm, ab_norm_g, m_ab_norm_g, v_ab_norm_g),
               ("ab_b_mod", g_ab_b_mod, ab_b_mod, m_ab_b_mod, v_ab_b_mod),
               ("ab_conv_w", g_conv, ab_conv_w, m_ab_conv_w, v_ab_conv_w),
               ("sg_norm_g", g_sg_norm, sg_norm_g, m_sg_norm_g, v_sg_norm_g),
               ("sg_b_mod", g_sg_b_mod, sg_b_mod, m_sg_b_mod, v_sg_b_mod),
               ("sg_ln_g", g_ln_g, sg_ln_g, m_sg_ln_g, v_sg_ln_g),
               ("sg_ln_b", g_ln_b, sg_ln_b, m_sg_ln_b, v_sg_ln_b),
               ("sg_w_s", g_w_s, sg_w_s, m_sg_w_s, v_sg_w_s),
               ("sg_b_s", g_b_s, sg_b_s, m_sg_b_s, v_sg_b_s),
               ("final_norm_g", g_final, final_norm_g, m_final_norm_g, v_final_norm_g)]
    packed = [_pack([t[k] for t in small_w]) for k in (1, 2, 3, 4)]
    upd = _adam_only(*packed, "adam_small")
    small_res = {}
    o = 0
    for nm, g, w, _, _ in small_w:
        size = int(np.prod(w.shape))
        small_res[nm] = [g] + [u.reshape(-1)[o:o + size].reshape(w.shape) for u in upd]
        o += size

    KP = 128
    sc_t = jnp.pad((c_all * jax.nn.sigmoid(c_all)).T, ((0, 0), (0, KP - NDEV)))
    mod_res = {}
    for kind, nm, (w, m, v) in ((0, "ab_w_mod", (ab_w_mod, m_ab_w_mod, v_ab_w_mod)),
                                (1, "sg_w_mod", (sg_w_mod, m_sg_w_mod, v_sg_w_mod))):
        dm_all = g3[:, offs[kind]:offs[kind + 1]].reshape(NDEV, L, 3 * D)
        cols = jnp.pad(shard(dm_all, n_mod).transpose(1, 0, 2), ((0, 0), (0, KP - NDEV), (0, 0)))
        mod_res[nm] = _wmod_grad_adam(sc_t, cols, w, m, v, f"adam_{nm}")

    order = ["ab_norm_g", "ab_w_mod", "ab_b_mod", "ab_w_in", "ab_conv_w", "ab_w_out", "sg_norm_g", "sg_w_mod",
             "sg_b_mod", "sg_w_in", "sg_ln_g", "sg_ln_b", "sg_w_s", "sg_b_s", "sg_w_out", "final_norm_g"]
    res = {**big_res, **small_res, **mod_res}
    outs = [loss, grad_x]
    for k in range(4):
        outs += [res[nm][k] for nm in order]
    return tuple(outs)
```

```python
import functools
import math

import numpy as np
import jax
import jax.numpy as jnp
from jax import lax
from jax.experimental import pallas as pl
from jax.experimental.pallas import tpu as pltpu

F32 = jnp.float32
BF16 = jnp.bfloat16

NDEV = 8
NCHIP = 4
EPS = 1e-6
HEAD_DIM = 128
ROPE_THETA = 10000.0
DILATED_PATTERNS = ((128, 1), (512, 4), (2048, 16))
NEG_INF = -1e30
C_CHUNK = 128
C_GROUPS = 8
ADAM_LR = 0.001
ADAM_B1 = 0.9
ADAM_B2 = 0.999
ADAM_EPS = 1e-08
ADAM_WD = 0.01
ADAM_STEP = 10
GELU_K = math.sqrt(2.0 / math.pi)
GELU_C = 0.044715

VMEM_LIMIT_BYTES = 56 * 1024 * 1024
ATTN_TILE = 512
HEADS_PER_STEP = 4
ATTN_ROW_CHUNK = 256
LANES = 128
ROW_TILE = 256
MESH = pl.DeviceIdType.MESH
ANY = pl.BlockSpec(memory_space=pl.ANY)


def _cp(*sem):
    return pltpu.CompilerParams(dimension_semantics=sem, vmem_limit_bytes=VMEM_LIMIT_BYTES)


def _sigmoid(z):
    return 0.5 * (jnp.tanh(0.5 * z) + 1.0)


def _silu_and_grad(z):
    s = _sigmoid(z)
    return z * s, s * (1.0 + z * (1.0 - s))


def _gelu_and_grad(x):
    x2 = x * x
    t = jnp.tanh(GELU_K * (x + GELU_C * x2 * x))
    g = 0.5 * x * (1.0 + t)
    dg = 0.5 * (1.0 + t) + 0.5 * x * (1.0 - t * t) * (GELU_K * (1.0 + 3.0 * GELU_C * x2))
    return g, dg


def _position():
    return lax.axis_index("x"), lax.axis_index("y"), lax.axis_index("c")


def _chips(x, y):
    return [(1 - x, y), (x, 1 - y), (1 - x, 1 - y)]


class _Gather:
    def __init__(self, arrs):
        n = len(arrs)
        self.arrs = list(arrs)
        self.out_shape = [jax.ShapeDtypeStruct((NDEV,) + a.shape, a.dtype) for a in arrs]
        self.scratch = [pltpu.SemaphoreType.DMA((n, 7)), pltpu.SemaphoreType.DMA((n, 7)),
                        pltpu.SemaphoreType.DMA((n,))]

    def _copies(self, ins, outs, sems, own=True):
        send_sems, recv_sems, local_sems = sems
        x, y, c = _position()

        def copy(a, k, block, to, src=None):
            dst = outs[a].at[4 * block[0] + 2 * block[1] + block[2]]
            return pltpu.make_async_remote_copy(
                src_ref=dst if src is None else src, dst_ref=dst,
                send_sem=send_sems.at[a, k], recv_sem=recv_sems.at[a, k],
                device_id=to, device_id_type=MESH)

        n = len(ins)
        me, sibling = (x, y, c), (x, y, 1 - c)
        mine, first = [], []
        if own:
            mine = [pltpu.make_async_copy(ins[a], outs[a].at[4 * x + 2 * y + c], local_sems.at[a])
                    for a in range(n)]
            for a in range(n):
                first.append(copy(a, 0, me, sibling, src=ins[a]))
                first += [copy(a, 1 + j, me, (*chip, c), src=ins[a]) for j, chip in enumerate(_chips(x, y))]
        return copy, mine, first

    def start(self, ins, outs, sems):
        _, mine, first = self._copies(ins, outs, sems)
        for cp in mine + first:
            cp.start()

    def middle(self, ins, outs, sems):
        copy = self._copies(ins, outs, sems, own=False)[0]
        x, y, c = _position()
        me, sibling = (x, y, c), (x, y, 1 - c)
        for j, chip in enumerate(_chips(x, y)):
            for a in range(len(ins)):
                copy(a, 1 + j, (*chip, c), me).wait_recv()
                copy(a, 4 + j, (*chip, c), sibling).start()

    def finish(self, ins, outs, sems):
        copy, mine, first = self._copies(ins, outs, sems)
        x, y, c = _position()
        me, sibling = (x, y, c), (x, y, 1 - c)
        passed = [copy(a, 4 + j, (*chip, c), sibling)
                  for j, chip in enumerate(_chips(x, y)) for a in range(len(ins))]
        for a in range(len(ins)):
            copy(a, 0, sibling, me).wait_recv()
            for j, chip in enumerate(_chips(x, y)):
                copy(a, 4 + j, (*chip, 1 - c), me).wait_recv()
        for cp in first + passed:
            cp.wait_send()
        for cp in mine:
            cp.wait()


class _GatherDirect:
    def __init__(self, arrs):
        n = len(arrs)
        self.arrs = list(arrs)
        self.out_shape = [jax.ShapeDtypeStruct((NDEV,) + a.shape, a.dtype) for a in arrs]
        self.scratch = [pltpu.SemaphoreType.DMA((n, 7)), pltpu.SemaphoreType.DMA((n, 7)),
                        pltpu.SemaphoreType.DMA((n,))]

    def _copies(self, ins, outs, sems, arrivals):
        send_sems, recv_sems, local_sems = sems
        x, y, c = _position()
        mine = [pltpu.make_async_copy(ins[a], outs[a].at[4 * x + 2 * y + c], local_sems.at[a])
                for a in range(len(ins))]
        sends, recvs = [], []
        for a in range(len(ins)):
            for k in range(1, NDEV):
                px = 1 - x if k & 4 else x
                py = 1 - y if k & 2 else y
                pc = 1 - c if k & 1 else c
                sends.append(pltpu.make_async_remote_copy(
                    src_ref=ins[a], dst_ref=outs[a].at[4 * x + 2 * y + c],
                    send_sem=send_sems.at[a, k - 1], recv_sem=recv_sems.at[a, k - 1],
                    device_id=(px, py, pc), device_id_type=MESH))
                if arrivals:
                    slot = outs[a].at[4 * px + 2 * py + pc]
                    recvs.append(pltpu.make_async_remote_copy(
                        src_ref=slot, dst_ref=slot, send_sem=send_sems.at[a, k - 1], recv_sem=recv_sems.at[a, k - 1],
                        device_id=(px, py, pc), device_id_type=MESH))
        return mine, sends, recvs

    def start(self, ins, outs, sems):
        mine, sends, _ = self._copies(ins, outs, sems, False)
        for cp in mine + sends:
            cp.start()

    def finish(self, ins, outs, sems):
        mine, sends, recvs = self._copies(ins, outs, sems, True)
        for cp in recvs:
            cp.wait_recv()
        for cp in sends:
            cp.wait_send()
        for cp in mine:
            cp.wait()


class _ToSibling:
    def __init__(self, gs):
        n = len(gs)
        self.arrs = list(gs)
        self.out_shape = [jax.ShapeDtypeStruct((NCHIP,) + g.shape[1:], g.dtype) for g in gs]
        self.scratch = [pltpu.SemaphoreType.DMA((n, NCHIP)), pltpu.SemaphoreType.DMA((n, NCHIP))]

    def _copies(self, ins, outs, sems):
        send_sems, recv_sems = sems
        x, y, c = _position()
        return [pltpu.make_async_remote_copy(
            src_ref=ins[a].at[2 * k + (1 - c)], dst_ref=outs[a].at[k],
            send_sem=send_sems.at[a, k], recv_sem=recv_sems.at[a, k],
            device_id=(x, y, 1 - c), device_id_type=MESH) for a in range(len(ins)) for k in range(NCHIP)]

    def start(self, ins, outs, sems):
        for cp in self._copies(ins, outs, sems):
            cp.start()

    def finish(self, ins, outs, sems):
        copies = self._copies(ins, outs, sems)
        for cp in copies:
            cp.wait_recv()
        for cp in copies:
            cp.wait_send()


class _ToChips:
    def __init__(self, ps):
        n = len(ps)
        self.arrs = list(ps)
        self.out_shape = [jax.ShapeDtypeStruct(p.shape, p.dtype) for p in ps]
        self.scratch = [pltpu.SemaphoreType.DMA((n, 3)), pltpu.SemaphoreType.DMA((n, 3)),
                        pltpu.SemaphoreType.DMA((n,))]

    def _copies(self, ins, outs, sems, arrivals):
        send_sems, recv_sems, local_sems = sems
        x, y, c = _position()
        mychip = 2 * x + y
        n = len(ins)
        mine = [pltpu.make_async_copy(ins[a].at[mychip], outs[a].at[mychip], local_sems.at[a]) for a in range(n)]
        sends, recvs = [], []
        for a in range(n):
            for j, chip in enumerate(_chips(x, y)):
                sends.append(pltpu.make_async_remote_copy(
                    src_ref=ins[a].at[2 * chip[0] + chip[1]], dst_ref=outs[a].at[mychip],
                    send_sem=send_sems.at[a, j], recv_sem=recv_sems.at[a, j],
                    device_id=(*chip, c), device_id_type=MESH))
                if arrivals:
                    slot = outs[a].at[2 * chip[0] + chip[1]]
                    recvs.append(pltpu.make_async_remote_copy(
                        src_ref=slot, dst_ref=slot, send_sem=send_sems.at[a, j], recv_sem=recv_sems.at[a, j],
                        device_id=(*chip, c), device_id_type=MESH))
        return mine, sends, recvs

    def start(self, ins, outs, sems):
        mine, sends, _ = self._copies(ins, outs, sems, False)
        for cp in mine + sends:
            cp.start()

    def finish(self, ins, outs, sems):
        mine, sends, recvs = self._copies(ins, outs, sems, True)
        for cp in recvs:
            cp.wait_recv()
        for cp in sends:
            cp.wait_send()
        for cp in mine:
            cp.wait()


HOSTED_MIDDLE_AT = 0.95


def _middle_of(comm, ins, outs, sems):
    if hasattr(comm, "middle"):
        comm.middle(ins, outs, sems)


class _Both:
    def __init__(self, first, second):
        self.parts = (first, second)
        self.arrs = first.arrs + second.arrs
        self.out_shape = first.out_shape + second.out_shape
        self.scratch = first.scratch + second.scratch

    def _split(self, ins, outs, sems):
        a, _ = self.parts
        ni, no, ns = len(a.arrs), len(a.out_shape), len(a.scratch)
        return (ins[:ni], outs[:no], sems[:ns]), (ins[ni:], outs[no:], sems[ns:])

    def start(self, ins, outs, sems):
        for part, refs in zip(self.parts, self._split(ins, outs, sems)):
            part.start(*refs)

    def middle(self, ins, outs, sems):
        for part, refs in zip(self.parts, self._split(ins, outs, sems)):
            _middle_of(part, *refs)

    def finish(self, ins, outs, sems):
        for part, refs in zip(self.parts, self._split(ins, outs, sems)):
            part.finish(*refs)

    def split_results(self, res):
        no = len(self.parts[0].out_shape)
        return res[:no], res[no:]


def _comm_only(comm, name):
    n_in, n_out = len(comm.arrs), len(comm.out_shape)

    def body(*refs):
        ins, outs, sems = refs[:n_in], refs[n_in:n_in + n_out], refs[n_in + n_out:]
        comm.start(ins, outs, sems)
        _middle_of(comm, ins, outs, sems)
        comm.finish(ins, outs, sems)

    return pl.pallas_call(
        body, name=name, out_shape=comm.out_shape, in_specs=[ANY] * n_in, out_specs=[ANY] * n_out,
        scratch_shapes=comm.scratch,
    )(*comm.arrs)


def _hosted_call(body, operands, *, name, grid, in_specs, out_specs, out_shape, scratch_shapes=(), sem=(),
                 aliases=None, comm=None):
    single = not isinstance(out_shape, (list, tuple))
    o_specs = [out_specs] if single else list(out_specs)
    o_shape = [out_shape] if single else list(out_shape)
    n_in, n_out, n_scr = len(in_specs), len(o_shape), len(scratch_shapes)
    if comm is None:
        res = pl.pallas_call(body, name=name, grid=grid, in_specs=list(in_specs), out_specs=o_specs,
                             out_shape=o_shape, scratch_shapes=list(scratch_shapes),
                             input_output_aliases=aliases or {}, compiler_params=_cp(*sem))(*operands)
        return (res[0] if single else res), []
    c_in, c_out = len(comm.arrs), len(comm.out_shape)

    def wrapped(*refs):
        ins, cins = refs[:n_in], refs[n_in:n_in + c_in]
        o0 = n_in + c_in
        outs, couts = refs[o0:o0 + n_out], refs[o0 + n_out:o0 + n_out + c_out]
        s0 = o0 + n_out + c_out
        scr, csems = refs[s0:s0 + n_scr], refs[s0 + n_scr:]
        pids = [pl.program_id(a) for a in range(len(grid))]
        step = functools.reduce(lambda acc, pg: acc * pg[1] + pg[0], zip(pids, grid), 0)
        total = int(np.prod(grid))
        late = min(total - 1, max(1, int(total * HOSTED_MIDDLE_AT)))

        @pl.when(step == 0)
        def _():
            comm.start(cins, couts, csems)

        @pl.when(step == late)
        def _():
            _middle_of(comm, cins, couts, csems)

        body(*ins, *outs, *scr)

        @pl.when(step == total - 1)
        def _():
            comm.finish(cins, couts, csems)

    res = pl.pallas_call(
        wrapped, name=name, grid=grid, in_specs=list(in_specs) + [ANY] * c_in, out_specs=o_specs + [ANY] * c_out,
        out_shape=o_shape + comm.out_shape, scratch_shapes=list(scratch_shapes) + comm.scratch,
        input_output_aliases=aliases or {}, compiler_params=_cp(*(["arbitrary"] * len(grid))),
    )(*operands, *comm.arrs)
    return (res[0] if single else res[:n_out]), res[n_out:]


def _adamw(w, g, m, v):
    m2 = ADAM_B1 * m + (1.0 - ADAM_B1) * g
    v2 = ADAM_B2 * v + (1.0 - ADAM_B2) * (g * g)
    m_hat = m2 / (1.0 - ADAM_B1 ** ADAM_STEP)
    v_hat = v2 / (1.0 - ADAM_B2 ** ADAM_STEP)
    delta = -ADAM_LR * (m_hat / (jnp.sqrt(v_hat) + ADAM_EPS) + ADAM_WD * w)
    return delta, m2, v2


def _add_sibling(g, recv, c_idx, name, part=(0, 1)):
    _, R, C = g.shape
    R = R // part[1]
    tr = min(R, 512)
    first = part[0] * (R // tr)

    def body(c_ref, g_ref, r_ref, o_ref):
        o_ref[...] = (g_ref[...] + r_ref[...]).astype(BF16)

    return pl.pallas_call(
        body, name=name,
        grid_spec=pltpu.PrefetchScalarGridSpec(
            num_scalar_prefetch=1, grid=(NCHIP, R // tr),
            in_specs=[pl.BlockSpec((1, tr, C), lambda k, i, c_ref: (2 * k + c_ref[0], first + i, 0)),
                      pl.BlockSpec((1, tr, C), lambda k, i, c_ref: (k, first + i, 0))],
            out_specs=pl.BlockSpec((1, tr, C), lambda k, i, c_ref: (k, i, 0))),
        out_shape=jax.ShapeDtypeStruct((NCHIP, R, C), BF16),
        compiler_params=_cp("parallel", "parallel"),
    )(c_idx, g, recv)


def _sum_adam(parts, w, m, v, row0, prev, name):
    K, R, C = parts.shape
    LR = w.shape[0]
    tr = min(R, 256)
    nb = R // tr
    first = row0 // tr

    def body(p_ref, w_ref, m_ref, v_ref, *rest):
        g_ref, d_ref, m2_ref, v2_ref = rest[-4:]
        g = p_ref[0].astype(F32)
        for k in range(1, K):
            g = g + p_ref[k].astype(F32)
        delta, m2, v2 = _adamw(w_ref[...], g, m_ref[...], v_ref[...])
        g_ref[...] = g
        d_ref[...] = delta
        m2_ref[...] = m2
        v2_ref[...] = v2

    blk = pl.BlockSpec((tr, C), lambda i: (first + i, 0))
    shp = jax.ShapeDtypeStruct((LR, C), F32)
    operands = [parts, w, m, v] + (list(prev) if prev is not None else [])
    return pl.pallas_call(
        body, name=name, grid=(nb,),
        in_specs=[pl.BlockSpec((K, tr, C), lambda i: (0, i, 0)), blk, blk, blk] + [ANY] * (len(operands) - 4),
        out_specs=[blk] * 4, out_shape=[shp] * 4,
        input_output_aliases={4 + k: k for k in range(len(operands) - 4)},
        compiler_params=_cp("parallel"),
    )(*operands)


def _sum_rows(parts, name):
    K, R, C = parts.shape
    tr = min(R, 256)
    while R % tr:
        tr //= 2

    def body(p_ref, o_ref):
        g = p_ref[0]
        for k in range(1, K):
            g = g + p_ref[k]
        o_ref[...] = g

    return pl.pallas_call(
        body, name=name, grid=(R // tr,),
        in_specs=[pl.BlockSpec((K, tr, C), lambda i: (0, i, 0))],
        out_specs=pl.BlockSpec((tr, C), lambda i: (i, 0)),
        out_shape=jax.ShapeDtypeStruct((R, C), F32),
        compiler_params=_cp("parallel"),
    )(parts)


def _adam_only(g, w, m, v, name):
    R, C = g.shape
    tr = min(R, 256)
    while R % tr:
        tr //= 2

    def body(g_ref, w_ref, m_ref, v_ref, d_ref, m2_ref, v2_ref):
        delta, m2, v2 = _adamw(w_ref[...], g_ref[...], m_ref[...], v_ref[...])
        d_ref[...] = delta
        m2_ref[...] = m2
        v2_ref[...] = v2

    blk = pl.BlockSpec((tr, C), lambda i: (i, 0))
    shp = jax.ShapeDtypeStruct((R, C), F32)
    return pl.pallas_call(
        body, name=name, grid=(R // tr,), in_specs=[blk] * 4, out_specs=[blk] * 3,
        out_shape=[shp] * 3, compiler_params=_cp("parallel"),
    )(g, w, m, v)


def _mod_fwd(c_all, w_mod, b_cols, name):
    L, D, n = w_mod.shape
    B = c_all.shape[0]

    def body(c_ref, w_ref, b_ref, o_ref):
        cv = c_ref[...]
        sc = (cv * _sigmoid(cv)).astype(BF16)
        o_ref[0] = jnp.dot(sc, w_ref[0].astype(BF16), preferred_element_type=F32) + b_ref[0]

    return pl.pallas_call(
        body, name=name, grid=(L,),
        in_specs=[pl.BlockSpec((B, D), lambda l: (0, 0)),
                  pl.BlockSpec((1, D, n), lambda l: (l, 0, 0)),
                  pl.BlockSpec((1, 1, n), lambda l: (l, 0, 0))],
        out_specs=pl.BlockSpec((1, B, n), lambda l: (l, 0, 0)),
        out_shape=jax.ShapeDtypeStruct((L, B, n), F32),
        compiler_params=_cp("parallel"),
    )(c_all, w_mod, b_cols)


def _wmod_grad_adam(sc_t, dm, w, m, v, name):
    L, D, n = w.shape
    KP = sc_t.shape[1]
    tr = min(D, 512)

    def body(s_ref, dm_ref, w_ref, m_ref, v_ref, g_ref, d_ref, m2_ref, v2_ref):
        g = jnp.dot(s_ref[...], dm_ref[0], preferred_element_type=F32,
                    precision=lax.Precision.HIGHEST)
        delta, m2, v2 = _adamw(w_ref[0], g, m_ref[0], v_ref[0])
        g_ref[0] = g
        d_ref[0] = delta
        m2_ref[0] = m2
        v2_ref[0] = v2

    blk = pl.BlockSpec((1, tr, n), lambda l, i: (l, i, 0))
    shp = jax.ShapeDtypeStruct((L, D, n), F32)
    return pl.pallas_call(
        body, name=name, grid=(L, D // tr),
        in_specs=[pl.BlockSpec((tr, KP), lambda l, i: (i, 0)),
                  pl.BlockSpec((1, KP, n), lambda l, i: (l, 0, 0)), blk, blk, blk],
        out_specs=[blk] * 4, out_shape=[shp] * 4,
        compiler_params=_cp("parallel", "parallel"),
    )(sc_t, dm, w, m, v)


def _vec_spec(D):
    return pl.BlockSpec((1, D), lambda i: (0, 0))


def _pre(x, res, gate, g, scale, shift, name):
    S, D = x.shape
    tr = min(S, ROW_TILE)
    has_res = res is not None
    row = pl.BlockSpec((tr, D), lambda i: (i, 0))

    def body(*refs):
        if has_res:
            x_ref, r_ref, gate_ref, g_ref, sc_ref, sh_ref, xl_ref, h_ref = refs
            xv = x_ref[...] + gate_ref[...] * r_ref[...]
            xl_ref[...] = xv
        else:
            x_ref, g_ref, sc_ref, sh_ref, h_ref = refs
            xv = x_ref[...]
        r = lax.rsqrt(jnp.mean(xv * xv, axis=-1, keepdims=True) + EPS)
        y = (xv * r) * g_ref[...]
        h_ref[...] = (y * (1.0 + sc_ref[...]) + sh_ref[...]).astype(BF16)

    vec = _vec_spec(D)
    if has_res:
        xl, h = pl.pallas_call(
            body, name=name, grid=(S // tr,),
            in_specs=[row, row, vec, vec, vec, vec], out_specs=[row, row],
            out_shape=[jax.ShapeDtypeStruct((S, D), F32), jax.ShapeDtypeStruct((S, D), BF16)],
            compiler_params=_cp("parallel"),
        )(x, res, gate, g, scale, shift)
        return xl, h
    h = pl.pallas_call(
        body, name=name, grid=(S // tr,),
        in_specs=[row, vec, vec, vec], out_specs=row,
        out_shape=jax.ShapeDtypeStruct((S, D), BF16),
        compiler_params=_cp("parallel"),
    )(x, g, scale, shift)
    return x, h


def _pre_bwd(xl, dh, dx_in, g, scale, name, comm=None, below=None):
    S, D = xl.shape
    tr = min(S, ROW_TILE)
    nsteps = S // tr
    row = pl.BlockSpec((tr, D), lambda i: (i, 0))
    vec = _vec_spec(D)

    def body(*refs):
        if below is None:
            x_ref, dh_ref, dxin_ref, g_ref, sc_ref, dx_ref, dsh_ref, dsc_ref, dg_ref, acc_sh, acc_t = refs
        else:
            (x_ref, dh_ref, dxin_ref, g_ref, sc_ref, o2_ref, gate2_ref,
             dx_ref, dsh_ref, dsc_ref, dg_ref, do2_ref, dgate2_ref, acc_sh, acc_t, acc_g2) = refs
        i = pl.program_id(0)
        xv = x_ref[...]
        dh = dh_ref[...]
        r = lax.rsqrt(jnp.mean(xv * xv, axis=-1, keepdims=True) + EPS)
        xn = xv * r
        part_sh = jnp.sum(dh.reshape(tr // 8, 8, D), axis=0)
        part_t = jnp.sum((dh * xn).reshape(tr // 8, 8, D), axis=0)

        @pl.when(i == 0)
        def _():
            acc_sh[...] = part_sh
            acc_t[...] = part_t

        @pl.when(i > 0)
        def _():
            acc_sh[...] += part_sh
            acc_t[...] += part_t

        dxn = dh * (g_ref[...] * (1.0 + sc_ref[...]))
        dxv = dxin_ref[...] + r * (dxn - xn * jnp.mean(dxn * xn, axis=-1, keepdims=True))
        dx_ref[...] = dxv
        if below is not None:
            do2_ref[...] = (dxv * gate2_ref[...]).astype(BF16)
            part_g2 = jnp.sum((dxv * o2_ref[...]).reshape(tr // 8, 8, D), axis=0)

            @pl.when(i == 0)
            def _():
                acc_g2[...] = part_g2

            @pl.when(i > 0)
            def _():
                acc_g2[...] += part_g2

        @pl.when(i == nsteps - 1)
        def _():
            t = jnp.sum(acc_t[...], axis=0, keepdims=True)
            dsh_ref[...] = jnp.sum(acc_sh[...], axis=0, keepdims=True)
            dsc_ref[...] = t * g_ref[...]
            dg_ref[...] = t * (1.0 + sc_ref[...])
            if below is not None:
                dgate2_ref[...] = jnp.sum(acc_g2[...], axis=0, keepdims=True)

    v = jax.ShapeDtypeStruct((1, D), F32)
    operands, in_specs = [xl, dh, dx_in, g, scale], [row, row, row, vec, vec]
    out_specs, out_shape = [row, vec, vec, vec], [jax.ShapeDtypeStruct((S, D), F32), v, v, v]
    scratch = [pltpu.VMEM((8, D), F32), pltpu.VMEM((8, D), F32)]
    if below is not None:
        operands, in_specs = operands + list(below), in_specs + [row, vec]
        out_specs, out_shape = out_specs + [row, vec], out_shape + [jax.ShapeDtypeStruct((S, D), BF16), v]
        scratch = scratch + [pltpu.VMEM((8, D), F32)]
    return _hosted_call(
        body, operands, name=name, grid=(nsteps,), in_specs=in_specs, out_specs=out_specs,
        out_shape=out_shape, scratch_shapes=scratch, sem=("arbitrary",), comm=comm)


def _post_bwd(dx, out, gate, name):
    S, D = dx.shape
    tr = min(S, ROW_TILE)
    nsteps = S // tr
    row = pl.BlockSpec((tr, D), lambda i: (i, 0))
    vec = _vec_spec(D)

    def body(dx_ref, o_ref, gate_ref, do_ref, dg_ref, acc):
        i = pl.program_id(0)
        dxv = dx_ref[...]
        do_ref[...] = (dxv * gate_ref[...]).astype(BF16)
        part = jnp.sum((dxv * o_ref[...]).reshape(tr // 8, 8, D), axis=0)

        @pl.when(i == 0)
        def _():
            acc[...] = part

        @pl.when(i > 0)
        def _():
            acc[...] += part

        @pl.when(i == nsteps - 1)
        def _():
            dg_ref[...] = jnp.sum(acc[...], axis=0, keepdims=True)

    return pl.pallas_call(
        body, name=name, grid=(nsteps,),
        in_specs=[row, row, vec], out_specs=[row, vec],
        out_shape=[jax.ShapeDtypeStruct((S, D), BF16), jax.ShapeDtypeStruct((1, D), F32)],
        scratch_shapes=[pltpu.VMEM((8, D), F32)],
        compiler_params=_cp("arbitrary"),
    )(dx, out, gate)


def _loss_head(x, res, gate, gf, tgt, name):
    S, D = x.shape
    tr = min(S, ROW_TILE)
    nsteps = S // tr
    row = pl.BlockSpec((tr, D), lambda i: (i, 0))
    vec = _vec_spec(D)

    def body(x_ref, r_ref, gate_ref, gf_ref, t_ref, dx_ref, loss_ref, dgf_ref, do_ref, dgate_ref, acc, lacc, acc_g):
        i = pl.program_id(0)
        xv = x_ref[...] + gate_ref[...] * r_ref[...]
        r = lax.rsqrt(jnp.mean(xv * xv, axis=-1, keepdims=True) + EPS)
        xn = xv * r
        err = xn * gf_ref[...] - t_ref[...]
        row_loss = jnp.mean(err * err, axis=-1, keepdims=True)
        lpart = 0.5 * jnp.sum(row_loss, axis=0, keepdims=True)
        dy = err * (1.0 / D)
        part = jnp.sum((dy * xn).reshape(tr // 8, 8, D), axis=0)

        @pl.when(i == 0)
        def _():
            acc[...] = part
            lacc[...] = lpart

        @pl.when(i > 0)
        def _():
            acc[...] += part
            lacc[...] += lpart

        dxn = dy * gf_ref[...]
        dxv = r * (dxn - xn * jnp.mean(dxn * xn, axis=-1, keepdims=True))
        dx_ref[...] = dxv
        do_ref[...] = (dxv * gate_ref[...]).astype(BF16)
        part_g = jnp.sum((dxv * r_ref[...]).reshape(tr // 8, 8, D), axis=0)

        @pl.when(i == 0)
        def _():
            acc_g[...] = part_g

        @pl.when(i > 0)
        def _():
            acc_g[...] += part_g

        @pl.when(i == nsteps - 1)
        def _():
            dgf_ref[...] = jnp.sum(acc[...], axis=0, keepdims=True)
            loss_ref[...] = lacc[...]
            dgate_ref[...] = jnp.sum(acc_g[...], axis=0, keepdims=True)

    return pl.pallas_call(
        body, name=name, grid=(nsteps,),
        in_specs=[row, row, vec, vec, row],
        out_specs=[row, pl.BlockSpec((1, 1), lambda i: (0, 0)), vec, row, vec],
        out_shape=[jax.ShapeDtypeStruct((S, D), F32), jax.ShapeDtypeStruct((1, 1), F32),
                   jax.ShapeDtypeStruct((1, D), F32), jax.ShapeDtypeStruct((S, D), BF16),
                   jax.ShapeDtypeStruct((1, D), F32)],
        scratch_shapes=[pltpu.VMEM((8, D), F32), pltpu.VMEM((1, 1), F32), pltpu.VMEM((8, D), F32)],
        compiler_params=_cp("arbitrary"),
    )(x, res, gate, gf, tgt)


NN = (((1,), (0,)), ((), ()))
NT = (((1,), (1,)), ((), ()))
TN = (((0,), (0,)), ((), ()))


def _mm(name, a, b, out_shape, grid, a_spec, b_spec, o_spec, dims, a2d, b2d, k_axis, sem, alias=None, comm=None):
    def body(*refs):
        a_ref, b_ref, o_ref = refs[0], refs[1], refs[-1]
        r = lax.dot_general(a_ref[...].reshape(a2d), b_ref[...].reshape(b2d), dims,
                            preferred_element_type=F32)
        r = r.reshape(o_ref.shape)
        if k_axis is None:
            o_ref[...] = r.astype(o_ref.dtype)
        else:
            k = pl.program_id(k_axis)

            @pl.when(k == 0)
            def _():
                o_ref[...] = r

            @pl.when(k > 0)
            def _():
                o_ref[...] += r

    operands, in_specs, aliases = [a, b], [a_spec, b_spec], {}
    if alias is not None:
        operands.append(alias)
        in_specs.append(ANY)
        aliases = {2: 0}
    res, extra = _hosted_call(body, operands, name=name, grid=grid, in_specs=in_specs, out_specs=o_spec,
                              out_shape=out_shape, sem=sem, aliases=aliases, comm=comm)
    return res if comm is None else (res, extra)


def _tile(n, pref):
    t = min(n, pref)
    while n % t:
        t -= 128
    return t


def _mm_nn_in(a, w, l, name, comm=None):
    M, K = a.shape
    _, _, _, n = w.shape
    tm, tn = min(M, 512), _tile(n, 1024)
    nb = n // tn
    return _mm(name, a, w, jax.ShapeDtypeStruct((M, NDEV * n), F32), (NDEV * nb, M // tm),
               pl.BlockSpec((tm, K), lambda j, i: (i, 0)),
               pl.BlockSpec((1, 1, K, tn), lambda j, i: (j // nb, l, 0, j % nb)),
               pl.BlockSpec((tm, tn), lambda j, i: (i, j)),
               NN, (tm, K), (K, tn), None, ("parallel", "parallel"), comm=comm)


def _mm_nn_out(a, w, l, name):
    M, K = a.shape
    _, _, kb, N = w.shape
    tm, tn = min(M, 512), _tile(N, 1024)
    return _mm(name, a, w, jax.ShapeDtypeStruct((M, N), F32), (N // tn, M // tm),
               pl.BlockSpec((tm, K), lambda j, i: (i, 0)),
               pl.BlockSpec((NDEV, 1, kb, tn), lambda j, i: (0, l, 0, j)),
               pl.BlockSpec((tm, tn), lambda j, i: (i, j)),
               NN, (tm, K), (K, tn), None, ("parallel", "parallel"))


def _mm_nt_in(a, w, l, name, comm=None):
    M, _ = a.shape
    _, _, K, n = w.shape
    tm, tk = min(M, 1024), _tile(K, 1024)
    gb = 2 if n <= 1024 else 1

    def body(a_ref, w_ref, o_ref):
        k = pl.program_id(2)
        r = lax.dot_general(a_ref[:, :n], w_ref[0, 0], NT, preferred_element_type=F32)
        for g in range(1, gb):
            r = r + lax.dot_general(a_ref[:, g * n:(g + 1) * n], w_ref[g, 0], NT, preferred_element_type=F32)

        @pl.when(k == 0)
        def _():
            o_ref[...] = r

        @pl.when(k > 0)
        def _():
            o_ref[...] += r

    res, extra = _hosted_call(
        body, [a, w], name=name, grid=(M // tm, K // tk, NDEV // gb),
        in_specs=[pl.BlockSpec((tm, gb * n), lambda i, j, k: (i, k)),
                  pl.BlockSpec((gb, 1, tk, n), lambda i, j, k: (k, l, j, 0))],
        out_specs=pl.BlockSpec((tm, tk), lambda i, j, k: (i, j)),
        out_shape=jax.ShapeDtypeStruct((M, K), F32),
        sem=("parallel", "parallel", "arbitrary"), comm=comm)
    return res if comm is None else (res, extra)


def _mm_nt_out(a, w, l, name):
    M, N = a.shape
    _, _, kb, _ = w.shape
    K = NDEV * kb
    tm, tk, tc = min(M, 1024), _tile(K, 1024), _tile(N, 2048)
    per = tk // kb
    return _mm(name, a, w, jax.ShapeDtypeStruct((M, K), F32), (M // tm, K // tk, N // tc),
               pl.BlockSpec((tm, tc), lambda i, j, k: (i, k)),
               pl.BlockSpec((per, 1, kb, tc), lambda i, j, k: (j, l, 0, k)),
               pl.BlockSpec((tm, tk), lambda i, j, k: (i, j)),
               NT, (tm, tc), (tk, tc), 2, ("parallel", "parallel", "arbitrary"))


def _mm_tn_in(a, b, l, L, buf, name, comm=None, part=(0, 1)):
    S, K = a.shape
    K = K // part[1]
    n = b.shape[1] // NDEV
    ts, tk, tn = min(S, 2048), _tile(K, 1024), _tile(n, 1024)
    nb = n // tn
    first = part[0] * (K // tk)
    return _mm(name, a, b, jax.ShapeDtypeStruct((NDEV, L, K, n), F32), (NDEV * nb, K // tk, S // ts),
               pl.BlockSpec((ts, tk), lambda j, i, s: (s, first + i)),
               pl.BlockSpec((ts, tn), lambda j, i, s: (s, j)),
               pl.BlockSpec((1, 1, tk, tn), lambda j, i, s: (j // nb, l, i, j % nb)),
               TN, (ts, tk), (ts, tn), 2, ("parallel", "parallel", "arbitrary"), alias=buf, comm=comm)


def _mm_tn_out(a, b, l, L, buf, name):
    S, K = a.shape
    N = b.shape[1]
    kb = K // NDEV
    ts, tk, tn = min(S, 2048), _tile(K, 1024), _tile(N, 1024)
    per = tk // kb
    return _mm(name, a, b, jax.ShapeDtypeStruct((NDEV, L, kb, N), F32), (N // tn, K // tk, S // ts),
               pl.BlockSpec((ts, tk), lambda j, i, s: (s, i)),
               pl.BlockSpec((ts, tn), lambda j, i, s: (s, j)),
               pl.BlockSpec((per, 1, kb, tn), lambda j, i, s: (i, l, 0, j)),
               TN, (ts, tk), (ts, tn), 2, ("parallel", "parallel", "arbitrary"), alias=buf)


def _attn_bias(T):
    reach = max(w // 2 for w, _ in DILATED_PATTERNS)
    hb = -(-reach // T)
    i = np.arange(T)[:, None]
    j = np.arange(T)[None, :]
    tiles = []
    for d in range(-hb, hb + 1):
        rel = j + d * T - i
        mult = np.zeros((T, T), np.float64)
        for window, dil in DILATED_PATTERNS:
            radius = window // (2 * dil)
            mult += (rel % dil == 0) & (np.abs(rel) <= radius * dil)
        tiles.append(np.where(mult > 0, np.log(np.maximum(mult, 1.0)), NEG_INF))
    return jnp.asarray(np.stack(tiles), F32)


def _rope_tables(S):
    half = HEAD_DIM // 2
    pos = jnp.arange(S, dtype=F32)
    inv = ROPE_THETA ** (-jnp.arange(half, dtype=F32) / half)
    ang = pos[:, None] * inv[None, :]
    cos, sin = jnp.cos(ang), jnp.sin(ang)
    return jnp.concatenate([cos, cos], axis=-1), jnp.concatenate([-sin, sin], axis=-1)


def _rope_apply(t, cosf, sinf, heads, sign):
    outs = []
    for hh in range(heads):
        th = t[:, hh * HEAD_DIM:(hh + 1) * HEAD_DIM]
        outs.append(th * cosf + sign * (pltpu.roll(th, HEAD_DIM // 2, 1) * sinf))
    return outs


def _rope_qkv(proj, cosf, sinf, W, name):
    S = proj.shape[0]
    tr = min(S, ROW_TILE)
    heads = W // HEAD_DIM

    def body(q_ref, k_ref, v_ref, c_ref, s_ref, qo_ref, ko_ref, vo_ref):
        cosf_v, sinf_v = c_ref[...], s_ref[...]
        for src, dst, mult in ((q_ref, qo_ref, HEAD_DIM ** -0.5), (k_ref, ko_ref, 1.0)):
            for hh, val in enumerate(_rope_apply(src[...], cosf_v, sinf_v, heads, 1.0)):
                dst[:, hh * HEAD_DIM:(hh + 1) * HEAD_DIM] = (val * mult).astype(BF16)
        vo_ref[...] = v_ref[...].astype(BF16)

    piece = lambda p: pl.BlockSpec((tr, W), lambda i: (i, p))
    tab = pl.BlockSpec((tr, HEAD_DIM), lambda i: (i, 0))
    out = pl.BlockSpec((tr, W), lambda i: (i, 0))
    shp = jax.ShapeDtypeStruct((S, W), BF16)
    return pl.pallas_call(
        body, name=name, grid=(S // tr,),
        in_specs=[piece(0), piece(1), piece(2), tab, tab], out_specs=[out] * 3, out_shape=[shp] * 3,
        compiler_params=_cp("parallel"),
    )(proj, proj, proj, cosf, sinf)


def _attn_fwd(q, k, v, bias, name, comm=None):
    S, W = q.shape
    H = W // HEAD_DIM
    nd, T, _ = bias.shape
    hb, nq = nd // 2, S // T
    scale = HEAD_DIM ** -0.5
    hp = min(H, HEADS_PER_STEP)
    rc = min(T, ATTN_ROW_CHUNK)
    wp = hp * HEAD_DIM

    def body(q_ref, k_ref, v_ref, b_ref, o_ref, lse_ref, m_s, l_s, acc_s):
        i, d = pl.program_id(1), pl.program_id(2)
        j = i + d - hb

        @pl.when(d == 0)
        def _():
            m_s[...] = jnp.full(m_s.shape, -jnp.inf, F32)
            l_s[...] = jnp.zeros(l_s.shape, F32)
            acc_s[...] = jnp.zeros(acc_s.shape, F32)

        @pl.when((j >= 0) & (j < nq))
        def _():
            items = [(hh, c) for hh in range(hp) for c in range(T // rc)]

            def scores(item):
                hh, c = item
                cols, rows = slice(hh * HEAD_DIM, (hh + 1) * HEAD_DIM), slice(c * rc, (c + 1) * rc)
                return (lax.dot_general(q_ref[rows, cols], k_ref[:, cols], NT, preferred_element_type=F32)
                        + b_ref[d, rows, :])

            def weighted_values(item, p, alpha):
                hh, c = item
                cols, rows = slice(hh * HEAD_DIM, (hh + 1) * HEAD_DIM), slice(c * rc, (c + 1) * rc)
                acc_s[rows, cols] = alpha * acc_s[rows, cols] + jnp.dot(p, v_ref[:, cols],
                                                                        preferred_element_type=F32)

            s_next, pending = scores(items[0]), None
            for n, (hh, c) in enumerate(items):
                rows = slice(c * rc, (c + 1) * rc)
                s = s_next
                if n + 1 < len(items):
                    s_next = scores(items[n + 1])
                if pending is not None:
                    weighted_values(*pending)
                parts = [s[:, t * LANES:(t + 1) * LANES] for t in range(T // LANES)]
                m_old = m_s[hh, rows, :]
                m_cur = jnp.max(functools.reduce(jnp.maximum, parts), axis=1, keepdims=True)
                m_new = jnp.maximum(m_old, m_cur)
                alpha = jnp.exp(m_old - m_new)
                ps = [jnp.exp(part - m_new) for part in parts]
                l_s[hh, rows, :] = alpha * l_s[hh, rows, :] + functools.reduce(jnp.add, ps)
                m_s[hh, rows, :] = m_new
                pending = ((hh, c), jnp.concatenate(ps, axis=1).astype(BF16), alpha)
            weighted_values(*pending)

        @pl.when(d == nd - 1)
        def _():
            for hh in range(hp):
                cols = slice(hh * HEAD_DIM, (hh + 1) * HEAD_DIM)
                l = jnp.sum(l_s[hh], axis=1, keepdims=True)
                o_ref[:, cols] = acc_s[:, cols] / l
                lse_ref[hh] = m_s[hh][:, :1] + jnp.log(l)

    kv = pl.BlockSpec((T, wp), lambda h, i, d: (jnp.clip(i + d - hb, 0, nq - 1), h))
    return _hosted_call(
        body, [q, k, v, bias], name=name, grid=(H // hp, nq, nd),
        in_specs=[pl.BlockSpec((T, wp), lambda h, i, d: (i, h)), kv, kv,
                  pl.BlockSpec((nd, T, T), lambda h, i, d: (0, 0, 0))],
        out_specs=[pl.BlockSpec((T, wp), lambda h, i, d: (i, h)),
                   pl.BlockSpec((hp, T, 1), lambda h, i, d: (h, i, 0))],
        out_shape=[jax.ShapeDtypeStruct((S, W), F32), jax.ShapeDtypeStruct((H, S, 1), F32)],
        scratch_shapes=[pltpu.VMEM((hp, T, LANES), F32), pltpu.VMEM((hp, T, LANES), F32),
                        pltpu.VMEM((T, wp), F32)],
        sem=("parallel", "parallel", "arbitrary"), comm=comm)


def _attn_bwd(q, k, v, do, lse, delta, bias, name, comm=None):
    S, W = q.shape
    H = W // HEAD_DIM
    nd, T, _ = bias.shape
    hb, nq = nd // 2, S // T
    scale = HEAD_DIM ** -0.5
    hp = min(H, HEADS_PER_STEP)
    rc = min(T, ATTN_ROW_CHUNK)
    wp = hp * HEAD_DIM

    def body(q_ref, do_ref, lse_ref, dl_ref, k_ref, v_ref, b_ref, dq_ref, dk_ref, dv_ref):
        j, d = pl.program_id(1), pl.program_id(2)
        i = j + d - hb

        @pl.when((j == 0) & (d == 0))
        def _():
            dq_ref[...] = jnp.zeros(dq_ref.shape, F32)

        @pl.when(d == 0)
        def _():
            dk_ref[...] = jnp.zeros(dk_ref.shape, F32)
            dv_ref[...] = jnp.zeros(dv_ref.shape, F32)

        @pl.when((i >= 0) & (i < nq))
        def _():
            items = [(hh, c) for hh in range(hp) for c in range(T // rc)]

            def slices(item):
                hh, c = item
                return slice(hh * HEAD_DIM, (hh + 1) * HEAD_DIM), slice(c * rc, (c + 1) * rc)

            def products(item):
                cols, rows = slices(item)
                s = (lax.dot_general(q_ref[rows, cols], k_ref[:, cols], NT, preferred_element_type=F32)
                     + b_ref[nd - 1 - d, rows, :])
                dp = lax.dot_general(do_ref[rows, cols], v_ref[:, cols], NT, preferred_element_type=F32)
                return s, dp

            def gradients(item, p, ds):
                cols, rows = slices(item)
                dv_ref[:, cols] += lax.dot_general(p, do_ref[rows, cols], TN, preferred_element_type=F32)
                dk_ref[:, cols] += lax.dot_general(ds, q_ref[rows, cols], TN, preferred_element_type=F32)
                q_rows = pl.ds(pl.multiple_of(i * T + item[1] * rc, rc), rc)
                dq_ref[q_rows, cols] += jnp.dot(ds, k_ref[:, cols], preferred_element_type=F32) * scale

            nxt, pending = products(items[0]), None
            for n, item in enumerate(items):
                s, dp = nxt
                if n + 1 < len(items):
                    nxt = products(items[n + 1])
                if pending is not None:
                    gradients(*pending)
                _, rows = slices(item)
                p = jnp.exp(s - lse_ref[item[0], rows, :])
                ds = p * (dp - dl_ref[item[0], rows, :])
                pending = (item, p.astype(BF16), ds.astype(BF16))
            gradients(*pending)

    qi = lambda h, j, d: (jnp.clip(j + d - hb, 0, nq - 1), h)
    qs = pl.BlockSpec((T, wp), qi)
    col = pl.BlockSpec((hp, T, 1), lambda h, j, d: (h, jnp.clip(j + d - hb, 0, nq - 1), 0))
    kv = pl.BlockSpec((T, wp), lambda h, j, d: (j, h))
    shp = jax.ShapeDtypeStruct((S, W), F32)
    return _hosted_call(
        body, [q, do, lse, delta, k, v, bias], name=name, grid=(H // hp, nq, nd),
        in_specs=[qs, qs, col, col, kv, kv, pl.BlockSpec((nd, T, T), lambda h, j, d: (0, 0, 0))],
        out_specs=[pl.BlockSpec((S, wp), lambda h, j, d: (0, h)), kv, kv],
        out_shape=[shp, shp, shp],
        sem=("parallel", "arbitrary", "arbitrary"), comm=comm)


def _halo_specs(S, tr, W, piece):
    per, last = tr // 8, S // 8 - 1
    prev = pl.BlockSpec((8, W), lambda i: (jnp.maximum(i * per - 1, 0), piece))
    nxt = pl.BlockSpec((8, W), lambda i: (jnp.minimum((i + 1) * per, last), piece))
    return prev, nxt


def _shifted(t, before, after, tr):
    rows = lax.broadcasted_iota(jnp.int32, (tr, 1), 0)
    prev = jnp.where(rows == 0, before, pltpu.roll(t, 1, 0))
    nxt = jnp.where(rows == tr - 1, after, pltpu.roll(t, tr - 1, 0))
    return prev, nxt


def _ab_mix(attn, proj, conv_w, W, name):
    S = attn.shape[0]
    tr = min(S, ROW_TILE)
    nsteps = S // tr

    def body(a_ref, za_ref, ub_ref, gb_ref, gc_ref, zb_ref, ubp, ubn, gcp, gcn, w_ref, y_ref):
        i = pl.program_id(0)
        t = gc_ref[...] * ub_ref[...]
        before = jnp.where(i == 0, 0.0, (gcp[...] * ubp[...])[7:8, :])
        after = jnp.where(i == nsteps - 1, 0.0, (gcn[...] * ubn[...])[0:1, :])
        t_prev, t_next = _shifted(t, before, after, tr)
        w = w_ref[...]
        cv = w[0:1, :] * t_prev + w[1:2, :] * t + w[2:3, :] * t_next
        silu_a, _ = _silu_and_grad(za_ref[...])
        silu_b, _ = _silu_and_grad(zb_ref[...])
        y_ref[:, :W] = (a_ref[...] * silu_a).astype(BF16)
        y_ref[:, W:] = (gb_ref[...] * cv * silu_b).astype(BF16)

    piece = lambda p: pl.BlockSpec((tr, W), lambda i: (i, p))
    ubp, ubn = _halo_specs(S, tr, W, 4)
    gcp, gcn = _halo_specs(S, tr, W, 6)
    return pl.pallas_call(
        body, name=name, grid=(nsteps,),
        in_specs=[pl.BlockSpec((tr, W), lambda i: (i, 0)), piece(3), piece(4), piece(5), piece(6), piece(7),
                  ubp, ubn, gcp, gcn, pl.BlockSpec((3, W), lambda i: (0, 0))],
        out_specs=pl.BlockSpec((tr, 2 * W), lambda i: (i, 0)),
        out_shape=jax.ShapeDtypeStruct((S, 2 * W), BF16),
        compiler_params=_cp("parallel"),
    )(attn, proj, proj, proj, proj, proj, proj, proj, proj, proj, conv_w)


def _dattn_prep(dy, proj, attn, W, name):
    S = attn.shape[0]
    H = W // HEAD_DIM
    tr = min(S, ROW_TILE)

    def body(dy_ref, za_ref, a_ref, do_ref, dl_ref):
        silu_a, _ = _silu_and_grad(za_ref[...])
        do = dy_ref[...] * silu_a
        do_ref[...] = do.astype(BF16)
        prod = do * a_ref[...]
        for hh in range(H):
            dl_ref[hh] = jnp.sum(prod[:, hh * HEAD_DIM:(hh + 1) * HEAD_DIM], axis=1, keepdims=True)

    row = pl.BlockSpec((tr, W), lambda i: (i, 0))
    return pl.pallas_call(
        body, name=name, grid=(S // tr,),
        in_specs=[row, pl.BlockSpec((tr, W), lambda i: (i, 3)), row],
        out_specs=[row, pl.BlockSpec((H, tr, 1), lambda i: (0, i, 0))],
        out_shape=[jax.ShapeDtypeStruct((S, W), BF16), jax.ShapeDtypeStruct((H, S, 1), F32)],
        compiler_params=_cp("parallel"),
    )(dy, proj, attn)


def _ab_bwd(dy, attn, proj, dqr, dkr, dv, cosf, sinf, conv_w, W, name):
    S = attn.shape[0]
    tr = min(S, ROW_TILE // 2)
    nsteps = S // tr
    heads = W // HEAD_DIM

    def body(dya_ref, dyb_ref, a_ref, za_ref, ub_ref, gb_ref, gc_ref, zb_ref, dq_ref, dk_ref, dv_ref,
             c_ref, s_ref, w_ref, dybp, dybn, gbp, gbn, zbp, zbn, ubp, ubn, gcp, gcn,
             dp_ref, dw_ref, acc):
        i = pl.program_id(0)
        first, last = i == 0, i == nsteps - 1
        w = w_ref[...]
        w0, w1, w2 = w[0:1, :], w[1:2, :], w[2:3, :]
        ub, gb, gc, zb = ub_ref[...], gb_ref[...], gc_ref[...], zb_ref[...]
        dyb = dyb_ref[...]
        silu_a, dsilu_a = _silu_and_grad(za_ref[...])
        silu_b, dsilu_b = _silu_and_grad(zb)
        t = gc * ub
        t_prev, t_next = _shifted(t, jnp.where(first, 0.0, (gcp[...] * ubp[...])[7:8, :]),
                                  jnp.where(last, 0.0, (gcn[...] * ubn[...])[0:1, :]), tr)
        cv = w0 * t_prev + w1 * t + w2 * t_next
        dcv = dyb * gb * silu_b
        halo_p = dybp[...] * gbp[...] * _silu_and_grad(zbp[...])[0]
        halo_n = dybn[...] * gbn[...] * _silu_and_grad(zbn[...])[0]
        dcv_prev, dcv_next = _shifted(dcv, jnp.where(first, 0.0, halo_p[7:8, :]),
                                      jnp.where(last, 0.0, halo_n[0:1, :]), tr)
        dt = w0 * dcv_next + w1 * dcv + w2 * dcv_prev
        cosf_v, sinf_v = c_ref[...], s_ref[...]
        for src, base in ((dq_ref, 0), (dk_ref, W)):
            for hh, val in enumerate(_rope_apply(src[...], cosf_v, sinf_v, heads, -1.0)):
                dp_ref[:, base + hh * HEAD_DIM:base + (hh + 1) * HEAD_DIM] = val.astype(BF16)
        dp_ref[:, 2 * W:3 * W] = dv_ref[...].astype(BF16)
        dp_ref[:, 3 * W:4 * W] = (dya_ref[...] * a_ref[...] * dsilu_a).astype(BF16)
        dp_ref[:, 4 * W:5 * W] = (dt * gc).astype(BF16)
        dp_ref[:, 5 * W:6 * W] = (dyb * cv * silu_b).astype(BF16)
        dp_ref[:, 6 * W:7 * W] = (dt * ub).astype(BF16)
        dp_ref[:, 7 * W:8 * W] = (dyb * gb * cv * dsilu_b).astype(BF16)
        tap = lax.broadcasted_iota(jnp.int32, (8, 1), 0)
        part = (jnp.where(tap == 0, jnp.sum(dcv * t_prev, axis=0, keepdims=True), 0.0)
                + jnp.where(tap == 1, jnp.sum(dcv * t, axis=0, keepdims=True), 0.0)
                + jnp.where(tap == 2, jnp.sum(dcv * t_next, axis=0, keepdims=True), 0.0))

        @pl.when(first)
        def _():
            acc[...] = part

        @pl.when(i > 0)
        def _():
            acc[...] += part

        @pl.when(last)
        def _():
            dw_ref[...] = acc[...]

    row = pl.BlockSpec((tr, W), lambda i: (i, 0))
    piece = lambda p: pl.BlockSpec((tr, W), lambda i: (i, p))
    tab = pl.BlockSpec((tr, HEAD_DIM), lambda i: (i, 0))
    dybp, dybn = _halo_specs(S, tr, W, 1)
    gbp, gbn = _halo_specs(S, tr, W, 5)
    zbp, zbn = _halo_specs(S, tr, W, 7)
    ubp, ubn = _halo_specs(S, tr, W, 4)
    gcp, gcn = _halo_specs(S, tr, W, 6)
    return pl.pallas_call(
        body, name=name, grid=(nsteps,),
        in_specs=[piece(0), piece(1), row, piece(3), piece(4), piece(5), piece(6), piece(7), row, row, row,
                  tab, tab, pl.BlockSpec((3, W), lambda i: (0, 0)),
                  dybp, dybn, gbp, gbn, zbp, zbn, ubp, ubn, gcp, gcn],
        out_specs=[pl.BlockSpec((tr, 8 * W), lambda i: (i, 0)), pl.BlockSpec((8, W), lambda i: (0, 0))],
        out_shape=[jax.ShapeDtypeStruct((S, 8 * W), BF16), jax.ShapeDtypeStruct((8, W), F32)],
        scratch_shapes=[pltpu.VMEM((8, W), F32)],
        compiler_params=_cp("arbitrary"),
    )(dy, dy, attn, proj, proj, proj, proj, proj, dqr, dkr, dv, cosf, sinf, conv_w,
      dy, dy, proj, proj, proj, proj, proj, proj, proj, proj)


def _sgu_centre(p_ref, vc_s, dvg_s, Dc):
    gw = Dc // C_GROUPS
    total = None
    for g in range(C_GROUPS):
        cs = slice(g * gw, (g + 1) * gw)
        vg, dvg = _gelu_and_grad(p_ref[:, Dc + g * gw:Dc + (g + 1) * gw])
        vc_s[:, cs] = vg
        if dvg_s is not None:
            dvg_s[:, cs] = dvg
        total = vg if total is None else total + vg
    mu = jnp.sum(total, axis=1, keepdims=True) * (1.0 / Dc)
    total = None
    for g in range(C_GROUPS):
        cs = slice(g * gw, (g + 1) * gw)
        vc = vc_s[:, cs] - mu
        vc_s[:, cs] = vc
        total = vc * vc if total is None else total + vc * vc
    return lax.rsqrt(jnp.sum(total, axis=1, keepdims=True) * (1.0 / Dc) + EPS)


def _sgu_fwd(proj, ln_g, ln_b, w_s, b_st, name, comm=None):
    S, Dc3 = proj.shape
    Dc = Dc3 // 3
    gw = Dc // C_GROUPS
    vec = pl.BlockSpec((1, Dc), lambda i: (0, 0))

    def body(p_ref, lng_ref, lnb_ref, ws_ref, bst_ref, y_ref, vc_s):
        rstd = _sgu_centre(p_ref, vc_s, None, Dc)
        bst = bst_ref[...]
        for g in range(C_GROUPS):
            cs = slice(g * gw, (g + 1) * gw)
            vn = (vc_s[:, cs] * rstd * lng_ref[:, cs] + lnb_ref[:, cs]).astype(BF16)
            mixed = jnp.dot(ws_ref[g].astype(BF16), vn, preferred_element_type=F32) + bst[:, g:g + 1]
            u, _ = _gelu_and_grad(p_ref[:, cs])
            sz, _ = _silu_and_grad(p_ref[:, 2 * Dc + g * gw:2 * Dc + (g + 1) * gw])
            y_ref[:, cs] = (u * mixed * sz).astype(BF16)

    return _hosted_call(
        body, [proj, ln_g, ln_b, w_s, b_st], name=name, grid=(S // C_CHUNK,),
        in_specs=[pl.BlockSpec((C_CHUNK, Dc3), lambda i: (i, 0)), vec, vec,
                  pl.BlockSpec((C_GROUPS, C_CHUNK, C_CHUNK), lambda i: (0, 0, 0)),
                  pl.BlockSpec((C_CHUNK, C_GROUPS), lambda i: (0, 0))],
        out_specs=pl.BlockSpec((C_CHUNK, Dc), lambda i: (i, 0)),
        out_shape=jax.ShapeDtypeStruct((S, Dc), BF16),
        scratch_shapes=[pltpu.VMEM((C_CHUNK, Dc), F32)],
        sem=("parallel",), comm=comm)


def _sgu_bwd(proj, dy, ln_g, ln_b, w_s, w_st, b_st, name):
    S, Dc3 = proj.shape
    Dc = Dc3 // 3
    gw = Dc // C_GROUPS
    nsteps = S // C_CHUNK
    vec = pl.BlockSpec((1, Dc), lambda i: (0, 0))
    wspec = pl.BlockSpec((C_GROUPS, C_CHUNK, C_CHUNK), lambda i: (0, 0, 0))

    def body(p_ref, dy_ref, lng_ref, lnb_ref, ws_ref, wst_ref, bst_ref,
             dp_ref, dws_ref, dbs_ref, dlg_ref, dlb_ref, acc_w, acc_b, acc_g, acc_lb, vc_s, dvg_s, dvh_s):
        i = pl.program_id(0)

        @pl.when(i == 0)
        def _():
            acc_w[...] = jnp.zeros(acc_w.shape, F32)
            acc_b[...] = jnp.zeros(acc_b.shape, F32)
            acc_g[...] = jnp.zeros(acc_g.shape, F32)
            acc_lb[...] = jnp.zeros(acc_lb.shape, F32)

        rstd = _sgu_centre(p_ref, vc_s, dvg_s, Dc)
        bst = bst_ref[...]
        octets = lambda t: jnp.sum(t.reshape(C_CHUNK // 8, 8, gw), axis=0)
        t1, t2 = None, None
        for g in range(C_GROUPS):
            cs = slice(g * gw, (g + 1) * gw)
            zs = slice(2 * Dc + g * gw, 2 * Dc + (g + 1) * gw)
            vhat = vc_s[:, cs] * rstd
            vn = (vhat * lng_ref[:, cs] + lnb_ref[:, cs]).astype(BF16)
            mixed = jnp.dot(ws_ref[g].astype(BF16), vn, preferred_element_type=F32) + bst[:, g:g + 1]
            u, du = _gelu_and_grad(p_ref[:, cs])
            sz, dsz = _silu_and_grad(p_ref[:, zs])
            dy = dy_ref[:, cs]
            dmixed = dy * u * sz
            dmb = dmixed.astype(BF16)
            acc_w[g] += lax.dot_general(dmb, vn, NT, preferred_element_type=F32)
            acc_b[g] += dmixed
            dvn = jnp.dot(wst_ref[g].astype(BF16), dmb, preferred_element_type=F32)
            acc_g[:, cs] += octets(dvn * vhat)
            acc_lb[:, cs] += octets(dvn)
            dvh = dvn * lng_ref[:, cs]
            dvh_s[:, cs] = dvh
            t1 = dvh if t1 is None else t1 + dvh
            t2 = dvh * vhat if t2 is None else t2 + dvh * vhat
            dp_ref[:, cs] = (dy * mixed * sz * du).astype(BF16)
            dp_ref[:, zs] = (dy * u * mixed * dsz).astype(BF16)
        m1 = jnp.sum(t1, axis=1, keepdims=True) * (1.0 / Dc)
        m2 = jnp.sum(t2, axis=1, keepdims=True) * (1.0 / Dc)
        for g in range(C_GROUPS):
            cs = slice(g * gw, (g + 1) * gw)
            dvgelu = rstd * (dvh_s[:, cs] - m1 - (vc_s[:, cs] * rstd) * m2)
            dp_ref[:, Dc + g * gw:Dc + (g + 1) * gw] = (dvgelu * dvg_s[:, cs]).astype(BF16)

        @pl.when(i == nsteps - 1)
        def _():
            dws_ref[...] = acc_w[...]
            for g in range(C_GROUPS):
                dbs_ref[g] = jnp.sum(acc_b[g], axis=1, keepdims=True)
            dlg_ref[...] = jnp.sum(acc_g[...], axis=0, keepdims=True)
            dlb_ref[...] = jnp.sum(acc_lb[...], axis=0, keepdims=True)

    v = jax.ShapeDtypeStruct((1, Dc), F32)
    return pl.pallas_call(
        body, name=name, grid=(nsteps,),
        in_specs=[pl.BlockSpec((C_CHUNK, Dc3), lambda i: (i, 0)), pl.BlockSpec((C_CHUNK, Dc), lambda i: (i, 0)),
                  vec, vec, wspec, wspec, pl.BlockSpec((C_CHUNK, C_GROUPS), lambda i: (0, 0))],
        out_specs=[pl.BlockSpec((C_CHUNK, Dc3), lambda i: (i, 0)), wspec,
                   pl.BlockSpec((C_GROUPS, C_CHUNK, 1), lambda i: (0, 0, 0)), vec, vec],
        out_shape=[jax.ShapeDtypeStruct((S, Dc3), BF16),
                   jax.ShapeDtypeStruct((C_GROUPS, C_CHUNK, C_CHUNK), F32),
                   jax.ShapeDtypeStruct((C_GROUPS, C_CHUNK, 1), F32), v, v],
        scratch_shapes=[pltpu.VMEM((C_GROUPS, C_CHUNK, C_CHUNK), F32), pltpu.VMEM((C_GROUPS, C_CHUNK, gw), F32),
                        pltpu.VMEM((8, Dc), F32), pltpu.VMEM((8, Dc), F32)]
                       + [pltpu.VMEM((C_CHUNK, Dc), F32)] * 3,
        compiler_params=_cp("arbitrary"),
    )(proj, dy, ln_g, ln_b, w_s, w_st, b_st)


PACK_COLS = 1024
PACK_ROWS = 64


def _pack(vectors, rows=PACK_ROWS):
    flat = jnp.concatenate([v.reshape(-1) for v in vectors])
    pad = (-flat.shape[0]) % (PACK_COLS * rows)
    return jnp.pad(flat, (0, pad)).reshape(-1, PACK_COLS)


def _unshard(g, off, shape):
    L, rest = shape[0], shape[1:]
    size = int(np.prod(shape))
    piece = g[:, off:off + size].reshape((NDEV,) + tuple(shape))
    nd = piece.ndim
    perm = tuple(range(1, nd - 1)) + (0, nd - 1)
    full = jnp.transpose(piece, perm)
    return full.reshape(tuple(shape[:-1]) + (NDEV * shape[-1],)), off + size


def kernel(x, c, ab_norm_g, ab_w_mod, ab_b_mod, ab_w_in, ab_conv_w, ab_w_out, sg_norm_g, sg_w_mod, sg_b_mod, sg_w_in, sg_ln_g, sg_ln_b, sg_w_s, sg_b_s, sg_w_out, final_norm_g, loss_target, m_ab_norm_g, m_ab_w_mod, m_ab_b_mod, m_ab_w_in, m_ab_conv_w, m_ab_w_out, m_sg_norm_g, m_sg_w_mod, m_sg_b_mod, m_sg_w_in, m_sg_ln_g, m_sg_ln_b, m_sg_w_s, m_sg_b_s, m_sg_w_out, m_final_norm_g, v_ab_norm_g, v_ab_w_mod, v_ab_b_mod, v_ab_w_in, v_ab_conv_w, v_ab_w_out, v_sg_norm_g, v_sg_w_mod, v_sg_b_mod, v_sg_w_in, v_sg_ln_g, v_sg_ln_b, v_sg_w_s, v_sg_b_s, v_sg_w_out, v_final_norm_g):
    _, S, D = x.shape
    L = ab_norm_g.shape[0]
    W = ab_conv_w.shape[2] * NDEV
    n_ab, n_sg = ab_w_in.shape[2], sg_w_in.shape[2]
    n_mod = ab_w_mod.shape[2]
    kb = ab_w_out.shape[1]
    xi, yi, ci = _position()
    dev = 4 * xi + 2 * yi + ci
    x2, tgt = x.reshape(S, D), loss_target.reshape(S, D)

    small = [c, ab_conv_w, sg_norm_g, sg_ln_g, sg_ln_b]
    (g1,) = _comm_only(_GatherDirect([_pack(small, 8)]), "ag_small")
    g1 = g1.reshape(NDEV, -1)
    c_all = g1[:, :D]
    off = D
    conv_full, off = _unshard(g1, off, ab_conv_w.shape)
    sg_norm_full, off = _unshard(g1, off, sg_norm_g.shape)
    ln_g_full, off = _unshard(g1, off, sg_ln_g.shape)
    ln_b_full, off = _unshard(g1, off, sg_ln_b.shape)

    ab_b_cols = lax.dynamic_slice_in_dim(ab_b_mod, dev * n_mod, n_mod, axis=1)
    m_ab = _mod_fwd(c_all, ab_w_mod, ab_b_cols.reshape(L, 1, n_mod), "mod_fwd_ab")
    m_sg = _mod_fwd(c_all, sg_w_mod, sg_b_mod.reshape(L, 1, n_mod), "mod_fwd_sg")
    m_part = jnp.stack([m_ab, m_sg]).transpose(2, 0, 1, 3).reshape(NDEV, 2 * L * n_mod)
    (g2,) = _comm_only(_GatherDirect([m_part]), "ag_mod")
    mine = lax.dynamic_index_in_dim(g2, dev, axis=1, keepdims=False)
    mods = mine.reshape(NDEV, 2, L, n_mod).transpose(1, 2, 0, 3).reshape(2, L, 3 * D)

    def mod_of(kind, i):
        m = mods[kind, i]
        return m[:D].reshape(1, D), m[D:2 * D].reshape(1, D), m[2 * D:].reshape(1, D)

    big_w = [[(ab_w_in, m_ab_w_in, v_ab_w_in), (ab_w_out, m_ab_w_out, v_ab_w_out)],
             [(sg_w_in, m_sg_w_in, v_sg_w_in), (sg_w_out, m_sg_w_out, v_sg_w_out)]]
    big_names = [["ab_w_in", "ab_w_out"], ["sg_w_in", "sg_w_out"]]
    n_layers = 2 * L
    shards = [[big_w[layer % 2][k][0][layer // 2].astype(BF16) for k in range(2)] for layer in range(n_layers)]
    gathered = {}

    def gather_of(keys):
        keys = [key for key in keys if key[0] < n_layers and key not in gathered]
        return keys, (_Gather([shards[layer][k] for layer, k in keys]) if keys else None)

    def keep_gathered(keys, res):
        for (layer, k), g in zip(keys, res):
            gathered[(layer, k)] = g.reshape((NDEV, 1, D, g.shape[-1]) if k == 0 else (NDEV, 1, kb, D))

    keys, comm = gather_of([(0, 0)])
    keep_gathered(keys, _comm_only(comm, "ag_w_in_layer0"))

    cosf, sinf = _rope_tables(S)
    T = min(S, ATTN_TILE)
    bias = _attn_bias(T)
    norm_g = [ab_norm_g, sg_norm_full]
    w_s_t = jnp.swapaxes(sg_w_s, -1, -2)
    b_s_t = jnp.swapaxes(sg_b_s, -1, -2)

    saved = []
    x_cur, res, gate_prev = x2, None, None
    for layer in range(2 * L):
        kind, i = layer % 2, layer // 2
        tag = f"{'ab' if kind == 0 else 'sg'}{i}"
        shift, scale, gate = mod_of(kind, i)
        g = norm_g[kind][i].reshape(1, D)
        xl, h = _pre(x_cur, res, gate_prev, g, scale, shift, f"pre_{tag}")
        keys, comm = gather_of([(layer + 1, 0)])
        if comm is None:
            proj = _mm_nn_in(h, gathered[(layer, 0)], 0, f"proj_{tag}")
        else:
            proj, got = _mm_nn_in(h, gathered[(layer, 0)], 0, f"proj_{tag}", comm)
            keep_gathered(keys, got)
        rec = dict(xl=xl, h=h, proj=proj, g=g, scale=scale, gate=gate)
        keys, comm = gather_of([(layer, 1), (layer + 1, 1), (layer + 2, 1)] if kind == 0 else [])
        if kind == 0:
            qr, kr, vb = _rope_qkv(proj, cosf, sinf, W, f"rope_{tag}")
            (attn, lse), got = _attn_fwd(qr, kr, vb, bias, f"attn_{tag}", comm)
            y = _ab_mix(attn, proj, conv_full[i], W, f"mix_{tag}")
            rec.update(qr=qr, kr=kr, vb=vb, attn=attn, lse=lse)
        else:
            y, got = _sgu_fwd(proj, ln_g_full[i].reshape(1, D), ln_b_full[i].reshape(1, D), sg_w_s[i], b_s_t[i],
                              f"sgu_{tag}", comm)
        keep_gathered(keys, got)
        out = _mm_nn_out(y, gathered[(layer, 1)], 0, f"out_{tag}")
        rec.update(y=y, out=out)
        saved.append(rec)
        x_cur, res, gate_prev = xl, out, gate

    dx, loss_part, d_final_g, *from_head = _loss_head(x_cur, res, gate_prev, final_norm_g.reshape(1, D), tgt,
                                                     "loss_head")
    loss = lax.psum(loss_part[0, 0], ("x", "y", "c"))

    c_idx = ci.reshape(1).astype(jnp.int32)
    big_res = {}
    pending = None
    carried = from_head

    def sum_and_update(done, k, from_chips, part=(0, 1)):
        nm = big_names[done % 2][k]
        w, m, v = big_w[done % 2][k]
        flat = lambda a: a.reshape(L * a.shape[1], a.shape[2])
        row0 = (done // 2) * w.shape[1] + part[0] * (w.shape[1] // part[1])
        big_res[nm] = _sum_adam(from_chips, flat(w), flat(m), flat(v), row0, big_res.get(nm),
                                f"adam_{nm}{done // 2}_{part[0]}")

    def finish_layer(done, from_chips):
        for k in range(2):
            sum_and_update(done, k, from_chips[k])

    dm = [[None] * L, [None] * L]
    d_norm = [[None] * L, [None] * L]
    d_conv, d_lng, d_lnb, d_ws, d_bs = [None] * L, [None] * L, [None] * L, [None] * L, [None] * L
    for layer in reversed(range(2 * L)):
        kind, i = layer % 2, layer // 2
        tag = f"{'ab' if kind == 0 else 'sg'}{i}"
        rec = saved[layer]
        w_in_l, w_out_l = gathered[(layer, 0)], gathered[(layer, 1)]
        if carried is None:
            dout, dgate = _post_bwd(dx, rec["out"], rec["gate"], f"post_bwd_{tag}")
        else:
            dout, dgate = carried
        dy = _mm_nt_out(dout, w_out_l, 0, f"dy_{tag}")
        dwo = _mm_tn_out(rec["y"], dout, 0, 1, None, f"dwout_{tag}")
        if kind == 0:
            do, delta = _dattn_prep(dy, rec["proj"], rec["attn"], W, f"dattn_{tag}")
            comm = _ToChips(pending[1]) if pending else None
            (dqr, dkr, dvv), got = _attn_bwd(rec["qr"], rec["kr"], rec["vb"], do, rec["lse"], delta, bias,
                                             f"attn_bwd_{tag}", comm)
            if pending:
                finish_layer(pending[0], got)
            dproj, dcw = _ab_bwd(dy, rec["attn"], rec["proj"], dqr, dkr, dvv, cosf, sinf, conv_full[i], W,
                                 f"mix_bwd_{tag}")
            d_conv[i] = dcw[:3]
            earlier = None
        else:
            dproj, d_ws[i], dbs, d_lng[i], d_lnb[i] = _sgu_bwd(
                rec["proj"], dy, ln_g_full[i].reshape(1, D), ln_b_full[i].reshape(1, D),
                sg_w_s[i], w_s_t[i], b_s_t[i], f"sgu_bwd_{tag}")
            d_bs[i] = dbs.reshape(C_GROUPS, C_CHUNK)
            earlier = pending
        if layer == 0:
            h_l = rec["h"]
            dwi_a = _mm_tn_in(h_l, dproj, 0, 1, None, f"dwin_a_{tag}", part=(0, 2))
            g_a = [dwi_a.reshape(NDEV, D // 2, -1), dwo.reshape(NDEV, kb, D)]
            dwi_b, sib_a = _mm_tn_in(h_l, dproj, 0, 1, None, f"dwin_b_{tag}", _ToSibling(g_a), part=(1, 2))
            p_a = [_add_sibling(g, r, c_idx, f"rs_add_a{k}_{tag}") for k, (g, r) in enumerate(zip(g_a, sib_a))]
            g_b = [dwi_b.reshape(NDEV, D // 2, -1)]
            both = _Both(_ToSibling(g_b), _ToChips(p_a))
            dh, got = _mm_nt_in(dproj, w_in_l, 0, f"dh_{tag}", both)
            sib_b, chips_a = both.split_results(got)
            p_b = [_add_sibling(g_b[0], sib_b[0], c_idx, f"rs_add_b_{tag}")]
            (dx, dshift, dscale, d_norm[kind][i]), chips_b = _pre_bwd(
                rec["xl"], dh, dx, rec["g"], rec["scale"], f"pre_bwd_{tag}", _ToChips(p_b))
            sum_and_update(0, 0, chips_a[0], (0, 2))
            sum_and_update(0, 1, chips_a[1])
            sum_and_update(0, 0, chips_b[0], (1, 2))
            dm[kind][i] = jnp.concatenate([dshift, dscale, dgate], axis=1).reshape(3 * D)
            continue
        if earlier:
            dwi, got_in = _mm_tn_in(rec["h"], dproj, 0, 1, None, f"dwin_{tag}", _ToChips(earlier[1][:1]))
        else:
            dwi = _mm_tn_in(rec["h"], dproj, 0, 1, None, f"dwin_{tag}")
        grads = [dwi.reshape(NDEV, D, -1), dwo.reshape(NDEV, kb, D)]
        if earlier:
            both = _Both(_ToSibling(grads), _ToChips(earlier[1][1:]))
            dh, got = _mm_nt_in(dproj, w_in_l, 0, f"dh_{tag}", both)
            from_sibling, got_rest = both.split_results(got)
            sum_and_update(earlier[0], 0, got_in[0], (0, 2))
            sum_and_update(earlier[0], 0, got_rest[0], (1, 2))
            sum_and_update(earlier[0], 1, got_rest[1])
        else:
            dh, from_sibling = _mm_nt_in(dproj, w_in_l, 0, f"dh_{tag}", _ToSibling(grads))
        name_in, name_out = (f"rs_add_{nm}{i}" for nm in big_names[kind])
        if kind == 0:
            parts = [_add_sibling(grads[0], from_sibling[0], c_idx, f"{name_in}_{p}", (p, 2)) for p in range(2)]
        else:
            parts = [_add_sibling(grads[0], from_sibling[0], c_idx, name_in)]
        pending = (layer, parts + [_add_sibling(grads[1], from_sibling[1], c_idx, name_out)])
        under = saved[layer - 1]
        (dx, dshift, dscale, d_norm[kind][i], *carried), _ = _pre_bwd(
            rec["xl"], dh, dx, rec["g"], rec["scale"], f"pre_bwd_{tag}", below=(under["out"], under["gate"]))
        dm[kind][i] = jnp.concatenate([dshift, dscale, dgate], axis=1).reshape(3 * D)
    grad_x = dx.reshape(1, S, D)
    for kind in range(2):
        for k in range(2):
            nm = big_names[kind][k]
            big_res[nm] = [o.reshape(big_w[kind][k][0].shape) for o in big_res[nm]]

    stack = lambda xs: jnp.stack(xs)
    pack_items = [stack(dm[0]), stack(dm[1]), stack(d_norm[0]).reshape(L, D), stack(d_conv),
                  stack(d_norm[1]).reshape(L, D), stack(d_lng).reshape(L, D), stack(d_lnb).reshape(L, D),
                  stack(d_ws), stack(d_bs), d_final_g]
    (g3,) = _comm_only(_Gather([_pack(pack_items)]), "ag_grads")
    P = g3.shape[1] * g3.shape[2]
    tot = _sum_rows(g3, "sum_small").reshape(P)
    g3 = g3.reshape(NDEV, P)
    sizes = [int(np.prod(p.shape)) for p in pack_items]
    offs = np.concatenate([[0], np.cumsum(sizes)]).tolist()
    seg = lambda k, shape: tot[offs[k]:offs[k + 1]].reshape(shape)

    def shard(full, n):
        return lax.dynamic_slice_in_dim(full, dev * n, n, axis=full.ndim - 1)

    g_ab_b_mod = seg(0, (L, 3 * D))
    g_sg_b_mod = shard(seg(1, (L, 3 * D)), n_mod)
    g_ab_norm = seg(2, (L, D))
    g_conv = shard(seg(3, (L, 3, W)), W // NDEV)
    g_sg_norm = shard(seg(4, (L, D)), kb)
    g_ln_g = shard(seg(5, (L, D)), kb)
    g_ln_b = shard(seg(6, (L, D)), kb)
    g_w_s = seg(7, sg_w_s.shape)
    g_b_s = seg(8, sg_b_s.shape)
    g_final = seg(9, (D,))

    small_w = [("ab_norm_g", g_ab_norm, ab_norm_g, m_ab_norm_g, v_ab_norm_g),
               ("ab_b_mod", g_ab_b_mod, ab_b_mod, m_ab_b_mod, v_ab_b_mod),
               ("ab_conv_w", g_conv, ab_conv_w, m_ab_conv_w, v_ab_conv_w),
               ("sg_norm_g", g_sg_norm, sg_norm_g, m_sg_norm_g, v_sg_norm_g),
               ("sg_b_mod", g_sg_b_mod, sg_b_mod, m_sg_b_mod, v_sg_b_mod),
               ("sg_ln_g", g_ln_g, sg_ln_g, m_sg_ln_g, v_sg_ln_g),
               ("sg_ln_b", g_ln_b, sg_ln_b, m_sg_ln_b, v_sg_ln_b),
               ("sg_w_s", g_w_s, sg_w_s, m_sg_w_s, v_sg_w_s),
               ("sg_b_s", g_b_s, sg_b_s, m_sg_b_s, v_sg_b_s),
               ("final_norm_g", g_final, final_norm_g, m_final_norm_g, v_final_norm_g)]
    packed = [_pack([t[k] for t in small_w]) for k in (1, 2, 3, 4)]
    upd = _adam_only(*packed, "adam_small")
    small_res = {}
    o = 0
    for nm, g, w, _, _ in small_w:
        size = int(np.prod(w.shape))
        small_res[nm] = [g] + [u.reshape(-1)[o:o + size].reshape(w.shape) for u in upd]
        o += size

    KP = 128
    sc_t = jnp.pad((c_all * jax.nn.sigmoid(c_all)).T, ((0, 0), (0, KP - NDEV)))
    mod_res = {}
    for kind, nm, (w, m, v) in ((0, "ab_w_mod", (ab_w_mod, m_ab_w_mod, v_ab_w_mod)),
                                (1, "sg_w_mod", (sg_w_mod, m_sg_w_mod, v_sg_w_mod))):
        dm_all = g3[:, offs[kind]:offs[kind + 1]].reshape(NDEV, L, 3 * D)
        cols = jnp.pad(shard(dm_all, n_mod).transpose(1, 0, 2), ((0, 0), (0, KP - NDEV), (0, 0)))
        mod_res[nm] = _wmod_grad_adam(sc_t, cols, w, m, v, f"adam_{nm}")

    order = ["ab_norm_g", "ab_w_mod", "ab_b_mod", "ab_w_in", "ab_conv_w", "ab_w_out", "sg_norm_g", "sg_w_mod",
             "sg_b_mod", "sg_w_in", "sg_ln_g", "sg_ln_b", "sg_w_s", "sg_b_s", "sg_w_out", "final_norm_g"]
    res = {**big_res, **small_res, **mod_res}
    outs = [loss, grad_x]
    for k in range(4):
        outs += [res[nm][k] for nm in order]
    return tuple(outs)
```
